```python
import math
import jax
import jax.numpy as jnp
from jax import lax
import numpy as np

D_MODEL = 1024
BATCH = 8
SEQ = 2048
DEPTH = 4
DEC_BATCH = 128
DEC_SEQ = 8
PAST_LEN = 16384
PAGE_SIZE = 128

F32 = jnp.float32
EPS = 1e-6
N_EVEN = (DEPTH + 1) // 2
N_ODD = DEPTH // 2
D_FF = 4 * D_MODEL
PLE_DIM = 256

A_HEADS = 4
A_DK = D_MODEL // 8
A_DV = D_MODEL // 8
A_CHUNK = 64
A_QK = A_HEADS * A_DK
A_V = A_HEADS * A_DV
A_COLS = 2 * A_QK + 2 * A_V + 2 * A_HEADS

B_HEADS = 8
B_DH = 64
B_W = B_HEADS * B_DH
B_W_RANK = 64
B_A_RANK = 64
B_G_RANK = 128
B_COLS = 3 * B_W + B_W_RANK + B_A_RANK + B_G_RANK
B_DECAY_OFFSET = 0.5
B_GN_EPS = 64e-5

EVEN_COLS = A_COLS + B_COLS
EVEN_OUT = A_V + B_W

C_INNER = 2 * D_MODEL
C_HEADDIM = 64
C_HEADS = C_INNER // C_HEADDIM
C_GROUPS = 4
C_HPG = C_HEADS // C_GROUPS
C_STATE = 128
C_CONV = 4
C_CHUNK = 128
C_CONV_DIM = C_INNER + 2 * C_GROUPS * C_STATE
ODD_COLS = C_INNER + C_CONV_DIM + C_HEADS

kernel_name = "xlstm_rwkv7_mamba2_hybrid_step"


def rmsnorm(x, g):
    xf = x.astype(F32)
    y = xf * lax.rsqrt(jnp.mean(xf * xf, axis=-1, keepdims=True) + EPS)
    return (y * g.astype(F32)).astype(x.dtype)


def to_chunks(a, T):
    b, L = a.shape[:2]
    return jnp.moveaxis(a.reshape((b, L // T, T) + a.shape[2:]), 1, 0)


def from_chunks(a):
    nc, b, T = a.shape[:3]
    return jnp.moveaxis(a, 0, 1).reshape((b, nc * T) + a.shape[3:])


def mlstm_chunked(q, k, v, logi, logf, C0, n0, m0):
    L = q.shape[1]
    T = math.gcd(L, A_CHUNK)
    causal = jnp.tril(jnp.ones((T, T), bool))[None, :, :, None]

    def step(carry, inp):
        C, n, m = carry
        qc, kc, vc, li, lf = inp
        bc = jnp.cumsum(lf, axis=1)
        dmat = jnp.where(causal, bc[:, :, None] - bc[:, None, :] + li[:, None, :], -jnp.inf)
        inter = bc + m[:, None]
        mt = jnp.maximum(inter, jnp.max(dmat, axis=2))
        wq = jnp.einsum('bthd,bshd->btsh', qc, kc) * jnp.exp(dmat - mt[:, :, None])
        wi = jnp.exp(inter - mt)
        num = jnp.einsum('btsh,bshe->bthe', wq, vc) + wi[..., None] * jnp.einsum('bthd,bhde->bthe', qc, C)
        den = jnp.sum(wq, axis=2) + wi * jnp.einsum('bthd,bhd->bth', qc, n)
        h = num / jnp.maximum(jnp.abs(den), jnp.exp(-mt))[..., None]
        bT = bc[:, -1]
        gs = bT[:, None] - bc + li
        m_new = jnp.maximum(bT + m, jnp.max(gs, axis=1))
        decay = jnp.exp(bT + m - m_new)
        ws = jnp.exp(gs - m_new[:, None])
        C_new = decay[..., None, None] * C + jnp.einsum('bsh,bshd,bshe->bhde', ws, kc, vc)
        n_new = decay[..., None] * n + jnp.einsum('bsh,bshd->bhd', ws, kc)
        return (C_new, n_new, m_new), h

    init = (C0.astype(F32), n0.astype(F32), m0.astype(F32))
    (C, n, m), h = lax.scan(step, init, [to_chunks(a, T) for a in (q, k, v, logi, logf)])
    return from_chunks(h), C, n, m


def rwkv7_mixer(z, shift0, S0, mu, w0, w2, a0, a2, g2, k_k, k_a, r_k, ln_w, ln_b):
    b, L, _ = z.shape
    zf = z.astype(F32)
    zprev = jnp.concatenate([shift0[:, None].astype(F32), zf[:, :-1]], axis=1)
    zs = zf + (zprev - zf) * mu
    o1, o2, o3 = B_W, 2 * B_W, 3 * B_W
    o4, o5 = o3 + B_W_RANK, o3 + B_W_RANK + B_A_RANK
    r, k, v = zs[..., :o1], zs[..., o1:o2], zs[..., o2:o3]
    zw, za, zg = zs[..., o3:o4], zs[..., o4:o5], zs[..., o5:]
    w_log = -jax.nn.softplus(-(w0 + jnp.tanh(zw) @ w2)) - B_DECAY_OFFSET
    decay = jnp.exp(-jnp.exp(w_log))
    a = jax.nn.sigmoid(a0 + za @ a2)
    g = jax.nn.sigmoid(zg) @ g2
    heads = lambda t: t.reshape(b, L, B_HEADS, B_DH)
    kk = heads(k * k_k)
    kk = kk * lax.rsqrt(jnp.maximum(jnp.sum(kk * kk, axis=-1, keepdims=True), 1e-24))
    k = k * (1.0 + (a - 1.0) * k_a)
    r_h, w_h, k_h, v_h, a_h = heads(r), heads(decay), heads(k), heads(v), heads(a)
    a_vec = -kk
    b_vec = kk * a_h

    def step(S, inp):
        rt, wt, kt, vt, at, bt = inp
        sa = jnp.einsum('bhij,bhj->bhi', S, at)
        S = S * wt[:, :, None, :] + sa[..., None] * bt[:, :, None, :] + vt[..., None] * kt[:, :, None, :]
        return S, jnp.einsum('bhij,bhj->bhi', S, rt)

    seq = [jnp.moveaxis(t, 1, 0) for t in (r_h, w_h, k_h, v_h, a_vec, b_vec)]
    S, y = lax.scan(step, S0.astype(F32), seq)
    y = jnp.moveaxis(y, 0, 1)
    ym = jnp.mean(y, axis=-1, keepdims=True)
    yv = jnp.mean(jnp.square(y - ym), axis=-1, keepdims=True)
    y = ((y - ym) * lax.rsqrt(yv + B_GN_EPS)).reshape(b, L, B_W) * ln_w + ln_b
    y = y + (jnp.sum(r_h * k_h * r_k, axis=-1, keepdims=True) * v_h).reshape(b, L, B_W)
    return y * g, S, z[:, -1]


def even_mixer(h, C0, n0, m0, S0, shift0, w_in, b_i, b_f, m_norm, mu, w0, w2, a0, a2, g2,
               k_k, k_a, r_k, ln_w, ln_b, w_out):
    b, L, _ = h.shape
    zin = h @ w_in
    za = zin[..., :A_COLS].astype(F32)
    zb = zin[..., A_COLS:]
    q = za[..., :A_QK].reshape(b, L, A_HEADS, A_DK) * (A_DK ** -0.5)
    k = za[..., A_QK:2 * A_QK].reshape(b, L, A_HEADS, A_DK)
    v = za[..., 2 * A_QK:2 * A_QK + A_V].reshape(b, L, A_HEADS, A_DV)
    o = za[..., 2 * A_QK + A_V:2 * A_QK + 2 * A_V]
    gi = za[..., 2 * A_QK + 2 * A_V:2 * A_QK + 2 * A_V + A_HEADS] + b_i
    logf = jax.nn.log_sigmoid(za[..., 2 * A_QK + 2 * A_V + A_HEADS:] + b_f)
    hA, C, n, m = mlstm_chunked(q, k, v, gi, logf, C0, n0, m0)
    hA = hA * lax.rsqrt(jnp.mean(hA * hA, axis=-1, keepdims=True) + EPS)
    hA = hA.reshape(b, L, A_V) * m_norm * jax.nn.sigmoid(o)
    hB, S, shift = rwkv7_mixer(zb, shift0, S0, mu, w0, w2, a0, a2, g2, k_k, k_a, r_k, ln_w, ln_b)
    out = jnp.concatenate([hA, hB], axis=-1).astype(h.dtype) @ w_out
    return out, (C, n, m, S, shift)


def ssd_chunked(x, dt, A, Bm, Cm, S0):
    L = x.shape[1]
    T = math.gcd(L, C_CHUNK)
    causal = jnp.tril(jnp.ones((T, T), bool))[None, :, :, None, None]

    def step(S, inp):
        xc, dtc, Bc, Cc = inp
        cum = jnp.cumsum(dtc * A, axis=1)
        seg = jnp.exp(jnp.where(causal, cum[:, :, None] - cum[:, None, :], -jnp.inf))
        cb = jnp.einsum('btgn,bsgn->btsg', Cc, Bc)
        mix = cb[..., None] * seg * dtc[:, None]
        y = jnp.einsum('btsgr,bsgrp->btgrp', mix, xc)
        y = y + jnp.exp(cum)[..., None] * jnp.einsum('btgn,bgrpn->btgrp', Cc, S)
        tail = jnp.exp(cum[:, -1:] - cum) * dtc
        S = jnp.exp(cum[:, -1])[..., None, None] * S + jnp.einsum('bsgn,bsgr,bsgrp->bgrpn', Bc, tail, xc)
        return S, y

    S, y = lax.scan(step, S0.astype(F32), [to_chunks(a, T) for a in (x, dt, Bm, Cm)])
    return from_chunks(y), S


def odd_mixer(h, ssm0, conv0, w_in, conv_w, conv_b, dt_bias, a_log, d_skip, norm_w, w_out):
    b, L, _ = h.shape
    zin = h @ w_in
    z = zin[..., :C_INNER].astype(F32)
    xbc = zin[..., C_INNER:C_INNER + C_CONV_DIM]
    dt = zin[..., C_INNER + C_CONV_DIM:].astype(F32)
    xpad = jnp.concatenate([conv0.astype(xbc.dtype), xbc], axis=1)
    xc = conv_b.astype(F32) + sum(xpad[:, j:j + L].astype(F32) * conv_w[j].astype(F32) for j in range(C_CONV))
    xc = jax.nn.silu(xc)
    new_conv = xpad[:, L:]
    xs = xc[..., :C_INNER].reshape(b, L, C_GROUPS, C_HPG, C_HEADDIM)
    Bm = xc[..., C_INNER:C_INNER + C_GROUPS * C_STATE].reshape(b, L, C_GROUPS, C_STATE)
    Cm = xc[..., C_INNER + C_GROUPS * C_STATE:].reshape(b, L, C_GROUPS, C_STATE)
    dt = jax.nn.softplus(dt + dt_bias).reshape(b, L, C_GROUPS, C_HPG)
    A = -jnp.exp(a_log.astype(F32)).reshape(C_GROUPS, C_HPG)
    S0 = ssm0.reshape(b, C_GROUPS, C_HPG, C_HEADDIM, C_STATE)
    y, S = ssd_chunked(xs, dt, A, Bm, Cm, S0)
    y = y + d_skip.reshape(C_GROUPS, C_HPG)[..., None] * xs
    y = y.reshape(b, L, C_INNER) * jax.nn.silu(z)
    yg = y.reshape(b, L, C_GROUPS, C_INNER // C_GROUPS)
    yg = yg * lax.rsqrt(jnp.mean(yg * yg, axis=-1, keepdims=True) + EPS)
    y = yg.reshape(b, L, C_INNER) * norm_w
    out = y.astype(h.dtype) @ w_out
    return out, (S.reshape(b, C_HEADS, C_HEADDIM, C_STATE), new_conv)


def trunk(x, p, mC, mn, mm, rS, rsh, ssm, conv, shared, even_p, odd_p):
    norm_mix, norm_ffn, w_ffn_up, w_ffn_down, w_ple_proj, norm_ple, w_ple_gate, norm_final = shared
    new_even = []
    new_odd = []
    for i in range(DEPTH):
        j = i // 2
        h = rmsnorm(x, norm_mix[i])
        if i % 2 == 0:
            out, st = even_mixer(h, mC[j], mn[j], mm[j], rS[j], rsh[j], *[t[j] for t in even_p])
            new_even.append(st)
        else:
            out, st = odd_mixer(h, ssm[j], conv[j], *[t[j] for t in odd_p])
            new_odd.append(st)
        x = x + out
        x = x + jnp.square(jax.nn.relu(rmsnorm(x, norm_ffn[i]) @ w_ffn_up[i])) @ w_ffn_down[i]
        gate = jax.nn.sigmoid(rmsnorm(x, norm_ple[i]) @ w_ple_gate[i])
        x = x + (p[i].astype(x.dtype) @ w_ple_proj[i]) * gate
    y = rmsnorm(x, norm_final)
    st_e = lambda idx: jnp.stack([s[idx] for s in new_even])
    st_o = lambda idx: jnp.stack([s[idx] for s in new_odd])
    return y, st_e(0), st_e(1), st_e(2), st_e(3), st_e(4), st_o(0), st_o(1)


def setup_inputs(seed: int = 0) -> dict:
    key = jax.random.key(seed)
    ks = iter(jax.random.split(key, 64))
    nrm = lambda shape, s=1.0: s * jax.random.normal(next(ks), shape, F32)
    gain = lambda shape: 1.0 + 0.01 * jax.random.normal(next(ks), shape, F32)
    dt0 = jnp.exp(jax.random.uniform(next(ks), (N_ODD, C_HEADS), F32, math.log(1e-3), math.log(1e-1)))
    return {
        "x_prompt": nrm((BATCH, SEQ, D_MODEL)),
        "x_sample": nrm((DEC_BATCH, DEC_SEQ, D_MODEL)),
        "state_mlstm_C": nrm((N_EVEN, DEC_BATCH, A_HEADS, A_DK, A_DV), 0.05),
        "state_mlstm_n": nrm((N_EVEN, DEC_BATCH, A_HEADS, A_DK), 0.1),
        "state_mlstm_m": nrm((N_EVEN, DEC_BATCH, A_HEADS), 0.5),
        "state_rwkv_S": nrm((N_EVEN, DEC_BATCH, B_HEADS, B_DH, B_DH), 0.1),
        "state_rwkv_shift": nrm((N_EVEN, DEC_BATCH, B_COLS)),
        "state_ssm": nrm((N_ODD, DEC_BATCH, C_HEADS, C_HEADDIM, C_STATE), 0.1),
        "state_conv": nrm((N_ODD, DEC_BATCH, C_CONV - 1, C_CONV_DIM)),
        "p_prompt": nrm((DEPTH, BATCH, SEQ, PLE_DIM)),
        "p_sample": nrm((DEPTH, DEC_BATCH, DEC_SEQ, PLE_DIM)),
        "norm_mix": gain((DEPTH, D_MODEL)),
        "norm_ffn": gain((DEPTH, D_MODEL)),
        "w_ffn_up": nrm((DEPTH, D_MODEL, D_FF), D_MODEL ** -0.5),
        "w_ffn_down": nrm((DEPTH, D_FF, D_MODEL), D_FF ** -0.5),
        "w_ple_proj": nrm((DEPTH, PLE_DIM, D_MODEL), PLE_DIM ** -0.5),
        "norm_ple": gain((DEPTH, D_MODEL)),
        "w_ple_gate": nrm((DEPTH, D_MODEL, D_MODEL), D_MODEL ** -0.5),
        "norm_final": gain((D_MODEL,)),
        "w_in_even": nrm((N_EVEN, D_MODEL, EVEN_COLS), D_MODEL ** -0.5),
        "mlstm_b_i": nrm((N_EVEN, A_HEADS), 0.1) - 1.0,
        "mlstm_b_f": nrm((N_EVEN, A_HEADS), 0.5) + 3.0,
        "mlstm_norm": gain((N_EVEN, A_V)),
        "rwkv_mu": jax.random.uniform(next(ks), (N_EVEN, B_COLS), F32),
        "rwkv_w0": nrm((N_EVEN, B_W), 0.5),
        "rwkv_w2": nrm((N_EVEN, B_W_RANK, B_W), B_W_RANK ** -0.5),
        "rwkv_a0": nrm((N_EVEN, B_W), 0.1),
        "rwkv_a2": nrm((N_EVEN, B_A_RANK, B_W), B_A_RANK ** -0.5),
        "rwkv_g2": nrm((N_EVEN, B_G_RANK, B_W), B_G_RANK ** -0.5),
        "rwkv_k_k": 1.0 + nrm((N_EVEN, B_W), 0.1),
        "rwkv_k_a": 1.0 + nrm((N_EVEN, B_W), 0.1),
        "rwkv_r_k": nrm((N_EVEN, B_HEADS, B_DH), 0.1),
        "rwkv_ln_w": gain((N_EVEN, B_W)),
        "rwkv_ln_b": nrm((N_EVEN, B_W), 0.01),
        "w_out_even": nrm((N_EVEN, EVEN_OUT, D_MODEL), EVEN_OUT ** -0.5),
        "w_in_odd": nrm((N_ODD, D_MODEL, ODD_COLS), D_MODEL ** -0.5),
        "conv_w": nrm((N_ODD, C_CONV, C_CONV_DIM), C_CONV ** -0.5),
        "conv_b": nrm((N_ODD, C_CONV_DIM), 0.01),
        "dt_bias": dt0 + jnp.log(-jnp.expm1(-dt0)),
        "a_log": jnp.log(jax.random.uniform(next(ks), (N_ODD, C_HEADS), F32, 1.0, 16.0)),
        "d_skip": 1.0 + nrm((N_ODD, C_HEADS), 0.1),
        "ssm_norm": gain((N_ODD, C_INNER)),
        "w_out_odd": nrm((N_ODD, C_INNER, D_MODEL), C_INNER ** -0.5),
    }


def reference(x_prompt, x_sample, state_mlstm_C, state_mlstm_n, state_mlstm_m, state_rwkv_S,
              state_rwkv_shift, state_ssm, state_conv, p_prompt, p_sample,
              norm_mix, norm_ffn, w_ffn_up, w_ffn_down, w_ple_proj, norm_ple, w_ple_gate, norm_final,
              w_in_even, mlstm_b_i, mlstm_b_f, mlstm_norm, rwkv_mu, rwkv_w0, rwkv_w2, rwkv_a0, rwkv_a2,
              rwkv_g2, rwkv_k_k, rwkv_k_a, rwkv_r_k, rwkv_ln_w, rwkv_ln_b, w_out_even,
              w_in_odd, conv_w, conv_b, dt_bias, a_log, d_skip, ssm_norm, w_out_odd):
    shared = (norm_mix, norm_ffn, w_ffn_up, w_ffn_down, w_ple_proj, norm_ple, w_ple_gate, norm_final)
    even_p = (w_in_even, mlstm_b_i, mlstm_b_f, mlstm_norm, rwkv_mu, rwkv_w0, rwkv_w2, rwkv_a0, rwkv_a2,
              rwkv_g2, rwkv_k_k, rwkv_k_a, rwkv_r_k, rwkv_ln_w, rwkv_ln_b, w_out_even)
    odd_p = (w_in_odd, conv_w, conv_b, dt_bias, a_log, d_skip, ssm_norm, w_out_odd)

    bp = x_prompt.shape[0]
    z_mC = jnp.zeros((N_EVEN, bp, A_HEADS, A_DK, A_DV), F32)
    z_mn = jnp.zeros((N_EVEN, bp, A_HEADS, A_DK), F32)
    z_mm = jnp.zeros((N_EVEN, bp, A_HEADS), F32)
    z_rS = jnp.zeros((N_EVEN, bp, B_HEADS, B_DH, B_DH), F32)
    z_rsh = jnp.zeros((N_EVEN, bp, B_COLS), x_prompt.dtype)
    z_ssm = jnp.zeros((N_ODD, bp, C_HEADS, C_HEADDIM, C_STATE), F32)
    z_conv = jnp.zeros((N_ODD, bp, C_CONV - 1, C_CONV_DIM), x_prompt.dtype)
    y_prompt, mC_p, mn_p, mm_p, rS_p, rsh_p, ssm_p, conv_p = trunk(
        x_prompt, p_prompt, z_mC, z_mn, z_mm, z_rS, z_rsh, z_ssm, z_conv, shared, even_p, odd_p)

    y_sample, mC_s, mn_s, mm_s, rS_s, rsh_s, ssm_s, conv_s = trunk(
        x_sample, p_sample, state_mlstm_C, state_mlstm_n, state_mlstm_m, state_rwkv_S,
        state_rwkv_shift, state_ssm, state_conv, shared, even_p, odd_p)

    return (y_prompt, y_sample, mC_p, mn_p, mm_p, rS_p, rsh_p, ssm_p, conv_p,
            mC_s, mn_s, mm_s, rS_s, rsh_s, ssm_s, conv_s)
```

```python
import math
import functools
import jax
import jax.numpy as jnp
from jax import lax
from jax.experimental import pallas as pl
from jax.experimental.pallas import tpu as pltpu

D_MODEL = 1024
DEPTH = 4
F32 = jnp.float32
EPS = 1e-6
N_EVEN = (DEPTH + 1) // 2
N_ODD = DEPTH // 2
D_FF = 4 * D_MODEL
PLE_DIM = 256

A_HEADS = 4
A_DK = D_MODEL // 8
A_DV = D_MODEL // 8
A_CHUNK = 64
A_QK = A_HEADS * A_DK
A_V = A_HEADS * A_DV
A_COLS = 2 * A_QK + 2 * A_V + 2 * A_HEADS

B_HEADS = 8
B_DH = 64
B_W = B_HEADS * B_DH
B_W_RANK = 64
B_A_RANK = 64
B_G_RANK = 128
B_COLS = 3 * B_W + B_W_RANK + B_A_RANK + B_G_RANK
B_DECAY_OFFSET = 0.5
B_GN_EPS = 64e-5

EVEN_COLS = A_COLS + B_COLS
EVEN_OUT = A_V + B_W

C_INNER = 2 * D_MODEL
C_HEADDIM = 64
C_HEADS = C_INNER // C_HEADDIM
C_GROUPS = 4
C_HPG = C_HEADS // C_GROUPS
C_STATE = 128
C_CONV = 4
C_CHUNK = 128
C_CONV_DIM = C_INNER + 2 * C_GROUPS * C_STATE
ODD_COLS = C_INNER + C_CONV_DIM + C_HEADS


def rmsnorm(x, g):
    xf = x.astype(F32)
    y = xf * lax.rsqrt(jnp.mean(xf * xf, axis=-1, keepdims=True) + EPS)
    return (y * g.astype(F32)).astype(x.dtype)


def _final_norm_kernel(x_ref, g_ref, o_ref):
    x = x_ref[...]
    y = x * lax.rsqrt(jnp.mean(x * x, axis=-1, keepdims=True) + EPS)
    o_ref[...] = y * g_ref[...]


def final_norm(x, g):
    shp = x.shape
    x2 = x.reshape(-1, D_MODEL)
    n = x2.shape[0]
    tm = 512
    out = pl.pallas_call(
        _final_norm_kernel,
        grid=(n // tm,),
        in_specs=[pl.BlockSpec((tm, D_MODEL), lambda i: (i, 0)),
                  pl.BlockSpec((1, D_MODEL), lambda i: (0, 0))],
        out_specs=pl.BlockSpec((tm, D_MODEL), lambda i: (i, 0)),
        out_shape=jax.ShapeDtypeStruct((n, D_MODEL), F32),
    )(x2, g.reshape(1, D_MODEL))
    return out.reshape(shp)


def to_chunks(a, T):
    b, L = a.shape[:2]
    return jnp.moveaxis(a.reshape((b, L // T, T) + a.shape[2:]), 1, 0)


def from_chunks(a):
    nc, b, T = a.shape[:3]
    return jnp.moveaxis(a, 0, 1).reshape((b, nc * T) + a.shape[3:])


def mlstm_chunked(q, k, v, logi, logf, C0, n0, m0):
    L = q.shape[1]
    T = math.gcd(L, A_CHUNK)
    causal = jnp.tril(jnp.ones((T, T), bool))[None, :, :, None]

    def step(carry, inp):
        C, n, m = carry
        qc, kc, vc, li, lf = inp
        bc = jnp.cumsum(lf, axis=1)
        dmat = jnp.where(causal, bc[:, :, None] - bc[:, None, :] + li[:, None, :], -jnp.inf)
        inter = bc + m[:, None]
        mt = jnp.maximum(inter, jnp.max(dmat, axis=2))
        wq = jnp.einsum('bthd,bshd->btsh', qc, kc) * jnp.exp(dmat - mt[:, :, None])
        wi = jnp.exp(inter - mt)
        num = jnp.einsum('btsh,bshe->bthe', wq, vc) + wi[..., None] * jnp.einsum('bthd,bhde->bthe', qc, C)
        den = jnp.sum(wq, axis=2) + wi * jnp.einsum('bthd,bhd->bth', qc, n)
        h = num / jnp.maximum(jnp.abs(den), jnp.exp(-mt))[..., None]
        bT = bc[:, -1]
        gs = bT[:, None] - bc + li
        m_new = jnp.maximum(bT + m, jnp.max(gs, axis=1))
        decay = jnp.exp(bT + m - m_new)
        ws = jnp.exp(gs - m_new[:, None])
        C_new = decay[..., None, None] * C + jnp.einsum('bsh,bshd,bshe->bhde', ws, kc, vc)
        n_new = decay[..., None] * n + jnp.einsum('bsh,bshd->bhd', ws, kc)
        return (C_new, n_new, m_new), h

    init = (C0.astype(F32), n0.astype(F32), m0.astype(F32))
    (C, n, m), h = lax.scan(step, init, [to_chunks(a, T) for a in (q, k, v, logi, logf)])
    return from_chunks(h), C, n, m


def rwkv7_mixer(z, shift0, S0, mu, w0, w2, a0, a2, g2, k_k, k_a, r_k, ln_w, ln_b):
    b, L, _ = z.shape
    zf = z.astype(F32)
    zprev = jnp.concatenate([shift0[:, None].astype(F32), zf[:, :-1]], axis=1)
    zs = zf + (zprev - zf) * mu
    o1, o2, o3 = B_W, 2 * B_W, 3 * B_W
    o4, o5 = o3 + B_W_RANK, o3 + B_W_RANK + B_A_RANK
    r, k, v = zs[..., :o1], zs[..., o1:o2], zs[..., o2:o3]
    zw, za, zg = zs[..., o3:o4], zs[..., o4:o5], zs[..., o5:]
    w_log = -jax.nn.softplus(-(w0 + jnp.tanh(zw) @ w2)) - B_DECAY_OFFSET
    decay = jnp.exp(-jnp.exp(w_log))
    a = jax.nn.sigmoid(a0 + za @ a2)
    g = jax.nn.sigmoid(zg) @ g2
    heads = lambda t: t.reshape(b, L, B_HEADS, B_DH)
    kk = heads(k * k_k)
    kk = kk * lax.rsqrt(jnp.maximum(jnp.sum(kk * kk, axis=-1, keepdims=True), 1e-24))
    k = k * (1.0 + (a - 1.0) * k_a)
    r_h, w_h, k_h, v_h, a_h = heads(r), heads(decay), heads(k), heads(v), heads(a)
    a_vec = -kk
    b_vec = kk * a_h

    def step(S, inp):
        rt, wt, kt, vt, at, bt = inp
        sa = jnp.einsum('bhij,bhj->bhi', S, at)
        S = S * wt[:, :, None, :] + sa[..., None] * bt[:, :, None, :] + vt[..., None] * kt[:, :, None, :]
        return S, jnp.einsum('bhij,bhj->bhi', S, rt)

    seq = [jnp.moveaxis(t, 1, 0) for t in (r_h, w_h, k_h, v_h, a_vec, b_vec)]
    S, y = lax.scan(step, S0.astype(F32), seq)
    y = jnp.moveaxis(y, 0, 1)
    ym = jnp.mean(y, axis=-1, keepdims=True)
    yv = jnp.mean(jnp.square(y - ym), axis=-1, keepdims=True)
    y = ((y - ym) * lax.rsqrt(yv + B_GN_EPS)).reshape(b, L, B_W) * ln_w + ln_b
    y = y + (jnp.sum(r_h * k_h * r_k, axis=-1, keepdims=True) * v_h).reshape(b, L, B_W)
    return y * g, S, z[:, -1]


def even_mixer(h, C0, n0, m0, S0, shift0, w_in, b_i, b_f, m_norm, mu, w0, w2, a0, a2, g2,
               k_k, k_a, r_k, ln_w, ln_b, w_out):
    b, L, _ = h.shape
    zin = h @ w_in
    za = zin[..., :A_COLS].astype(F32)
    zb = zin[..., A_COLS:]
    q = za[..., :A_QK].reshape(b, L, A_HEADS, A_DK) * (A_DK ** -0.5)
    k = za[..., A_QK:2 * A_QK].reshape(b, L, A_HEADS, A_DK)
    v = za[..., 2 * A_QK:2 * A_QK + A_V].reshape(b, L, A_HEADS, A_DV)
    o = za[..., 2 * A_QK + A_V:2 * A_QK + 2 * A_V]
    gi = za[..., 2 * A_QK + 2 * A_V:2 * A_QK + 2 * A_V + A_HEADS] + b_i
    logf = jax.nn.log_sigmoid(za[..., 2 * A_QK + 2 * A_V + A_HEADS:] + b_f)
    hA, C, n, m = mlstm_chunked(q, k, v, gi, logf, C0, n0, m0)
    hA = hA * lax.rsqrt(jnp.mean(hA * hA, axis=-1, keepdims=True) + EPS)
    hA = hA.reshape(b, L, A_V) * m_norm * jax.nn.sigmoid(o)
    hB, S, shift = rwkv7_mixer(zb, shift0, S0, mu, w0, w2, a0, a2, g2, k_k, k_a, r_k, ln_w, ln_b)
    out = jnp.concatenate([hA, hB], axis=-1).astype(h.dtype) @ w_out
    return out, (C, n, m, S, shift)


def ssd_chunked(x, dt, A, Bm, Cm, S0):
    L = x.shape[1]
    T = math.gcd(L, C_CHUNK)
    causal = jnp.tril(jnp.ones((T, T), bool))[None, :, :, None, None]

    def step(S, inp):
        xc, dtc, Bc, Cc = inp
        cum = jnp.cumsum(dtc * A, axis=1)
        seg = jnp.exp(jnp.where(causal, cum[:, :, None] - cum[:, None, :], -jnp.inf))
        cb = jnp.einsum('btgn,bsgn->btsg', Cc, Bc)
        mix = cb[..., None] * seg * dtc[:, None]
        y = jnp.einsum('btsgr,bsgrp->btgrp', mix, xc)
        y = y + jnp.exp(cum)[..., None] * jnp.einsum('btgn,bgrpn->btgrp', Cc, S)
        tail = jnp.exp(cum[:, -1:] - cum) * dtc
        S = jnp.exp(cum[:, -1])[..., None, None] * S + jnp.einsum('bsgn,bsgr,bsgrp->bgrpn', Bc, tail, xc)
        return S, y

    S, y = lax.scan(step, S0.astype(F32), [to_chunks(a, T) for a in (x, dt, Bm, Cm)])
    return from_chunks(y), S


def odd_mixer(h, ssm0, conv0, w_in, conv_w, conv_b, dt_bias, a_log, d_skip, norm_w, w_out):
    b, L, _ = h.shape
    zin = h @ w_in
    z = zin[..., :C_INNER].astype(F32)
    xbc = zin[..., C_INNER:C_INNER + C_CONV_DIM]
    dt = zin[..., C_INNER + C_CONV_DIM:].astype(F32)
    xpad = jnp.concatenate([conv0.astype(xbc.dtype), xbc], axis=1)
    xc = conv_b.astype(F32) + sum(xpad[:, j:j + L].astype(F32) * conv_w[j].astype(F32) for j in range(C_CONV))
    xc = jax.nn.silu(xc)
    new_conv = xpad[:, L:]
    xs = xc[..., :C_INNER].reshape(b, L, C_GROUPS, C_HPG, C_HEADDIM)
    Bm = xc[..., C_INNER:C_INNER + C_GROUPS * C_STATE].reshape(b, L, C_GROUPS, C_STATE)
    Cm = xc[..., C_INNER + C_GROUPS * C_STATE:].reshape(b, L, C_GROUPS, C_STATE)
    dt = jax.nn.softplus(dt + dt_bias).reshape(b, L, C_GROUPS, C_HPG)
    A = -jnp.exp(a_log.astype(F32)).reshape(C_GROUPS, C_HPG)
    S0 = ssm0.reshape(b, C_GROUPS, C_HPG, C_HEADDIM, C_STATE)
    y, S = ssd_chunked(xs, dt, A, Bm, Cm, S0)
    y = y + d_skip.reshape(C_GROUPS, C_HPG)[..., None] * xs
    y = y.reshape(b, L, C_INNER) * jax.nn.silu(z)
    yg = y.reshape(b, L, C_GROUPS, C_INNER // C_GROUPS)
    yg = yg * lax.rsqrt(jnp.mean(yg * yg, axis=-1, keepdims=True) + EPS)
    y = yg.reshape(b, L, C_INNER) * norm_w
    out = y.astype(h.dtype) @ w_out
    return out, (S.reshape(b, C_HEADS, C_HEADDIM, C_STATE), new_conv)


def trunk(x, p, mC, mn, mm, rS, rsh, ssm, conv, shared, even_p, odd_p):
    norm_mix, norm_ffn, w_ffn_up, w_ffn_down, w_ple_proj, norm_ple, w_ple_gate, norm_final = shared
    new_even = []
    new_odd = []
    for i in range(DEPTH):
        j = i // 2
        h = rmsnorm(x, norm_mix[i])
        if i % 2 == 0:
            out, st = even_mixer(h, mC[j], mn[j], mm[j], rS[j], rsh[j], *[t[j] for t in even_p])
            new_even.append(st)
        else:
            out, st = odd_mixer(h, ssm[j], conv[j], *[t[j] for t in odd_p])
            new_odd.append(st)
        x = x + out
        x = x + jnp.square(jax.nn.relu(rmsnorm(x, norm_ffn[i]) @ w_ffn_up[i])) @ w_ffn_down[i]
        gate = jax.nn.sigmoid(rmsnorm(x, norm_ple[i]) @ w_ple_gate[i])
        x = x + (p[i].astype(x.dtype) @ w_ple_proj[i]) * gate
    y = final_norm(x, norm_final)
    st_e = lambda idx: jnp.stack([s[idx] for s in new_even])
    st_o = lambda idx: jnp.stack([s[idx] for s in new_odd])
    return y, st_e(0), st_e(1), st_e(2), st_e(3), st_e(4), st_o(0), st_o(1)


def kernel(x_prompt, x_sample, state_mlstm_C, state_mlstm_n, state_mlstm_m, state_rwkv_S,
           state_rwkv_shift, state_ssm, state_conv, p_prompt, p_sample,
           norm_mix, norm_ffn, w_ffn_up, w_ffn_down, w_ple_proj, norm_ple, w_ple_gate, norm_final,
           w_in_even, mlstm_b_i, mlstm_b_f, mlstm_norm, rwkv_mu, rwkv_w0, rwkv_w2, rwkv_a0, rwkv_a2,
           rwkv_g2, rwkv_k_k, rwkv_k_a, rwkv_r_k, rwkv_ln_w, rwkv_ln_b, w_out_even,
           w_in_odd, conv_w, conv_b, dt_bias, a_log, d_skip, ssm_norm, w_out_odd):
    shared = (norm_mix, norm_ffn, w_ffn_up, w_ffn_down, w_ple_proj, norm_ple, w_ple_gate, norm_final)
    even_p = (w_in_even, mlstm_b_i, mlstm_b_f, mlstm_norm, rwkv_mu, rwkv_w0, rwkv_w2, rwkv_a0, rwkv_a2,
              rwkv_g2, rwkv_k_k, rwkv_k_a, rwkv_r_k, rwkv_ln_w, rwkv_ln_b, w_out_even)
    odd_p = (w_in_odd, conv_w, conv_b, dt_bias, a_log, d_skip, ssm_norm, w_out_odd)
    bp = x_prompt.shape[0]
    z_mC = jnp.zeros((N_EVEN, bp, A_HEADS, A_DK, A_DV), F32)
    z_mn = jnp.zeros((N_EVEN, bp, A_HEADS, A_DK), F32)
    z_mm = jnp.zeros((N_EVEN, bp, A_HEADS), F32)
    z_rS = jnp.zeros((N_EVEN, bp, B_HEADS, B_DH, B_DH), F32)
    z_rsh = jnp.zeros((N_EVEN, bp, B_COLS), x_prompt.dtype)
    z_ssm = jnp.zeros((N_ODD, bp, C_HEADS, C_HEADDIM, C_STATE), F32)
    z_conv = jnp.zeros((N_ODD, bp, C_CONV - 1, C_CONV_DIM), x_prompt.dtype)
    y_prompt, mC_p, mn_p, mm_p, rS_p, rsh_p, ssm_p, conv_p = trunk(
        x_prompt, p_prompt, z_mC, z_mn, z_mm, z_rS, z_rsh, z_ssm, z_conv, shared, even_p, odd_p)
    y_sample, mC_s, mn_s, mm_s, rS_s, rsh_s, ssm_s, conv_s = trunk(
        x_sample, p_sample, state_mlstm_C, state_mlstm_n, state_mlstm_m, state_rwkv_S,
        state_rwkv_shift, state_ssm, state_conv, shared, even_p, odd_p)
    return (y_prompt, y_sample, mC_p, mn_p, mm_p, rS_p, rsh_p, ssm_p, conv_p,
            mC_s, mn_s, mm_s, rS_s, rsh_s, ssm_s, conv_s)
```

```python
import math
import functools
import jax
import jax.numpy as jnp
from jax import lax
from jax.experimental import pallas as pl
from jax.experimental.pallas import tpu as pltpu

D_MODEL = 1024
DEPTH = 4
F32 = jnp.float32
BF16 = jnp.bfloat16
EPS = 1e-6
N_EVEN = (DEPTH + 1) // 2
N_ODD = DEPTH // 2
D_FF = 4 * D_MODEL
PLE_DIM = 256

A_HEADS = 4
A_DK = D_MODEL // 8
A_DV = D_MODEL // 8
A_CHUNK = 64
A_QK = A_HEADS * A_DK
A_V = A_HEADS * A_DV
A_COLS = 2 * A_QK + 2 * A_V + 2 * A_HEADS

B_HEADS = 8
B_DH = 64
B_W = B_HEADS * B_DH
B_W_RANK = 64
B_A_RANK = 64
B_G_RANK = 128
B_COLS = 3 * B_W + B_W_RANK + B_A_RANK + B_G_RANK
B_DECAY_OFFSET = 0.5
B_GN_EPS = 64e-5

EVEN_COLS = A_COLS + B_COLS
EVEN_OUT = A_V + B_W

C_INNER = 2 * D_MODEL
C_HEADDIM = 64
C_HEADS = C_INNER // C_HEADDIM
C_GROUPS = 4
C_HPG = C_HEADS // C_GROUPS
C_STATE = 128
C_CONV = 4
C_CHUNK = 128
C_CONV_DIM = C_INNER + 2 * C_GROUPS * C_STATE
ODD_COLS = C_INNER + C_CONV_DIM + C_HEADS

LANES = 128
VMEM_LIMIT = 56 * 1024 * 1024
TOKEN_TILE = 256

EVEN_GATE_PAD = 256
EVEN_N = 2 * A_QK + 2 * A_V + B_COLS + EVEN_GATE_PAD
EVEN_B_OFF = 2 * A_QK + 2 * A_V
EVEN_G_OFF = EVEN_B_OFF + B_COLS
ODD_DT_PAD = 256
ODD_N = C_INNER + C_CONV_DIM + ODD_DT_PAD
ODD_DT_OFF = C_INNER + C_CONV_DIM


def _cparams(sem):
    return pltpu.CompilerParams(dimension_semantics=sem, vmem_limit_bytes=VMEM_LIMIT)


def _rms(x, g):
    return x * lax.rsqrt(jnp.mean(x * x, axis=-1, keepdims=True) + EPS) * g


def _resident(shape):
    nd = len(shape)
    return pl.BlockSpec(shape, lambda *_: (0,) * nd, pipeline_mode=pl.Buffered(1))


def _rows(width):
    return pl.BlockSpec((TOKEN_TILE, width), lambda i: (i, 0))


def _norm_matmul_kernel(x_ref, g_ref, w_ref, o_ref, *, tn):
    xn = _rms(x_ref[...], g_ref[...]).astype(BF16)
    for n0 in range(0, w_ref.shape[1], tn):
        o_ref[:, n0:n0 + tn] = jnp.dot(xn, w_ref[:, n0:n0 + tn], preferred_element_type=F32)


def norm_matmul(x, g, w, tn):
    m, n = x.shape[0], w.shape[1]
    return pl.pallas_call(
        functools.partial(_norm_matmul_kernel, tn=tn),
        grid=(m // TOKEN_TILE,),
        in_specs=[_rows(D_MODEL), _resident((1, D_MODEL)), _resident(w.shape)],
        out_specs=_rows(n),
        out_shape=jax.ShapeDtypeStruct((m, n), F32),
        compiler_params=_cparams(("arbitrary",)),
        name="norm_matmul",
    )(x, g.reshape(1, D_MODEL), w)


def _matmul_res_kernel(a_ref, w_ref, x_ref, o_ref):
    o_ref[...] = x_ref[...] + jnp.dot(a_ref[...].astype(BF16), w_ref[...], preferred_element_type=F32)


def matmul_res(a, w, x):
    m, k = a.shape
    return pl.pallas_call(
        _matmul_res_kernel,
        grid=(m // TOKEN_TILE,),
        in_specs=[_rows(k), _resident(w.shape), _rows(D_MODEL)],
        out_specs=_rows(D_MODEL),
        out_shape=jax.ShapeDtypeStruct((m, D_MODEL), F32),
        compiler_params=_cparams(("arbitrary",)),
        name="matmul_res",
    )(a, w, x)


FFN_CHUNK = 512


def _ffn_kernel(x_ref, g_ref, wu_ref, wd_ref, o_ref):
    x = x_ref[...]
    xn = _rms(x, g_ref[...]).astype(BF16)
    acc = x
    for c0 in range(0, D_FF, FFN_CHUNK):
        h = jnp.dot(xn, wu_ref[:, c0:c0 + FFN_CHUNK], preferred_element_type=F32)
        h = jnp.square(jnp.maximum(h, 0.0)).astype(BF16)
        acc = acc + jnp.dot(h, wd_ref[c0:c0 + FFN_CHUNK, :], preferred_element_type=F32)
    o_ref[...] = acc


def ffn(x, g, wu, wd):
    m = x.shape[0]
    return pl.pallas_call(
        _ffn_kernel,
        grid=(m // TOKEN_TILE,),
        in_specs=[_rows(D_MODEL), _resident((1, D_MODEL)), _resident(wu.shape), _resident(wd.shape)],
        out_specs=_rows(D_MODEL),
        out_shape=jax.ShapeDtypeStruct((m, D_MODEL), F32),
        compiler_params=_cparams(("arbitrary",)),
        name="ffn",
    )(x, g.reshape(1, D_MODEL), wu, wd)


def _ple_kernel(x_ref, g_ref, wg_ref, p_ref, wp_ref, o_ref):
    x = x_ref[...]
    xn = _rms(x, g_ref[...]).astype(BF16)
    gate = jax.nn.sigmoid(jnp.dot(xn, wg_ref[...], preferred_element_type=F32))
    proj = jnp.dot(p_ref[...].astype(BF16), wp_ref[...], preferred_element_type=F32)
    o_ref[...] = x + proj * gate


def ple(x, g, wg, p, wp):
    m = x.shape[0]
    return pl.pallas_call(
        _ple_kernel,
        grid=(m // TOKEN_TILE,),
        in_specs=[_rows(D_MODEL), _resident((1, D_MODEL)), _resident(wg.shape),
                  _rows(PLE_DIM), _resident(wp.shape)],
        out_specs=_rows(D_MODEL),
        out_shape=jax.ShapeDtypeStruct((m, D_MODEL), F32),
        compiler_params=_cparams(("arbitrary",)),
        name="ple",
    )(x, g.reshape(1, D_MODEL), wg, p, wp)


def _final_norm_kernel(x_ref, g_ref, o_ref):
    o_ref[...] = _rms(x_ref[...], g_ref[...])


def final_norm(x, g):
    m = x.shape[0]
    return pl.pallas_call(
        _final_norm_kernel,
        grid=(m // TOKEN_TILE,),
        in_specs=[_rows(D_MODEL), _resident((1, D_MODEL))],
        out_specs=_rows(D_MODEL),
        out_shape=jax.ShapeDtypeStruct((m, D_MODEL), F32),
        compiler_params=_cparams(("arbitrary",)),
        name="final_norm",
    )(x, g.reshape(1, D_MODEL))


def _rwkv_scan_kernel(r_ref, w_ref, k_ref, a_ref, b_ref, v_ref, s0_ref, y_ref, s_ref, *, nlt, tc):
    @pl.when(pl.program_id(1) == 0)
    def _():
        s_ref[...] = s0_ref[...]

    def step(t, carry):
        a = a_ref[t]
        w = w_ref[t]
        b = b_ref[t]
        k = k_ref[t]
        r = r_ref[t]
        vt = v_ref[t]
        ys = []
        for lt in range(nlt):
            s = s_ref[lt]
            sa = jnp.sum(s * a, axis=0, keepdims=True)
            s = s * w + sa * b + vt[lt:lt + 1] * k
            s_ref[lt] = s
            ys.append(jnp.sum(s * r, axis=0, keepdims=True))
        y_ref[t] = jnp.concatenate(ys, axis=0)
        return carry

    lax.fori_loop(0, tc, step, 0)


def rwkv_scan(r, w, k, a, b, v, s0, nlt, tc):
    L = r.shape[0]
    G = s0.shape[0]
    col = pl.BlockSpec((tc, B_DH, LANES), lambda g, t: (t, 0, g))
    row = pl.BlockSpec((tc, nlt, LANES), lambda g, t: (t, 0, g))
    st = pl.BlockSpec((None, nlt, B_DH, LANES), lambda g, t: (g, 0, 0, 0))
    return pl.pallas_call(
        functools.partial(_rwkv_scan_kernel, nlt=nlt, tc=tc),
        grid=(G, L // tc),
        in_specs=[col, col, col, col, col, row, st],
        out_specs=[row, st],
        out_shape=[jax.ShapeDtypeStruct(v.shape, F32), jax.ShapeDtypeStruct(s0.shape, F32)],
        compiler_params=_cparams(("arbitrary", "arbitrary")),
        name="rwkv_scan",
    )(r, w, k, a, b, v, s0)


def _to_cols(x, b, L):
    return x.reshape(b, L, B_HEADS, B_DH).transpose(1, 3, 0, 2).reshape(L, B_DH, b * B_HEADS)


def rwkv_recurrence(r, w, k, v, a_vec, b_vec, S0):
    b, L, _ = r.shape
    bh = b * B_HEADS
    cols = [_to_cols(t, b, L) for t in (r, w, k, a_vec, b_vec)]
    vc = _to_cols(v, b, L)
    if bh < LANES:
        rep = LANES // bh
        nlt = B_DH // rep
        cols = [jnp.concatenate([c] * rep, axis=-1) for c in cols]
        vrow = vc.reshape(L, nlt, LANES)
        s0 = S0.reshape(b, B_HEADS, nlt, rep, B_DH).transpose(2, 4, 3, 0, 1).reshape(1, nlt, B_DH, LANES)
        tc = math.gcd(L, 32)
        y, s = rwkv_scan(*cols, vrow, s0, nlt, tc)
        y = y.reshape(L, B_DH, b, B_HEADS).transpose(2, 0, 3, 1).reshape(b, L, B_W)
        s = s.reshape(nlt, B_DH, rep, b, B_HEADS).transpose(3, 4, 0, 2, 1).reshape(b, B_HEADS, B_DH, B_DH)
        return y, s
    G = bh // LANES
    nlt = B_DH
    s0 = S0.reshape(G, LANES, B_DH, B_DH).transpose(0, 2, 3, 1)
    tc = math.gcd(L, 32)
    y, s = rwkv_scan(*cols, vc, s0, nlt, tc)
    y = y.reshape(L, B_DH, b, B_HEADS).transpose(2, 0, 3, 1).reshape(b, L, B_W)
    s = s.transpose(0, 3, 1, 2).reshape(b, B_HEADS, B_DH, B_DH)
    return y, s


def to_chunks(a, T):
    b, L = a.shape[:2]
    return jnp.moveaxis(a.reshape((b, L // T, T) + a.shape[2:]), 1, 0)


def from_chunks(a):
    nc, b, T = a.shape[:3]
    return jnp.moveaxis(a, 0, 1).reshape((b, nc * T) + a.shape[3:])


def mlstm_chunked(q, k, v, logi, logf, C0, n0, m0):
    L = q.shape[1]
    T = math.gcd(L, A_CHUNK)
    causal = jnp.tril(jnp.ones((T, T), bool))[None, :, :, None]

    def step(carry, inp):
        C, n, m = carry
        qc, kc, vc, li, lf = inp
        bc = jnp.cumsum(lf, axis=1)
        dmat = jnp.where(causal, bc[:, :, None] - bc[:, None, :] + li[:, None, :], -jnp.inf)
        inter = bc + m[:, None]
        mt = jnp.maximum(inter, jnp.max(dmat, axis=2))
        wq = jnp.einsum('bthd,bshd->btsh', qc, kc) * jnp.exp(dmat - mt[:, :, None])
        wi = jnp.exp(inter - mt)
        num = jnp.einsum('btsh,bshe->bthe', wq, vc) + wi[..., None] * jnp.einsum('bthd,bhde->bthe', qc, C)
        den = jnp.sum(wq, axis=2) + wi * jnp.einsum('bthd,bhd->bth', qc, n)
        h = num / jnp.maximum(jnp.abs(den), jnp.exp(-mt))[..., None]
        bT = bc[:, -1]
        gs = bT[:, None] - bc + li
        m_new = jnp.maximum(bT + m, jnp.max(gs, axis=1))
        decay = jnp.exp(bT + m - m_new)
        ws = jnp.exp(gs - m_new[:, None])
        C_new = decay[..., None, None] * C + jnp.einsum('bsh,bshd,bshe->bhde', ws, kc, vc)
        n_new = decay[..., None] * n + jnp.einsum('bsh,bshd->bhd', ws, kc)
        return (C_new, n_new, m_new), h

    init = (C0.astype(F32), n0.astype(F32), m0.astype(F32))
    (C, n, m), h = lax.scan(step, init, [to_chunks(a, T) for a in (q, k, v, logi, logf)])
    return from_chunks(h), C, n, m


def rwkv7_mixer(z, shift0, S0, mu, w0, w2, a0, a2, g2, k_k, k_a, r_k, ln_w, ln_b):
    b, L, _ = z.shape
    zprev = jnp.concatenate([shift0[:, None], z[:, :-1]], axis=1)
    zs = z + (zprev - z) * mu
    o1, o2, o3 = B_W, 2 * B_W, 3 * B_W
    o4, o5 = o3 + B_W_RANK, o3 + B_W_RANK + B_A_RANK
    r, k, v = zs[..., :o1], zs[..., o1:o2], zs[..., o2:o3]
    zw, za, zg = zs[..., o3:o4], zs[..., o4:o5], zs[..., o5:]
    w_log = -jax.nn.softplus(-(w0 + jnp.tanh(zw) @ w2)) - B_DECAY_OFFSET
    decay = jnp.exp(-jnp.exp(w_log))
    a = jax.nn.sigmoid(a0 + za @ a2)
    g = jax.nn.sigmoid(zg) @ g2
    heads = lambda t: t.reshape(b, L, B_HEADS, B_DH)
    kk = heads(k * k_k)
    kk = kk * lax.rsqrt(jnp.maximum(jnp.sum(kk * kk, axis=-1, keepdims=True), 1e-24))
    k = k * (1.0 + (a - 1.0) * k_a)
    r_h, k_h, v_h, a_h = heads(r), heads(k), heads(v), heads(a)
    a_vec = (-kk).reshape(b, L, B_W)
    b_vec = (kk * a_h).reshape(b, L, B_W)
    y, S = rwkv_recurrence(r, decay, k, v, a_vec, b_vec, S0)
    y = heads(y)
    ym = jnp.mean(y, axis=-1, keepdims=True)
    yv = jnp.mean(jnp.square(y - ym), axis=-1, keepdims=True)
    y = ((y - ym) * lax.rsqrt(yv + B_GN_EPS)).reshape(b, L, B_W) * ln_w + ln_b
    y = y + (jnp.sum(r_h * k_h * r_k, axis=-1, keepdims=True) * v_h).reshape(b, L, B_W)
    return y * g, S, z[:, -1]


def even_mixer(zin, C0, n0, m0, S0, shift0, b_i, b_f, m_norm, mu, w0, w2, a0, a2, g2,
               k_k, k_a, r_k, ln_w, ln_b):
    b, L, _ = zin.shape
    q = zin[..., :A_QK].reshape(b, L, A_HEADS, A_DK) * (A_DK ** -0.5)
    k = zin[..., A_QK:2 * A_QK].reshape(b, L, A_HEADS, A_DK)
    v = zin[..., 2 * A_QK:2 * A_QK + A_V].reshape(b, L, A_HEADS, A_DV)
    o = zin[..., 2 * A_QK + A_V:2 * A_QK + 2 * A_V]
    gi = zin[..., EVEN_G_OFF:EVEN_G_OFF + A_HEADS] + b_i
    logf = jax.nn.log_sigmoid(zin[..., EVEN_G_OFF + A_HEADS:EVEN_G_OFF + 2 * A_HEADS] + b_f)
    zb = zin[..., EVEN_B_OFF:EVEN_B_OFF + B_COLS]
    hA, C, n, m = mlstm_chunked(q, k, v, gi, logf, C0, n0, m0)
    hA = hA * lax.rsqrt(jnp.mean(hA * hA, axis=-1, keepdims=True) + EPS)
    hA = hA.reshape(b, L, A_V) * m_norm * jax.nn.sigmoid(o)
    hB, S, shift = rwkv7_mixer(zb, shift0, S0, mu, w0, w2, a0, a2, g2, k_k, k_a, r_k, ln_w, ln_b)
    return jnp.concatenate([hA, hB], axis=-1), (C, n, m, S, shift)


def ssd_chunked(x, dt, A, Bm, Cm, S0):
    L = x.shape[1]
    T = math.gcd(L, C_CHUNK)
    causal = jnp.tril(jnp.ones((T, T), bool))[None, :, :, None, None]

    def step(S, inp):
        xc, dtc, Bc, Cc = inp
        cum = jnp.cumsum(dtc * A, axis=1)
        seg = jnp.exp(jnp.where(causal, cum[:, :, None] - cum[:, None, :], -jnp.inf))
        cb = jnp.einsum('btgn,bsgn->btsg', Cc, Bc)
        mix = cb[..., None] * seg * dtc[:, None]
        y = jnp.einsum('btsgr,bsgrp->btgrp', mix, xc)
        y = y + jnp.exp(cum)[..., None] * jnp.einsum('btgn,bgrpn->btgrp', Cc, S)
        tail = jnp.exp(cum[:, -1:] - cum) * dtc
        S = jnp.exp(cum[:, -1])[..., None, None] * S + jnp.einsum('bsgn,bsgr,bsgrp->bgrpn', Bc, tail, xc)
        return S, y

    S, y = lax.scan(step, S0.astype(F32), [to_chunks(a, T) for a in (x, dt, Bm, Cm)])
    return from_chunks(y), S


def odd_mixer(zin, ssm0, conv0, conv_w, conv_b, dt_bias, a_log, d_skip, norm_w):
    b, L, _ = zin.shape
    z = zin[..., :C_INNER]
    xbc = zin[..., C_INNER:C_INNER + C_CONV_DIM]
    dt = zin[..., ODD_DT_OFF:ODD_DT_OFF + C_HEADS]
    xpad = jnp.concatenate([conv0, xbc], axis=1)
    xc = conv_b + sum(xpad[:, j:j + L] * conv_w[j] for j in range(C_CONV))
    xc = jax.nn.silu(xc)
    new_conv = xpad[:, L:]
    xs = xc[..., :C_INNER].reshape(b, L, C_GROUPS, C_HPG, C_HEADDIM)
    Bm = xc[..., C_INNER:C_INNER + C_GROUPS * C_STATE].reshape(b, L, C_GROUPS, C_STATE)
    Cm = xc[..., C_INNER + C_GROUPS * C_STATE:].reshape(b, L, C_GROUPS, C_STATE)
    dt = jax.nn.softplus(dt + dt_bias).reshape(b, L, C_GROUPS, C_HPG)
    A = -jnp.exp(a_log).reshape(C_GROUPS, C_HPG)
    S0 = ssm0.reshape(b, C_GROUPS, C_HPG, C_HEADDIM, C_STATE)
    y, S = ssd_chunked(xs, dt, A, Bm, Cm, S0)
    y = y + d_skip.reshape(C_GROUPS, C_HPG)[..., None] * xs
    y = y.reshape(b, L, C_INNER) * jax.nn.silu(z)
    yg = y.reshape(b, L, C_GROUPS, C_INNER // C_GROUPS)
    yg = yg * lax.rsqrt(jnp.mean(yg * yg, axis=-1, keepdims=True) + EPS)
    y = yg.reshape(b, L, C_INNER) * norm_w
    return y, (S.reshape(b, C_HEADS, C_HEADDIM, C_STATE), new_conv)


def _even_w_in(w):
    qkvo = w[:, :EVEN_B_OFF]
    gates = w[:, EVEN_B_OFF:A_COLS]
    rwkv = w[:, A_COLS:]
    pad = jnp.zeros((D_MODEL, EVEN_GATE_PAD - 2 * A_HEADS), w.dtype)
    return jnp.concatenate([qkvo, rwkv, gates, pad], axis=1).astype(BF16)


def _odd_w_in(w):
    pad = jnp.zeros((D_MODEL, ODD_DT_PAD - C_HEADS), w.dtype)
    return jnp.concatenate([w, pad], axis=1).astype(BF16)


def kernel(x_prompt, x_sample, state_mlstm_C, state_mlstm_n, state_mlstm_m, state_rwkv_S,
           state_rwkv_shift, state_ssm, state_conv, p_prompt, p_sample,
           norm_mix, norm_ffn, w_ffn_up, w_ffn_down, w_ple_proj, norm_ple, w_ple_gate, norm_final,
           w_in_even, mlstm_b_i, mlstm_b_f, mlstm_norm, rwkv_mu, rwkv_w0, rwkv_w2, rwkv_a0, rwkv_a2,
           rwkv_g2, rwkv_k_k, rwkv_k_a, rwkv_r_k, rwkv_ln_w, rwkv_ln_b, w_out_even,
           w_in_odd, conv_w, conv_b, dt_bias, a_log, d_skip, ssm_norm, w_out_odd):
    bp, Lp, _ = x_prompt.shape
    bs, Ls, _ = x_sample.shape
    n_p = bp * Lp
    x = jnp.concatenate([x_prompt.reshape(n_p, D_MODEL), x_sample.reshape(bs * Ls, D_MODEL)], axis=0)
    p_all = jnp.concatenate([p_prompt.reshape(DEPTH, n_p, PLE_DIM),
                             p_sample.reshape(DEPTH, bs * Ls, PLE_DIM)], axis=1)

    even_small = (mlstm_b_i, mlstm_b_f, mlstm_norm, rwkv_mu, rwkv_w0, rwkv_w2, rwkv_a0, rwkv_a2,
                  rwkv_g2, rwkv_k_k, rwkv_k_a, rwkv_r_k, rwkv_ln_w, rwkv_ln_b)
    odd_small = (conv_w, conv_b, dt_bias, a_log, d_skip, ssm_norm)

    zeros_even = (jnp.zeros((bp, A_HEADS, A_DK, A_DV), F32), jnp.zeros((bp, A_HEADS, A_DK), F32),
                  jnp.zeros((bp, A_HEADS), F32), jnp.zeros((bp, B_HEADS, B_DH, B_DH), F32),
                  jnp.zeros((bp, B_COLS), F32))
    zeros_odd = (jnp.zeros((bp, C_HEADS, C_HEADDIM, C_STATE), F32),
                 jnp.zeros((bp, C_CONV - 1, C_CONV_DIM), F32))

    st_p_even, st_s_even, st_p_odd, st_s_odd = [], [], [], []
    for i in range(DEPTH):
        j = i // 2
        if i % 2 == 0:
            zin = norm_matmul(x, norm_mix[i], _even_w_in(w_in_even[j]), 512)
            small = [t[j] for t in even_small]
            out_p, sp = even_mixer(zin[:n_p].reshape(bp, Lp, EVEN_N), *zeros_even, *small)
            out_s, ss = even_mixer(zin[n_p:].reshape(bs, Ls, EVEN_N), state_mlstm_C[j], state_mlstm_n[j],
                                   state_mlstm_m[j], state_rwkv_S[j], state_rwkv_shift[j], *small)
            st_p_even.append(sp)
            st_s_even.append(ss)
            mix = jnp.concatenate([out_p.reshape(n_p, EVEN_OUT), out_s.reshape(bs * Ls, EVEN_OUT)], axis=0)
            x = matmul_res(mix, w_out_even[j].astype(BF16), x)
        else:
            zin = norm_matmul(x, norm_mix[i], _odd_w_in(w_in_odd[j]), 768)
            small = [t[j] for t in odd_small]
            out_p, sp = odd_mixer(zin[:n_p].reshape(bp, Lp, ODD_N), *zeros_odd, *small)
            out_s, ss = odd_mixer(zin[n_p:].reshape(bs, Ls, ODD_N), state_ssm[j], state_conv[j], *small)
            st_p_odd.append(sp)
            st_s_odd.append(ss)
            mix = jnp.concatenate([out_p.reshape(n_p, C_INNER), out_s.reshape(bs * Ls, C_INNER)], axis=0)
            x = matmul_res(mix, w_out_odd[j].astype(BF16), x)
        x = ffn(x, norm_ffn[i], w_ffn_up[i].astype(BF16), w_ffn_down[i].astype(BF16))
        x = ple(x, norm_ple[i], w_ple_gate[i].astype(BF16), p_all[i], w_ple_proj[i].astype(BF16))
    y = final_norm(x, norm_final)
    y_prompt = y[:n_p].reshape(bp, Lp, D_MODEL)
    y_sample = y[n_p:].reshape(bs, Ls, D_MODEL)
    stack = lambda sts, idx: jnp.stack([s[idx] for s in sts])
    return (y_prompt, y_sample,
            stack(st_p_even, 0), stack(st_p_even, 1), stack(st_p_even, 2), stack(st_p_even, 3),
            stack(st_p_even, 4), stack(st_p_odd, 0), stack(st_p_odd, 1),
            stack(st_s_even, 0), stack(st_s_even, 1), stack(st_s_even, 2), stack(st_s_even, 3),
            stack(st_s_even, 4), stack(st_s_odd, 0), stack(st_s_odd, 1))
```

```python
import math
import functools
import jax
import jax.numpy as jnp
from jax import lax
from jax.experimental import pallas as pl
from jax.experimental.pallas import tpu as pltpu

D_MODEL = 1024
DEPTH = 4
F32 = jnp.float32
BF16 = jnp.bfloat16
EPS = 1e-6
N_EVEN = (DEPTH + 1) // 2
N_ODD = DEPTH // 2
D_FF = 4 * D_MODEL
PLE_DIM = 256

A_HEADS = 4
A_DK = D_MODEL // 8
A_DV = D_MODEL // 8
A_CHUNK = 64
A_QK = A_HEADS * A_DK
A_V = A_HEADS * A_DV
A_COLS = 2 * A_QK + 2 * A_V + 2 * A_HEADS

B_HEADS = 8
B_DH = 64
B_W = B_HEADS * B_DH
B_W_RANK = 64
B_A_RANK = 64
B_G_RANK = 128
B_COLS = 3 * B_W + B_W_RANK + B_A_RANK + B_G_RANK
B_DECAY_OFFSET = 0.5
B_GN_EPS = 64e-5

EVEN_COLS = A_COLS + B_COLS
EVEN_OUT = A_V + B_W

C_INNER = 2 * D_MODEL
C_HEADDIM = 64
C_HEADS = C_INNER // C_HEADDIM
C_GROUPS = 4
C_HPG = C_HEADS // C_GROUPS
C_STATE = 128
C_CONV = 4
C_CHUNK = 128
C_CONV_DIM = C_INNER + 2 * C_GROUPS * C_STATE
ODD_COLS = C_INNER + C_CONV_DIM + C_HEADS

LANES = 128
VMEM_LIMIT = 56 * 1024 * 1024
TOKEN_TILE = 256

EVEN_GATE_PAD = 256
EVEN_N = B_COLS + EVEN_GATE_PAD + 2 * A_QK + 2 * A_V
EVEN_G_OFF = B_COLS
EVEN_A_OFF = B_COLS + EVEN_GATE_PAD
ODD_DT_PAD = 256
ODD_N = C_INNER + C_CONV_DIM + ODD_DT_PAD
ODD_DT_OFF = C_INNER + C_CONV_DIM


def _cparams(sem):
    return pltpu.CompilerParams(dimension_semantics=sem, vmem_limit_bytes=VMEM_LIMIT)


def _rms(x, g):
    return x * lax.rsqrt(jnp.mean(x * x, axis=-1, keepdims=True) + EPS) * g


def _resident(shape):
    nd = len(shape)
    return pl.BlockSpec(shape, lambda *_: (0,) * nd, pipeline_mode=pl.Buffered(1))


def _rows(width):
    return pl.BlockSpec((TOKEN_TILE, width), lambda i: (i, 0))


def _norm_matmul_kernel(x_ref, g_ref, w_ref, o_ref, *, tn):
    xn = _rms(x_ref[...], g_ref[...]).astype(BF16)
    for n0 in range(0, w_ref.shape[1], tn):
        o_ref[:, n0:n0 + tn] = jnp.dot(xn, w_ref[:, n0:n0 + tn], preferred_element_type=F32)


def norm_matmul(x, g, w, tn):
    m, n = x.shape[0], w.shape[1]
    return pl.pallas_call(
        functools.partial(_norm_matmul_kernel, tn=tn),
        grid=(m // TOKEN_TILE,),
        in_specs=[_rows(D_MODEL), _resident((1, D_MODEL)), _resident(w.shape)],
        out_specs=_rows(n),
        out_shape=jax.ShapeDtypeStruct((m, n), F32),
        compiler_params=_cparams(("arbitrary",)),
        name="norm_matmul",
    )(x, g.reshape(1, D_MODEL), w)


def _matmul_res_kernel(*refs, gated):
    x_ref, o_ref = refs[-2], refs[-1]
    acc = x_ref[...]
    pos = 0
    for has_gate in gated:
        a = refs[pos][...]
        if has_gate:
            a = a * refs[pos + 1][...]
        w_ref = refs[pos + 1 + has_gate]
        pos += 2 + has_gate
        acc = acc + jnp.dot(a.astype(BF16), w_ref[...], preferred_element_type=F32)
    o_ref[...] = acc


def matmul_res(terms, x):
    m = x.shape[0]
    specs, args = [], []
    for a, gate, w in terms:
        specs += [_rows(a.shape[1])] + ([_rows(a.shape[1])] if gate is not None else []) + [_resident(w.shape)]
        args += [a] + ([gate] if gate is not None else []) + [w]
    gated = tuple(int(gate is not None) for _, gate, _ in terms)
    return pl.pallas_call(
        functools.partial(_matmul_res_kernel, gated=gated),
        grid=(m // TOKEN_TILE,),
        in_specs=specs + [_rows(D_MODEL)],
        out_specs=_rows(D_MODEL),
        out_shape=jax.ShapeDtypeStruct((m, D_MODEL), F32),
        compiler_params=_cparams(("arbitrary",)),
        name="matmul_res",
    )(*args, x)


FFN_CHUNK = 512


def _ffn_kernel(x_ref, g_ref, wu_ref, wd_ref, o_ref):
    x = x_ref[...]
    xn = _rms(x, g_ref[...]).astype(BF16)
    acc = x
    for c0 in range(0, D_FF, FFN_CHUNK):
        h = jnp.dot(xn, wu_ref[:, c0:c0 + FFN_CHUNK], preferred_element_type=F32)
        h = jnp.square(jnp.maximum(h, 0.0)).astype(BF16)
        acc = acc + jnp.dot(h, wd_ref[c0:c0 + FFN_CHUNK, :], preferred_element_type=F32)
    o_ref[...] = acc


def ffn(x, g, wu, wd):
    m = x.shape[0]
    return pl.pallas_call(
        _ffn_kernel,
        grid=(m // TOKEN_TILE,),
        in_specs=[_rows(D_MODEL), _resident((1, D_MODEL)), _resident(wu.shape), _resident(wd.shape)],
        out_specs=_rows(D_MODEL),
        out_shape=jax.ShapeDtypeStruct((m, D_MODEL), F32),
        compiler_params=_cparams(("arbitrary",)),
        name="ffn",
    )(x, g.reshape(1, D_MODEL), wu, wd)


def _ple_kernel(x_ref, g_ref, wg_ref, p_ref, wp_ref, o_ref):
    x = x_ref[...]
    xn = _rms(x, g_ref[...]).astype(BF16)
    gate = jax.nn.sigmoid(jnp.dot(xn, wg_ref[...], preferred_element_type=F32))
    proj = jnp.dot(p_ref[...].astype(BF16), wp_ref[...], preferred_element_type=F32)
    o_ref[...] = x + proj * gate


def ple(x, g, wg, p, wp):
    m = x.shape[0]
    return pl.pallas_call(
        _ple_kernel,
        grid=(m // TOKEN_TILE,),
        in_specs=[_rows(D_MODEL), _resident((1, D_MODEL)), _resident(wg.shape),
                  _rows(PLE_DIM), _resident(wp.shape)],
        out_specs=_rows(D_MODEL),
        out_shape=jax.ShapeDtypeStruct((m, D_MODEL), F32),
        compiler_params=_cparams(("arbitrary",)),
        name="ple",
    )(x, g.reshape(1, D_MODEL), wg, p, wp)


def _final_norm_kernel(x_ref, g_ref, o_ref):
    o_ref[...] = _rms(x_ref[...], g_ref[...])


def final_norm(x, g):
    m = x.shape[0]
    return pl.pallas_call(
        _final_norm_kernel,
        grid=(m // TOKEN_TILE,),
        in_specs=[_rows(D_MODEL), _resident((1, D_MODEL))],
        out_specs=_rows(D_MODEL),
        out_shape=jax.ShapeDtypeStruct((m, D_MODEL), F32),
        compiler_params=_cparams(("arbitrary",)),
        name="final_norm",
    )(x, g.reshape(1, D_MODEL))


def _rwkv_scan_kernel(r_ref, w_ref, k_ref, kk_ref, a_ref, v_ref, rk_ref, lnw_ref, lnb_ref, s0_ref,
                      y_ref, s_ref, *, nlt, tc, rep):
    @pl.when(pl.program_id(1) == 0)
    def _():
        s_ref[...] = s0_ref[...]

    def head_sum(x):
        tot = jnp.sum(x, axis=0, keepdims=True)
        span = LANES // rep
        while span < LANES:
            tot = tot + pltpu.roll(tot, span, axis=1)
            span *= 2
        return tot

    def step(t, carry):
        r = r_ref[t]
        w = w_ref[t]
        k = k_ref[t]
        kk = kk_ref[t]
        kk = kk * lax.rsqrt(jnp.maximum(jnp.sum(kk * kk, axis=0, keepdims=True), 1e-24))
        a = -kk
        b = kk * a_ref[t]
        bonus = jnp.sum(r * k * rk_ref[...], axis=0, keepdims=True)
        vt = v_ref[t]
        ys = []
        for lt in range(nlt):
            s = s_ref[lt]
            sa = jnp.sum(s * a, axis=0, keepdims=True)
            s = s * w + sa * b + vt[lt:lt + 1] * k
            s_ref[lt] = s
            ys.append(jnp.sum(s * r, axis=0, keepdims=True))
        y = jnp.concatenate(ys, axis=0)
        d = y - head_sum(y) * (1.0 / B_DH)
        var = head_sum(d * d) * (1.0 / B_DH)
        y_ref[t] = d * lax.rsqrt(var + B_GN_EPS) * lnw_ref[...] + lnb_ref[...] + bonus * vt
        return carry

    lax.fori_loop(0, tc, step, 0)


def rwkv_scan(cols, v, rk, lnw, lnb, s0, nlt, tc, rep):
    L = v.shape[0]
    G = s0.shape[0]
    col = pl.BlockSpec((tc, B_DH, LANES), lambda g, t: (t, 0, g))
    row = pl.BlockSpec((tc, nlt, LANES), lambda g, t: (t, 0, g))
    st = pl.BlockSpec((None, nlt, B_DH, LANES), lambda g, t: (g, 0, 0, 0))
    const = lambda shape: pl.BlockSpec(shape, lambda g, t: (0,) * len(shape))
    return pl.pallas_call(
        functools.partial(_rwkv_scan_kernel, nlt=nlt, tc=tc, rep=rep),
        grid=(G, L // tc),
        in_specs=[col, col, col, col, col, row, const((B_DH, LANES)), const((nlt, LANES)), const((nlt, LANES)), st],
        out_specs=[row, st],
        out_shape=[jax.ShapeDtypeStruct(v.shape, F32), jax.ShapeDtypeStruct(s0.shape, F32)],
        compiler_params=_cparams(("arbitrary", "arbitrary")),
        name="rwkv_scan",
    )(*cols, v, rk, lnw, lnb, s0)


def _to_cols(x, b, L):
    return x.reshape(b, L, B_HEADS, B_DH).transpose(1, 3, 0, 2).reshape(L, B_DH, b * B_HEADS)


def rwkv_recurrence(r, w, k, kk, a, v, S0, b, L, r_k, ln_w, ln_b):
    bh = b * B_HEADS
    cols = [_to_cols(t, b, L) for t in (r, w, k, kk, a)]
    vc = _to_cols(v, b, L)
    rep = max(LANES // bh, 1)
    nlt = B_DH // rep
    G = max(bh // LANES, 1)
    lane = jnp.arange(LANES)
    head = lane % B_HEADS
    val = jnp.arange(nlt)[:, None] * rep + (lane // (LANES // rep))[None, :]
    rk_t = r_k[head].T
    lnw_t = ln_w.reshape(B_HEADS, B_DH)[head[None, :], val]
    lnb_t = ln_b.reshape(B_HEADS, B_DH)[head[None, :], val]
    tc = math.gcd(L, 32)
    if rep > 1:
        cols = [jnp.concatenate([c] * rep, axis=-1) for c in cols]
        vc = vc.reshape(L, nlt, LANES)
        s0 = S0.reshape(b, B_HEADS, nlt, rep, B_DH).transpose(2, 4, 3, 0, 1).reshape(1, nlt, B_DH, LANES)
    else:
        s0 = S0.reshape(G, LANES, B_DH, B_DH).transpose(0, 2, 3, 1)
    y, s = rwkv_scan(cols, vc, rk_t, lnw_t, lnb_t, s0, nlt, tc, rep)
    y = y.reshape(L, B_DH, b, B_HEADS).transpose(2, 0, 3, 1).reshape(b * L, B_W)
    if rep > 1:
        s = s.reshape(nlt, B_DH, rep, b, B_HEADS).transpose(3, 4, 0, 2, 1)
    else:
        s = s.transpose(0, 3, 1, 2)
    return y, s.reshape(b, B_HEADS, B_DH, B_DH)


RWKV_PREP_ROWS = 256
RWKV_SHORT_ROWS = 64
B_LORA_OFF = 3 * B_W


def _rwkv_prep_kernel(z_ref, sh0_ref, mu_ref, wwa_ref, g2_ref, w0_ref, a0_ref, kk_ref, ka_ref,
                      r_ref, w_ref, k_ref, kkn_ref, a_ref, v_ref, g_ref, sh_ref, *, T, nseq):
    @pl.when(pl.program_id(1) == 0)
    def _():
        sh_ref[...] = sh0_ref[...]

    z = z_ref[...]
    rowid = lax.broadcasted_iota(jnp.int32, (z.shape[0], 1), 0)
    zprev = pltpu.roll(z, 1, axis=0)
    for u in range(nseq):
        zprev = jnp.where(rowid == u * T, sh_ref[u], zprev)
    for u in range(nseq):
        sh_ref[u] = z[(u + 1) * T - 1:(u + 1) * T, :]
    zs = z + (zprev - z) * mu_ref[...]
    r = zs[:, :B_W]
    k = zs[:, B_W:2 * B_W]
    lora = zs[:, B_LORA_OFF:B_LORA_OFF + LANES]
    lane = lax.broadcasted_iota(jnp.int32, lora.shape, 1)
    lora = jnp.where(lane < B_W_RANK, jnp.tanh(lora), lora).astype(BF16)
    wa = jnp.dot(lora, wwa_ref[...], preferred_element_type=F32)
    w_log = -jax.nn.softplus(-(w0_ref[...] + wa[:, :B_W])) - B_DECAY_OFFSET
    a = jax.nn.sigmoid(a0_ref[...] + wa[:, B_W:])
    zg = zs[:, B_LORA_OFF + LANES:B_LORA_OFF + LANES + B_G_RANK]
    r_ref[...] = r
    w_ref[...] = jnp.exp(-jnp.exp(w_log))
    k_ref[...] = k * (1.0 + (a - 1.0) * ka_ref[...])
    kkn_ref[...] = k * kk_ref[...]
    a_ref[...] = a
    v_ref[...] = zs[:, 2 * B_W:3 * B_W]
    g_ref[...] = jnp.dot(jax.nn.sigmoid(zg).astype(BF16), g2_ref[...], preferred_element_type=F32)


def rwkv_prep(zin, b, L, shift0, mu, w0, w2, a0, a2, g2, k_k, k_a):
    assert B_W_RANK + B_A_RANK == LANES
    T = math.gcd(L, RWKV_PREP_ROWS)
    nseq = 1 if T == RWKV_PREP_ROWS else RWKV_SHORT_ROWS // T
    assert nseq == 1 or (L == T and b % nseq == 0)
    R = T * nseq
    nchunk = L // T
    wwa = jnp.zeros((LANES, 2 * B_W), F32).at[:B_W_RANK, :B_W].set(w2).at[B_W_RANK:, B_W:].set(a2).astype(BF16)
    row = lambda v: v.reshape(1, -1)
    blk = pl.BlockSpec((R, B_W), lambda i, c: (i * nchunk + c, 0))
    shspec = pl.BlockSpec((nseq, 1, B_COLS), lambda i, c: (i, 0, 0))
    const = lambda shape: pl.BlockSpec(shape, lambda i, c: (0,) * len(shape))
    outs = pl.pallas_call(
        functools.partial(_rwkv_prep_kernel, T=T, nseq=nseq),
        grid=(b // nseq, nchunk),
        in_specs=[pl.BlockSpec((R, B_COLS), lambda i, c: (i * nchunk + c, 0)), shspec, const((1, B_COLS)),
                  const((LANES, 2 * B_W)), const((B_G_RANK, B_W)),
                  const((1, B_W)), const((1, B_W)), const((1, B_W)), const((1, B_W))],
        out_specs=[blk] * 7 + [shspec],
        out_shape=[jax.ShapeDtypeStruct((b * L, B_W), F32)] * 7 + [jax.ShapeDtypeStruct((b, 1, B_COLS), F32)],
        compiler_params=_cparams(("arbitrary", "arbitrary")),
        name="rwkv_prep",
    )(zin, shift0.reshape(b, 1, B_COLS), row(mu), wwa, g2.astype(BF16), row(w0), row(a0), row(k_k), row(k_a))
    return outs[:7], outs[7].reshape(b, B_COLS)


MLSTM_ROWS = 64
HIGHEST = lax.Precision.HIGHEST


def _mlstm_kernel(q_ref, k_ref, v_ref, o_ref, g_ref, gt_ref, brow_ref, bcol_ref, nw_ref,
                  c0_ref, n0_ref, m0_ref, h_ref, c_ref, n_ref, m_ref, *, T, nseq):
    R = MLSTM_ROWS

    @pl.when(pl.program_id(1) == 0)
    def _():
        c_ref[...] = c0_ref[...]
        n_ref[...] = n0_ref[...]
        m_ref[...] = m0_ref[...]

    shift = T.bit_length() - 1
    ri = lax.broadcasted_iota(jnp.int32, (R, R), 0)
    ci = lax.broadcasted_iota(jnp.int32, (R, R), 1)
    mask = (ci <= ri) & (jnp.right_shift(ri, shift) == jnp.right_shift(ci, shift))
    lmat = mask.astype(F32)
    rowid = lax.broadcasted_iota(jnp.int32, (R, 1), 0)
    rsel = [(rowid >= u * T) & (rowid < (u + 1) * T) for u in range(nseq)]

    g = g_ref[...] + brow_ref[...]
    lane = lax.broadcasted_iota(jnp.int32, g.shape, 1)
    glog = jnp.where((lane >= A_HEADS) & (lane < 2 * A_HEADS), jax.nn.log_sigmoid(g), g)
    gt = gt_ref[...] + bcol_ref[...]
    sub = lax.broadcasted_iota(jnp.int32, gt.shape, 0)
    gtlog = jnp.where(sub >= A_HEADS, jax.nn.log_sigmoid(gt), gt)
    bc_col = jnp.dot(lmat, glog, precision=HIGHEST, preferred_element_type=F32)
    bc_row = lax.dot_general(gtlog, lmat, (((1,), (1,)), ((), ())), precision=HIGHEST,
                             preferred_element_type=F32)
    lane_m = lax.broadcasted_iota(jnp.int32, (1, LANES), 1)
    m_old = [m_ref[u] for u in range(nseq)]
    m_out = [jnp.zeros((1, LANES), F32) for _ in range(nseq)]

    for h in range(A_HEADS):
        hs = slice(h * A_DK, (h + 1) * A_DK)
        bcc = bc_col[:, A_HEADS + h:A_HEADS + h + 1]
        bcr = bc_row[A_HEADS + h:A_HEADS + h + 1, :]
        lir = gtlog[h:h + 1, :]
        lic = glog[:, h:h + 1]
        m_u = [m_old[u][:, h:h + 1] for u in range(nseq)]
        m_col = m_u[0]
        for u in range(1, nseq):
            m_col = jnp.where(rsel[u], m_u[u], m_col)
        dmat = jnp.where(mask, bcc - bcr + lir, -jnp.inf)
        inter = bcc + m_col
        mt = jnp.maximum(inter, jnp.max(dmat, axis=1, keepdims=True))
        p = jnp.exp(dmat - mt)
        qh = q_ref[:, hs] * (A_DK ** -0.5)
        kh = k_ref[:, hs]
        qb, kb, vb = qh.astype(BF16), kh.astype(BF16), v_ref[:, hs].astype(BF16)
        wq = lax.dot_general(qb, kb, (((1,), (1,)), ((), ())), preferred_element_type=F32) * p
        wi = jnp.exp(inter - mt)
        c_old = [c_ref[u, h] for u in range(nseq)]
        n_old = [n_ref[u, h:h + 1, :] for u in range(nseq)]
        qc = jnp.dot(qb, c_old[0].astype(BF16), preferred_element_type=F32)
        qn = jnp.sum(qh * n_old[0], axis=1, keepdims=True)
        for u in range(1, nseq):
            qc = jnp.where(rsel[u], jnp.dot(qb, c_old[u].astype(BF16), preferred_element_type=F32), qc)
            qn = jnp.where(rsel[u], jnp.sum(qh * n_old[u], axis=1, keepdims=True), qn)
        num = jnp.dot(wq.astype(BF16), vb, preferred_element_type=F32) + wi * qc
        den = jnp.sum(wq, axis=1, keepdims=True) + wi * qn
        hh = num / jnp.maximum(jnp.abs(den), jnp.exp(-mt))
        hh = hh * lax.rsqrt(jnp.mean(hh * hh, axis=-1, keepdims=True) + EPS)
        h_ref[:, hs] = hh * nw_ref[:, hs] * jax.nn.sigmoid(o_ref[:, hs])
        for u in range(nseq):
            b_last = bcc[(u + 1) * T - 1:(u + 1) * T, :]
            gs = b_last - bcc + lic
            gmax = jnp.max(gs if nseq == 1 else jnp.where(rsel[u], gs, -jnp.inf), axis=0, keepdims=True)
            m_new = jnp.maximum(b_last + m_u[u], gmax)
            decay = jnp.exp(b_last + m_u[u] - m_new)
            ws = jnp.exp(gs - m_new)
            if nseq > 1:
                ws = jnp.where(rsel[u], ws, 0.0)
            kw = kh * ws
            c_ref[u, h] = decay * c_old[u] + lax.dot_general(
                kw.astype(BF16), vb, (((0,), (0,)), ((), ())), preferred_element_type=F32)
            n_ref[u, h:h + 1, :] = decay * n_old[u] + jnp.sum(kw, axis=0, keepdims=True)
            m_out[u] = jnp.where(lane_m == h, m_new, m_out[u])
    for u in range(nseq):
        m_ref[u] = m_out[u]


def mlstm(zin, b, L, b_i, b_f, m_norm, C0, n0, m0):
    R = MLSTM_ROWS
    T = math.gcd(L, R)
    nseq = R // T
    assert nseq == 1 or (L == T and b % nseq == 0)
    nchunk = L // T
    nblk = b * L // R
    gates = zin[:, EVEN_G_OFF:EVEN_G_OFF + 2 * A_HEADS]
    gates_t = gates.reshape(nblk, R, 2 * A_HEADS).transpose(0, 2, 1)
    bias = jnp.concatenate([b_i, b_f])
    bias_row = jnp.zeros((1, LANES), F32).at[0, :2 * A_HEADS].set(bias)
    bias_col = bias.reshape(2 * A_HEADS, 1)
    m0p = jnp.zeros((b, 1, LANES), F32).at[:, 0, :A_HEADS].set(m0)
    rowblk = lambda col: pl.BlockSpec((R, A_QK), lambda i, c: (i * nchunk + c, EVEN_A_OFF // A_QK + col))
    cspec = pl.BlockSpec((nseq, A_HEADS, A_DK, A_DV), lambda i, c: (i, 0, 0, 0))
    nspec = pl.BlockSpec((nseq, A_HEADS, A_DK), lambda i, c: (i, 0, 0))
    mspec = pl.BlockSpec((nseq, 1, LANES), lambda i, c: (i, 0, 0))
    const = lambda shape: pl.BlockSpec(shape, lambda i, c: (0,) * len(shape))
    h, C, n, m = pl.pallas_call(
        functools.partial(_mlstm_kernel, T=T, nseq=nseq),
        grid=(b // nseq, nchunk),
        in_specs=[rowblk(0), rowblk(1), rowblk(2), rowblk(3),
                  pl.BlockSpec((R, LANES), lambda i, c: (i * nchunk + c, EVEN_G_OFF // LANES)),
                  pl.BlockSpec((None, 2 * A_HEADS, R), lambda i, c: (i * nchunk + c, 0, 0)),
                  const((1, LANES)), const((2 * A_HEADS, 1)), const((1, A_V)),
                  cspec, nspec, mspec],
        out_specs=[pl.BlockSpec((R, A_V), lambda i, c: (i * nchunk + c, 0)), cspec, nspec, mspec],
        out_shape=[jax.ShapeDtypeStruct((b * L, A_V), F32),
                   jax.ShapeDtypeStruct(C0.shape, F32), jax.ShapeDtypeStruct(n0.shape, F32),
                   jax.ShapeDtypeStruct(m0p.shape, F32)],
        compiler_params=_cparams(("arbitrary", "arbitrary")),
        name="mlstm",
    )(zin, zin, zin, zin, zin, gates_t, bias_row, bias_col, m_norm.reshape(1, A_V), C0, n0, m0p)
    return h, C, n, m[:, 0, :A_HEADS]


def even_mixer(zin, b, L, C0, n0, m0, S0, shift0, b_i, b_f, m_norm, mu, w0, w2, a0, a2, g2,
               k_k, k_a, r_k, ln_w, ln_b):
    hA, C, n, m = mlstm(zin, b, L, b_i, b_f, m_norm, C0, n0, m0)
    (r, w, k, kk, a, v, g), shift = rwkv_prep(zin, b, L, shift0, mu, w0, w2, a0, a2, g2, k_k, k_a)
    hB, S = rwkv_recurrence(r, w, k, kk, a, v, S0, b, L, r_k, ln_w, ln_b)
    return hA, hB, g, (C, n, m, S, shift)


SUBLANES = 8
C_PAIRS = C_HEADS // 2
C_GROUP_W = C_INNER // C_GROUPS
C_BC_W = 2 * C_GROUPS * C_STATE
SSD_SHORT_ROWS = 32


def _ssd_kernel(z_ref, x_ref, bc_ref, dt_ref, dtt_ref, cw_ref, cb_ref, dtb_ref, dtbt_ref, al_ref, alt_ref,
                dsk_ref, nw_ref, s0_ref, cv0_ref, y_ref, s_ref, cv_ref, *, T, nseq):
    R = T * nseq

    @pl.when(pl.program_id(1) == 0)
    def _():
        s_ref[...] = s0_ref[...]
        cv_ref[...] = cv0_ref[...]

    shift = T.bit_length() - 1
    ri = lax.broadcasted_iota(jnp.int32, (R, R), 0)
    ci = lax.broadcasted_iota(jnp.int32, (R, R), 1)
    mask = (ci <= ri) & (jnp.right_shift(ri, shift) == jnp.right_shift(ci, shift))
    lmat = mask.astype(F32)
    rowid = lax.broadcasted_iota(jnp.int32, (R, 1), 0)
    rsel = [(rowid >= u * T) & (rowid < (u + 1) * T) for u in range(nseq)]

    def conv_silu(x, cols):
        acc = cb_ref[:, cols] + x * cw_ref[C_CONV - 1:C_CONV, cols]
        for d in range(1, C_CONV):
            xd = pltpu.roll(x, d, axis=0)
            for u in range(nseq):
                for t in range(d):
                    src = SUBLANES - d + t
                    xd = jnp.where(rowid == u * T + t, cv_ref[u, src:src + 1, cols], xd)
            acc = acc + xd * cw_ref[C_CONV - 1 - d:C_CONV - d, cols]
        for u in range(nseq):
            cv_ref[u, :, cols] = x[(u + 1) * T - SUBLANES:(u + 1) * T, :]
        return acc * jax.nn.sigmoid(acc)

    xc = conv_silu(x_ref[...], slice(0, C_INNER))
    bcc_ = conv_silu(bc_ref[...], slice(C_INNER, C_CONV_DIM))

    dtv = jax.nn.softplus(dt_ref[...] + dtb_ref[...])
    dtt = jax.nn.softplus(dtt_ref[...] + dtbt_ref[...])
    cum_col = jnp.dot(lmat, dtv * (-jnp.exp(al_ref[...])), precision=HIGHEST, preferred_element_type=F32)
    cum_row = lax.dot_general(dtt * (-jnp.exp(alt_ref[...])), lmat, (((1,), (1,)), ((), ())),
                              precision=HIGHEST, preferred_element_type=F32)

    lo = lax.broadcasted_iota(jnp.int32, (R, LANES), 1) < C_HEADDIM
    rlo = lax.broadcasted_iota(jnp.int32, (LANES, 1), 0) < C_HEADDIM
    nt = (((1,), (1,)), ((), ()))
    tn = (((0,), (0,)), ((), ()))
    pairs_per_group = C_PAIRS // C_GROUPS
    for g in range(C_GROUPS):
        bg = bcc_[:, g * C_STATE:(g + 1) * C_STATE].astype(BF16)
        cg = bcc_[:, (C_GROUPS + g) * C_STATE:(C_GROUPS + g + 1) * C_STATE].astype(BF16)
        cbm = lax.dot_general(cg, bg, nt, preferred_element_type=F32)
        ys = None
        for u in range(nseq):
            sg = s_ref[u, g * pairs_per_group:(g + 1) * pairs_per_group].reshape(C_GROUP_W, C_STATE)
            t_u = lax.dot_general(cg, sg.astype(BF16), nt, preferred_element_type=F32)
            ys = t_u if u == 0 else jnp.where(rsel[u], t_u, ys)
        for q in range(pairs_per_group):
            pr = g * pairs_per_group + q
            ps = slice(pr * LANES, (pr + 1) * LANES)
            xp = xc[:, ps]
            cc = [cum_col[:, 2 * pr + e:2 * pr + e + 1] for e in range(2)]
            intra = None
            for e, keep in ((0, lo), (1, jnp.logical_not(lo))):
                hh = 2 * pr + e
                seg = jnp.exp(jnp.where(mask, cc[e] - cum_row[hh:hh + 1, :], -jnp.inf))
                mix = cbm * seg * dtt[hh:hh + 1, :]
                part = jnp.dot(mix.astype(BF16), jnp.where(keep, xp, 0.0).astype(BF16),
                               preferred_element_type=F32)
                intra = part if intra is None else intra + part
            scale = jnp.where(lo, jnp.exp(cc[0]), jnp.exp(cc[1]))
            yp = intra + scale * ys[:, q * LANES:(q + 1) * LANES] + dsk_ref[:, ps] * xp
            zp = z_ref[:, ps]
            y_ref[:, ps] = yp * (zp * jax.nn.sigmoid(zp))
            for u in range(nseq):
                last = (u + 1) * T - 1
                ct = [cc[e][last:last + 1, :] for e in range(2)]
                tail = jnp.where(lo, jnp.exp(ct[0] - cc[0]) * dtv[:, 2 * pr:2 * pr + 1],
                                 jnp.exp(ct[1] - cc[1]) * dtv[:, 2 * pr + 1:2 * pr + 2])
                xw = xp * tail
                if nseq > 1:
                    xw = jnp.where(rsel[u], xw, 0.0)
                upd = lax.dot_general(xw.astype(BF16), bg, tn, preferred_element_type=F32)
                dec = jnp.where(rlo, jnp.exp(ct[0]), jnp.exp(ct[1]))
                s_ref[u, pr] = dec * s_ref[u, pr] + upd

    for g in range(C_GROUPS):
        gs_ = slice(g * C_GROUP_W, (g + 1) * C_GROUP_W)
        yg = y_ref[:, gs_]
        y_ref[:, gs_] = yg * lax.rsqrt(jnp.mean(yg * yg, axis=-1, keepdims=True) + EPS) * nw_ref[:, gs_]


def ssd_mixer(zin, b, L, ssm0, conv0, conv_w, conv_b, dt_bias, a_log, d_skip, norm_w):
    T = math.gcd(L, C_CHUNK)
    nseq = 1 if T == C_CHUNK else SSD_SHORT_ROWS // T
    assert nseq == 1 or (L == T and b % nseq == 0)
    R = T * nseq
    nchunk = L // T
    nblk = b * L // R
    dt_t = zin[:, ODD_DT_OFF:ODD_DT_OFF + C_HEADS].reshape(nblk, R, C_HEADS).transpose(0, 2, 1)
    pad_row = lambda v: jnp.zeros((1, LANES), F32).at[0, :C_HEADS].set(v)
    s0 = ssm0.reshape(b, C_PAIRS, 2 * C_HEADDIM, C_STATE)
    cv0 = jnp.concatenate([jnp.zeros((b, SUBLANES - (C_CONV - 1), C_CONV_DIM), F32), conv0], axis=1)
    blk = lambda i, c: i * nchunk + c
    sspec = pl.BlockSpec((nseq, C_PAIRS, 2 * C_HEADDIM, C_STATE), lambda i, c: (i, 0, 0, 0))
    cvspec = pl.BlockSpec((nseq, SUBLANES, C_CONV_DIM), lambda i, c: (i, 0, 0))
    const = lambda shape: pl.BlockSpec(shape, lambda i, c: (0,) * len(shape))
    y, s, cv = pl.pallas_call(
        functools.partial(_ssd_kernel, T=T, nseq=nseq),
        grid=(b // nseq, nchunk),
        in_specs=[pl.BlockSpec((R, C_INNER), lambda i, c: (blk(i, c), 0)),
                  pl.BlockSpec((R, C_INNER), lambda i, c: (blk(i, c), 1)),
                  pl.BlockSpec((R, C_BC_W), lambda i, c: (blk(i, c), 2 * C_INNER // C_BC_W)),
                  pl.BlockSpec((R, LANES), lambda i, c: (blk(i, c), ODD_DT_OFF // LANES)),
                  pl.BlockSpec((None, C_HEADS, R), lambda i, c: (blk(i, c), 0, 0)),
                  const((C_CONV, C_CONV_DIM)), const((1, C_CONV_DIM)),
                  const((1, LANES)), const((C_HEADS, 1)), const((1, LANES)), const((C_HEADS, 1)),
                  const((1, C_INNER)), const((1, C_INNER)), sspec, cvspec],
        out_specs=[pl.BlockSpec((R, C_INNER), lambda i, c: (blk(i, c), 0)), sspec, cvspec],
        out_shape=[jax.ShapeDtypeStruct((b * L, C_INNER), F32),
                   jax.ShapeDtypeStruct(s0.shape, F32), jax.ShapeDtypeStruct(cv0.shape, F32)],
        compiler_params=_cparams(("arbitrary", "arbitrary")),
        name="ssd",
    )(zin, zin, zin, zin, dt_t, conv_w, conv_b.reshape(1, C_CONV_DIM),
      pad_row(dt_bias), dt_bias.reshape(C_HEADS, 1), pad_row(a_log), a_log.reshape(C_HEADS, 1),
      jnp.repeat(d_skip, C_HEADDIM).reshape(1, C_INNER), norm_w.reshape(1, C_INNER), s0, cv0)
    return y, (s.reshape(b, C_HEADS, C_HEADDIM, C_STATE), cv[:, SUBLANES - (C_CONV - 1):])


def _even_w_in(w):
    qkvo = w[:, :2 * A_QK + 2 * A_V]
    gates = w[:, 2 * A_QK + 2 * A_V:A_COLS]
    rwkv = w[:, A_COLS:]
    pad = jnp.zeros((D_MODEL, EVEN_GATE_PAD - 2 * A_HEADS), w.dtype)
    return jnp.concatenate([rwkv, gates, pad, qkvo], axis=1).astype(BF16)


def _odd_w_in(w):
    pad = jnp.zeros((D_MODEL, ODD_DT_PAD - C_HEADS), w.dtype)
    return jnp.concatenate([w, pad], axis=1).astype(BF16)


def kernel(x_prompt, x_sample, state_mlstm_C, state_mlstm_n, state_mlstm_m, state_rwkv_S,
           state_rwkv_shift, state_ssm, state_conv, p_prompt, p_sample,
           norm_mix, norm_ffn, w_ffn_up, w_ffn_down, w_ple_proj, norm_ple, w_ple_gate, norm_final,
           w_in_even, mlstm_b_i, mlstm_b_f, mlstm_norm, rwkv_mu, rwkv_w0, rwkv_w2, rwkv_a0, rwkv_a2,
           rwkv_g2, rwkv_k_k, rwkv_k_a, rwkv_r_k, rwkv_ln_w, rwkv_ln_b, w_out_even,
           w_in_odd, conv_w, conv_b, dt_bias, a_log, d_skip, ssm_norm, w_out_odd):
    bp, Lp, _ = x_prompt.shape
    bs, Ls, _ = x_sample.shape
    n_p = bp * Lp
    x = jnp.concatenate([x_prompt.reshape(n_p, D_MODEL), x_sample.reshape(bs * Ls, D_MODEL)], axis=0)
    p_all = jnp.concatenate([p_prompt.reshape(DEPTH, n_p, PLE_DIM),
                             p_sample.reshape(DEPTH, bs * Ls, PLE_DIM)], axis=1)

    even_small = (mlstm_b_i, mlstm_b_f, mlstm_norm, rwkv_mu, rwkv_w0, rwkv_w2, rwkv_a0, rwkv_a2,
                  rwkv_g2, rwkv_k_k, rwkv_k_a, rwkv_r_k, rwkv_ln_w, rwkv_ln_b)
    odd_small = (conv_w, conv_b, dt_bias, a_log, d_skip, ssm_norm)

    zeros_even = (jnp.zeros((bp, A_HEADS, A_DK, A_DV), F32), jnp.zeros((bp, A_HEADS, A_DK), F32),
                  jnp.zeros((bp, A_HEADS), F32), jnp.zeros((bp, B_HEADS, B_DH, B_DH), F32),
                  jnp.zeros((bp, B_COLS), F32))
    zeros_odd = (jnp.zeros((bp, C_HEADS, C_HEADDIM, C_STATE), F32),
                 jnp.zeros((bp, C_CONV - 1, C_CONV_DIM), F32))

    st_p_even, st_s_even, st_p_odd, st_s_odd = [], [], [], []
    for i in range(DEPTH):
        j = i // 2
        if i % 2 == 0:
            zin = norm_matmul(x, norm_mix[i], _even_w_in(w_in_even[j]), 512)
            small = [t[j] for t in even_small]
            ha_p, hb_p, g_p, sp = even_mixer(zin[:n_p], bp, Lp, *zeros_even, *small)
            ha_s, hb_s, g_s, ss = even_mixer(zin[n_p:], bs, Ls, state_mlstm_C[j], state_mlstm_n[j],
                                             state_mlstm_m[j], state_rwkv_S[j], state_rwkv_shift[j], *small)
            st_p_even.append(sp)
            st_s_even.append(ss)
            wo = w_out_even[j].astype(BF16)
            cat = lambda p, s: jnp.concatenate([p, s], axis=0)
            x = matmul_res([(cat(ha_p, ha_s), None, wo[:A_V]), (cat(hb_p, hb_s), cat(g_p, g_s), wo[A_V:])], x)
        else:
            zin = norm_matmul(x, norm_mix[i], _odd_w_in(w_in_odd[j]), 768)
            small = [t[j] for t in odd_small]
            out_p, sp = ssd_mixer(zin[:n_p], bp, Lp, *zeros_odd, *small)
            out_s, ss = ssd_mixer(zin[n_p:], bs, Ls, state_ssm[j], state_conv[j], *small)
            st_p_odd.append(sp)
            st_s_odd.append(ss)
            mix = jnp.concatenate([out_p, out_s], axis=0)
            x = matmul_res([(mix, None, w_out_odd[j].astype(BF16))], x)
        x = ffn(x, norm_ffn[i], w_ffn_up[i].astype(BF16), w_ffn_down[i].astype(BF16))
        x = ple(x, norm_ple[i], w_ple_gate[i].astype(BF16), p_all[i], w_ple_proj[i].astype(BF16))
    y = final_norm(x, norm_final)
    y_prompt = y[:n_p].reshape(bp, Lp, D_MODEL)
    y_sample = y[n_p:].reshape(bs, Ls, D_MODEL)
    stack = lambda sts, idx: jnp.stack([s[idx] for s in sts])
    return (y_prompt, y_sample,
            stack(st_p_even, 0), stack(st_p_even, 1), stack(st_p_even, 2), stack(st_p_even, 3),
            stack(st_p_even, 4), stack(st_p_odd, 0), stack(st_p_odd, 1),
            stack(st_s_even, 0), stack(st_s_even, 1), stack(st_s_even, 2), stack(st_s_even, 3),
            stack(st_s_even, 4), stack(st_s_odd, 0), stack(st_s_odd, 1))
```

```python
import math
import functools
import jax
import jax.numpy as jnp
from jax import lax
from jax.experimental import pallas as pl
from jax.experimental.pallas import tpu as pltpu

D_MODEL = 1024
DEPTH = 4
F32 = jnp.float32
BF16 = jnp.bfloat16
EPS = 1e-6
N_EVEN = (DEPTH + 1) // 2
N_ODD = DEPTH // 2
D_FF = 4 * D_MODEL
PLE_DIM = 256

A_HEADS = 4
A_DK = D_MODEL // 8
A_DV = D_MODEL // 8
A_CHUNK = 64
A_QK = A_HEADS * A_DK
A_V = A_HEADS * A_DV
A_COLS = 2 * A_QK + 2 * A_V + 2 * A_HEADS

B_HEADS = 8
B_DH = 64
B_W = B_HEADS * B_DH
B_W_RANK = 64
B_A_RANK = 64
B_G_RANK = 128
B_COLS = 3 * B_W + B_W_RANK + B_A_RANK + B_G_RANK
B_DECAY_OFFSET = 0.5
B_GN_EPS = 64e-5

EVEN_COLS = A_COLS + B_COLS
EVEN_OUT = A_V + B_W

C_INNER = 2 * D_MODEL
C_HEADDIM = 64
C_HEADS = C_INNER // C_HEADDIM
C_GROUPS = 4
C_HPG = C_HEADS // C_GROUPS
C_STATE = 128
C_CONV = 4
C_CHUNK = 128
C_CONV_DIM = C_INNER + 2 * C_GROUPS * C_STATE
ODD_COLS = C_INNER + C_CONV_DIM + C_HEADS

LANES = 128
VMEM_LIMIT = 56 * 1024 * 1024
TOKEN_TILE = 256

EVEN_GATE_PAD = 256
EVEN_N = B_COLS + EVEN_GATE_PAD + 2 * A_QK + 2 * A_V
EVEN_G_OFF = B_COLS
EVEN_A_OFF = B_COLS + EVEN_GATE_PAD
ODD_DT_PAD = 256
ODD_N = C_INNER + C_CONV_DIM + ODD_DT_PAD
ODD_DT_OFF = C_INNER + C_CONV_DIM


def _cparams(sem):
    return pltpu.CompilerParams(dimension_semantics=sem, vmem_limit_bytes=VMEM_LIMIT)


def _rms(x, g):
    return x * lax.rsqrt(jnp.mean(x * x, axis=-1, keepdims=True) + EPS) * g


def _resident(shape):
    nd = len(shape)
    return pl.BlockSpec(shape, lambda *_: (0,) * nd, pipeline_mode=pl.Buffered(1))


def _rows(width):
    return pl.BlockSpec((TOKEN_TILE, width), lambda i: (i, 0))


def _fill_into(body, n_in, into, out_idx):
    if into is None:
        return body, [], [], {}

    def skipping(*refs):
        return body(*refs[:n_in], *refs[n_in + 1:])

    return skipping, [pl.BlockSpec(memory_space=pl.ANY)], [into], {n_in: out_idx}


def _norm_matmul_kernel(x_ref, g_ref, w_ref, o_ref, *, tn):
    xn = _rms(x_ref[...], g_ref[...]).astype(BF16)
    for n0 in range(0, w_ref.shape[1], tn):
        o_ref[:, n0:n0 + tn] = jnp.dot(xn, w_ref[:, n0:n0 + tn], preferred_element_type=F32)


def norm_matmul(x, g, w, tn):
    m, n = x.shape[0], w.shape[1]
    return pl.pallas_call(
        functools.partial(_norm_matmul_kernel, tn=tn),
        grid=(m // TOKEN_TILE,),
        in_specs=[_rows(D_MODEL), _resident((1, D_MODEL)), _resident(w.shape)],
        out_specs=_rows(n),
        out_shape=jax.ShapeDtypeStruct((m, n), F32),
        compiler_params=_cparams(("arbitrary",)),
        name="norm_matmul",
    )(x, g.reshape(1, D_MODEL), w)


def _matmul_res_kernel(*refs, gated):
    x_ref, o_ref = refs[-2], refs[-1]
    acc = x_ref[...]
    pos = 0
    for has_gate in gated:
        a = refs[pos][...]
        if has_gate:
            a = a * refs[pos + 1][...]
        w_ref = refs[pos + 1 + has_gate]
        pos += 2 + has_gate
        acc = acc + jnp.dot(a.astype(BF16), w_ref[...], preferred_element_type=F32)
    o_ref[...] = acc


def matmul_res(terms, x):
    m = x.shape[0]
    specs, args = [], []
    for a, gate, w in terms:
        specs += [_rows(a.shape[1])] + ([_rows(a.shape[1])] if gate is not None else []) + [_resident(w.shape)]
        args += [a] + ([gate] if gate is not None else []) + [w]
    gated = tuple(int(gate is not None) for _, gate, _ in terms)
    return pl.pallas_call(
        functools.partial(_matmul_res_kernel, gated=gated),
        grid=(m // TOKEN_TILE,),
        in_specs=specs + [_rows(D_MODEL)],
        out_specs=_rows(D_MODEL),
        out_shape=jax.ShapeDtypeStruct((m, D_MODEL), F32),
        compiler_params=_cparams(("arbitrary",)),
        name="matmul_res",
    )(*args, x)


FFN_CHUNK = 512


def _ffn_kernel(x_ref, g_ref, wu_ref, wd_ref, o_ref):
    x = x_ref[...]
    xn = _rms(x, g_ref[...]).astype(BF16)
    acc = x
    for c0 in range(0, D_FF, FFN_CHUNK):
        h = jnp.dot(xn, wu_ref[:, c0:c0 + FFN_CHUNK], preferred_element_type=F32)
        h = jnp.square(jnp.maximum(h, 0.0)).astype(BF16)
        acc = acc + jnp.dot(h, wd_ref[c0:c0 + FFN_CHUNK, :], preferred_element_type=F32)
    o_ref[...] = acc


def ffn(x, g, wu, wd):
    m = x.shape[0]
    return pl.pallas_call(
        _ffn_kernel,
        grid=(m // TOKEN_TILE,),
        in_specs=[_rows(D_MODEL), _resident((1, D_MODEL)), _resident(wu.shape), _resident(wd.shape)],
        out_specs=_rows(D_MODEL),
        out_shape=jax.ShapeDtypeStruct((m, D_MODEL), F32),
        compiler_params=_cparams(("arbitrary",)),
        name="ffn",
    )(x, g.reshape(1, D_MODEL), wu, wd)


def _ple_kernel(x_ref, g_ref, wg_ref, p_ref, wp_ref, o_ref):
    x = x_ref[...]
    xn = _rms(x, g_ref[...]).astype(BF16)
    gate = jax.nn.sigmoid(jnp.dot(xn, wg_ref[...], preferred_element_type=F32))
    proj = jnp.dot(p_ref[...].astype(BF16), wp_ref[...], preferred_element_type=F32)
    o_ref[...] = x + proj * gate


def ple(x, g, wg, p, wp):
    m = x.shape[0]
    return pl.pallas_call(
        _ple_kernel,
        grid=(m // TOKEN_TILE,),
        in_specs=[_rows(D_MODEL), _resident((1, D_MODEL)), _resident(wg.shape),
                  _rows(PLE_DIM), _resident(wp.shape)],
        out_specs=_rows(D_MODEL),
        out_shape=jax.ShapeDtypeStruct((m, D_MODEL), F32),
        compiler_params=_cparams(("arbitrary",)),
        name="ple",
    )(x, g.reshape(1, D_MODEL), wg, p, wp)


def _final_norm_kernel(x_ref, g_ref, o_ref):
    o_ref[...] = _rms(x_ref[...], g_ref[...])


def final_norm(x, g):
    m = x.shape[0]
    return pl.pallas_call(
        _final_norm_kernel,
        grid=(m // TOKEN_TILE,),
        in_specs=[_rows(D_MODEL), _resident((1, D_MODEL))],
        out_specs=_rows(D_MODEL),
        out_shape=jax.ShapeDtypeStruct((m, D_MODEL), F32),
        compiler_params=_cparams(("arbitrary",)),
        name="final_norm",
    )(x, g.reshape(1, D_MODEL))


def _rwkv_scan_kernel(r_ref, w_ref, k_ref, kk_ref, a_ref, v_ref, rk_ref, lnw_ref, lnb_ref, s0_ref,
                      y_ref, s_ref, av_ref, bv_ref, *, nlt, tc, rep):
    @pl.when(pl.program_id(1) == 0)
    def _():
        s_ref[...] = s0_ref[...]

    kk = kk_ref[...]
    kk = kk * lax.rsqrt(jnp.maximum(jnp.sum(kk * kk, axis=1, keepdims=True), 1e-24))
    av_ref[...] = -kk
    bv_ref[...] = kk * a_ref[...]

    def step(t, carry):
        a = av_ref[t]
        b = bv_ref[t]
        w = w_ref[t]
        k = k_ref[t]
        r = r_ref[t]
        vt = v_ref[t]
        ys = []
        for lt in range(nlt):
            s = s_ref[lt]
            sa = jnp.sum(s * a, axis=0, keepdims=True)
            s = s * w + sa * b + vt[lt:lt + 1] * k
            s_ref[lt] = s
            ys.append(jnp.sum(s * r, axis=0, keepdims=True))
        y_ref[t] = jnp.concatenate(ys, axis=0)
        return carry

    lax.fori_loop(0, tc, step, 0)

    def head_sum(x):
        tot = jnp.sum(x, axis=1)
        span = LANES // rep
        while span < LANES:
            tot = tot + pltpu.roll(tot, span, axis=1)
            span *= 2
        return tot

    y = y_ref[...]
    d = y - (head_sum(y) * (1.0 / B_DH))[:, None, :]
    var = head_sum(d * d) * (1.0 / B_DH)
    bonus = jnp.sum(r_ref[...] * k_ref[...] * rk_ref[...], axis=1, keepdims=True)
    y_ref[...] = (d * lax.rsqrt(var + B_GN_EPS)[:, None, :] * lnw_ref[...] + lnb_ref[...]
                  + bonus * v_ref[...])


def rwkv_scan(cols, v, rk, lnw, lnb, s0, nlt, tc, rep):
    L = v.shape[0]
    G = s0.shape[0]
    col = pl.BlockSpec((tc, B_DH, LANES), lambda g, t: (t, 0, g))
    row = pl.BlockSpec((tc, nlt, LANES), lambda g, t: (t, 0, g))
    st = pl.BlockSpec((None, nlt, B_DH, LANES), lambda g, t: (g, 0, 0, 0))
    const = lambda shape: pl.BlockSpec(shape, lambda g, t: (0,) * len(shape))
    return pl.pallas_call(
        functools.partial(_rwkv_scan_kernel, nlt=nlt, tc=tc, rep=rep),
        grid=(G, L // tc),
        in_specs=[col, col, col, col, col, row, const((B_DH, LANES)), const((nlt, LANES)), const((nlt, LANES)), st],
        out_specs=[row, st],
        out_shape=[jax.ShapeDtypeStruct(v.shape, F32), jax.ShapeDtypeStruct(s0.shape, F32)],
        scratch_shapes=[pltpu.VMEM((tc, B_DH, LANES), F32), pltpu.VMEM((tc, B_DH, LANES), F32)],
        compiler_params=_cparams(("arbitrary", "arbitrary")),
        name="rwkv_scan",
    )(*cols, v, rk, lnw, lnb, s0)


def _to_cols(x, b, L):
    return x.reshape(b, L, B_HEADS, B_DH).transpose(1, 3, 0, 2).reshape(L, B_DH, b * B_HEADS)


def rwkv_recurrence(r, w, k, kk, a, v, S0, b, L, r_k, ln_w, ln_b):
    bh = b * B_HEADS
    cols = [_to_cols(t, b, L) for t in (r, w, k, kk, a)]
    vc = _to_cols(v, b, L)
    rep = max(LANES // bh, 1)
    nlt = B_DH // rep
    G = max(bh // LANES, 1)
    lane = jnp.arange(LANES)
    head = lane % B_HEADS
    val = jnp.arange(nlt)[:, None] * rep + (lane // (LANES // rep))[None, :]
    rk_t = r_k[head].T
    lnw_t = ln_w.reshape(B_HEADS, B_DH)[head[None, :], val]
    lnb_t = ln_b.reshape(B_HEADS, B_DH)[head[None, :], val]
    tc = math.gcd(L, 32)
    if rep > 1:
        cols = [jnp.concatenate([c] * rep, axis=-1) for c in cols]
        vc = vc.reshape(L, nlt, LANES)
        s0 = S0.reshape(b, B_HEADS, nlt, rep, B_DH).transpose(2, 4, 3, 0, 1).reshape(1, nlt, B_DH, LANES)
    else:
        s0 = S0.reshape(G, LANES, B_DH, B_DH).transpose(0, 2, 3, 1)
    y, s = rwkv_scan(cols, vc, rk_t, lnw_t, lnb_t, s0, nlt, tc, rep)
    y = y.reshape(L, B_DH, b, B_HEADS).transpose(2, 0, 3, 1).reshape(b * L, B_W)
    if rep > 1:
        s = s.reshape(nlt, B_DH, rep, b, B_HEADS).transpose(3, 4, 0, 2, 1)
    else:
        s = s.transpose(0, 3, 1, 2)
    return y, s.reshape(b, B_HEADS, B_DH, B_DH)


RWKV_PREP_ROWS = 256
RWKV_SHORT_ROWS = 64
B_LORA_OFF = 3 * B_W


def _rwkv_prep_kernel(z_ref, sh0_ref, mu_ref, wwa_ref, g2_ref, w0_ref, a0_ref, kk_ref, ka_ref,
                      r_ref, w_ref, k_ref, kkn_ref, a_ref, v_ref, g_ref, sh_ref, *, T, nseq):
    @pl.when(pl.program_id(1) == 0)
    def _():
        sh_ref[...] = sh0_ref[...]

    z = z_ref[...]
    rowid = lax.broadcasted_iota(jnp.int32, (z.shape[0], 1), 0)
    zprev = pltpu.roll(z, 1, axis=0)
    for u in range(nseq):
        zprev = jnp.where(rowid == u * T, sh_ref[u], zprev)
    for u in range(nseq):
        sh_ref[u] = z[(u + 1) * T - 1:(u + 1) * T, :]
    zs = z + (zprev - z) * mu_ref[...]
    r = zs[:, :B_W]
    k = zs[:, B_W:2 * B_W]
    lora = zs[:, B_LORA_OFF:B_LORA_OFF + LANES]
    lane = lax.broadcasted_iota(jnp.int32, lora.shape, 1)
    lora = jnp.where(lane < B_W_RANK, jnp.tanh(lora), lora).astype(BF16)
    wa = jnp.dot(lora, wwa_ref[...], preferred_element_type=F32)
    w_log = -jax.nn.softplus(-(w0_ref[...] + wa[:, :B_W])) - B_DECAY_OFFSET
    a = jax.nn.sigmoid(a0_ref[...] + wa[:, B_W:])
    zg = zs[:, B_LORA_OFF + LANES:B_LORA_OFF + LANES + B_G_RANK]
    r_ref[...] = r
    w_ref[...] = jnp.exp(-jnp.exp(w_log))
    k_ref[...] = k * (1.0 + (a - 1.0) * ka_ref[...])
    kkn_ref[...] = k * kk_ref[...]
    a_ref[...] = a
    v_ref[...] = zs[:, 2 * B_W:3 * B_W]
    g_ref[...] = jnp.dot(jax.nn.sigmoid(zg).astype(BF16), g2_ref[...], preferred_element_type=F32)


def rwkv_prep(zin, row0, b, L, gate_into, shift0, mu, w0, w2, a0, a2, g2, k_k, k_a):
    assert B_W_RANK + B_A_RANK == LANES
    T = math.gcd(L, RWKV_PREP_ROWS)
    nseq = 1 if T == RWKV_PREP_ROWS else RWKV_SHORT_ROWS // T
    assert nseq == 1 or (L == T and b % nseq == 0)
    R = T * nseq
    nchunk = L // T
    blk0 = row0 // R
    wwa = jnp.zeros((LANES, 2 * B_W), F32).at[:B_W_RANK, :B_W].set(w2).at[B_W_RANK:, B_W:].set(a2).astype(BF16)
    row = lambda v: v.reshape(1, -1)
    blk = pl.BlockSpec((R, B_W), lambda i, c: (i * nchunk + c, 0))
    gblk = pl.BlockSpec((R, B_W), lambda i, c: (blk0 + i * nchunk + c, 0))
    shspec = pl.BlockSpec((nseq, 1, B_COLS), lambda i, c: (i, 0, 0))
    const = lambda shape: pl.BlockSpec(shape, lambda i, c: (0,) * len(shape))
    in_specs = [pl.BlockSpec((R, B_COLS), lambda i, c: (blk0 + i * nchunk + c, 0)), shspec, const((1, B_COLS)),
                const((LANES, 2 * B_W)), const((B_G_RANK, B_W)),
                const((1, B_W)), const((1, B_W)), const((1, B_W)), const((1, B_W))]
    body, xspecs, xargs, aliases = _fill_into(
        functools.partial(_rwkv_prep_kernel, T=T, nseq=nseq), len(in_specs), gate_into, 6)
    outs = pl.pallas_call(
        body,
        grid=(b // nseq, nchunk),
        in_specs=in_specs + xspecs,
        out_specs=[blk] * 6 + [gblk, shspec],
        out_shape=[jax.ShapeDtypeStruct((b * L, B_W), F32)] * 6
        + [jax.ShapeDtypeStruct((zin.shape[0], B_W), F32), jax.ShapeDtypeStruct((b, 1, B_COLS), F32)],
        input_output_aliases=aliases,
        compiler_params=_cparams(("arbitrary", "arbitrary")),
        name="rwkv_prep",
    )(zin, shift0.reshape(b, 1, B_COLS), row(mu), wwa, g2.astype(BF16), row(w0), row(a0), row(k_k), row(k_a),
      *xargs)
    return outs[:6], outs[6], outs[7].reshape(b, B_COLS)


MLSTM_ROWS = 64
HIGHEST = lax.Precision.HIGHEST


def _mlstm_kernel(q_ref, k_ref, v_ref, o_ref, g_ref, gt_ref, brow_ref, bcol_ref, nw_ref,
                  c0_ref, n0_ref, m0_ref, h_ref, c_ref, n_ref, m_ref, *, T, nseq):
    R = MLSTM_ROWS

    @pl.when(pl.program_id(1) == 0)
    def _():
        c_ref[...] = c0_ref[...]
        n_ref[...] = n0_ref[...]
        m_ref[...] = m0_ref[...]

    shift = T.bit_length() - 1
    ri = lax.broadcasted_iota(jnp.int32, (R, R), 0)
    ci = lax.broadcasted_iota(jnp.int32, (R, R), 1)
    mask = (ci <= ri) & (jnp.right_shift(ri, shift) == jnp.right_shift(ci, shift))
    lmat = mask.astype(F32)
    rowid = lax.broadcasted_iota(jnp.int32, (R, 1), 0)
    rsel = [(rowid >= u * T) & (rowid < (u + 1) * T) for u in range(nseq)]

    g = g_ref[...] + brow_ref[...]
    lane = lax.broadcasted_iota(jnp.int32, g.shape, 1)
    glog = jnp.where((lane >= A_HEADS) & (lane < 2 * A_HEADS), jax.nn.log_sigmoid(g), g)
    gt = gt_ref[...] + bcol_ref[...]
    sub = lax.broadcasted_iota(jnp.int32, gt.shape, 0)
    gtlog = jnp.where(sub >= A_HEADS, jax.nn.log_sigmoid(gt), gt)
    bc_col = jnp.dot(lmat, glog, precision=HIGHEST, preferred_element_type=F32)
    bc_row = lax.dot_general(gtlog, lmat, (((1,), (1,)), ((), ())), precision=HIGHEST,
                             preferred_element_type=F32)
    lane_m = lax.broadcasted_iota(jnp.int32, (1, LANES), 1)
    m_old = [m_ref[u] for u in range(nseq)]
    m_out = [jnp.zeros((1, LANES), F32) for _ in range(nseq)]

    for h in range(A_HEADS):
        hs = slice(h * A_DK, (h + 1) * A_DK)
        bcc = bc_col[:, A_HEADS + h:A_HEADS + h + 1]
        bcr = bc_row[A_HEADS + h:A_HEADS + h + 1, :]
        lir = gtlog[h:h + 1, :]
        lic = glog[:, h:h + 1]
        m_u = [m_old[u][:, h:h + 1] for u in range(nseq)]
        m_col = m_u[0]
        for u in range(1, nseq):
            m_col = jnp.where(rsel[u], m_u[u], m_col)
        dmat = jnp.where(mask, bcc - bcr + lir, -jnp.inf)
        inter = bcc + m_col
        mt = jnp.maximum(inter, jnp.max(dmat, axis=1, keepdims=True))
        p = jnp.exp(dmat - mt)
        qh = q_ref[:, hs] * (A_DK ** -0.5)
        kh = k_ref[:, hs]
        qb, kb, vb = qh.astype(BF16), kh.astype(BF16), v_ref[:, hs].astype(BF16)
        wq = lax.dot_general(qb, kb, (((1,), (1,)), ((), ())), preferred_element_type=F32) * p
        wi = jnp.exp(inter - mt)
        c_old = [c_ref[u, h] for u in range(nseq)]
        n_old = [n_ref[u, h:h + 1, :] for u in range(nseq)]
        qc = jnp.dot(qb, c_old[0].astype(BF16), preferred_element_type=F32)
        qn = jnp.sum(qh * n_old[0], axis=1, keepdims=True)
        for u in range(1, nseq):
            qc = jnp.where(rsel[u], jnp.dot(qb, c_old[u].astype(BF16), preferred_element_type=F32), qc)
            qn = jnp.where(rsel[u], jnp.sum(qh * n_old[u], axis=1, keepdims=True), qn)
        num = jnp.dot(wq.astype(BF16), vb, preferred_element_type=F32) + wi * qc
        den = jnp.sum(wq, axis=1, keepdims=True) + wi * qn
        hh = num / jnp.maximum(jnp.abs(den), jnp.exp(-mt))
        hh = hh * lax.rsqrt(jnp.mean(hh * hh, axis=-1, keepdims=True) + EPS)
        h_ref[:, hs] = hh * nw_ref[:, hs] * jax.nn.sigmoid(o_ref[:, hs])
        for u in range(nseq):
            b_last = bcc[(u + 1) * T - 1:(u + 1) * T, :]
            gs = b_last - bcc + lic
            gmax = jnp.max(gs if nseq == 1 else jnp.where(rsel[u], gs, -jnp.inf), axis=0, keepdims=True)
            m_new = jnp.maximum(b_last + m_u[u], gmax)
            decay = jnp.exp(b_last + m_u[u] - m_new)
            ws = jnp.exp(gs - m_new)
            if nseq > 1:
                ws = jnp.where(rsel[u], ws, 0.0)
            kw = kh * ws
            c_ref[u, h] = decay * c_old[u] + lax.dot_general(
                kw.astype(BF16), vb, (((0,), (0,)), ((), ())), preferred_element_type=F32)
            n_ref[u, h:h + 1, :] = decay * n_old[u] + jnp.sum(kw, axis=0, keepdims=True)
            m_out[u] = jnp.where(lane_m == h, m_new, m_out[u])
    for u in range(nseq):
        m_ref[u] = m_out[u]


def mlstm(zin, gates_t, row0, b, L, h_into, b_i, b_f, m_norm, C0, n0, m0):
    R = MLSTM_ROWS
    T = math.gcd(L, R)
    nseq = R // T
    assert nseq == 1 or (L == T and b % nseq == 0)
    nchunk = L // T
    blk0 = row0 // R
    bias = jnp.concatenate([b_i, b_f])
    bias_row = jnp.zeros((1, LANES), F32).at[0, :2 * A_HEADS].set(bias)
    bias_col = bias.reshape(2 * A_HEADS, 1)
    m0p = jnp.zeros((b, 1, LANES), F32).at[:, 0, :A_HEADS].set(m0)
    blk = lambda i, c: blk0 + i * nchunk + c
    rowblk = lambda col: pl.BlockSpec((R, A_QK), lambda i, c: (blk(i, c), EVEN_A_OFF // A_QK + col))
    cspec = pl.BlockSpec((nseq, A_HEADS, A_DK, A_DV), lambda i, c: (i, 0, 0, 0))
    nspec = pl.BlockSpec((nseq, A_HEADS, A_DK), lambda i, c: (i, 0, 0))
    mspec = pl.BlockSpec((nseq, 1, LANES), lambda i, c: (i, 0, 0))
    const = lambda shape: pl.BlockSpec(shape, lambda i, c: (0,) * len(shape))
    in_specs = [rowblk(0), rowblk(1), rowblk(2), rowblk(3),
                pl.BlockSpec((R, LANES), lambda i, c: (blk(i, c), EVEN_G_OFF // LANES)),
                pl.BlockSpec((None, 2 * A_HEADS, R), lambda i, c: (blk(i, c), 0, 0)),
                const((1, LANES)), const((2 * A_HEADS, 1)), const((1, A_V)),
                cspec, nspec, mspec]
    body, xspecs, xargs, aliases = _fill_into(
        functools.partial(_mlstm_kernel, T=T, nseq=nseq), len(in_specs), h_into, 0)
    h, C, n, m = pl.pallas_call(
        body,
        grid=(b // nseq, nchunk),
        in_specs=in_specs + xspecs,
        out_specs=[pl.BlockSpec((R, A_V), lambda i, c: (blk(i, c), 0)), cspec, nspec, mspec],
        out_shape=[jax.ShapeDtypeStruct((zin.shape[0], A_V), F32),
                   jax.ShapeDtypeStruct(C0.shape, F32), jax.ShapeDtypeStruct(n0.shape, F32),
                   jax.ShapeDtypeStruct(m0p.shape, F32)],
        input_output_aliases=aliases,
        compiler_params=_cparams(("arbitrary", "arbitrary")),
        name="mlstm",
    )(zin, zin, zin, zin, zin, gates_t, bias_row, bias_col, m_norm.reshape(1, A_V), C0, n0, m0p, *xargs)
    return h, C, n, m[:, 0, :A_HEADS]


def even_mixer(zin, gates_t, row0, b, L, into, C0, n0, m0, S0, shift0, b_i, b_f, m_norm, mu, w0, w2, a0, a2, g2,
               k_k, k_a, r_k, ln_w, ln_b):
    hA, C, n, m = mlstm(zin, gates_t, row0, b, L, into[0], b_i, b_f, m_norm, C0, n0, m0)
    (r, w, k, kk, a, v), g, shift = rwkv_prep(zin, row0, b, L, into[1], shift0, mu, w0, w2, a0, a2, g2, k_k, k_a)
    hB, S = rwkv_recurrence(r, w, k, kk, a, v, S0, b, L, r_k, ln_w, ln_b)
    return hA, hB, g, (C, n, m, S, shift)


SUBLANES = 8
C_PAIRS = C_HEADS // 2
C_GROUP_W = C_INNER // C_GROUPS
C_BC_W = 2 * C_GROUPS * C_STATE
SSD_SHORT_ROWS = 32


def _ssd_kernel(z_ref, x_ref, bc_ref, dt_ref, dtt_ref, cw_ref, cb_ref, dtb_ref, dtbt_ref, al_ref, alt_ref,
                dsk_ref, nw_ref, s0_ref, cv0_ref, y_ref, s_ref, cv_ref, *, T, nseq):
    R = T * nseq

    @pl.when(pl.program_id(1) == 0)
    def _():
        s_ref[...] = s0_ref[...]
        cv_ref[...] = cv0_ref[...]

    shift = T.bit_length() - 1
    ri = lax.broadcasted_iota(jnp.int32, (R, R), 0)
    ci = lax.broadcasted_iota(jnp.int32, (R, R), 1)
    mask = (ci <= ri) & (jnp.right_shift(ri, shift) == jnp.right_shift(ci, shift))
    lmat = mask.astype(F32)
    rowid = lax.broadcasted_iota(jnp.int32, (R, 1), 0)
    rsel = [(rowid >= u * T) & (rowid < (u + 1) * T) for u in range(nseq)]

    def conv_silu(x, cols):
        acc = cb_ref[:, cols] + x * cw_ref[C_CONV - 1:C_CONV, cols]
        for d in range(1, C_CONV):
            xd = pltpu.roll(x, d, axis=0)
            for u in range(nseq):
                for t in range(d):
                    src = SUBLANES - d + t
                    xd = jnp.where(rowid == u * T + t, cv_ref[u, src:src + 1, cols], xd)
            acc = acc + xd * cw_ref[C_CONV - 1 - d:C_CONV - d, cols]
        for u in range(nseq):
            cv_ref[u, :, cols] = x[(u + 1) * T - SUBLANES:(u + 1) * T, :]
        return acc * jax.nn.sigmoid(acc)

    xc = conv_silu(x_ref[...], slice(0, C_INNER))
    bcc_ = conv_silu(bc_ref[...], slice(C_INNER, C_CONV_DIM))

    dtv = jax.nn.softplus(dt_ref[...] + dtb_ref[...])
    dtt = jax.nn.softplus(dtt_ref[...] + dtbt_ref[...])
    cum_col = jnp.dot(lmat, dtv * (-jnp.exp(al_ref[...])), precision=HIGHEST, preferred_element_type=F32)
    cum_row = lax.dot_general(dtt * (-jnp.exp(alt_ref[...])), lmat, (((1,), (1,)), ((), ())),
                              precision=HIGHEST, preferred_element_type=F32)

    lo = lax.broadcasted_iota(jnp.int32, (R, LANES), 1) < C_HEADDIM
    rlo = lax.broadcasted_iota(jnp.int32, (LANES, 1), 0) < C_HEADDIM
    nt = (((1,), (1,)), ((), ()))
    tn = (((0,), (0,)), ((), ()))
    pairs_per_group = C_PAIRS // C_GROUPS
    for g in range(C_GROUPS):
        bg = bcc_[:, g * C_STATE:(g + 1) * C_STATE].astype(BF16)
        cg = bcc_[:, (C_GROUPS + g) * C_STATE:(C_GROUPS + g + 1) * C_STATE].astype(BF16)
        cbm = lax.dot_general(cg, bg, nt, preferred_element_type=F32)
        ys = None
        for u in range(nseq):
            sg = s_ref[u, g * pairs_per_group:(g + 1) * pairs_per_group].reshape(C_GROUP_W, C_STATE)
            t_u = lax.dot_general(cg, sg.astype(BF16), nt, preferred_element_type=F32)
            ys = t_u if u == 0 else jnp.where(rsel[u], t_u, ys)
        for q in range(pairs_per_group):
            pr = g * pairs_per_group + q
            ps = slice(pr * LANES, (pr + 1) * LANES)
            xp = xc[:, ps]
            cc = [cum_col[:, 2 * pr + e:2 * pr + e + 1] for e in range(2)]
            intra = None
            for e, keep in ((0, lo), (1, jnp.logical_not(lo))):
                hh = 2 * pr + e
                seg = jnp.exp(jnp.where(mask, cc[e] - cum_row[hh:hh + 1, :], -jnp.inf))
                mix = cbm * seg * dtt[hh:hh + 1, :]
                part = jnp.dot(mix.astype(BF16), jnp.where(keep, xp, 0.0).astype(BF16),
                               preferred_element_type=F32)
                intra = part if intra is None else intra + part
            scale = jnp.where(lo, jnp.exp(cc[0]), jnp.exp(cc[1]))
            yp = intra + scale * ys[:, q * LANES:(q + 1) * LANES] + dsk_ref[:, ps] * xp
            zp = z_ref[:, ps]
            y_ref[:, ps] = yp * (zp * jax.nn.sigmoid(zp))
            for u in range(nseq):
                last = (u + 1) * T - 1
                ct = [cc[e][last:last + 1, :] for e in range(2)]
                tail = jnp.where(lo, jnp.exp(ct[0] - cc[0]) * dtv[:, 2 * pr:2 * pr + 1],
                                 jnp.exp(ct[1] - cc[1]) * dtv[:, 2 * pr + 1:2 * pr + 2])
                xw = xp * tail
                if nseq > 1:
                    xw = jnp.where(rsel[u], xw, 0.0)
                upd = lax.dot_general(xw.astype(BF16), bg, tn, preferred_element_type=F32)
                dec = jnp.where(rlo, jnp.exp(ct[0]), jnp.exp(ct[1]))
                s_ref[u, pr] = dec * s_ref[u, pr] + upd

    for g in range(C_GROUPS):
        gs_ = slice(g * C_GROUP_W, (g + 1) * C_GROUP_W)
        yg = y_ref[:, gs_]
        y_ref[:, gs_] = yg * lax.rsqrt(jnp.mean(yg * yg, axis=-1, keepdims=True) + EPS) * nw_ref[:, gs_]


def ssd_mixer(zin, row0, b, L, y_into, ssm0, conv0, conv_w, conv_b, dt_bias, a_log, d_skip, norm_w):
    T = math.gcd(L, C_CHUNK)
    nseq = 1 if T == C_CHUNK else SSD_SHORT_ROWS // T
    assert nseq == 1 or (L == T and b % nseq == 0)
    R = T * nseq
    nchunk = L // T
    nblk = b * L // R
    blk0 = row0 // R
    dt_t = zin[row0:row0 + b * L, ODD_DT_OFF:ODD_DT_OFF + C_HEADS].reshape(nblk, R, C_HEADS).transpose(0, 2, 1)
    pad_row = lambda v: jnp.zeros((1, LANES), F32).at[0, :C_HEADS].set(v)
    s0 = ssm0.reshape(b, C_PAIRS, 2 * C_HEADDIM, C_STATE)
    cv0 = jnp.concatenate([jnp.zeros((b, SUBLANES - (C_CONV - 1), C_CONV_DIM), F32), conv0], axis=1)
    blk = lambda i, c: blk0 + i * nchunk + c
    sspec = pl.BlockSpec((nseq, C_PAIRS, 2 * C_HEADDIM, C_STATE), lambda i, c: (i, 0, 0, 0))
    cvspec = pl.BlockSpec((nseq, SUBLANES, C_CONV_DIM), lambda i, c: (i, 0, 0))
    const = lambda shape: pl.BlockSpec(shape, lambda i, c: (0,) * len(shape))
    in_specs = [pl.BlockSpec((R, C_INNER), lambda i, c: (blk(i, c), 0)),
                pl.BlockSpec((R, C_INNER), lambda i, c: (blk(i, c), 1)),
                pl.BlockSpec((R, C_BC_W), lambda i, c: (blk(i, c), 2 * C_INNER // C_BC_W)),
                pl.BlockSpec((R, LANES), lambda i, c: (blk(i, c), ODD_DT_OFF // LANES)),
                pl.BlockSpec((None, C_HEADS, R), lambda i, c: (i * nchunk + c, 0, 0)),
                const((C_CONV, C_CONV_DIM)), const((1, C_CONV_DIM)),
                const((1, LANES)), const((C_HEADS, 1)), const((1, LANES)), const((C_HEADS, 1)),
                const((1, C_INNER)), const((1, C_INNER)), sspec, cvspec]
    body, xspecs, xargs, aliases = _fill_into(
        functools.partial(_ssd_kernel, T=T, nseq=nseq), len(in_specs), y_into, 0)
    y, s, cv = pl.pallas_call(
        body,
        grid=(b // nseq, nchunk),
        in_specs=in_specs + xspecs,
        out_specs=[pl.BlockSpec((R, C_INNER), lambda i, c: (blk(i, c), 0)), sspec, cvspec],
        out_shape=[jax.ShapeDtypeStruct((zin.shape[0], C_INNER), F32),
                   jax.ShapeDtypeStruct(s0.shape, F32), jax.ShapeDtypeStruct(cv0.shape, F32)],
        input_output_aliases=aliases,
        compiler_params=_cparams(("arbitrary", "arbitrary")),
        name="ssd",
    )(zin, zin, zin, zin, dt_t, conv_w, conv_b.reshape(1, C_CONV_DIM),
      pad_row(dt_bias), dt_bias.reshape(C_HEADS, 1), pad_row(a_log), a_log.reshape(C_HEADS, 1),
      jnp.repeat(d_skip, C_HEADDIM).reshape(1, C_INNER), norm_w.reshape(1, C_INNER), s0, cv0, *xargs)
    return y, (s.reshape(b, C_HEADS, C_HEADDIM, C_STATE), cv[:, SUBLANES - (C_CONV - 1):])


def _even_w_in(w):
    qkvo = w[:, :2 * A_QK + 2 * A_V]
    gates = w[:, 2 * A_QK + 2 * A_V:A_COLS]
    rwkv = w[:, A_COLS:]
    pad = jnp.zeros((D_MODEL, EVEN_GATE_PAD - 2 * A_HEADS), w.dtype)
    return jnp.concatenate([rwkv, gates, pad, qkvo], axis=1).astype(BF16)


def _odd_w_in(w):
    pad = jnp.zeros((D_MODEL, ODD_DT_PAD - C_HEADS), w.dtype)
    return jnp.concatenate([w, pad], axis=1).astype(BF16)


def kernel(x_prompt, x_sample, state_mlstm_C, state_mlstm_n, state_mlstm_m, state_rwkv_S,
           state_rwkv_shift, state_ssm, state_conv, p_prompt, p_sample,
           norm_mix, norm_ffn, w_ffn_up, w_ffn_down, w_ple_proj, norm_ple, w_ple_gate, norm_final,
           w_in_even, mlstm_b_i, mlstm_b_f, mlstm_norm, rwkv_mu, rwkv_w0, rwkv_w2, rwkv_a0, rwkv_a2,
           rwkv_g2, rwkv_k_k, rwkv_k_a, rwkv_r_k, rwkv_ln_w, rwkv_ln_b, w_out_even,
           w_in_odd, conv_w, conv_b, dt_bias, a_log, d_skip, ssm_norm, w_out_odd):
    bp, Lp, _ = x_prompt.shape
    bs, Ls, _ = x_sample.shape
    n_p = bp * Lp
    x = jnp.concatenate([x_prompt.reshape(n_p, D_MODEL), x_sample.reshape(bs * Ls, D_MODEL)], axis=0)
    p_all = jnp.concatenate([p_prompt.reshape(DEPTH, n_p, PLE_DIM),
                             p_sample.reshape(DEPTH, bs * Ls, PLE_DIM)], axis=1)

    even_small = (mlstm_b_i, mlstm_b_f, mlstm_norm, rwkv_mu, rwkv_w0, rwkv_w2, rwkv_a0, rwkv_a2,
                  rwkv_g2, rwkv_k_k, rwkv_k_a, rwkv_r_k, rwkv_ln_w, rwkv_ln_b)
    odd_small = (conv_w, conv_b, dt_bias, a_log, d_skip, ssm_norm)

    zeros_even = (jnp.zeros((bp, A_HEADS, A_DK, A_DV), F32), jnp.zeros((bp, A_HEADS, A_DK), F32),
                  jnp.zeros((bp, A_HEADS), F32), jnp.zeros((bp, B_HEADS, B_DH, B_DH), F32),
                  jnp.zeros((bp, B_COLS), F32))
    zeros_odd = (jnp.zeros((bp, C_HEADS, C_HEADDIM, C_STATE), F32),
                 jnp.zeros((bp, C_CONV - 1, C_CONV_DIM), F32))

    st_p_even, st_s_even, st_p_odd, st_s_odd = [], [], [], []
    for i in range(DEPTH):
        j = i // 2
        if i % 2 == 0:
            zin = norm_matmul(x, norm_mix[i], _even_w_in(w_in_even[j]), 512)
            small = [t[j] for t in even_small]
            gates_t = zin[:, EVEN_G_OFF:EVEN_G_OFF + 2 * A_HEADS].reshape(
                -1, MLSTM_ROWS, 2 * A_HEADS).transpose(0, 2, 1)
            ha, hb_p, g, sp = even_mixer(zin, gates_t, 0, bp, Lp, (None, None), *zeros_even, *small)
            ha, hb_s, g, ss = even_mixer(zin, gates_t, n_p, bs, Ls, (ha, g), state_mlstm_C[j], state_mlstm_n[j],
                                         state_mlstm_m[j], state_rwkv_S[j], state_rwkv_shift[j], *small)
            st_p_even.append(sp)
            st_s_even.append(ss)
            wo = w_out_even[j].astype(BF16)
            hb = jnp.concatenate([hb_p, hb_s], axis=0)
            x = matmul_res([(ha, None, wo[:A_V]), (hb, g, wo[A_V:])], x)
        else:
            zin = norm_matmul(x, norm_mix[i], _odd_w_in(w_in_odd[j]), 768)
            small = [t[j] for t in odd_small]
            mix, sp = ssd_mixer(zin, 0, bp, Lp, None, *zeros_odd, *small)
            mix, ss = ssd_mixer(zin, n_p, bs, Ls, mix, state_ssm[j], state_conv[j], *small)
            st_p_odd.append(sp)
            st_s_odd.append(ss)
            x = matmul_res([(mix, None, w_out_odd[j].astype(BF16))], x)
        x = ffn(x, norm_ffn[i], w_ffn_up[i].astype(BF16), w_ffn_down[i].astype(BF16))
        x = ple(x, norm_ple[i], w_ple_gate[i].astype(BF16), p_all[i], w_ple_proj[i].astype(BF16))
    y = final_norm(x, norm_final)
    y_prompt = y[:n_p].reshape(bp, Lp, D_MODEL)
    y_sample = y[n_p:].reshape(bs, Ls, D_MODEL)
    stack = lambda sts, idx: jnp.stack([s[idx] for s in sts])
    return (y_prompt, y_sample,
            stack(st_p_even, 0), stack(st_p_even, 1), stack(st_p_even, 2), stack(st_p_even, 3),
            stack(st_p_even, 4), stack(st_p_odd, 0), stack(st_p_odd, 1),
            stack(st_s_even, 0), stack(st_s_even, 1), stack(st_s_even, 2), stack(st_s_even, 3),
            stack(st_s_even, 4), stack(st_s_odd, 0), stack(st_s_odd, 1))
```

```python
import math
import functools
import jax
import jax.numpy as jnp
from jax import lax
from jax.experimental import pallas as pl
from jax.experimental.pallas import tpu as pltpu

D_MODEL = 1024
DEPTH = 4
F32 = jnp.float32
BF16 = jnp.bfloat16
EPS = 1e-6
N_EVEN = (DEPTH + 1) // 2
N_ODD = DEPTH // 2
D_FF = 4 * D_MODEL
PLE_DIM = 256

A_HEADS = 4
A_DK = D_MODEL // 8
A_DV = D_MODEL // 8
A_CHUNK = 64
A_QK = A_HEADS * A_DK
A_V = A_HEADS * A_DV
A_COLS = 2 * A_QK + 2 * A_V + 2 * A_HEADS

B_HEADS = 8
B_DH = 64
B_W = B_HEADS * B_DH
B_W_RANK = 64
B_A_RANK = 64
B_G_RANK = 128
B_COLS = 3 * B_W + B_W_RANK + B_A_RANK + B_G_RANK
B_DECAY_OFFSET = 0.5
B_GN_EPS = 64e-5

EVEN_COLS = A_COLS + B_COLS
EVEN_OUT = A_V + B_W

C_INNER = 2 * D_MODEL
C_HEADDIM = 64
C_HEADS = C_INNER // C_HEADDIM
C_GROUPS = 4
C_HPG = C_HEADS // C_GROUPS
C_STATE = 128
C_CONV = 4
C_CHUNK = 128
C_CONV_DIM = C_INNER + 2 * C_GROUPS * C_STATE
ODD_COLS = C_INNER + C_CONV_DIM + C_HEADS

LANES = 128
VMEM_LIMIT = 56 * 1024 * 1024
TOKEN_TILE = 256

EVEN_GATE_PAD = 256
EVEN_N = B_COLS + EVEN_GATE_PAD + 2 * A_QK + 2 * A_V
EVEN_G_OFF = B_COLS
EVEN_A_OFF = B_COLS + EVEN_GATE_PAD
ODD_DT_PAD = 256
ODD_N = C_INNER + C_CONV_DIM + ODD_DT_PAD
ODD_DT_OFF = C_INNER + C_CONV_DIM


def _cparams(sem):
    return pltpu.CompilerParams(dimension_semantics=sem, vmem_limit_bytes=VMEM_LIMIT)


def _rms(x, g):
    return x * lax.rsqrt(jnp.mean(x * x, axis=-1, keepdims=True) + EPS) * g


def _resident(shape):
    nd = len(shape)
    return pl.BlockSpec(shape, lambda *_: (0,) * nd, pipeline_mode=pl.Buffered(1))


def _rows(width):
    return pl.BlockSpec((TOKEN_TILE, width), lambda i: (i, 0))


def _fill_into(body, n_in, into, out_idx):
    if into is None:
        return body, [], [], {}

    def skipping(*refs):
        return body(*refs[:n_in], *refs[n_in + 1:])

    return skipping, [pl.BlockSpec(memory_space=pl.ANY)], [into], {n_in: out_idx}


def _norm_matmul_kernel(x_ref, g_ref, w_ref, o_ref, *, tn):
    xn = _rms(x_ref[...], g_ref[...]).astype(BF16)
    for n0 in range(0, w_ref.shape[1], tn):
        o_ref[:, n0:n0 + tn] = jnp.dot(xn, w_ref[:, n0:n0 + tn], preferred_element_type=F32)


def norm_matmul(x, g, w, tn):
    m, n = x.shape[0], w.shape[1]
    return pl.pallas_call(
        functools.partial(_norm_matmul_kernel, tn=tn),
        grid=(m // TOKEN_TILE,),
        in_specs=[_rows(D_MODEL), _resident((1, D_MODEL)), _resident(w.shape)],
        out_specs=_rows(n),
        out_shape=jax.ShapeDtypeStruct((m, n), F32),
        compiler_params=_cparams(("arbitrary",)),
        name="norm_matmul",
    )(x, g.reshape(1, D_MODEL), w)


def _matmul_res_kernel(*refs, kinds):
    x_ref, o_ref = refs[-2], refs[-1]
    acc = x_ref[...]
    pos = 0
    for has_gate, channel_major in kinds:
        a = refs[pos][...]
        if channel_major:
            a = a.T
        if has_gate:
            a = a * refs[pos + 1][...]
        w_ref = refs[pos + 1 + has_gate]
        pos += 2 + has_gate
        acc = acc + jnp.dot(a.astype(BF16), w_ref[...], preferred_element_type=F32)
    o_ref[...] = acc


def matmul_res(terms, x, row0=0, nrows=None, into=None):
    m = x.shape[0]
    nrows = m if nrows is None else nrows
    t0 = row0 // TOKEN_TILE
    rows = lambda width: pl.BlockSpec((TOKEN_TILE, width), lambda i: (t0 + i, 0))
    specs, args, kinds = [], [], []
    for a, gate, w, seq_len in terms:
        if seq_len is None:
            specs.append(rows(a.shape[1]))
        elif seq_len == 0:
            specs.append(pl.BlockSpec((TOKEN_TILE, a.shape[1]), lambda i: (i, 0)))
        else:
            per_seq = seq_len // TOKEN_TILE
            specs.append(pl.BlockSpec((None, a.shape[1], TOKEN_TILE), lambda i: (i // per_seq, 0, i % per_seq)))
        specs += ([rows(w.shape[0])] if gate is not None else []) + [_resident(w.shape)]
        args += [a] + ([gate] if gate is not None else []) + [w]
        kinds.append((int(gate is not None), bool(seq_len)))
    specs.append(rows(D_MODEL))
    body, xspecs, xargs, aliases = _fill_into(
        functools.partial(_matmul_res_kernel, kinds=tuple(kinds)), len(specs), into, 0)
    return pl.pallas_call(
        body,
        grid=(nrows // TOKEN_TILE,),
        in_specs=specs + xspecs,
        out_specs=rows(D_MODEL),
        out_shape=jax.ShapeDtypeStruct((m, D_MODEL), F32),
        input_output_aliases=aliases,
        compiler_params=_cparams(("arbitrary",)),
        name="matmul_res",
    )(*args, x, *xargs)


FFN_CHUNK = 512


def _ffn_kernel(x_ref, g_ref, wu_ref, wd_ref, o_ref):
    x = x_ref[...]
    xn = _rms(x, g_ref[...]).astype(BF16)
    acc = x
    for c0 in range(0, D_FF, FFN_CHUNK):
        h = jnp.dot(xn, wu_ref[:, c0:c0 + FFN_CHUNK], preferred_element_type=F32)
        h = jnp.square(jnp.maximum(h, 0.0)).astype(BF16)
        acc = acc + jnp.dot(h, wd_ref[c0:c0 + FFN_CHUNK, :], preferred_element_type=F32)
    o_ref[...] = acc


def ffn(x, g, wu, wd):
    m = x.shape[0]
    return pl.pallas_call(
        _ffn_kernel,
        grid=(m // TOKEN_TILE,),
        in_specs=[_rows(D_MODEL), _resident((1, D_MODEL)), _resident(wu.shape), _resident(wd.shape)],
        out_specs=_rows(D_MODEL),
        out_shape=jax.ShapeDtypeStruct((m, D_MODEL), F32),
        compiler_params=_cparams(("arbitrary",)),
        name="ffn",
    )(x, g.reshape(1, D_MODEL), wu, wd)


def _ple_kernel(x_ref, g_ref, wg_ref, p_ref, wp_ref, o_ref):
    x = x_ref[...]
    xn = _rms(x, g_ref[...]).astype(BF16)
    gate = jax.nn.sigmoid(jnp.dot(xn, wg_ref[...], preferred_element_type=F32))
    proj = jnp.dot(p_ref[...].astype(BF16), wp_ref[...], preferred_element_type=F32)
    o_ref[...] = x + proj * gate


def ple(x, g, wg, p, wp):
    m = x.shape[0]
    return pl.pallas_call(
        _ple_kernel,
        grid=(m // TOKEN_TILE,),
        in_specs=[_rows(D_MODEL), _resident((1, D_MODEL)), _resident(wg.shape),
                  _rows(PLE_DIM), _resident(wp.shape)],
        out_specs=_rows(D_MODEL),
        out_shape=jax.ShapeDtypeStruct((m, D_MODEL), F32),
        compiler_params=_cparams(("arbitrary",)),
        name="ple",
    )(x, g.reshape(1, D_MODEL), wg, p, wp)


def _final_norm_kernel(x_ref, g_ref, o_ref):
    o_ref[...] = _rms(x_ref[...], g_ref[...])


def final_norm(x, g):
    m = x.shape[0]
    return pl.pallas_call(
        _final_norm_kernel,
        grid=(m // TOKEN_TILE,),
        in_specs=[_rows(D_MODEL), _resident((1, D_MODEL))],
        out_specs=_rows(D_MODEL),
        out_shape=jax.ShapeDtypeStruct((m, D_MODEL), F32),
        compiler_params=_cparams(("arbitrary",)),
        name="final_norm",
    )(x, g.reshape(1, D_MODEL))


def _rwkv_operands(k, a, r, kk_t, ka_t, rk_t):
    kk = k * kk_t
    kk = kk * lax.rsqrt(jnp.maximum(jnp.sum(kk * kk, axis=1, keepdims=True), 1e-24))
    k2 = k * (1.0 + (a - 1.0) * ka_t)
    return -kk, kk * a, k2, jnp.sum(r * k2 * rk_t, axis=1)


def _rwkv_step(s_ref, tile0, ntiles, r, w, k, a, b, vrows):
    ys = []
    for lt in range(ntiles):
        rows = slice((tile0 + lt) * B_DH, (tile0 + lt + 1) * B_DH)
        s = s_ref[rows, :]
        sa = jnp.sum(s * a, axis=0, keepdims=True)
        s = s * w + sa * b + vrows[lt:lt + 1] * k
        s_ref[rows, :] = s
        ys.append(jnp.sum(s * r, axis=0, keepdims=True))
    return jnp.concatenate(ys, axis=0)


LONG_T = 128
LONG_HALF = 64
LONG_NLT = B_DH // 2


def _rwkv_scan_long_kernel(r_ref, w_ref, k_ref, a_ref, v_ref, kkt_ref, kat_ref, rkt_ref, lnw_ref, lnb_ref, s0_ref,
                           yt_ref, s_ref, or_s, ow_s, ok_s, oa_s, av_s, v_s, y_s, bonus_s):
    nb = r_ref.shape[0]

    @pl.when(pl.program_id(0) == 0)
    def _():
        s_ref[...] = s0_ref[...]

    lo = lax.broadcasted_iota(jnp.int32, (LONG_HALF, LANES), 1) < LANES // 2

    for half in range(LONG_T // LONG_HALF):
        hs = slice(half * LONG_HALF, (half + 1) * LONG_HALF)

        def tile(ref, c):
            m = jnp.concatenate([ref[b, pl.ds(c, B_HEADS, stride=B_DH), :] for b in range(nb)], axis=0)[:, hs]
            return jnp.concatenate([m, m], axis=0).T

        def relayout(c, carry):
            for ref, dst in ((r_ref, or_s), (w_ref, ow_s), (k_ref, ok_s), (a_ref, oa_s)):
                dst[pl.ds(c, LONG_HALF, stride=B_DH), :] = tile(ref, c)
                dst[pl.ds(c + LONG_NLT, LONG_HALF, stride=B_DH), :] = tile(ref, c + LONG_NLT)
            v_s[pl.ds(half * LONG_HALF * LONG_NLT + c, LONG_HALF, stride=LONG_NLT), :] = jnp.where(
                lo, tile(v_ref, c), tile(v_ref, c + LONG_NLT))
            return carry

        lax.fori_loop(0, LONG_NLT, relayout, 0)

        shape3 = (LONG_HALF, B_DH, LANES)
        av, bv, k2, bonus = _rwkv_operands(ok_s[...].reshape(shape3), oa_s[...].reshape(shape3),
                                           or_s[...].reshape(shape3), kkt_ref[...], kat_ref[...], rkt_ref[...])
        av_s[...] = av.reshape(LONG_HALF * B_DH, LANES)
        oa_s[...] = bv.reshape(LONG_HALF * B_DH, LANES)
        ok_s[...] = k2.reshape(LONG_HALF * B_DH, LANES)
        bonus_s[hs, :] = bonus

        def step(t, carry):
            kr = pl.ds(pl.multiple_of(t * B_DH, B_DH), B_DH)
            vr = pl.ds(pl.multiple_of((half * LONG_HALF + t) * LONG_NLT, LONG_NLT), LONG_NLT)
            y_s[vr, :] = _rwkv_step(s_ref, 0, LONG_NLT, or_s[kr, :], ow_s[kr, :], ok_s[kr, :], av_s[kr, :],
                                    oa_s[kr, :], v_s[vr, :])
            return carry

        lax.fori_loop(0, LONG_HALF, step, 0)

    def head_sum(x):
        tot = jnp.sum(x, axis=1)
        return tot + pltpu.roll(tot, LANES // 2, axis=1)

    y = y_s[...].reshape(LONG_T, LONG_NLT, LANES)
    d = y - (head_sum(y) * (1.0 / B_DH))[:, None, :]
    var = head_sum(d * d) * (1.0 / B_DH)
    y = (d * lax.rsqrt(var + B_GN_EPS)[:, None, :] * lnw_ref[...] + lnb_ref[...]
         + bonus_s[...][:, None, :] * v_s[...].reshape(LONG_T, LONG_NLT, LANES))
    y_s[...] = y.reshape(LONG_T * LONG_NLT, LANES)

    def relayout_out(lt, carry):
        yt = y_s[pl.ds(lt, LONG_T, stride=LONG_NLT), :].T
        for i2 in range(2):
            for b in range(nb):
                row = i2 * (LANES // 2) + b * B_HEADS
                yt_ref[b, pl.ds(lt + LONG_NLT * i2, B_HEADS, stride=B_DH), :] = yt[row:row + B_HEADS, :]
        return carry

    lax.fori_loop(0, LONG_NLT, relayout_out, 0)


def rwkv_scan_long(rt, wt, kt, at, vt, S0, k_k, k_a, r_k, ln_w, ln_b):
    b, _, L = rt.shape
    assert b * B_HEADS * 2 == LANES and L % LONG_T == 0
    lane = jnp.arange(LANES)
    head = lane % B_HEADS
    chan = jnp.arange(B_DH)
    per_key = lambda p: p.reshape(B_HEADS, B_DH)[head[None, :], chan[:, None]]
    val = jnp.arange(LONG_NLT)[:, None] + LONG_NLT * (lane // (LANES // 2))[None, :]
    per_val = lambda p: p.reshape(B_HEADS, B_DH)[head[None, :], val]
    s0 = S0.reshape(b, B_HEADS, 2, LONG_NLT, B_DH).transpose(3, 4, 2, 0, 1).reshape(LONG_NLT * B_DH, LANES)
    blk = pl.BlockSpec((b, B_W, LONG_T), lambda c: (0, 0, c))
    const = lambda shape: pl.BlockSpec(shape, lambda c: (0,) * len(shape))
    big = pltpu.VMEM((LONG_HALF * B_DH, LANES), F32)
    small = pltpu.VMEM((LONG_T * LONG_NLT, LANES), F32)
    yt, s = pl.pallas_call(
        _rwkv_scan_long_kernel,
        grid=(L // LONG_T,),
        in_specs=[blk] * 5 + [const((B_DH, LANES))] * 3 + [const((LONG_NLT, LANES))] * 2
        + [const((LONG_NLT * B_DH, LANES))],
        out_specs=[blk, const((LONG_NLT * B_DH, LANES))],
        out_shape=[jax.ShapeDtypeStruct(rt.shape, F32), jax.ShapeDtypeStruct(s0.shape, F32)],
        scratch_shapes=[big] * 5 + [small, small, pltpu.VMEM((LONG_T, LANES), F32)],
        compiler_params=_cparams(("arbitrary",)),
        name="rwkv_scan_long",
    )(rt, wt, kt, at, vt, per_key(k_k), per_key(k_a), per_key(r_k), per_val(ln_w), per_val(ln_b), s0)
    s = s.reshape(LONG_NLT, B_DH, 2, b, B_HEADS).transpose(3, 4, 2, 0, 1).reshape(b, B_HEADS, B_DH, B_DH)
    return yt, s


def _rwkv_scan_short_kernel(r_ref, w_ref, k_ref, a_ref, v_ref, kkt_ref, kat_ref, rkt_ref, lnw_ref, lnb_ref, s0_ref,
                            y_ref, sout_ref, s_s, or_s, ow_s, ok_s, oa_s, av_s, ov_s, y_s, *, L):
    nt = 2 * B_DH
    for q in range(nt * B_DH // LANES):
        s_s[q * LANES:(q + 1) * LANES, :] = s0_ref[:, q * LANES:(q + 1) * LANES].T
    for ref, dst in ((r_ref, or_s), (w_ref, ow_s), (k_ref, ok_s), (a_ref, oa_s), (v_ref, ov_s)):
        for t in range(L):
            dst[t] = ref[pl.ds(t, LANES, stride=L), :].T.reshape(2, B_DH, LANES)
    shape3 = (L * 2, B_DH, LANES)
    tiles = lambda ref: jnp.concatenate([ref[...]] * L, axis=0)
    av, bv, k2, bonus = _rwkv_operands(ok_s[...].reshape(shape3), oa_s[...].reshape(shape3),
                                       or_s[...].reshape(shape3), tiles(kkt_ref), tiles(kat_ref), tiles(rkt_ref))
    av_s[...] = av.reshape(L, 2, B_DH, LANES)
    oa_s[...] = bv.reshape(L, 2, B_DH, LANES)
    ok_s[...] = k2.reshape(L, 2, B_DH, LANES)

    def step(t, carry):
        for h in range(2):
            y_s[t, h] = _rwkv_step(s_s, h * B_DH, B_DH, or_s[t, h], ow_s[t, h], ok_s[t, h], av_s[t, h],
                                   oa_s[t, h], ov_s[t, h])
        return carry

    lax.fori_loop(0, L, step, 0)

    y = y_s[...]
    d = y - jnp.mean(y, axis=2, keepdims=True)
    var = jnp.mean(d * d, axis=2, keepdims=True)
    y = (d * lax.rsqrt(var + B_GN_EPS) * lnw_ref[...] + lnb_ref[...]
         + bonus.reshape(L, 2, 1, LANES) * ov_s[...])
    for t in range(L):
        y_ref[pl.ds(t, LANES, stride=L), :] = y[t].reshape(nt, LANES).T
    for q in range(nt * B_DH // LANES):
        sout_ref[:, q * LANES:(q + 1) * LANES] = s_s[q * LANES:(q + 1) * LANES, :].T


def rwkv_scan_short(r, w, k, a, v, S0, L, k_k, k_a, r_k, ln_w, ln_b):
    n = r.shape[0]
    b = n // L
    assert b == LANES
    npair = B_HEADS // 2
    wide = lambda p: jnp.broadcast_to(p.reshape(npair, 2, B_DH, 1), (npair, 2, B_DH, LANES))
    s0 = S0.reshape(b, B_HEADS * B_DH * B_DH)
    blk = pl.BlockSpec((n, LANES), lambda p: (0, p))
    cblk = pl.BlockSpec((None, 2, B_DH, LANES), lambda p: (p, 0, 0, 0))
    sblk = pl.BlockSpec((b, 2 * B_DH * B_DH), lambda p: (0, p))
    op = pltpu.VMEM((L, 2, B_DH, LANES), F32)
    y, s = pl.pallas_call(
        functools.partial(_rwkv_scan_short_kernel, L=L),
        grid=(npair,),
        in_specs=[blk] * 5 + [cblk] * 5 + [sblk],
        out_specs=[blk, sblk],
        out_shape=[jax.ShapeDtypeStruct((n, B_W), F32), jax.ShapeDtypeStruct(s0.shape, F32)],
        scratch_shapes=[pltpu.VMEM((2 * B_DH * B_DH, LANES), F32)] + [op] * 7,
        compiler_params=_cparams(("arbitrary",)),
        name="rwkv_scan_short",
    )(r, w, k, a, v, wide(k_k), wide(k_a), wide(r_k), wide(ln_w), wide(ln_b), s0)
    return y, s.reshape(b, B_HEADS, B_DH, B_DH)


RWKV_PREP_ROWS = 256
RWKV_SHORT_ROWS = 64
B_LORA_OFF = 3 * B_W


def _rwkv_prep_kernel(z_ref, sh0_ref, mu_ref, wwa_ref, g2_ref, w0_ref, a0_ref,
                      r_ref, w_ref, k_ref, a_ref, v_ref, g_ref, sh_ref, *, T, nseq, channel_major):
    @pl.when(pl.program_id(1) == 0)
    def _():
        sh_ref[...] = sh0_ref[...]

    z = z_ref[...]
    rowid = lax.broadcasted_iota(jnp.int32, (z.shape[0], 1), 0)
    zprev = pltpu.roll(z, 1, axis=0)
    for u in range(nseq):
        zprev = jnp.where(rowid == u * T, sh_ref[u], zprev)
    for u in range(nseq):
        sh_ref[u] = z[(u + 1) * T - 1:(u + 1) * T, :]
    zs = z + (zprev - z) * mu_ref[...]
    r = zs[:, :B_W]
    k = zs[:, B_W:2 * B_W]
    lora = zs[:, B_LORA_OFF:B_LORA_OFF + LANES]
    lane = lax.broadcasted_iota(jnp.int32, lora.shape, 1)
    lora = jnp.where(lane < B_W_RANK, jnp.tanh(lora), lora).astype(BF16)
    wa = jnp.dot(lora, wwa_ref[...], preferred_element_type=F32)
    w_log = -jax.nn.softplus(-(w0_ref[...] + wa[:, :B_W])) - B_DECAY_OFFSET
    a = jax.nn.sigmoid(a0_ref[...] + wa[:, B_W:])
    zg = zs[:, B_LORA_OFF + LANES:B_LORA_OFF + LANES + B_G_RANK]
    lay = (lambda x: x.T) if channel_major else (lambda x: x)
    r_ref[...] = lay(r)
    w_ref[...] = lay(jnp.exp(-jnp.exp(w_log)))
    k_ref[...] = lay(k)
    a_ref[...] = lay(a)
    v_ref[...] = lay(zs[:, 2 * B_W:3 * B_W])
    g_ref[...] = jnp.dot(jax.nn.sigmoid(zg).astype(BF16), g2_ref[...], preferred_element_type=F32)


def rwkv_prep(zin, row0, b, L, gate_into, shift0, mu, w0, w2, a0, a2, g2):
    assert B_W_RANK + B_A_RANK == LANES
    T = math.gcd(L, RWKV_PREP_ROWS)
    channel_major = T == RWKV_PREP_ROWS
    nseq = 1 if channel_major else RWKV_SHORT_ROWS // T
    assert nseq == 1 or (L == T and b % nseq == 0)
    R = T * nseq
    nchunk = L // T
    blk0 = row0 // R
    wwa = jnp.zeros((LANES, 2 * B_W), F32).at[:B_W_RANK, :B_W].set(w2).at[B_W_RANK:, B_W:].set(a2).astype(BF16)
    row = lambda v: v.reshape(1, -1)
    if channel_major:
        blk = pl.BlockSpec((None, B_W, R), lambda i, c: (i, 0, c))
        oshape = jax.ShapeDtypeStruct((b, B_W, L), F32)
    else:
        blk = pl.BlockSpec((R, B_W), lambda i, c: (i * nchunk + c, 0))
        oshape = jax.ShapeDtypeStruct((b * L, B_W), F32)
    gblk = pl.BlockSpec((R, B_W), lambda i, c: (blk0 + i * nchunk + c, 0))
    shspec = pl.BlockSpec((nseq, 1, B_COLS), lambda i, c: (i, 0, 0))
    const = lambda shape: pl.BlockSpec(shape, lambda i, c: (0,) * len(shape))
    in_specs = [pl.BlockSpec((R, B_COLS), lambda i, c: (blk0 + i * nchunk + c, 0)), shspec, const((1, B_COLS)),
                const((LANES, 2 * B_W)), const((B_G_RANK, B_W)), const((1, B_W)), const((1, B_W))]
    body, xspecs, xargs, aliases = _fill_into(
        functools.partial(_rwkv_prep_kernel, T=T, nseq=nseq, channel_major=channel_major),
        len(in_specs), gate_into, 5)
    outs = pl.pallas_call(
        body,
        grid=(b // nseq, nchunk),
        in_specs=in_specs + xspecs,
        out_specs=[blk] * 5 + [gblk, shspec],
        out_shape=[oshape] * 5
        + [jax.ShapeDtypeStruct((zin.shape[0], B_W), F32), jax.ShapeDtypeStruct((b, 1, B_COLS), F32)],
        input_output_aliases=aliases,
        compiler_params=_cparams(("arbitrary", "arbitrary")),
        name="rwkv_prep",
    )(zin, shift0.reshape(b, 1, B_COLS), row(mu), wwa, g2.astype(BF16), row(w0), row(a0), *xargs)
    return outs[:5], outs[5], outs[6].reshape(b, B_COLS)


MLSTM_ROWS = 64
HIGHEST = lax.Precision.HIGHEST


def _mlstm_kernel(q_ref, k_ref, v_ref, o_ref, g_ref, gt_ref, brow_ref, bcol_ref, nw_ref,
                  c0_ref, n0_ref, m0_ref, h_ref, c_ref, n_ref, m_ref, *, T, nseq):
    R = MLSTM_ROWS

    @pl.when(pl.program_id(1) == 0)
    def _():
        c_ref[...] = c0_ref[...]
        n_ref[...] = n0_ref[...]
        m_ref[...] = m0_ref[...]

    shift = T.bit_length() - 1
    ri = lax.broadcasted_iota(jnp.int32, (R, R), 0)
    ci = lax.broadcasted_iota(jnp.int32, (R, R), 1)
    mask = (ci <= ri) & (jnp.right_shift(ri, shift) == jnp.right_shift(ci, shift))
    lmat = mask.astype(F32)
    rowid = lax.broadcasted_iota(jnp.int32, (R, 1), 0)
    rsel = [(rowid >= u * T) & (rowid < (u + 1) * T) for u in range(nseq)]

    g = g_ref[...] + brow_ref[...]
    lane = lax.broadcasted_iota(jnp.int32, g.shape, 1)
    glog = jnp.where((lane >= A_HEADS) & (lane < 2 * A_HEADS), jax.nn.log_sigmoid(g), g)
    gt = gt_ref[...] + bcol_ref[...]
    sub = lax.broadcasted_iota(jnp.int32, gt.shape, 0)
    gtlog = jnp.where(sub >= A_HEADS, jax.nn.log_sigmoid(gt), gt)
    bc_col = jnp.dot(lmat, glog, precision=HIGHEST, preferred_element_type=F32)
    bc_row = lax.dot_general(gtlog, lmat, (((1,), (1,)), ((), ())), precision=HIGHEST,
                             preferred_element_type=F32)
    lane_m = lax.broadcasted_iota(jnp.int32, (1, LANES), 1)
    m_old = [m_ref[u] for u in range(nseq)]
    m_out = [jnp.zeros((1, LANES), F32) for _ in range(nseq)]

    for h in range(A_HEADS):
        hs = slice(h * A_DK, (h + 1) * A_DK)
        bcc = bc_col[:, A_HEADS + h:A_HEADS + h + 1]
        bcr = bc_row[A_HEADS + h:A_HEADS + h + 1, :]
        lir = gtlog[h:h + 1, :]
        lic = glog[:, h:h + 1]
        m_u = [m_old[u][:, h:h + 1] for u in range(nseq)]
        m_col = m_u[0]
        for u in range(1, nseq):
            m_col = jnp.where(rsel[u], m_u[u], m_col)
        dmat = jnp.where(mask, bcc - bcr + lir, -jnp.inf)
        inter = bcc + m_col
        mt = jnp.maximum(inter, jnp.max(dmat, axis=1, keepdims=True))
        p = jnp.exp(dmat - mt)
        qh = q_ref[:, hs] * (A_DK ** -0.5)
        kh = k_ref[:, hs]
        qb, kb, vb = qh.astype(BF16), kh.astype(BF16), v_ref[:, hs].astype(BF16)
        wq = lax.dot_general(qb, kb, (((1,), (1,)), ((), ())), preferred_element_type=F32) * p
        wi = jnp.exp(inter - mt)
        c_old = [c_ref[u, h] for u in range(nseq)]
        n_old = [n_ref[u, h:h + 1, :] for u in range(nseq)]
        qc = jnp.dot(qb, c_old[0].astype(BF16), preferred_element_type=F32)
        qn = jnp.sum(qh * n_old[0], axis=1, keepdims=True)
        for u in range(1, nseq):
            qc = jnp.where(rsel[u], jnp.dot(qb, c_old[u].astype(BF16), preferred_element_type=F32), qc)
            qn = jnp.where(rsel[u], jnp.sum(qh * n_old[u], axis=1, keepdims=True), qn)
        num = jnp.dot(wq.astype(BF16), vb, preferred_element_type=F32) + wi * qc
        den = jnp.sum(wq, axis=1, keepdims=True) + wi * qn
        hh = num / jnp.maximum(jnp.abs(den), jnp.exp(-mt))
        hh = hh * lax.rsqrt(jnp.mean(hh * hh, axis=-1, keepdims=True) + EPS)
        h_ref[:, hs] = hh * nw_ref[:, hs] * jax.nn.sigmoid(o_ref[:, hs])
        for u in range(nseq):
            b_last = bcc[(u + 1) * T - 1:(u + 1) * T, :]
            gs = b_last - bcc + lic
            gmax = jnp.max(gs if nseq == 1 else jnp.where(rsel[u], gs, -jnp.inf), axis=0, keepdims=True)
            m_new = jnp.maximum(b_last + m_u[u], gmax)
            decay = jnp.exp(b_last + m_u[u] - m_new)
            ws = jnp.exp(gs - m_new)
            if nseq > 1:
                ws = jnp.where(rsel[u], ws, 0.0)
            kw = kh * ws
            c_ref[u, h] = decay * c_old[u] + lax.dot_general(
                kw.astype(BF16), vb, (((0,), (0,)), ((), ())), preferred_element_type=F32)
            n_ref[u, h:h + 1, :] = decay * n_old[u] + jnp.sum(kw, axis=0, keepdims=True)
            m_out[u] = jnp.where(lane_m == h, m_new, m_out[u])
    for u in range(nseq):
        m_ref[u] = m_out[u]


def mlstm(zin, gates_t, row0, b, L, h_into, b_i, b_f, m_norm, C0, n0, m0):
    R = MLSTM_ROWS
    T = math.gcd(L, R)
    nseq = R // T
    assert nseq == 1 or (L == T and b % nseq == 0)
    nchunk = L // T
    blk0 = row0 // R
    bias = jnp.concatenate([b_i, b_f])
    bias_row = jnp.zeros((1, LANES), F32).at[0, :2 * A_HEADS].set(bias)
    bias_col = bias.reshape(2 * A_HEADS, 1)
    m0p = jnp.zeros((b, 1, LANES), F32).at[:, 0, :A_HEADS].set(m0)
    blk = lambda i, c: blk0 + i * nchunk + c
    rowblk = lambda col: pl.BlockSpec((R, A_QK), lambda i, c: (blk(i, c), EVEN_A_OFF // A_QK + col))
    cspec = pl.BlockSpec((nseq, A_HEADS, A_DK, A_DV), lambda i, c: (i, 0, 0, 0))
    nspec = pl.BlockSpec((nseq, A_HEADS, A_DK), lambda i, c: (i, 0, 0))
    mspec = pl.BlockSpec((nseq, 1, LANES), lambda i, c: (i, 0, 0))
    const = lambda shape: pl.BlockSpec(shape, lambda i, c: (0,) * len(shape))
    in_specs = [rowblk(0), rowblk(1), rowblk(2), rowblk(3),
                pl.BlockSpec((R, LANES), lambda i, c: (blk(i, c), EVEN_G_OFF // LANES)),
                pl.BlockSpec((None, 2 * A_HEADS, R), lambda i, c: (blk(i, c), 0, 0)),
                const((1, LANES)), const((2 * A_HEADS, 1)), const((1, A_V)),
                cspec, nspec, mspec]
    body, xspecs, xargs, aliases = _fill_into(
        functools.partial(_mlstm_kernel, T=T, nseq=nseq), len(in_specs), h_into, 0)
    h, C, n, m = pl.pallas_call(
        body,
        grid=(b // nseq, nchunk),
        in_specs=in_specs + xspecs,
        out_specs=[pl.BlockSpec((R, A_V), lambda i, c: (blk(i, c), 0)), cspec, nspec, mspec],
        out_shape=[jax.ShapeDtypeStruct((zin.shape[0], A_V), F32),
                   jax.ShapeDtypeStruct(C0.shape, F32), jax.ShapeDtypeStruct(n0.shape, F32),
                   jax.ShapeDtypeStruct(m0p.shape, F32)],
        input_output_aliases=aliases,
        compiler_params=_cparams(("arbitrary", "arbitrary")),
        name="mlstm",
    )(zin, zin, zin, zin, zin, gates_t, bias_row, bias_col, m_norm.reshape(1, A_V), C0, n0, m0p, *xargs)
    return h, C, n, m[:, 0, :A_HEADS]


def even_mixer(zin, gates_t, row0, b, L, into, C0, n0, m0, S0, shift0, b_i, b_f, m_norm, mu, w0, w2, a0, a2, g2,
               k_k, k_a, r_k, ln_w, ln_b):
    hA, C, n, m = mlstm(zin, gates_t, row0, b, L, into[0], b_i, b_f, m_norm, C0, n0, m0)
    ops, g, shift = rwkv_prep(zin, row0, b, L, into[1], shift0, mu, w0, w2, a0, a2, g2)
    if ops[0].ndim == 3:
        hB, S = rwkv_scan_long(*ops, S0, k_k, k_a, r_k, ln_w, ln_b)
    else:
        hB, S = rwkv_scan_short(*ops, S0, L, k_k, k_a, r_k, ln_w, ln_b)
    return hA, hB, g, (C, n, m, S, shift)


SUBLANES = 8
C_PAIRS = C_HEADS // 2
C_GROUP_W = C_INNER // C_GROUPS
C_BC_W = 2 * C_GROUPS * C_STATE
SSD_SHORT_ROWS = 32


def _ssd_kernel(z_ref, x_ref, bc_ref, dt_ref, dtt_ref, cw_ref, cb_ref, dtb_ref, dtbt_ref, al_ref, alt_ref,
                dsk_ref, nw_ref, s0_ref, cv0_ref, y_ref, s_ref, cv_ref, *, T, nseq):
    R = T * nseq

    @pl.when(pl.program_id(1) == 0)
    def _():
        s_ref[...] = s0_ref[...]
        cv_ref[...] = cv0_ref[...]

    shift = T.bit_length() - 1
    ri = lax.broadcasted_iota(jnp.int32, (R, R), 0)
    ci = lax.broadcasted_iota(jnp.int32, (R, R), 1)
    mask = (ci <= ri) & (jnp.right_shift(ri, shift) == jnp.right_shift(ci, shift))
    lmat = mask.astype(F32)
    rowid = lax.broadcasted_iota(jnp.int32, (R, 1), 0)
    rsel = [(rowid >= u * T) & (rowid < (u + 1) * T) for u in range(nseq)]

    def conv_silu(x, cols):
        acc = cb_ref[:, cols] + x * cw_ref[C_CONV - 1:C_CONV, cols]
        for d in range(1, C_CONV):
            xd = pltpu.roll(x, d, axis=0)
            for u in range(nseq):
                for t in range(d):
                    src = SUBLANES - d + t
                    xd = jnp.where(rowid == u * T + t, cv_ref[u, src:src + 1, cols], xd)
            acc = acc + xd * cw_ref[C_CONV - 1 - d:C_CONV - d, cols]
        for u in range(nseq):
            cv_ref[u, :, cols] = x[(u + 1) * T - SUBLANES:(u + 1) * T, :]
        return acc * jax.nn.sigmoid(acc)

    xc = conv_silu(x_ref[...], slice(0, C_INNER))
    bcc_ = conv_silu(bc_ref[...], slice(C_INNER, C_CONV_DIM))

    dtv = jax.nn.softplus(dt_ref[...] + dtb_ref[...])
    dtt = jax.nn.softplus(dtt_ref[...] + dtbt_ref[...])
    cum_col = jnp.dot(lmat, dtv * (-jnp.exp(al_ref[...])), precision=HIGHEST, preferred_element_type=F32)
    cum_row = lax.dot_general(dtt * (-jnp.exp(alt_ref[...])), lmat, (((1,), (1,)), ((), ())),
                              precision=HIGHEST, preferred_element_type=F32)

    lo = lax.broadcasted_iota(jnp.int32, (R, LANES), 1) < C_HEADDIM
    rlo = lax.broadcasted_iota(jnp.int32, (LANES, 1), 0) < C_HEADDIM
    nt = (((1,), (1,)), ((), ()))
    tn = (((0,), (0,)), ((), ()))
    pairs_per_group = C_PAIRS // C_GROUPS
    for g in range(C_GROUPS):
        bg = bcc_[:, g * C_STATE:(g + 1) * C_STATE].astype(BF16)
        cg = bcc_[:, (C_GROUPS + g) * C_STATE:(C_GROUPS + g + 1) * C_STATE].astype(BF16)
        cbm = lax.dot_general(cg, bg, nt, preferred_element_type=F32)
        ys = None
        for u in range(nseq):
            sg = s_ref[u, g * pairs_per_group:(g + 1) * pairs_per_group].reshape(C_GROUP_W, C_STATE)
            t_u = lax.dot_general(cg, sg.astype(BF16), nt, preferred_element_type=F32)
            ys = t_u if u == 0 else jnp.where(rsel[u], t_u, ys)
        for q in range(pairs_per_group):
            pr = g * pairs_per_group + q
            ps = slice(pr * LANES, (pr + 1) * LANES)
            xp = xc[:, ps]
            cc = [cum_col[:, 2 * pr + e:2 * pr + e + 1] for e in range(2)]
            intra = None
            for e, keep in ((0, lo), (1, jnp.logical_not(lo))):
                hh = 2 * pr + e
                seg = jnp.exp(jnp.where(mask, cc[e] - cum_row[hh:hh + 1, :], -jnp.inf))
                mix = cbm * seg * dtt[hh:hh + 1, :]
                part = jnp.dot(mix.astype(BF16), jnp.where(keep, xp, 0.0).astype(BF16),
                               preferred_element_type=F32)
                intra = part if intra is None else intra + part
            scale = jnp.where(lo, jnp.exp(cc[0]), jnp.exp(cc[1]))
            yp = intra + scale * ys[:, q * LANES:(q + 1) * LANES] + dsk_ref[:, ps] * xp
            zp = z_ref[:, ps]
            y_ref[:, ps] = yp * (zp * jax.nn.sigmoid(zp))
            for u in range(nseq):
                last = (u + 1) * T - 1
                ct = [cc[e][last:last + 1, :] for e in range(2)]
                tail = jnp.where(lo, jnp.exp(ct[0] - cc[0]) * dtv[:, 2 * pr:2 * pr + 1],
                                 jnp.exp(ct[1] - cc[1]) * dtv[:, 2 * pr + 1:2 * pr + 2])
                xw = xp * tail
                if nseq > 1:
                    xw = jnp.where(rsel[u], xw, 0.0)
                upd = lax.dot_general(xw.astype(BF16), bg, tn, preferred_element_type=F32)
                dec = jnp.where(rlo, jnp.exp(ct[0]), jnp.exp(ct[1]))
                s_ref[u, pr] = dec * s_ref[u, pr] + upd

    for g in range(C_GROUPS):
        gs_ = slice(g * C_GROUP_W, (g + 1) * C_GROUP_W)
        yg = y_ref[:, gs_]
        y_ref[:, gs_] = yg * lax.rsqrt(jnp.mean(yg * yg, axis=-1, keepdims=True) + EPS) * nw_ref[:, gs_]


def ssd_mixer(zin, row0, b, L, y_into, ssm0, conv0, conv_w, conv_b, dt_bias, a_log, d_skip, norm_w):
    T = math.gcd(L, C_CHUNK)
    nseq = 1 if T == C_CHUNK else SSD_SHORT_ROWS // T
    assert nseq == 1 or (L == T and b % nseq == 0)
    R = T * nseq
    nchunk = L // T
    nblk = b * L // R
    blk0 = row0 // R
    dt_t = zin[row0:row0 + b * L, ODD_DT_OFF:ODD_DT_OFF + C_HEADS].reshape(nblk, R, C_HEADS).transpose(0, 2, 1)
    pad_row = lambda v: jnp.zeros((1, LANES), F32).at[0, :C_HEADS].set(v)
    s0 = ssm0.reshape(b, C_PAIRS, 2 * C_HEADDIM, C_STATE)
    cv0 = jnp.concatenate([jnp.zeros((b, SUBLANES - (C_CONV - 1), C_CONV_DIM), F32), conv0], axis=1)
    blk = lambda i, c: blk0 + i * nchunk + c
    sspec = pl.BlockSpec((nseq, C_PAIRS, 2 * C_HEADDIM, C_STATE), lambda i, c: (i, 0, 0, 0))
    cvspec = pl.BlockSpec((nseq, SUBLANES, C_CONV_DIM), lambda i, c: (i, 0, 0))
    const = lambda shape: pl.BlockSpec(shape, lambda i, c: (0,) * len(shape))
    in_specs = [pl.BlockSpec((R, C_INNER), lambda i, c: (blk(i, c), 0)),
                pl.BlockSpec((R, C_INNER), lambda i, c: (blk(i, c), 1)),
                pl.BlockSpec((R, C_BC_W), lambda i, c: (blk(i, c), 2 * C_INNER // C_BC_W)),
                pl.BlockSpec((R, LANES), lambda i, c: (blk(i, c), ODD_DT_OFF // LANES)),
                pl.BlockSpec((None, C_HEADS, R), lambda i, c: (i * nchunk + c, 0, 0)),
                const((C_CONV, C_CONV_DIM)), const((1, C_CONV_DIM)),
                const((1, LANES)), const((C_HEADS, 1)), const((1, LANES)), const((C_HEADS, 1)),
                const((1, C_INNER)), const((1, C_INNER)), sspec, cvspec]
    body, xspecs, xargs, aliases = _fill_into(
        functools.partial(_ssd_kernel, T=T, nseq=nseq), len(in_specs), y_into, 0)
    y, s, cv = pl.pallas_call(
        body,
        grid=(b // nseq, nchunk),
        in_specs=in_specs + xspecs,
        out_specs=[pl.BlockSpec((R, C_INNER), lambda i, c: (blk(i, c), 0)), sspec, cvspec],
        out_shape=[jax.ShapeDtypeStruct((zin.shape[0], C_INNER), F32),
                   jax.ShapeDtypeStruct(s0.shape, F32), jax.ShapeDtypeStruct(cv0.shape, F32)],
        input_output_aliases=aliases,
        compiler_params=_cparams(("arbitrary", "arbitrary")),
        name="ssd",
    )(zin, zin, zin, zin, dt_t, conv_w, conv_b.reshape(1, C_CONV_DIM),
      pad_row(dt_bias), dt_bias.reshape(C_HEADS, 1), pad_row(a_log), a_log.reshape(C_HEADS, 1),
      jnp.repeat(d_skip, C_HEADDIM).reshape(1, C_INNER), norm_w.reshape(1, C_INNER), s0, cv0, *xargs)
    return y, (s.reshape(b, C_HEADS, C_HEADDIM, C_STATE), cv[:, SUBLANES - (C_CONV - 1):])


def _even_w_in(w):
    qkvo = w[:, :2 * A_QK + 2 * A_V]
    gates = w[:, 2 * A_QK + 2 * A_V:A_COLS]
    rwkv = w[:, A_COLS:]
    pad = jnp.zeros((D_MODEL, EVEN_GATE_PAD - 2 * A_HEADS), w.dtype)
    return jnp.concatenate([rwkv, gates, pad, qkvo], axis=1).astype(BF16)


def _odd_w_in(w):
    pad = jnp.zeros((D_MODEL, ODD_DT_PAD - C_HEADS), w.dtype)
    return jnp.concatenate([w, pad], axis=1).astype(BF16)


def kernel(x_prompt, x_sample, state_mlstm_C, state_mlstm_n, state_mlstm_m, state_rwkv_S,
           state_rwkv_shift, state_ssm, state_conv, p_prompt, p_sample,
           norm_mix, norm_ffn, w_ffn_up, w_ffn_down, w_ple_proj, norm_ple, w_ple_gate, norm_final,
           w_in_even, mlstm_b_i, mlstm_b_f, mlstm_norm, rwkv_mu, rwkv_w0, rwkv_w2, rwkv_a0, rwkv_a2,
           rwkv_g2, rwkv_k_k, rwkv_k_a, rwkv_r_k, rwkv_ln_w, rwkv_ln_b, w_out_even,
           w_in_odd, conv_w, conv_b, dt_bias, a_log, d_skip, ssm_norm, w_out_odd):
    bp, Lp, _ = x_prompt.shape
    bs, Ls, _ = x_sample.shape
    n_p = bp * Lp
    x = jnp.concatenate([x_prompt.reshape(n_p, D_MODEL), x_sample.reshape(bs * Ls, D_MODEL)], axis=0)
    p_all = jnp.concatenate([p_prompt.reshape(DEPTH, n_p, PLE_DIM),
                             p_sample.reshape(DEPTH, bs * Ls, PLE_DIM)], axis=1)

    even_small = (mlstm_b_i, mlstm_b_f, mlstm_norm, rwkv_mu, rwkv_w0, rwkv_w2, rwkv_a0, rwkv_a2,
                  rwkv_g2, rwkv_k_k, rwkv_k_a, rwkv_r_k, rwkv_ln_w, rwkv_ln_b)
    odd_small = (conv_w, conv_b, dt_bias, a_log, d_skip, ssm_norm)

    zeros_even = (jnp.zeros((bp, A_HEADS, A_DK, A_DV), F32), jnp.zeros((bp, A_HEADS, A_DK), F32),
                  jnp.zeros((bp, A_HEADS), F32), jnp.zeros((bp, B_HEADS, B_DH, B_DH), F32),
                  jnp.zeros((bp, B_COLS), F32))
    zeros_odd = (jnp.zeros((bp, C_HEADS, C_HEADDIM, C_STATE), F32),
                 jnp.zeros((bp, C_CONV - 1, C_CONV_DIM), F32))

    st_p_even, st_s_even, st_p_odd, st_s_odd = [], [], [], []
    for i in range(DEPTH):
        j = i // 2
        if i % 2 == 0:
            zin = norm_matmul(x, norm_mix[i], _even_w_in(w_in_even[j]), 512)
            small = [t[j] for t in even_small]
            gates_t = zin[:, EVEN_G_OFF:EVEN_G_OFF + 2 * A_HEADS].reshape(
                -1, MLSTM_ROWS, 2 * A_HEADS).transpose(0, 2, 1)
            ha, hb_p, g, sp = even_mixer(zin, gates_t, 0, bp, Lp, (None, None), *zeros_even, *small)
            ha, hb_s, g, ss = even_mixer(zin, gates_t, n_p, bs, Ls, (ha, g), state_mlstm_C[j], state_mlstm_n[j],
                                         state_mlstm_m[j], state_rwkv_S[j], state_rwkv_shift[j], *small)
            st_p_even.append(sp)
            st_s_even.append(ss)
            wo = w_out_even[j].astype(BF16)
            xo = matmul_res([(ha, None, wo[:A_V], None), (hb_p, g, wo[A_V:], Lp)], x, 0, n_p)
            x = matmul_res([(ha, None, wo[:A_V], None), (hb_s, g, wo[A_V:], 0)], x, n_p, bs * Ls, xo)
        else:
            zin = norm_matmul(x, norm_mix[i], _odd_w_in(w_in_odd[j]), 768)
            small = [t[j] for t in odd_small]
            mix, sp = ssd_mixer(zin, 0, bp, Lp, None, *zeros_odd, *small)
            mix, ss = ssd_mixer(zin, n_p, bs, Ls, mix, state_ssm[j], state_conv[j], *small)
            st_p_odd.append(sp)
            st_s_odd.append(ss)
            x = matmul_res([(mix, None, w_out_odd[j].astype(BF16), None)], x)
        x = ffn(x, norm_ffn[i], w_ffn_up[i].astype(BF16), w_ffn_down[i].astype(BF16))
        x = ple(x, norm_ple[i], w_ple_gate[i].astype(BF16), p_all[i], w_ple_proj[i].astype(BF16))
    y = final_norm(x, norm_final)
    y_prompt = y[:n_p].reshape(bp, Lp, D_MODEL)
    y_sample = y[n_p:].reshape(bs, Ls, D_MODEL)
    stack = lambda sts, idx: jnp.stack([s[idx] for s in sts])
    return (y_prompt, y_sample,
            stack(st_p_even, 0), stack(st_p_even, 1), stack(st_p_even, 2), stack(st_p_even, 3),
            stack(st_p_even, 4), stack(st_p_odd, 0), stack(st_p_odd, 1),
            stack(st_s_even, 0), stack(st_s_even, 1), stack(st_s_even, 2), stack(st_s_even, 3),
            stack(st_s_even, 4), stack(st_s_odd, 0), stack(st_s_odd, 1))
```

```python
import math
import functools
import jax
import jax.numpy as jnp
from jax import lax
from jax.experimental import pallas as pl
from jax.experimental.pallas import tpu as pltpu

D_MODEL = 1024
DEPTH = 4
F32 = jnp.float32
BF16 = jnp.bfloat16
EPS = 1e-6
N_EVEN = (DEPTH + 1) // 2
N_ODD = DEPTH // 2
D_FF = 4 * D_MODEL
PLE_DIM = 256

A_HEADS = 4
A_DK = D_MODEL // 8
A_DV = D_MODEL // 8
A_CHUNK = 64
A_QK = A_HEADS * A_DK
A_V = A_HEADS * A_DV
A_COLS = 2 * A_QK + 2 * A_V + 2 * A_HEADS

B_HEADS = 8
B_DH = 64
B_W = B_HEADS * B_DH
B_W_RANK = 64
B_A_RANK = 64
B_G_RANK = 128
B_COLS = 3 * B_W + B_W_RANK + B_A_RANK + B_G_RANK
B_DECAY_OFFSET = 0.5
B_GN_EPS = 64e-5

EVEN_COLS = A_COLS + B_COLS
EVEN_OUT = A_V + B_W

C_INNER = 2 * D_MODEL
C_HEADDIM = 64
C_HEADS = C_INNER // C_HEADDIM
C_GROUPS = 4
C_HPG = C_HEADS // C_GROUPS
C_STATE = 128
C_CONV = 4
C_CHUNK = 128
C_CONV_DIM = C_INNER + 2 * C_GROUPS * C_STATE
ODD_COLS = C_INNER + C_CONV_DIM + C_HEADS

LANES = 128
SUBLANES = 8
VMEM_LIMIT = 56 * 1024 * 1024
TOKEN_TILE = 256

EVEN_GATE_PAD = 256
EVEN_N = B_COLS + EVEN_GATE_PAD + 2 * A_QK + 2 * A_V
EVEN_G_OFF = B_COLS
EVEN_A_OFF = B_COLS + EVEN_GATE_PAD
ODD_DT_PAD = 256
ODD_N = C_INNER + C_CONV_DIM + ODD_DT_PAD
ODD_DT_OFF = C_INNER + C_CONV_DIM


def _cparams(sem):
    return pltpu.CompilerParams(dimension_semantics=sem, vmem_limit_bytes=VMEM_LIMIT)


def _rms(x, g):
    return x * lax.rsqrt(jnp.mean(x * x, axis=-1, keepdims=True) + EPS) * g


def _resident(shape):
    nd = len(shape)
    return pl.BlockSpec(shape, lambda *_: (0,) * nd, pipeline_mode=pl.Buffered(1))


def _rows(width):
    return pl.BlockSpec((TOKEN_TILE, width), lambda i: (i, 0))


def _fill_into(body, n_in, into, out_idx):
    if into is None:
        return body, [], [], {}

    def skipping(*refs):
        return body(*refs[:n_in], *refs[n_in + 1:])

    return skipping, [pl.BlockSpec(memory_space=pl.ANY)], [into], {n_in: out_idx}


def _norm_matmul_kernel(x_ref, g_ref, w_ref, o_ref, *, tn):
    xn = _rms(x_ref[...], g_ref[...]).astype(BF16)
    for n0 in range(0, w_ref.shape[1], tn):
        o_ref[:, n0:n0 + tn] = jnp.dot(xn, w_ref[:, n0:n0 + tn], preferred_element_type=F32)


def norm_matmul(x, g, w, tn):
    m, n = x.shape[0], w.shape[1]
    return pl.pallas_call(
        functools.partial(_norm_matmul_kernel, tn=tn),
        grid=(m // TOKEN_TILE,),
        in_specs=[_rows(D_MODEL), _resident((1, D_MODEL)), _resident(w.shape)],
        out_specs=_rows(n),
        out_shape=jax.ShapeDtypeStruct((m, n), F32),
        compiler_params=_cparams(("arbitrary",)),
        name="norm_matmul",
    )(x, g.reshape(1, D_MODEL), w)


def _matmul_res_kernel(*refs, kinds):
    x_ref, o_ref = refs[-2], refs[-1]
    acc = x_ref[...]
    pos = 0
    for has_gate, channel_major in kinds:
        a = refs[pos][...]
        if channel_major:
            a = a.T
        if has_gate:
            a = a * refs[pos + 1][...]
        w_ref = refs[pos + 1 + has_gate]
        pos += 2 + has_gate
        acc = acc + jnp.dot(a.astype(BF16), w_ref[...], preferred_element_type=F32)
    o_ref[...] = acc


def matmul_res(terms, x, row0=0, nrows=None, into=None):
    m = x.shape[0]
    nrows = m if nrows is None else nrows
    t0 = row0 // TOKEN_TILE
    rows = lambda width: pl.BlockSpec((TOKEN_TILE, width), lambda i: (t0 + i, 0))
    specs, args, kinds = [], [], []
    for a, gate, w, seq_len in terms:
        if seq_len is None:
            specs.append(rows(a.shape[1]))
        elif seq_len == 0:
            specs.append(pl.BlockSpec((TOKEN_TILE, a.shape[1]), lambda i: (i, 0)))
        else:
            per_seq = seq_len // TOKEN_TILE
            specs.append(pl.BlockSpec((None, a.shape[1], TOKEN_TILE), lambda i: (i // per_seq, 0, i % per_seq)))
        specs += ([rows(w.shape[0])] if gate is not None else []) + [_resident(w.shape)]
        args += [a] + ([gate] if gate is not None else []) + [w]
        kinds.append((int(gate is not None), bool(seq_len)))
    specs.append(rows(D_MODEL))
    body, xspecs, xargs, aliases = _fill_into(
        functools.partial(_matmul_res_kernel, kinds=tuple(kinds)), len(specs), into, 0)
    return pl.pallas_call(
        body,
        grid=(nrows // TOKEN_TILE,),
        in_specs=specs + xspecs,
        out_specs=rows(D_MODEL),
        out_shape=jax.ShapeDtypeStruct((m, D_MODEL), F32),
        input_output_aliases=aliases,
        compiler_params=_cparams(("arbitrary",)),
        name="matmul_res",
    )(*args, x, *xargs)


FFN_CHUNK = 512


def _ffn_kernel(x_ref, g_ref, wu_ref, wd_ref, o_ref):
    x = x_ref[...]
    xn = _rms(x, g_ref[...]).astype(BF16)
    acc = x
    for c0 in range(0, D_FF, FFN_CHUNK):
        h = jnp.dot(xn, wu_ref[:, c0:c0 + FFN_CHUNK], preferred_element_type=F32)
        h = jnp.square(jnp.maximum(h, 0.0)).astype(BF16)
        acc = acc + jnp.dot(h, wd_ref[c0:c0 + FFN_CHUNK, :], preferred_element_type=F32)
    o_ref[...] = acc


def ffn(x, g, wu, wd):
    m = x.shape[0]
    return pl.pallas_call(
        _ffn_kernel,
        grid=(m // TOKEN_TILE,),
        in_specs=[_rows(D_MODEL), _resident((1, D_MODEL)), _resident(wu.shape), _resident(wd.shape)],
        out_specs=_rows(D_MODEL),
        out_shape=jax.ShapeDtypeStruct((m, D_MODEL), F32),
        compiler_params=_cparams(("arbitrary",)),
        name="ffn",
    )(x, g.reshape(1, D_MODEL), wu, wd)


def _ple_kernel(x_ref, g_ref, wg_ref, p_ref, wp_ref, o_ref):
    x = x_ref[...]
    xn = _rms(x, g_ref[...]).astype(BF16)
    gate = jax.nn.sigmoid(jnp.dot(xn, wg_ref[...], preferred_element_type=F32))
    proj = jnp.dot(p_ref[...].astype(BF16), wp_ref[...], preferred_element_type=F32)
    o_ref[...] = x + proj * gate


def ple(x, g, wg, p, wp):
    m = x.shape[0]
    return pl.pallas_call(
        _ple_kernel,
        grid=(m // TOKEN_TILE,),
        in_specs=[_rows(D_MODEL), _resident((1, D_MODEL)), _resident(wg.shape),
                  _rows(PLE_DIM), _resident(wp.shape)],
        out_specs=_rows(D_MODEL),
        out_shape=jax.ShapeDtypeStruct((m, D_MODEL), F32),
        compiler_params=_cparams(("arbitrary",)),
        name="ple",
    )(x, g.reshape(1, D_MODEL), wg, p, wp)


def _final_norm_kernel(x_ref, g_ref, o_ref):
    o_ref[...] = _rms(x_ref[...], g_ref[...])


def final_norm(x, g):
    m = x.shape[0]
    return pl.pallas_call(
        _final_norm_kernel,
        grid=(m // TOKEN_TILE,),
        in_specs=[_rows(D_MODEL), _resident((1, D_MODEL))],
        out_specs=_rows(D_MODEL),
        out_shape=jax.ShapeDtypeStruct((m, D_MODEL), F32),
        compiler_params=_cparams(("arbitrary",)),
        name="final_norm",
    )(x, g.reshape(1, D_MODEL))


def _rwkv_operands(k, a, r, kk_t, ka_t, rk_t):
    kk = k * kk_t
    kk = kk * lax.rsqrt(jnp.maximum(jnp.sum(kk * kk, axis=1, keepdims=True), 1e-24))
    k2 = k * (1.0 + (a - 1.0) * ka_t)
    return -kk, kk * a, k2, jnp.sum(r * k2 * rk_t, axis=1)


def _rwkv_step(s_ref, tile0, ntiles, r, w, k, a, b, vrows):
    ys = []
    for lt in range(ntiles):
        rows = slice((tile0 + lt) * B_DH, (tile0 + lt + 1) * B_DH)
        s = s_ref[rows, :]
        sa = jnp.sum(s * a, axis=0, keepdims=True)
        s = s * w + sa * b + vrows[lt:lt + 1] * k
        s_ref[rows, :] = s
        ys.append(jnp.sum(s * r, axis=0, keepdims=True))
    return jnp.concatenate(ys, axis=0)


LONG_T = 128
LONG_NLT = B_DH // 2
KEY_PITCH = B_DH + SUBLANES
VAL_PITCH = LONG_NLT + SUBLANES


def _rwkv_scan_long_kernel(r_ref, w_ref, k_ref, a_ref, v_ref, kkt_ref, kat_ref, rkt_ref, lnw_ref, lnb_ref, s0_ref,
                           yt_ref, s_ref, or_s, ow_s, ok_s, oa_s, av_s, v_s, y_s):
    nb = yt_ref.shape[0]

    @pl.when(pl.program_id(0) == 0)
    def _():
        s_ref[...] = s0_ref[...]
        for scr in (or_s, ow_s, ok_s, oa_s, av_s, v_s, y_s):
            scr[...] = jnp.zeros(scr.shape, F32)

    lo = lax.broadcasted_iota(jnp.int32, (LONG_T, LANES), 1) < LANES // 2

    def tile(ref, c):
        m = ref[c]
        return jnp.concatenate([m, m], axis=0).T

    def relayout(c, carry):
        for ref, dst in ((r_ref, or_s), (w_ref, ow_s), (k_ref, ok_s), (a_ref, oa_s)):
            dst[pl.ds(c, LONG_T, stride=KEY_PITCH), :] = tile(ref, c)
            dst[pl.ds(c + LONG_NLT, LONG_T, stride=KEY_PITCH), :] = tile(ref, c + LONG_NLT)
        v_s[pl.ds(c, LONG_T, stride=VAL_PITCH), :] = jnp.where(lo, tile(v_ref, c), tile(v_ref, c + LONG_NLT))
        return carry

    lax.fori_loop(0, LONG_NLT, relayout, 0)

    def unpad(scr, pitch, n):
        return scr[...].reshape(LONG_T, pitch, LANES)[:, :n]

    def pad(x, pitch):
        zeros = jnp.zeros((LONG_T, pitch - x.shape[1], LANES), F32)
        return jnp.concatenate([x, zeros], axis=1).reshape(LONG_T * pitch, LANES)

    av, bv, k2, bonus = _rwkv_operands(unpad(ok_s, KEY_PITCH, B_DH), unpad(oa_s, KEY_PITCH, B_DH),
                                       unpad(or_s, KEY_PITCH, B_DH), kkt_ref[...], kat_ref[...], rkt_ref[...])
    av_s[...] = pad(av, KEY_PITCH)
    oa_s[...] = pad(bv, KEY_PITCH)
    ok_s[...] = pad(k2, KEY_PITCH)

    def step(t, carry):
        kr = pl.ds(pl.multiple_of(t * KEY_PITCH, SUBLANES), B_DH)
        vr = pl.ds(pl.multiple_of(t * VAL_PITCH, SUBLANES), LONG_NLT)
        y_s[vr, :] = _rwkv_step(s_ref, 0, LONG_NLT, or_s[kr, :], ow_s[kr, :], ok_s[kr, :], av_s[kr, :],
                                oa_s[kr, :], v_s[vr, :])
        return carry

    lax.fori_loop(0, LONG_T, step, 0)

    def head_sum(x):
        tot = jnp.sum(x, axis=1)
        return tot + pltpu.roll(tot, LANES // 2, axis=1)

    y = unpad(y_s, VAL_PITCH, LONG_NLT)
    d = y - (head_sum(y) * (1.0 / B_DH))[:, None, :]
    var = head_sum(d * d) * (1.0 / B_DH)
    y = (d * lax.rsqrt(var + B_GN_EPS)[:, None, :] * lnw_ref[...] + lnb_ref[...]
         + bonus[:, None, :] * unpad(v_s, VAL_PITCH, LONG_NLT))
    y_s[...] = pad(y, VAL_PITCH)

    def relayout_out(lt, carry):
        yt = y_s[pl.ds(lt, LONG_T, stride=VAL_PITCH), :].T
        for i2 in range(2):
            for b in range(nb):
                row = i2 * (LANES // 2) + b * B_HEADS
                yt_ref[b, pl.ds(lt + LONG_NLT * i2, B_HEADS, stride=B_DH), :] = yt[row:row + B_HEADS, :]
        return carry

    lax.fori_loop(0, LONG_NLT, relayout_out, 0)


def rwkv_scan_long(rc, wc, kc, ac, vc, S0, k_k, k_a, r_k, ln_w, ln_b):
    _, bh, L = rc.shape
    b = bh // B_HEADS
    assert bh * 2 == LANES and L % LONG_T == 0
    per_key = lambda p: jnp.tile(p.reshape(B_HEADS, B_DH).T, (1, LANES // B_HEADS))
    per_val = lambda p: jnp.concatenate(
        [jnp.tile(p.reshape(B_HEADS, 2, LONG_NLT)[:, i2].T, (1, b)) for i2 in range(2)], axis=1)
    s0 = S0.reshape(b, B_HEADS, 2, LONG_NLT, B_DH).transpose(3, 4, 2, 0, 1).reshape(LONG_NLT * B_DH, LANES)
    blk = pl.BlockSpec((B_DH, bh, LONG_T), lambda c: (0, 0, c), pipeline_mode=pl.Buffered(1))
    oblk = pl.BlockSpec((b, B_W, LONG_T), lambda c: (0, 0, c))
    const = lambda shape: pl.BlockSpec(shape, lambda c: (0,) * len(shape))
    big = pltpu.VMEM((LONG_T * KEY_PITCH, LANES), F32)
    small = pltpu.VMEM((LONG_T * VAL_PITCH, LANES), F32)
    yt, s = pl.pallas_call(
        _rwkv_scan_long_kernel,
        grid=(L // LONG_T,),
        in_specs=[blk] * 5 + [const((B_DH, LANES))] * 3 + [const((LONG_NLT, LANES))] * 2
        + [const((LONG_NLT * B_DH, LANES))],
        out_specs=[oblk, const((LONG_NLT * B_DH, LANES))],
        out_shape=[jax.ShapeDtypeStruct((b, B_W, L), F32), jax.ShapeDtypeStruct(s0.shape, F32)],
        scratch_shapes=[big] * 5 + [small, small],
        compiler_params=_cparams(("arbitrary",)),
        name="rwkv_scan_long",
    )(rc, wc, kc, ac, vc, per_key(k_k), per_key(k_a), per_key(r_k), per_val(ln_w), per_val(ln_b), s0)
    s = s.reshape(LONG_NLT, B_DH, 2, b, B_HEADS).transpose(3, 4, 2, 0, 1).reshape(b, B_HEADS, B_DH, B_DH)
    return yt, s


def _rwkv_scan_short_kernel(r_ref, w_ref, k_ref, a_ref, v_ref, kkt_ref, kat_ref, rkt_ref, lnw_ref, lnb_ref, s0_ref,
                            y_ref, sout_ref, s_s, or_s, ow_s, ok_s, oa_s, av_s, ov_s, y_s, *, L):
    nt = 2 * B_DH
    for q in range(nt * B_DH // LANES):
        s_s[q * LANES:(q + 1) * LANES, :] = s0_ref[:, q * LANES:(q + 1) * LANES].T
    for ref, dst in ((r_ref, or_s), (w_ref, ow_s), (k_ref, ok_s), (a_ref, oa_s), (v_ref, ov_s)):
        for t in range(L):
            dst[t] = ref[pl.ds(t, LANES, stride=L), :].T.reshape(2, B_DH, LANES)
    shape3 = (L * 2, B_DH, LANES)
    tiles = lambda ref: jnp.concatenate([ref[...]] * L, axis=0)
    av, bv, k2, bonus = _rwkv_operands(ok_s[...].reshape(shape3), oa_s[...].reshape(shape3),
                                       or_s[...].reshape(shape3), tiles(kkt_ref), tiles(kat_ref), tiles(rkt_ref))
    av_s[...] = av.reshape(L, 2, B_DH, LANES)
    oa_s[...] = bv.reshape(L, 2, B_DH, LANES)
    ok_s[...] = k2.reshape(L, 2, B_DH, LANES)

    def step(t, carry):
        for h in range(2):
            y_s[t, h] = _rwkv_step(s_s, h * B_DH, B_DH, or_s[t, h], ow_s[t, h], ok_s[t, h], av_s[t, h],
                                   oa_s[t, h], ov_s[t, h])
        return carry

    lax.fori_loop(0, L, step, 0)

    y = y_s[...]
    d = y - jnp.mean(y, axis=2, keepdims=True)
    var = jnp.mean(d * d, axis=2, keepdims=True)
    y = (d * lax.rsqrt(var + B_GN_EPS) * lnw_ref[...] + lnb_ref[...]
         + bonus.reshape(L, 2, 1, LANES) * ov_s[...])
    for t in range(L):
        y_ref[pl.ds(t, LANES, stride=L), :] = y[t].reshape(nt, LANES).T
    for q in range(nt * B_DH // LANES):
        sout_ref[:, q * LANES:(q + 1) * LANES] = s_s[q * LANES:(q + 1) * LANES, :].T


def rwkv_scan_short(r, w, k, a, v, S0, L, k_k, k_a, r_k, ln_w, ln_b):
    n = r.shape[0]
    b = n // L
    assert b == LANES
    npair = B_HEADS // 2
    wide = lambda p: jnp.broadcast_to(p.reshape(npair, 2, B_DH, 1), (npair, 2, B_DH, LANES))
    s0 = S0.reshape(b, B_HEADS * B_DH * B_DH)
    blk = pl.BlockSpec((n, LANES), lambda p: (0, p))
    cblk = pl.BlockSpec((None, 2, B_DH, LANES), lambda p: (p, 0, 0, 0))
    sblk = pl.BlockSpec((b, 2 * B_DH * B_DH), lambda p: (0, p))
    op = pltpu.VMEM((L, 2, B_DH, LANES), F32)
    y, s = pl.pallas_call(
        functools.partial(_rwkv_scan_short_kernel, L=L),
        grid=(npair,),
        in_specs=[blk] * 5 + [cblk] * 5 + [sblk],
        out_specs=[blk, sblk],
        out_shape=[jax.ShapeDtypeStruct((n, B_W), F32), jax.ShapeDtypeStruct(s0.shape, F32)],
        scratch_shapes=[pltpu.VMEM((2 * B_DH * B_DH, LANES), F32)] + [op] * 7,
        compiler_params=_cparams(("arbitrary",)),
        name="rwkv_scan_short",
    )(r, w, k, a, v, wide(k_k), wide(k_a), wide(r_k), wide(ln_w), wide(ln_b), s0)
    return y, s.reshape(b, B_HEADS, B_DH, B_DH)


RWKV_PREP_ROWS = 256
RWKV_SHORT_ROWS = 64
B_LORA_OFF = 3 * B_W


def _rwkv_prep_kernel(z_ref, sh0_ref, mu_ref, wwa_ref, g2_ref, w0_ref, a0_ref,
                      r_ref, w_ref, k_ref, a_ref, v_ref, g_ref, sh_ref, *scratch, T, nseq, channel_major):
    @pl.when(pl.program_id(1) == 0)
    def _():
        sh_ref[...] = sh0_ref[...]

    z = z_ref[...]
    rowid = lax.broadcasted_iota(jnp.int32, (z.shape[0], 1), 0)
    zprev = pltpu.roll(z, 1, axis=0)
    for u in range(nseq):
        zprev = jnp.where(rowid == u * T, sh_ref[u], zprev)
    for u in range(nseq):
        sh_ref[u] = z[(u + 1) * T - 1:(u + 1) * T, :]
    zs = z + (zprev - z) * mu_ref[...]
    r = zs[:, :B_W]
    k = zs[:, B_W:2 * B_W]
    lora = zs[:, B_LORA_OFF:B_LORA_OFF + LANES]
    lane = lax.broadcasted_iota(jnp.int32, lora.shape, 1)
    lora = jnp.where(lane < B_W_RANK, jnp.tanh(lora), lora).astype(BF16)
    wa = jnp.dot(lora, wwa_ref[...], preferred_element_type=F32)
    w_log = -jax.nn.softplus(-(w0_ref[...] + wa[:, :B_W])) - B_DECAY_OFFSET
    a = jax.nn.sigmoid(a0_ref[...] + wa[:, B_W:])
    zg = zs[:, B_LORA_OFF + LANES:B_LORA_OFF + LANES + B_G_RANK]
    def emit(ref, x):
        if not channel_major:
            ref[...] = x
            return
        xt_s = scratch[0]
        xt = x.T
        for j in range(xt_s.shape[0]):
            for h in range(B_HEADS):
                xt_s[j, h * KEY_PITCH:h * KEY_PITCH + B_DH, :] = xt[h * B_DH:(h + 1) * B_DH, j * LANES:(j + 1) * LANES]
        for c in range(B_DH):
            for j in range(xt_s.shape[0]):
                ref[c, :, j * LANES:(j + 1) * LANES] = xt_s[j, pl.ds(c, B_HEADS, stride=KEY_PITCH), :]

    emit(r_ref, r)
    emit(w_ref, jnp.exp(-jnp.exp(w_log)))
    emit(k_ref, k)
    emit(a_ref, a)
    emit(v_ref, zs[:, 2 * B_W:3 * B_W])
    g_ref[...] = jnp.dot(jax.nn.sigmoid(zg).astype(BF16), g2_ref[...], preferred_element_type=F32)


def rwkv_prep(zin, row0, b, L, gate_into, shift0, mu, w0, w2, a0, a2, g2):
    assert B_W_RANK + B_A_RANK == LANES
    T = math.gcd(L, RWKV_PREP_ROWS)
    channel_major = T == RWKV_PREP_ROWS
    nseq = 1 if channel_major else RWKV_SHORT_ROWS // T
    assert nseq == 1 or (L == T and b % nseq == 0)
    R = T * nseq
    nchunk = L // T
    blk0 = row0 // R
    wwa = jnp.zeros((LANES, 2 * B_W), F32).at[:B_W_RANK, :B_W].set(w2).at[B_W_RANK:, B_W:].set(a2).astype(BF16)
    row = lambda v: v.reshape(1, -1)
    if channel_major:
        blk = pl.BlockSpec((B_DH, B_HEADS, R), lambda i, c: (0, i, c))
        oshape = jax.ShapeDtypeStruct((B_DH, b * B_HEADS, L), F32)
    else:
        blk = pl.BlockSpec((R, B_W), lambda i, c: (i * nchunk + c, 0))
        oshape = jax.ShapeDtypeStruct((b * L, B_W), F32)
    gblk = pl.BlockSpec((R, B_W), lambda i, c: (blk0 + i * nchunk + c, 0))
    shspec = pl.BlockSpec((nseq, 1, B_COLS), lambda i, c: (i, 0, 0))
    const = lambda shape: pl.BlockSpec(shape, lambda i, c: (0,) * len(shape))
    in_specs = [pl.BlockSpec((R, B_COLS), lambda i, c: (blk0 + i * nchunk + c, 0)), shspec, const((1, B_COLS)),
                const((LANES, 2 * B_W)), const((B_G_RANK, B_W)), const((1, B_W)), const((1, B_W))]
    body, xspecs, xargs, aliases = _fill_into(
        functools.partial(_rwkv_prep_kernel, T=T, nseq=nseq, channel_major=channel_major),
        len(in_specs), gate_into, 5)
    outs = pl.pallas_call(
        body,
        grid=(b // nseq, nchunk),
        in_specs=in_specs + xspecs,
        out_specs=[blk] * 5 + [gblk, shspec],
        out_shape=[oshape] * 5
        + [jax.ShapeDtypeStruct((zin.shape[0], B_W), F32), jax.ShapeDtypeStruct((b, 1, B_COLS), F32)],
        input_output_aliases=aliases,
        scratch_shapes=[pltpu.VMEM((R // LANES, B_HEADS * KEY_PITCH, LANES), F32)] if channel_major else [],
        compiler_params=_cparams(("arbitrary", "arbitrary")),
        name="rwkv_prep",
    )(zin, shift0.reshape(b, 1, B_COLS), row(mu), wwa, g2.astype(BF16), row(w0), row(a0), *xargs)
    return outs[:5], outs[5], outs[6].reshape(b, B_COLS)


MLSTM_ROWS = 64
HIGHEST = lax.Precision.HIGHEST


def _mlstm_kernel(q_ref, k_ref, v_ref, o_ref, g_ref, gt_ref, brow_ref, bcol_ref, nw_ref,
                  c0_ref, n0_ref, m0_ref, h_ref, c_ref, n_ref, m_ref, *, T, nseq):
    R = MLSTM_ROWS

    @pl.when(pl.program_id(1) == 0)
    def _():
        c_ref[...] = c0_ref[...]
        n_ref[...] = n0_ref[...]
        m_ref[...] = m0_ref[...]

    shift = T.bit_length() - 1
    ri = lax.broadcasted_iota(jnp.int32, (R, R), 0)
    ci = lax.broadcasted_iota(jnp.int32, (R, R), 1)
    mask = (ci <= ri) & (jnp.right_shift(ri, shift) == jnp.right_shift(ci, shift))
    lmat = mask.astype(F32)
    rowid = lax.broadcasted_iota(jnp.int32, (R, 1), 0)
    rsel = [(rowid >= u * T) & (rowid < (u + 1) * T) for u in range(nseq)]

    g = g_ref[...] + brow_ref[...]
    lane = lax.broadcasted_iota(jnp.int32, g.shape, 1)
    glog = jnp.where((lane >= A_HEADS) & (lane < 2 * A_HEADS), jax.nn.log_sigmoid(g), g)
    gt = gt_ref[...] + bcol_ref[...]
    sub = lax.broadcasted_iota(jnp.int32, gt.shape, 0)
    gtlog = jnp.where(sub >= A_HEADS, jax.nn.log_sigmoid(gt), gt)
    bc_col = jnp.dot(lmat, glog, precision=HIGHEST, preferred_element_type=F32)
    bc_row = lax.dot_general(gtlog, lmat, (((1,), (1,)), ((), ())), precision=HIGHEST,
                             preferred_element_type=F32)
    lane_m = lax.broadcasted_iota(jnp.int32, (1, LANES), 1)
    m_old = [m_ref[u] for u in range(nseq)]
    m_out = [jnp.zeros((1, LANES), F32) for _ in range(nseq)]

    for h in range(A_HEADS):
        hs = slice(h * A_DK, (h + 1) * A_DK)
        bcc = bc_col[:, A_HEADS + h:A_HEADS + h + 1]
        bcr = bc_row[A_HEADS + h:A_HEADS + h + 1, :]
        lir = gtlog[h:h + 1, :]
        lic = glog[:, h:h + 1]
        m_u = [m_old[u][:, h:h + 1] for u in range(nseq)]
        m_col = m_u[0]
        for u in range(1, nseq):
            m_col = jnp.where(rsel[u], m_u[u], m_col)
        dmat = jnp.where(mask, bcc - bcr + lir, -jnp.inf)
        inter = bcc + m_col
        mt = jnp.maximum(inter, jnp.max(dmat, axis=1, keepdims=True))
        p = jnp.exp(dmat - mt)
        qh = q_ref[:, hs] * (A_DK ** -0.5)
        kh = k_ref[:, hs]
        qb, kb, vb = qh.astype(BF16), kh.astype(BF16), v_ref[:, hs].astype(BF16)
        wq = lax.dot_general(qb, kb, (((1,), (1,)), ((), ())), preferred_element_type=F32) * p
        wi = jnp.exp(inter - mt)
        c_old = [c_ref[u, h] for u in range(nseq)]
        n_old = [n_ref[u, h:h + 1, :] for u in range(nseq)]
        qc = jnp.dot(qb, c_old[0].astype(BF16), preferred_element_type=F32)
        qn = jnp.sum(qh * n_old[0], axis=1, keepdims=True)
        for u in range(1, nseq):
            qc = jnp.where(rsel[u], jnp.dot(qb, c_old[u].astype(BF16), preferred_element_type=F32), qc)
            qn = jnp.where(rsel[u], jnp.sum(qh * n_old[u], axis=1, keepdims=True), qn)
        num = jnp.dot(wq.astype(BF16), vb, preferred_element_type=F32) + wi * qc
        den = jnp.sum(wq, axis=1, keepdims=True) + wi * qn
        hh = num / jnp.maximum(jnp.abs(den), jnp.exp(-mt))
        hh = hh * lax.rsqrt(jnp.mean(hh * hh, axis=-1, keepdims=True) + EPS)
        h_ref[:, hs] = hh * nw_ref[:, hs] * jax.nn.sigmoid(o_ref[:, hs])
        for u in range(nseq):
            b_last = bcc[(u + 1) * T - 1:(u + 1) * T, :]
            gs = b_last - bcc + lic
            gmax = jnp.max(gs if nseq == 1 else jnp.where(rsel[u], gs, -jnp.inf), axis=0, keepdims=True)
            m_new = jnp.maximum(b_last + m_u[u], gmax)
            decay = jnp.exp(b_last + m_u[u] - m_new)
            ws = jnp.exp(gs - m_new)
            if nseq > 1:
                ws = jnp.where(rsel[u], ws, 0.0)
            kw = kh * ws
            c_ref[u, h] = decay * c_old[u] + lax.dot_general(
                kw.astype(BF16), vb, (((0,), (0,)), ((), ())), preferred_element_type=F32)
            n_ref[u, h:h + 1, :] = decay * n_old[u] + jnp.sum(kw, axis=0, keepdims=True)
            m_out[u] = jnp.where(lane_m == h, m_new, m_out[u])
    for u in range(nseq):
        m_ref[u] = m_out[u]


def mlstm(zin, gates_t, row0, b, L, h_into, b_i, b_f, m_norm, C0, n0, m0):
    R = MLSTM_ROWS
    T = math.gcd(L, R)
    nseq = R // T
    assert nseq == 1 or (L == T and b % nseq == 0)
    nchunk = L // T
    blk0 = row0 // R
    bias = jnp.concatenate([b_i, b_f])
    bias_row = jnp.zeros((1, LANES), F32).at[0, :2 * A_HEADS].set(bias)
    bias_col = bias.reshape(2 * A_HEADS, 1)
    m0p = jnp.zeros((b, 1, LANES), F32).at[:, 0, :A_HEADS].set(m0)
    blk = lambda i, c: blk0 + i * nchunk + c
    rowblk = lambda col: pl.BlockSpec((R, A_QK), lambda i, c: (blk(i, c), EVEN_A_OFF // A_QK + col))
    cspec = pl.BlockSpec((nseq, A_HEADS, A_DK, A_DV), lambda i, c: (i, 0, 0, 0))
    nspec = pl.BlockSpec((nseq, A_HEADS, A_DK), lambda i, c: (i, 0, 0))
    mspec = pl.BlockSpec((nseq, 1, LANES), lambda i, c: (i, 0, 0))
    const = lambda shape: pl.BlockSpec(shape, lambda i, c: (0,) * len(shape))
    in_specs = [rowblk(0), rowblk(1), rowblk(2), rowblk(3),
                pl.BlockSpec((R, LANES), lambda i, c: (blk(i, c), EVEN_G_OFF // LANES)),
                pl.BlockSpec((None, 2 * A_HEADS, R), lambda i, c: (blk(i, c), 0, 0)),
                const((1, LANES)), const((2 * A_HEADS, 1)), const((1, A_V)),
                cspec, nspec, mspec]
    body, xspecs, xargs, aliases = _fill_into(
        functools.partial(_mlstm_kernel, T=T, nseq=nseq), len(in_specs), h_into, 0)
    h, C, n, m = pl.pallas_call(
        body,
        grid=(b // nseq, nchunk),
        in_specs=in_specs + xspecs,
        out_specs=[pl.BlockSpec((R, A_V), lambda i, c: (blk(i, c), 0)), cspec, nspec, mspec],
        out_shape=[jax.ShapeDtypeStruct((zin.shape[0], A_V), F32),
                   jax.ShapeDtypeStruct(C0.shape, F32), jax.ShapeDtypeStruct(n0.shape, F32),
                   jax.ShapeDtypeStruct(m0p.shape, F32)],
        input_output_aliases=aliases,
        compiler_params=_cparams(("arbitrary", "arbitrary")),
        name="mlstm",
    )(zin, zin, zin, zin, zin, gates_t, bias_row, bias_col, m_norm.reshape(1, A_V), C0, n0, m0p, *xargs)
    return h, C, n, m[:, 0, :A_HEADS]


def even_mixer(zin, gates_t, row0, b, L, into, C0, n0, m0, S0, shift0, b_i, b_f, m_norm, mu, w0, w2, a0, a2, g2,
               k_k, k_a, r_k, ln_w, ln_b):
    hA, C, n, m = mlstm(zin, gates_t, row0, b, L, into[0], b_i, b_f, m_norm, C0, n0, m0)
    ops, g, shift = rwkv_prep(zin, row0, b, L, into[1], shift0, mu, w0, w2, a0, a2, g2)
    if ops[0].ndim == 3:
        hB, S = rwkv_scan_long(*ops, S0, k_k, k_a, r_k, ln_w, ln_b)
    else:
        hB, S = rwkv_scan_short(*ops, S0, L, k_k, k_a, r_k, ln_w, ln_b)
    return hA, hB, g, (C, n, m, S, shift)


C_PAIRS = C_HEADS // 2
C_GROUP_W = C_INNER // C_GROUPS
C_BC_W = 2 * C_GROUPS * C_STATE
SSD_SHORT_ROWS = 32


def _ssd_kernel(z_ref, x_ref, bc_ref, dt_ref, dtt_ref, cw_ref, cb_ref, dtb_ref, dtbt_ref, al_ref, alt_ref,
                dsk_ref, nw_ref, s0_ref, cv0_ref, y_ref, s_ref, cv_ref, *, T, nseq):
    R = T * nseq

    @pl.when(pl.program_id(1) == 0)
    def _():
        s_ref[...] = s0_ref[...]
        cv_ref[...] = cv0_ref[...]

    shift = T.bit_length() - 1
    ri = lax.broadcasted_iota(jnp.int32, (R, R), 0)
    ci = lax.broadcasted_iota(jnp.int32, (R, R), 1)
    mask = (ci <= ri) & (jnp.right_shift(ri, shift) == jnp.right_shift(ci, shift))
    lmat = mask.astype(F32)
    rowid = lax.broadcasted_iota(jnp.int32, (R, 1), 0)
    rsel = [(rowid >= u * T) & (rowid < (u + 1) * T) for u in range(nseq)]

    def conv_silu(x, cols):
        acc = cb_ref[:, cols] + x * cw_ref[C_CONV - 1:C_CONV, cols]
        for d in range(1, C_CONV):
            xd = pltpu.roll(x, d, axis=0)
            for u in range(nseq):
                for t in range(d):
                    src = SUBLANES - d + t
                    xd = jnp.where(rowid == u * T + t, cv_ref[u, src:src + 1, cols], xd)
            acc = acc + xd * cw_ref[C_CONV - 1 - d:C_CONV - d, cols]
        for u in range(nseq):
            cv_ref[u, :, cols] = x[(u + 1) * T - SUBLANES:(u + 1) * T, :]
        return acc * jax.nn.sigmoid(acc)

    xc = conv_silu(x_ref[...], slice(0, C_INNER))
    bcc_ = conv_silu(bc_ref[...], slice(C_INNER, C_CONV_DIM))

    dtv = jax.nn.softplus(dt_ref[...] + dtb_ref[...])
    dtt = jax.nn.softplus(dtt_ref[...] + dtbt_ref[...])
    cum_col = jnp.dot(lmat, dtv * (-jnp.exp(al_ref[...])), precision=HIGHEST, preferred_element_type=F32)
    cum_row = lax.dot_general(dtt * (-jnp.exp(alt_ref[...])), lmat, (((1,), (1,)), ((), ())),
                              precision=HIGHEST, preferred_element_type=F32)

    lo = lax.broadcasted_iota(jnp.int32, (R, LANES), 1) < C_HEADDIM
    rlo = lax.broadcasted_iota(jnp.int32, (LANES, 1), 0) < C_HEADDIM
    nt = (((1,), (1,)), ((), ()))
    tn = (((0,), (0,)), ((), ()))
    pairs_per_group = C_PAIRS // C_GROUPS
    for g in range(C_GROUPS):
        bg = bcc_[:, g * C_STATE:(g + 1) * C_STATE].astype(BF16)
        cg = bcc_[:, (C_GROUPS + g) * C_STATE:(C_GROUPS + g + 1) * C_STATE].astype(BF16)
        cbm = lax.dot_general(cg, bg, nt, preferred_element_type=F32)
        ys = None
        for u in range(nseq):
            sg = s_ref[u, g * pairs_per_group:(g + 1) * pairs_per_group].reshape(C_GROUP_W, C_STATE)
            t_u = lax.dot_general(cg, sg.astype(BF16), nt, preferred_element_type=F32)
            ys = t_u if u == 0 else jnp.where(rsel[u], t_u, ys)
        for q in range(pairs_per_group):
            pr = g * pairs_per_group + q
            ps = slice(pr * LANES, (pr + 1) * LANES)
            xp = xc[:, ps]
            cc = [cum_col[:, 2 * pr + e:2 * pr + e + 1] for e in range(2)]
            intra = None
            for e, keep in ((0, lo), (1, jnp.logical_not(lo))):
                hh = 2 * pr + e
                seg = jnp.exp(jnp.where(mask, cc[e] - cum_row[hh:hh + 1, :], -jnp.inf))
                mix = cbm * seg * dtt[hh:hh + 1, :]
                part = jnp.dot(mix.astype(BF16), jnp.where(keep, xp, 0.0).astype(BF16),
                               preferred_element_type=F32)
                intra = part if intra is None else intra + part
            scale = jnp.where(lo, jnp.exp(cc[0]), jnp.exp(cc[1]))
            yp = intra + scale * ys[:, q * LANES:(q + 1) * LANES] + dsk_ref[:, ps] * xp
            zp = z_ref[:, ps]
            y_ref[:, ps] = yp * (zp * jax.nn.sigmoid(zp))
            for u in range(nseq):
                last = (u + 1) * T - 1
                ct = [cc[e][last:last + 1, :] for e in range(2)]
                tail = jnp.where(lo, jnp.exp(ct[0] - cc[0]) * dtv[:, 2 * pr:2 * pr + 1],
                                 jnp.exp(ct[1] - cc[1]) * dtv[:, 2 * pr + 1:2 * pr + 2])
                xw = xp * tail
                if nseq > 1:
                    xw = jnp.where(rsel[u], xw, 0.0)
                upd = lax.dot_general(xw.astype(BF16), bg, tn, preferred_element_type=F32)
                dec = jnp.where(rlo, jnp.exp(ct[0]), jnp.exp(ct[1]))
                s_ref[u, pr] = dec * s_ref[u, pr] + upd

    for g in range(C_GROUPS):
        gs_ = slice(g * C_GROUP_W, (g + 1) * C_GROUP_W)
        yg = y_ref[:, gs_]
        y_ref[:, gs_] = yg * lax.rsqrt(jnp.mean(yg * yg, axis=-1, keepdims=True) + EPS) * nw_ref[:, gs_]


def ssd_mixer(zin, row0, b, L, y_into, ssm0, conv0, conv_w, conv_b, dt_bias, a_log, d_skip, norm_w):
    T = math.gcd(L, C_CHUNK)
    nseq = 1 if T == C_CHUNK else SSD_SHORT_ROWS // T
    assert nseq == 1 or (L == T and b % nseq == 0)
    R = T * nseq
    nchunk = L // T
    nblk = b * L // R
    blk0 = row0 // R
    dt_t = zin[row0:row0 + b * L, ODD_DT_OFF:ODD_DT_OFF + C_HEADS].reshape(nblk, R, C_HEADS).transpose(0, 2, 1)
    pad_row = lambda v: jnp.zeros((1, LANES), F32).at[0, :C_HEADS].set(v)
    s0 = ssm0.reshape(b, C_PAIRS, 2 * C_HEADDIM, C_STATE)
    cv0 = jnp.concatenate([jnp.zeros((b, SUBLANES - (C_CONV - 1), C_CONV_DIM), F32), conv0], axis=1)
    blk = lambda i, c: blk0 + i * nchunk + c
    sspec = pl.BlockSpec((nseq, C_PAIRS, 2 * C_HEADDIM, C_STATE), lambda i, c: (i, 0, 0, 0))
    cvspec = pl.BlockSpec((nseq, SUBLANES, C_CONV_DIM), lambda i, c: (i, 0, 0))
    const = lambda shape: pl.BlockSpec(shape, lambda i, c: (0,) * len(shape))
    in_specs = [pl.BlockSpec((R, C_INNER), lambda i, c: (blk(i, c), 0)),
                pl.BlockSpec((R, C_INNER), lambda i, c: (blk(i, c), 1)),
                pl.BlockSpec((R, C_BC_W), lambda i, c: (blk(i, c), 2 * C_INNER // C_BC_W)),
                pl.BlockSpec((R, LANES), lambda i, c: (blk(i, c), ODD_DT_OFF // LANES)),
                pl.BlockSpec((None, C_HEADS, R), lambda i, c: (i * nchunk + c, 0, 0)),
                const((C_CONV, C_CONV_DIM)), const((1, C_CONV_DIM)),
                const((1, LANES)), const((C_HEADS, 1)), const((1, LANES)), const((C_HEADS, 1)),
                const((1, C_INNER)), const((1, C_INNER)), sspec, cvspec]
    body, xspecs, xargs, aliases = _fill_into(
        functools.partial(_ssd_kernel, T=T, nseq=nseq), len(in_specs), y_into, 0)
    y, s, cv = pl.pallas_call(
        body,
        grid=(b // nseq, nchunk),
        in_specs=in_specs + xspecs,
        out_specs=[pl.BlockSpec((R, C_INNER), lambda i, c: (blk(i, c), 0)), sspec, cvspec],
        out_shape=[jax.ShapeDtypeStruct((zin.shape[0], C_INNER), F32),
                   jax.ShapeDtypeStruct(s0.shape, F32), jax.ShapeDtypeStruct(cv0.shape, F32)],
        input_output_aliases=aliases,
        compiler_params=_cparams(("arbitrary", "arbitrary")),
        name="ssd",
    )(zin, zin, zin, zin, dt_t, conv_w, conv_b.reshape(1, C_CONV_DIM),
      pad_row(dt_bias), dt_bias.reshape(C_HEADS, 1), pad_row(a_log), a_log.reshape(C_HEADS, 1),
      jnp.repeat(d_skip, C_HEADDIM).reshape(1, C_INNER), norm_w.reshape(1, C_INNER), s0, cv0, *xargs)
    return y, (s.reshape(b, C_HEADS, C_HEADDIM, C_STATE), cv[:, SUBLANES - (C_CONV - 1):])


def _even_w_in(w):
    qkvo = w[:, :2 * A_QK + 2 * A_V]
    gates = w[:, 2 * A_QK + 2 * A_V:A_COLS]
    rwkv = w[:, A_COLS:]
    pad = jnp.zeros((D_MODEL, EVEN_GATE_PAD - 2 * A_HEADS), w.dtype)
    return jnp.concatenate([rwkv, gates, pad, qkvo], axis=1).astype(BF16)


def _odd_w_in(w):
    pad = jnp.zeros((D_MODEL, ODD_DT_PAD - C_HEADS), w.dtype)
    return jnp.concatenate([w, pad], axis=1).astype(BF16)


def kernel(x_prompt, x_sample, state_mlstm_C, state_mlstm_n, state_mlstm_m, state_rwkv_S,
           state_rwkv_shift, state_ssm, state_conv, p_prompt, p_sample,
           norm_mix, norm_ffn, w_ffn_up, w_ffn_down, w_ple_proj, norm_ple, w_ple_gate, norm_final,
           w_in_even, mlstm_b_i, mlstm_b_f, mlstm_norm, rwkv_mu, rwkv_w0, rwkv_w2, rwkv_a0, rwkv_a2,
           rwkv_g2, rwkv_k_k, rwkv_k_a, rwkv_r_k, rwkv_ln_w, rwkv_ln_b, w_out_even,
           w_in_odd, conv_w, conv_b, dt_bias, a_log, d_skip, ssm_norm, w_out_odd):
    bp, Lp, _ = x_prompt.shape
    bs, Ls, _ = x_sample.shape
    n_p = bp * Lp
    x = jnp.concatenate([x_prompt.reshape(n_p, D_MODEL), x_sample.reshape(bs * Ls, D_MODEL)], axis=0)
    p_all = jnp.concatenate([p_prompt.reshape(DEPTH, n_p, PLE_DIM),
                             p_sample.reshape(DEPTH, bs * Ls, PLE_DIM)], axis=1)

    even_small = (mlstm_b_i, mlstm_b_f, mlstm_norm, rwkv_mu, rwkv_w0, rwkv_w2, rwkv_a0, rwkv_a2,
                  rwkv_g2, rwkv_k_k, rwkv_k_a, rwkv_r_k, rwkv_ln_w, rwkv_ln_b)
    odd_small = (conv_w, conv_b, dt_bias, a_log, d_skip, ssm_norm)

    zeros_even = (jnp.zeros((bp, A_HEADS, A_DK, A_DV), F32), jnp.zeros((bp, A_HEADS, A_DK), F32),
                  jnp.zeros((bp, A_HEADS), F32), jnp.zeros((bp, B_HEADS, B_DH, B_DH), F32),
                  jnp.zeros((bp, B_COLS), F32))
    zeros_odd = (jnp.zeros((bp, C_HEADS, C_HEADDIM, C_STATE), F32),
                 jnp.zeros((bp, C_CONV - 1, C_CONV_DIM), F32))

    st_p_even, st_s_even, st_p_odd, st_s_odd = [], [], [], []
    for i in range(DEPTH):
        j = i // 2
        if i % 2 == 0:
            zin = norm_matmul(x, norm_mix[i], _even_w_in(w_in_even[j]), 512)
            small = [t[j] for t in even_small]
            gates_t = zin[:, EVEN_G_OFF:EVEN_G_OFF + 2 * A_HEADS].reshape(
                -1, MLSTM_ROWS, 2 * A_HEADS).transpose(0, 2, 1)
            ha, hb_p, g, sp = even_mixer(zin, gates_t, 0, bp, Lp, (None, None), *zeros_even, *small)
            ha, hb_s, g, ss = even_mixer(zin, gates_t, n_p, bs, Ls, (ha, g), state_mlstm_C[j], state_mlstm_n[j],
                                         state_mlstm_m[j], state_rwkv_S[j], state_rwkv_shift[j], *small)
            st_p_even.append(sp)
            st_s_even.append(ss)
            wo = w_out_even[j].astype(BF16)
            xo = matmul_res([(ha, None, wo[:A_V], None), (hb_p, g, wo[A_V:], Lp)], x, 0, n_p)
            x = matmul_res([(ha, None, wo[:A_V], None), (hb_s, g, wo[A_V:], 0)], x, n_p, bs * Ls, xo)
        else:
            zin = norm_matmul(x, norm_mix[i], _odd_w_in(w_in_odd[j]), 768)
            small = [t[j] for t in odd_small]
            mix, sp = ssd_mixer(zin, 0, bp, Lp, None, *zeros_odd, *small)
            mix, ss = ssd_mixer(zin, n_p, bs, Ls, mix, state_ssm[j], state_conv[j], *small)
            st_p_odd.append(sp)
            st_s_odd.append(ss)
            x = matmul_res([(mix, None, w_out_odd[j].astype(BF16), None)], x)
        x = ffn(x, norm_ffn[i], w_ffn_up[i].astype(BF16), w_ffn_down[i].astype(BF16))
        x = ple(x, norm_ple[i], w_ple_gate[i].astype(BF16), p_all[i], w_ple_proj[i].astype(BF16))
    y = final_norm(x, norm_final)
    y_prompt = y[:n_p].reshape(bp, Lp, D_MODEL)
    y_sample = y[n_p:].reshape(bs, Ls, D_MODEL)
    stack = lambda sts, idx: jnp.stack([s[idx] for s in sts])
    return (y_prompt, y_sample,
            stack(st_p_even, 0), stack(st_p_even, 1), stack(st_p_even, 2), stack(st_p_even, 3),
            stack(st_p_even, 4), stack(st_p_odd, 0), stack(st_p_odd, 1),
            stack(st_s_even, 0), stack(st_s_even, 1), stack(st_s_even, 2), stack(st_s_even, 3),
            stack(st_s_even, 4), stack(st_s_odd, 0), stack(st_s_odd, 1))
```

```python
import math
import functools
import jax
import jax.numpy as jnp
from jax import lax
from jax.experimental import pallas as pl
from jax.experimental.pallas import tpu as pltpu

D_MODEL = 1024
DEPTH = 4
F32 = jnp.float32
BF16 = jnp.bfloat16
EPS = 1e-6
N_EVEN = (DEPTH + 1) // 2
N_ODD = DEPTH // 2
D_FF = 4 * D_MODEL
PLE_DIM = 256

A_HEADS = 4
A_DK = D_MODEL // 8
A_DV = D_MODEL // 8
A_CHUNK = 64
A_QK = A_HEADS * A_DK
A_V = A_HEADS * A_DV
A_COLS = 2 * A_QK + 2 * A_V + 2 * A_HEADS

B_HEADS = 8
B_DH = 64
B_W = B_HEADS * B_DH
B_W_RANK = 64
B_A_RANK = 64
B_G_RANK = 128
B_COLS = 3 * B_W + B_W_RANK + B_A_RANK + B_G_RANK
B_DECAY_OFFSET = 0.5
B_GN_EPS = 64e-5

EVEN_COLS = A_COLS + B_COLS
EVEN_OUT = A_V + B_W

C_INNER = 2 * D_MODEL
C_HEADDIM = 64
C_HEADS = C_INNER // C_HEADDIM
C_GROUPS = 4
C_HPG = C_HEADS // C_GROUPS
C_STATE = 128
C_CONV = 4
C_CHUNK = 128
C_CONV_DIM = C_INNER + 2 * C_GROUPS * C_STATE
ODD_COLS = C_INNER + C_CONV_DIM + C_HEADS

LANES = 128
SUBLANES = 8
VMEM_LIMIT = 56 * 1024 * 1024
TOKEN_TILE = 256

EVEN_GATE_PAD = 256
EVEN_N = B_COLS + EVEN_GATE_PAD + 2 * A_QK + 2 * A_V
EVEN_G_OFF = B_COLS
EVEN_A_OFF = B_COLS + EVEN_GATE_PAD
ODD_DT_PAD = 256
ODD_N = C_INNER + C_CONV_DIM + ODD_DT_PAD
ODD_DT_OFF = C_INNER + C_CONV_DIM


def _cparams(sem):
    return pltpu.CompilerParams(dimension_semantics=sem, vmem_limit_bytes=VMEM_LIMIT)


def _rms(x, g):
    return x * lax.rsqrt(jnp.mean(x * x, axis=-1, keepdims=True) + EPS) * g


def _resident(shape):
    nd = len(shape)
    return pl.BlockSpec(shape, lambda *_: (0,) * nd, pipeline_mode=pl.Buffered(1))


def _rows(width):
    return pl.BlockSpec((TOKEN_TILE, width), lambda i: (i, 0))


def _fill_into(body, n_in, into, out_idx):
    if into is None:
        return body, [], [], {}

    def skipping(*refs):
        return body(*refs[:n_in], *refs[n_in + 1:])

    return skipping, [pl.BlockSpec(memory_space=pl.ANY)], [into], {n_in: out_idx}


def _norm_matmul_kernel(x_ref, g_ref, w_ref, o_ref, *, tn):
    xn = _rms(x_ref[...], g_ref[...]).astype(BF16)
    for n0 in range(0, w_ref.shape[1], tn):
        o_ref[:, n0:n0 + tn] = jnp.dot(xn, w_ref[:, n0:n0 + tn], preferred_element_type=F32)


def norm_matmul(x, g, w, tn, row0=0, total=None, into=None):
    m, n = x.shape[0], w.shape[1]
    total = m if total is None else total
    t0 = row0 // TOKEN_TILE
    in_specs = [_rows(D_MODEL), _resident((1, D_MODEL)), _resident(w.shape)]
    body, xspecs, xargs, aliases = _fill_into(functools.partial(_norm_matmul_kernel, tn=tn), len(in_specs), into, 0)
    return pl.pallas_call(
        body,
        grid=(m // TOKEN_TILE,),
        in_specs=in_specs + xspecs,
        out_specs=pl.BlockSpec((TOKEN_TILE, n), lambda i: (t0 + i, 0)),
        out_shape=jax.ShapeDtypeStruct((total, n), F32),
        input_output_aliases=aliases,
        compiler_params=_cparams(("arbitrary",)),
        name="norm_matmul",
    )(x, g.reshape(1, D_MODEL), w, *xargs)


def _matmul_res_kernel(*refs, kinds):
    x_ref, o_ref = refs[-2], refs[-1]
    acc = x_ref[...]
    pos = 0
    for has_gate, channel_major in kinds:
        a = refs[pos][...]
        if channel_major:
            a = a.T
        if has_gate:
            a = a * refs[pos + 1][...]
        w_ref = refs[pos + 1 + has_gate]
        pos += 2 + has_gate
        acc = acc + jnp.dot(a.astype(BF16), w_ref[...], preferred_element_type=F32)
    o_ref[...] = acc


def matmul_res(terms, x, row0=0, nrows=None, into=None, total=None):
    m = x.shape[0] if total is None else total
    nrows = m if nrows is None else nrows
    t0 = row0 // TOKEN_TILE
    rows = lambda width: pl.BlockSpec((TOKEN_TILE, width), lambda i: (t0 + i, 0))
    xspec = rows(D_MODEL) if total is None else _rows(D_MODEL)
    specs, args, kinds = [], [], []
    for a, gate, w, seq_len in terms:
        if seq_len is None:
            specs.append(rows(a.shape[1]))
        elif seq_len == 0:
            specs.append(pl.BlockSpec((TOKEN_TILE, a.shape[1]), lambda i: (i, 0)))
        else:
            per_seq = seq_len // TOKEN_TILE
            specs.append(pl.BlockSpec((None, a.shape[1], TOKEN_TILE), lambda i: (i // per_seq, 0, i % per_seq)))
        specs += ([rows(w.shape[0])] if gate is not None else []) + [_resident(w.shape)]
        args += [a] + ([gate] if gate is not None else []) + [w]
        kinds.append((int(gate is not None), bool(seq_len)))
    specs.append(xspec)
    body, xspecs, xargs, aliases = _fill_into(
        functools.partial(_matmul_res_kernel, kinds=tuple(kinds)), len(specs), into, 0)
    return pl.pallas_call(
        body,
        grid=(nrows // TOKEN_TILE,),
        in_specs=specs + xspecs,
        out_specs=rows(D_MODEL),
        out_shape=jax.ShapeDtypeStruct((m, D_MODEL), F32),
        input_output_aliases=aliases,
        compiler_params=_cparams(("arbitrary",)),
        name="matmul_res",
    )(*args, x, *xargs)


FFN_CHUNK = 512


def _ffn_kernel(x_ref, g_ref, wu_ref, wd_ref, o_ref):
    x = x_ref[...]
    xn = _rms(x, g_ref[...]).astype(BF16)
    acc = x
    for c0 in range(0, D_FF, FFN_CHUNK):
        h = jnp.dot(xn, wu_ref[:, c0:c0 + FFN_CHUNK], preferred_element_type=F32)
        h = jnp.square(jnp.maximum(h, 0.0)).astype(BF16)
        acc = acc + jnp.dot(h, wd_ref[c0:c0 + FFN_CHUNK, :], preferred_element_type=F32)
    o_ref[...] = acc


FFN_TILE = 512


def ffn(x, g, wu, wd):
    m = x.shape[0]
    rows = pl.BlockSpec((FFN_TILE, D_MODEL), lambda i: (i, 0))
    return pl.pallas_call(
        _ffn_kernel,
        grid=(m // FFN_TILE,),
        in_specs=[rows, _resident((1, D_MODEL)), _resident(wu.shape), _resident(wd.shape)],
        out_specs=rows,
        out_shape=jax.ShapeDtypeStruct((m, D_MODEL), F32),
        compiler_params=_cparams(("arbitrary",)),
        name="ffn",
    )(x, g.reshape(1, D_MODEL), wu, wd)


def _ple_kernel(x_ref, g_ref, wg_ref, p_ref, wp_ref, o_ref):
    x = x_ref[...]
    xn = _rms(x, g_ref[...]).astype(BF16)
    gate = jax.nn.sigmoid(jnp.dot(xn, wg_ref[...], preferred_element_type=F32))
    proj = jnp.dot(p_ref[...].astype(BF16), wp_ref[...], preferred_element_type=F32)
    o_ref[...] = x + proj * gate


def ple(x, g, wg, p, wp, row0, into=None):
    t0 = row0 // TOKEN_TILE
    rows = pl.BlockSpec((TOKEN_TILE, D_MODEL), lambda i: (t0 + i, 0))
    in_specs = [rows, _resident((1, D_MODEL)), _resident(wg.shape), _rows(PLE_DIM), _resident(wp.shape)]
    body, xspecs, xargs, aliases = _fill_into(_ple_kernel, len(in_specs), into, 0)
    return pl.pallas_call(
        body,
        grid=(p.shape[0] // TOKEN_TILE,),
        in_specs=in_specs + xspecs,
        out_specs=rows,
        out_shape=jax.ShapeDtypeStruct(x.shape, F32),
        input_output_aliases=aliases,
        compiler_params=_cparams(("arbitrary",)),
        name="ple",
    )(x, g.reshape(1, D_MODEL), wg, p, wp, *xargs)


def _final_norm_kernel(x_ref, g_ref, o_ref):
    o_ref[...] = _rms(x_ref[...], g_ref[...])


def final_norm(x, g, row0, nrows):
    t0 = row0 // TOKEN_TILE
    return pl.pallas_call(
        _final_norm_kernel,
        grid=(nrows // TOKEN_TILE,),
        in_specs=[pl.BlockSpec((TOKEN_TILE, D_MODEL), lambda i: (t0 + i, 0)), _resident((1, D_MODEL))],
        out_specs=_rows(D_MODEL),
        out_shape=jax.ShapeDtypeStruct((nrows, D_MODEL), F32),
        compiler_params=_cparams(("arbitrary",)),
        name="final_norm",
    )(x, g.reshape(1, D_MODEL))


def _rwkv_operands(k, a, r, kk_t, ka_t, rk_t):
    kk = k * kk_t
    kk = kk * lax.rsqrt(jnp.maximum(jnp.sum(kk * kk, axis=1, keepdims=True), 1e-24))
    k2 = k * (1.0 + (a - 1.0) * ka_t)
    return -kk, kk * a, k2, jnp.sum(r * k2 * rk_t, axis=1)


def _rwkv_step(s_ref, tile0, ntiles, r, w, k, a, b, vrows):
    ys = []
    for lt in range(ntiles):
        rows = slice((tile0 + lt) * B_DH, (tile0 + lt + 1) * B_DH)
        s = s_ref[rows, :]
        sa = jnp.sum(s * a, axis=0, keepdims=True)
        s = s * w + sa * b + vrows[lt:lt + 1] * k
        s_ref[rows, :] = s
        ys.append(jnp.sum(s * r, axis=0, keepdims=True))
    return jnp.concatenate(ys, axis=0)


LONG_T = 128
LONG_NLT = B_DH // 2
KEY_PITCH = B_DH + SUBLANES
VAL_PITCH = LONG_NLT + SUBLANES


def _rwkv_scan_long_kernel(r_ref, w_ref, k_ref, a_ref, v_ref, kkt_ref, kat_ref, rkt_ref, lnw_ref, lnb_ref, s0_ref,
                           yt_ref, s_ref, or_s, ow_s, ok_s, oa_s, av_s, v_s, y_s):
    nb = yt_ref.shape[0]

    @pl.when(pl.program_id(0) == 0)
    def _():
        s_ref[...] = s0_ref[...]
        for scr in (or_s, ow_s, ok_s, oa_s, av_s, v_s, y_s):
            scr[...] = jnp.zeros(scr.shape, F32)

    lo = lax.broadcasted_iota(jnp.int32, (LONG_T, LANES), 1) < LANES // 2

    def tile(ref, c):
        m = ref[c]
        return jnp.concatenate([m, m], axis=0).T

    def relayout(c, carry):
        for ref, dst in ((r_ref, or_s), (w_ref, ow_s), (k_ref, ok_s), (a_ref, oa_s)):
            dst[pl.ds(c, LONG_T, stride=KEY_PITCH), :] = tile(ref, c)
            dst[pl.ds(c + LONG_NLT, LONG_T, stride=KEY_PITCH), :] = tile(ref, c + LONG_NLT)
        v_s[pl.ds(c, LONG_T, stride=VAL_PITCH), :] = jnp.where(lo, tile(v_ref, c), tile(v_ref, c + LONG_NLT))
        return carry

    lax.fori_loop(0, LONG_NLT, relayout, 0)

    def unpad(scr, pitch, n):
        return scr[...].reshape(LONG_T, pitch, LANES)[:, :n]

    def pad(x, pitch):
        zeros = jnp.zeros((LONG_T, pitch - x.shape[1], LANES), F32)
        return jnp.concatenate([x, zeros], axis=1).reshape(LONG_T * pitch, LANES)

    av, bv, k2, bonus = _rwkv_operands(unpad(ok_s, KEY_PITCH, B_DH), unpad(oa_s, KEY_PITCH, B_DH),
                                       unpad(or_s, KEY_PITCH, B_DH), kkt_ref[...], kat_ref[...], rkt_ref[...])
    av_s[...] = pad(av, KEY_PITCH)
    oa_s[...] = pad(bv, KEY_PITCH)
    ok_s[...] = pad(k2, KEY_PITCH)

    def step(t, carry):
        kr = pl.ds(pl.multiple_of(t * KEY_PITCH, SUBLANES), B_DH)
        vr = pl.ds(pl.multiple_of(t * VAL_PITCH, SUBLANES), LONG_NLT)
        y_s[vr, :] = _rwkv_step(s_ref, 0, LONG_NLT, or_s[kr, :], ow_s[kr, :], ok_s[kr, :], av_s[kr, :],
                                oa_s[kr, :], v_s[vr, :])
        return carry

    lax.fori_loop(0, LONG_T, step, 0)

    def head_sum(x):
        tot = jnp.sum(x, axis=1)
        return tot + pltpu.roll(tot, LANES // 2, axis=1)

    y = unpad(y_s, VAL_PITCH, LONG_NLT)
    d = y - (head_sum(y) * (1.0 / B_DH))[:, None, :]
    var = head_sum(d * d) * (1.0 / B_DH)
    y = (d * lax.rsqrt(var + B_GN_EPS)[:, None, :] * lnw_ref[...] + lnb_ref[...]
         + bonus[:, None, :] * unpad(v_s, VAL_PITCH, LONG_NLT))
    y_s[...] = pad(y, VAL_PITCH)

    def relayout_out(lt, carry):
        yt = y_s[pl.ds(lt, LONG_T, stride=VAL_PITCH), :].T
        for i2 in range(2):
            for b in range(nb):
                row = i2 * (LANES // 2) + b * B_HEADS
                yt_ref[b, pl.ds(lt + LONG_NLT * i2, B_HEADS, stride=B_DH), :] = yt[row:row + B_HEADS, :]
        return carry

    lax.fori_loop(0, LONG_NLT, relayout_out, 0)


def rwkv_scan_long(rc, wc, kc, ac, vc, S0, k_k, k_a, r_k, ln_w, ln_b):
    _, bh, L = rc.shape
    b = bh // B_HEADS
    assert bh * 2 == LANES and L % LONG_T == 0
    per_key = lambda p: jnp.tile(p.reshape(B_HEADS, B_DH).T, (1, LANES // B_HEADS))
    per_val = lambda p: jnp.concatenate(
        [jnp.tile(p.reshape(B_HEADS, 2, LONG_NLT)[:, i2].T, (1, b)) for i2 in range(2)], axis=1)
    s0 = S0.reshape(b, B_HEADS, 2, LONG_NLT, B_DH).transpose(3, 4, 2, 0, 1).reshape(LONG_NLT * B_DH, LANES)
    blk = pl.BlockSpec((B_DH, bh, LONG_T), lambda c: (0, 0, c), pipeline_mode=pl.Buffered(1))
    oblk = pl.BlockSpec((b, B_W, LONG_T), lambda c: (0, 0, c))
    const = lambda shape: pl.BlockSpec(shape, lambda c: (0,) * len(shape))
    big = pltpu.VMEM((LONG_T * KEY_PITCH, LANES), F32)
    small = pltpu.VMEM((LONG_T * VAL_PITCH, LANES), F32)
    yt, s = pl.pallas_call(
        _rwkv_scan_long_kernel,
        grid=(L // LONG_T,),
        in_specs=[blk] * 5 + [const((B_DH, LANES))] * 3 + [const((LONG_NLT, LANES))] * 2
        + [const((LONG_NLT * B_DH, LANES))],
        out_specs=[oblk, const((LONG_NLT * B_DH, LANES))],
        out_shape=[jax.ShapeDtypeStruct((b, B_W, L), F32), jax.ShapeDtypeStruct(s0.shape, F32)],
        scratch_shapes=[big] * 5 + [small, small],
        compiler_params=_cparams(("arbitrary",)),
        name="rwkv_scan_long",
    )(rc, wc, kc, ac, vc, per_key(k_k), per_key(k_a), per_key(r_k), per_val(ln_w), per_val(ln_b), s0)
    s = s.reshape(LONG_NLT, B_DH, 2, b, B_HEADS).transpose(3, 4, 2, 0, 1).reshape(b, B_HEADS, B_DH, B_DH)
    return yt, s


def _rwkv_scan_short_kernel(r_ref, w_ref, k_ref, a_ref, v_ref, kkt_ref, kat_ref, rkt_ref, lnw_ref, lnb_ref, s0_ref,
                            y_ref, sout_ref, s_s, or_s, ow_s, ok_s, oa_s, av_s, ov_s, y_s, *, L):
    nt = 2 * B_DH
    for q in range(nt * B_DH // LANES):
        s_s[q * LANES:(q + 1) * LANES, :] = s0_ref[:, q * LANES:(q + 1) * LANES].T
    for ref, dst in ((r_ref, or_s), (w_ref, ow_s), (k_ref, ok_s), (a_ref, oa_s), (v_ref, ov_s)):
        for t in range(L):
            dst[t] = ref[pl.ds(t, LANES, stride=L), :].T.reshape(2, B_DH, LANES)
    shape3 = (L * 2, B_DH, LANES)
    tiles = lambda ref: jnp.concatenate([ref[...]] * L, axis=0)
    av, bv, k2, bonus = _rwkv_operands(ok_s[...].reshape(shape3), oa_s[...].reshape(shape3),
                                       or_s[...].reshape(shape3), tiles(kkt_ref), tiles(kat_ref), tiles(rkt_ref))
    av_s[...] = av.reshape(L, 2, B_DH, LANES)
    oa_s[...] = bv.reshape(L, 2, B_DH, LANES)
    ok_s[...] = k2.reshape(L, 2, B_DH, LANES)

    def step(t, carry):
        for h in range(2):
            y_s[t, h] = _rwkv_step(s_s, h * B_DH, B_DH, or_s[t, h], ow_s[t, h], ok_s[t, h], av_s[t, h],
                                   oa_s[t, h], ov_s[t, h])
        return carry

    lax.fori_loop(0, L, step, 0)

    y = y_s[...]
    d = y - jnp.mean(y, axis=2, keepdims=True)
    var = jnp.mean(d * d, axis=2, keepdims=True)
    y = (d * lax.rsqrt(var + B_GN_EPS) * lnw_ref[...] + lnb_ref[...]
         + bonus.reshape(L, 2, 1, LANES) * ov_s[...])
    for t in range(L):
        y_ref[pl.ds(t, LANES, stride=L), :] = y[t].reshape(nt, LANES).T
    for q in range(nt * B_DH // LANES):
        sout_ref[:, q * LANES:(q + 1) * LANES] = s_s[q * LANES:(q + 1) * LANES, :].T


def rwkv_scan_short(r, w, k, a, v, S0, L, k_k, k_a, r_k, ln_w, ln_b):
    n = r.shape[0]
    b = n // L
    assert b == LANES
    npair = B_HEADS // 2
    wide = lambda p: jnp.broadcast_to(p.reshape(npair, 2, B_DH, 1), (npair, 2, B_DH, LANES))
    s0 = S0.reshape(b, B_HEADS * B_DH * B_DH)
    blk = pl.BlockSpec((n, LANES), lambda p: (0, p))
    cblk = pl.BlockSpec((None, 2, B_DH, LANES), lambda p: (p, 0, 0, 0))
    sblk = pl.BlockSpec((b, 2 * B_DH * B_DH), lambda p: (0, p))
    op = pltpu.VMEM((L, 2, B_DH, LANES), F32)
    y, s = pl.pallas_call(
        functools.partial(_rwkv_scan_short_kernel, L=L),
        grid=(npair,),
        in_specs=[blk] * 5 + [cblk] * 5 + [sblk],
        out_specs=[blk, sblk],
        out_shape=[jax.ShapeDtypeStruct((n, B_W), F32), jax.ShapeDtypeStruct(s0.shape, F32)],
        scratch_shapes=[pltpu.VMEM((2 * B_DH * B_DH, LANES), F32)] + [op] * 7,
        compiler_params=_cparams(("arbitrary",)),
        name="rwkv_scan_short",
    )(r, w, k, a, v, wide(k_k), wide(k_a), wide(r_k), wide(ln_w), wide(ln_b), s0)
    return y, s.reshape(b, B_HEADS, B_DH, B_DH)


RWKV_PREP_ROWS = 256
RWKV_SHORT_ROWS = 64
B_LORA_OFF = 3 * B_W


def _rwkv_prep_kernel(z_ref, sh0_ref, mu_ref, wwa_ref, g2_ref, w0_ref, a0_ref,
                      r_ref, w_ref, k_ref, a_ref, v_ref, g_ref, sh_ref, *scratch, T, nseq, channel_major):
    @pl.when(pl.program_id(1) == 0)
    def _():
        sh_ref[...] = sh0_ref[...]

    z = z_ref[...]
    rowid = lax.broadcasted_iota(jnp.int32, (z.shape[0], 1), 0)
    zprev = pltpu.roll(z, 1, axis=0)
    for u in range(nseq):
        zprev = jnp.where(rowid == u * T, sh_ref[u], zprev)
    for u in range(nseq):
        sh_ref[u] = z[(u + 1) * T - 1:(u + 1) * T, :]
    zs = z + (zprev - z) * mu_ref[...]
    r = zs[:, :B_W]
    k = zs[:, B_W:2 * B_W]
    lora = zs[:, B_LORA_OFF:B_LORA_OFF + LANES]
    lane = lax.broadcasted_iota(jnp.int32, lora.shape, 1)
    lora = jnp.where(lane < B_W_RANK, jnp.tanh(lora), lora).astype(BF16)
    wa = jnp.dot(lora, wwa_ref[...], preferred_element_type=F32)
    w_log = -jax.nn.softplus(-(w0_ref[...] + wa[:, :B_W])) - B_DECAY_OFFSET
    a = jax.nn.sigmoid(a0_ref[...] + wa[:, B_W:])
    zg = zs[:, B_LORA_OFF + LANES:B_LORA_OFF + LANES + B_G_RANK]
    def emit(ref, x):
        if not channel_major:
            ref[...] = x
            return
        xt_s = scratch[0]
        xt = x.T
        for j in range(xt_s.shape[0]):
            for h in range(B_HEADS):
                xt_s[j, h * KEY_PITCH:h * KEY_PITCH + B_DH, :] = xt[h * B_DH:(h + 1) * B_DH, j * LANES:(j + 1) * LANES]
        for c in range(B_DH):
            for j in range(xt_s.shape[0]):
                ref[c, :, j * LANES:(j + 1) * LANES] = xt_s[j, pl.ds(c, B_HEADS, stride=KEY_PITCH), :]

    emit(r_ref, r)
    emit(w_ref, jnp.exp(-jnp.exp(w_log)))
    emit(k_ref, k)
    emit(a_ref, a)
    emit(v_ref, zs[:, 2 * B_W:3 * B_W])
    g_ref[...] = jnp.dot(jax.nn.sigmoid(zg).astype(BF16), g2_ref[...], preferred_element_type=F32)


def rwkv_prep(zin, row0, b, L, gate_into, shift0, mu, w0, w2, a0, a2, g2):
    assert B_W_RANK + B_A_RANK == LANES
    T = math.gcd(L, RWKV_PREP_ROWS)
    channel_major = T == RWKV_PREP_ROWS
    nseq = 1 if channel_major else RWKV_SHORT_ROWS // T
    assert nseq == 1 or (L == T and b % nseq == 0)
    R = T * nseq
    nchunk = L // T
    blk0 = row0 // R
    wwa = jnp.zeros((LANES, 2 * B_W), F32).at[:B_W_RANK, :B_W].set(w2).at[B_W_RANK:, B_W:].set(a2).astype(BF16)
    row = lambda v: v.reshape(1, -1)
    if channel_major:
        blk = pl.BlockSpec((B_DH, B_HEADS, R), lambda i, c: (0, i, c))
        oshape = jax.ShapeDtypeStruct((B_DH, b * B_HEADS, L), F32)
    else:
        blk = pl.BlockSpec((R, B_W), lambda i, c: (i * nchunk + c, 0))
        oshape = jax.ShapeDtypeStruct((b * L, B_W), F32)
    gblk = pl.BlockSpec((R, B_W), lambda i, c: (blk0 + i * nchunk + c, 0))
    shspec = pl.BlockSpec((nseq, 1, B_COLS), lambda i, c: (i, 0, 0))
    const = lambda shape: pl.BlockSpec(shape, lambda i, c: (0,) * len(shape))
    in_specs = [pl.BlockSpec((R, B_COLS), lambda i, c: (blk0 + i * nchunk + c, 0)), shspec, const((1, B_COLS)),
                const((LANES, 2 * B_W)), const((B_G_RANK, B_W)), const((1, B_W)), const((1, B_W))]
    body, xspecs, xargs, aliases = _fill_into(
        functools.partial(_rwkv_prep_kernel, T=T, nseq=nseq, channel_major=channel_major),
        len(in_specs), gate_into, 5)
    outs = pl.pallas_call(
        body,
        grid=(b // nseq, nchunk),
        in_specs=in_specs + xspecs,
        out_specs=[blk] * 5 + [gblk, shspec],
        out_shape=[oshape] * 5
        + [jax.ShapeDtypeStruct((zin.shape[0], B_W), F32), jax.ShapeDtypeStruct((b, 1, B_COLS), F32)],
        input_output_aliases=aliases,
        scratch_shapes=[pltpu.VMEM((R // LANES, B_HEADS * KEY_PITCH, LANES), F32)] if channel_major else [],
        compiler_params=_cparams(("arbitrary", "arbitrary")),
        name="rwkv_prep",
    )(zin, shift0.reshape(b, 1, B_COLS), row(mu), wwa, g2.astype(BF16), row(w0), row(a0), *xargs)
    return outs[:5], outs[5], outs[6].reshape(b, B_COLS)


MLSTM_ROWS = 64
HIGHEST = lax.Precision.HIGHEST


def _mlstm_kernel(q_ref, k_ref, v_ref, o_ref, g_ref, gt_ref, brow_ref, bcol_ref, nw_ref,
                  c0_ref, n0_ref, m0_ref, h_ref, c_ref, n_ref, m_ref, *, T, nseq):
    R = MLSTM_ROWS

    @pl.when(pl.program_id(1) == 0)
    def _():
        c_ref[...] = c0_ref[...]
        n_ref[...] = n0_ref[...]
        m_ref[...] = m0_ref[...]

    shift = T.bit_length() - 1
    ri = lax.broadcasted_iota(jnp.int32, (R, R), 0)
    ci = lax.broadcasted_iota(jnp.int32, (R, R), 1)
    mask = (ci <= ri) & (jnp.right_shift(ri, shift) == jnp.right_shift(ci, shift))
    lmat = mask.astype(F32)
    rowid = lax.broadcasted_iota(jnp.int32, (R, 1), 0)
    rsel = [(rowid >= u * T) & (rowid < (u + 1) * T) for u in range(nseq)]

    g = g_ref[...] + brow_ref[...]
    lane = lax.broadcasted_iota(jnp.int32, g.shape, 1)
    glog = jnp.where((lane >= A_HEADS) & (lane < 2 * A_HEADS), jax.nn.log_sigmoid(g), g)
    gt = gt_ref[...] + bcol_ref[...]
    sub = lax.broadcasted_iota(jnp.int32, gt.shape, 0)
    gtlog = jnp.where(sub >= A_HEADS, jax.nn.log_sigmoid(gt), gt)
    bc_col = jnp.dot(lmat, glog, precision=HIGHEST, preferred_element_type=F32)
    bc_row = lax.dot_general(gtlog, lmat, (((1,), (1,)), ((), ())), precision=HIGHEST,
                             preferred_element_type=F32)
    lane_m = lax.broadcasted_iota(jnp.int32, (1, LANES), 1)
    m_old = [m_ref[u] for u in range(nseq)]
    m_out = [jnp.zeros((1, LANES), F32) for _ in range(nseq)]

    for h in range(A_HEADS):
        hs = slice(h * A_DK, (h + 1) * A_DK)
        bcc = bc_col[:, A_HEADS + h:A_HEADS + h + 1]
        bcr = bc_row[A_HEADS + h:A_HEADS + h + 1, :]
        lir = gtlog[h:h + 1, :]
        lic = glog[:, h:h + 1]
        m_u = [m_old[u][:, h:h + 1] for u in range(nseq)]
        m_col = m_u[0]
        for u in range(1, nseq):
            m_col = jnp.where(rsel[u], m_u[u], m_col)
        dmat = jnp.where(mask, bcc - bcr + lir, -jnp.inf)
        inter = bcc + m_col
        mt = jnp.maximum(inter, jnp.max(dmat, axis=1, keepdims=True))
        p = jnp.exp(dmat - mt)
        qh = q_ref[:, hs] * (A_DK ** -0.5)
        kh = k_ref[:, hs]
        qb, kb, vb = qh.astype(BF16), kh.astype(BF16), v_ref[:, hs].astype(BF16)
        wq = lax.dot_general(qb, kb, (((1,), (1,)), ((), ())), preferred_element_type=F32) * p
        wi = jnp.exp(inter - mt)
        c_old = [c_ref[u, h] for u in range(nseq)]
        n_old = [n_ref[u, h:h + 1, :] for u in range(nseq)]
        qc = jnp.dot(qb, c_old[0].astype(BF16), preferred_element_type=F32)
        qn = jnp.sum(qh * n_old[0], axis=1, keepdims=True)
        for u in range(1, nseq):
            qc = jnp.where(rsel[u], jnp.dot(qb, c_old[u].astype(BF16), preferred_element_type=F32), qc)
            qn = jnp.where(rsel[u], jnp.sum(qh * n_old[u], axis=1, keepdims=True), qn)
        num = jnp.dot(wq.astype(BF16), vb, preferred_element_type=F32) + wi * qc
        den = jnp.sum(wq, axis=1, keepdims=True) + wi * qn
        hh = num / jnp.maximum(jnp.abs(den), jnp.exp(-mt))
        hh = hh * lax.rsqrt(jnp.mean(hh * hh, axis=-1, keepdims=True) + EPS)
        h_ref[:, hs] = hh * nw_ref[:, hs] * jax.nn.sigmoid(o_ref[:, hs])
        for u in range(nseq):
            b_last = bcc[(u + 1) * T - 1:(u + 1) * T, :]
            gs = b_last - bcc + lic
            gmax = jnp.max(gs if nseq == 1 else jnp.where(rsel[u], gs, -jnp.inf), axis=0, keepdims=True)
            m_new = jnp.maximum(b_last + m_u[u], gmax)
            decay = jnp.exp(b_last + m_u[u] - m_new)
            ws = jnp.exp(gs - m_new)
            if nseq > 1:
                ws = jnp.where(rsel[u], ws, 0.0)
            kw = kh * ws
            c_ref[u, h] = decay * c_old[u] + lax.dot_general(
                kw.astype(BF16), vb, (((0,), (0,)), ((), ())), preferred_element_type=F32)
            n_ref[u, h:h + 1, :] = decay * n_old[u] + jnp.sum(kw, axis=0, keepdims=True)
            m_out[u] = jnp.where(lane_m == h, m_new, m_out[u])
    for u in range(nseq):
        m_ref[u] = m_out[u]


def mlstm(zin, gates_t, row0, b, L, h_into, b_i, b_f, m_norm, C0, n0, m0):
    R = MLSTM_ROWS
    T = math.gcd(L, R)
    nseq = R // T
    assert nseq == 1 or (L == T and b % nseq == 0)
    nchunk = L // T
    blk0 = row0 // R
    bias = jnp.concatenate([b_i, b_f])
    bias_row = jnp.zeros((1, LANES), F32).at[0, :2 * A_HEADS].set(bias)
    bias_col = bias.reshape(2 * A_HEADS, 1)
    m0p = jnp.zeros((b, 1, LANES), F32).at[:, 0, :A_HEADS].set(m0)
    blk = lambda i, c: blk0 + i * nchunk + c
    rowblk = lambda col: pl.BlockSpec((R, A_QK), lambda i, c: (blk(i, c), EVEN_A_OFF // A_QK + col))
    cspec = pl.BlockSpec((nseq, A_HEADS, A_DK, A_DV), lambda i, c: (i, 0, 0, 0))
    nspec = pl.BlockSpec((nseq, A_HEADS, A_DK), lambda i, c: (i, 0, 0))
    mspec = pl.BlockSpec((nseq, 1, LANES), lambda i, c: (i, 0, 0))
    const = lambda shape: pl.BlockSpec(shape, lambda i, c: (0,) * len(shape))
    in_specs = [rowblk(0), rowblk(1), rowblk(2), rowblk(3),
                pl.BlockSpec((R, LANES), lambda i, c: (blk(i, c), EVEN_G_OFF // LANES)),
                pl.BlockSpec((None, 2 * A_HEADS, R), lambda i, c: (blk(i, c), 0, 0)),
                const((1, LANES)), const((2 * A_HEADS, 1)), const((1, A_V)),
                cspec, nspec, mspec]
    body, xspecs, xargs, aliases = _fill_into(
        functools.partial(_mlstm_kernel, T=T, nseq=nseq), len(in_specs), h_into, 0)
    h, C, n, m = pl.pallas_call(
        body,
        grid=(b // nseq, nchunk),
        in_specs=in_specs + xspecs,
        out_specs=[pl.BlockSpec((R, A_V), lambda i, c: (blk(i, c), 0)), cspec, nspec, mspec],
        out_shape=[jax.ShapeDtypeStruct((zin.shape[0], A_V), F32),
                   jax.ShapeDtypeStruct(C0.shape, F32), jax.ShapeDtypeStruct(n0.shape, F32),
                   jax.ShapeDtypeStruct(m0p.shape, F32)],
        input_output_aliases=aliases,
        compiler_params=_cparams(("arbitrary", "arbitrary")),
        name="mlstm",
    )(zin, zin, zin, zin, zin, gates_t, bias_row, bias_col, m_norm.reshape(1, A_V), C0, n0, m0p, *xargs)
    return h, C, n, m[:, 0, :A_HEADS]


def even_mixer(zin, gates_t, row0, b, L, into, C0, n0, m0, S0, shift0, b_i, b_f, m_norm, mu, w0, w2, a0, a2, g2,
               k_k, k_a, r_k, ln_w, ln_b):
    hA, C, n, m = mlstm(zin, gates_t, row0, b, L, into[0], b_i, b_f, m_norm, C0, n0, m0)
    ops, g, shift = rwkv_prep(zin, row0, b, L, into[1], shift0, mu, w0, w2, a0, a2, g2)
    if ops[0].ndim == 3:
        hB, S = rwkv_scan_long(*ops, S0, k_k, k_a, r_k, ln_w, ln_b)
    else:
        hB, S = rwkv_scan_short(*ops, S0, L, k_k, k_a, r_k, ln_w, ln_b)
    return hA, hB, g, (C, n, m, S, shift)


C_PAIRS = C_HEADS // 2
C_GROUP_W = C_INNER // C_GROUPS
C_BC_W = 2 * C_GROUPS * C_STATE
SSD_SHORT_ROWS = 32


def _ssd_kernel(z_ref, x_ref, bc_ref, dt_ref, dtt_ref, cw_ref, cb_ref, dtb_ref, dtbt_ref, al_ref, alt_ref,
                dsk_ref, nw_ref, s0_ref, cv0_ref, y_ref, s_ref, cv_ref, stg_ref, *, T, nseq):
    R = T * nseq

    @pl.when(pl.program_id(1) == 0)
    def _():
        s_ref[...] = s0_ref[...]
        cv_ref[...] = cv0_ref[...]

    shift = T.bit_length() - 1
    ri = lax.broadcasted_iota(jnp.int32, (R, R), 0)
    ci = lax.broadcasted_iota(jnp.int32, (R, R), 1)
    mask = (ci <= ri) & (jnp.right_shift(ri, shift) == jnp.right_shift(ci, shift))
    lmat = mask.astype(F32)
    rowid = lax.broadcasted_iota(jnp.int32, (R, 1), 0)
    rsel = [(rowid >= u * T) & (rowid < (u + 1) * T) for u in range(nseq)]

    def conv_silu(src_ref, cols):
        accs = []
        for u in range(nseq):
            stg_ref[u, 0:SUBLANES, cols] = cv_ref[u, :, cols]
            stg_ref[u, SUBLANES:SUBLANES + T, cols] = src_ref[u * T:(u + 1) * T, :]
            acc = cb_ref[:, cols]
            for d in range(C_CONV):
                acc = acc + stg_ref[u, SUBLANES - d:SUBLANES - d + T, cols] * cw_ref[C_CONV - 1 - d:C_CONV - d, cols]
            accs.append(acc)
            cv_ref[u, :, cols] = stg_ref[u, T:T + SUBLANES, cols]
        acc = accs[0] if nseq == 1 else jnp.concatenate(accs, axis=0)
        return acc * jax.nn.sigmoid(acc)

    xc = conv_silu(x_ref, slice(0, C_INNER))
    bcc_ = conv_silu(bc_ref, slice(C_INNER, C_CONV_DIM))

    dtv = jax.nn.softplus(dt_ref[...] + dtb_ref[...])
    dtt = jax.nn.softplus(dtt_ref[...] + dtbt_ref[...])
    cum_col = jnp.dot(lmat, dtv * (-jnp.exp(al_ref[...])), precision=HIGHEST, preferred_element_type=F32)
    cum_row = lax.dot_general(dtt * (-jnp.exp(alt_ref[...])), lmat, (((1,), (1,)), ((), ())),
                              precision=HIGHEST, preferred_element_type=F32)

    lo = lax.broadcasted_iota(jnp.int32, (R, LANES), 1) < C_HEADDIM
    rlo = lax.broadcasted_iota(jnp.int32, (LANES, 1), 0) < C_HEADDIM
    nt = (((1,), (1,)), ((), ()))
    tn = (((0,), (0,)), ((), ()))
    pairs_per_group = C_PAIRS // C_GROUPS
    for g in range(C_GROUPS):
        bg = bcc_[:, g * C_STATE:(g + 1) * C_STATE].astype(BF16)
        cg = bcc_[:, (C_GROUPS + g) * C_STATE:(C_GROUPS + g + 1) * C_STATE].astype(BF16)
        cbm = lax.dot_general(cg, bg, nt, preferred_element_type=F32)
        ys = None
        for u in range(nseq):
            sg = s_ref[u, g * pairs_per_group:(g + 1) * pairs_per_group].reshape(C_GROUP_W, C_STATE)
            t_u = lax.dot_general(cg, sg.astype(BF16), nt, preferred_element_type=F32)
            ys = t_u if u == 0 else jnp.where(rsel[u], t_u, ys)
        for q in range(pairs_per_group):
            pr = g * pairs_per_group + q
            ps = slice(pr * LANES, (pr + 1) * LANES)
            xp = xc[:, ps]
            cc = [cum_col[:, 2 * pr + e:2 * pr + e + 1] for e in range(2)]
            intra = None
            for e, keep in ((0, lo), (1, jnp.logical_not(lo))):
                hh = 2 * pr + e
                seg = jnp.exp(jnp.where(mask, cc[e] - cum_row[hh:hh + 1, :], -jnp.inf))
                mix = cbm * seg * dtt[hh:hh + 1, :]
                part = jnp.dot(mix.astype(BF16), jnp.where(keep, xp, 0.0).astype(BF16),
                               preferred_element_type=F32)
                intra = part if intra is None else intra + part
            scale = jnp.where(lo, jnp.exp(cc[0]), jnp.exp(cc[1]))
            yp = intra + scale * ys[:, q * LANES:(q + 1) * LANES] + dsk_ref[:, ps] * xp
            zp = z_ref[:, ps]
            y_ref[:, ps] = yp * (zp * jax.nn.sigmoid(zp))
            for u in range(nseq):
                last = (u + 1) * T - 1
                ct = [cc[e][last:last + 1, :] for e in range(2)]
                tail = jnp.where(lo, jnp.exp(ct[0] - cc[0]) * dtv[:, 2 * pr:2 * pr + 1],
                                 jnp.exp(ct[1] - cc[1]) * dtv[:, 2 * pr + 1:2 * pr + 2])
                xw = xp * tail
                if nseq > 1:
                    xw = jnp.where(rsel[u], xw, 0.0)
                upd = lax.dot_general(xw.astype(BF16), bg, tn, preferred_element_type=F32)
                dec = jnp.where(rlo, jnp.exp(ct[0]), jnp.exp(ct[1]))
                s_ref[u, pr] = dec * s_ref[u, pr] + upd

    for g in range(C_GROUPS):
        gs_ = slice(g * C_GROUP_W, (g + 1) * C_GROUP_W)
        yg = y_ref[:, gs_]
        y_ref[:, gs_] = yg * lax.rsqrt(jnp.mean(yg * yg, axis=-1, keepdims=True) + EPS) * nw_ref[:, gs_]


def ssd_mixer(zin, row0, b, L, y_into, ssm0, conv0, conv_w, conv_b, dt_bias, a_log, d_skip, norm_w):
    T = math.gcd(L, C_CHUNK)
    nseq = 1 if T == C_CHUNK else SSD_SHORT_ROWS // T
    assert nseq == 1 or (L == T and b % nseq == 0)
    R = T * nseq
    nchunk = L // T
    nblk = b * L // R
    blk0 = row0 // R
    dt_t = zin[row0:row0 + b * L, ODD_DT_OFF:ODD_DT_OFF + C_HEADS].reshape(nblk, R, C_HEADS).transpose(0, 2, 1)
    pad_row = lambda v: jnp.zeros((1, LANES), F32).at[0, :C_HEADS].set(v)
    s0 = ssm0.reshape(b, C_PAIRS, 2 * C_HEADDIM, C_STATE)
    cv0 = jnp.concatenate([jnp.zeros((b, SUBLANES - (C_CONV - 1), C_CONV_DIM), F32), conv0], axis=1)
    blk = lambda i, c: blk0 + i * nchunk + c
    sspec = pl.BlockSpec((nseq, C_PAIRS, 2 * C_HEADDIM, C_STATE), lambda i, c: (i, 0, 0, 0))
    cvspec = pl.BlockSpec((nseq, SUBLANES, C_CONV_DIM), lambda i, c: (i, 0, 0))
    const = lambda shape: pl.BlockSpec(shape, lambda i, c: (0,) * len(shape))
    in_specs = [pl.BlockSpec((R, C_INNER), lambda i, c: (blk(i, c), 0)),
                pl.BlockSpec((R, C_INNER), lambda i, c: (blk(i, c), 1)),
                pl.BlockSpec((R, C_BC_W), lambda i, c: (blk(i, c), 2 * C_INNER // C_BC_W)),
                pl.BlockSpec((R, LANES), lambda i, c: (blk(i, c), ODD_DT_OFF // LANES)),
                pl.BlockSpec((None, C_HEADS, R), lambda i, c: (i * nchunk + c, 0, 0)),
                const((C_CONV, C_CONV_DIM)), const((1, C_CONV_DIM)),
                const((1, LANES)), const((C_HEADS, 1)), const((1, LANES)), const((C_HEADS, 1)),
                const((1, C_INNER)), const((1, C_INNER)), sspec, cvspec]
    body, xspecs, xargs, aliases = _fill_into(
        functools.partial(_ssd_kernel, T=T, nseq=nseq), len(in_specs), y_into, 0)
    y, s, cv = pl.pallas_call(
        body,
        grid=(b // nseq, nchunk),
        in_specs=in_specs + xspecs,
        out_specs=[pl.BlockSpec((R, C_INNER), lambda i, c: (blk(i, c), 0)), sspec, cvspec],
        out_shape=[jax.ShapeDtypeStruct((zin.shape[0], C_INNER), F32),
                   jax.ShapeDtypeStruct(s0.shape, F32), jax.ShapeDtypeStruct(cv0.shape, F32)],
        input_output_aliases=aliases,
        scratch_shapes=[pltpu.VMEM((nseq, SUBLANES + T, C_CONV_DIM), F32)],
        compiler_params=_cparams(("arbitrary", "arbitrary")),
        name="ssd",
    )(zin, zin, zin, zin, dt_t, conv_w, conv_b.reshape(1, C_CONV_DIM),
      pad_row(dt_bias), dt_bias.reshape(C_HEADS, 1), pad_row(a_log), a_log.reshape(C_HEADS, 1),
      jnp.repeat(d_skip, C_HEADDIM).reshape(1, C_INNER), norm_w.reshape(1, C_INNER), s0, cv0, *xargs)
    return y, (s.reshape(b, C_HEADS, C_HEADDIM, C_STATE), cv[:, SUBLANES - (C_CONV - 1):])


def _even_w_in(w):
    qkvo = w[:, :2 * A_QK + 2 * A_V]
    gates = w[:, 2 * A_QK + 2 * A_V:A_COLS]
    rwkv = w[:, A_COLS:]
    pad = jnp.zeros((D_MODEL, EVEN_GATE_PAD - 2 * A_HEADS), w.dtype)
    return jnp.concatenate([rwkv, gates, pad, qkvo], axis=1).astype(BF16)


def _odd_w_in(w):
    pad = jnp.zeros((D_MODEL, ODD_DT_PAD - C_HEADS), w.dtype)
    return jnp.concatenate([w, pad], axis=1).astype(BF16)


def kernel(x_prompt, x_sample, state_mlstm_C, state_mlstm_n, state_mlstm_m, state_rwkv_S,
           state_rwkv_shift, state_ssm, state_conv, p_prompt, p_sample,
           norm_mix, norm_ffn, w_ffn_up, w_ffn_down, w_ple_proj, norm_ple, w_ple_gate, norm_final,
           w_in_even, mlstm_b_i, mlstm_b_f, mlstm_norm, rwkv_mu, rwkv_w0, rwkv_w2, rwkv_a0, rwkv_a2,
           rwkv_g2, rwkv_k_k, rwkv_k_a, rwkv_r_k, rwkv_ln_w, rwkv_ln_b, w_out_even,
           w_in_odd, conv_w, conv_b, dt_bias, a_log, d_skip, ssm_norm, w_out_odd):
    bp, Lp, _ = x_prompt.shape
    bs, Ls, _ = x_sample.shape
    n_p, n_s = bp * Lp, bs * Ls
    n_tot = n_p + n_s
    xp = x_prompt.reshape(n_p, D_MODEL)
    xs = x_sample.reshape(n_s, D_MODEL)
    pp = p_prompt.reshape(DEPTH, n_p, PLE_DIM)
    ps = p_sample.reshape(DEPTH, n_s, PLE_DIM)
    x = None

    even_small = (mlstm_b_i, mlstm_b_f, mlstm_norm, rwkv_mu, rwkv_w0, rwkv_w2, rwkv_a0, rwkv_a2,
                  rwkv_g2, rwkv_k_k, rwkv_k_a, rwkv_r_k, rwkv_ln_w, rwkv_ln_b)
    odd_small = (conv_w, conv_b, dt_bias, a_log, d_skip, ssm_norm)

    zeros_even = (jnp.zeros((bp, A_HEADS, A_DK, A_DV), F32), jnp.zeros((bp, A_HEADS, A_DK), F32),
                  jnp.zeros((bp, A_HEADS), F32), jnp.zeros((bp, B_HEADS, B_DH, B_DH), F32),
                  jnp.zeros((bp, B_COLS), F32))
    zeros_odd = (jnp.zeros((bp, C_HEADS, C_HEADDIM, C_STATE), F32),
                 jnp.zeros((bp, C_CONV - 1, C_CONV_DIM), F32))

    st_p_even, st_s_even, st_p_odd, st_s_odd = [], [], [], []
    for i in range(DEPTH):
        j = i // 2
        if i % 2 == 0:
            w_in = _even_w_in(w_in_even[j])
            if x is None:
                zin = norm_matmul(xp, norm_mix[i], w_in, 512, 0, n_tot)
                zin = norm_matmul(xs, norm_mix[i], w_in, 512, n_p, n_tot, zin)
            else:
                zin = norm_matmul(x, norm_mix[i], w_in, 512)
            small = [t[j] for t in even_small]
            gates_t = zin[:, EVEN_G_OFF:EVEN_G_OFF + 2 * A_HEADS].reshape(
                -1, MLSTM_ROWS, 2 * A_HEADS).transpose(0, 2, 1)
            ha, hb_p, g, sp = even_mixer(zin, gates_t, 0, bp, Lp, (None, None), *zeros_even, *small)
            ha, hb_s, g, ss = even_mixer(zin, gates_t, n_p, bs, Ls, (ha, g), state_mlstm_C[j], state_mlstm_n[j],
                                         state_mlstm_m[j], state_rwkv_S[j], state_rwkv_shift[j], *small)
            st_p_even.append(sp)
            st_s_even.append(ss)
            wo = w_out_even[j].astype(BF16)
            terms_p = [(ha, None, wo[:A_V], None), (hb_p, g, wo[A_V:], Lp)]
            terms_s = [(ha, None, wo[:A_V], None), (hb_s, g, wo[A_V:], 0)]
            if x is None:
                xo = matmul_res(terms_p, xp, 0, n_p, None, n_tot)
                x = matmul_res(terms_s, xs, n_p, n_s, xo, n_tot)
            else:
                xo = matmul_res(terms_p, x, 0, n_p)
                x = matmul_res(terms_s, x, n_p, n_s, xo)
        else:
            zin = norm_matmul(x, norm_mix[i], _odd_w_in(w_in_odd[j]), 768)
            small = [t[j] for t in odd_small]
            mix, sp = ssd_mixer(zin, 0, bp, Lp, None, *zeros_odd, *small)
            mix, ss = ssd_mixer(zin, n_p, bs, Ls, mix, state_ssm[j], state_conv[j], *small)
            st_p_odd.append(sp)
            st_s_odd.append(ss)
            x = matmul_res([(mix, None, w_out_odd[j].astype(BF16), None)], x)
        x = ffn(x, norm_ffn[i], w_ffn_up[i].astype(BF16), w_ffn_down[i].astype(BF16))
        wg, wp = w_ple_gate[i].astype(BF16), w_ple_proj[i].astype(BF16)
        xo = ple(x, norm_ple[i], wg, pp[i], wp, 0)
        x = ple(x, norm_ple[i], wg, ps[i], wp, n_p, xo)
    y_prompt = final_norm(x, norm_final, 0, n_p).reshape(bp, Lp, D_MODEL)
    y_sample = final_norm(x, norm_final, n_p, n_s).reshape(bs, Ls, D_MODEL)
    stack = lambda sts, idx: jnp.stack([s[idx] for s in sts])
    return (y_prompt, y_sample,
            stack(st_p_even, 0), stack(st_p_even, 1), stack(st_p_even, 2), stack(st_p_even, 3),
            stack(st_p_even, 4), stack(st_p_odd, 0), stack(st_p_odd, 1),
            stack(st_s_even, 0), stack(st_s_even, 1), stack(st_s_even, 2), stack(st_s_even, 3),
            stack(st_s_even, 4), stack(st_s_odd, 0), stack(st_s_odd, 1))
```

```python
import math
import functools
import jax
import jax.numpy as jnp
from jax import lax
from jax.experimental import pallas as pl
from jax.experimental.pallas import tpu as pltpu

D_MODEL = 1024
DEPTH = 4
F32 = jnp.float32
BF16 = jnp.bfloat16
EPS = 1e-6
N_EVEN = (DEPTH + 1) // 2
N_ODD = DEPTH // 2
D_FF = 4 * D_MODEL
PLE_DIM = 256

A_HEADS = 4
A_DK = D_MODEL // 8
A_DV = D_MODEL // 8
A_CHUNK = 64
A_QK = A_HEADS * A_DK
A_V = A_HEADS * A_DV
A_COLS = 2 * A_QK + 2 * A_V + 2 * A_HEADS

B_HEADS = 8
B_DH = 64
B_W = B_HEADS * B_DH
B_W_RANK = 64
B_A_RANK = 64
B_G_RANK = 128
B_COLS = 3 * B_W + B_W_RANK + B_A_RANK + B_G_RANK
B_DECAY_OFFSET = 0.5
B_GN_EPS = 64e-5

EVEN_COLS = A_COLS + B_COLS
EVEN_OUT = A_V + B_W

C_INNER = 2 * D_MODEL
C_HEADDIM = 64
C_HEADS = C_INNER // C_HEADDIM
C_GROUPS = 4
C_HPG = C_HEADS // C_GROUPS
C_STATE = 128
C_CONV = 4
C_CHUNK = 128
C_CONV_DIM = C_INNER + 2 * C_GROUPS * C_STATE
ODD_COLS = C_INNER + C_CONV_DIM + C_HEADS

LANES = 128
SUBLANES = 8
VMEM_LIMIT = 56 * 1024 * 1024
TOKEN_TILE = 512

EVEN_GATE_PAD = 256
EVEN_N = B_COLS + EVEN_GATE_PAD + 2 * A_QK + 2 * A_V
EVEN_G_OFF = B_COLS
EVEN_A_OFF = B_COLS + EVEN_GATE_PAD
ODD_DT_PAD = 256
ODD_N = C_INNER + C_CONV_DIM + ODD_DT_PAD
ODD_DT_OFF = C_INNER + C_CONV_DIM


def _cparams(sem):
    return pltpu.CompilerParams(dimension_semantics=sem, vmem_limit_bytes=VMEM_LIMIT)


def _rms(x, g):
    return x * lax.rsqrt(jnp.mean(x * x, axis=-1, keepdims=True) + EPS) * g


def _resident(shape):
    nd = len(shape)
    return pl.BlockSpec(shape, lambda *_: (0,) * nd, pipeline_mode=pl.Buffered(1))


def _rows(width):
    return pl.BlockSpec((TOKEN_TILE, width), lambda i: (i, 0))


def _fill_into(body, n_in, into, out_idx, into2=None, out_idx2=None):
    pairs = [(a, o) for a, o in ((into, out_idx), (into2, out_idx2)) if a is not None]
    if not pairs:
        return body, [], [], {}

    def skipping(*refs):
        return body(*refs[:n_in], *refs[n_in + len(pairs):])

    return (skipping, [pl.BlockSpec(memory_space=pl.ANY)] * len(pairs), [a for a, _ in pairs],
            {n_in + i: o for i, (_, o) in enumerate(pairs)})


def _norm_matmul_kernel(x_ref, g_ref, w_ref, o_ref, *, tn):
    xn = _rms(x_ref[...], g_ref[...]).astype(BF16)
    for n0 in range(0, w_ref.shape[1], tn):
        o_ref[:, n0:n0 + tn] = jnp.dot(xn, w_ref[:, n0:n0 + tn], preferred_element_type=F32)


def norm_matmul(x, g, w, tn, row0=0, total=None, into=None):
    m, n = x.shape[0], w.shape[1]
    total = m if total is None else total
    t0 = row0 // TOKEN_TILE
    in_specs = [_rows(D_MODEL), _resident((1, D_MODEL)), _resident(w.shape)]
    body, xspecs, xargs, aliases = _fill_into(functools.partial(_norm_matmul_kernel, tn=tn), len(in_specs), into, 0)
    return pl.pallas_call(
        body,
        grid=(m // TOKEN_TILE,),
        in_specs=in_specs + xspecs,
        out_specs=pl.BlockSpec((TOKEN_TILE, n), lambda i: (t0 + i, 0)),
        out_shape=jax.ShapeDtypeStruct((total, n), F32),
        input_output_aliases=aliases,
        compiler_params=_cparams(("arbitrary",)),
        name="norm_matmul",
    )(x, g.reshape(1, D_MODEL), w, *xargs)


FFN_CHUNK = 512


def _matmul_res_kernel(*refs, kinds):
    x_ref, o_ref = refs[-2], refs[-1]
    acc = x_ref[...]
    pos = 0
    for has_gate, channel_major in kinds:
        a = refs[pos][...]
        if channel_major:
            a = a.T
        if has_gate:
            a = a * refs[pos + 1][...]
        w_ref = refs[pos + 1 + has_gate]
        pos += 2 + has_gate
        acc = acc + jnp.dot(a.astype(BF16), w_ref[...], preferred_element_type=F32)
    o_ref[...] = acc


def matmul_res(terms, x, row0=0, nrows=None, into=None, total=None):
    m = x.shape[0] if total is None else total
    nrows = m if nrows is None else nrows
    t0 = row0 // TOKEN_TILE
    rows = lambda width: pl.BlockSpec((TOKEN_TILE, width), lambda i: (t0 + i, 0))
    xspec = rows(D_MODEL) if total is None else _rows(D_MODEL)
    specs, args, kinds = [], [], []
    for a, gate, w, seq_len in terms:
        if seq_len is None:
            specs.append(rows(a.shape[1]))
        elif seq_len == 0:
            specs.append(pl.BlockSpec((TOKEN_TILE, a.shape[1]), lambda i: (i, 0)))
        else:
            assert seq_len % TOKEN_TILE == 0
            per_seq = seq_len // TOKEN_TILE
            specs.append(pl.BlockSpec((None, a.shape[1], TOKEN_TILE), lambda i: (i // per_seq, 0, i % per_seq)))
        specs += ([rows(w.shape[0])] if gate is not None else []) + [_resident(w.shape)]
        args += [a] + ([gate] if gate is not None else []) + [w]
        kinds.append((int(gate is not None), bool(seq_len)))
    specs.append(xspec)
    body, xspecs, xargs, aliases = _fill_into(
        functools.partial(_matmul_res_kernel, kinds=tuple(kinds)), len(specs), into, 0)
    return pl.pallas_call(
        body,
        grid=(nrows // TOKEN_TILE,),
        in_specs=specs + xspecs,
        out_specs=rows(D_MODEL),
        out_shape=jax.ShapeDtypeStruct((m, D_MODEL), F32),
        input_output_aliases=aliases,
        compiler_params=_cparams(("arbitrary",)),
        name="matmul_res",
    )(*args, x, *xargs)


def _ffn_kernel(x_ref, g_ref, wu_ref, wd_ref, o_ref):
    x = x_ref[...]
    xn = _rms(x, g_ref[...]).astype(BF16)
    acc = x
    for c0 in range(0, D_FF, FFN_CHUNK):
        h = jnp.dot(xn, wu_ref[:, c0:c0 + FFN_CHUNK], preferred_element_type=F32)
        h = jnp.square(jnp.maximum(h, 0.0)).astype(BF16)
        acc = acc + jnp.dot(h, wd_ref[c0:c0 + FFN_CHUNK, :], preferred_element_type=F32)
    o_ref[...] = acc


FFN_TILE = 1024


def ffn(x, g, wu, wd):
    m = x.shape[0]
    rows = pl.BlockSpec((FFN_TILE, D_MODEL), lambda i: (i, 0))
    return pl.pallas_call(
        _ffn_kernel,
        grid=(m // FFN_TILE,),
        in_specs=[rows, _resident((1, D_MODEL)), _resident(wu.shape), _resident(wd.shape)],
        out_specs=rows,
        out_shape=jax.ShapeDtypeStruct((m, D_MODEL), F32),
        compiler_params=_cparams(("arbitrary",)),
        name="ffn",
    )(x, g.reshape(1, D_MODEL), wu, wd)


def _ple_kernel(x_ref, g_ref, wg_ref, p_ref, wp_ref, o_ref):
    x = x_ref[...]
    xn = _rms(x, g_ref[...]).astype(BF16)
    gate = jax.nn.sigmoid(jnp.dot(xn, wg_ref[...], preferred_element_type=F32))
    proj = jnp.dot(p_ref[...].astype(BF16), wp_ref[...], preferred_element_type=F32)
    o_ref[...] = x + proj * gate


def ple(x, g, wg, p, wp, row0, into=None):
    t0 = row0 // TOKEN_TILE
    rows = pl.BlockSpec((TOKEN_TILE, D_MODEL), lambda i: (t0 + i, 0))
    in_specs = [rows, _resident((1, D_MODEL)), _resident(wg.shape), _rows(PLE_DIM), _resident(wp.shape)]
    body, xspecs, xargs, aliases = _fill_into(_ple_kernel, len(in_specs), into, 0)
    return pl.pallas_call(
        body,
        grid=(p.shape[0] // TOKEN_TILE,),
        in_specs=in_specs + xspecs,
        out_specs=rows,
        out_shape=jax.ShapeDtypeStruct(x.shape, F32),
        input_output_aliases=aliases,
        compiler_params=_cparams(("arbitrary",)),
        name="ple",
    )(x, g.reshape(1, D_MODEL), wg, p, wp, *xargs)


def _final_norm_kernel(x_ref, g_ref, o_ref):
    o_ref[...] = _rms(x_ref[...], g_ref[...])


def final_norm(x, g, row0, nrows):
    t0 = row0 // TOKEN_TILE
    return pl.pallas_call(
        _final_norm_kernel,
        grid=(nrows // TOKEN_TILE,),
        in_specs=[pl.BlockSpec((TOKEN_TILE, D_MODEL), lambda i: (t0 + i, 0)), _resident((1, D_MODEL))],
        out_specs=_rows(D_MODEL),
        out_shape=jax.ShapeDtypeStruct((nrows, D_MODEL), F32),
        compiler_params=_cparams(("arbitrary",)),
        name="final_norm",
    )(x, g.reshape(1, D_MODEL))


def _rwkv_operands(k, a, r, kk_t, ka_t, rk_t):
    kk = k * kk_t
    kk = kk * lax.rsqrt(jnp.maximum(jnp.sum(kk * kk, axis=1, keepdims=True), 1e-24))
    k2 = k * (1.0 + (a - 1.0) * ka_t)
    return -kk, kk * a, k2, jnp.sum(r * k2 * rk_t, axis=1)


def _rwkv_step(s_ref, tile0, ntiles, r, w, k, a, b, vrows):
    ys = []
    for lt in range(ntiles):
        rows = slice((tile0 + lt) * B_DH, (tile0 + lt + 1) * B_DH)
        s = s_ref[rows, :]
        sa = jnp.sum(s * a, axis=0, keepdims=True)
        s = s * w + sa * b + vrows[lt:lt + 1] * k
        s_ref[rows, :] = s
        ys.append(jnp.sum(s * r, axis=0, keepdims=True))
    return jnp.concatenate(ys, axis=0)


LONG_T = 128
LONG_NLT = B_DH // 2
KEY_PITCH = B_DH + SUBLANES
VAL_PITCH = LONG_NLT + SUBLANES


def _rwkv_scan_long_kernel(r_ref, w_ref, k_ref, a_ref, v_ref, kkt_ref, kat_ref, rkt_ref, lnw_ref, lnb_ref, s0_ref,
                           yt_ref, s_ref, or_s, ow_s, ok_s, oa_s, av_s, v_s, y_s):
    nb = yt_ref.shape[0]

    @pl.when(pl.program_id(0) == 0)
    def _():
        s_ref[...] = s0_ref[...]
        for scr in (or_s, ow_s, ok_s, oa_s, av_s, v_s, y_s):
            scr[...] = jnp.zeros(scr.shape, F32)

    lo = lax.broadcasted_iota(jnp.int32, (LONG_T, LANES), 1) < LANES // 2

    def tile(ref, c):
        m = ref[c]
        return jnp.concatenate([m, m], axis=0).T

    def relayout(c, carry):
        for ref, dst in ((r_ref, or_s), (w_ref, ow_s), (k_ref, ok_s), (a_ref, oa_s)):
            dst[pl.ds(c, LONG_T, stride=KEY_PITCH), :] = tile(ref, c)
            dst[pl.ds(c + LONG_NLT, LONG_T, stride=KEY_PITCH), :] = tile(ref, c + LONG_NLT)
        v_s[pl.ds(c, LONG_T, stride=VAL_PITCH), :] = jnp.where(lo, tile(v_ref, c), tile(v_ref, c + LONG_NLT))
        return carry

    lax.fori_loop(0, LONG_NLT, relayout, 0)

    def unpad(scr, pitch, n):
        return scr[...].reshape(LONG_T, pitch, LANES)[:, :n]

    def pad(x, pitch):
        zeros = jnp.zeros((LONG_T, pitch - x.shape[1], LANES), F32)
        return jnp.concatenate([x, zeros], axis=1).reshape(LONG_T * pitch, LANES)

    av, bv, k2, bonus = _rwkv_operands(unpad(ok_s, KEY_PITCH, B_DH), unpad(oa_s, KEY_PITCH, B_DH),
                                       unpad(or_s, KEY_PITCH, B_DH), kkt_ref[...], kat_ref[...], rkt_ref[...])
    av_s[...] = pad(av, KEY_PITCH)
    oa_s[...] = pad(bv, KEY_PITCH)
    ok_s[...] = pad(k2, KEY_PITCH)

    def step(t, carry):
        kr = pl.ds(pl.multiple_of(t * KEY_PITCH, SUBLANES), B_DH)
        vr = pl.ds(pl.multiple_of(t * VAL_PITCH, SUBLANES), LONG_NLT)
        y_s[vr, :] = _rwkv_step(s_ref, 0, LONG_NLT, or_s[kr, :], ow_s[kr, :], ok_s[kr, :], av_s[kr, :],
                                oa_s[kr, :], v_s[vr, :])
        return carry

    lax.fori_loop(0, LONG_T, step, 0)

    def head_sum(x):
        tot = jnp.sum(x, axis=1)
        return tot + pltpu.roll(tot, LANES // 2, axis=1)

    y = unpad(y_s, VAL_PITCH, LONG_NLT)
    d = y - (head_sum(y) * (1.0 / B_DH))[:, None, :]
    var = head_sum(d * d) * (1.0 / B_DH)
    y = (d * lax.rsqrt(var + B_GN_EPS)[:, None, :] * lnw_ref[...] + lnb_ref[...]
         + bonus[:, None, :] * unpad(v_s, VAL_PITCH, LONG_NLT))
    y_s[...] = pad(y, VAL_PITCH)

    def relayout_out(lt, carry):
        yt = y_s[pl.ds(lt, LONG_T, stride=VAL_PITCH), :].T
        for i2 in range(2):
            for b in range(nb):
                row = i2 * (LANES // 2) + b * B_HEADS
                yt_ref[b, pl.ds(lt + LONG_NLT * i2, B_HEADS, stride=B_DH), :] = yt[row:row + B_HEADS, :]
        return carry

    lax.fori_loop(0, LONG_NLT, relayout_out, 0)


def rwkv_scan_long(rc, wc, kc, ac, vc, S0, k_k, k_a, r_k, ln_w, ln_b):
    _, bh, L = rc.shape
    b = bh // B_HEADS
    assert bh * 2 == LANES and L % LONG_T == 0
    per_key = lambda p: jnp.tile(p.reshape(B_HEADS, B_DH).T, (1, LANES // B_HEADS))
    per_val = lambda p: jnp.concatenate(
        [jnp.tile(p.reshape(B_HEADS, 2, LONG_NLT)[:, i2].T, (1, b)) for i2 in range(2)], axis=1)
    s0 = S0.reshape(b, B_HEADS, 2, LONG_NLT, B_DH).transpose(3, 4, 2, 0, 1).reshape(LONG_NLT * B_DH, LANES)
    blk = pl.BlockSpec((B_DH, bh, LONG_T), lambda c: (0, 0, c), pipeline_mode=pl.Buffered(1))
    oblk = pl.BlockSpec((b, B_W, LONG_T), lambda c: (0, 0, c))
    const = lambda shape: pl.BlockSpec(shape, lambda c: (0,) * len(shape))
    big = pltpu.VMEM((LONG_T * KEY_PITCH, LANES), F32)
    small = pltpu.VMEM((LONG_T * VAL_PITCH, LANES), F32)
    yt, s = pl.pallas_call(
        _rwkv_scan_long_kernel,
        grid=(L // LONG_T,),
        in_specs=[blk] * 5 + [const((B_DH, LANES))] * 3 + [const((LONG_NLT, LANES))] * 2
        + [const((LONG_NLT * B_DH, LANES))],
        out_specs=[oblk, const((LONG_NLT * B_DH, LANES))],
        out_shape=[jax.ShapeDtypeStruct((b, B_W, L), F32), jax.ShapeDtypeStruct(s0.shape, F32)],
        scratch_shapes=[big] * 5 + [small, small],
        compiler_params=_cparams(("arbitrary",)),
        name="rwkv_scan_long",
    )(rc, wc, kc, ac, vc, per_key(k_k), per_key(k_a), per_key(r_k), per_val(ln_w), per_val(ln_b), s0)
    s = s.reshape(LONG_NLT, B_DH, 2, b, B_HEADS).transpose(3, 4, 2, 0, 1).reshape(b, B_HEADS, B_DH, B_DH)
    return yt, s


def _rwkv_scan_short_kernel(r_ref, w_ref, k_ref, a_ref, v_ref, kkt_ref, kat_ref, rkt_ref, lnw_ref, lnb_ref, s0_ref,
                            y_ref, sout_ref, s_s, or_s, ow_s, ok_s, oa_s, av_s, ov_s, y_s, *, L):
    nt = 2 * B_DH
    for q in range(nt * B_DH // LANES):
        s_s[q * LANES:(q + 1) * LANES, :] = s0_ref[:, q * LANES:(q + 1) * LANES].T
    for ref, dst in ((r_ref, or_s), (w_ref, ow_s), (k_ref, ok_s), (a_ref, oa_s), (v_ref, ov_s)):
        for t in range(L):
            dst[t] = ref[pl.ds(t, LANES, stride=L), :].T.reshape(2, B_DH, LANES)
    shape3 = (L * 2, B_DH, LANES)
    tiles = lambda ref: jnp.concatenate([ref[...]] * L, axis=0)
    av, bv, k2, bonus = _rwkv_operands(ok_s[...].reshape(shape3), oa_s[...].reshape(shape3),
                                       or_s[...].reshape(shape3), tiles(kkt_ref), tiles(kat_ref), tiles(rkt_ref))
    av_s[...] = av.reshape(L, 2, B_DH, LANES)
    oa_s[...] = bv.reshape(L, 2, B_DH, LANES)
    ok_s[...] = k2.reshape(L, 2, B_DH, LANES)

    def step(t, carry):
        for h in range(2):
            y_s[t, h] = _rwkv_step(s_s, h * B_DH, B_DH, or_s[t, h], ow_s[t, h], ok_s[t, h], av_s[t, h],
                                   oa_s[t, h], ov_s[t, h])
        return carry

    lax.fori_loop(0, L, step, 0)

    y = y_s[...]
    d = y - jnp.mean(y, axis=2, keepdims=True)
    var = jnp.mean(d * d, axis=2, keepdims=True)
    y = (d * lax.rsqrt(var + B_GN_EPS) * lnw_ref[...] + lnb_ref[...]
         + bonus.reshape(L, 2, 1, LANES) * ov_s[...])
    for t in range(L):
        y_ref[pl.ds(t, LANES, stride=L), :] = y[t].reshape(nt, LANES).T
    for q in range(nt * B_DH // LANES):
        sout_ref[:, q * LANES:(q + 1) * LANES] = s_s[q * LANES:(q + 1) * LANES, :].T


def rwkv_scan_short(r, w, k, a, v, S0, L, k_k, k_a, r_k, ln_w, ln_b):
    n = r.shape[0]
    b = n // L
    assert b == LANES
    npair = B_HEADS // 2
    wide = lambda p: jnp.broadcast_to(p.reshape(npair, 2, B_DH, 1), (npair, 2, B_DH, LANES))
    s0 = S0.reshape(b, B_HEADS * B_DH * B_DH)
    blk = pl.BlockSpec((n, LANES), lambda p: (0, p))
    cblk = pl.BlockSpec((None, 2, B_DH, LANES), lambda p: (p, 0, 0, 0))
    sblk = pl.BlockSpec((b, 2 * B_DH * B_DH), lambda p: (0, p))
    op = pltpu.VMEM((L, 2, B_DH, LANES), F32)
    y, s = pl.pallas_call(
        functools.partial(_rwkv_scan_short_kernel, L=L),
        grid=(npair,),
        in_specs=[blk] * 5 + [cblk] * 5 + [sblk],
        out_specs=[blk, sblk],
        out_shape=[jax.ShapeDtypeStruct((n, B_W), F32), jax.ShapeDtypeStruct(s0.shape, F32)],
        scratch_shapes=[pltpu.VMEM((2 * B_DH * B_DH, LANES), F32)] + [op] * 7,
        compiler_params=_cparams(("arbitrary",)),
        name="rwkv_scan_short",
    )(r, w, k, a, v, wide(k_k), wide(k_a), wide(r_k), wide(ln_w), wide(ln_b), s0)
    return y, s.reshape(b, B_HEADS, B_DH, B_DH)


RWKV_PREP_ROWS = 256
RWKV_SHORT_ROWS = 64
B_LORA_OFF = 3 * B_W


def _rwkv_prep_kernel(z_ref, sh0_ref, mu_ref, wwa_ref, g2_ref, w0_ref, a0_ref,
                      r_ref, w_ref, k_ref, a_ref, v_ref, g_ref, sh_ref, *scratch, T, nseq, channel_major):
    @pl.when(pl.program_id(1) == 0)
    def _():
        sh_ref[...] = sh0_ref[...]

    z = z_ref[...]
    rowid = lax.broadcasted_iota(jnp.int32, (z.shape[0], 1), 0)
    zprev = pltpu.roll(z, 1, axis=0)
    for u in range(nseq):
        zprev = jnp.where(rowid == u * T, sh_ref[u], zprev)
    for u in range(nseq):
        sh_ref[u] = z[(u + 1) * T - 1:(u + 1) * T, :]
    zs = z + (zprev - z) * mu_ref[...]
    r = zs[:, :B_W]
    k = zs[:, B_W:2 * B_W]
    lora = zs[:, B_LORA_OFF:B_LORA_OFF + LANES]
    lane = lax.broadcasted_iota(jnp.int32, lora.shape, 1)
    lora = jnp.where(lane < B_W_RANK, jnp.tanh(lora), lora).astype(BF16)
    wa = jnp.dot(lora, wwa_ref[...], preferred_element_type=F32)
    w_log = -jax.nn.softplus(-(w0_ref[...] + wa[:, :B_W])) - B_DECAY_OFFSET
    a = jax.nn.sigmoid(a0_ref[...] + wa[:, B_W:])
    zg = zs[:, B_LORA_OFF + LANES:B_LORA_OFF + LANES + B_G_RANK]
    def emit(ref, x):
        if not channel_major:
            ref[...] = x
            return
        xt_s = scratch[0]
        xt = x.T
        for j in range(xt_s.shape[0]):
            for h in range(B_HEADS):
                xt_s[j, h * KEY_PITCH:h * KEY_PITCH + B_DH, :] = xt[h * B_DH:(h + 1) * B_DH, j * LANES:(j + 1) * LANES]
        for c in range(B_DH):
            for j in range(xt_s.shape[0]):
                ref[c, :, j * LANES:(j + 1) * LANES] = xt_s[j, pl.ds(c, B_HEADS, stride=KEY_PITCH), :]

    emit(r_ref, r)
    emit(w_ref, jnp.exp(-jnp.exp(w_log)))
    emit(k_ref, k)
    emit(a_ref, a)
    emit(v_ref, zs[:, 2 * B_W:3 * B_W])
    g_ref[...] = jnp.dot(jax.nn.sigmoid(zg).astype(BF16), g2_ref[...], preferred_element_type=F32)


def rwkv_prep(zin, row0, b, L, gate_into, shift0, mu, w0, w2, a0, a2, g2):
    assert B_W_RANK + B_A_RANK == LANES
    T = math.gcd(L, RWKV_PREP_ROWS)
    channel_major = T == RWKV_PREP_ROWS
    nseq = 1 if channel_major else RWKV_SHORT_ROWS // T
    assert nseq == 1 or (L == T and b % nseq == 0)
    R = T * nseq
    nchunk = L // T
    blk0 = row0 // R
    wwa = jnp.zeros((LANES, 2 * B_W), F32).at[:B_W_RANK, :B_W].set(w2).at[B_W_RANK:, B_W:].set(a2).astype(BF16)
    row = lambda v: v.reshape(1, -1)
    if channel_major:
        blk = pl.BlockSpec((B_DH, B_HEADS, R), lambda i, c: (0, i, c))
        oshape = jax.ShapeDtypeStruct((B_DH, b * B_HEADS, L), F32)
    else:
        blk = pl.BlockSpec((R, B_W), lambda i, c: (i * nchunk + c, 0))
        oshape = jax.ShapeDtypeStruct((b * L, B_W), F32)
    gblk = pl.BlockSpec((R, B_W), lambda i, c: (blk0 + i * nchunk + c, 0))
    shspec = pl.BlockSpec((nseq, 1, B_COLS), lambda i, c: (i, 0, 0))
    const = lambda shape: pl.BlockSpec(shape, lambda i, c: (0,) * len(shape))
    in_specs = [pl.BlockSpec((R, B_COLS), lambda i, c: (blk0 + i * nchunk + c, 0)), shspec, const((1, B_COLS)),
                const((LANES, 2 * B_W)), const((B_G_RANK, B_W)), const((1, B_W)), const((1, B_W))]
    body, xspecs, xargs, aliases = _fill_into(
        functools.partial(_rwkv_prep_kernel, T=T, nseq=nseq, channel_major=channel_major),
        len(in_specs), gate_into, 5)
    outs = pl.pallas_call(
        body,
        grid=(b // nseq, nchunk),
        in_specs=in_specs + xspecs,
        out_specs=[blk] * 5 + [gblk, shspec],
        out_shape=[oshape] * 5
        + [jax.ShapeDtypeStruct((zin.shape[0], B_W), F32), jax.ShapeDtypeStruct((b, 1, B_COLS), F32)],
        input_output_aliases=aliases,
        scratch_shapes=[pltpu.VMEM((R // LANES, B_HEADS * KEY_PITCH, LANES), F32)] if channel_major else [],
        compiler_params=_cparams(("arbitrary", "arbitrary")),
        name="rwkv_prep",
    )(zin, shift0.reshape(b, 1, B_COLS), row(mu), wwa, g2.astype(BF16), row(w0), row(a0), *xargs)
    return outs[:5], outs[5], outs[6].reshape(b, B_COLS)


MLSTM_ROWS = 64
HIGHEST = lax.Precision.HIGHEST


def _mlstm_kernel(q_ref, k_ref, v_ref, o_ref, g_ref, gt_ref, brow_ref, bcol_ref, nw_ref,
                  c0_ref, n0_ref, m0_ref, h_ref, c_ref, n_ref, m_ref, *, T, nseq):
    R = MLSTM_ROWS

    @pl.when(pl.program_id(1) == 0)
    def _():
        c_ref[...] = c0_ref[...]
        n_ref[...] = n0_ref[...]
        m_ref[...] = m0_ref[...]

    shift = T.bit_length() - 1
    ri = lax.broadcasted_iota(jnp.int32, (R, R), 0)
    ci = lax.broadcasted_iota(jnp.int32, (R, R), 1)
    mask = (ci <= ri) & (jnp.right_shift(ri, shift) == jnp.right_shift(ci, shift))
    lmat = mask.astype(F32)
    rowid = lax.broadcasted_iota(jnp.int32, (R, 1), 0)
    rsel = [(rowid >= u * T) & (rowid < (u + 1) * T) for u in range(nseq)]

    g = g_ref[...] + brow_ref[...]
    lane = lax.broadcasted_iota(jnp.int32, g.shape, 1)
    glog = jnp.where((lane >= A_HEADS) & (lane < 2 * A_HEADS), jax.nn.log_sigmoid(g), g)
    gt = gt_ref[...] + bcol_ref[...]
    sub = lax.broadcasted_iota(jnp.int32, gt.shape, 0)
    gtlog = jnp.where(sub >= A_HEADS, jax.nn.log_sigmoid(gt), gt)
    bc_col = jnp.dot(lmat, glog, precision=HIGHEST, preferred_element_type=F32)
    bc_row = lax.dot_general(gtlog, lmat, (((1,), (1,)), ((), ())), precision=HIGHEST,
                             preferred_element_type=F32)
    lane_m = lax.broadcasted_iota(jnp.int32, (1, LANES), 1)
    m_old = [m_ref[u] for u in range(nseq)]
    m_out = [jnp.zeros((1, LANES), F32) for _ in range(nseq)]

    for h in range(A_HEADS):
        hs = slice(h * A_DK, (h + 1) * A_DK)
        bcc = bc_col[:, A_HEADS + h:A_HEADS + h + 1]
        bcr = bc_row[A_HEADS + h:A_HEADS + h + 1, :]
        lir = gtlog[h:h + 1, :]
        lic = glog[:, h:h + 1]
        m_u = [m_old[u][:, h:h + 1] for u in range(nseq)]
        m_col = m_u[0]
        for u in range(1, nseq):
            m_col = jnp.where(rsel[u], m_u[u], m_col)
        dmat = jnp.where(mask, bcc - bcr + lir, -jnp.inf)
        inter = bcc + m_col
        mt = jnp.maximum(inter, jnp.max(dmat, axis=1, keepdims=True))
        p = jnp.exp(dmat - mt)
        qh = q_ref[:, hs] * (A_DK ** -0.5)
        kh = k_ref[:, hs]
        qb, kb, vb = qh.astype(BF16), kh.astype(BF16), v_ref[:, hs].astype(BF16)
        wq = lax.dot_general(qb, kb, (((1,), (1,)), ((), ())), preferred_element_type=F32) * p
        wi = jnp.exp(inter - mt)
        c_old = [c_ref[u, h] for u in range(nseq)]
        n_old = [n_ref[u, h:h + 1, :] for u in range(nseq)]
        qc = jnp.dot(qb, c_old[0].astype(BF16), preferred_element_type=F32)
        qn = jnp.sum(qh * n_old[0], axis=1, keepdims=True)
        for u in range(1, nseq):
            qc = jnp.where(rsel[u], jnp.dot(qb, c_old[u].astype(BF16), preferred_element_type=F32), qc)
            qn = jnp.where(rsel[u], jnp.sum(qh * n_old[u], axis=1, keepdims=True), qn)
        num = jnp.dot(wq.astype(BF16), vb, preferred_element_type=F32) + wi * qc
        den = jnp.sum(wq, axis=1, keepdims=True) + wi * qn
        hh = num / jnp.maximum(jnp.abs(den), jnp.exp(-mt))
        hh = hh * lax.rsqrt(jnp.mean(hh * hh, axis=-1, keepdims=True) + EPS)
        h_ref[:, hs] = hh * nw_ref[:, hs] * jax.nn.sigmoid(o_ref[:, hs])
        for u in range(nseq):
            b_last = bcc[(u + 1) * T - 1:(u + 1) * T, :]
            gs = b_last - bcc + lic
            gmax = jnp.max(gs if nseq == 1 else jnp.where(rsel[u], gs, -jnp.inf), axis=0, keepdims=True)
            m_new = jnp.maximum(b_last + m_u[u], gmax)
            decay = jnp.exp(b_last + m_u[u] - m_new)
            ws = jnp.exp(gs - m_new)
            if nseq > 1:
                ws = jnp.where(rsel[u], ws, 0.0)
            kw = kh * ws
            c_ref[u, h] = decay * c_old[u] + lax.dot_general(
                kw.astype(BF16), vb, (((0,), (0,)), ((), ())), preferred_element_type=F32)
            n_ref[u, h:h + 1, :] = decay * n_old[u] + jnp.sum(kw, axis=0, keepdims=True)
            m_out[u] = jnp.where(lane_m == h, m_new, m_out[u])
    for u in range(nseq):
        m_ref[u] = m_out[u]


def mlstm(zin, gates_t, row0, b, L, h_into, layer, c_into, b_i, b_f, m_norm, C0, n0, m0):
    R = MLSTM_ROWS
    T = math.gcd(L, R)
    nseq = R // T
    assert nseq == 1 or (L == T and b % nseq == 0)
    nchunk = L // T
    blk0 = row0 // R
    bias = jnp.concatenate([b_i, b_f])
    bias_row = jnp.zeros((1, LANES), F32).at[0, :2 * A_HEADS].set(bias)
    bias_col = bias.reshape(2 * A_HEADS, 1)
    m0p = jnp.zeros((b, 1, LANES), F32).at[:, 0, :A_HEADS].set(m0)
    blk = lambda i, c: blk0 + i * nchunk + c
    rowblk = lambda col: pl.BlockSpec((R, A_QK), lambda i, c: (blk(i, c), EVEN_A_OFF // A_QK + col))
    cspec = pl.BlockSpec((nseq, A_HEADS, A_DK, A_DV), lambda i, c: (i, 0, 0, 0))
    cout = pl.BlockSpec((None, nseq, A_HEADS, A_DK, A_DV), lambda i, c: (layer, i, 0, 0, 0))
    nspec = pl.BlockSpec((nseq, A_HEADS, A_DK), lambda i, c: (i, 0, 0))
    mspec = pl.BlockSpec((nseq, 1, LANES), lambda i, c: (i, 0, 0))
    const = lambda shape: pl.BlockSpec(shape, lambda i, c: (0,) * len(shape))
    in_specs = [rowblk(0), rowblk(1), rowblk(2), rowblk(3),
                pl.BlockSpec((R, LANES), lambda i, c: (blk(i, c), EVEN_G_OFF // LANES)),
                pl.BlockSpec((None, 2 * A_HEADS, R), lambda i, c: (blk(i, c), 0, 0)),
                const((1, LANES)), const((2 * A_HEADS, 1)), const((1, A_V)),
                cspec, nspec, mspec]
    body, xspecs, xargs, aliases = _fill_into(
        functools.partial(_mlstm_kernel, T=T, nseq=nseq), len(in_specs), h_into, 0, c_into, 1)
    h, C, n, m = pl.pallas_call(
        body,
        grid=(b // nseq, nchunk),
        in_specs=in_specs + xspecs,
        out_specs=[pl.BlockSpec((R, A_V), lambda i, c: (blk(i, c), 0)), cout, nspec, mspec],
        out_shape=[jax.ShapeDtypeStruct((zin.shape[0], A_V), F32),
                   jax.ShapeDtypeStruct((N_EVEN,) + C0.shape, F32), jax.ShapeDtypeStruct(n0.shape, F32),
                   jax.ShapeDtypeStruct(m0p.shape, F32)],
        input_output_aliases=aliases,
        compiler_params=_cparams(("arbitrary", "arbitrary")),
        name="mlstm",
    )(zin, zin, zin, zin, zin, gates_t, bias_row, bias_col, m_norm.reshape(1, A_V), C0, n0, m0p, *xargs)
    return h, C, n, m[:, 0, :A_HEADS]


def even_mixer(zin, gates_t, row0, b, L, into, layer, c_into, C0, n0, m0, S0, shift0, b_i, b_f, m_norm,
               mu, w0, w2, a0, a2, g2, k_k, k_a, r_k, ln_w, ln_b):
    hA, C, n, m = mlstm(zin, gates_t, row0, b, L, into[0], layer, c_into, b_i, b_f, m_norm, C0, n0, m0)
    ops, g, shift = rwkv_prep(zin, row0, b, L, into[1], shift0, mu, w0, w2, a0, a2, g2)
    if ops[0].ndim == 3:
        hB, S = rwkv_scan_long(*ops, S0, k_k, k_a, r_k, ln_w, ln_b)
    else:
        hB, S = rwkv_scan_short(*ops, S0, L, k_k, k_a, r_k, ln_w, ln_b)
    return hA, hB, g, (C, n, m, S, shift)


C_PAIRS = C_HEADS // 2
C_GROUP_W = C_INNER // C_GROUPS
C_BC_W = 2 * C_GROUPS * C_STATE
SSD_SHORT_ROWS = 32


def _ssd_kernel(z_ref, x_ref, bc_ref, dt_ref, dtt_ref, cw_ref, cb_ref, dtb_ref, dtbt_ref, al_ref, alt_ref,
                dsk_ref, nw_ref, s0_ref, cv0_ref, y_ref, s_ref, cv_ref, stg_ref, *, T, nseq):
    R = T * nseq

    @pl.when(pl.program_id(1) == 0)
    def _():
        s_ref[...] = s0_ref[...]
        cv_ref[...] = cv0_ref[...]

    shift = T.bit_length() - 1
    ri = lax.broadcasted_iota(jnp.int32, (R, R), 0)
    ci = lax.broadcasted_iota(jnp.int32, (R, R), 1)
    mask = (ci <= ri) & (jnp.right_shift(ri, shift) == jnp.right_shift(ci, shift))
    lmat = mask.astype(F32)
    rowid = lax.broadcasted_iota(jnp.int32, (R, 1), 0)
    rsel = [(rowid >= u * T) & (rowid < (u + 1) * T) for u in range(nseq)]

    def conv_silu(src_ref, cols):
        accs = []
        for u in range(nseq):
            stg_ref[u, 0:SUBLANES, cols] = cv_ref[u, :, cols]
            stg_ref[u, SUBLANES:SUBLANES + T, cols] = src_ref[u * T:(u + 1) * T, :]
            acc = cb_ref[:, cols]
            for d in range(C_CONV):
                acc = acc + stg_ref[u, SUBLANES - d:SUBLANES - d + T, cols] * cw_ref[C_CONV - 1 - d:C_CONV - d, cols]
            accs.append(acc)
            cv_ref[u, :, cols] = stg_ref[u, T:T + SUBLANES, cols]
        acc = accs[0] if nseq == 1 else jnp.concatenate(accs, axis=0)
        return acc * jax.nn.sigmoid(acc)

    xc = conv_silu(x_ref, slice(0, C_INNER))
    bcc_ = conv_silu(bc_ref, slice(C_INNER, C_CONV_DIM))

    dtv = jax.nn.softplus(dt_ref[...] + dtb_ref[...])
    dtt = jax.nn.softplus(dtt_ref[...] + dtbt_ref[...])
    cum_col = jnp.dot(lmat, dtv * (-jnp.exp(al_ref[...])), precision=HIGHEST, preferred_element_type=F32)
    cum_row = lax.dot_general(dtt * (-jnp.exp(alt_ref[...])), lmat, (((1,), (1,)), ((), ())),
                              precision=HIGHEST, preferred_element_type=F32)

    lo = lax.broadcasted_iota(jnp.int32, (R, LANES), 1) < C_HEADDIM
    rlo = lax.broadcasted_iota(jnp.int32, (LANES, 1), 0) < C_HEADDIM
    nt = (((1,), (1,)), ((), ()))
    tn = (((0,), (0,)), ((), ()))
    pairs_per_group = C_PAIRS // C_GROUPS
    for g in range(C_GROUPS):
        bg = bcc_[:, g * C_STATE:(g + 1) * C_STATE].astype(BF16)
        cg = bcc_[:, (C_GROUPS + g) * C_STATE:(C_GROUPS + g + 1) * C_STATE].astype(BF16)
        cbm = lax.dot_general(cg, bg, nt, preferred_element_type=F32)
        ys = None
        for u in range(nseq):
            sg = s_ref[u, g * pairs_per_group:(g + 1) * pairs_per_group].reshape(C_GROUP_W, C_STATE)
            t_u = lax.dot_general(cg, sg.astype(BF16), nt, preferred_element_type=F32)
            ys = t_u if u == 0 else jnp.where(rsel[u], t_u, ys)
        for q in range(pairs_per_group):
            pr = g * pairs_per_group + q
            ps = slice(pr * LANES, (pr + 1) * LANES)
            xp = xc[:, ps]
            cc = [cum_col[:, 2 * pr + e:2 * pr + e + 1] for e in range(2)]
            intra = None
            for e, keep in ((0, lo), (1, jnp.logical_not(lo))):
                hh = 2 * pr + e
                seg = jnp.exp(jnp.where(mask, cc[e] - cum_row[hh:hh + 1, :], -jnp.inf))
                mix = cbm * seg * dtt[hh:hh + 1, :]
                part = jnp.dot(mix.astype(BF16), jnp.where(keep, xp, 0.0).astype(BF16),
                               preferred_element_type=F32)
                intra = part if intra is None else intra + part
            scale = jnp.where(lo, jnp.exp(cc[0]), jnp.exp(cc[1]))
            yp = intra + scale * ys[:, q * LANES:(q + 1) * LANES] + dsk_ref[:, ps] * xp
            zp = z_ref[:, ps]
            y_ref[:, ps] = yp * (zp * jax.nn.sigmoid(zp))
            for u in range(nseq):
                last = (u + 1) * T - 1
                ct = [cc[e][last:last + 1, :] for e in range(2)]
                tail = jnp.where(lo, jnp.exp(ct[0] - cc[0]) * dtv[:, 2 * pr:2 * pr + 1],
                                 jnp.exp(ct[1] - cc[1]) * dtv[:, 2 * pr + 1:2 * pr + 2])
                xw = xp * tail
                if nseq > 1:
                    xw = jnp.where(rsel[u], xw, 0.0)
                upd = lax.dot_general(xw.astype(BF16), bg, tn, preferred_element_type=F32)
                dec = jnp.where(rlo, jnp.exp(ct[0]), jnp.exp(ct[1]))
                s_ref[u, pr] = dec * s_ref[u, pr] + upd

    for g in range(C_GROUPS):
        gs_ = slice(g * C_GROUP_W, (g + 1) * C_GROUP_W)
        yg = y_ref[:, gs_]
        y_ref[:, gs_] = yg * lax.rsqrt(jnp.mean(yg * yg, axis=-1, keepdims=True) + EPS) * nw_ref[:, gs_]


def ssd_mixer(zin, row0, b, L, y_into, layer, s_into, ssm0, conv0, conv_w, conv_b, dt_bias, a_log, d_skip, norm_w):
    T = math.gcd(L, C_CHUNK)
    nseq = 1 if T == C_CHUNK else SSD_SHORT_ROWS // T
    assert nseq == 1 or (L == T and b % nseq == 0)
    R = T * nseq
    nchunk = L // T
    nblk = b * L // R
    blk0 = row0 // R
    dt_t = zin[row0:row0 + b * L, ODD_DT_OFF:ODD_DT_OFF + C_HEADS].reshape(nblk, R, C_HEADS).transpose(0, 2, 1)
    pad_row = lambda v: jnp.zeros((1, LANES), F32).at[0, :C_HEADS].set(v)
    s0 = ssm0.reshape(b, C_PAIRS, 2 * C_HEADDIM, C_STATE)
    cv0 = jnp.concatenate([jnp.zeros((b, SUBLANES - (C_CONV - 1), C_CONV_DIM), F32), conv0], axis=1)
    blk = lambda i, c: blk0 + i * nchunk + c
    sspec = pl.BlockSpec((nseq, C_PAIRS, 2 * C_HEADDIM, C_STATE), lambda i, c: (i, 0, 0, 0))
    sout = pl.BlockSpec((None, nseq, C_PAIRS, 2 * C_HEADDIM, C_STATE), lambda i, c: (layer, i, 0, 0, 0))
    cvspec = pl.BlockSpec((nseq, SUBLANES, C_CONV_DIM), lambda i, c: (i, 0, 0))
    const = lambda shape: pl.BlockSpec(shape, lambda i, c: (0,) * len(shape))
    in_specs = [pl.BlockSpec((R, C_INNER), lambda i, c: (blk(i, c), 0)),
                pl.BlockSpec((R, C_INNER), lambda i, c: (blk(i, c), 1)),
                pl.BlockSpec((R, C_BC_W), lambda i, c: (blk(i, c), 2 * C_INNER // C_BC_W)),
                pl.BlockSpec((R, LANES), lambda i, c: (blk(i, c), ODD_DT_OFF // LANES)),
                pl.BlockSpec((None, C_HEADS, R), lambda i, c: (i * nchunk + c, 0, 0)),
                const((C_CONV, C_CONV_DIM)), const((1, C_CONV_DIM)),
                const((1, LANES)), const((C_HEADS, 1)), const((1, LANES)), const((C_HEADS, 1)),
                const((1, C_INNER)), const((1, C_INNER)), sspec, cvspec]
    body, xspecs, xargs, aliases = _fill_into(
        functools.partial(_ssd_kernel, T=T, nseq=nseq), len(in_specs), y_into, 0, s_into, 1)
    y, s, cv = pl.pallas_call(
        body,
        grid=(b // nseq, nchunk),
        in_specs=in_specs + xspecs,
        out_specs=[pl.BlockSpec((R, C_INNER), lambda i, c: (blk(i, c), 0)), sout, cvspec],
        out_shape=[jax.ShapeDtypeStruct((zin.shape[0], C_INNER), F32),
                   jax.ShapeDtypeStruct((N_ODD,) + s0.shape, F32), jax.ShapeDtypeStruct(cv0.shape, F32)],
        input_output_aliases=aliases,
        scratch_shapes=[pltpu.VMEM((nseq, SUBLANES + T, C_CONV_DIM), F32)],
        compiler_params=_cparams(("arbitrary", "arbitrary")),
        name="ssd",
    )(zin, zin, zin, zin, dt_t, conv_w, conv_b.reshape(1, C_CONV_DIM),
      pad_row(dt_bias), dt_bias.reshape(C_HEADS, 1), pad_row(a_log), a_log.reshape(C_HEADS, 1),
      jnp.repeat(d_skip, C_HEADDIM).reshape(1, C_INNER), norm_w.reshape(1, C_INNER), s0, cv0, *xargs)
    return y, s, cv[:, SUBLANES - (C_CONV - 1):]


def _even_w_in(w):
    qkvo = w[:, :2 * A_QK + 2 * A_V]
    gates = w[:, 2 * A_QK + 2 * A_V:A_COLS]
    rwkv = w[:, A_COLS:]
    pad = jnp.zeros((D_MODEL, EVEN_GATE_PAD - 2 * A_HEADS), w.dtype)
    return jnp.concatenate([rwkv, gates, pad, qkvo], axis=1).astype(BF16)


def _odd_w_in(w):
    pad = jnp.zeros((D_MODEL, ODD_DT_PAD - C_HEADS), w.dtype)
    return jnp.concatenate([w, pad], axis=1).astype(BF16)


def kernel(x_prompt, x_sample, state_mlstm_C, state_mlstm_n, state_mlstm_m, state_rwkv_S,
           state_rwkv_shift, state_ssm, state_conv, p_prompt, p_sample,
           norm_mix, norm_ffn, w_ffn_up, w_ffn_down, w_ple_proj, norm_ple, w_ple_gate, norm_final,
           w_in_even, mlstm_b_i, mlstm_b_f, mlstm_norm, rwkv_mu, rwkv_w0, rwkv_w2, rwkv_a0, rwkv_a2,
           rwkv_g2, rwkv_k_k, rwkv_k_a, rwkv_r_k, rwkv_ln_w, rwkv_ln_b, w_out_even,
           w_in_odd, conv_w, conv_b, dt_bias, a_log, d_skip, ssm_norm, w_out_odd):
    bp, Lp, _ = x_prompt.shape
    bs, Ls, _ = x_sample.shape
    n_p, n_s = bp * Lp, bs * Ls
    n_tot = n_p + n_s
    xp = x_prompt.reshape(n_p, D_MODEL)
    xs = x_sample.reshape(n_s, D_MODEL)
    pp = p_prompt.reshape(DEPTH, n_p, PLE_DIM)
    ps = p_sample.reshape(DEPTH, n_s, PLE_DIM)
    x = None

    even_small = (mlstm_b_i, mlstm_b_f, mlstm_norm, rwkv_mu, rwkv_w0, rwkv_w2, rwkv_a0, rwkv_a2,
                  rwkv_g2, rwkv_k_k, rwkv_k_a, rwkv_r_k, rwkv_ln_w, rwkv_ln_b)
    odd_small = (conv_w, conv_b, dt_bias, a_log, d_skip, ssm_norm)

    zeros_even = (jnp.zeros((bp, A_HEADS, A_DK, A_DV), F32), jnp.zeros((bp, A_HEADS, A_DK), F32),
                  jnp.zeros((bp, A_HEADS), F32), jnp.zeros((bp, B_HEADS, B_DH, B_DH), F32),
                  jnp.zeros((bp, B_COLS), F32))
    zeros_odd = (jnp.zeros((bp, C_HEADS, C_HEADDIM, C_STATE), F32),
                 jnp.zeros((bp, C_CONV - 1, C_CONV_DIM), F32))

    st_p_even, st_s_even, st_p_odd, st_s_odd = [], [], [], []
    mC_p = mC_s = ssm_p = ssm_s = None
    for i in range(DEPTH):
        j = i // 2
        if i % 2 == 0:
            w_in = _even_w_in(w_in_even[j])
            if x is None:
                zin = norm_matmul(xp, norm_mix[i], w_in, 512, 0, n_tot)
                zin = norm_matmul(xs, norm_mix[i], w_in, 512, n_p, n_tot, zin)
            else:
                zin = norm_matmul(x, norm_mix[i], w_in, 512)
            small = [t[j] for t in even_small]
            gates_t = zin[:, EVEN_G_OFF:EVEN_G_OFF + 2 * A_HEADS].reshape(
                -1, MLSTM_ROWS, 2 * A_HEADS).transpose(0, 2, 1)
            ha, hb_p, g, sp = even_mixer(zin, gates_t, 0, bp, Lp, (None, None), j, mC_p, *zeros_even, *small)
            ha, hb_s, g, ss = even_mixer(zin, gates_t, n_p, bs, Ls, (ha, g), j, mC_s, state_mlstm_C[j],
                                         state_mlstm_n[j], state_mlstm_m[j], state_rwkv_S[j], state_rwkv_shift[j],
                                         *small)
            mC_p, mC_s = sp[0], ss[0]
            st_p_even.append(sp)
            st_s_even.append(ss)
            wo = w_out_even[j].astype(BF16)
            terms_p = [(ha, None, wo[:A_V], None), (hb_p, g, wo[A_V:], Lp)]
            terms_s = [(ha, None, wo[:A_V], None), (hb_s, g, wo[A_V:], 0)]
            if x is None:
                xo = matmul_res(terms_p, xp, 0, n_p, None, n_tot)
                x = matmul_res(terms_s, xs, n_p, n_s, xo, n_tot)
            else:
                xo = matmul_res(terms_p, x, 0, n_p)
                x = matmul_res(terms_s, x, n_p, n_s, xo)
        else:
            zin = norm_matmul(x, norm_mix[i], _odd_w_in(w_in_odd[j]), 768)
            small = [t[j] for t in odd_small]
            mix, ssm_p, cv_p = ssd_mixer(zin, 0, bp, Lp, None, j, ssm_p, *zeros_odd, *small)
            mix, ssm_s, cv_s = ssd_mixer(zin, n_p, bs, Ls, mix, j, ssm_s, state_ssm[j], state_conv[j], *small)
            st_p_odd.append(cv_p)
            st_s_odd.append(cv_s)
            x = matmul_res([(mix, None, w_out_odd[j].astype(BF16), None)], x)
        x = ffn(x, norm_ffn[i], w_ffn_up[i].astype(BF16), w_ffn_down[i].astype(BF16))
        wg, wp = w_ple_gate[i].astype(BF16), w_ple_proj[i].astype(BF16)
        xo = ple(x, norm_ple[i], wg, pp[i], wp, 0)
        x = ple(x, norm_ple[i], wg, ps[i], wp, n_p, xo)
    y_prompt = final_norm(x, norm_final, 0, n_p).reshape(bp, Lp, D_MODEL)
    y_sample = final_norm(x, norm_final, n_p, n_s).reshape(bs, Ls, D_MODEL)
    stack = lambda sts, idx: jnp.stack([s[idx] for s in sts])
    ssm_shape = lambda b: (N_ODD, b, C_HEADS, C_HEADDIM, C_STATE)
    return (y_prompt, y_sample,
            mC_p, stack(st_p_even, 1), stack(st_p_even, 2), stack(st_p_even, 3),
            stack(st_p_even, 4), ssm_p.reshape(ssm_shape(bp)), jnp.stack(st_p_odd),
            mC_s, stack(st_s_even, 1), stack(st_s_even, 2), stack(st_s_even, 3),
            stack(st_s_even, 4), ssm_s.reshape(ssm_shape(bs)), jnp.stack(st_s_odd))
```

```python
import math
import functools
import jax
import jax.numpy as jnp
from jax import lax
from jax.experimental import pallas as pl
from jax.experimental.pallas import tpu as pltpu

D_MODEL = 1024
DEPTH = 4
F32 = jnp.float32
BF16 = jnp.bfloat16
EPS = 1e-6
N_EVEN = (DEPTH + 1) // 2
N_ODD = DEPTH // 2
D_FF = 4 * D_MODEL
PLE_DIM = 256

A_HEADS = 4
A_DK = D_MODEL // 8
A_DV = D_MODEL // 8
A_CHUNK = 64
A_QK = A_HEADS * A_DK
A_V = A_HEADS * A_DV
A_COLS = 2 * A_QK + 2 * A_V + 2 * A_HEADS

B_HEADS = 8
B_DH = 64
B_W = B_HEADS * B_DH
B_W_RANK = 64
B_A_RANK = 64
B_G_RANK = 128
B_COLS = 3 * B_W + B_W_RANK + B_A_RANK + B_G_RANK
B_DECAY_OFFSET = 0.5
B_GN_EPS = 64e-5

EVEN_COLS = A_COLS + B_COLS
EVEN_OUT = A_V + B_W

C_INNER = 2 * D_MODEL
C_HEADDIM = 64
C_HEADS = C_INNER // C_HEADDIM
C_GROUPS = 4
C_HPG = C_HEADS // C_GROUPS
C_STATE = 128
C_CONV = 4
C_CHUNK = 128
C_CONV_DIM = C_INNER + 2 * C_GROUPS * C_STATE
ODD_COLS = C_INNER + C_CONV_DIM + C_HEADS

LANES = 128
SUBLANES = 8
VMEM_LIMIT = 56 * 1024 * 1024
TOKEN_TILE = 512

EVEN_GATE_PAD = 256
EVEN_N = B_COLS + EVEN_GATE_PAD + 2 * A_QK + 2 * A_V
EVEN_G_OFF = B_COLS
EVEN_A_OFF = B_COLS + EVEN_GATE_PAD
ODD_DT_PAD = 256
ODD_N = C_INNER + C_CONV_DIM + ODD_DT_PAD
ODD_DT_OFF = C_INNER + C_CONV_DIM


def _cparams(sem):
    return pltpu.CompilerParams(dimension_semantics=sem, vmem_limit_bytes=VMEM_LIMIT)


def _rms(x, g):
    return x * lax.rsqrt(jnp.mean(x * x, axis=-1, keepdims=True) + EPS) * g


def _resident(shape):
    nd = len(shape)
    return pl.BlockSpec(shape, lambda *_: (0,) * nd, pipeline_mode=pl.Buffered(1))


def _rows(width):
    return pl.BlockSpec((TOKEN_TILE, width), lambda i: (i, 0))


def _fill_into(body, n_in, into, out_idx, into2=None, out_idx2=None):
    pairs = [(a, o) for a, o in ((into, out_idx), (into2, out_idx2)) if a is not None]
    if not pairs:
        return body, [], [], {}

    def skipping(*refs):
        return body(*refs[:n_in], *refs[n_in + len(pairs):])

    return (skipping, [pl.BlockSpec(memory_space=pl.ANY)] * len(pairs), [a for a, _ in pairs],
            {n_in + i: o for i, (_, o) in enumerate(pairs)})


def _norm_matmul_kernel(x_ref, g_ref, w_ref, o_ref, *, tn):
    xn = _rms(x_ref[...], g_ref[...]).astype(BF16)
    for n0 in range(0, w_ref.shape[1], tn):
        o_ref[:, n0:n0 + tn] = jnp.dot(xn, w_ref[:, n0:n0 + tn], preferred_element_type=F32)


def norm_matmul(x, g, w, tn, row0=0, total=None, into=None):
    m, n = x.shape[0], w.shape[1]
    total = m if total is None else total
    t0 = row0 // TOKEN_TILE
    in_specs = [_rows(D_MODEL), _resident((1, D_MODEL)), _resident(w.shape)]
    body, xspecs, xargs, aliases = _fill_into(functools.partial(_norm_matmul_kernel, tn=tn), len(in_specs), into, 0)
    return pl.pallas_call(
        body,
        grid=(m // TOKEN_TILE,),
        in_specs=in_specs + xspecs,
        out_specs=pl.BlockSpec((TOKEN_TILE, n), lambda i: (t0 + i, 0)),
        out_shape=jax.ShapeDtypeStruct((total, n), F32),
        input_output_aliases=aliases,
        compiler_params=_cparams(("arbitrary",)),
        name="norm_matmul",
    )(x, g.reshape(1, D_MODEL), w, *xargs)


FFN_CHUNK = 512


def _matmul_res_kernel(*refs, kinds):
    x_ref, o_ref = refs[-2], refs[-1]
    acc = x_ref[...]
    pos = 0
    for has_gate, channel_major in kinds:
        a = refs[pos][...]
        if channel_major:
            a = a.T
        if has_gate:
            a = a * refs[pos + 1][...]
        w_ref = refs[pos + 1 + has_gate]
        pos += 2 + has_gate
        acc = acc + jnp.dot(a.astype(BF16), w_ref[...], preferred_element_type=F32)
    o_ref[...] = acc


def matmul_res(terms, x, row0=0, nrows=None, into=None, total=None):
    m = x.shape[0] if total is None else total
    nrows = m if nrows is None else nrows
    t0 = row0 // TOKEN_TILE
    rows = lambda width: pl.BlockSpec((TOKEN_TILE, width), lambda i: (t0 + i, 0))
    xspec = rows(D_MODEL) if total is None else _rows(D_MODEL)
    specs, args, kinds = [], [], []
    for a, gate, w, seq_len in terms:
        if seq_len is None:
            specs.append(rows(a.shape[1]))
        elif seq_len == 0:
            specs.append(pl.BlockSpec((TOKEN_TILE, a.shape[1]), lambda i: (i, 0)))
        else:
            assert seq_len % TOKEN_TILE == 0
            per_seq = seq_len // TOKEN_TILE
            specs.append(pl.BlockSpec((None, a.shape[1], TOKEN_TILE), lambda i: (i // per_seq, 0, i % per_seq)))
        specs += ([rows(w.shape[0])] if gate is not None else []) + [_resident(w.shape)]
        args += [a] + ([gate] if gate is not None else []) + [w]
        kinds.append((int(gate is not None), bool(seq_len)))
    specs.append(xspec)
    body, xspecs, xargs, aliases = _fill_into(
        functools.partial(_matmul_res_kernel, kinds=tuple(kinds)), len(specs), into, 0)
    return pl.pallas_call(
        body,
        grid=(nrows // TOKEN_TILE,),
        in_specs=specs + xspecs,
        out_specs=rows(D_MODEL),
        out_shape=jax.ShapeDtypeStruct((m, D_MODEL), F32),
        input_output_aliases=aliases,
        compiler_params=_cparams(("arbitrary",)),
        name="matmul_res",
    )(*args, x, *xargs)


def _ffn_kernel(x_ref, g_ref, wu_ref, wd_ref, o_ref):
    x = x_ref[...]
    xn = _rms(x, g_ref[...]).astype(BF16)
    acc = x
    for c0 in range(0, D_FF, FFN_CHUNK):
        h = jnp.dot(xn, wu_ref[:, c0:c0 + FFN_CHUNK], preferred_element_type=F32)
        h = jnp.square(jnp.maximum(h, 0.0)).astype(BF16)
        acc = acc + jnp.dot(h, wd_ref[c0:c0 + FFN_CHUNK, :], preferred_element_type=F32)
    o_ref[...] = acc


FFN_TILE = 1024


def ffn(x, g, wu, wd):
    m = x.shape[0]
    rows = pl.BlockSpec((FFN_TILE, D_MODEL), lambda i: (i, 0))
    return pl.pallas_call(
        _ffn_kernel,
        grid=(m // FFN_TILE,),
        in_specs=[rows, _resident((1, D_MODEL)), _resident(wu.shape), _resident(wd.shape)],
        out_specs=rows,
        out_shape=jax.ShapeDtypeStruct((m, D_MODEL), F32),
        compiler_params=_cparams(("arbitrary",)),
        name="ffn",
    )(x, g.reshape(1, D_MODEL), wu, wd)


def _ple_kernel(x_ref, g_ref, wg_ref, p_ref, wp_ref, o_ref):
    x = x_ref[...]
    xn = _rms(x, g_ref[...]).astype(BF16)
    gate = jax.nn.sigmoid(jnp.dot(xn, wg_ref[...], preferred_element_type=F32))
    proj = jnp.dot(p_ref[...].astype(BF16), wp_ref[...], preferred_element_type=F32)
    o_ref[...] = x + proj * gate


def ple(x, g, wg, p, wp, row0, into=None):
    t0 = row0 // TOKEN_TILE
    rows = pl.BlockSpec((TOKEN_TILE, D_MODEL), lambda i: (t0 + i, 0))
    in_specs = [rows, _resident((1, D_MODEL)), _resident(wg.shape), _rows(PLE_DIM), _resident(wp.shape)]
    body, xspecs, xargs, aliases = _fill_into(_ple_kernel, len(in_specs), into, 0)
    return pl.pallas_call(
        body,
        grid=(p.shape[0] // TOKEN_TILE,),
        in_specs=in_specs + xspecs,
        out_specs=rows,
        out_shape=jax.ShapeDtypeStruct(x.shape, F32),
        input_output_aliases=aliases,
        compiler_params=_cparams(("arbitrary",)),
        name="ple",
    )(x, g.reshape(1, D_MODEL), wg, p, wp, *xargs)


def _final_norm_kernel(x_ref, g_ref, o_ref):
    o_ref[...] = _rms(x_ref[...], g_ref[...])


def final_norm(x, g, row0, nrows):
    t0 = row0 // TOKEN_TILE
    return pl.pallas_call(
        _final_norm_kernel,
        grid=(nrows // TOKEN_TILE,),
        in_specs=[pl.BlockSpec((TOKEN_TILE, D_MODEL), lambda i: (t0 + i, 0)), _resident((1, D_MODEL))],
        out_specs=_rows(D_MODEL),
        out_shape=jax.ShapeDtypeStruct((nrows, D_MODEL), F32),
        compiler_params=_cparams(("arbitrary",)),
        name="final_norm",
    )(x, g.reshape(1, D_MODEL))


def _rwkv_operands(k, a, r, kk_t, ka_t, rk_t):
    kk = k * kk_t
    kk = kk * lax.rsqrt(jnp.maximum(jnp.sum(kk * kk, axis=1, keepdims=True), 1e-24))
    k2 = k * (1.0 + (a - 1.0) * ka_t)
    return -kk, kk * a, k2, jnp.sum(r * k2 * rk_t, axis=1)


def _rwkv_step(s_ref, tile0, ntiles, r, w, k, a, b, vrows):
    ys = []
    for lt in range(ntiles):
        rows = slice((tile0 + lt) * B_DH, (tile0 + lt + 1) * B_DH)
        s = s_ref[rows, :]
        sa = jnp.sum(s * a, axis=0, keepdims=True)
        s = s * w + sa * b + vrows[lt:lt + 1] * k
        s_ref[rows, :] = s
        ys.append(jnp.sum(s * r, axis=0, keepdims=True))
    return jnp.concatenate(ys, axis=0)


LONG_T = 128
LONG_NLT = B_DH // 2
KEY_PITCH = B_DH + SUBLANES
VAL_PITCH = LONG_NLT + SUBLANES


def _rwkv_scan_long_kernel(r_ref, w_ref, k_ref, a_ref, v_ref, kkt_ref, kat_ref, rkt_ref, lnw_ref, lnb_ref, s0_ref,
                           yt_ref, s_ref, or_s, ow_s, ok_s, oa_s, av_s, v_s, y_s):
    nb = yt_ref.shape[0]

    @pl.when(pl.program_id(0) == 0)
    def _():
        s_ref[...] = s0_ref[...]
        for scr in (or_s, ow_s, ok_s, oa_s, av_s, v_s, y_s):
            scr[...] = jnp.zeros(scr.shape, F32)

    lo = lax.broadcasted_iota(jnp.int32, (LONG_T, LANES), 1) < LANES // 2

    def tile(ref, c):
        m = ref[c]
        return jnp.concatenate([m, m], axis=0).T

    def relayout(c, carry):
        for ref, dst in ((r_ref, or_s), (w_ref, ow_s), (k_ref, ok_s), (a_ref, oa_s)):
            dst[pl.ds(c, LONG_T, stride=KEY_PITCH), :] = tile(ref, c)
            dst[pl.ds(c + LONG_NLT, LONG_T, stride=KEY_PITCH), :] = tile(ref, c + LONG_NLT)
        v_s[pl.ds(c, LONG_T, stride=VAL_PITCH), :] = jnp.where(lo, tile(v_ref, c), tile(v_ref, c + LONG_NLT))
        return carry

    lax.fori_loop(0, LONG_NLT, relayout, 0)

    def unpad(scr, pitch, n):
        return scr[...].reshape(LONG_T, pitch, LANES)[:, :n]

    def pad(x, pitch):
        zeros = jnp.zeros((LONG_T, pitch - x.shape[1], LANES), F32)
        return jnp.concatenate([x, zeros], axis=1).reshape(LONG_T * pitch, LANES)

    av, bv, k2, bonus = _rwkv_operands(unpad(ok_s, KEY_PITCH, B_DH), unpad(oa_s, KEY_PITCH, B_DH),
                                       unpad(or_s, KEY_PITCH, B_DH), kkt_ref[...], kat_ref[...], rkt_ref[...])
    av_s[...] = pad(av, KEY_PITCH)
    oa_s[...] = pad(bv, KEY_PITCH)
    ok_s[...] = pad(k2, KEY_PITCH)

    def step(t, carry):
        kr = pl.ds(pl.multiple_of(t * KEY_PITCH, SUBLANES), B_DH)
        vr = pl.ds(pl.multiple_of(t * VAL_PITCH, SUBLANES), LONG_NLT)
        y_s[vr, :] = _rwkv_step(s_ref, 0, LONG_NLT, or_s[kr, :], ow_s[kr, :], ok_s[kr, :], av_s[kr, :],
                                oa_s[kr, :], v_s[vr, :])
        return carry

    lax.fori_loop(0, LONG_T, step, 0)

    def head_sum(x):
        tot = jnp.sum(x, axis=1)
        return tot + pltpu.roll(tot, LANES // 2, axis=1)

    y = unpad(y_s, VAL_PITCH, LONG_NLT)
    d = y - (head_sum(y) * (1.0 / B_DH))[:, None, :]
    var = head_sum(d * d) * (1.0 / B_DH)
    y = (d * lax.rsqrt(var + B_GN_EPS)[:, None, :] * lnw_ref[...] + lnb_ref[...]
         + bonus[:, None, :] * unpad(v_s, VAL_PITCH, LONG_NLT))
    y_s[...] = pad(y, VAL_PITCH)

    def relayout_out(lt, carry):
        yt = y_s[pl.ds(lt, LONG_T, stride=VAL_PITCH), :].T
        for i2 in range(2):
            for b in range(nb):
                row = i2 * (LANES // 2) + b * B_HEADS
                yt_ref[b, pl.ds(lt + LONG_NLT * i2, B_HEADS, stride=B_DH), :] = yt[row:row + B_HEADS, :]
        return carry

    lax.fori_loop(0, LONG_NLT, relayout_out, 0)


def rwkv_scan_long(rc, wc, kc, ac, vc, S0, k_k, k_a, r_k, ln_w, ln_b):
    _, bh, L = rc.shape
    b = bh // B_HEADS
    assert bh * 2 == LANES and L % LONG_T == 0
    per_key = lambda p: jnp.tile(p.reshape(B_HEADS, B_DH).T, (1, LANES // B_HEADS))
    per_val = lambda p: jnp.concatenate(
        [jnp.tile(p.reshape(B_HEADS, 2, LONG_NLT)[:, i2].T, (1, b)) for i2 in range(2)], axis=1)
    s0 = S0.reshape(b, B_HEADS, 2, LONG_NLT, B_DH).transpose(3, 4, 2, 0, 1).reshape(LONG_NLT * B_DH, LANES)
    blk = pl.BlockSpec((B_DH, bh, LONG_T), lambda c: (0, 0, c))
    blk1 = pl.BlockSpec((B_DH, bh, LONG_T), lambda c: (0, 0, c), pipeline_mode=pl.Buffered(1))
    oblk = pl.BlockSpec((b, B_W, LONG_T), lambda c: (0, 0, c))
    const = lambda shape: pl.BlockSpec(shape, lambda c: (0,) * len(shape))
    big = pltpu.VMEM((LONG_T * KEY_PITCH, LANES), F32)
    small = pltpu.VMEM((LONG_T * VAL_PITCH, LANES), F32)
    yt, s = pl.pallas_call(
        _rwkv_scan_long_kernel,
        grid=(L // LONG_T,),
        in_specs=[blk] * 3 + [blk1] * 2 + [const((B_DH, LANES))] * 3 + [const((LONG_NLT, LANES))] * 2
        + [const((LONG_NLT * B_DH, LANES))],
        out_specs=[oblk, const((LONG_NLT * B_DH, LANES))],
        out_shape=[jax.ShapeDtypeStruct((b, B_W, L), F32), jax.ShapeDtypeStruct(s0.shape, F32)],
        scratch_shapes=[big] * 5 + [small, small],
        compiler_params=_cparams(("arbitrary",)),
        name="rwkv_scan_long",
    )(rc, wc, kc, ac, vc, per_key(k_k), per_key(k_a), per_key(r_k), per_val(ln_w), per_val(ln_b), s0)
    s = s.reshape(LONG_NLT, B_DH, 2, b, B_HEADS).transpose(3, 4, 2, 0, 1).reshape(b, B_HEADS, B_DH, B_DH)
    return yt, s


def _rwkv_scan_short_kernel(r_ref, w_ref, k_ref, a_ref, v_ref, kkt_ref, kat_ref, rkt_ref, lnw_ref, lnb_ref, s0_ref,
                            y_ref, sout_ref, s_s, or_s, ow_s, ok_s, oa_s, av_s, ov_s, y_s, *, L):
    nt = 2 * B_DH
    for q in range(nt * B_DH // LANES):
        s_s[q * LANES:(q + 1) * LANES, :] = s0_ref[:, q * LANES:(q + 1) * LANES].T
    for ref, dst in ((r_ref, or_s), (w_ref, ow_s), (k_ref, ok_s), (a_ref, oa_s), (v_ref, ov_s)):
        for t in range(L):
            dst[t] = ref[pl.ds(t, LANES, stride=L), :].T.reshape(2, B_DH, LANES)
    shape3 = (L * 2, B_DH, LANES)
    tiles = lambda ref: jnp.concatenate([ref[...]] * L, axis=0)
    av, bv, k2, bonus = _rwkv_operands(ok_s[...].reshape(shape3), oa_s[...].reshape(shape3),
                                       or_s[...].reshape(shape3), tiles(kkt_ref), tiles(kat_ref), tiles(rkt_ref))
    av_s[...] = av.reshape(L, 2, B_DH, LANES)
    oa_s[...] = bv.reshape(L, 2, B_DH, LANES)
    ok_s[...] = k2.reshape(L, 2, B_DH, LANES)

    def step(t, carry):
        for h in range(2):
            y_s[t, h] = _rwkv_step(s_s, h * B_DH, B_DH, or_s[t, h], ow_s[t, h], ok_s[t, h], av_s[t, h],
                                   oa_s[t, h], ov_s[t, h])
        return carry

    lax.fori_loop(0, L, step, 0)

    y = y_s[...]
    d = y - jnp.mean(y, axis=2, keepdims=True)
    var = jnp.mean(d * d, axis=2, keepdims=True)
    y = (d * lax.rsqrt(var + B_GN_EPS) * lnw_ref[...] + lnb_ref[...]
         + bonus.reshape(L, 2, 1, LANES) * ov_s[...])
    for t in range(L):
        y_ref[pl.ds(t, LANES, stride=L), :] = y[t].reshape(nt, LANES).T
    for q in range(nt * B_DH // LANES):
        sout_ref[:, q * LANES:(q + 1) * LANES] = s_s[q * LANES:(q + 1) * LANES, :].T


def rwkv_scan_short(r, w, k, a, v, S0, L, k_k, k_a, r_k, ln_w, ln_b):
    n = r.shape[0]
    b = n // L
    assert b == LANES
    npair = B_HEADS // 2
    wide = lambda p: jnp.broadcast_to(p.reshape(npair, 2, B_DH, 1), (npair, 2, B_DH, LANES))
    s0 = S0.reshape(b, B_HEADS * B_DH * B_DH)
    blk = pl.BlockSpec((n, LANES), lambda p: (0, p))
    cblk = pl.BlockSpec((None, 2, B_DH, LANES), lambda p: (p, 0, 0, 0))
    sblk = pl.BlockSpec((b, 2 * B_DH * B_DH), lambda p: (0, p))
    op = pltpu.VMEM((L, 2, B_DH, LANES), F32)
    y, s = pl.pallas_call(
        functools.partial(_rwkv_scan_short_kernel, L=L),
        grid=(npair,),
        in_specs=[blk] * 5 + [cblk] * 5 + [sblk],
        out_specs=[blk, sblk],
        out_shape=[jax.ShapeDtypeStruct((n, B_W), F32), jax.ShapeDtypeStruct(s0.shape, F32)],
        scratch_shapes=[pltpu.VMEM((2 * B_DH * B_DH, LANES), F32)] + [op] * 7,
        compiler_params=_cparams(("arbitrary",)),
        name="rwkv_scan_short",
    )(r, w, k, a, v, wide(k_k), wide(k_a), wide(r_k), wide(ln_w), wide(ln_b), s0)
    return y, s.reshape(b, B_HEADS, B_DH, B_DH)


RWKV_PREP_ROWS = 256
RWKV_SHORT_ROWS = 64
B_LORA_OFF = 3 * B_W


def _rwkv_prep_kernel(z_ref, sh0_ref, mu_ref, wwa_ref, g2_ref, w0_ref, a0_ref,
                      r_ref, w_ref, k_ref, a_ref, v_ref, g_ref, sh_ref, *scratch, T, nseq, channel_major):
    @pl.when(pl.program_id(1) == 0)
    def _():
        sh_ref[...] = sh0_ref[...]

    z = z_ref[...]
    rowid = lax.broadcasted_iota(jnp.int32, (z.shape[0], 1), 0)
    zprev = pltpu.roll(z, 1, axis=0)
    for u in range(nseq):
        zprev = jnp.where(rowid == u * T, sh_ref[u], zprev)
    for u in range(nseq):
        sh_ref[u] = z[(u + 1) * T - 1:(u + 1) * T, :]
    zs = z + (zprev - z) * mu_ref[...]
    r = zs[:, :B_W]
    k = zs[:, B_W:2 * B_W]
    lora = zs[:, B_LORA_OFF:B_LORA_OFF + LANES]
    lane = lax.broadcasted_iota(jnp.int32, lora.shape, 1)
    lora = jnp.where(lane < B_W_RANK, jnp.tanh(lora), lora).astype(BF16)
    wa = jnp.dot(lora, wwa_ref[...], preferred_element_type=F32)
    w_log = -jax.nn.softplus(-(w0_ref[...] + wa[:, :B_W])) - B_DECAY_OFFSET
    a = jax.nn.sigmoid(a0_ref[...] + wa[:, B_W:])
    zg = zs[:, B_LORA_OFF + LANES:B_LORA_OFF + LANES + B_G_RANK]
    def emit(ref, x):
        if not channel_major:
            ref[...] = x
            return
        xt_s = scratch[0]
        xt = x.T
        for j in range(xt_s.shape[0]):
            for h in range(B_HEADS):
                xt_s[j, h * KEY_PITCH:h * KEY_PITCH + B_DH, :] = xt[h * B_DH:(h + 1) * B_DH, j * LANES:(j + 1) * LANES]
        for c in range(B_DH):
            for j in range(xt_s.shape[0]):
                ref[c, :, j * LANES:(j + 1) * LANES] = xt_s[j, pl.ds(c, B_HEADS, stride=KEY_PITCH), :]

    emit(r_ref, r)
    emit(w_ref, jnp.exp(-jnp.exp(w_log)))
    emit(k_ref, k)
    emit(a_ref, a)
    emit(v_ref, zs[:, 2 * B_W:3 * B_W])
    g_ref[...] = jnp.dot(jax.nn.sigmoid(zg).astype(BF16), g2_ref[...], preferred_element_type=F32)


def rwkv_prep(zin, row0, b, L, gate_into, shift0, mu, w0, w2, a0, a2, g2):
    assert B_W_RANK + B_A_RANK == LANES
    T = math.gcd(L, RWKV_PREP_ROWS)
    channel_major = T == RWKV_PREP_ROWS
    nseq = 1 if channel_major else RWKV_SHORT_ROWS // T
    assert nseq == 1 or (L == T and b % nseq == 0)
    R = T * nseq
    nchunk = L // T
    blk0 = row0 // R
    wwa = jnp.zeros((LANES, 2 * B_W), F32).at[:B_W_RANK, :B_W].set(w2).at[B_W_RANK:, B_W:].set(a2).astype(BF16)
    row = lambda v: v.reshape(1, -1)
    if channel_major:
        blk = pl.BlockSpec((B_DH, B_HEADS, R), lambda i, c: (0, i, c))
        oshape = jax.ShapeDtypeStruct((B_DH, b * B_HEADS, L), F32)
    else:
        blk = pl.BlockSpec((R, B_W), lambda i, c: (i * nchunk + c, 0))
        oshape = jax.ShapeDtypeStruct((b * L, B_W), F32)
    gblk = pl.BlockSpec((R, B_W), lambda i, c: (blk0 + i * nchunk + c, 0))
    shspec = pl.BlockSpec((nseq, 1, B_COLS), lambda i, c: (i, 0, 0))
    const = lambda shape: pl.BlockSpec(shape, lambda i, c: (0,) * len(shape))
    in_specs = [pl.BlockSpec((R, B_COLS), lambda i, c: (blk0 + i * nchunk + c, 0)), shspec, const((1, B_COLS)),
                const((LANES, 2 * B_W)), const((B_G_RANK, B_W)), const((1, B_W)), const((1, B_W))]
    body, xspecs, xargs, aliases = _fill_into(
        functools.partial(_rwkv_prep_kernel, T=T, nseq=nseq, channel_major=channel_major),
        len(in_specs), gate_into, 5)
    outs = pl.pallas_call(
        body,
        grid=(b // nseq, nchunk),
        in_specs=in_specs + xspecs,
        out_specs=[blk] * 5 + [gblk, shspec],
        out_shape=[oshape] * 5
        + [jax.ShapeDtypeStruct((zin.shape[0], B_W), F32), jax.ShapeDtypeStruct((b, 1, B_COLS), F32)],
        input_output_aliases=aliases,
        scratch_shapes=[pltpu.VMEM((R // LANES, B_HEADS * KEY_PITCH, LANES), F32)] if channel_major else [],
        compiler_params=_cparams(("arbitrary", "arbitrary")),
        name="rwkv_prep",
    )(zin, shift0.reshape(b, 1, B_COLS), row(mu), wwa, g2.astype(BF16), row(w0), row(a0), *xargs)
    return outs[:5], outs[5], outs[6].reshape(b, B_COLS)


MLSTM_ROWS = 128
HIGHEST = lax.Precision.HIGHEST


def _mlstm_kernel(q_ref, k_ref, v_ref, o_ref, g_ref, gt_ref, brow_ref, bcol_ref, nw_ref,
                  c0_ref, n0_ref, m0_ref, h_ref, c_ref, n_ref, m_ref, *, T, nseq):
    R = MLSTM_ROWS

    @pl.when(pl.program_id(1) == 0)
    def _():
        c_ref[...] = c0_ref[...]
        n_ref[...] = n0_ref[...]
        m_ref[...] = m0_ref[...]

    shift = T.bit_length() - 1
    ri = lax.broadcasted_iota(jnp.int32, (R, R), 0)
    ci = lax.broadcasted_iota(jnp.int32, (R, R), 1)
    mask = (ci <= ri) & (jnp.right_shift(ri, shift) == jnp.right_shift(ci, shift))
    lmat = mask.astype(F32)
    rowid = lax.broadcasted_iota(jnp.int32, (R, 1), 0)
    rsel = [(rowid >= u * T) & (rowid < (u + 1) * T) for u in range(nseq)]

    g = g_ref[...] + brow_ref[...]
    lane = lax.broadcasted_iota(jnp.int32, g.shape, 1)
    glog = jnp.where((lane >= A_HEADS) & (lane < 2 * A_HEADS), jax.nn.log_sigmoid(g), g)
    gt = gt_ref[...] + bcol_ref[...]
    sub = lax.broadcasted_iota(jnp.int32, gt.shape, 0)
    gtlog = jnp.where(sub >= A_HEADS, jax.nn.log_sigmoid(gt), gt)
    bc_col = jnp.dot(lmat, glog, precision=HIGHEST, preferred_element_type=F32)
    bc_row = lax.dot_general(gtlog, lmat, (((1,), (1,)), ((), ())), precision=HIGHEST,
                             preferred_element_type=F32)
    lane_m = lax.broadcasted_iota(jnp.int32, (1, LANES), 1)
    m_old = [m_ref[u] for u in range(nseq)]
    m_out = [jnp.zeros((1, LANES), F32) for _ in range(nseq)]

    for h in range(A_HEADS):
        hs = slice(h * A_DK, (h + 1) * A_DK)
        bcc = bc_col[:, A_HEADS + h:A_HEADS + h + 1]
        bcr = bc_row[A_HEADS + h:A_HEADS + h + 1, :]
        lir = gtlog[h:h + 1, :]
        lic = glog[:, h:h + 1]
        m_u = [m_old[u][:, h:h + 1] for u in range(nseq)]
        m_col = m_u[0]
        for u in range(1, nseq):
            m_col = jnp.where(rsel[u], m_u[u], m_col)
        dmat = jnp.where(mask, bcc - bcr + lir, -jnp.inf)
        inter = bcc + m_col
        mt = jnp.maximum(inter, jnp.max(dmat, axis=1, keepdims=True))
        p = jnp.exp(dmat - mt)
        qh = q_ref[:, hs] * (A_DK ** -0.5)
        kh = k_ref[:, hs]
        qb, kb, vb = qh.astype(BF16), kh.astype(BF16), v_ref[:, hs].astype(BF16)
        wq = lax.dot_general(qb, kb, (((1,), (1,)), ((), ())), preferred_element_type=F32) * p
        wi = jnp.exp(inter - mt)
        c_old = [c_ref[u, h] for u in range(nseq)]
        n_old = [n_ref[u, h:h + 1, :] for u in range(nseq)]
        qc = jnp.dot(qb, c_old[0].astype(BF16), preferred_element_type=F32)
        qn = jnp.sum(qh * n_old[0], axis=1, keepdims=True)
        for u in range(1, nseq):
            qc = jnp.where(rsel[u], jnp.dot(qb, c_old[u].astype(BF16), preferred_element_type=F32), qc)
            qn = jnp.where(rsel[u], jnp.sum(qh * n_old[u], axis=1, keepdims=True), qn)
        num = jnp.dot(wq.astype(BF16), vb, preferred_element_type=F32) + wi * qc
        den = jnp.sum(wq, axis=1, keepdims=True) + wi * qn
        hh = num / jnp.maximum(jnp.abs(den), jnp.exp(-mt))
        hh = hh * lax.rsqrt(jnp.mean(hh * hh, axis=-1, keepdims=True) + EPS)
        h_ref[:, hs] = hh * nw_ref[:, hs] * jax.nn.sigmoid(o_ref[:, hs])
        for u in range(nseq):
            b_last = bcc[(u + 1) * T - 1:(u + 1) * T, :]
            gs = b_last - bcc + lic
            gmax = jnp.max(gs if nseq == 1 else jnp.where(rsel[u], gs, -jnp.inf), axis=0, keepdims=True)
            m_new = jnp.maximum(b_last + m_u[u], gmax)
            decay = jnp.exp(b_last + m_u[u] - m_new)
            ws = jnp.exp(gs - m_new)
            if nseq > 1:
                ws = jnp.where(rsel[u], ws, 0.0)
            kw = kh * ws
            c_ref[u, h] = decay * c_old[u] + lax.dot_general(
                kw.astype(BF16), vb, (((0,), (0,)), ((), ())), preferred_element_type=F32)
            n_ref[u, h:h + 1, :] = decay * n_old[u] + jnp.sum(kw, axis=0, keepdims=True)
            m_out[u] = jnp.where(lane_m == h, m_new, m_out[u])
    for u in range(nseq):
        m_ref[u] = m_out[u]


def mlstm(zin, gates_t, row0, b, L, h_into, layer, c_into, b_i, b_f, m_norm, C0, n0, m0):
    R = MLSTM_ROWS
    T = math.gcd(L, R)
    nseq = R // T
    assert nseq == 1 or (L == T and b % nseq == 0)
    nchunk = L // T
    blk0 = row0 // R
    bias = jnp.concatenate([b_i, b_f])
    bias_row = jnp.zeros((1, LANES), F32).at[0, :2 * A_HEADS].set(bias)
    bias_col = bias.reshape(2 * A_HEADS, 1)
    m0p = jnp.zeros((b, 1, LANES), F32).at[:, 0, :A_HEADS].set(m0)
    blk = lambda i, c: blk0 + i * nchunk + c
    rowblk = lambda col: pl.BlockSpec((R, A_QK), lambda i, c: (blk(i, c), EVEN_A_OFF // A_QK + col))
    cspec = pl.BlockSpec((nseq, A_HEADS, A_DK, A_DV), lambda i, c: (i, 0, 0, 0))
    cout = pl.BlockSpec((None, nseq, A_HEADS, A_DK, A_DV), lambda i, c: (layer, i, 0, 0, 0))
    nspec = pl.BlockSpec((nseq, A_HEADS, A_DK), lambda i, c: (i, 0, 0))
    mspec = pl.BlockSpec((nseq, 1, LANES), lambda i, c: (i, 0, 0))
    const = lambda shape: pl.BlockSpec(shape, lambda i, c: (0,) * len(shape))
    in_specs = [rowblk(0), rowblk(1), rowblk(2), rowblk(3),
                pl.BlockSpec((R, LANES), lambda i, c: (blk(i, c), EVEN_G_OFF // LANES)),
                pl.BlockSpec((None, 2 * A_HEADS, R), lambda i, c: (blk(i, c), 0, 0)),
                const((1, LANES)), const((2 * A_HEADS, 1)), const((1, A_V)),
                cspec if C0.ndim == 4 else cout, nspec, mspec]
    body, xspecs, xargs, aliases = _fill_into(
        functools.partial(_mlstm_kernel, T=T, nseq=nseq), len(in_specs), h_into, 0, c_into, 1)
    h, C, n, m = pl.pallas_call(
        body,
        grid=(b // nseq, nchunk),
        in_specs=in_specs + xspecs,
        out_specs=[pl.BlockSpec((R, A_V), lambda i, c: (blk(i, c), 0)), cout, nspec, mspec],
        out_shape=[jax.ShapeDtypeStruct((zin.shape[0], A_V), F32),
                   jax.ShapeDtypeStruct((N_EVEN,) + C0.shape[-4:], F32), jax.ShapeDtypeStruct(n0.shape, F32),
                   jax.ShapeDtypeStruct(m0p.shape, F32)],
        input_output_aliases=aliases,
        compiler_params=_cparams(("arbitrary", "arbitrary")),
        name="mlstm",
    )(zin, zin, zin, zin, zin, gates_t, bias_row, bias_col, m_norm.reshape(1, A_V), C0, n0, m0p, *xargs)
    return h, C, n, m[:, 0, :A_HEADS]


def even_mixer(zin, gates_t, row0, b, L, into, layer, c_into, C0, n0, m0, S0, shift0, b_i, b_f, m_norm,
               mu, w0, w2, a0, a2, g2, k_k, k_a, r_k, ln_w, ln_b):
    hA, C, n, m = mlstm(zin, gates_t, row0, b, L, into[0], layer, c_into, b_i, b_f, m_norm, C0, n0, m0)
    ops, g, shift = rwkv_prep(zin, row0, b, L, into[1], shift0, mu, w0, w2, a0, a2, g2)
    if ops[0].ndim == 3:
        hB, S = rwkv_scan_long(*ops, S0, k_k, k_a, r_k, ln_w, ln_b)
    else:
        hB, S = rwkv_scan_short(*ops, S0, L, k_k, k_a, r_k, ln_w, ln_b)
    return hA, hB, g, (C, n, m, S, shift)


C_PAIRS = C_HEADS // 2
C_GROUP_W = C_INNER // C_GROUPS
C_BC_W = 2 * C_GROUPS * C_STATE
SSD_SHORT_ROWS = 32


def _ssd_kernel(z_ref, x_ref, bc_ref, dt_ref, dtt_ref, cw_ref, cb_ref, dtb_ref, dtbt_ref, al_ref, alt_ref,
                dsk_ref, nw_ref, s0_ref, cv0_ref, y_ref, s_ref, cv_ref, stg_ref, *, T, nseq):
    R = T * nseq

    @pl.when(pl.program_id(1) == 0)
    def _():
        s_ref[...] = s0_ref[...]
        cv_ref[...] = cv0_ref[...]

    shift = T.bit_length() - 1
    ri = lax.broadcasted_iota(jnp.int32, (R, R), 0)
    ci = lax.broadcasted_iota(jnp.int32, (R, R), 1)
    mask = (ci <= ri) & (jnp.right_shift(ri, shift) == jnp.right_shift(ci, shift))
    lmat = mask.astype(F32)
    rowid = lax.broadcasted_iota(jnp.int32, (R, 1), 0)
    rsel = [(rowid >= u * T) & (rowid < (u + 1) * T) for u in range(nseq)]

    def conv_silu(src_ref, cols):
        accs = []
        for u in range(nseq):
            stg_ref[u, 0:SUBLANES, cols] = cv_ref[u, :, cols]
            stg_ref[u, SUBLANES:SUBLANES + T, cols] = src_ref[u * T:(u + 1) * T, :]
            acc = cb_ref[:, cols]
            for d in range(C_CONV):
                acc = acc + stg_ref[u, SUBLANES - d:SUBLANES - d + T, cols] * cw_ref[C_CONV - 1 - d:C_CONV - d, cols]
            accs.append(acc)
            cv_ref[u, :, cols] = stg_ref[u, T:T + SUBLANES, cols]
        acc = accs[0] if nseq == 1 else jnp.concatenate(accs, axis=0)
        return acc * jax.nn.sigmoid(acc)

    xc = conv_silu(x_ref, slice(0, C_INNER))
    bcc_ = conv_silu(bc_ref, slice(C_INNER, C_CONV_DIM))

    dtv = jax.nn.softplus(dt_ref[...] + dtb_ref[...])
    dtt = jax.nn.softplus(dtt_ref[...] + dtbt_ref[...])
    cum_col = jnp.dot(lmat, dtv * (-jnp.exp(al_ref[...])), precision=HIGHEST, preferred_element_type=F32)
    cum_row = lax.dot_general(dtt * (-jnp.exp(alt_ref[...])), lmat, (((1,), (1,)), ((), ())),
                              precision=HIGHEST, preferred_element_type=F32)

    lo = lax.broadcasted_iota(jnp.int32, (R, LANES), 1) < C_HEADDIM
    rlo = lax.broadcasted_iota(jnp.int32, (LANES, 1), 0) < C_HEADDIM
    nt = (((1,), (1,)), ((), ()))
    tn = (((0,), (0,)), ((), ()))
    pairs_per_group = C_PAIRS // C_GROUPS
    for g in range(C_GROUPS):
        bg = bcc_[:, g * C_STATE:(g + 1) * C_STATE].astype(BF16)
        cg = bcc_[:, (C_GROUPS + g) * C_STATE:(C_GROUPS + g + 1) * C_STATE].astype(BF16)
        cbm = lax.dot_general(cg, bg, nt, preferred_element_type=F32)
        ys = None
        for u in range(nseq):
            sg = s_ref[u, g * pairs_per_group:(g + 1) * pairs_per_group].reshape(C_GROUP_W, C_STATE)
            t_u = lax.dot_general(cg, sg.astype(BF16), nt, preferred_element_type=F32)
            ys = t_u if u == 0 else jnp.where(rsel[u], t_u, ys)
        for q in range(pairs_per_group):
            pr = g * pairs_per_group + q
            ps = slice(pr * LANES, (pr + 1) * LANES)
            xp = xc[:, ps]
            cc = [cum_col[:, 2 * pr + e:2 * pr + e + 1] for e in range(2)]
            intra = None
            for e, keep in ((0, lo), (1, jnp.logical_not(lo))):
                hh = 2 * pr + e
                seg = jnp.exp(jnp.where(mask, cc[e] - cum_row[hh:hh + 1, :], -jnp.inf))
                mix = cbm * seg * dtt[hh:hh + 1, :]
                part = jnp.dot(mix.astype(BF16), jnp.where(keep, xp, 0.0).astype(BF16),
                               preferred_element_type=F32)
                intra = part if intra is None else intra + part
            scale = jnp.where(lo, jnp.exp(cc[0]), jnp.exp(cc[1]))
            yp = intra + scale * ys[:, q * LANES:(q + 1) * LANES] + dsk_ref[:, ps] * xp
            zp = z_ref[:, ps]
            y_ref[:, ps] = yp * (zp * jax.nn.sigmoid(zp))
            for u in range(nseq):
                last = (u + 1) * T - 1
                ct = [cc[e][last:last + 1, :] for e in range(2)]
                tail = jnp.where(lo, jnp.exp(ct[0] - cc[0]) * dtv[:, 2 * pr:2 * pr + 1],
                                 jnp.exp(ct[1] - cc[1]) * dtv[:, 2 * pr + 1:2 * pr + 2])
                xw = xp * tail
                if nseq > 1:
                    xw = jnp.where(rsel[u], xw, 0.0)
                upd = lax.dot_general(xw.astype(BF16), bg, tn, preferred_element_type=F32)
                dec = jnp.where(rlo, jnp.exp(ct[0]), jnp.exp(ct[1]))
                s_ref[u, pr] = dec * s_ref[u, pr] + upd

    for g in range(C_GROUPS):
        gs_ = slice(g * C_GROUP_W, (g + 1) * C_GROUP_W)
        yg = y_ref[:, gs_]
        y_ref[:, gs_] = yg * lax.rsqrt(jnp.mean(yg * yg, axis=-1, keepdims=True) + EPS) * nw_ref[:, gs_]


def ssd_mixer(zin, row0, b, L, y_into, layer, s_into, ssm0, conv0, conv_w, conv_b, dt_bias, a_log, d_skip, norm_w):
    T = math.gcd(L, C_CHUNK)
    nseq = 1 if T == C_CHUNK else SSD_SHORT_ROWS // T
    assert nseq == 1 or (L == T and b % nseq == 0)
    R = T * nseq
    nchunk = L // T
    nblk = b * L // R
    blk0 = row0 // R
    dt_t = zin[row0:row0 + b * L, ODD_DT_OFF:ODD_DT_OFF + C_HEADS].reshape(nblk, R, C_HEADS).transpose(0, 2, 1)
    pad_row = lambda v: jnp.zeros((1, LANES), F32).at[0, :C_HEADS].set(v)
    pair_shape = (b, C_PAIRS, 2 * C_HEADDIM, C_STATE)
    per_layer = ssm0.ndim == 4
    s0 = ssm0.reshape(pair_shape if per_layer else (ssm0.shape[0],) + pair_shape)
    cv0 = jnp.concatenate([jnp.zeros((b, SUBLANES - (C_CONV - 1), C_CONV_DIM), F32), conv0], axis=1)
    blk = lambda i, c: blk0 + i * nchunk + c
    sspec = pl.BlockSpec((nseq, C_PAIRS, 2 * C_HEADDIM, C_STATE), lambda i, c: (i, 0, 0, 0))
    sout = pl.BlockSpec((None, nseq, C_PAIRS, 2 * C_HEADDIM, C_STATE), lambda i, c: (layer, i, 0, 0, 0))
    cvspec = pl.BlockSpec((nseq, SUBLANES, C_CONV_DIM), lambda i, c: (i, 0, 0))
    const = lambda shape: pl.BlockSpec(shape, lambda i, c: (0,) * len(shape))
    in_specs = [pl.BlockSpec((R, C_INNER), lambda i, c: (blk(i, c), 0)),
                pl.BlockSpec((R, C_INNER), lambda i, c: (blk(i, c), 1)),
                pl.BlockSpec((R, C_BC_W), lambda i, c: (blk(i, c), 2 * C_INNER // C_BC_W)),
                pl.BlockSpec((R, LANES), lambda i, c: (blk(i, c), ODD_DT_OFF // LANES)),
                pl.BlockSpec((None, C_HEADS, R), lambda i, c: (i * nchunk + c, 0, 0)),
                const((C_CONV, C_CONV_DIM)), const((1, C_CONV_DIM)),
                const((1, LANES)), const((C_HEADS, 1)), const((1, LANES)), const((C_HEADS, 1)),
                const((1, C_INNER)), const((1, C_INNER)), sspec if per_layer else sout, cvspec]
    body, xspecs, xargs, aliases = _fill_into(
        functools.partial(_ssd_kernel, T=T, nseq=nseq), len(in_specs), y_into, 0, s_into, 1)
    y, s, cv = pl.pallas_call(
        body,
        grid=(b // nseq, nchunk),
        in_specs=in_specs + xspecs,
        out_specs=[pl.BlockSpec((R, C_INNER), lambda i, c: (blk(i, c), 0)), sout, cvspec],
        out_shape=[jax.ShapeDtypeStruct((zin.shape[0], C_INNER), F32),
                   jax.ShapeDtypeStruct((N_ODD,) + pair_shape, F32), jax.ShapeDtypeStruct(cv0.shape, F32)],
        input_output_aliases=aliases,
        scratch_shapes=[pltpu.VMEM((nseq, SUBLANES + T, C_CONV_DIM), F32)],
        compiler_params=_cparams(("arbitrary", "arbitrary")),
        name="ssd",
    )(zin, zin, zin, zin, dt_t, conv_w, conv_b.reshape(1, C_CONV_DIM),
      pad_row(dt_bias), dt_bias.reshape(C_HEADS, 1), pad_row(a_log), a_log.reshape(C_HEADS, 1),
      jnp.repeat(d_skip, C_HEADDIM).reshape(1, C_INNER), norm_w.reshape(1, C_INNER), s0, cv0, *xargs)
    return y, s, cv[:, SUBLANES - (C_CONV - 1):]


def _even_w_in(w):
    qkvo = w[:, :2 * A_QK + 2 * A_V]
    gates = w[:, 2 * A_QK + 2 * A_V:A_COLS]
    rwkv = w[:, A_COLS:]
    pad = jnp.zeros((D_MODEL, EVEN_GATE_PAD - 2 * A_HEADS), w.dtype)
    return jnp.concatenate([rwkv, gates, pad, qkvo], axis=1).astype(BF16)


def _odd_w_in(w):
    pad = jnp.zeros((D_MODEL, ODD_DT_PAD - C_HEADS), w.dtype)
    return jnp.concatenate([w, pad], axis=1).astype(BF16)


def kernel(x_prompt, x_sample, state_mlstm_C, state_mlstm_n, state_mlstm_m, state_rwkv_S,
           state_rwkv_shift, state_ssm, state_conv, p_prompt, p_sample,
           norm_mix, norm_ffn, w_ffn_up, w_ffn_down, w_ple_proj, norm_ple, w_ple_gate, norm_final,
           w_in_even, mlstm_b_i, mlstm_b_f, mlstm_norm, rwkv_mu, rwkv_w0, rwkv_w2, rwkv_a0, rwkv_a2,
           rwkv_g2, rwkv_k_k, rwkv_k_a, rwkv_r_k, rwkv_ln_w, rwkv_ln_b, w_out_even,
           w_in_odd, conv_w, conv_b, dt_bias, a_log, d_skip, ssm_norm, w_out_odd):
    bp, Lp, _ = x_prompt.shape
    bs, Ls, _ = x_sample.shape
    n_p, n_s = bp * Lp, bs * Ls
    n_tot = n_p + n_s
    xp = x_prompt.reshape(n_p, D_MODEL)
    xs = x_sample.reshape(n_s, D_MODEL)
    pp = p_prompt.reshape(DEPTH, n_p, PLE_DIM)
    ps = p_sample.reshape(DEPTH, n_s, PLE_DIM)
    x = None

    even_small = (mlstm_b_i, mlstm_b_f, mlstm_norm, rwkv_mu, rwkv_w0, rwkv_w2, rwkv_a0, rwkv_a2,
                  rwkv_g2, rwkv_k_k, rwkv_k_a, rwkv_r_k, rwkv_ln_w, rwkv_ln_b)
    odd_small = (conv_w, conv_b, dt_bias, a_log, d_skip, ssm_norm)

    zeros_even = (jnp.zeros((bp, A_HEADS, A_DK, A_DV), F32), jnp.zeros((bp, A_HEADS, A_DK), F32),
                  jnp.zeros((bp, A_HEADS), F32), jnp.zeros((bp, B_HEADS, B_DH, B_DH), F32),
                  jnp.zeros((bp, B_COLS), F32))
    zeros_odd = (jnp.zeros((bp, C_HEADS, C_HEADDIM, C_STATE), F32),
                 jnp.zeros((bp, C_CONV - 1, C_CONV_DIM), F32))

    st_p_even, st_s_even, st_p_odd, st_s_odd = [], [], [], []
    mC_p = mC_s = ssm_p = ssm_s = None
    for i in range(DEPTH):
        j = i // 2
        if i % 2 == 0:
            w_in = _even_w_in(w_in_even[j])
            if x is None:
                zin = norm_matmul(xp, norm_mix[i], w_in, 512, 0, n_tot)
                zin = norm_matmul(xs, norm_mix[i], w_in, 512, n_p, n_tot, zin)
            else:
                zin = norm_matmul(x, norm_mix[i], w_in, 512)
            small = [t[j] for t in even_small]
            gates_t = zin[:, EVEN_G_OFF:EVEN_G_OFF + 2 * A_HEADS].reshape(
                -1, MLSTM_ROWS, 2 * A_HEADS).transpose(0, 2, 1)
            ha, hb_p, g, sp = even_mixer(zin, gates_t, 0, bp, Lp, (None, None), j, mC_p, *zeros_even, *small)
            ha, hb_s, g, ss = even_mixer(zin, gates_t, n_p, bs, Ls, (ha, g), j, mC_s, state_mlstm_C,
                                         state_mlstm_n[j], state_mlstm_m[j], state_rwkv_S[j], state_rwkv_shift[j],
                                         *small)
            mC_p, mC_s = sp[0], ss[0]
            st_p_even.append(sp)
            st_s_even.append(ss)
            wo = w_out_even[j].astype(BF16)
            terms_p = [(ha, None, wo[:A_V], None), (hb_p, g, wo[A_V:], Lp)]
            terms_s = [(ha, None, wo[:A_V], None), (hb_s, g, wo[A_V:], 0)]
            if x is None:
                xo = matmul_res(terms_p, xp, 0, n_p, None, n_tot)
                x = matmul_res(terms_s, xs, n_p, n_s, xo, n_tot)
            else:
                xo = matmul_res(terms_p, x, 0, n_p)
                x = matmul_res(terms_s, x, n_p, n_s, xo)
        else:
            zin = norm_matmul(x, norm_mix[i], _odd_w_in(w_in_odd[j]), 768)
            small = [t[j] for t in odd_small]
            mix, ssm_p, cv_p = ssd_mixer(zin, 0, bp, Lp, None, j, ssm_p, *zeros_odd, *small)
            mix, ssm_s, cv_s = ssd_mixer(zin, n_p, bs, Ls, mix, j, ssm_s, state_ssm, state_conv[j], *small)
            st_p_odd.append(cv_p)
            st_s_odd.append(cv_s)
            x = matmul_res([(mix, None, w_out_odd[j].astype(BF16), None)], x)
        x = ffn(x, norm_ffn[i], w_ffn_up[i].astype(BF16), w_ffn_down[i].astype(BF16))
        wg, wp = w_ple_gate[i].astype(BF16), w_ple_proj[i].astype(BF16)
        xo = ple(x, norm_ple[i], wg, pp[i], wp, 0)
        x = ple(x, norm_ple[i], wg, ps[i], wp, n_p, xo)
    y_prompt = final_norm(x, norm_final, 0, n_p).reshape(bp, Lp, D_MODEL)
    y_sample = final_norm(x, norm_final, n_p, n_s).reshape(bs, Ls, D_MODEL)
    stack = lambda sts, idx: jnp.stack([s[idx] for s in sts])
    ssm_shape = lambda b: (N_ODD, b, C_HEADS, C_HEADDIM, C_STATE)
    return (y_prompt, y_sample,
            mC_p, stack(st_p_even, 1), stack(st_p_even, 2), stack(st_p_even, 3),
            stack(st_p_even, 4), ssm_p.reshape(ssm_shape(bp)), jnp.stack(st_p_odd),
            mC_s, stack(st_s_even, 1), stack(st_s_even, 2), stack(st_s_even, 3),
            stack(st_s_even, 4), ssm_s.reshape(ssm_shape(bs)), jnp.stack(st_s_odd))
```

```python
import math
import functools
import jax
import jax.numpy as jnp
from jax import lax
from jax.experimental import pallas as pl
from jax.experimental.pallas import tpu as pltpu

D_MODEL = 1024
DEPTH = 4
F32 = jnp.float32
BF16 = jnp.bfloat16
EPS = 1e-6
N_EVEN = (DEPTH + 1) // 2
N_ODD = DEPTH // 2
D_FF = 4 * D_MODEL
PLE_DIM = 256

A_HEADS = 4
A_DK = D_MODEL // 8
A_DV = D_MODEL // 8
A_CHUNK = 64
A_QK = A_HEADS * A_DK
A_V = A_HEADS * A_DV
A_COLS = 2 * A_QK + 2 * A_V + 2 * A_HEADS

B_HEADS = 8
B_DH = 64
B_W = B_HEADS * B_DH
B_W_RANK = 64
B_A_RANK = 64
B_G_RANK = 128
B_COLS = 3 * B_W + B_W_RANK + B_A_RANK + B_G_RANK
B_DECAY_OFFSET = 0.5
B_GN_EPS = 64e-5

EVEN_COLS = A_COLS + B_COLS
EVEN_OUT = A_V + B_W

C_INNER = 2 * D_MODEL
C_HEADDIM = 64
C_HEADS = C_INNER // C_HEADDIM
C_GROUPS = 4
C_HPG = C_HEADS // C_GROUPS
C_STATE = 128
C_CONV = 4
C_CHUNK = 128
C_CONV_DIM = C_INNER + 2 * C_GROUPS * C_STATE
ODD_COLS = C_INNER + C_CONV_DIM + C_HEADS

LANES = 128
SUBLANES = 8
VMEM_LIMIT = 56 * 1024 * 1024
TOKEN_TILE = 512

EVEN_GATE_PAD = 256
EVEN_N = B_COLS + EVEN_GATE_PAD + 2 * A_QK + 2 * A_V
EVEN_G_OFF = B_COLS
EVEN_A_OFF = B_COLS + EVEN_GATE_PAD
ODD_DT_PAD = 256
ODD_N = C_INNER + C_CONV_DIM + ODD_DT_PAD
ODD_DT_OFF = C_INNER + C_CONV_DIM


def _cparams(sem):
    return pltpu.CompilerParams(dimension_semantics=sem, vmem_limit_bytes=VMEM_LIMIT)


def _rms(x, g):
    return x * lax.rsqrt(jnp.mean(x * x, axis=-1, keepdims=True) + EPS) * g


def _resident(shape):
    nd = len(shape)
    return pl.BlockSpec(shape, lambda *_: (0,) * nd, pipeline_mode=pl.Buffered(1))


def _rows(width):
    return pl.BlockSpec((TOKEN_TILE, width), lambda i: (i, 0))


def _fill_into(body, n_in, into, out_idx, into2=None, out_idx2=None):
    pairs = [(a, o) for a, o in ((into, out_idx), (into2, out_idx2)) if a is not None]
    if not pairs:
        return body, [], [], {}

    def skipping(*refs):
        return body(*refs[:n_in], *refs[n_in + len(pairs):])

    return (skipping, [pl.BlockSpec(memory_space=pl.ANY)] * len(pairs), [a for a, _ in pairs],
            {n_in + i: o for i, (_, o) in enumerate(pairs)})


def _norm_matmul_kernel(x_ref, g_ref, w_ref, o_ref, *, tn):
    xn = _rms(x_ref[...], g_ref[...]).astype(BF16)
    for n0 in range(0, w_ref.shape[1], tn):
        o_ref[:, n0:n0 + tn] = jnp.dot(xn, w_ref[:, n0:n0 + tn], preferred_element_type=F32)


def norm_matmul(x, g, w, tn, row0=0, total=None, into=None):
    m, n = x.shape[0], w.shape[1]
    total = m if total is None else total
    t0 = row0 // TOKEN_TILE
    in_specs = [_rows(D_MODEL), _resident((1, D_MODEL)), _resident(w.shape)]
    body, xspecs, xargs, aliases = _fill_into(functools.partial(_norm_matmul_kernel, tn=tn), len(in_specs), into, 0)
    return pl.pallas_call(
        body,
        grid=(m // TOKEN_TILE,),
        in_specs=in_specs + xspecs,
        out_specs=pl.BlockSpec((TOKEN_TILE, n), lambda i: (t0 + i, 0)),
        out_shape=jax.ShapeDtypeStruct((total, n), F32),
        input_output_aliases=aliases,
        compiler_params=_cparams(("arbitrary",)),
        name="norm_matmul",
    )(x, g.reshape(1, D_MODEL), w, *xargs)


FFN_CHUNK = 512


def _matmul_res_kernel(*refs, kinds):
    x_ref, o_ref = refs[-2], refs[-1]
    acc = x_ref[...]
    pos = 0
    for has_gate, channel_major in kinds:
        a = refs[pos][...]
        if channel_major:
            a = a.T
        if has_gate:
            a = a * refs[pos + 1][...]
        w_ref = refs[pos + 1 + has_gate]
        pos += 2 + has_gate
        acc = acc + jnp.dot(a.astype(BF16), w_ref[...], preferred_element_type=F32)
    o_ref[...] = acc


def matmul_res(terms, x, row0=0, nrows=None, into=None, total=None):
    m = x.shape[0] if total is None else total
    nrows = m if nrows is None else nrows
    t0 = row0 // TOKEN_TILE
    rows = lambda width: pl.BlockSpec((TOKEN_TILE, width), lambda i: (t0 + i, 0))
    xspec = rows(D_MODEL) if total is None else _rows(D_MODEL)
    specs, args, kinds = [], [], []
    for a, gate, w, seq_len in terms:
        if seq_len is None:
            specs.append(rows(a.shape[1]))
        elif seq_len == 0:
            specs.append(pl.BlockSpec((TOKEN_TILE, a.shape[1]), lambda i: (i, 0)))
        else:
            assert seq_len % TOKEN_TILE == 0
            per_seq = seq_len // TOKEN_TILE
            specs.append(pl.BlockSpec((None, a.shape[1], TOKEN_TILE), lambda i: (i // per_seq, 0, i % per_seq)))
        specs += ([rows(w.shape[0])] if gate is not None else []) + [_resident(w.shape)]
        args += [a] + ([gate] if gate is not None else []) + [w]
        kinds.append((int(gate is not None), bool(seq_len)))
    specs.append(xspec)
    body, xspecs, xargs, aliases = _fill_into(
        functools.partial(_matmul_res_kernel, kinds=tuple(kinds)), len(specs), into, 0)
    return pl.pallas_call(
        body,
        grid=(nrows // TOKEN_TILE,),
        in_specs=specs + xspecs,
        out_specs=rows(D_MODEL),
        out_shape=jax.ShapeDtypeStruct((m, D_MODEL), F32),
        input_output_aliases=aliases,
        compiler_params=_cparams(("arbitrary",)),
        name="matmul_res",
    )(*args, x, *xargs)


def _ffn_ple_kernel(x_ref, gf_ref, wu_ref, wd_ref, gp_ref, wg_ref, p_ref, wp_ref, o_ref, y_s):
    x = x_ref[...]
    xn = _rms(x, gf_ref[...]).astype(BF16)
    y_s[...] = x
    for c0 in range(0, D_FF, FFN_CHUNK):
        h = jnp.dot(xn, wu_ref[:, c0:c0 + FFN_CHUNK], preferred_element_type=F32)
        h = jnp.square(jnp.maximum(h, 0.0)).astype(BF16)
        y_s[...] += jnp.dot(h, wd_ref[c0:c0 + FFN_CHUNK, :], preferred_element_type=F32)
    y = y_s[...]
    yn = _rms(y, gp_ref[...]).astype(BF16)
    gate = jax.nn.sigmoid(jnp.dot(yn, wg_ref[...], preferred_element_type=F32))
    proj = jnp.dot(p_ref[...].astype(BF16), wp_ref[...], preferred_element_type=F32)
    o_ref[...] = y + proj * gate


def ffn_ple(x, g_ffn, wu, wd, g_ple, wg, p, wp, row0, into=None):
    t0 = row0 // TOKEN_TILE
    rows = pl.BlockSpec((TOKEN_TILE, D_MODEL), lambda i: (t0 + i, 0))
    row1 = lambda v: v.reshape(1, D_MODEL)
    in_specs = [rows, _resident((1, D_MODEL)), _resident(wu.shape), _resident(wd.shape),
                _resident((1, D_MODEL)), _resident(wg.shape), _rows(PLE_DIM), _resident(wp.shape)]
    body, xspecs, xargs, aliases = _fill_into(_ffn_ple_kernel, len(in_specs), into, 0)
    return pl.pallas_call(
        body,
        grid=(p.shape[0] // TOKEN_TILE,),
        in_specs=in_specs + xspecs,
        out_specs=rows,
        out_shape=jax.ShapeDtypeStruct(x.shape, F32),
        input_output_aliases=aliases,
        scratch_shapes=[pltpu.VMEM((TOKEN_TILE, D_MODEL), F32)],
        compiler_params=_cparams(("arbitrary",)),
        name="ffn_ple",
    )(x, row1(g_ffn), wu, wd, row1(g_ple), wg, p, wp, *xargs)


def _final_norm_kernel(x_ref, g_ref, o_ref):
    o_ref[...] = _rms(x_ref[...], g_ref[...])


def final_norm(x, g, row0, nrows):
    t0 = row0 // TOKEN_TILE
    return pl.pallas_call(
        _final_norm_kernel,
        grid=(nrows // TOKEN_TILE,),
        in_specs=[pl.BlockSpec((TOKEN_TILE, D_MODEL), lambda i: (t0 + i, 0)), _resident((1, D_MODEL))],
        out_specs=_rows(D_MODEL),
        out_shape=jax.ShapeDtypeStruct((nrows, D_MODEL), F32),
        compiler_params=_cparams(("arbitrary",)),
        name="final_norm",
    )(x, g.reshape(1, D_MODEL))


def _rwkv_operands(k, a, r, kk_t, ka_t, rk_t):
    kk = k * kk_t
    kk = kk * lax.rsqrt(jnp.maximum(jnp.sum(kk * kk, axis=1, keepdims=True), 1e-24))
    k2 = k * (1.0 + (a - 1.0) * ka_t)
    return -kk, kk * a, k2, jnp.sum(r * k2 * rk_t, axis=1)


def _rwkv_step(s_ref, tile0, ntiles, r, w, k, a, b, vrows):
    ys = []
    for lt in range(ntiles):
        rows = slice((tile0 + lt) * B_DH, (tile0 + lt + 1) * B_DH)
        s = s_ref[rows, :]
        sa = jnp.sum(s * a, axis=0, keepdims=True)
        s = s * w + sa * b + vrows[lt:lt + 1] * k
        s_ref[rows, :] = s
        ys.append(jnp.sum(s * r, axis=0, keepdims=True))
    return jnp.concatenate(ys, axis=0)


LONG_T = 128
LONG_NLT = B_DH // 2
KEY_PITCH = B_DH + SUBLANES
VAL_PITCH = LONG_NLT + SUBLANES


def _rwkv_scan_long_kernel(r_ref, w_ref, k_ref, a_ref, v_ref, kkt_ref, kat_ref, rkt_ref, lnw_ref, lnb_ref, s0_ref,
                           yt_ref, s_ref, or_s, ow_s, ok_s, oa_s, av_s, v_s, y_s):
    nb = yt_ref.shape[0]

    @pl.when(pl.program_id(0) == 0)
    def _():
        s_ref[...] = s0_ref[...]
        for scr in (or_s, ow_s, ok_s, oa_s, av_s, v_s, y_s):
            scr[...] = jnp.zeros(scr.shape, F32)

    lo = lax.broadcasted_iota(jnp.int32, (LONG_T, LANES), 1) < LANES // 2

    def tile(ref, c):
        m = ref[c]
        return jnp.concatenate([m, m], axis=0).T

    def relayout(c, carry):
        for ref, dst in ((r_ref, or_s), (w_ref, ow_s), (k_ref, ok_s), (a_ref, oa_s)):
            dst[pl.ds(c, LONG_T, stride=KEY_PITCH), :] = tile(ref, c)
            dst[pl.ds(c + LONG_NLT, LONG_T, stride=KEY_PITCH), :] = tile(ref, c + LONG_NLT)
        v_s[pl.ds(c, LONG_T, stride=VAL_PITCH), :] = jnp.where(lo, tile(v_ref, c), tile(v_ref, c + LONG_NLT))
        return carry

    lax.fori_loop(0, LONG_NLT, relayout, 0)

    def unpad(scr, pitch, n):
        return scr[...].reshape(LONG_T, pitch, LANES)[:, :n]

    def pad(x, pitch):
        zeros = jnp.zeros((LONG_T, pitch - x.shape[1], LANES), F32)
        return jnp.concatenate([x, zeros], axis=1).reshape(LONG_T * pitch, LANES)

    av, bv, k2, bonus = _rwkv_operands(unpad(ok_s, KEY_PITCH, B_DH), unpad(oa_s, KEY_PITCH, B_DH),
                                       unpad(or_s, KEY_PITCH, B_DH), kkt_ref[...], kat_ref[...], rkt_ref[...])
    av_s[...] = pad(av, KEY_PITCH)
    oa_s[...] = pad(bv, KEY_PITCH)
    ok_s[...] = pad(k2, KEY_PITCH)

    def step(t, carry):
        kr = pl.ds(pl.multiple_of(t * KEY_PITCH, SUBLANES), B_DH)
        vr = pl.ds(pl.multiple_of(t * VAL_PITCH, SUBLANES), LONG_NLT)
        y_s[vr, :] = _rwkv_step(s_ref, 0, LONG_NLT, or_s[kr, :], ow_s[kr, :], ok_s[kr, :], av_s[kr, :],
                                oa_s[kr, :], v_s[vr, :])
        return carry

    lax.fori_loop(0, LONG_T, step, 0)

    def head_sum(x):
        tot = jnp.sum(x, axis=1)
        return tot + pltpu.roll(tot, LANES // 2, axis=1)

    y = unpad(y_s, VAL_PITCH, LONG_NLT)
    d = y - (head_sum(y) * (1.0 / B_DH))[:, None, :]
    var = head_sum(d * d) * (1.0 / B_DH)
    y = (d * lax.rsqrt(var + B_GN_EPS)[:, None, :] * lnw_ref[...] + lnb_ref[...]
         + bonus[:, None, :] * unpad(v_s, VAL_PITCH, LONG_NLT))
    y_s[...] = pad(y, VAL_PITCH)

    def relayout_out(lt, carry):
        yt = y_s[pl.ds(lt, LONG_T, stride=VAL_PITCH), :].T
        for i2 in range(2):
            for b in range(nb):
                row = i2 * (LANES // 2) + b * B_HEADS
                yt_ref[b, pl.ds(lt + LONG_NLT * i2, B_HEADS, stride=B_DH), :] = yt[row:row + B_HEADS, :]
        return carry

    lax.fori_loop(0, LONG_NLT, relayout_out, 0)


def rwkv_scan_long(rc, wc, kc, ac, vc, S0, k_k, k_a, r_k, ln_w, ln_b):
    _, bh, L = rc.shape
    b = bh // B_HEADS
    assert bh * 2 == LANES and L % LONG_T == 0
    per_key = lambda p: jnp.tile(p.reshape(B_HEADS, B_DH).T, (1, LANES // B_HEADS))
    per_val = lambda p: jnp.concatenate(
        [jnp.tile(p.reshape(B_HEADS, 2, LONG_NLT)[:, i2].T, (1, b)) for i2 in range(2)], axis=1)
    s0 = S0.reshape(b, B_HEADS, 2, LONG_NLT, B_DH).transpose(3, 4, 2, 0, 1).reshape(LONG_NLT * B_DH, LANES)
    blk = pl.BlockSpec((B_DH, bh, LONG_T), lambda c: (0, 0, c))
    blk1 = pl.BlockSpec((B_DH, bh, LONG_T), lambda c: (0, 0, c), pipeline_mode=pl.Buffered(1))
    oblk = pl.BlockSpec((b, B_W, LONG_T), lambda c: (0, 0, c))
    const = lambda shape: pl.BlockSpec(shape, lambda c: (0,) * len(shape))
    big = pltpu.VMEM((LONG_T * KEY_PITCH, LANES), F32)
    small = pltpu.VMEM((LONG_T * VAL_PITCH, LANES), F32)
    yt, s = pl.pallas_call(
        _rwkv_scan_long_kernel,
        grid=(L // LONG_T,),
        in_specs=[blk] * 3 + [blk1] * 2 + [const((B_DH, LANES))] * 3 + [const((LONG_NLT, LANES))] * 2
        + [const((LONG_NLT * B_DH, LANES))],
        out_specs=[oblk, const((LONG_NLT * B_DH, LANES))],
        out_shape=[jax.ShapeDtypeStruct((b, B_W, L), F32), jax.ShapeDtypeStruct(s0.shape, F32)],
        scratch_shapes=[big] * 5 + [small, small],
        compiler_params=_cparams(("arbitrary",)),
        name="rwkv_scan_long",
    )(rc, wc, kc, ac, vc, per_key(k_k), per_key(k_a), per_key(r_k), per_val(ln_w), per_val(ln_b), s0)
    s = s.reshape(LONG_NLT, B_DH, 2, b, B_HEADS).transpose(3, 4, 2, 0, 1).reshape(b, B_HEADS, B_DH, B_DH)
    return yt, s


def _rwkv_scan_short_kernel(r_ref, w_ref, k_ref, a_ref, v_ref, kkt_ref, kat_ref, rkt_ref, lnw_ref, lnb_ref, s0_ref,
                            y_ref, sout_ref, s_s, or_s, ow_s, ok_s, oa_s, av_s, ov_s, y_s, *, L):
    nt = 2 * B_DH
    for q in range(nt * B_DH // LANES):
        s_s[q * LANES:(q + 1) * LANES, :] = s0_ref[:, q * LANES:(q + 1) * LANES].T
    for ref, dst in ((r_ref, or_s), (w_ref, ow_s), (k_ref, ok_s), (a_ref, oa_s), (v_ref, ov_s)):
        for t in range(L):
            dst[t] = ref[pl.ds(t, LANES, stride=L), :].T.reshape(2, B_DH, LANES)
    shape3 = (L * 2, B_DH, LANES)
    tiles = lambda ref: jnp.concatenate([ref[...]] * L, axis=0)
    av, bv, k2, bonus = _rwkv_operands(ok_s[...].reshape(shape3), oa_s[...].reshape(shape3),
                                       or_s[...].reshape(shape3), tiles(kkt_ref), tiles(kat_ref), tiles(rkt_ref))
    av_s[...] = av.reshape(L, 2, B_DH, LANES)
    oa_s[...] = bv.reshape(L, 2, B_DH, LANES)
    ok_s[...] = k2.reshape(L, 2, B_DH, LANES)

    def step(t, carry):
        for h in range(2):
            y_s[t, h] = _rwkv_step(s_s, h * B_DH, B_DH, or_s[t, h], ow_s[t, h], ok_s[t, h], av_s[t, h],
                                   oa_s[t, h], ov_s[t, h])
        return carry

    lax.fori_loop(0, L, step, 0)

    y = y_s[...]
    d = y - jnp.mean(y, axis=2, keepdims=True)
    var = jnp.mean(d * d, axis=2, keepdims=True)
    y = (d * lax.rsqrt(var + B_GN_EPS) * lnw_ref[...] + lnb_ref[...]
         + bonus.reshape(L, 2, 1, LANES) * ov_s[...])
    for t in range(L):
        y_ref[pl.ds(t, LANES, stride=L), :] = y[t].reshape(nt, LANES).T
    for q in range(nt * B_DH // LANES):
        sout_ref[:, q * LANES:(q + 1) * LANES] = s_s[q * LANES:(q + 1) * LANES, :].T


def rwkv_scan_short(r, w, k, a, v, S0, L, k_k, k_a, r_k, ln_w, ln_b):
    n = r.shape[0]
    b = n // L
    assert b == LANES
    npair = B_HEADS // 2
    wide = lambda p: jnp.broadcast_to(p.reshape(npair, 2, B_DH, 1), (npair, 2, B_DH, LANES))
    s0 = S0.reshape(b, B_HEADS * B_DH * B_DH)
    blk = pl.BlockSpec((n, LANES), lambda p: (0, p))
    cblk = pl.BlockSpec((None, 2, B_DH, LANES), lambda p: (p, 0, 0, 0))
    sblk = pl.BlockSpec((b, 2 * B_DH * B_DH), lambda p: (0, p))
    op = pltpu.VMEM((L, 2, B_DH, LANES), F32)
    y, s = pl.pallas_call(
        functools.partial(_rwkv_scan_short_kernel, L=L),
        grid=(npair,),
        in_specs=[blk] * 5 + [cblk] * 5 + [sblk],
        out_specs=[blk, sblk],
        out_shape=[jax.ShapeDtypeStruct((n, B_W), F32), jax.ShapeDtypeStruct(s0.shape, F32)],
        scratch_shapes=[pltpu.VMEM((2 * B_DH * B_DH, LANES), F32)] + [op] * 7,
        compiler_params=_cparams(("arbitrary",)),
        name="rwkv_scan_short",
    )(r, w, k, a, v, wide(k_k), wide(k_a), wide(r_k), wide(ln_w), wide(ln_b), s0)
    return y, s.reshape(b, B_HEADS, B_DH, B_DH)


RWKV_PREP_ROWS = 256
RWKV_SHORT_ROWS = 64
B_LORA_OFF = 3 * B_W


def _rwkv_prep_kernel(z_ref, sh0_ref, mu_ref, wwa_ref, g2_ref, w0_ref, a0_ref,
                      r_ref, w_ref, k_ref, a_ref, v_ref, g_ref, sh_ref, *scratch, T, nseq, channel_major):
    @pl.when(pl.program_id(1) == 0)
    def _():
        sh_ref[...] = sh0_ref[...]

    z = z_ref[...]
    rowid = lax.broadcasted_iota(jnp.int32, (z.shape[0], 1), 0)
    zprev = pltpu.roll(z, 1, axis=0)
    for u in range(nseq):
        zprev = jnp.where(rowid == u * T, sh_ref[u], zprev)
    for u in range(nseq):
        sh_ref[u] = z[(u + 1) * T - 1:(u + 1) * T, :]
    zs = z + (zprev - z) * mu_ref[...]
    r = zs[:, :B_W]
    k = zs[:, B_W:2 * B_W]
    lora = zs[:, B_LORA_OFF:B_LORA_OFF + LANES]
    lane = lax.broadcasted_iota(jnp.int32, lora.shape, 1)
    lora = jnp.where(lane < B_W_RANK, jnp.tanh(lora), lora).astype(BF16)
    wa = jnp.dot(lora, wwa_ref[...], preferred_element_type=F32)
    w_log = -jax.nn.softplus(-(w0_ref[...] + wa[:, :B_W])) - B_DECAY_OFFSET
    a = jax.nn.sigmoid(a0_ref[...] + wa[:, B_W:])
    zg = zs[:, B_LORA_OFF + LANES:B_LORA_OFF + LANES + B_G_RANK]
    def emit(ref, x):
        if not channel_major:
            ref[...] = x
            return
        xt_s = scratch[0]
        xt = x.T
        for j in range(xt_s.shape[0]):
            for h in range(B_HEADS):
                xt_s[j, h * KEY_PITCH:h * KEY_PITCH + B_DH, :] = xt[h * B_DH:(h + 1) * B_DH, j * LANES:(j + 1) * LANES]
        for c in range(B_DH):
            for j in range(xt_s.shape[0]):
                ref[c, :, j * LANES:(j + 1) * LANES] = xt_s[j, pl.ds(c, B_HEADS, stride=KEY_PITCH), :]

    emit(r_ref, r)
    emit(w_ref, jnp.exp(-jnp.exp(w_log)))
    emit(k_ref, k)
    emit(a_ref, a)
    emit(v_ref, zs[:, 2 * B_W:3 * B_W])
    g_ref[...] = jnp.dot(jax.nn.sigmoid(zg).astype(BF16), g2_ref[...], preferred_element_type=F32)


def rwkv_prep(zin, row0, b, L, gate_into, shift0, mu, w0, w2, a0, a2, g2):
    assert B_W_RANK + B_A_RANK == LANES
    T = math.gcd(L, RWKV_PREP_ROWS)
    channel_major = T == RWKV_PREP_ROWS
    nseq = 1 if channel_major else RWKV_SHORT_ROWS // T
    assert nseq == 1 or (L == T and b % nseq == 0)
    R = T * nseq
    nchunk = L // T
    blk0 = row0 // R
    wwa = jnp.zeros((LANES, 2 * B_W), F32).at[:B_W_RANK, :B_W].set(w2).at[B_W_RANK:, B_W:].set(a2).astype(BF16)
    row = lambda v: v.reshape(1, -1)
    if channel_major:
        blk = pl.BlockSpec((B_DH, B_HEADS, R), lambda i, c: (0, i, c))
        oshape = jax.ShapeDtypeStruct((B_DH, b * B_HEADS, L), F32)
    else:
        blk = pl.BlockSpec((R, B_W), lambda i, c: (i * nchunk + c, 0))
        oshape = jax.ShapeDtypeStruct((b * L, B_W), F32)
    gblk = pl.BlockSpec((R, B_W), lambda i, c: (blk0 + i * nchunk + c, 0))
    shspec = pl.BlockSpec((nseq, 1, B_COLS), lambda i, c: (i, 0, 0))
    const = lambda shape: pl.BlockSpec(shape, lambda i, c: (0,) * len(shape))
    in_specs = [pl.BlockSpec((R, B_COLS), lambda i, c: (blk0 + i * nchunk + c, 0)), shspec, const((1, B_COLS)),
                const((LANES, 2 * B_W)), const((B_G_RANK, B_W)), const((1, B_W)), const((1, B_W))]
    body, xspecs, xargs, aliases = _fill_into(
        functools.partial(_rwkv_prep_kernel, T=T, nseq=nseq, channel_major=channel_major),
        len(in_specs), gate_into, 5)
    outs = pl.pallas_call(
        body,
        grid=(b // nseq, nchunk),
        in_specs=in_specs + xspecs,
        out_specs=[blk] * 5 + [gblk, shspec],
        out_shape=[oshape] * 5
        + [jax.ShapeDtypeStruct((zin.shape[0], B_W), F32), jax.ShapeDtypeStruct((b, 1, B_COLS), F32)],
        input_output_aliases=aliases,
        scratch_shapes=[pltpu.VMEM((R // LANES, B_HEADS * KEY_PITCH, LANES), F32)] if channel_major else [],
        compiler_params=_cparams(("arbitrary", "arbitrary")),
        name="rwkv_prep",
    )(zin, shift0.reshape(b, 1, B_COLS), row(mu), wwa, g2.astype(BF16), row(w0), row(a0), *xargs)
    return outs[:5], outs[5], outs[6].reshape(b, B_COLS)


MLSTM_ROWS = 128
HIGHEST = lax.Precision.HIGHEST


def _mlstm_kernel(q_ref, k_ref, v_ref, o_ref, g_ref, gt_ref, brow_ref, bcol_ref, nw_ref,
                  c0_ref, n0_ref, m0_ref, h_ref, c_ref, n_ref, m_ref, *, T, nseq):
    R = MLSTM_ROWS

    @pl.when(pl.program_id(1) == 0)
    def _():
        c_ref[...] = c0_ref[...]
        n_ref[...] = n0_ref[...]
        m_ref[...] = m0_ref[...]

    shift = T.bit_length() - 1
    ri = lax.broadcasted_iota(jnp.int32, (R, R), 0)
    ci = lax.broadcasted_iota(jnp.int32, (R, R), 1)
    mask = (ci <= ri) & (jnp.right_shift(ri, shift) == jnp.right_shift(ci, shift))
    lmat = mask.astype(F32)
    rowid = lax.broadcasted_iota(jnp.int32, (R, 1), 0)
    rsel = [(rowid >= u * T) & (rowid < (u + 1) * T) for u in range(nseq)]

    g = g_ref[...] + brow_ref[...]
    lane = lax.broadcasted_iota(jnp.int32, g.shape, 1)
    glog = jnp.where((lane >= A_HEADS) & (lane < 2 * A_HEADS), jax.nn.log_sigmoid(g), g)
    gt = gt_ref[...] + bcol_ref[...]
    sub = lax.broadcasted_iota(jnp.int32, gt.shape, 0)
    gtlog = jnp.where(sub >= A_HEADS, jax.nn.log_sigmoid(gt), gt)
    bc_col = jnp.dot(lmat, glog, precision=HIGHEST, preferred_element_type=F32)
    bc_row = lax.dot_general(gtlog, lmat, (((1,), (1,)), ((), ())), precision=HIGHEST,
                             preferred_element_type=F32)
    lane_m = lax.broadcasted_iota(jnp.int32, (1, LANES), 1)
    m_old = [m_ref[u] for u in range(nseq)]
    m_out = [jnp.zeros((1, LANES), F32) for _ in range(nseq)]

    for h in range(A_HEADS):
        hs = slice(h * A_DK, (h + 1) * A_DK)
        bcc = bc_col[:, A_HEADS + h:A_HEADS + h + 1]
        bcr = bc_row[A_HEADS + h:A_HEADS + h + 1, :]
        lir = gtlog[h:h + 1, :]
        lic = glog[:, h:h + 1]
        m_u = [m_old[u][:, h:h + 1] for u in range(nseq)]
        m_col = m_u[0]
        for u in range(1, nseq):
            m_col = jnp.where(rsel[u], m_u[u], m_col)
        dmat = jnp.where(mask, bcc - bcr + lir, -jnp.inf)
        inter = bcc + m_col
        mt = jnp.maximum(inter, jnp.max(dmat, axis=1, keepdims=True))
        p = jnp.exp(dmat - mt)
        qh = q_ref[:, hs] * (A_DK ** -0.5)
        kh = k_ref[:, hs]
        qb, kb, vb = qh.astype(BF16), kh.astype(BF16), v_ref[:, hs].astype(BF16)
        wq = lax.dot_general(qb, kb, (((1,), (1,)), ((), ())), preferred_element_type=F32) * p
        wi = jnp.exp(inter - mt)
        c_old = [c_ref[u, h] for u in range(nseq)]
        n_old = [n_ref[u, h:h + 1, :] for u in range(nseq)]
        qc = jnp.dot(qb, c_old[0].astype(BF16), preferred_element_type=F32)
        qn = jnp.sum(qh * n_old[0], axis=1, keepdims=True)
        for u in range(1, nseq):
            qc = jnp.where(rsel[u], jnp.dot(qb, c_old[u].astype(BF16), preferred_element_type=F32), qc)
            qn = jnp.where(rsel[u], jnp.sum(qh * n_old[u], axis=1, keepdims=True), qn)
        num = jnp.dot(wq.astype(BF16), vb, preferred_element_type=F32) + wi * qc
        den = jnp.sum(wq, axis=1, keepdims=True) + wi * qn
        hh = num / jnp.maximum(jnp.abs(den), jnp.exp(-mt))
        hh = hh * lax.rsqrt(jnp.mean(hh * hh, axis=-1, keepdims=True) + EPS)
        h_ref[:, hs] = hh * nw_ref[:, hs] * jax.nn.sigmoid(o_ref[:, hs])
        for u in range(nseq):
            b_last = bcc[(u + 1) * T - 1:(u + 1) * T, :]
            gs = b_last - bcc + lic
            gmax = jnp.max(gs if nseq == 1 else jnp.where(rsel[u], gs, -jnp.inf), axis=0, keepdims=True)
            m_new = jnp.maximum(b_last + m_u[u], gmax)
            decay = jnp.exp(b_last + m_u[u] - m_new)
            ws = jnp.exp(gs - m_new)
            if nseq > 1:
                ws = jnp.where(rsel[u], ws, 0.0)
            kw = kh * ws
            c_ref[u, h] = decay * c_old[u] + lax.dot_general(
                kw.astype(BF16), vb, (((0,), (0,)), ((), ())), preferred_element_type=F32)
            n_ref[u, h:h + 1, :] = decay * n_old[u] + jnp.sum(kw, axis=0, keepdims=True)
            m_out[u] = jnp.where(lane_m == h, m_new, m_out[u])
    for u in range(nseq):
        m_ref[u] = m_out[u]


def mlstm(zin, gates_t, row0, b, L, h_into, layer, c_into, b_i, b_f, m_norm, C0, n0, m0):
    R = MLSTM_ROWS
    T = math.gcd(L, R)
    nseq = R // T
    assert nseq == 1 or (L == T and b % nseq == 0)
    nchunk = L // T
    blk0 = row0 // R
    bias = jnp.concatenate([b_i, b_f])
    bias_row = jnp.zeros((1, LANES), F32).at[0, :2 * A_HEADS].set(bias)
    bias_col = bias.reshape(2 * A_HEADS, 1)
    m0p = jnp.zeros((b, 1, LANES), F32).at[:, 0, :A_HEADS].set(m0)
    blk = lambda i, c: blk0 + i * nchunk + c
    rowblk = lambda col: pl.BlockSpec((R, A_QK), lambda i, c: (blk(i, c), EVEN_A_OFF // A_QK + col))
    cspec = pl.BlockSpec((nseq, A_HEADS, A_DK, A_DV), lambda i, c: (i, 0, 0, 0))
    cout = pl.BlockSpec((None, nseq, A_HEADS, A_DK, A_DV), lambda i, c: (layer, i, 0, 0, 0))
    nspec = pl.BlockSpec((nseq, A_HEADS, A_DK), lambda i, c: (i, 0, 0))
    mspec = pl.BlockSpec((nseq, 1, LANES), lambda i, c: (i, 0, 0))
    const = lambda shape: pl.BlockSpec(shape, lambda i, c: (0,) * len(shape))
    in_specs = [rowblk(0), rowblk(1), rowblk(2), rowblk(3),
                pl.BlockSpec((R, LANES), lambda i, c: (blk(i, c), EVEN_G_OFF // LANES)),
                pl.BlockSpec((None, 2 * A_HEADS, R), lambda i, c: (blk(i, c), 0, 0)),
                const((1, LANES)), const((2 * A_HEADS, 1)), const((1, A_V)),
                cspec if C0.ndim == 4 else cout, nspec, mspec]
    body, xspecs, xargs, aliases = _fill_into(
        functools.partial(_mlstm_kernel, T=T, nseq=nseq), len(in_specs), h_into, 0, c_into, 1)
    h, C, n, m = pl.pallas_call(
        body,
        grid=(b // nseq, nchunk),
        in_specs=in_specs + xspecs,
        out_specs=[pl.BlockSpec((R, A_V), lambda i, c: (blk(i, c), 0)), cout, nspec, mspec],
        out_shape=[jax.ShapeDtypeStruct((zin.shape[0], A_V), F32),
                   jax.ShapeDtypeStruct((N_EVEN,) + C0.shape[-4:], F32), jax.ShapeDtypeStruct(n0.shape, F32),
                   jax.ShapeDtypeStruct(m0p.shape, F32)],
        input_output_aliases=aliases,
        compiler_params=_cparams(("arbitrary", "arbitrary")),
        name="mlstm",
    )(zin, zin, zin, zin, zin, gates_t, bias_row, bias_col, m_norm.reshape(1, A_V), C0, n0, m0p, *xargs)
    return h, C, n, m[:, 0, :A_HEADS]


def even_mixer(zin, gates_t, row0, b, L, into, layer, c_into, C0, n0, m0, S0, shift0, b_i, b_f, m_norm,
               mu, w0, w2, a0, a2, g2, k_k, k_a, r_k, ln_w, ln_b):
    hA, C, n, m = mlstm(zin, gates_t, row0, b, L, into[0], layer, c_into, b_i, b_f, m_norm, C0, n0, m0)
    ops, g, shift = rwkv_prep(zin, row0, b, L, into[1], shift0, mu, w0, w2, a0, a2, g2)
    if ops[0].ndim == 3:
        hB, S = rwkv_scan_long(*ops, S0, k_k, k_a, r_k, ln_w, ln_b)
    else:
        hB, S = rwkv_scan_short(*ops, S0, L, k_k, k_a, r_k, ln_w, ln_b)
    return hA, hB, g, (C, n, m, S, shift)


C_PAIRS = C_HEADS // 2
C_GROUP_W = C_INNER // C_GROUPS
C_BC_W = 2 * C_GROUPS * C_STATE
SSD_SHORT_ROWS = 32


def _ssd_kernel(z_ref, x_ref, bc_ref, dt_ref, dtt_ref, cw_ref, cb_ref, dtb_ref, dtbt_ref, al_ref, alt_ref,
                dsk_ref, nw_ref, s0_ref, cv0_ref, y_ref, s_ref, cv_ref, stg_ref, *, T, nseq):
    R = T * nseq

    @pl.when(pl.program_id(1) == 0)
    def _():
        s_ref[...] = s0_ref[...]
        cv_ref[...] = cv0_ref[...]

    shift = T.bit_length() - 1
    ri = lax.broadcasted_iota(jnp.int32, (R, R), 0)
    ci = lax.broadcasted_iota(jnp.int32, (R, R), 1)
    mask = (ci <= ri) & (jnp.right_shift(ri, shift) == jnp.right_shift(ci, shift))
    lmat = mask.astype(F32)
    rowid = lax.broadcasted_iota(jnp.int32, (R, 1), 0)
    rsel = [(rowid >= u * T) & (rowid < (u + 1) * T) for u in range(nseq)]

    def conv_silu(src_ref, cols):
        accs = []
        for u in range(nseq):
            stg_ref[u, 0:SUBLANES, cols] = cv_ref[u, :, cols]
            stg_ref[u, SUBLANES:SUBLANES + T, cols] = src_ref[u * T:(u + 1) * T, :]
            acc = cb_ref[:, cols]
            for d in range(C_CONV):
                acc = acc + stg_ref[u, SUBLANES - d:SUBLANES - d + T, cols] * cw_ref[C_CONV - 1 - d:C_CONV - d, cols]
            accs.append(acc)
            cv_ref[u, :, cols] = stg_ref[u, T:T + SUBLANES, cols]
        acc = accs[0] if nseq == 1 else jnp.concatenate(accs, axis=0)
        return acc * jax.nn.sigmoid(acc)

    xc = conv_silu(x_ref, slice(0, C_INNER))
    bcc_ = conv_silu(bc_ref, slice(C_INNER, C_CONV_DIM))

    dtv = jax.nn.softplus(dt_ref[...] + dtb_ref[...])
    dtt = jax.nn.softplus(dtt_ref[...] + dtbt_ref[...])
    cum_col = jnp.dot(lmat, dtv * (-jnp.exp(al_ref[...])), precision=HIGHEST, preferred_element_type=F32)
    cum_row = lax.dot_general(dtt * (-jnp.exp(alt_ref[...])), lmat, (((1,), (1,)), ((), ())),
                              precision=HIGHEST, preferred_element_type=F32)

    lo = lax.broadcasted_iota(jnp.int32, (R, LANES), 1) < C_HEADDIM
    rlo = lax.broadcasted_iota(jnp.int32, (LANES, 1), 0) < C_HEADDIM
    nt = (((1,), (1,)), ((), ()))
    tn = (((0,), (0,)), ((), ()))
    pairs_per_group = C_PAIRS // C_GROUPS
    for g in range(C_GROUPS):
        bg = bcc_[:, g * C_STATE:(g + 1) * C_STATE].astype(BF16)
        cg = bcc_[:, (C_GROUPS + g) * C_STATE:(C_GROUPS + g + 1) * C_STATE].astype(BF16)
        cbm = lax.dot_general(cg, bg, nt, preferred_element_type=F32)
        ys = None
        for u in range(nseq):
            sg = s_ref[u, g * pairs_per_group:(g + 1) * pairs_per_group].reshape(C_GROUP_W, C_STATE)
            t_u = lax.dot_general(cg, sg.astype(BF16), nt, preferred_element_type=F32)
            ys = t_u if u == 0 else jnp.where(rsel[u], t_u, ys)
        for q in range(pairs_per_group):
            pr = g * pairs_per_group + q
            ps = slice(pr * LANES, (pr + 1) * LANES)
            xp = xc[:, ps]
            cc = [cum_col[:, 2 * pr + e:2 * pr + e + 1] for e in range(2)]
            intra = None
            for e, keep in ((0, lo), (1, jnp.logical_not(lo))):
                hh = 2 * pr + e
                seg = jnp.exp(jnp.where(mask, cc[e] - cum_row[hh:hh + 1, :], -jnp.inf))
                mix = cbm * seg * dtt[hh:hh + 1, :]
                part = jnp.dot(mix.astype(BF16), jnp.where(keep, xp, 0.0).astype(BF16),
                               preferred_element_type=F32)
                intra = part if intra is None else intra + part
            scale = jnp.where(lo, jnp.exp(cc[0]), jnp.exp(cc[1]))
            yp = intra + scale * ys[:, q * LANES:(q + 1) * LANES] + dsk_ref[:, ps] * xp
            zp = z_ref[:, ps]
            y_ref[:, ps] = yp * (zp * jax.nn.sigmoid(zp))
            for u in range(nseq):
                last = (u + 1) * T - 1
                ct = [cc[e][last:last + 1, :] for e in range(2)]
                tail = jnp.where(lo, jnp.exp(ct[0] - cc[0]) * dtv[:, 2 * pr:2 * pr + 1],
                                 jnp.exp(ct[1] - cc[1]) * dtv[:, 2 * pr + 1:2 * pr + 2])
                xw = xp * tail
                if nseq > 1:
                    xw = jnp.where(rsel[u], xw, 0.0)
                upd = lax.dot_general(xw.astype(BF16), bg, tn, preferred_element_type=F32)
                dec = jnp.where(rlo, jnp.exp(ct[0]), jnp.exp(ct[1]))
                s_ref[u, pr] = dec * s_ref[u, pr] + upd

    for g in range(C_GROUPS):
        gs_ = slice(g * C_GROUP_W, (g + 1) * C_GROUP_W)
        yg = y_ref[:, gs_]
        y_ref[:, gs_] = yg * lax.rsqrt(jnp.mean(yg * yg, axis=-1, keepdims=True) + EPS) * nw_ref[:, gs_]


def ssd_mixer(zin, row0, b, L, y_into, layer, s_into, ssm0, conv0, conv_w, conv_b, dt_bias, a_log, d_skip, norm_w):
    T = math.gcd(L, C_CHUNK)
    nseq = 1 if T == C_CHUNK else SSD_SHORT_ROWS // T
    assert nseq == 1 or (L == T and b % nseq == 0)
    R = T * nseq
    nchunk = L // T
    nblk = b * L // R
    blk0 = row0 // R
    dt_t = zin[row0:row0 + b * L, ODD_DT_OFF:ODD_DT_OFF + C_HEADS].reshape(nblk, R, C_HEADS).transpose(0, 2, 1)
    pad_row = lambda v: jnp.zeros((1, LANES), F32).at[0, :C_HEADS].set(v)
    pair_shape = (b, C_PAIRS, 2 * C_HEADDIM, C_STATE)
    per_layer = ssm0.ndim == 4
    s0 = ssm0.reshape(pair_shape if per_layer else (ssm0.shape[0],) + pair_shape)
    cv0 = jnp.concatenate([jnp.zeros((b, SUBLANES - (C_CONV - 1), C_CONV_DIM), F32), conv0], axis=1)
    blk = lambda i, c: blk0 + i * nchunk + c
    sspec = pl.BlockSpec((nseq, C_PAIRS, 2 * C_HEADDIM, C_STATE), lambda i, c: (i, 0, 0, 0))
    sout = pl.BlockSpec((None, nseq, C_PAIRS, 2 * C_HEADDIM, C_STATE), lambda i, c: (layer, i, 0, 0, 0))
    cvspec = pl.BlockSpec((nseq, SUBLANES, C_CONV_DIM), lambda i, c: (i, 0, 0))
    const = lambda shape: pl.BlockSpec(shape, lambda i, c: (0,) * len(shape))
    in_specs = [pl.BlockSpec((R, C_INNER), lambda i, c: (blk(i, c), 0)),
                pl.BlockSpec((R, C_INNER), lambda i, c: (blk(i, c), 1)),
                pl.BlockSpec((R, C_BC_W), lambda i, c: (blk(i, c), 2 * C_INNER // C_BC_W)),
                pl.BlockSpec((R, LANES), lambda i, c: (blk(i, c), ODD_DT_OFF // LANES)),
                pl.BlockSpec((None, C_HEADS, R), lambda i, c: (i * nchunk + c, 0, 0)),
                const((C_CONV, C_CONV_DIM)), const((1, C_CONV_DIM)),
                const((1, LANES)), const((C_HEADS, 1)), const((1, LANES)), const((C_HEADS, 1)),
                const((1, C_INNER)), const((1, C_INNER)), sspec if per_layer else sout, cvspec]
    body, xspecs, xargs, aliases = _fill_into(
        functools.partial(_ssd_kernel, T=T, nseq=nseq), len(in_specs), y_into, 0, s_into, 1)
    y, s, cv = pl.pallas_call(
        body,
        grid=(b // nseq, nchunk),
        in_specs=in_specs + xspecs,
        out_specs=[pl.BlockSpec((R, C_INNER), lambda i, c: (blk(i, c), 0)), sout, cvspec],
        out_shape=[jax.ShapeDtypeStruct((zin.shape[0], C_INNER), F32),
                   jax.ShapeDtypeStruct((N_ODD,) + pair_shape, F32), jax.ShapeDtypeStruct(cv0.shape, F32)],
        input_output_aliases=aliases,
        scratch_shapes=[pltpu.VMEM((nseq, SUBLANES + T, C_CONV_DIM), F32)],
        compiler_params=_cparams(("arbitrary", "arbitrary")),
        name="ssd",
    )(zin, zin, zin, zin, dt_t, conv_w, conv_b.reshape(1, C_CONV_DIM),
      pad_row(dt_bias), dt_bias.reshape(C_HEADS, 1), pad_row(a_log), a_log.reshape(C_HEADS, 1),
      jnp.repeat(d_skip, C_HEADDIM).reshape(1, C_INNER), norm_w.reshape(1, C_INNER), s0, cv0, *xargs)
    return y, s, cv[:, SUBLANES - (C_CONV - 1):]


def _even_w_in(w):
    qkvo = w[:, :2 * A_QK + 2 * A_V]
    gates = w[:, 2 * A_QK + 2 * A_V:A_COLS]
    rwkv = w[:, A_COLS:]
    pad = jnp.zeros((D_MODEL, EVEN_GATE_PAD - 2 * A_HEADS), w.dtype)
    return jnp.concatenate([rwkv, gates, pad, qkvo], axis=1).astype(BF16)


def _odd_w_in(w):
    pad = jnp.zeros((D_MODEL, ODD_DT_PAD - C_HEADS), w.dtype)
    return jnp.concatenate([w, pad], axis=1).astype(BF16)


def kernel(x_prompt, x_sample, state_mlstm_C, state_mlstm_n, state_mlstm_m, state_rwkv_S,
           state_rwkv_shift, state_ssm, state_conv, p_prompt, p_sample,
           norm_mix, norm_ffn, w_ffn_up, w_ffn_down, w_ple_proj, norm_ple, w_ple_gate, norm_final,
           w_in_even, mlstm_b_i, mlstm_b_f, mlstm_norm, rwkv_mu, rwkv_w0, rwkv_w2, rwkv_a0, rwkv_a2,
           rwkv_g2, rwkv_k_k, rwkv_k_a, rwkv_r_k, rwkv_ln_w, rwkv_ln_b, w_out_even,
           w_in_odd, conv_w, conv_b, dt_bias, a_log, d_skip, ssm_norm, w_out_odd):
    bp, Lp, _ = x_prompt.shape
    bs, Ls, _ = x_sample.shape
    n_p, n_s = bp * Lp, bs * Ls
    n_tot = n_p + n_s
    xp = x_prompt.reshape(n_p, D_MODEL)
    xs = x_sample.reshape(n_s, D_MODEL)
    pp = p_prompt.reshape(DEPTH, n_p, PLE_DIM)
    ps = p_sample.reshape(DEPTH, n_s, PLE_DIM)
    x = None

    even_small = (mlstm_b_i, mlstm_b_f, mlstm_norm, rwkv_mu, rwkv_w0, rwkv_w2, rwkv_a0, rwkv_a2,
                  rwkv_g2, rwkv_k_k, rwkv_k_a, rwkv_r_k, rwkv_ln_w, rwkv_ln_b)
    odd_small = (conv_w, conv_b, dt_bias, a_log, d_skip, ssm_norm)

    zeros_even = (jnp.zeros((bp, A_HEADS, A_DK, A_DV), F32), jnp.zeros((bp, A_HEADS, A_DK), F32),
                  jnp.zeros((bp, A_HEADS), F32), jnp.zeros((bp, B_HEADS, B_DH, B_DH), F32),
                  jnp.zeros((bp, B_COLS), F32))
    zeros_odd = (jnp.zeros((bp, C_HEADS, C_HEADDIM, C_STATE), F32),
                 jnp.zeros((bp, C_CONV - 1, C_CONV_DIM), F32))

    st_p_even, st_s_even, st_p_odd, st_s_odd = [], [], [], []
    mC_p = mC_s = ssm_p = ssm_s = None
    for i in range(DEPTH):
        j = i // 2
        if i % 2 == 0:
            w_in = _even_w_in(w_in_even[j])
            if x is None:
                zin = norm_matmul(xp, norm_mix[i], w_in, 512, 0, n_tot)
                zin = norm_matmul(xs, norm_mix[i], w_in, 512, n_p, n_tot, zin)
            else:
                zin = norm_matmul(x, norm_mix[i], w_in, 512)
            small = [t[j] for t in even_small]
            gates_t = zin[:, EVEN_G_OFF:EVEN_G_OFF + 2 * A_HEADS].reshape(
                -1, MLSTM_ROWS, 2 * A_HEADS).transpose(0, 2, 1)
            ha, hb_p, g, sp = even_mixer(zin, gates_t, 0, bp, Lp, (None, None), j, mC_p, *zeros_even, *small)
            ha, hb_s, g, ss = even_mixer(zin, gates_t, n_p, bs, Ls, (ha, g), j, mC_s, state_mlstm_C,
                                         state_mlstm_n[j], state_mlstm_m[j], state_rwkv_S[j], state_rwkv_shift[j],
                                         *small)
            mC_p, mC_s = sp[0], ss[0]
            st_p_even.append(sp)
            st_s_even.append(ss)
            wo = w_out_even[j].astype(BF16)
            terms_p = [(ha, None, wo[:A_V], None), (hb_p, g, wo[A_V:], Lp)]
            terms_s = [(ha, None, wo[:A_V], None), (hb_s, g, wo[A_V:], 0)]
            if x is None:
                xo = matmul_res(terms_p, xp, 0, n_p, None, n_tot)
                x = matmul_res(terms_s, xs, n_p, n_s, xo, n_tot)
            else:
                xo = matmul_res(terms_p, x, 0, n_p)
                x = matmul_res(terms_s, x, n_p, n_s, xo)
        else:
            zin = norm_matmul(x, norm_mix[i], _odd_w_in(w_in_odd[j]), 768)
            small = [t[j] for t in odd_small]
            mix, ssm_p, cv_p = ssd_mixer(zin, 0, bp, Lp, None, j, ssm_p, *zeros_odd, *small)
            mix, ssm_s, cv_s = ssd_mixer(zin, n_p, bs, Ls, mix, j, ssm_s, state_ssm, state_conv[j], *small)
            st_p_odd.append(cv_p)
            st_s_odd.append(cv_s)
            x = matmul_res([(mix, None, w_out_odd[j].astype(BF16), None)], x)
        tail = (norm_ffn[i], w_ffn_up[i].astype(BF16), w_ffn_down[i].astype(BF16),
                norm_ple[i], w_ple_gate[i].astype(BF16))
        wp = w_ple_proj[i].astype(BF16)
        xo = ffn_ple(x, *tail, pp[i], wp, 0)
        x = ffn_ple(x, *tail, ps[i], wp, n_p, xo)
    y_prompt = final_norm(x, norm_final, 0, n_p).reshape(bp, Lp, D_MODEL)
    y_sample = final_norm(x, norm_final, n_p, n_s).reshape(bs, Ls, D_MODEL)
    stack = lambda sts, idx: jnp.stack([s[idx] for s in sts])
    ssm_shape = lambda b: (N_ODD, b, C_HEADS, C_HEADDIM, C_STATE)
    return (y_prompt, y_sample,
            mC_p, stack(st_p_even, 1), stack(st_p_even, 2), stack(st_p_even, 3),
            stack(st_p_even, 4), ssm_p.reshape(ssm_shape(bp)), jnp.stack(st_p_odd),
            mC_s, stack(st_s_even, 1), stack(st_s_even, 2), stack(st_s_even, 3),
            stack(st_s_even, 4), ssm_s.reshape(ssm_shape(bs)), jnp.stack(st_s_odd))
```

```python
import math
import functools
import jax
import jax.numpy as jnp
from jax import lax
from jax.experimental import pallas as pl
from jax.experimental.pallas import tpu as pltpu

D_MODEL = 1024
DEPTH = 4
F32 = jnp.float32
BF16 = jnp.bfloat16
EPS = 1e-6
N_EVEN = (DEPTH + 1) // 2
N_ODD = DEPTH // 2
D_FF = 4 * D_MODEL
PLE_DIM = 256

A_HEADS = 4
A_DK = D_MODEL // 8
A_DV = D_MODEL // 8
A_QK = A_HEADS * A_DK
A_V = A_HEADS * A_DV
A_COLS = 2 * A_QK + 2 * A_V + 2 * A_HEADS

B_HEADS = 8
B_DH = 64
B_W = B_HEADS * B_DH
B_W_RANK = 64
B_A_RANK = 64
B_G_RANK = 128
B_COLS = 3 * B_W + B_W_RANK + B_A_RANK + B_G_RANK
B_DECAY_OFFSET = 0.5
B_GN_EPS = 64e-5

EVEN_COLS = A_COLS + B_COLS
EVEN_OUT = A_V + B_W

C_INNER = 2 * D_MODEL
C_HEADDIM = 64
C_HEADS = C_INNER // C_HEADDIM
C_GROUPS = 4
C_HPG = C_HEADS // C_GROUPS
C_STATE = 128
C_CONV = 4
C_CHUNK = 128
C_CONV_DIM = C_INNER + 2 * C_GROUPS * C_STATE
ODD_COLS = C_INNER + C_CONV_DIM + C_HEADS

LANES = 128
SUBLANES = 8
VMEM_LIMIT = 56 * 1024 * 1024
TOKEN_TILE = 512

EVEN_GATE_PAD = 256
EVEN_N = B_COLS + EVEN_GATE_PAD + 2 * A_QK + 2 * A_V
EVEN_G_OFF = B_COLS
EVEN_A_OFF = B_COLS + EVEN_GATE_PAD
ODD_DT_PAD = 256
ODD_N = C_INNER + C_CONV_DIM + ODD_DT_PAD
ODD_DT_OFF = C_INNER + C_CONV_DIM


def _cparams(sem):
    return pltpu.CompilerParams(dimension_semantics=sem, vmem_limit_bytes=VMEM_LIMIT)


def _rms(x, g):
    return x * lax.rsqrt(jnp.mean(x * x, axis=-1, keepdims=True) + EPS) * g


def _resident(shape):
    nd = len(shape)
    return pl.BlockSpec(shape, lambda *_: (0,) * nd, pipeline_mode=pl.Buffered(1))


def _rows(width):
    return pl.BlockSpec((TOKEN_TILE, width), lambda i: (i, 0))


def _fill_into(body, n_in, into, out_idx, into2=None, out_idx2=None):
    pairs = [(a, o) for a, o in ((into, out_idx), (into2, out_idx2)) if a is not None]
    if not pairs:
        return body, [], [], {}

    def skipping(*refs):
        return body(*refs[:n_in], *refs[n_in + len(pairs):])

    return (skipping, [pl.BlockSpec(memory_space=pl.ANY)] * len(pairs), [a for a, _ in pairs],
            {n_in + i: o for i, (_, o) in enumerate(pairs)})


def _norm_matmul_kernel(x_ref, g_ref, w_ref, o_ref, *, tn):
    xn = _rms(x_ref[...], g_ref[...]).astype(BF16)
    for n0 in range(0, w_ref.shape[1], tn):
        o_ref[:, n0:n0 + tn] = jnp.dot(xn, w_ref[:, n0:n0 + tn], preferred_element_type=F32)


def norm_matmul(x, g, w, tn, row0=0, total=None, into=None):
    m, n = x.shape[0], w.shape[1]
    total = m if total is None else total
    t0 = row0 // TOKEN_TILE
    in_specs = [_rows(D_MODEL), _resident((1, D_MODEL)), _resident(w.shape)]
    body, xspecs, xargs, aliases = _fill_into(functools.partial(_norm_matmul_kernel, tn=tn), len(in_specs), into, 0)
    return pl.pallas_call(
        body,
        grid=(m // TOKEN_TILE,),
        in_specs=in_specs + xspecs,
        out_specs=pl.BlockSpec((TOKEN_TILE, n), lambda i: (t0 + i, 0)),
        out_shape=jax.ShapeDtypeStruct((total, n), F32),
        input_output_aliases=aliases,
        compiler_params=_cparams(("arbitrary",)),
        name="norm_matmul",
    )(x, g.reshape(1, D_MODEL), w, *xargs)


FFN_CHUNK = 512


def _matmul_res_kernel(*refs, kinds):
    x_ref, o_ref = refs[-2], refs[-1]
    acc = x_ref[...]
    pos = 0
    for has_gate, channel_major in kinds:
        a = refs[pos][...]
        if channel_major:
            a = a.T
        if has_gate:
            a = a * refs[pos + 1][...]
        w_ref = refs[pos + 1 + has_gate]
        pos += 2 + has_gate
        acc = acc + jnp.dot(a.astype(BF16), w_ref[...], preferred_element_type=F32)
    o_ref[...] = acc


def matmul_res(terms, x, row0=0, nrows=None, into=None, total=None):
    m = x.shape[0] if total is None else total
    nrows = m if nrows is None else nrows
    t0 = row0 // TOKEN_TILE
    rows = lambda width: pl.BlockSpec((TOKEN_TILE, width), lambda i: (t0 + i, 0))
    xspec = rows(D_MODEL) if total is None else _rows(D_MODEL)
    specs, args, kinds = [], [], []
    for a, gate, w, seq_len in terms:
        if seq_len is None:
            specs.append(rows(a.shape[1]))
        elif seq_len == 0:
            specs.append(pl.BlockSpec((TOKEN_TILE, a.shape[1]), lambda i: (i, 0)))
        else:
            assert seq_len % TOKEN_TILE == 0
            per_seq = seq_len // TOKEN_TILE
            specs.append(pl.BlockSpec((None, a.shape[1], TOKEN_TILE), lambda i: (i // per_seq, 0, i % per_seq)))
        specs += ([rows(w.shape[0])] if gate is not None else []) + [_resident(w.shape)]
        args += [a] + ([gate] if gate is not None else []) + [w]
        kinds.append((int(gate is not None), bool(seq_len)))
    specs.append(xspec)
    body, xspecs, xargs, aliases = _fill_into(
        functools.partial(_matmul_res_kernel, kinds=tuple(kinds)), len(specs), into, 0)
    return pl.pallas_call(
        body,
        grid=(nrows // TOKEN_TILE,),
        in_specs=specs + xspecs,
        out_specs=rows(D_MODEL),
        out_shape=jax.ShapeDtypeStruct((m, D_MODEL), F32),
        input_output_aliases=aliases,
        compiler_params=_cparams(("arbitrary",)),
        name="matmul_res",
    )(*args, x, *xargs)


def _ffn_ple_kernel(x_ref, gf_ref, wu_ref, wd_ref, gp_ref, wg_ref, p_ref, wp_ref, o_ref, y_s):
    x = x_ref[...]
    xn = _rms(x, gf_ref[...]).astype(BF16)
    y_s[...] = x
    for c0 in range(0, D_FF, FFN_CHUNK):
        h = jnp.dot(xn, wu_ref[:, c0:c0 + FFN_CHUNK], preferred_element_type=F32)
        h = jnp.square(jnp.maximum(h, 0.0)).astype(BF16)
        y_s[...] += jnp.dot(h, wd_ref[c0:c0 + FFN_CHUNK, :], preferred_element_type=F32)
    y = y_s[...]
    yn = _rms(y, gp_ref[...]).astype(BF16)
    gate = jax.nn.sigmoid(jnp.dot(yn, wg_ref[...], preferred_element_type=F32))
    proj = jnp.dot(p_ref[...].astype(BF16), wp_ref[...], preferred_element_type=F32)
    o_ref[...] = y + proj * gate


def ffn_ple(x, g_ffn, wu, wd, g_ple, wg, p, wp, row0, into=None):
    t0 = row0 // TOKEN_TILE
    rows = pl.BlockSpec((TOKEN_TILE, D_MODEL), lambda i: (t0 + i, 0))
    row1 = lambda v: v.reshape(1, D_MODEL)
    in_specs = [rows, _resident((1, D_MODEL)), _resident(wu.shape), _resident(wd.shape),
                _resident((1, D_MODEL)), _resident(wg.shape), _rows(PLE_DIM), _resident(wp.shape)]
    body, xspecs, xargs, aliases = _fill_into(_ffn_ple_kernel, len(in_specs), into, 0)
    return pl.pallas_call(
        body,
        grid=(p.shape[0] // TOKEN_TILE,),
        in_specs=in_specs + xspecs,
        out_specs=rows,
        out_shape=jax.ShapeDtypeStruct(x.shape, F32),
        input_output_aliases=aliases,
        scratch_shapes=[pltpu.VMEM((TOKEN_TILE, D_MODEL), F32)],
        compiler_params=_cparams(("arbitrary",)),
        name="ffn_ple",
    )(x, row1(g_ffn), wu, wd, row1(g_ple), wg, p, wp, *xargs)


def _final_norm_kernel(x_ref, g_ref, o_ref):
    o_ref[...] = _rms(x_ref[...], g_ref[...])


def final_norm(x, g, row0, nrows):
    t0 = row0 // TOKEN_TILE
    return pl.pallas_call(
        _final_norm_kernel,
        grid=(nrows // TOKEN_TILE,),
        in_specs=[pl.BlockSpec((TOKEN_TILE, D_MODEL), lambda i: (t0 + i, 0)), _resident((1, D_MODEL))],
        out_specs=_rows(D_MODEL),
        out_shape=jax.ShapeDtypeStruct((nrows, D_MODEL), F32),
        compiler_params=_cparams(("arbitrary",)),
        name="final_norm",
    )(x, g.reshape(1, D_MODEL))


def _rwkv_operands(k, a, r, kk_t, ka_t, rk_t):
    kk = k * kk_t
    kk = kk * lax.rsqrt(jnp.maximum(jnp.sum(kk * kk, axis=1, keepdims=True), 1e-24))
    k2 = k * (1.0 + (a - 1.0) * ka_t)
    return -kk, kk * a, k2, jnp.sum(r * k2 * rk_t, axis=1)


def _rwkv_step(s_ref, tile0, ntiles, r, w, k, a, b, vrows):
    ys = []
    for lt in range(ntiles):
        rows = slice((tile0 + lt) * B_DH, (tile0 + lt + 1) * B_DH)
        s = s_ref[rows, :]
        sa = jnp.sum(s * a, axis=0, keepdims=True)
        s = s * w + sa * b + vrows[lt:lt + 1] * k
        s_ref[rows, :] = s
        ys.append(jnp.sum(s * r, axis=0, keepdims=True))
    return jnp.concatenate(ys, axis=0)


LONG_T = 128
LONG_NLT = B_DH // 2
KEY_PITCH = B_DH + SUBLANES
VAL_PITCH = LONG_NLT + SUBLANES


def _rwkv_scan_long_kernel(r_ref, w_ref, k_ref, a_ref, v_ref, kkt_ref, kat_ref, rkt_ref, lnw_ref, lnb_ref, s0_ref,
                           yt_ref, s_ref, or_s, ow_s, ok_s, oa_s, av_s, v_s, y_s):
    nb = yt_ref.shape[0]

    @pl.when(pl.program_id(0) == 0)
    def _():
        s_ref[...] = s0_ref[...]
        for scr in (or_s, ow_s, ok_s, oa_s, av_s, v_s, y_s):
            scr[...] = jnp.zeros(scr.shape, F32)

    lo = lax.broadcasted_iota(jnp.int32, (LONG_T, LANES), 1) < LANES // 2

    def tile(ref, c):
        m = ref[c]
        return jnp.concatenate([m, m], axis=0).T

    def relayout(c, carry):
        for ref, dst in ((r_ref, or_s), (w_ref, ow_s), (k_ref, ok_s), (a_ref, oa_s)):
            dst[pl.ds(c, LONG_T, stride=KEY_PITCH), :] = tile(ref, c)
            dst[pl.ds(c + LONG_NLT, LONG_T, stride=KEY_PITCH), :] = tile(ref, c + LONG_NLT)
        v_s[pl.ds(c, LONG_T, stride=VAL_PITCH), :] = jnp.where(lo, tile(v_ref, c), tile(v_ref, c + LONG_NLT))
        return carry

    lax.fori_loop(0, LONG_NLT, relayout, 0)

    def unpad(scr, pitch, n):
        return scr[...].reshape(LONG_T, pitch, LANES)[:, :n]

    def pad(x, pitch):
        zeros = jnp.zeros((LONG_T, pitch - x.shape[1], LANES), F32)
        return jnp.concatenate([x, zeros], axis=1).reshape(LONG_T * pitch, LANES)

    av, bv, k2, bonus = _rwkv_operands(unpad(ok_s, KEY_PITCH, B_DH), unpad(oa_s, KEY_PITCH, B_DH),
                                       unpad(or_s, KEY_PITCH, B_DH), kkt_ref[...], kat_ref[...], rkt_ref[...])
    av_s[...] = pad(av, KEY_PITCH)
    oa_s[...] = pad(bv, KEY_PITCH)
    ok_s[...] = pad(k2, KEY_PITCH)

    def step(t, carry):
        kr = pl.ds(pl.multiple_of(t * KEY_PITCH, SUBLANES), B_DH)
        vr = pl.ds(pl.multiple_of(t * VAL_PITCH, SUBLANES), LONG_NLT)
        y_s[vr, :] = _rwkv_step(s_ref, 0, LONG_NLT, or_s[kr, :], ow_s[kr, :], ok_s[kr, :], av_s[kr, :],
                                oa_s[kr, :], v_s[vr, :])
        return carry

    lax.fori_loop(0, LONG_T, step, 0)

    def head_sum(x):
        tot = jnp.sum(x, axis=1)
        return tot + pltpu.roll(tot, LANES // 2, axis=1)

    y = unpad(y_s, VAL_PITCH, LONG_NLT)
    d = y - (head_sum(y) * (1.0 / B_DH))[:, None, :]
    var = head_sum(d * d) * (1.0 / B_DH)
    y = (d * lax.rsqrt(var + B_GN_EPS)[:, None, :] * lnw_ref[...] + lnb_ref[...]
         + bonus[:, None, :] * unpad(v_s, VAL_PITCH, LONG_NLT))
    y_s[...] = pad(y, VAL_PITCH)

    def relayout_out(lt, carry):
        yt = y_s[pl.ds(lt, LONG_T, stride=VAL_PITCH), :].T
        for i2 in range(2):
            for b in range(nb):
                row = i2 * (LANES // 2) + b * B_HEADS
                yt_ref[b, pl.ds(lt + LONG_NLT * i2, B_HEADS, stride=B_DH), :] = yt[row:row + B_HEADS, :]
        return carry

    lax.fori_loop(0, LONG_NLT, relayout_out, 0)


def rwkv_scan_long(rc, wc, kc, ac, vc, S0, k_k, k_a, r_k, ln_w, ln_b):
    _, bh, L = rc.shape
    b = bh // B_HEADS
    assert bh * 2 == LANES and L % LONG_T == 0
    per_key = lambda p: jnp.tile(p.reshape(B_HEADS, B_DH).T, (1, LANES // B_HEADS))
    per_val = lambda p: jnp.concatenate(
        [jnp.tile(p.reshape(B_HEADS, 2, LONG_NLT)[:, i2].T, (1, b)) for i2 in range(2)], axis=1)
    s0 = S0.reshape(b, B_HEADS, 2, LONG_NLT, B_DH).transpose(3, 4, 2, 0, 1).reshape(LONG_NLT * B_DH, LANES)
    blk = pl.BlockSpec((B_DH, bh, LONG_T), lambda c: (0, 0, c))
    blk1 = pl.BlockSpec((B_DH, bh, LONG_T), lambda c: (0, 0, c), pipeline_mode=pl.Buffered(1))
    oblk = pl.BlockSpec((b, B_W, LONG_T), lambda c: (0, 0, c))
    const = lambda shape: pl.BlockSpec(shape, lambda c: (0,) * len(shape))
    big = pltpu.VMEM((LONG_T * KEY_PITCH, LANES), F32)
    small = pltpu.VMEM((LONG_T * VAL_PITCH, LANES), F32)
    yt, s = pl.pallas_call(
        _rwkv_scan_long_kernel,
        grid=(L // LONG_T,),
        in_specs=[blk] * 3 + [blk1] * 2 + [const((B_DH, LANES))] * 3 + [const((LONG_NLT, LANES))] * 2
        + [const((LONG_NLT * B_DH, LANES))],
        out_specs=[oblk, const((LONG_NLT * B_DH, LANES))],
        out_shape=[jax.ShapeDtypeStruct((b, B_W, L), F32), jax.ShapeDtypeStruct(s0.shape, F32)],
        scratch_shapes=[big] * 5 + [small, small],
        compiler_params=_cparams(("arbitrary",)),
        name="rwkv_scan_long",
    )(rc, wc, kc, ac, vc, per_key(k_k), per_key(k_a), per_key(r_k), per_val(ln_w), per_val(ln_b), s0)
    s = s.reshape(LONG_NLT, B_DH, 2, b, B_HEADS).transpose(3, 4, 2, 0, 1).reshape(b, B_HEADS, B_DH, B_DH)
    return yt, s


def _rwkv_scan_short_kernel(r_ref, w_ref, k_ref, a_ref, v_ref, kkt_ref, kat_ref, rkt_ref, lnw_ref, lnb_ref, s0_ref,
                            y_ref, sout_ref, s_s, or_s, ow_s, ok_s, oa_s, av_s, ov_s, y_s, *, L):
    nt = 2 * B_DH
    for q in range(nt * B_DH // LANES):
        s_s[q * LANES:(q + 1) * LANES, :] = s0_ref[:, q * LANES:(q + 1) * LANES].T
    for ref, dst in ((r_ref, or_s), (w_ref, ow_s), (k_ref, ok_s), (a_ref, oa_s), (v_ref, ov_s)):
        for t in range(L):
            dst[t] = ref[pl.ds(t, LANES, stride=L), :].T.reshape(2, B_DH, LANES)
    shape3 = (L * 2, B_DH, LANES)
    tiles = lambda ref: jnp.concatenate([ref[...]] * L, axis=0)
    av, bv, k2, bonus = _rwkv_operands(ok_s[...].reshape(shape3), oa_s[...].reshape(shape3),
                                       or_s[...].reshape(shape3), tiles(kkt_ref), tiles(kat_ref), tiles(rkt_ref))
    av_s[...] = av.reshape(L, 2, B_DH, LANES)
    oa_s[...] = bv.reshape(L, 2, B_DH, LANES)
    ok_s[...] = k2.reshape(L, 2, B_DH, LANES)

    def step(t, carry):
        for h in range(2):
            y_s[t, h] = _rwkv_step(s_s, h * B_DH, B_DH, or_s[t, h], ow_s[t, h], ok_s[t, h], av_s[t, h],
                                   oa_s[t, h], ov_s[t, h])
        return carry

    lax.fori_loop(0, L, step, 0)

    y = y_s[...]
    d = y - jnp.mean(y, axis=2, keepdims=True)
    var = jnp.mean(d * d, axis=2, keepdims=True)
    y = (d * lax.rsqrt(var + B_GN_EPS) * lnw_ref[...] + lnb_ref[...]
         + bonus.reshape(L, 2, 1, LANES) * ov_s[...])
    for t in range(L):
        y_ref[pl.ds(t, LANES, stride=L), :] = y[t].reshape(nt, LANES).T
    for q in range(nt * B_DH // LANES):
        sout_ref[:, q * LANES:(q + 1) * LANES] = s_s[q * LANES:(q + 1) * LANES, :].T


def rwkv_scan_short(r, w, k, a, v, S0, L, k_k, k_a, r_k, ln_w, ln_b):
    n = r.shape[0]
    b = n // L
    assert b == LANES
    npair = B_HEADS // 2
    wide = lambda p: jnp.broadcast_to(p.reshape(npair, 2, B_DH, 1), (npair, 2, B_DH, LANES))
    s0 = S0.reshape(b, B_HEADS * B_DH * B_DH)
    blk = pl.BlockSpec((n, LANES), lambda p: (0, p))
    cblk = pl.BlockSpec((None, 2, B_DH, LANES), lambda p: (p, 0, 0, 0))
    sblk = pl.BlockSpec((b, 2 * B_DH * B_DH), lambda p: (0, p))
    op = pltpu.VMEM((L, 2, B_DH, LANES), F32)
    y, s = pl.pallas_call(
        functools.partial(_rwkv_scan_short_kernel, L=L),
        grid=(npair,),
        in_specs=[blk] * 5 + [cblk] * 5 + [sblk],
        out_specs=[blk, sblk],
        out_shape=[jax.ShapeDtypeStruct((n, B_W), F32), jax.ShapeDtypeStruct(s0.shape, F32)],
        scratch_shapes=[pltpu.VMEM((2 * B_DH * B_DH, LANES), F32)] + [op] * 7,
        compiler_params=_cparams(("arbitrary",)),
        name="rwkv_scan_short",
    )(r, w, k, a, v, wide(k_k), wide(k_a), wide(r_k), wide(ln_w), wide(ln_b), s0)
    return y, s.reshape(b, B_HEADS, B_DH, B_DH)


RWKV_PREP_ROWS = 256
RWKV_SHORT_ROWS = 64
B_LORA_OFF = 3 * B_W


def _rwkv_prep_kernel(z_ref, sh0_ref, mu_ref, wwa_ref, g2_ref, w0_ref, a0_ref,
                      r_ref, w_ref, k_ref, a_ref, v_ref, g_ref, sh_ref, *scratch, T, nseq, channel_major):
    @pl.when(pl.program_id(1) == 0)
    def _():
        sh_ref[...] = sh0_ref[...]

    z = z_ref[...]
    rowid = lax.broadcasted_iota(jnp.int32, (z.shape[0], 1), 0)
    zprev = pltpu.roll(z, 1, axis=0)
    for u in range(nseq):
        zprev = jnp.where(rowid == u * T, sh_ref[u], zprev)
    for u in range(nseq):
        sh_ref[u] = z[(u + 1) * T - 1:(u + 1) * T, :]
    zs = z + (zprev - z) * mu_ref[...]
    r = zs[:, :B_W]
    k = zs[:, B_W:2 * B_W]
    lora = zs[:, B_LORA_OFF:B_LORA_OFF + LANES]
    lane = lax.broadcasted_iota(jnp.int32, lora.shape, 1)
    lora = jnp.where(lane < B_W_RANK, jnp.tanh(lora), lora).astype(BF16)
    wa = jnp.dot(lora, wwa_ref[...], preferred_element_type=F32)
    w_log = -jax.nn.softplus(-(w0_ref[...] + wa[:, :B_W])) - B_DECAY_OFFSET
    a = jax.nn.sigmoid(a0_ref[...] + wa[:, B_W:])
    zg = zs[:, B_LORA_OFF + LANES:B_LORA_OFF + LANES + B_G_RANK]
    def emit(ref, x):
        if not channel_major:
            ref[...] = x
            return
        xt_s = scratch[0]
        xt = x.T
        for j in range(xt_s.shape[0]):
            for h in range(B_HEADS):
                xt_s[j, h * KEY_PITCH:h * KEY_PITCH + B_DH, :] = xt[h * B_DH:(h + 1) * B_DH, j * LANES:(j + 1) * LANES]
        for c in range(B_DH):
            for j in range(xt_s.shape[0]):
                ref[c, :, j * LANES:(j + 1) * LANES] = xt_s[j, pl.ds(c, B_HEADS, stride=KEY_PITCH), :]

    emit(r_ref, r)
    emit(w_ref, jnp.exp(-jnp.exp(w_log)))
    emit(k_ref, k)
    emit(a_ref, a)
    emit(v_ref, zs[:, 2 * B_W:3 * B_W])
    g_ref[...] = jnp.dot(jax.nn.sigmoid(zg).astype(BF16), g2_ref[...], preferred_element_type=F32)


def rwkv_prep(zin, row0, b, L, gate_into, shift0, mu, w0, w2, a0, a2, g2):
    assert B_W_RANK + B_A_RANK == LANES
    T = math.gcd(L, RWKV_PREP_ROWS)
    channel_major = T == RWKV_PREP_ROWS
    nseq = 1 if channel_major else RWKV_SHORT_ROWS // T
    assert nseq == 1 or (L == T and b % nseq == 0)
    R = T * nseq
    nchunk = L // T
    blk0 = row0 // R
    wwa = jnp.zeros((LANES, 2 * B_W), F32).at[:B_W_RANK, :B_W].set(w2).at[B_W_RANK:, B_W:].set(a2).astype(BF16)
    row = lambda v: v.reshape(1, -1)
    if channel_major:
        blk = pl.BlockSpec((B_DH, B_HEADS, R), lambda i, c: (0, i, c))
        oshape = jax.ShapeDtypeStruct((B_DH, b * B_HEADS, L), F32)
    else:
        blk = pl.BlockSpec((R, B_W), lambda i, c: (i * nchunk + c, 0))
        oshape = jax.ShapeDtypeStruct((b * L, B_W), F32)
    gblk = pl.BlockSpec((R, B_W), lambda i, c: (blk0 + i * nchunk + c, 0))
    shspec = pl.BlockSpec((nseq, 1, B_COLS), lambda i, c: (i, 0, 0))
    const = lambda shape: pl.BlockSpec(shape, lambda i, c: (0,) * len(shape))
    in_specs = [pl.BlockSpec((R, B_COLS), lambda i, c: (blk0 + i * nchunk + c, 0)), shspec, const((1, B_COLS)),
                const((LANES, 2 * B_W)), const((B_G_RANK, B_W)), const((1, B_W)), const((1, B_W))]
    body, xspecs, xargs, aliases = _fill_into(
        functools.partial(_rwkv_prep_kernel, T=T, nseq=nseq, channel_major=channel_major),
        len(in_specs), gate_into, 5)
    outs = pl.pallas_call(
        body,
        grid=(b // nseq, nchunk),
        in_specs=in_specs + xspecs,
        out_specs=[blk] * 5 + [gblk, shspec],
        out_shape=[oshape] * 5
        + [jax.ShapeDtypeStruct((zin.shape[0], B_W), F32), jax.ShapeDtypeStruct((b, 1, B_COLS), F32)],
        input_output_aliases=aliases,
        scratch_shapes=[pltpu.VMEM((R // LANES, B_HEADS * KEY_PITCH, LANES), F32)] if channel_major else [],
        compiler_params=_cparams(("arbitrary", "arbitrary")),
        name="rwkv_prep",
    )(zin, shift0.reshape(b, 1, B_COLS), row(mu), wwa, g2.astype(BF16), row(w0), row(a0), *xargs)
    return outs[:5], outs[5], outs[6].reshape(b, B_COLS)


MLSTM_ROWS = 128
HIGHEST = lax.Precision.HIGHEST


def _mlstm_kernel(q_ref, k_ref, v_ref, o_ref, g_ref, gt_ref, brow_ref, bcol_ref, nw_ref,
                  c0_ref, n0_ref, m0_ref, h_ref, c_ref, n_ref, m_ref, *, T, nseq):
    R = MLSTM_ROWS

    @pl.when(pl.program_id(1) == 0)
    def _():
        c_ref[...] = c0_ref[...]
        n_ref[...] = n0_ref[...]
        m_ref[...] = m0_ref[...]

    shift = T.bit_length() - 1
    ri = lax.broadcasted_iota(jnp.int32, (R, R), 0)
    ci = lax.broadcasted_iota(jnp.int32, (R, R), 1)
    mask = (ci <= ri) & (jnp.right_shift(ri, shift) == jnp.right_shift(ci, shift))
    lmat = mask.astype(F32)
    rowid = lax.broadcasted_iota(jnp.int32, (R, 1), 0)
    rsel = [(rowid >= u * T) & (rowid < (u + 1) * T) for u in range(nseq)]

    g = g_ref[...] + brow_ref[...]
    lane = lax.broadcasted_iota(jnp.int32, g.shape, 1)
    glog = jnp.where((lane >= A_HEADS) & (lane < 2 * A_HEADS), jax.nn.log_sigmoid(g), g)
    gt = gt_ref[...] + bcol_ref[...]
    sub = lax.broadcasted_iota(jnp.int32, gt.shape, 0)
    gtlog = jnp.where(sub >= A_HEADS, jax.nn.log_sigmoid(gt), gt)
    bc_col = jnp.dot(lmat, glog, precision=HIGHEST, preferred_element_type=F32)
    bc_row = lax.dot_general(gtlog, lmat, (((1,), (1,)), ((), ())), precision=HIGHEST,
                             preferred_element_type=F32)
    lane_m = lax.broadcasted_iota(jnp.int32, (1, LANES), 1)
    m_old = [m_ref[u] for u in range(nseq)]
    m_out = [jnp.zeros((1, LANES), F32) for _ in range(nseq)]

    for h in range(A_HEADS):
        hs = slice(h * A_DK, (h + 1) * A_DK)
        bcc = bc_col[:, A_HEADS + h:A_HEADS + h + 1]
        bcr = bc_row[A_HEADS + h:A_HEADS + h + 1, :]
        lir = gtlog[h:h + 1, :]
        lic = glog[:, h:h + 1]
        m_u = [m_old[u][:, h:h + 1] for u in range(nseq)]
        m_col = m_u[0]
        for u in range(1, nseq):
            m_col = jnp.where(rsel[u], m_u[u], m_col)
        dmat = jnp.where(mask, bcc - bcr + lir, -jnp.inf)
        inter = bcc + m_col
        mt = jnp.maximum(inter, jnp.max(dmat, axis=1, keepdims=True))
        p = jnp.exp(dmat - mt)
        qh = q_ref[:, hs] * (A_DK ** -0.5)
        kh = k_ref[:, hs]
        qb, kb, vb = qh.astype(BF16), kh.astype(BF16), v_ref[:, hs].astype(BF16)
        wq = lax.dot_general(qb, kb, (((1,), (1,)), ((), ())), preferred_element_type=F32) * p
        wi = jnp.exp(inter - mt)
        c_old = [c_ref[u, h] for u in range(nseq)]
        n_old = [n_ref[u, h:h + 1, :] for u in range(nseq)]
        qc = jnp.dot(qb, c_old[0].astype(BF16), preferred_element_type=F32)
        qn = jnp.sum(qh * n_old[0], axis=1, keepdims=True)
        for u in range(1, nseq):
            qc = jnp.where(rsel[u], jnp.dot(qb, c_old[u].astype(BF16), preferred_element_type=F32), qc)
            qn = jnp.where(rsel[u], jnp.sum(qh * n_old[u], axis=1, keepdims=True), qn)
        num = jnp.dot(wq.astype(BF16), vb, preferred_element_type=F32) + wi * qc
        den = jnp.sum(wq, axis=1, keepdims=True) + wi * qn
        hh = num / jnp.maximum(jnp.abs(den), jnp.exp(-mt))
        hh = hh * lax.rsqrt(jnp.mean(hh * hh, axis=-1, keepdims=True) + EPS)
        h_ref[:, hs] = hh * nw_ref[:, hs] * jax.nn.sigmoid(o_ref[:, hs])
        for u in range(nseq):
            b_last = bcc[(u + 1) * T - 1:(u + 1) * T, :]
            gs = b_last - bcc + lic
            gmax = jnp.max(gs if nseq == 1 else jnp.where(rsel[u], gs, -jnp.inf), axis=0, keepdims=True)
            m_new = jnp.maximum(b_last + m_u[u], gmax)
            decay = jnp.exp(b_last + m_u[u] - m_new)
            ws = jnp.exp(gs - m_new)
            if nseq > 1:
                ws = jnp.where(rsel[u], ws, 0.0)
            kw = kh * ws
            c_ref[u, h] = decay * c_old[u] + lax.dot_general(
                kw.astype(BF16), vb, (((0,), (0,)), ((), ())), preferred_element_type=F32)
            n_ref[u, h:h + 1, :] = decay * n_old[u] + jnp.sum(kw, axis=0, keepdims=True)
            m_out[u] = jnp.where(lane_m == h, m_new, m_out[u])
    for u in range(nseq):
        m_ref[u] = m_out[u]


def mlstm(zin, gates_t, row0, b, L, h_into, layer, c_into, b_i, b_f, m_norm, C0, n0, m0):
    R = MLSTM_ROWS
    T = math.gcd(L, R)
    nseq = R // T
    assert nseq == 1 or (L == T and b % nseq == 0)
    nchunk = L // T
    blk0 = row0 // R
    bias = jnp.concatenate([b_i, b_f])
    bias_row = jnp.zeros((1, LANES), F32).at[0, :2 * A_HEADS].set(bias)
    bias_col = bias.reshape(2 * A_HEADS, 1)
    m0p = jnp.zeros((b, 1, LANES), F32).at[:, 0, :A_HEADS].set(m0)
    blk = lambda i, c: blk0 + i * nchunk + c
    rowblk = lambda col: pl.BlockSpec((R, A_QK), lambda i, c: (blk(i, c), EVEN_A_OFF // A_QK + col))
    cspec = pl.BlockSpec((nseq, A_HEADS, A_DK, A_DV), lambda i, c: (i, 0, 0, 0))
    cout = pl.BlockSpec((None, nseq, A_HEADS, A_DK, A_DV), lambda i, c: (layer, i, 0, 0, 0))
    nspec = pl.BlockSpec((nseq, A_HEADS, A_DK), lambda i, c: (i, 0, 0))
    mspec = pl.BlockSpec((nseq, 1, LANES), lambda i, c: (i, 0, 0))
    const = lambda shape: pl.BlockSpec(shape, lambda i, c: (0,) * len(shape))
    in_specs = [rowblk(0), rowblk(1), rowblk(2), rowblk(3),
                pl.BlockSpec((R, LANES), lambda i, c: (blk(i, c), EVEN_G_OFF // LANES)),
                pl.BlockSpec((None, 2 * A_HEADS, R), lambda i, c: (blk(i, c), 0, 0)),
                const((1, LANES)), const((2 * A_HEADS, 1)), const((1, A_V)),
                cspec if C0.ndim == 4 else cout, nspec, mspec]
    body, xspecs, xargs, aliases = _fill_into(
        functools.partial(_mlstm_kernel, T=T, nseq=nseq), len(in_specs), h_into, 0, c_into, 1)
    h, C, n, m = pl.pallas_call(
        body,
        grid=(b // nseq, nchunk),
        in_specs=in_specs + xspecs,
        out_specs=[pl.BlockSpec((R, A_V), lambda i, c: (blk(i, c), 0)), cout, nspec, mspec],
        out_shape=[jax.ShapeDtypeStruct((zin.shape[0], A_V), F32),
                   jax.ShapeDtypeStruct((N_EVEN,) + C0.shape[-4:], F32), jax.ShapeDtypeStruct(n0.shape, F32),
                   jax.ShapeDtypeStruct(m0p.shape, F32)],
        input_output_aliases=aliases,
        compiler_params=_cparams(("arbitrary", "arbitrary")),
        name="mlstm",
    )(zin, zin, zin, zin, zin, gates_t, bias_row, bias_col, m_norm.reshape(1, A_V), C0, n0, m0p, *xargs)
    return h, C, n, m[:, 0, :A_HEADS]


def even_mixer(zin, gates_t, row0, b, L, into, layer, c_into, C0, n0, m0, S0, shift0, b_i, b_f, m_norm,
               mu, w0, w2, a0, a2, g2, k_k, k_a, r_k, ln_w, ln_b):
    hA, C, n, m = mlstm(zin, gates_t, row0, b, L, into[0], layer, c_into, b_i, b_f, m_norm, C0, n0, m0)
    ops, g, shift = rwkv_prep(zin, row0, b, L, into[1], shift0, mu, w0, w2, a0, a2, g2)
    if ops[0].ndim == 3:
        hB, S = rwkv_scan_long(*ops, S0, k_k, k_a, r_k, ln_w, ln_b)
    else:
        hB, S = rwkv_scan_short(*ops, S0, L, k_k, k_a, r_k, ln_w, ln_b)
    return hA, hB, g, (C, n, m, S, shift)


C_PAIRS = C_HEADS // 2
C_GROUP_W = C_INNER // C_GROUPS
C_BC_W = 2 * C_GROUPS * C_STATE
SSD_LONG_ROWS = C_CHUNK
SSD_SHORT_ROWS = 32


def _ssd_kernel(z_ref, x_ref, bc_ref, dt_ref, dtt_ref, cw_ref, cb_ref, dtb_ref, dtbt_ref, al_ref, alt_ref,
                dsk_ref, nw_ref, s0_ref, cv0_ref, y_ref, s_ref, cv_ref, stg_ref, *, T, nseq):
    R = T * nseq

    @pl.when(pl.program_id(1) == 0)
    def _():
        s_ref[...] = s0_ref[...]
        cv_ref[...] = cv0_ref[...]

    shift = T.bit_length() - 1
    ri = lax.broadcasted_iota(jnp.int32, (R, R), 0)
    ci = lax.broadcasted_iota(jnp.int32, (R, R), 1)
    mask = (ci <= ri) & (jnp.right_shift(ri, shift) == jnp.right_shift(ci, shift))
    lmat = mask.astype(F32)
    rowid = lax.broadcasted_iota(jnp.int32, (R, 1), 0)
    rsel = [(rowid >= u * T) & (rowid < (u + 1) * T) for u in range(nseq)]

    def conv_silu(src_ref, cols):
        accs = []
        for u in range(nseq):
            stg_ref[u, 0:SUBLANES, cols] = cv_ref[u, :, cols]
            stg_ref[u, SUBLANES:SUBLANES + T, cols] = src_ref[u * T:(u + 1) * T, :]
            acc = cb_ref[:, cols]
            for d in range(C_CONV):
                acc = acc + stg_ref[u, SUBLANES - d:SUBLANES - d + T, cols] * cw_ref[C_CONV - 1 - d:C_CONV - d, cols]
            accs.append(acc)
            cv_ref[u, :, cols] = stg_ref[u, T:T + SUBLANES, cols]
        acc = accs[0] if nseq == 1 else jnp.concatenate(accs, axis=0)
        return acc * jax.nn.sigmoid(acc)

    xc = conv_silu(x_ref, slice(0, C_INNER))
    bcc_ = conv_silu(bc_ref, slice(C_INNER, C_CONV_DIM))

    dtv = jax.nn.softplus(dt_ref[...] + dtb_ref[...])
    dtt = jax.nn.softplus(dtt_ref[...] + dtbt_ref[...])
    cum_col = jnp.dot(lmat, dtv * (-jnp.exp(al_ref[...])), precision=HIGHEST, preferred_element_type=F32)
    cum_row = lax.dot_general(dtt * (-jnp.exp(alt_ref[...])), lmat, (((1,), (1,)), ((), ())),
                              precision=HIGHEST, preferred_element_type=F32)

    lo = lax.broadcasted_iota(jnp.int32, (R, LANES), 1) < C_HEADDIM
    rlo = lax.broadcasted_iota(jnp.int32, (LANES, 1), 0) < C_HEADDIM
    nt = (((1,), (1,)), ((), ()))
    tn = (((0,), (0,)), ((), ()))
    pairs_per_group = C_PAIRS // C_GROUPS
    for g in range(C_GROUPS):
        bg = bcc_[:, g * C_STATE:(g + 1) * C_STATE].astype(BF16)
        cg = bcc_[:, (C_GROUPS + g) * C_STATE:(C_GROUPS + g + 1) * C_STATE].astype(BF16)
        cbm = lax.dot_general(cg, bg, nt, preferred_element_type=F32)
        ys = None
        for u in range(nseq):
            sg = s_ref[u, g * pairs_per_group:(g + 1) * pairs_per_group].reshape(C_GROUP_W, C_STATE)
            t_u = lax.dot_general(cg, sg.astype(BF16), nt, preferred_element_type=F32)
            ys = t_u if u == 0 else jnp.where(rsel[u], t_u, ys)
        for q in range(pairs_per_group):
            pr = g * pairs_per_group + q
            ps = slice(pr * LANES, (pr + 1) * LANES)
            xp = xc[:, ps]
            cc = [cum_col[:, 2 * pr + e:2 * pr + e + 1] for e in range(2)]
            intra = None
            for e, keep in ((0, lo), (1, jnp.logical_not(lo))):
                hh = 2 * pr + e
                seg = jnp.exp(jnp.where(mask, cc[e] - cum_row[hh:hh + 1, :], -jnp.inf))
                mix = cbm * seg * dtt[hh:hh + 1, :]
                part = jnp.dot(mix.astype(BF16), jnp.where(keep, xp, 0.0).astype(BF16),
                               preferred_element_type=F32)
                intra = part if intra is None else intra + part
            scale = jnp.where(lo, jnp.exp(cc[0]), jnp.exp(cc[1]))
            yp = intra + scale * ys[:, q * LANES:(q + 1) * LANES] + dsk_ref[:, ps] * xp
            zp = z_ref[:, ps]
            y_ref[:, ps] = yp * (zp * jax.nn.sigmoid(zp))
            for u in range(nseq):
                last = (u + 1) * T - 1
                ct = [cc[e][last:last + 1, :] for e in range(2)]
                tail = jnp.where(lo, jnp.exp(ct[0] - cc[0]) * dtv[:, 2 * pr:2 * pr + 1],
                                 jnp.exp(ct[1] - cc[1]) * dtv[:, 2 * pr + 1:2 * pr + 2])
                xw = xp * tail
                if nseq > 1:
                    xw = jnp.where(rsel[u], xw, 0.0)
                upd = lax.dot_general(xw.astype(BF16), bg, tn, preferred_element_type=F32)
                dec = jnp.where(rlo, jnp.exp(ct[0]), jnp.exp(ct[1]))
                s_ref[u, pr] = dec * s_ref[u, pr] + upd

    for g in range(C_GROUPS):
        gs_ = slice(g * C_GROUP_W, (g + 1) * C_GROUP_W)
        yg = y_ref[:, gs_]
        y_ref[:, gs_] = yg * lax.rsqrt(jnp.mean(yg * yg, axis=-1, keepdims=True) + EPS) * nw_ref[:, gs_]


def ssd_mixer(zin, row0, b, L, y_into, layer, s_into, ssm0, conv0, conv_w, conv_b, dt_bias, a_log, d_skip, norm_w):
    T = math.gcd(L, SSD_LONG_ROWS)
    nseq = 1 if T == SSD_LONG_ROWS else SSD_SHORT_ROWS // T
    assert nseq == 1 or (L == T and b % nseq == 0)
    R = T * nseq
    nchunk = L // T
    nblk = b * L // R
    blk0 = row0 // R
    dt_t = zin[row0:row0 + b * L, ODD_DT_OFF:ODD_DT_OFF + C_HEADS].reshape(nblk, R, C_HEADS).transpose(0, 2, 1)
    pad_row = lambda v: jnp.zeros((1, LANES), F32).at[0, :C_HEADS].set(v)
    pair_shape = (b, C_PAIRS, 2 * C_HEADDIM, C_STATE)
    per_layer = ssm0.ndim == 4
    s0 = ssm0.reshape(pair_shape if per_layer else (ssm0.shape[0],) + pair_shape)
    cv0 = jnp.concatenate([jnp.zeros((b, SUBLANES - (C_CONV - 1), C_CONV_DIM), F32), conv0], axis=1)
    blk = lambda i, c: blk0 + i * nchunk + c
    sspec = pl.BlockSpec((nseq, C_PAIRS, 2 * C_HEADDIM, C_STATE), lambda i, c: (i, 0, 0, 0))
    sout = pl.BlockSpec((None, nseq, C_PAIRS, 2 * C_HEADDIM, C_STATE), lambda i, c: (layer, i, 0, 0, 0))
    cvspec = pl.BlockSpec((nseq, SUBLANES, C_CONV_DIM), lambda i, c: (i, 0, 0))
    const = lambda shape: pl.BlockSpec(shape, lambda i, c: (0,) * len(shape))
    in_specs = [pl.BlockSpec((R, C_INNER), lambda i, c: (blk(i, c), 0)),
                pl.BlockSpec((R, C_INNER), lambda i, c: (blk(i, c), 1)),
                pl.BlockSpec((R, C_BC_W), lambda i, c: (blk(i, c), 2 * C_INNER // C_BC_W)),
                pl.BlockSpec((R, LANES), lambda i, c: (blk(i, c), ODD_DT_OFF // LANES)),
                pl.BlockSpec((None, C_HEADS, R), lambda i, c: (i * nchunk + c, 0, 0)),
                const((C_CONV, C_CONV_DIM)), const((1, C_CONV_DIM)),
                const((1, LANES)), const((C_HEADS, 1)), const((1, LANES)), const((C_HEADS, 1)),
                const((1, C_INNER)), const((1, C_INNER)), sspec if per_layer else sout, cvspec]
    body, xspecs, xargs, aliases = _fill_into(
        functools.partial(_ssd_kernel, T=T, nseq=nseq), len(in_specs), y_into, 0, s_into, 1)
    y, s, cv = pl.pallas_call(
        body,
        grid=(b // nseq, nchunk),
        in_specs=in_specs + xspecs,
        out_specs=[pl.BlockSpec((R, C_INNER), lambda i, c: (blk(i, c), 0)), sout, cvspec],
        out_shape=[jax.ShapeDtypeStruct((zin.shape[0], C_INNER), F32),
                   jax.ShapeDtypeStruct((N_ODD,) + pair_shape, F32), jax.ShapeDtypeStruct(cv0.shape, F32)],
        input_output_aliases=aliases,
        scratch_shapes=[pltpu.VMEM((nseq, SUBLANES + T, C_CONV_DIM), F32)],
        compiler_params=_cparams(("arbitrary", "arbitrary")),
        name="ssd",
    )(zin, zin, zin, zin, dt_t, conv_w, conv_b.reshape(1, C_CONV_DIM),
      pad_row(dt_bias), dt_bias.reshape(C_HEADS, 1), pad_row(a_log), a_log.reshape(C_HEADS, 1),
      jnp.repeat(d_skip, C_HEADDIM).reshape(1, C_INNER), norm_w.reshape(1, C_INNER), s0, cv0, *xargs)
    return y, s, cv[:, SUBLANES - (C_CONV - 1):]


def _even_w_in(w):
    qkvo = w[:, :2 * A_QK + 2 * A_V]
    gates = w[:, 2 * A_QK + 2 * A_V:A_COLS]
    rwkv = w[:, A_COLS:]
    pad = jnp.zeros((D_MODEL, EVEN_GATE_PAD - 2 * A_HEADS), w.dtype)
    return jnp.concatenate([rwkv, gates, pad, qkvo], axis=1).astype(BF16)


def _odd_w_in(w):
    pad = jnp.zeros((D_MODEL, ODD_DT_PAD - C_HEADS), w.dtype)
    return jnp.concatenate([w, pad], axis=1).astype(BF16)


def kernel(x_prompt, x_sample, state_mlstm_C, state_mlstm_n, state_mlstm_m, state_rwkv_S,
           state_rwkv_shift, state_ssm, state_conv, p_prompt, p_sample,
           norm_mix, norm_ffn, w_ffn_up, w_ffn_down, w_ple_proj, norm_ple, w_ple_gate, norm_final,
           w_in_even, mlstm_b_i, mlstm_b_f, mlstm_norm, rwkv_mu, rwkv_w0, rwkv_w2, rwkv_a0, rwkv_a2,
           rwkv_g2, rwkv_k_k, rwkv_k_a, rwkv_r_k, rwkv_ln_w, rwkv_ln_b, w_out_even,
           w_in_odd, conv_w, conv_b, dt_bias, a_log, d_skip, ssm_norm, w_out_odd):
    bp, Lp, _ = x_prompt.shape
    bs, Ls, _ = x_sample.shape
    n_p, n_s = bp * Lp, bs * Ls
    n_tot = n_p + n_s
    xp = x_prompt.reshape(n_p, D_MODEL)
    xs = x_sample.reshape(n_s, D_MODEL)
    pp = p_prompt.reshape(DEPTH, n_p, PLE_DIM)
    ps = p_sample.reshape(DEPTH, n_s, PLE_DIM)
    x = None

    even_small = (mlstm_b_i, mlstm_b_f, mlstm_norm, rwkv_mu, rwkv_w0, rwkv_w2, rwkv_a0, rwkv_a2,
                  rwkv_g2, rwkv_k_k, rwkv_k_a, rwkv_r_k, rwkv_ln_w, rwkv_ln_b)
    odd_small = (conv_w, conv_b, dt_bias, a_log, d_skip, ssm_norm)

    zeros_even = (jnp.zeros((bp, A_HEADS, A_DK, A_DV), F32), jnp.zeros((bp, A_HEADS, A_DK), F32),
                  jnp.zeros((bp, A_HEADS), F32), jnp.zeros((bp, B_HEADS, B_DH, B_DH), F32),
                  jnp.zeros((bp, B_COLS), F32))
    zeros_odd = (jnp.zeros((bp, C_HEADS, C_HEADDIM, C_STATE), F32),
                 jnp.zeros((bp, C_CONV - 1, C_CONV_DIM), F32))

    st_p_even, st_s_even, st_p_odd, st_s_odd = [], [], [], []
    mC_p = mC_s = ssm_p = ssm_s = None
    for i in range(DEPTH):
        j = i // 2
        if i % 2 == 0:
            w_in = _even_w_in(w_in_even[j])
            if x is None:
                zin = norm_matmul(xp, norm_mix[i], w_in, 512, 0, n_tot)
                zin = norm_matmul(xs, norm_mix[i], w_in, 512, n_p, n_tot, zin)
            else:
                zin = norm_matmul(x, norm_mix[i], w_in, 512)
            small = [t[j] for t in even_small]
            gates_t = zin[:, EVEN_G_OFF:EVEN_G_OFF + 2 * A_HEADS].reshape(
                -1, MLSTM_ROWS, 2 * A_HEADS).transpose(0, 2, 1)
            ha, hb_p, g, sp = even_mixer(zin, gates_t, 0, bp, Lp, (None, None), j, mC_p, *zeros_even, *small)
            ha, hb_s, g, ss = even_mixer(zin, gates_t, n_p, bs, Ls, (ha, g), j, mC_s, state_mlstm_C,
                                         state_mlstm_n[j], state_mlstm_m[j], state_rwkv_S[j], state_rwkv_shift[j],
                                         *small)
            mC_p, mC_s = sp[0], ss[0]
            st_p_even.append(sp)
            st_s_even.append(ss)
            wo = w_out_even[j].astype(BF16)
            terms_p = [(ha, None, wo[:A_V], None), (hb_p, g, wo[A_V:], Lp)]
            terms_s = [(ha, None, wo[:A_V], None), (hb_s, g, wo[A_V:], 0)]
            if x is None:
                xo = matmul_res(terms_p, xp, 0, n_p, None, n_tot)
                x = matmul_res(terms_s, xs, n_p, n_s, xo, n_tot)
            else:
                xo = matmul_res(terms_p, x, 0, n_p)
                x = matmul_res(terms_s, x, n_p, n_s, xo)
        else:
            zin = norm_matmul(x, norm_mix[i], _odd_w_in(w_in_odd[j]), 768)
            small = [t[j] for t in odd_small]
            mix, ssm_p, cv_p = ssd_mixer(zin, 0, bp, Lp, None, j, ssm_p, *zeros_odd, *small)
            mix, ssm_s, cv_s = ssd_mixer(zin, n_p, bs, Ls, mix, j, ssm_s, state_ssm, state_conv[j], *small)
            st_p_odd.append(cv_p)
            st_s_odd.append(cv_s)
            x = matmul_res([(mix, None, w_out_odd[j].astype(BF16), None)], x)
        tail = (norm_ffn[i], w_ffn_up[i].astype(BF16), w_ffn_down[i].astype(BF16),
                norm_ple[i], w_ple_gate[i].astype(BF16))
        wp = w_ple_proj[i].astype(BF16)
        xo = ffn_ple(x, *tail, pp[i], wp, 0)
        x = ffn_ple(x, *tail, ps[i], wp, n_p, xo)
    y_prompt = final_norm(x, norm_final, 0, n_p).reshape(bp, Lp, D_MODEL)
    y_sample = final_norm(x, norm_final, n_p, n_s).reshape(bs, Ls, D_MODEL)
    stack = lambda sts, idx: jnp.stack([s[idx] for s in sts])
    ssm_shape = lambda b: (N_ODD, b, C_HEADS, C_HEADDIM, C_STATE)
    return (y_prompt, y_sample,
            mC_p, stack(st_p_even, 1), stack(st_p_even, 2), stack(st_p_even, 3),
            stack(st_p_even, 4), ssm_p.reshape(ssm_shape(bp)), jnp.stack(st_p_odd),
            mC_s, stack(st_s_even, 1), stack(st_s_even, 2), stack(st_s_even, 3),
            stack(st_s_even, 4), ssm_s.reshape(ssm_shape(bs)), jnp.stack(st_s_odd))
```

```python
import math
import functools
import jax
import jax.numpy as jnp
from jax import lax
from jax.experimental import pallas as pl
from jax.experimental.pallas import tpu as pltpu

D_MODEL = 1024
DEPTH = 4
F32 = jnp.float32
BF16 = jnp.bfloat16
EPS = 1e-6
N_EVEN = (DEPTH + 1) // 2
N_ODD = DEPTH // 2
D_FF = 4 * D_MODEL
PLE_DIM = 256

A_HEADS = 4
A_DK = D_MODEL // 8
A_DV = D_MODEL // 8
A_QK = A_HEADS * A_DK
A_V = A_HEADS * A_DV
A_COLS = 2 * A_QK + 2 * A_V + 2 * A_HEADS

B_HEADS = 8
B_DH = 64
B_W = B_HEADS * B_DH
B_W_RANK = 64
B_A_RANK = 64
B_G_RANK = 128
B_COLS = 3 * B_W + B_W_RANK + B_A_RANK + B_G_RANK
B_DECAY_OFFSET = 0.5
B_GN_EPS = 64e-5

EVEN_COLS = A_COLS + B_COLS
EVEN_OUT = A_V + B_W

C_INNER = 2 * D_MODEL
C_HEADDIM = 64
C_HEADS = C_INNER // C_HEADDIM
C_GROUPS = 4
C_HPG = C_HEADS // C_GROUPS
C_STATE = 128
C_CONV = 4
C_CHUNK = 128
C_CONV_DIM = C_INNER + 2 * C_GROUPS * C_STATE
ODD_COLS = C_INNER + C_CONV_DIM + C_HEADS

LANES = 128
SUBLANES = 8
VMEM_LIMIT = 56 * 1024 * 1024
TOKEN_TILE = 512

EVEN_GATE_PAD = 256
EVEN_N = B_COLS + EVEN_GATE_PAD + 2 * A_QK + 2 * A_V
EVEN_G_OFF = B_COLS
EVEN_A_OFF = B_COLS + EVEN_GATE_PAD
ODD_DT_PAD = 256
ODD_N = C_INNER + C_CONV_DIM + ODD_DT_PAD
ODD_DT_OFF = C_INNER + C_CONV_DIM


def _cparams(sem):
    return pltpu.CompilerParams(dimension_semantics=sem, vmem_limit_bytes=VMEM_LIMIT)


def _rms(x, g):
    return x * lax.rsqrt(jnp.mean(x * x, axis=-1, keepdims=True) + EPS) * g


def _resident(shape):
    nd = len(shape)
    return pl.BlockSpec(shape, lambda *_: (0,) * nd, pipeline_mode=pl.Buffered(1))


def _rows(width):
    return pl.BlockSpec((TOKEN_TILE, width), lambda i: (i, 0))


def _fill_into(body, n_in, into, out_idx, into2=None, out_idx2=None):
    pairs = [(a, o) for a, o in ((into, out_idx), (into2, out_idx2)) if a is not None]
    if not pairs:
        return body, [], [], {}

    def skipping(*refs):
        return body(*refs[:n_in], *refs[n_in + len(pairs):])

    return (skipping, [pl.BlockSpec(memory_space=pl.ANY)] * len(pairs), [a for a, _ in pairs],
            {n_in + i: o for i, (_, o) in enumerate(pairs)})


def _norm_matmul_kernel(x_ref, g_ref, w_ref, o_ref, *, tn):
    xn = _rms(x_ref[...], g_ref[...]).astype(BF16)
    for n0 in range(0, w_ref.shape[1], tn):
        o_ref[:, n0:n0 + tn] = jnp.dot(xn, w_ref[:, n0:n0 + tn], preferred_element_type=F32)


def norm_matmul(x, g, w, tn, row0=0, total=None, into=None):
    m, n = x.shape[0], w.shape[1]
    total = m if total is None else total
    t0 = row0 // TOKEN_TILE
    in_specs = [_rows(D_MODEL), _resident((1, D_MODEL)), _resident(w.shape)]
    body, xspecs, xargs, aliases = _fill_into(functools.partial(_norm_matmul_kernel, tn=tn), len(in_specs), into, 0)
    return pl.pallas_call(
        body,
        grid=(m // TOKEN_TILE,),
        in_specs=in_specs + xspecs,
        out_specs=pl.BlockSpec((TOKEN_TILE, n), lambda i: (t0 + i, 0)),
        out_shape=jax.ShapeDtypeStruct((total, n), F32),
        input_output_aliases=aliases,
        compiler_params=_cparams(("arbitrary",)),
        name="norm_matmul",
    )(x, g.reshape(1, D_MODEL), w, *xargs)


FFN_CHUNK = 512


def _matmul_res_kernel(*refs, kinds):
    x_ref, o_ref = refs[-2], refs[-1]
    acc = x_ref[...]
    pos = 0
    for has_gate, channel_major in kinds:
        a = refs[pos][...]
        if channel_major:
            a = a.T
        if has_gate:
            a = a * refs[pos + 1][...]
        w_ref = refs[pos + 1 + has_gate]
        pos += 2 + has_gate
        acc = acc + jnp.dot(a.astype(BF16), w_ref[...], preferred_element_type=F32)
    o_ref[...] = acc


def matmul_res(terms, x, row0=0, nrows=None, into=None, total=None):
    m = x.shape[0] if total is None else total
    nrows = m if nrows is None else nrows
    t0 = row0 // TOKEN_TILE
    rows = lambda width: pl.BlockSpec((TOKEN_TILE, width), lambda i: (t0 + i, 0))
    xspec = rows(D_MODEL) if total is None else _rows(D_MODEL)
    specs, args, kinds = [], [], []
    for a, gate, w, seq_len in terms:
        if seq_len is None:
            specs.append(rows(a.shape[1]))
        elif seq_len == 0:
            specs.append(pl.BlockSpec((TOKEN_TILE, a.shape[1]), lambda i: (i, 0)))
        else:
            assert seq_len % TOKEN_TILE == 0
            per_seq = seq_len // TOKEN_TILE
            specs.append(pl.BlockSpec((None, a.shape[1], TOKEN_TILE), lambda i: (i // per_seq, 0, i % per_seq)))
        specs += ([rows(w.shape[0])] if gate is not None else []) + [_resident(w.shape)]
        args += [a] + ([gate] if gate is not None else []) + [w]
        kinds.append((int(gate is not None), bool(seq_len)))
    specs.append(xspec)
    body, xspecs, xargs, aliases = _fill_into(
        functools.partial(_matmul_res_kernel, kinds=tuple(kinds)), len(specs), into, 0)
    return pl.pallas_call(
        body,
        grid=(nrows // TOKEN_TILE,),
        in_specs=specs + xspecs,
        out_specs=rows(D_MODEL),
        out_shape=jax.ShapeDtypeStruct((m, D_MODEL), F32),
        input_output_aliases=aliases,
        compiler_params=_cparams(("arbitrary",)),
        name="matmul_res",
    )(*args, x, *xargs)


def _ffn_ple_kernel(x_ref, gf_ref, wu_ref, wd_ref, gp_ref, wg_ref, p_ref, wp_ref, o_ref, y_s):
    x = x_ref[...]
    xn = _rms(x, gf_ref[...]).astype(BF16)
    y_s[...] = x
    for c0 in range(0, D_FF, FFN_CHUNK):
        h = jnp.dot(xn, wu_ref[:, c0:c0 + FFN_CHUNK], preferred_element_type=F32)
        h = jnp.square(jnp.maximum(h, 0.0)).astype(BF16)
        y_s[...] += jnp.dot(h, wd_ref[c0:c0 + FFN_CHUNK, :], preferred_element_type=F32)
    y = y_s[...]
    yn = _rms(y, gp_ref[...]).astype(BF16)
    gate = jax.nn.sigmoid(jnp.dot(yn, wg_ref[...], preferred_element_type=F32))
    proj = jnp.dot(p_ref[...].astype(BF16), wp_ref[...], preferred_element_type=F32)
    o_ref[...] = y + proj * gate


def ffn_ple(x, g_ffn, wu, wd, g_ple, wg, p, wp, row0, into=None):
    t0 = row0 // TOKEN_TILE
    rows = pl.BlockSpec((TOKEN_TILE, D_MODEL), lambda i: (t0 + i, 0))
    row1 = lambda v: v.reshape(1, D_MODEL)
    in_specs = [rows, _resident((1, D_MODEL)), _resident(wu.shape), _resident(wd.shape),
                _resident((1, D_MODEL)), _resident(wg.shape), _rows(PLE_DIM), _resident(wp.shape)]
    body, xspecs, xargs, aliases = _fill_into(_ffn_ple_kernel, len(in_specs), into, 0)
    return pl.pallas_call(
        body,
        grid=(p.shape[0] // TOKEN_TILE,),
        in_specs=in_specs + xspecs,
        out_specs=rows,
        out_shape=jax.ShapeDtypeStruct(x.shape, F32),
        input_output_aliases=aliases,
        scratch_shapes=[pltpu.VMEM((TOKEN_TILE, D_MODEL), F32)],
        compiler_params=_cparams(("arbitrary",)),
        name="ffn_ple",
    )(x, row1(g_ffn), wu, wd, row1(g_ple), wg, p, wp, *xargs)


def _final_norm_kernel(x_ref, g_ref, o_ref):
    o_ref[...] = _rms(x_ref[...], g_ref[...])


def final_norm(x, g, row0, nrows):
    t0 = row0 // TOKEN_TILE
    return pl.pallas_call(
        _final_norm_kernel,
        grid=(nrows // TOKEN_TILE,),
        in_specs=[pl.BlockSpec((TOKEN_TILE, D_MODEL), lambda i: (t0 + i, 0)), _resident((1, D_MODEL))],
        out_specs=_rows(D_MODEL),
        out_shape=jax.ShapeDtypeStruct((nrows, D_MODEL), F32),
        compiler_params=_cparams(("arbitrary",)),
        name="final_norm",
    )(x, g.reshape(1, D_MODEL))


def _rwkv_operands(k, a, r, kk_t, ka_t, rk_t):
    kk = k * kk_t
    kk = kk * lax.rsqrt(jnp.maximum(jnp.sum(kk * kk, axis=1, keepdims=True), 1e-24))
    k2 = k * (1.0 + (a - 1.0) * ka_t)
    return -kk, kk * a, k2, jnp.sum(r * k2 * rk_t, axis=1)


def _rwkv_step(s_ref, tile0, ntiles, r, w, k, a, b, vrows):
    ys = []
    for lt in range(ntiles):
        rows = slice((tile0 + lt) * B_DH, (tile0 + lt + 1) * B_DH)
        s = s_ref[rows, :]
        sa = jnp.sum(s * a, axis=0, keepdims=True)
        s = s * w + sa * b + vrows[lt:lt + 1] * k
        s_ref[rows, :] = s
        ys.append(jnp.sum(s * r, axis=0, keepdims=True))
    return jnp.concatenate(ys, axis=0)


LONG_T = 128
LONG_NLT = B_DH // 2
KEY_PITCH = B_DH + SUBLANES
VAL_PITCH = LONG_NLT + SUBLANES


def _rwkv_scan_long_kernel(r_ref, w_ref, k_ref, a_ref, v_ref, kkt_ref, kat_ref, rkt_ref, lnw_ref, lnb_ref, s0_ref,
                           yt_ref, s_ref, or_s, ow_s, ok_s, oa_s, av_s, v_s, y_s):
    nb = yt_ref.shape[0]

    @pl.when(pl.program_id(0) == 0)
    def _():
        s_ref[...] = s0_ref[...]
        for scr in (or_s, ow_s, ok_s, oa_s, av_s, v_s, y_s):
            scr[...] = jnp.zeros(scr.shape, F32)

    lo = lax.broadcasted_iota(jnp.int32, (LONG_T, LANES), 1) < LANES // 2

    def tile(ref, c):
        m = ref[c]
        return jnp.concatenate([m, m], axis=0).T

    def relayout(c, carry):
        for ref, dst in ((r_ref, or_s), (w_ref, ow_s), (k_ref, ok_s), (a_ref, oa_s)):
            dst[pl.ds(c, LONG_T, stride=KEY_PITCH), :] = tile(ref, c)
            dst[pl.ds(c + LONG_NLT, LONG_T, stride=KEY_PITCH), :] = tile(ref, c + LONG_NLT)
        v_s[pl.ds(c, LONG_T, stride=VAL_PITCH), :] = jnp.where(lo, tile(v_ref, c), tile(v_ref, c + LONG_NLT))
        return carry

    lax.fori_loop(0, LONG_NLT, relayout, 0)

    def unpad(scr, pitch, n):
        return scr[...].reshape(LONG_T, pitch, LANES)[:, :n]

    def pad(x, pitch):
        zeros = jnp.zeros((LONG_T, pitch - x.shape[1], LANES), F32)
        return jnp.concatenate([x, zeros], axis=1).reshape(LONG_T * pitch, LANES)

    av, bv, k2, bonus = _rwkv_operands(unpad(ok_s, KEY_PITCH, B_DH), unpad(oa_s, KEY_PITCH, B_DH),
                                       unpad(or_s, KEY_PITCH, B_DH), kkt_ref[...], kat_ref[...], rkt_ref[...])
    av_s[...] = pad(av, KEY_PITCH)
    oa_s[...] = pad(bv, KEY_PITCH)
    ok_s[...] = pad(k2, KEY_PITCH)

    def step(t, carry):
        kr = pl.ds(pl.multiple_of(t * KEY_PITCH, SUBLANES), B_DH)
        vr = pl.ds(pl.multiple_of(t * VAL_PITCH, SUBLANES), LONG_NLT)
        y_s[vr, :] = _rwkv_step(s_ref, 0, LONG_NLT, or_s[kr, :], ow_s[kr, :], ok_s[kr, :], av_s[kr, :],
                                oa_s[kr, :], v_s[vr, :])
        return carry

    lax.fori_loop(0, LONG_T, step, 0)

    def head_sum(x):
        tot = jnp.sum(x, axis=1)
        return tot + pltpu.roll(tot, LANES // 2, axis=1)

    y = unpad(y_s, VAL_PITCH, LONG_NLT)
    d = y - (head_sum(y) * (1.0 / B_DH))[:, None, :]
    var = head_sum(d * d) * (1.0 / B_DH)
    y = (d * lax.rsqrt(var + B_GN_EPS)[:, None, :] * lnw_ref[...] + lnb_ref[...]
         + bonus[:, None, :] * unpad(v_s, VAL_PITCH, LONG_NLT))
    y_s[...] = pad(y, VAL_PITCH)

    def relayout_out(lt, carry):
        yt = y_s[pl.ds(lt, LONG_T, stride=VAL_PITCH), :].T
        for i2 in range(2):
            for b in range(nb):
                row = i2 * (LANES // 2) + b * B_HEADS
                yt_ref[b, pl.ds(lt + LONG_NLT * i2, B_HEADS, stride=B_DH), :] = yt[row:row + B_HEADS, :]
        return carry

    lax.fori_loop(0, LONG_NLT, relayout_out, 0)


def rwkv_scan_long(rc, wc, kc, ac, vc, S0, k_k, k_a, r_k, ln_w, ln_b):
    _, bh, L = rc.shape
    b = bh // B_HEADS
    assert bh * 2 == LANES and L % LONG_T == 0
    per_key = lambda p: jnp.tile(p.reshape(B_HEADS, B_DH).T, (1, LANES // B_HEADS))
    per_val = lambda p: jnp.concatenate(
        [jnp.tile(p.reshape(B_HEADS, 2, LONG_NLT)[:, i2].T, (1, b)) for i2 in range(2)], axis=1)
    s0 = S0.reshape(b, B_HEADS, 2, LONG_NLT, B_DH).transpose(3, 4, 2, 0, 1).reshape(LONG_NLT * B_DH, LANES)
    blk = pl.BlockSpec((B_DH, bh, LONG_T), lambda c: (0, 0, c))
    blk1 = pl.BlockSpec((B_DH, bh, LONG_T), lambda c: (0, 0, c), pipeline_mode=pl.Buffered(1))
    oblk = pl.BlockSpec((b, B_W, LONG_T), lambda c: (0, 0, c))
    const = lambda shape: pl.BlockSpec(shape, lambda c: (0,) * len(shape))
    big = pltpu.VMEM((LONG_T * KEY_PITCH, LANES), F32)
    small = pltpu.VMEM((LONG_T * VAL_PITCH, LANES), F32)
    yt, s = pl.pallas_call(
        _rwkv_scan_long_kernel,
        grid=(L // LONG_T,),
        in_specs=[blk] * 3 + [blk1] * 2 + [const((B_DH, LANES))] * 3 + [const((LONG_NLT, LANES))] * 2
        + [const((LONG_NLT * B_DH, LANES))],
        out_specs=[oblk, const((LONG_NLT * B_DH, LANES))],
        out_shape=[jax.ShapeDtypeStruct((b, B_W, L), F32), jax.ShapeDtypeStruct(s0.shape, F32)],
        scratch_shapes=[big] * 5 + [small, small],
        compiler_params=_cparams(("arbitrary",)),
        name="rwkv_scan_long",
    )(rc, wc, kc, ac, vc, per_key(k_k), per_key(k_a), per_key(r_k), per_val(ln_w), per_val(ln_b), s0)
    s = s.reshape(LONG_NLT, B_DH, 2, b, B_HEADS).transpose(3, 4, 2, 0, 1).reshape(b, B_HEADS, B_DH, B_DH)
    return yt, s


def _rwkv_scan_short_kernel(r_ref, w_ref, k_ref, a_ref, v_ref, kkt_ref, kat_ref, rkt_ref, lnw_ref, lnb_ref, s0_ref,
                            y_ref, sout_ref, s_s, or_s, ow_s, ok_s, oa_s, av_s, ov_s, y_s, *, L):
    nt = 2 * B_DH
    for q in range(nt * B_DH // LANES):
        s_s[q * LANES:(q + 1) * LANES, :] = s0_ref[:, q * LANES:(q + 1) * LANES].T
    for ref, dst in ((r_ref, or_s), (w_ref, ow_s), (k_ref, ok_s), (a_ref, oa_s), (v_ref, ov_s)):
        for t in range(L):
            dst[t] = ref[pl.ds(t, LANES, stride=L), :].T.reshape(2, B_DH, LANES)
    shape3 = (L * 2, B_DH, LANES)
    tiles = lambda ref: jnp.concatenate([ref[...]] * L, axis=0)
    av, bv, k2, bonus = _rwkv_operands(ok_s[...].reshape(shape3), oa_s[...].reshape(shape3),
                                       or_s[...].reshape(shape3), tiles(kkt_ref), tiles(kat_ref), tiles(rkt_ref))
    av_s[...] = av.reshape(L, 2, B_DH, LANES)
    oa_s[...] = bv.reshape(L, 2, B_DH, LANES)
    ok_s[...] = k2.reshape(L, 2, B_DH, LANES)

    def step(t, carry):
        for h in range(2):
            y_s[t, h] = _rwkv_step(s_s, h * B_DH, B_DH, or_s[t, h], ow_s[t, h], ok_s[t, h], av_s[t, h],
                                   oa_s[t, h], ov_s[t, h])
        return carry

    lax.fori_loop(0, L, step, 0)

    y = y_s[...]
    d = y - jnp.mean(y, axis=2, keepdims=True)
    var = jnp.mean(d * d, axis=2, keepdims=True)
    y = (d * lax.rsqrt(var + B_GN_EPS) * lnw_ref[...] + lnb_ref[...]
         + bonus.reshape(L, 2, 1, LANES) * ov_s[...])
    for t in range(L):
        y_ref[pl.ds(t, LANES, stride=L), :] = y[t].reshape(nt, LANES).T
    for q in range(nt * B_DH // LANES):
        sout_ref[:, q * LANES:(q + 1) * LANES] = s_s[q * LANES:(q + 1) * LANES, :].T


def rwkv_scan_short(r, w, k, a, v, S0, L, k_k, k_a, r_k, ln_w, ln_b):
    n = r.shape[0]
    b = n // L
    assert b == LANES
    npair = B_HEADS // 2
    wide = lambda p: jnp.broadcast_to(p.reshape(npair, 2, B_DH, 1), (npair, 2, B_DH, LANES))
    s0 = S0.reshape(b, B_HEADS * B_DH * B_DH)
    blk = pl.BlockSpec((n, LANES), lambda p: (0, p))
    cblk = pl.BlockSpec((None, 2, B_DH, LANES), lambda p: (p, 0, 0, 0))
    sblk = pl.BlockSpec((b, 2 * B_DH * B_DH), lambda p: (0, p))
    op = pltpu.VMEM((L, 2, B_DH, LANES), F32)
    y, s = pl.pallas_call(
        functools.partial(_rwkv_scan_short_kernel, L=L),
        grid=(npair,),
        in_specs=[blk] * 5 + [cblk] * 5 + [sblk],
        out_specs=[blk, sblk],
        out_shape=[jax.ShapeDtypeStruct((n, B_W), F32), jax.ShapeDtypeStruct(s0.shape, F32)],
        scratch_shapes=[pltpu.VMEM((2 * B_DH * B_DH, LANES), F32)] + [op] * 7,
        compiler_params=_cparams(("arbitrary",)),
        name="rwkv_scan_short",
    )(r, w, k, a, v, wide(k_k), wide(k_a), wide(r_k), wide(ln_w), wide(ln_b), s0)
    return y, s.reshape(b, B_HEADS, B_DH, B_DH)


RWKV_PREP_ROWS = 256
RWKV_SHORT_ROWS = 64
B_LORA_OFF = 3 * B_W


def _rwkv_prep_kernel(z_ref, sh0_ref, mu_ref, wwa_ref, g2_ref, w0_ref, a0_ref,
                      r_ref, w_ref, k_ref, a_ref, v_ref, g_ref, sh_ref, *scratch, T, nseq, channel_major):
    @pl.when(pl.program_id(1) == 0)
    def _():
        sh_ref[...] = sh0_ref[...]

    z = z_ref[...]
    rowid = lax.broadcasted_iota(jnp.int32, (z.shape[0], 1), 0)
    zprev = pltpu.roll(z, 1, axis=0)
    for u in range(nseq):
        zprev = jnp.where(rowid == u * T, sh_ref[u], zprev)
    for u in range(nseq):
        sh_ref[u] = z[(u + 1) * T - 1:(u + 1) * T, :]
    zs = z + (zprev - z) * mu_ref[...]
    r = zs[:, :B_W]
    k = zs[:, B_W:2 * B_W]
    lora = zs[:, B_LORA_OFF:B_LORA_OFF + LANES]
    lane = lax.broadcasted_iota(jnp.int32, lora.shape, 1)
    lora = jnp.where(lane < B_W_RANK, jnp.tanh(lora), lora).astype(BF16)
    wa = jnp.dot(lora, wwa_ref[...], preferred_element_type=F32)
    w_log = -jax.nn.softplus(-(w0_ref[...] + wa[:, :B_W])) - B_DECAY_OFFSET
    a = jax.nn.sigmoid(a0_ref[...] + wa[:, B_W:])
    zg = zs[:, B_LORA_OFF + LANES:B_LORA_OFF + LANES + B_G_RANK]
    def emit(ref, x):
        if not channel_major:
            ref[...] = x
            return
        xt_s = scratch[0]
        xt = x.T
        for j in range(xt_s.shape[0]):
            for h in range(B_HEADS):
                xt_s[j, h * KEY_PITCH:h * KEY_PITCH + B_DH, :] = xt[h * B_DH:(h + 1) * B_DH, j * LANES:(j + 1) * LANES]
        for c in range(B_DH):
            for j in range(xt_s.shape[0]):
                ref[c, :, j * LANES:(j + 1) * LANES] = xt_s[j, pl.ds(c, B_HEADS, stride=KEY_PITCH), :]

    emit(r_ref, r)
    emit(w_ref, jnp.exp(-jnp.exp(w_log)))
    emit(k_ref, k)
    emit(a_ref, a)
    emit(v_ref, zs[:, 2 * B_W:3 * B_W])
    g_ref[...] = jnp.dot(jax.nn.sigmoid(zg).astype(BF16), g2_ref[...], preferred_element_type=F32)


def rwkv_prep(zin, row0, b, L, gate_into, shift0, mu, w0, w2, a0, a2, g2):
    assert B_W_RANK + B_A_RANK == LANES
    T = math.gcd(L, RWKV_PREP_ROWS)
    channel_major = T == RWKV_PREP_ROWS
    nseq = 1 if channel_major else RWKV_SHORT_ROWS // T
    assert nseq == 1 or (L == T and b % nseq == 0)
    R = T * nseq
    nchunk = L // T
    blk0 = row0 // R
    wwa = jnp.zeros((LANES, 2 * B_W), F32).at[:B_W_RANK, :B_W].set(w2).at[B_W_RANK:, B_W:].set(a2).astype(BF16)
    row = lambda v: v.reshape(1, -1)
    if channel_major:
        blk = pl.BlockSpec((B_DH, B_HEADS, R), lambda i, c: (0, i, c))
        oshape = jax.ShapeDtypeStruct((B_DH, b * B_HEADS, L), F32)
    else:
        blk = pl.BlockSpec((R, B_W), lambda i, c: (i * nchunk + c, 0))
        oshape = jax.ShapeDtypeStruct((b * L, B_W), F32)
    gblk = pl.BlockSpec((R, B_W), lambda i, c: (blk0 + i * nchunk + c, 0))
    shspec = pl.BlockSpec((nseq, 1, B_COLS), lambda i, c: (i, 0, 0))
    const = lambda shape: pl.BlockSpec(shape, lambda i, c: (0,) * len(shape))
    in_specs = [pl.BlockSpec((R, B_COLS), lambda i, c: (blk0 + i * nchunk + c, 0)), shspec, const((1, B_COLS)),
                const((LANES, 2 * B_W)), const((B_G_RANK, B_W)), const((1, B_W)), const((1, B_W))]
    body, xspecs, xargs, aliases = _fill_into(
        functools.partial(_rwkv_prep_kernel, T=T, nseq=nseq, channel_major=channel_major),
        len(in_specs), gate_into, 5)
    outs = pl.pallas_call(
        body,
        grid=(b // nseq, nchunk),
        in_specs=in_specs + xspecs,
        out_specs=[blk] * 5 + [gblk, shspec],
        out_shape=[oshape] * 5
        + [jax.ShapeDtypeStruct((zin.shape[0], B_W), F32), jax.ShapeDtypeStruct((b, 1, B_COLS), F32)],
        input_output_aliases=aliases,
        scratch_shapes=[pltpu.VMEM((R // LANES, B_HEADS * KEY_PITCH, LANES), F32)] if channel_major else [],
        compiler_params=_cparams(("arbitrary", "arbitrary")),
        name="rwkv_prep",
    )(zin, shift0.reshape(b, 1, B_COLS), row(mu), wwa, g2.astype(BF16), row(w0), row(a0), *xargs)
    return outs[:5], outs[5], outs[6].reshape(b, B_COLS)


MLSTM_ROWS = 128
HIGHEST = lax.Precision.HIGHEST


def _mlstm_kernel(q_ref, k_ref, v_ref, o_ref, g_ref, gt_ref, brow_ref, bcol_ref, nw_ref,
                  c0_ref, n0_ref, m0_ref, h_ref, c_ref, n_ref, m_ref, *, T, nseq):
    R = MLSTM_ROWS

    @pl.when(pl.program_id(1) == 0)
    def _():
        c_ref[...] = c0_ref[...]
        n_ref[...] = n0_ref[...]
        m_ref[...] = m0_ref[...]

    shift = T.bit_length() - 1
    ri = lax.broadcasted_iota(jnp.int32, (R, R), 0)
    ci = lax.broadcasted_iota(jnp.int32, (R, R), 1)
    mask = (ci <= ri) & (jnp.right_shift(ri, shift) == jnp.right_shift(ci, shift))
    lmat = mask.astype(F32)
    rowid = lax.broadcasted_iota(jnp.int32, (R, 1), 0)
    rsel = [(rowid >= u * T) & (rowid < (u + 1) * T) for u in range(nseq)]

    g = g_ref[...] + brow_ref[...]
    lane = lax.broadcasted_iota(jnp.int32, g.shape, 1)
    glog = jnp.where((lane >= A_HEADS) & (lane < 2 * A_HEADS), jax.nn.log_sigmoid(g), g)
    gt = gt_ref[...] + bcol_ref[...]
    sub = lax.broadcasted_iota(jnp.int32, gt.shape, 0)
    gtlog = jnp.where(sub >= A_HEADS, jax.nn.log_sigmoid(gt), gt)
    bc_col = jnp.dot(lmat, glog, precision=HIGHEST, preferred_element_type=F32)
    bc_row = lax.dot_general(gtlog, lmat, (((1,), (1,)), ((), ())), precision=HIGHEST,
                             preferred_element_type=F32)
    lane_m = lax.broadcasted_iota(jnp.int32, (1, LANES), 1)
    m_old = [m_ref[u] for u in range(nseq)]
    m_out = [jnp.zeros((1, LANES), F32) for _ in range(nseq)]

    for h in range(A_HEADS):
        hs = slice(h * A_DK, (h + 1) * A_DK)
        bcc = bc_col[:, A_HEADS + h:A_HEADS + h + 1]
        bcr = bc_row[A_HEADS + h:A_HEADS + h + 1, :]
        lir = gtlog[h:h + 1, :]
        lic = glog[:, h:h + 1]
        m_u = [m_old[u][:, h:h + 1] for u in range(nseq)]
        m_col = m_u[0]
        for u in range(1, nseq):
            m_col = jnp.where(rsel[u], m_u[u], m_col)
        dmat = jnp.where(mask, bcc - bcr + lir, -jnp.inf)
        inter = bcc + m_col
        mt = jnp.maximum(inter, jnp.max(dmat, axis=1, keepdims=True))
        p = jnp.exp(dmat - mt)
        qh = q_ref[:, hs] * (A_DK ** -0.5)
        kh = k_ref[:, hs]
        qb, kb, vb = qh.astype(BF16), kh.astype(BF16), v_ref[:, hs].astype(BF16)
        wq = lax.dot_general(qb, kb, (((1,), (1,)), ((), ())), preferred_element_type=F32) * p
        wi = jnp.exp(inter - mt)
        c_old = [c_ref[u, h] for u in range(nseq)]
        n_old = [n_ref[u, h:h + 1, :] for u in range(nseq)]
        qc = jnp.dot(qb, c_old[0].astype(BF16), preferred_element_type=F32)
        qn = jnp.sum(qh * n_old[0], axis=1, keepdims=True)
        for u in range(1, nseq):
            qc = jnp.where(rsel[u], jnp.dot(qb, c_old[u].astype(BF16), preferred_element_type=F32), qc)
            qn = jnp.where(rsel[u], jnp.sum(qh * n_old[u], axis=1, keepdims=True), qn)
        num = jnp.dot(wq.astype(BF16), vb, preferred_element_type=F32) + wi * qc
        den = jnp.sum(wq, axis=1, keepdims=True) + wi * qn
        hh = num / jnp.maximum(jnp.abs(den), jnp.exp(-mt))
        hh = hh * lax.rsqrt(jnp.mean(hh * hh, axis=-1, keepdims=True) + EPS)
        h_ref[:, hs] = hh * nw_ref[:, hs] * jax.nn.sigmoid(o_ref[:, hs])
        for u in range(nseq):
            b_last = bcc[(u + 1) * T - 1:(u + 1) * T, :]
            gs = b_last - bcc + lic
            gmax = jnp.max(gs if nseq == 1 else jnp.where(rsel[u], gs, -jnp.inf), axis=0, keepdims=True)
            m_new = jnp.maximum(b_last + m_u[u], gmax)
            decay = jnp.exp(b_last + m_u[u] - m_new)
            ws = jnp.exp(gs - m_new)
            if nseq > 1:
                ws = jnp.where(rsel[u], ws, 0.0)
            kw = kh * ws
            c_ref[u, h] = decay * c_old[u] + lax.dot_general(
                kw.astype(BF16), vb, (((0,), (0,)), ((), ())), preferred_element_type=F32)
            n_ref[u, h:h + 1, :] = decay * n_old[u] + jnp.sum(kw, axis=0, keepdims=True)
            m_out[u] = jnp.where(lane_m == h, m_new, m_out[u])
    for u in range(nseq):
        m_ref[u] = m_out[u]


def mlstm(zin, gates_t, row0, b, L, h_into, layer, c_into, b_i, b_f, m_norm, C0, n0, m0):
    R = MLSTM_ROWS
    T = math.gcd(L, R)
    nseq = R // T
    assert nseq == 1 or (L == T and b % nseq == 0)
    nchunk = L // T
    blk0 = row0 // R
    bias = jnp.concatenate([b_i, b_f])
    bias_row = jnp.zeros((1, LANES), F32).at[0, :2 * A_HEADS].set(bias)
    bias_col = bias.reshape(2 * A_HEADS, 1)
    m0p = jnp.zeros((b, 1, LANES), F32).at[:, 0, :A_HEADS].set(m0)
    blk = lambda i, c: blk0 + i * nchunk + c
    rowblk = lambda col: pl.BlockSpec((R, A_QK), lambda i, c: (blk(i, c), EVEN_A_OFF // A_QK + col))
    cspec = pl.BlockSpec((nseq, A_HEADS, A_DK, A_DV), lambda i, c: (i, 0, 0, 0))
    cout = pl.BlockSpec((None, nseq, A_HEADS, A_DK, A_DV), lambda i, c: (layer, i, 0, 0, 0))
    nspec = pl.BlockSpec((nseq, A_HEADS, A_DK), lambda i, c: (i, 0, 0))
    mspec = pl.BlockSpec((nseq, 1, LANES), lambda i, c: (i, 0, 0))
    const = lambda shape: pl.BlockSpec(shape, lambda i, c: (0,) * len(shape))
    in_specs = [rowblk(0), rowblk(1), rowblk(2), rowblk(3),
                pl.BlockSpec((R, LANES), lambda i, c: (blk(i, c), EVEN_G_OFF // LANES)),
                pl.BlockSpec((None, 2 * A_HEADS, R), lambda i, c: (blk(i, c), 0, 0)),
                const((1, LANES)), const((2 * A_HEADS, 1)), const((1, A_V)),
                cspec if C0.ndim == 4 else cout, nspec, mspec]
    body, xspecs, xargs, aliases = _fill_into(
        functools.partial(_mlstm_kernel, T=T, nseq=nseq), len(in_specs), h_into, 0, c_into, 1)
    h, C, n, m = pl.pallas_call(
        body,
        grid=(b // nseq, nchunk),
        in_specs=in_specs + xspecs,
        out_specs=[pl.BlockSpec((R, A_V), lambda i, c: (blk(i, c), 0)), cout, nspec, mspec],
        out_shape=[jax.ShapeDtypeStruct((zin.shape[0], A_V), F32),
                   jax.ShapeDtypeStruct((N_EVEN,) + C0.shape[-4:], F32), jax.ShapeDtypeStruct(n0.shape, F32),
                   jax.ShapeDtypeStruct(m0p.shape, F32)],
        input_output_aliases=aliases,
        compiler_params=_cparams(("arbitrary", "arbitrary")),
        name="mlstm",
    )(zin, zin, zin, zin, zin, gates_t, bias_row, bias_col, m_norm.reshape(1, A_V), C0, n0, m0p, *xargs)
    return h, C, n, m[:, 0, :A_HEADS]


def even_mixer(zin, gates_t, row0, b, L, into, layer, c_into, C0, n0, m0, S0, shift0, b_i, b_f, m_norm,
               mu, w0, w2, a0, a2, g2, k_k, k_a, r_k, ln_w, ln_b):
    hA, C, n, m = mlstm(zin, gates_t, row0, b, L, into[0], layer, c_into, b_i, b_f, m_norm, C0, n0, m0)
    ops, g, shift = rwkv_prep(zin, row0, b, L, into[1], shift0, mu, w0, w2, a0, a2, g2)
    if ops[0].ndim == 3:
        hB, S = rwkv_scan_long(*ops, S0, k_k, k_a, r_k, ln_w, ln_b)
    else:
        hB, S = rwkv_scan_short(*ops, S0, L, k_k, k_a, r_k, ln_w, ln_b)
    return hA, hB, g, (C, n, m, S, shift)


C_PAIRS = C_HEADS // 2
C_GROUP_W = C_INNER // C_GROUPS
C_BC_W = 2 * C_GROUPS * C_STATE
SSD_LONG_ROWS = C_CHUNK
SSD_SHORT_ROWS = 32


def _ssd_kernel(z_ref, x_ref, bc_ref, dt_ref, dtt_ref, cw_ref, cb_ref, dtb_ref, dtbt_ref, al_ref, alt_ref,
                dsk_ref, nw_ref, s0_ref, cv0_ref, y_ref, s_ref, cv_ref, stg_ref, *, T, nseq, nchunk):
    R = T * nseq
    tstate = nchunk > 1
    assert not tstate or nseq == 1

    @pl.when(pl.program_id(1) == 0)
    def _():
        if tstate:
            for pr in range(C_PAIRS):
                s_ref[0, pr] = s0_ref[0, pr].T
        else:
            s_ref[...] = s0_ref[...]
        cv_ref[...] = cv0_ref[...]

    shift = T.bit_length() - 1
    ri = lax.broadcasted_iota(jnp.int32, (R, R), 0)
    ci = lax.broadcasted_iota(jnp.int32, (R, R), 1)
    mask = (ci <= ri) & (jnp.right_shift(ri, shift) == jnp.right_shift(ci, shift))
    lmat = mask.astype(F32)
    rowid = lax.broadcasted_iota(jnp.int32, (R, 1), 0)
    rsel = [(rowid >= u * T) & (rowid < (u + 1) * T) for u in range(nseq)]

    def conv_silu(src_ref, src_col, col):
        cols = slice(col, col + LANES)
        accs = []
        for u in range(nseq):
            stg_ref[u, 0:SUBLANES, cols] = cv_ref[u, :, cols]
            stg_ref[u, SUBLANES:SUBLANES + T, cols] = src_ref[u * T:(u + 1) * T, src_col:src_col + LANES]
            acc = cb_ref[:, cols]
            for d in range(C_CONV):
                acc = acc + stg_ref[u, SUBLANES - d:SUBLANES - d + T, cols] * cw_ref[C_CONV - 1 - d:C_CONV - d, cols]
            accs.append(acc)
            cv_ref[u, :, cols] = stg_ref[u, T:T + SUBLANES, cols]
        acc = accs[0] if nseq == 1 else jnp.concatenate(accs, axis=0)
        return acc * jax.nn.sigmoid(acc)

    dtv = jax.nn.softplus(dt_ref[...] + dtb_ref[...])
    dtt = jax.nn.softplus(dtt_ref[...] + dtbt_ref[...])
    cum_col = jnp.dot(lmat, dtv * (-jnp.exp(al_ref[...])), precision=HIGHEST, preferred_element_type=F32)
    cum_row = lax.dot_general(dtt * (-jnp.exp(alt_ref[...])), lmat, (((1,), (1,)), ((), ())),
                              precision=HIGHEST, preferred_element_type=F32)

    lo = lax.broadcasted_iota(jnp.int32, (R, LANES), 1) < C_HEADDIM
    rlo = lax.broadcasted_iota(jnp.int32, (LANES, 1), 0) < C_HEADDIM
    nt = (((1,), (1,)), ((), ()))
    tn = (((0,), (0,)), ((), ()))
    pairs_per_group = C_PAIRS // C_GROUPS
    for g in range(C_GROUPS):
        bgf = conv_silu(bc_ref, g * C_STATE, C_INNER + g * C_STATE)
        bg = bgf.astype(BF16)
        cg = conv_silu(bc_ref, (C_GROUPS + g) * C_STATE, C_INNER + (C_GROUPS + g) * C_STATE).astype(BF16)
        cbm = lax.dot_general(cg, bg, nt, preferred_element_type=F32)
        ys = None
        if tstate:
            bgt = bgf.T.astype(BF16)
            sgt = jnp.concatenate([s_ref[0, g * pairs_per_group + q] for q in range(pairs_per_group)], axis=1)
            ys = jnp.dot(cg, sgt.astype(BF16), preferred_element_type=F32)
        for u in range(0 if tstate else nseq):
            sg = s_ref[u, g * pairs_per_group:(g + 1) * pairs_per_group].reshape(C_GROUP_W, C_STATE)
            t_u = lax.dot_general(cg, sg.astype(BF16), nt, preferred_element_type=F32)
            ys = t_u if u == 0 else jnp.where(rsel[u], t_u, ys)
        for q in range(pairs_per_group):
            pr = g * pairs_per_group + q
            ps = slice(pr * LANES, (pr + 1) * LANES)
            xp = conv_silu(x_ref, pr * LANES, pr * LANES)
            cc = [cum_col[:, 2 * pr + e:2 * pr + e + 1] for e in range(2)]
            intra = None
            for e, keep in ((0, lo), (1, jnp.logical_not(lo))):
                hh = 2 * pr + e
                seg = jnp.exp(jnp.where(mask, cc[e] - cum_row[hh:hh + 1, :], -jnp.inf))
                mix = cbm * seg * dtt[hh:hh + 1, :]
                part = jnp.dot(mix.astype(BF16), jnp.where(keep, xp, 0.0).astype(BF16),
                               preferred_element_type=F32)
                intra = part if intra is None else intra + part
            scale = jnp.where(lo, jnp.exp(cc[0]), jnp.exp(cc[1]))
            yp = intra + scale * ys[:, q * LANES:(q + 1) * LANES] + dsk_ref[:, ps] * xp
            zp = z_ref[:, ps]
            y_ref[:, ps] = yp * (zp * jax.nn.sigmoid(zp))
            for u in range(nseq):
                last = (u + 1) * T - 1
                ct = [cc[e][last:last + 1, :] for e in range(2)]
                tail = jnp.where(lo, jnp.exp(ct[0] - cc[0]) * dtv[:, 2 * pr:2 * pr + 1],
                                 jnp.exp(ct[1] - cc[1]) * dtv[:, 2 * pr + 1:2 * pr + 2])
                xw = xp * tail
                if nseq > 1:
                    xw = jnp.where(rsel[u], xw, 0.0)
                if tstate:
                    upd = jnp.dot(bgt, xw.astype(BF16), preferred_element_type=F32)
                    dec = jnp.where(lo[:1], jnp.exp(ct[0]), jnp.exp(ct[1]))
                else:
                    upd = lax.dot_general(xw.astype(BF16), bg, tn, preferred_element_type=F32)
                    dec = jnp.where(rlo, jnp.exp(ct[0]), jnp.exp(ct[1]))
                s_ref[u, pr] = dec * s_ref[u, pr] + upd

    if tstate:
        @pl.when(pl.program_id(1) == nchunk - 1)
        def _():
            for pr in range(C_PAIRS):
                s_ref[0, pr] = s_ref[0, pr].T

    for g in range(C_GROUPS):
        gs_ = slice(g * C_GROUP_W, (g + 1) * C_GROUP_W)
        yg = y_ref[:, gs_]
        y_ref[:, gs_] = yg * lax.rsqrt(jnp.mean(yg * yg, axis=-1, keepdims=True) + EPS) * nw_ref[:, gs_]


def ssd_mixer(zin, row0, b, L, y_into, layer, s_into, ssm0, conv0, conv_w, conv_b, dt_bias, a_log, d_skip, norm_w):
    T = math.gcd(L, SSD_LONG_ROWS)
    nseq = 1 if T == SSD_LONG_ROWS else SSD_SHORT_ROWS // T
    assert nseq == 1 or (L == T and b % nseq == 0)
    R = T * nseq
    nchunk = L // T
    nblk = b * L // R
    blk0 = row0 // R
    dt_t = zin[row0:row0 + b * L, ODD_DT_OFF:ODD_DT_OFF + C_HEADS].reshape(nblk, R, C_HEADS).transpose(0, 2, 1)
    pad_row = lambda v: jnp.zeros((1, LANES), F32).at[0, :C_HEADS].set(v)
    pair_shape = (b, C_PAIRS, 2 * C_HEADDIM, C_STATE)
    per_layer = ssm0.ndim == 4
    s0 = ssm0.reshape(pair_shape if per_layer else (ssm0.shape[0],) + pair_shape)
    cv0 = jnp.concatenate([jnp.zeros((b, SUBLANES - (C_CONV - 1), C_CONV_DIM), F32), conv0], axis=1)
    blk = lambda i, c: blk0 + i * nchunk + c
    sspec = pl.BlockSpec((nseq, C_PAIRS, 2 * C_HEADDIM, C_STATE), lambda i, c: (i, 0, 0, 0))
    sout = pl.BlockSpec((None, nseq, C_PAIRS, 2 * C_HEADDIM, C_STATE), lambda i, c: (layer, i, 0, 0, 0))
    cvspec = pl.BlockSpec((nseq, SUBLANES, C_CONV_DIM), lambda i, c: (i, 0, 0))
    const = lambda shape: pl.BlockSpec(shape, lambda i, c: (0,) * len(shape))
    in_specs = [pl.BlockSpec((R, C_INNER), lambda i, c: (blk(i, c), 0)),
                pl.BlockSpec((R, C_INNER), lambda i, c: (blk(i, c), 1)),
                pl.BlockSpec((R, C_BC_W), lambda i, c: (blk(i, c), 2 * C_INNER // C_BC_W)),
                pl.BlockSpec((R, LANES), lambda i, c: (blk(i, c), ODD_DT_OFF // LANES)),
                pl.BlockSpec((None, C_HEADS, R), lambda i, c: (i * nchunk + c, 0, 0)),
                const((C_CONV, C_CONV_DIM)), const((1, C_CONV_DIM)),
                const((1, LANES)), const((C_HEADS, 1)), const((1, LANES)), const((C_HEADS, 1)),
                const((1, C_INNER)), const((1, C_INNER)), sspec if per_layer else sout, cvspec]
    body, xspecs, xargs, aliases = _fill_into(
        functools.partial(_ssd_kernel, T=T, nseq=nseq, nchunk=nchunk), len(in_specs), y_into, 0, s_into, 1)
    y, s, cv = pl.pallas_call(
        body,
        grid=(b // nseq, nchunk),
        in_specs=in_specs + xspecs,
        out_specs=[pl.BlockSpec((R, C_INNER), lambda i, c: (blk(i, c), 0)), sout, cvspec],
        out_shape=[jax.ShapeDtypeStruct((zin.shape[0], C_INNER), F32),
                   jax.ShapeDtypeStruct((N_ODD,) + pair_shape, F32), jax.ShapeDtypeStruct(cv0.shape, F32)],
        input_output_aliases=aliases,
        scratch_shapes=[pltpu.VMEM((nseq, SUBLANES + T, C_CONV_DIM), F32)],
        compiler_params=_cparams(("arbitrary", "arbitrary")),
        name="ssd",
    )(zin, zin, zin, zin, dt_t, conv_w, conv_b.reshape(1, C_CONV_DIM),
      pad_row(dt_bias), dt_bias.reshape(C_HEADS, 1), pad_row(a_log), a_log.reshape(C_HEADS, 1),
      jnp.repeat(d_skip, C_HEADDIM).reshape(1, C_INNER), norm_w.reshape(1, C_INNER), s0, cv0, *xargs)
    return y, s, cv[:, SUBLANES - (C_CONV - 1):]


def _even_w_in(w):
    qkvo = w[:, :2 * A_QK + 2 * A_V]
    gates = w[:, 2 * A_QK + 2 * A_V:A_COLS]
    rwkv = w[:, A_COLS:]
    pad = jnp.zeros((D_MODEL, EVEN_GATE_PAD - 2 * A_HEADS), w.dtype)
    return jnp.concatenate([rwkv, gates, pad, qkvo], axis=1).astype(BF16)


def _odd_w_in(w):
    pad = jnp.zeros((D_MODEL, ODD_DT_PAD - C_HEADS), w.dtype)
    return jnp.concatenate([w, pad], axis=1).astype(BF16)


def kernel(x_prompt, x_sample, state_mlstm_C, state_mlstm_n, state_mlstm_m, state_rwkv_S,
           state_rwkv_shift, state_ssm, state_conv, p_prompt, p_sample,
           norm_mix, norm_ffn, w_ffn_up, w_ffn_down, w_ple_proj, norm_ple, w_ple_gate, norm_final,
           w_in_even, mlstm_b_i, mlstm_b_f, mlstm_norm, rwkv_mu, rwkv_w0, rwkv_w2, rwkv_a0, rwkv_a2,
           rwkv_g2, rwkv_k_k, rwkv_k_a, rwkv_r_k, rwkv_ln_w, rwkv_ln_b, w_out_even,
           w_in_odd, conv_w, conv_b, dt_bias, a_log, d_skip, ssm_norm, w_out_odd):
    bp, Lp, _ = x_prompt.shape
    bs, Ls, _ = x_sample.shape
    n_p, n_s = bp * Lp, bs * Ls
    n_tot = n_p + n_s
    xp = x_prompt.reshape(n_p, D_MODEL)
    xs = x_sample.reshape(n_s, D_MODEL)
    pp = p_prompt.reshape(DEPTH, n_p, PLE_DIM)
    ps = p_sample.reshape(DEPTH, n_s, PLE_DIM)
    x = None

    even_small = (mlstm_b_i, mlstm_b_f, mlstm_norm, rwkv_mu, rwkv_w0, rwkv_w2, rwkv_a0, rwkv_a2,
                  rwkv_g2, rwkv_k_k, rwkv_k_a, rwkv_r_k, rwkv_ln_w, rwkv_ln_b)
    odd_small = (conv_w, conv_b, dt_bias, a_log, d_skip, ssm_norm)

    zeros_even = (jnp.zeros((bp, A_HEADS, A_DK, A_DV), F32), jnp.zeros((bp, A_HEADS, A_DK), F32),
                  jnp.zeros((bp, A_HEADS), F32), jnp.zeros((bp, B_HEADS, B_DH, B_DH), F32),
                  jnp.zeros((bp, B_COLS), F32))
    zeros_odd = (jnp.zeros((bp, C_HEADS, C_HEADDIM, C_STATE), F32),
                 jnp.zeros((bp, C_CONV - 1, C_CONV_DIM), F32))

    st_p_even, st_s_even, st_p_odd, st_s_odd = [], [], [], []
    mC_p = mC_s = ssm_p = ssm_s = None
    for i in range(DEPTH):
        j = i // 2
        if i % 2 == 0:
            w_in = _even_w_in(w_in_even[j])
            if x is None:
                zin = norm_matmul(xp, norm_mix[i], w_in, 512, 0, n_tot)
                zin = norm_matmul(xs, norm_mix[i], w_in, 512, n_p, n_tot, zin)
            else:
                zin = norm_matmul(x, norm_mix[i], w_in, 512)
            small = [t[j] for t in even_small]
            gates_t = zin[:, EVEN_G_OFF:EVEN_G_OFF + 2 * A_HEADS].reshape(
                -1, MLSTM_ROWS, 2 * A_HEADS).transpose(0, 2, 1)
            ha, hb_p, g, sp = even_mixer(zin, gates_t, 0, bp, Lp, (None, None), j, mC_p, *zeros_even, *small)
            ha, hb_s, g, ss = even_mixer(zin, gates_t, n_p, bs, Ls, (ha, g), j, mC_s, state_mlstm_C,
                                         state_mlstm_n[j], state_mlstm_m[j], state_rwkv_S[j], state_rwkv_shift[j],
                                         *small)
            mC_p, mC_s = sp[0], ss[0]
            st_p_even.append(sp)
            st_s_even.append(ss)
            wo = w_out_even[j].astype(BF16)
            terms_p = [(ha, None, wo[:A_V], None), (hb_p, g, wo[A_V:], Lp)]
            terms_s = [(ha, None, wo[:A_V], None), (hb_s, g, wo[A_V:], 0)]
            if x is None:
                xo = matmul_res(terms_p, xp, 0, n_p, None, n_tot)
                x = matmul_res(terms_s, xs, n_p, n_s, xo, n_tot)
            else:
                xo = matmul_res(terms_p, x, 0, n_p)
                x = matmul_res(terms_s, x, n_p, n_s, xo)
        else:
            zin = norm_matmul(x, norm_mix[i], _odd_w_in(w_in_odd[j]), 768)
            small = [t[j] for t in odd_small]
            mix, ssm_p, cv_p = ssd_mixer(zin, 0, bp, Lp, None, j, ssm_p, *zeros_odd, *small)
            mix, ssm_s, cv_s = ssd_mixer(zin, n_p, bs, Ls, mix, j, ssm_s, state_ssm, state_conv[j], *small)
            st_p_odd.append(cv_p)
            st_s_odd.append(cv_s)
            x = matmul_res([(mix, None, w_out_odd[j].astype(BF16), None)], x)
        tail = (norm_ffn[i], w_ffn_up[i].astype(BF16), w_ffn_down[i].astype(BF16),
                norm_ple[i], w_ple_gate[i].astype(BF16))
        wp = w_ple_proj[i].astype(BF16)
        xo = ffn_ple(x, *tail, pp[i], wp, 0)
        x = ffn_ple(x, *tail, ps[i], wp, n_p, xo)
    y_prompt = final_norm(x, norm_final, 0, n_p).reshape(bp, Lp, D_MODEL)
    y_sample = final_norm(x, norm_final, n_p, n_s).reshape(bs, Ls, D_MODEL)
    stack = lambda sts, idx: jnp.stack([s[idx] for s in sts])
    ssm_shape = lambda b: (N_ODD, b, C_HEADS, C_HEADDIM, C_STATE)
    return (y_prompt, y_sample,
            mC_p, stack(st_p_even, 1), stack(st_p_even, 2), stack(st_p_even, 3),
            stack(st_p_even, 4), ssm_p.reshape(ssm_shape(bp)), jnp.stack(st_p_odd),
            mC_s, stack(st_s_even, 1), stack(st_s_even, 2), stack(st_s_even, 3),
            stack(st_s_even, 4), ssm_s.reshape(ssm_shape(bs)), jnp.stack(st_s_odd))
```

```python
import math
import functools
import jax
import jax.numpy as jnp
from jax import lax
from jax.experimental import pallas as pl
from jax.experimental.pallas import tpu as pltpu

D_MODEL = 1024
DEPTH = 4
F32 = jnp.float32
BF16 = jnp.bfloat16
EPS = 1e-6
N_EVEN = (DEPTH + 1) // 2
N_ODD = DEPTH // 2
D_FF = 4 * D_MODEL
PLE_DIM = 256

A_HEADS = 4
A_DK = D_MODEL // 8
A_DV = D_MODEL // 8
A_QK = A_HEADS * A_DK
A_V = A_HEADS * A_DV
A_COLS = 2 * A_QK + 2 * A_V + 2 * A_HEADS

B_HEADS = 8
B_DH = 64
B_W = B_HEADS * B_DH
B_W_RANK = 64
B_A_RANK = 64
B_G_RANK = 128
B_COLS = 3 * B_W + B_W_RANK + B_A_RANK + B_G_RANK
B_DECAY_OFFSET = 0.5
B_GN_EPS = 64e-5

EVEN_COLS = A_COLS + B_COLS
EVEN_OUT = A_V + B_W

C_INNER = 2 * D_MODEL
C_HEADDIM = 64
C_HEADS = C_INNER // C_HEADDIM
C_GROUPS = 4
C_HPG = C_HEADS // C_GROUPS
C_STATE = 128
C_CONV = 4
C_CHUNK = 128
C_CONV_DIM = C_INNER + 2 * C_GROUPS * C_STATE
ODD_COLS = C_INNER + C_CONV_DIM + C_HEADS

LANES = 128
SUBLANES = 8
VMEM_LIMIT = 56 * 1024 * 1024
SCAN_VMEM_LIMIT = 60 * 1024 * 1024
TOKEN_TILE = 512

EVEN_GATE_PAD = 256
EVEN_N = B_COLS + EVEN_GATE_PAD + 2 * A_QK + 2 * A_V
EVEN_G_OFF = B_COLS
EVEN_A_OFF = B_COLS + EVEN_GATE_PAD
ODD_DT_PAD = 256
ODD_N = C_INNER + C_CONV_DIM + ODD_DT_PAD
ODD_DT_OFF = C_INNER + C_CONV_DIM


def _cparams(sem, vmem_limit=VMEM_LIMIT):
    return pltpu.CompilerParams(dimension_semantics=sem, vmem_limit_bytes=vmem_limit)


def _rms(x, g):
    return x * lax.rsqrt(jnp.mean(x * x, axis=-1, keepdims=True) + EPS) * g


def _resident(shape):
    nd = len(shape)
    return pl.BlockSpec(shape, lambda *_: (0,) * nd, pipeline_mode=pl.Buffered(1))


def _rows(width):
    return pl.BlockSpec((TOKEN_TILE, width), lambda i: (i, 0))


def _fill_into(body, n_in, into, out_idx, into2=None, out_idx2=None):
    pairs = [(a, o) for a, o in ((into, out_idx), (into2, out_idx2)) if a is not None]
    if not pairs:
        return body, [], [], {}

    def skipping(*refs):
        return body(*refs[:n_in], *refs[n_in + len(pairs):])

    return (skipping, [pl.BlockSpec(memory_space=pl.ANY)] * len(pairs), [a for a, _ in pairs],
            {n_in + i: o for i, (_, o) in enumerate(pairs)})


def _norm_matmul_kernel(x_ref, g_ref, w_ref, o_ref, *, tn):
    xn = _rms(x_ref[...], g_ref[...]).astype(BF16)
    for n0 in range(0, w_ref.shape[1], tn):
        o_ref[:, n0:n0 + tn] = jnp.dot(xn, w_ref[:, n0:n0 + tn], preferred_element_type=F32)


def norm_matmul(x, g, w, tn, row0=0, total=None, into=None):
    m, n = x.shape[0], w.shape[1]
    total = m if total is None else total
    t0 = row0 // TOKEN_TILE
    in_specs = [_rows(D_MODEL), _resident((1, D_MODEL)), _resident(w.shape)]
    body, xspecs, xargs, aliases = _fill_into(functools.partial(_norm_matmul_kernel, tn=tn), len(in_specs), into, 0)
    return pl.pallas_call(
        body,
        grid=(m // TOKEN_TILE,),
        in_specs=in_specs + xspecs,
        out_specs=pl.BlockSpec((TOKEN_TILE, n), lambda i: (t0 + i, 0)),
        out_shape=jax.ShapeDtypeStruct((total, n), F32),
        input_output_aliases=aliases,
        compiler_params=_cparams(("arbitrary",)),
        name="norm_matmul",
    )(x, g.reshape(1, D_MODEL), w, *xargs)


FFN_CHUNK = 512


def _matmul_res_kernel(*refs, kinds):
    x_ref, o_ref = refs[-2], refs[-1]
    acc = x_ref[...]
    pos = 0
    for has_gate, channel_major in kinds:
        a = refs[pos][...]
        if channel_major:
            a = a.T
        if has_gate:
            a = a * refs[pos + 1][...]
        w_ref = refs[pos + 1 + has_gate]
        pos += 2 + has_gate
        acc = acc + jnp.dot(a.astype(BF16), w_ref[...], preferred_element_type=F32)
    o_ref[...] = acc


def matmul_res(terms, x, row0=0, nrows=None, into=None, total=None):
    m = x.shape[0] if total is None else total
    nrows = m if nrows is None else nrows
    t0 = row0 // TOKEN_TILE
    rows = lambda width: pl.BlockSpec((TOKEN_TILE, width), lambda i: (t0 + i, 0))
    xspec = rows(D_MODEL) if total is None else _rows(D_MODEL)
    specs, args, kinds = [], [], []
    for a, gate, w, seq_len in terms:
        if seq_len is None:
            specs.append(rows(a.shape[1]))
        elif seq_len == 0:
            specs.append(pl.BlockSpec((TOKEN_TILE, a.shape[1]), lambda i: (i, 0)))
        else:
            assert seq_len % TOKEN_TILE == 0
            per_seq = seq_len // TOKEN_TILE
            specs.append(pl.BlockSpec((None, a.shape[1], TOKEN_TILE), lambda i: (i // per_seq, 0, i % per_seq)))
        specs += ([rows(w.shape[0])] if gate is not None else []) + [_resident(w.shape)]
        args += [a] + ([gate] if gate is not None else []) + [w]
        kinds.append((int(gate is not None), bool(seq_len)))
    specs.append(xspec)
    body, xspecs, xargs, aliases = _fill_into(
        functools.partial(_matmul_res_kernel, kinds=tuple(kinds)), len(specs), into, 0)
    return pl.pallas_call(
        body,
        grid=(nrows // TOKEN_TILE,),
        in_specs=specs + xspecs,
        out_specs=rows(D_MODEL),
        out_shape=jax.ShapeDtypeStruct((m, D_MODEL), F32),
        input_output_aliases=aliases,
        compiler_params=_cparams(("arbitrary",)),
        name="matmul_res",
    )(*args, x, *xargs)


def _ffn_ple_kernel(x_ref, gf_ref, wu_ref, wd_ref, gp_ref, wg_ref, p_ref, wp_ref, o_ref, y_s):
    x = x_ref[...]
    xn = _rms(x, gf_ref[...]).astype(BF16)
    y_s[...] = x
    for c0 in range(0, D_FF, FFN_CHUNK):
        h = jnp.dot(xn, wu_ref[:, c0:c0 + FFN_CHUNK], preferred_element_type=F32)
        h = jnp.square(jnp.maximum(h, 0.0)).astype(BF16)
        y_s[...] += jnp.dot(h, wd_ref[c0:c0 + FFN_CHUNK, :], preferred_element_type=F32)
    y = y_s[...]
    yn = _rms(y, gp_ref[...]).astype(BF16)
    gate = jax.nn.sigmoid(jnp.dot(yn, wg_ref[...], preferred_element_type=F32))
    proj = jnp.dot(p_ref[...].astype(BF16), wp_ref[...], preferred_element_type=F32)
    o_ref[...] = y + proj * gate


def ffn_ple(x, g_ffn, wu, wd, g_ple, wg, p, wp, row0, into=None):
    t0 = row0 // TOKEN_TILE
    rows = pl.BlockSpec((TOKEN_TILE, D_MODEL), lambda i: (t0 + i, 0))
    row1 = lambda v: v.reshape(1, D_MODEL)
    in_specs = [rows, _resident((1, D_MODEL)), _resident(wu.shape), _resident(wd.shape),
                _resident((1, D_MODEL)), _resident(wg.shape), _rows(PLE_DIM), _resident(wp.shape)]
    body, xspecs, xargs, aliases = _fill_into(_ffn_ple_kernel, len(in_specs), into, 0)
    return pl.pallas_call(
        body,
        grid=(p.shape[0] // TOKEN_TILE,),
        in_specs=in_specs + xspecs,
        out_specs=rows,
        out_shape=jax.ShapeDtypeStruct(x.shape, F32),
        input_output_aliases=aliases,
        scratch_shapes=[pltpu.VMEM((TOKEN_TILE, D_MODEL), F32)],
        compiler_params=_cparams(("arbitrary",)),
        name="ffn_ple",
    )(x, row1(g_ffn), wu, wd, row1(g_ple), wg, p, wp, *xargs)


def _final_norm_kernel(x_ref, g_ref, o_ref):
    o_ref[...] = _rms(x_ref[...], g_ref[...])


def final_norm(x, g, row0, nrows):
    t0 = row0 // TOKEN_TILE
    return pl.pallas_call(
        _final_norm_kernel,
        grid=(nrows // TOKEN_TILE,),
        in_specs=[pl.BlockSpec((TOKEN_TILE, D_MODEL), lambda i: (t0 + i, 0)), _resident((1, D_MODEL))],
        out_specs=_rows(D_MODEL),
        out_shape=jax.ShapeDtypeStruct((nrows, D_MODEL), F32),
        compiler_params=_cparams(("arbitrary",)),
        name="final_norm",
    )(x, g.reshape(1, D_MODEL))


def _rwkv_operands(k, a, r, kk_t, ka_t, rk_t):
    kk = k * kk_t
    kk = kk * lax.rsqrt(jnp.maximum(jnp.sum(kk * kk, axis=1, keepdims=True), 1e-24))
    k2 = k * (1.0 + (a - 1.0) * ka_t)
    return -kk, kk * a, k2, jnp.sum(r * k2 * rk_t, axis=1)


def _rwkv_step(s_ref, tile0, ntiles, r, w, k, a, b, vrows):
    ys = []
    for lt in range(ntiles):
        rows = slice((tile0 + lt) * B_DH, (tile0 + lt + 1) * B_DH)
        s = s_ref[rows, :]
        sa = jnp.sum(s * a, axis=0, keepdims=True)
        s = s * w + sa * b + vrows[lt:lt + 1] * k
        s_ref[rows, :] = s
        ys.append(jnp.sum(s * r, axis=0, keepdims=True))
    return jnp.concatenate(ys, axis=0)


LONG_T = 128
LONG_NLT = B_DH // 2
KEY_PITCH = B_DH + SUBLANES
VAL_PITCH = LONG_NLT + SUBLANES


def _rwkv_scan_long_kernel(r_ref, w_ref, k_ref, a_ref, v_ref, kkt_ref, kat_ref, rkt_ref, lnw_ref, lnb_ref, s0_ref,
                           yt_ref, s_ref, or_s, ow_s, ok_s, oa_s, av_s, v_s, y_s):
    nb = yt_ref.shape[0]

    @pl.when(pl.program_id(0) == 0)
    def _():
        s_ref[...] = s0_ref[...]
        for scr in (or_s, ow_s, ok_s, oa_s, av_s, v_s, y_s):
            scr[...] = jnp.zeros(scr.shape, F32)

    lo = lax.broadcasted_iota(jnp.int32, (LONG_T, LANES), 1) < LANES // 2

    def tile(ref, c):
        m = ref[c]
        return jnp.concatenate([m, m], axis=0).T

    def relayout(c, carry):
        for ref, dst in ((r_ref, or_s), (w_ref, ow_s), (k_ref, ok_s), (a_ref, oa_s)):
            dst[pl.ds(c, LONG_T, stride=KEY_PITCH), :] = tile(ref, c)
            dst[pl.ds(c + LONG_NLT, LONG_T, stride=KEY_PITCH), :] = tile(ref, c + LONG_NLT)
        v_s[pl.ds(c, LONG_T, stride=VAL_PITCH), :] = jnp.where(lo, tile(v_ref, c), tile(v_ref, c + LONG_NLT))
        return carry

    lax.fori_loop(0, LONG_NLT, relayout, 0)

    def unpad(scr, pitch, n):
        return scr[...].reshape(LONG_T, pitch, LANES)[:, :n]

    def pad(x, pitch):
        zeros = jnp.zeros((LONG_T, pitch - x.shape[1], LANES), F32)
        return jnp.concatenate([x, zeros], axis=1).reshape(LONG_T * pitch, LANES)

    av, bv, k2, bonus = _rwkv_operands(unpad(ok_s, KEY_PITCH, B_DH), unpad(oa_s, KEY_PITCH, B_DH),
                                       unpad(or_s, KEY_PITCH, B_DH), kkt_ref[...], kat_ref[...], rkt_ref[...])
    av_s[...] = pad(av, KEY_PITCH)
    oa_s[...] = pad(bv, KEY_PITCH)
    ok_s[...] = pad(k2, KEY_PITCH)

    def step(t, carry):
        kr = pl.ds(pl.multiple_of(t * KEY_PITCH, SUBLANES), B_DH)
        vr = pl.ds(pl.multiple_of(t * VAL_PITCH, SUBLANES), LONG_NLT)
        y_s[vr, :] = _rwkv_step(s_ref, 0, LONG_NLT, or_s[kr, :], ow_s[kr, :], ok_s[kr, :], av_s[kr, :],
                                oa_s[kr, :], v_s[vr, :])
        return carry

    lax.fori_loop(0, LONG_T, step, 0)

    def head_sum(x):
        tot = jnp.sum(x, axis=1)
        return tot + pltpu.roll(tot, LANES // 2, axis=1)

    y = unpad(y_s, VAL_PITCH, LONG_NLT)
    d = y - (head_sum(y) * (1.0 / B_DH))[:, None, :]
    var = head_sum(d * d) * (1.0 / B_DH)
    y = (d * lax.rsqrt(var + B_GN_EPS)[:, None, :] * lnw_ref[...] + lnb_ref[...]
         + bonus[:, None, :] * unpad(v_s, VAL_PITCH, LONG_NLT))
    y_s[...] = pad(y, VAL_PITCH)

    def relayout_out(lt, carry):
        yt = y_s[pl.ds(lt, LONG_T, stride=VAL_PITCH), :].T
        for i2 in range(2):
            for b in range(nb):
                row = i2 * (LANES // 2) + b * B_HEADS
                yt_ref[b, pl.ds(lt + LONG_NLT * i2, B_HEADS, stride=B_DH), :] = yt[row:row + B_HEADS, :]
        return carry

    lax.fori_loop(0, LONG_NLT, relayout_out, 0)


def rwkv_scan_long(rc, wc, kc, ac, vc, S0, k_k, k_a, r_k, ln_w, ln_b):
    _, bh, L = rc.shape
    b = bh // B_HEADS
    assert bh * 2 == LANES and L % LONG_T == 0
    per_key = lambda p: jnp.tile(p.reshape(B_HEADS, B_DH).T, (1, LANES // B_HEADS))
    per_val = lambda p: jnp.concatenate(
        [jnp.tile(p.reshape(B_HEADS, 2, LONG_NLT)[:, i2].T, (1, b)) for i2 in range(2)], axis=1)
    s0 = S0.reshape(b, B_HEADS, 2, LONG_NLT, B_DH).transpose(3, 4, 2, 0, 1).reshape(LONG_NLT * B_DH, LANES)
    blk = pl.BlockSpec((B_DH, bh, LONG_T), lambda c: (0, 0, c))
    oblk = pl.BlockSpec((b, B_W, LONG_T), lambda c: (0, 0, c))
    const = lambda shape: pl.BlockSpec(shape, lambda c: (0,) * len(shape))
    big = pltpu.VMEM((LONG_T * KEY_PITCH, LANES), F32)
    small = pltpu.VMEM((LONG_T * VAL_PITCH, LANES), F32)
    yt, s = pl.pallas_call(
        _rwkv_scan_long_kernel,
        grid=(L // LONG_T,),
        in_specs=[blk] * 5 + [const((B_DH, LANES))] * 3 + [const((LONG_NLT, LANES))] * 2
        + [_resident((LONG_NLT * B_DH, LANES))],
        out_specs=[oblk, const((LONG_NLT * B_DH, LANES))],
        out_shape=[jax.ShapeDtypeStruct((b, B_W, L), F32), jax.ShapeDtypeStruct(s0.shape, F32)],
        scratch_shapes=[big] * 5 + [small, small],
        compiler_params=_cparams(("arbitrary",), SCAN_VMEM_LIMIT),
        name="rwkv_scan_long",
    )(rc, wc, kc, ac, vc, per_key(k_k), per_key(k_a), per_key(r_k), per_val(ln_w), per_val(ln_b), s0)
    s = s.reshape(LONG_NLT, B_DH, 2, b, B_HEADS).transpose(3, 4, 2, 0, 1).reshape(b, B_HEADS, B_DH, B_DH)
    return yt, s


def _rwkv_scan_short_kernel(r_ref, w_ref, k_ref, a_ref, v_ref, kkt_ref, kat_ref, rkt_ref, lnw_ref, lnb_ref, s0_ref,
                            y_ref, sout_ref, s_s, or_s, ow_s, ok_s, oa_s, av_s, ov_s, y_s, *, L):
    nt = 2 * B_DH
    for q in range(nt * B_DH // LANES):
        s_s[q * LANES:(q + 1) * LANES, :] = s0_ref[:, q * LANES:(q + 1) * LANES].T
    for ref, dst in ((r_ref, or_s), (w_ref, ow_s), (k_ref, ok_s), (a_ref, oa_s), (v_ref, ov_s)):
        for t in range(L):
            dst[t] = ref[pl.ds(t, LANES, stride=L), :].T.reshape(2, B_DH, LANES)
    shape3 = (L * 2, B_DH, LANES)
    tiles = lambda ref: jnp.concatenate([ref[...]] * L, axis=0)
    av, bv, k2, bonus = _rwkv_operands(ok_s[...].reshape(shape3), oa_s[...].reshape(shape3),
                                       or_s[...].reshape(shape3), tiles(kkt_ref), tiles(kat_ref), tiles(rkt_ref))
    av_s[...] = av.reshape(L, 2, B_DH, LANES)
    oa_s[...] = bv.reshape(L, 2, B_DH, LANES)
    ok_s[...] = k2.reshape(L, 2, B_DH, LANES)

    def step(t, carry):
        for h in range(2):
            y_s[t, h] = _rwkv_step(s_s, h * B_DH, B_DH, or_s[t, h], ow_s[t, h], ok_s[t, h], av_s[t, h],
                                   oa_s[t, h], ov_s[t, h])
        return carry

    lax.fori_loop(0, L, step, 0)

    y = y_s[...]
    d = y - jnp.mean(y, axis=2, keepdims=True)
    var = jnp.mean(d * d, axis=2, keepdims=True)
    y = (d * lax.rsqrt(var + B_GN_EPS) * lnw_ref[...] + lnb_ref[...]
         + bonus.reshape(L, 2, 1, LANES) * ov_s[...])
    for t in range(L):
        y_ref[pl.ds(t, LANES, stride=L), :] = y[t].reshape(nt, LANES).T
    for q in range(nt * B_DH // LANES):
        sout_ref[:, q * LANES:(q + 1) * LANES] = s_s[q * LANES:(q + 1) * LANES, :].T


def rwkv_scan_short(r, w, k, a, v, S0, L, k_k, k_a, r_k, ln_w, ln_b):
    n = r.shape[0]
    b = n // L
    assert b == LANES
    npair = B_HEADS // 2
    wide = lambda p: jnp.broadcast_to(p.reshape(npair, 2, B_DH, 1), (npair, 2, B_DH, LANES))
    s0 = S0.reshape(b, B_HEADS * B_DH * B_DH)
    blk = pl.BlockSpec((n, LANES), lambda p: (0, p))
    cblk = pl.BlockSpec((None, 2, B_DH, LANES), lambda p: (p, 0, 0, 0))
    sblk = pl.BlockSpec((b, 2 * B_DH * B_DH), lambda p: (0, p))
    op = pltpu.VMEM((L, 2, B_DH, LANES), F32)
    y, s = pl.pallas_call(
        functools.partial(_rwkv_scan_short_kernel, L=L),
        grid=(npair,),
        in_specs=[blk] * 5 + [cblk] * 5 + [sblk],
        out_specs=[blk, sblk],
        out_shape=[jax.ShapeDtypeStruct((n, B_W), F32), jax.ShapeDtypeStruct(s0.shape, F32)],
        scratch_shapes=[pltpu.VMEM((2 * B_DH * B_DH, LANES), F32)] + [op] * 7,
        compiler_params=_cparams(("arbitrary",)),
        name="rwkv_scan_short",
    )(r, w, k, a, v, wide(k_k), wide(k_a), wide(r_k), wide(ln_w), wide(ln_b), s0)
    return y, s.reshape(b, B_HEADS, B_DH, B_DH)


RWKV_PREP_ROWS = 256
RWKV_SHORT_ROWS = 64
B_LORA_OFF = 3 * B_W


def _rwkv_prep_kernel(z_ref, sh0_ref, mu_ref, wwa_ref, g2_ref, w0_ref, a0_ref,
                      r_ref, w_ref, k_ref, a_ref, v_ref, g_ref, sh_ref, *scratch, T, nseq, channel_major):
    @pl.when(pl.program_id(1) == 0)
    def _():
        sh_ref[...] = sh0_ref[...]

    z = z_ref[...]
    rowid = lax.broadcasted_iota(jnp.int32, (z.shape[0], 1), 0)
    zprev = pltpu.roll(z, 1, axis=0)
    for u in range(nseq):
        zprev = jnp.where(rowid == u * T, sh_ref[u], zprev)
    for u in range(nseq):
        sh_ref[u] = z[(u + 1) * T - 1:(u + 1) * T, :]
    zs = z + (zprev - z) * mu_ref[...]
    r = zs[:, :B_W]
    k = zs[:, B_W:2 * B_W]
    lora = zs[:, B_LORA_OFF:B_LORA_OFF + LANES]
    lane = lax.broadcasted_iota(jnp.int32, lora.shape, 1)
    lora = jnp.where(lane < B_W_RANK, jnp.tanh(lora), lora).astype(BF16)
    wa = jnp.dot(lora, wwa_ref[...], preferred_element_type=F32)
    w_log = -jax.nn.softplus(-(w0_ref[...] + wa[:, :B_W])) - B_DECAY_OFFSET
    a = jax.nn.sigmoid(a0_ref[...] + wa[:, B_W:])
    zg = zs[:, B_LORA_OFF + LANES:B_LORA_OFF + LANES + B_G_RANK]
    def emit(ref, x):
        if not channel_major:
            ref[...] = x
            return
        xt_s = scratch[0]
        xt = x.T
        for j in range(xt_s.shape[0]):
            for h in range(B_HEADS):
                xt_s[j, h * KEY_PITCH:h * KEY_PITCH + B_DH, :] = xt[h * B_DH:(h + 1) * B_DH, j * LANES:(j + 1) * LANES]
        for c in range(B_DH):
            for j in range(xt_s.shape[0]):
                ref[c, :, j * LANES:(j + 1) * LANES] = xt_s[j, pl.ds(c, B_HEADS, stride=KEY_PITCH), :]

    emit(r_ref, r)
    emit(w_ref, jnp.exp(-jnp.exp(w_log)))
    emit(k_ref, k)
    emit(a_ref, a)
    emit(v_ref, zs[:, 2 * B_W:3 * B_W])
    g_ref[...] = jnp.dot(jax.nn.sigmoid(zg).astype(BF16), g2_ref[...], preferred_element_type=F32)


def rwkv_prep(zin, row0, b, L, gate_into, shift0, mu, w0, w2, a0, a2, g2):
    assert B_W_RANK + B_A_RANK == LANES
    T = math.gcd(L, RWKV_PREP_ROWS)
    channel_major = T == RWKV_PREP_ROWS
    nseq = 1 if channel_major else RWKV_SHORT_ROWS // T
    assert nseq == 1 or (L == T and b % nseq == 0)
    R = T * nseq
    nchunk = L // T
    blk0 = row0 // R
    wwa = jnp.zeros((LANES, 2 * B_W), F32).at[:B_W_RANK, :B_W].set(w2).at[B_W_RANK:, B_W:].set(a2).astype(BF16)
    row = lambda v: v.reshape(1, -1)
    if channel_major:
        blk = pl.BlockSpec((B_DH, B_HEADS, R), lambda i, c: (0, i, c))
        oshape = jax.ShapeDtypeStruct((B_DH, b * B_HEADS, L), F32)
    else:
        blk = pl.BlockSpec((R, B_W), lambda i, c: (i * nchunk + c, 0))
        oshape = jax.ShapeDtypeStruct((b * L, B_W), F32)
    gblk = pl.BlockSpec((R, B_W), lambda i, c: (blk0 + i * nchunk + c, 0))
    shspec = pl.BlockSpec((nseq, 1, B_COLS), lambda i, c: (i, 0, 0))
    const = lambda shape: pl.BlockSpec(shape, lambda i, c: (0,) * len(shape))
    in_specs = [pl.BlockSpec((R, B_COLS), lambda i, c: (blk0 + i * nchunk + c, 0)), shspec, const((1, B_COLS)),
                const((LANES, 2 * B_W)), const((B_G_RANK, B_W)), const((1, B_W)), const((1, B_W))]
    body, xspecs, xargs, aliases = _fill_into(
        functools.partial(_rwkv_prep_kernel, T=T, nseq=nseq, channel_major=channel_major),
        len(in_specs), gate_into, 5)
    outs = pl.pallas_call(
        body,
        grid=(b // nseq, nchunk),
        in_specs=in_specs + xspecs,
        out_specs=[blk] * 5 + [gblk, shspec],
        out_shape=[oshape] * 5
        + [jax.ShapeDtypeStruct((zin.shape[0], B_W), F32), jax.ShapeDtypeStruct((b, 1, B_COLS), F32)],
        input_output_aliases=aliases,
        scratch_shapes=[pltpu.VMEM((R // LANES, B_HEADS * KEY_PITCH, LANES), F32)] if channel_major else [],
        compiler_params=_cparams(("arbitrary", "arbitrary")),
        name="rwkv_prep",
    )(zin, shift0.reshape(b, 1, B_COLS), row(mu), wwa, g2.astype(BF16), row(w0), row(a0), *xargs)
    return outs[:5], outs[5], outs[6].reshape(b, B_COLS)


MLSTM_ROWS = 128
HIGHEST = lax.Precision.HIGHEST


def _mlstm_kernel(q_ref, k_ref, v_ref, o_ref, g_ref, gt_ref, brow_ref, bcol_ref, nw_ref,
                  c0_ref, n0_ref, m0_ref, h_ref, c_ref, n_ref, m_ref, *, T, nseq):
    R = MLSTM_ROWS

    @pl.when(pl.program_id(1) == 0)
    def _():
        c_ref[...] = c0_ref[...]
        n_ref[...] = n0_ref[...]
        m_ref[...] = m0_ref[...]

    shift = T.bit_length() - 1
    ri = lax.broadcasted_iota(jnp.int32, (R, R), 0)
    ci = lax.broadcasted_iota(jnp.int32, (R, R), 1)
    mask = (ci <= ri) & (jnp.right_shift(ri, shift) == jnp.right_shift(ci, shift))
    lmat = mask.astype(F32)
    rowid = lax.broadcasted_iota(jnp.int32, (R, 1), 0)
    rsel = [(rowid >= u * T) & (rowid < (u + 1) * T) for u in range(nseq)]

    g = g_ref[...] + brow_ref[...]
    lane = lax.broadcasted_iota(jnp.int32, g.shape, 1)
    glog = jnp.where((lane >= A_HEADS) & (lane < 2 * A_HEADS), jax.nn.log_sigmoid(g), g)
    gt = gt_ref[...] + bcol_ref[...]
    sub = lax.broadcasted_iota(jnp.int32, gt.shape, 0)
    gtlog = jnp.where(sub >= A_HEADS, jax.nn.log_sigmoid(gt), gt)
    bc_col = jnp.dot(lmat, glog, precision=HIGHEST, preferred_element_type=F32)
    bc_row = lax.dot_general(gtlog, lmat, (((1,), (1,)), ((), ())), precision=HIGHEST,
                             preferred_element_type=F32)
    lane_m = lax.broadcasted_iota(jnp.int32, (1, LANES), 1)
    m_old = [m_ref[u] for u in range(nseq)]
    m_out = [jnp.zeros((1, LANES), F32) for _ in range(nseq)]

    for h in range(A_HEADS):
        hs = slice(h * A_DK, (h + 1) * A_DK)
        bcc = bc_col[:, A_HEADS + h:A_HEADS + h + 1]
        bcr = bc_row[A_HEADS + h:A_HEADS + h + 1, :]
        lir = gtlog[h:h + 1, :]
        lic = glog[:, h:h + 1]
        m_u = [m_old[u][:, h:h + 1] for u in range(nseq)]
        m_col = m_u[0]
        for u in range(1, nseq):
            m_col = jnp.where(rsel[u], m_u[u], m_col)
        dmat = jnp.where(mask, bcc - bcr + lir, -jnp.inf)
        inter = bcc + m_col
        mt = jnp.maximum(inter, jnp.max(dmat, axis=1, keepdims=True))
        p = jnp.exp(dmat - mt)
        qh = q_ref[:, hs] * (A_DK ** -0.5)
        kh = k_ref[:, hs]
        qb, kb, vb = qh.astype(BF16), kh.astype(BF16), v_ref[:, hs].astype(BF16)
        wq = lax.dot_general(qb, kb, (((1,), (1,)), ((), ())), preferred_element_type=F32) * p
        wi = jnp.exp(inter - mt)
        c_old = [c_ref[u, h] for u in range(nseq)]
        n_old = [n_ref[u, h:h + 1, :] for u in range(nseq)]
        qc = jnp.dot(qb, c_old[0].astype(BF16), preferred_element_type=F32)
        qn = jnp.sum(qh * n_old[0], axis=1, keepdims=True)
        for u in range(1, nseq):
            qc = jnp.where(rsel[u], jnp.dot(qb, c_old[u].astype(BF16), preferred_element_type=F32), qc)
            qn = jnp.where(rsel[u], jnp.sum(qh * n_old[u], axis=1, keepdims=True), qn)
        num = jnp.dot(wq.astype(BF16), vb, preferred_element_type=F32) + wi * qc
        den = jnp.sum(wq, axis=1, keepdims=True) + wi * qn
        hh = num / jnp.maximum(jnp.abs(den), jnp.exp(-mt))
        hh = hh * lax.rsqrt(jnp.mean(hh * hh, axis=-1, keepdims=True) + EPS)
        h_ref[:, hs] = hh * nw_ref[:, hs] * jax.nn.sigmoid(o_ref[:, hs])
        for u in range(nseq):
            b_last = bcc[(u + 1) * T - 1:(u + 1) * T, :]
            gs = b_last - bcc + lic
            gmax = jnp.max(gs if nseq == 1 else jnp.where(rsel[u], gs, -jnp.inf), axis=0, keepdims=True)
            m_new = jnp.maximum(b_last + m_u[u], gmax)
            decay = jnp.exp(b_last + m_u[u] - m_new)
            ws = jnp.exp(gs - m_new)
            if nseq > 1:
                ws = jnp.where(rsel[u], ws, 0.0)
            kw = kh * ws
            c_ref[u, h] = decay * c_old[u] + lax.dot_general(
                kw.astype(BF16), vb, (((0,), (0,)), ((), ())), preferred_element_type=F32)
            n_ref[u, h:h + 1, :] = decay * n_old[u] + jnp.sum(kw, axis=0, keepdims=True)
            m_out[u] = jnp.where(lane_m == h, m_new, m_out[u])
    for u in range(nseq):
        m_ref[u] = m_out[u]


def mlstm(zin, gates_t, row0, b, L, h_into, layer, c_into, b_i, b_f, m_norm, C0, n0, m0):
    R = MLSTM_ROWS
    T = math.gcd(L, R)
    nseq = R // T
    assert nseq == 1 or (L == T and b % nseq == 0)
    nchunk = L // T
    blk0 = row0 // R
    bias = jnp.concatenate([b_i, b_f])
    bias_row = jnp.zeros((1, LANES), F32).at[0, :2 * A_HEADS].set(bias)
    bias_col = bias.reshape(2 * A_HEADS, 1)
    m0p = jnp.zeros((b, 1, LANES), F32).at[:, 0, :A_HEADS].set(m0)
    blk = lambda i, c: blk0 + i * nchunk + c
    rowblk = lambda col: pl.BlockSpec((R, A_QK), lambda i, c: (blk(i, c), EVEN_A_OFF // A_QK + col))
    cspec = pl.BlockSpec((nseq, A_HEADS, A_DK, A_DV), lambda i, c: (i, 0, 0, 0))
    cout = pl.BlockSpec((None, nseq, A_HEADS, A_DK, A_DV), lambda i, c: (layer, i, 0, 0, 0))
    nspec = pl.BlockSpec((nseq, A_HEADS, A_DK), lambda i, c: (i, 0, 0))
    mspec = pl.BlockSpec((nseq, 1, LANES), lambda i, c: (i, 0, 0))
    const = lambda shape: pl.BlockSpec(shape, lambda i, c: (0,) * len(shape))
    in_specs = [rowblk(0), rowblk(1), rowblk(2), rowblk(3),
                pl.BlockSpec((R, LANES), lambda i, c: (blk(i, c), EVEN_G_OFF // LANES)),
                pl.BlockSpec((None, 2 * A_HEADS, R), lambda i, c: (blk(i, c), 0, 0)),
                const((1, LANES)), const((2 * A_HEADS, 1)), const((1, A_V)),
                cspec if C0.ndim == 4 else cout, nspec, mspec]
    body, xspecs, xargs, aliases = _fill_into(
        functools.partial(_mlstm_kernel, T=T, nseq=nseq), len(in_specs), h_into, 0, c_into, 1)
    h, C, n, m = pl.pallas_call(
        body,
        grid=(b // nseq, nchunk),
        in_specs=in_specs + xspecs,
        out_specs=[pl.BlockSpec((R, A_V), lambda i, c: (blk(i, c), 0)), cout, nspec, mspec],
        out_shape=[jax.ShapeDtypeStruct((zin.shape[0], A_V), F32),
                   jax.ShapeDtypeStruct((N_EVEN,) + C0.shape[-4:], F32), jax.ShapeDtypeStruct(n0.shape, F32),
                   jax.ShapeDtypeStruct(m0p.shape, F32)],
        input_output_aliases=aliases,
        compiler_params=_cparams(("arbitrary", "arbitrary")),
        name="mlstm",
    )(zin, zin, zin, zin, zin, gates_t, bias_row, bias_col, m_norm.reshape(1, A_V), C0, n0, m0p, *xargs)
    return h, C, n, m[:, 0, :A_HEADS]


def even_mixer(zin, gates_t, row0, b, L, into, layer, c_into, C0, n0, m0, S0, shift0, b_i, b_f, m_norm,
               mu, w0, w2, a0, a2, g2, k_k, k_a, r_k, ln_w, ln_b):
    hA, C, n, m = mlstm(zin, gates_t, row0, b, L, into[0], layer, c_into, b_i, b_f, m_norm, C0, n0, m0)
    ops, g, shift = rwkv_prep(zin, row0, b, L, into[1], shift0, mu, w0, w2, a0, a2, g2)
    if ops[0].ndim == 3:
        hB, S = rwkv_scan_long(*ops, S0, k_k, k_a, r_k, ln_w, ln_b)
    else:
        hB, S = rwkv_scan_short(*ops, S0, L, k_k, k_a, r_k, ln_w, ln_b)
    return hA, hB, g, (C, n, m, S, shift)


C_PAIRS = C_HEADS // 2
C_GROUP_W = C_INNER // C_GROUPS
C_BC_W = 2 * C_GROUPS * C_STATE
SSD_LONG_ROWS = C_CHUNK
SSD_SHORT_ROWS = 32


def _ssd_kernel(z_ref, x_ref, bc_ref, dt_ref, dtt_ref, cw_ref, cb_ref, dtb_ref, dtbt_ref, al_ref, alt_ref,
                dsk_ref, nw_ref, s0_ref, cv0_ref, y_ref, s_ref, cv_ref, stg_ref, *, T, nseq, nchunk):
    R = T * nseq
    tstate = nchunk > 1
    assert not tstate or nseq == 1

    @pl.when(pl.program_id(1) == 0)
    def _():
        if tstate:
            for pr in range(C_PAIRS):
                s_ref[0, pr] = s0_ref[0, pr].T
        else:
            s_ref[...] = s0_ref[...]
        cv_ref[...] = cv0_ref[...]

    shift = T.bit_length() - 1
    ri = lax.broadcasted_iota(jnp.int32, (R, R), 0)
    ci = lax.broadcasted_iota(jnp.int32, (R, R), 1)
    mask = (ci <= ri) & (jnp.right_shift(ri, shift) == jnp.right_shift(ci, shift))
    lmat = mask.astype(F32)
    rowid = lax.broadcasted_iota(jnp.int32, (R, 1), 0)
    rsel = [(rowid >= u * T) & (rowid < (u + 1) * T) for u in range(nseq)]

    def conv_silu(src_ref, src_col, col):
        cols = slice(col, col + LANES)
        accs = []
        for u in range(nseq):
            stg_ref[u, 0:SUBLANES, cols] = cv_ref[u, :, cols]
            stg_ref[u, SUBLANES:SUBLANES + T, cols] = src_ref[u * T:(u + 1) * T, src_col:src_col + LANES]
            acc = cb_ref[:, cols]
            for d in range(C_CONV):
                acc = acc + stg_ref[u, SUBLANES - d:SUBLANES - d + T, cols] * cw_ref[C_CONV - 1 - d:C_CONV - d, cols]
            accs.append(acc)
            cv_ref[u, :, cols] = stg_ref[u, T:T + SUBLANES, cols]
        acc = accs[0] if nseq == 1 else jnp.concatenate(accs, axis=0)
        return acc * jax.nn.sigmoid(acc)

    dtv = jax.nn.softplus(dt_ref[...] + dtb_ref[...])
    dtt = jax.nn.softplus(dtt_ref[...] + dtbt_ref[...])
    cum_col = jnp.dot(lmat, dtv * (-jnp.exp(al_ref[...])), precision=HIGHEST, preferred_element_type=F32)
    cum_row = lax.dot_general(dtt * (-jnp.exp(alt_ref[...])), lmat, (((1,), (1,)), ((), ())),
                              precision=HIGHEST, preferred_element_type=F32)

    lo = lax.broadcasted_iota(jnp.int32, (R, LANES), 1) < C_HEADDIM
    rlo = lax.broadcasted_iota(jnp.int32, (LANES, 1), 0) < C_HEADDIM
    nt = (((1,), (1,)), ((), ()))
    tn = (((0,), (0,)), ((), ()))
    pairs_per_group = C_PAIRS // C_GROUPS
    for g in range(C_GROUPS):
        bgf = conv_silu(bc_ref, g * C_STATE, C_INNER + g * C_STATE)
        bg = bgf.astype(BF16)
        cg = conv_silu(bc_ref, (C_GROUPS + g) * C_STATE, C_INNER + (C_GROUPS + g) * C_STATE).astype(BF16)
        cbm = lax.dot_general(cg, bg, nt, preferred_element_type=F32)
        ys = None
        if tstate:
            bgt = bgf.T.astype(BF16)
            sgt = jnp.concatenate([s_ref[0, g * pairs_per_group + q] for q in range(pairs_per_group)], axis=1)
            ys = jnp.dot(cg, sgt.astype(BF16), preferred_element_type=F32)
        for u in range(0 if tstate else nseq):
            sg = s_ref[u, g * pairs_per_group:(g + 1) * pairs_per_group].reshape(C_GROUP_W, C_STATE)
            t_u = lax.dot_general(cg, sg.astype(BF16), nt, preferred_element_type=F32)
            ys = t_u if u == 0 else jnp.where(rsel[u], t_u, ys)
        for q in range(pairs_per_group):
            pr = g * pairs_per_group + q
            ps = slice(pr * LANES, (pr + 1) * LANES)
            xp = conv_silu(x_ref, pr * LANES, pr * LANES)
            cc = [cum_col[:, 2 * pr + e:2 * pr + e + 1] for e in range(2)]
            intra = None
            for e, keep in ((0, lo), (1, jnp.logical_not(lo))):
                hh = 2 * pr + e
                seg = jnp.exp(jnp.where(mask, cc[e] - cum_row[hh:hh + 1, :], -jnp.inf))
                mix = cbm * seg * dtt[hh:hh + 1, :]
                part = jnp.dot(mix.astype(BF16), jnp.where(keep, xp, 0.0).astype(BF16),
                               preferred_element_type=F32)
                intra = part if intra is None else intra + part
            scale = jnp.where(lo, jnp.exp(cc[0]), jnp.exp(cc[1]))
            yp = intra + scale * ys[:, q * LANES:(q + 1) * LANES] + dsk_ref[:, ps] * xp
            zp = z_ref[:, ps]
            y_ref[:, ps] = yp * (zp * jax.nn.sigmoid(zp))
            for u in range(nseq):
                last = (u + 1) * T - 1
                ct = [cc[e][last:last + 1, :] for e in range(2)]
                tail = jnp.where(lo, jnp.exp(ct[0] - cc[0]) * dtv[:, 2 * pr:2 * pr + 1],
                                 jnp.exp(ct[1] - cc[1]) * dtv[:, 2 * pr + 1:2 * pr + 2])
                xw = xp * tail
                if nseq > 1:
                    xw = jnp.where(rsel[u], xw, 0.0)
                if tstate:
                    upd = jnp.dot(bgt, xw.astype(BF16), preferred_element_type=F32)
                    dec = jnp.where(lo[:1], jnp.exp(ct[0]), jnp.exp(ct[1]))
                else:
                    upd = lax.dot_general(xw.astype(BF16), bg, tn, preferred_element_type=F32)
                    dec = jnp.where(rlo, jnp.exp(ct[0]), jnp.exp(ct[1]))
                s_ref[u, pr] = dec * s_ref[u, pr] + upd

    if tstate:
        @pl.when(pl.program_id(1) == nchunk - 1)
        def _():
            for pr in range(C_PAIRS):
                s_ref[0, pr] = s_ref[0, pr].T

    for g in range(C_GROUPS):
        gs_ = slice(g * C_GROUP_W, (g + 1) * C_GROUP_W)
        yg = y_ref[:, gs_]
        y_ref[:, gs_] = yg * lax.rsqrt(jnp.mean(yg * yg, axis=-1, keepdims=True) + EPS) * nw_ref[:, gs_]


def ssd_mixer(zin, row0, b, L, y_into, layer, s_into, ssm0, conv0, conv_w, conv_b, dt_bias, a_log, d_skip, norm_w):
    T = math.gcd(L, SSD_LONG_ROWS)
    nseq = 1 if T == SSD_LONG_ROWS else SSD_SHORT_ROWS // T
    assert nseq == 1 or (L == T and b % nseq == 0)
    R = T * nseq
    nchunk = L // T
    nblk = b * L // R
    blk0 = row0 // R
    dt_t = zin[row0:row0 + b * L, ODD_DT_OFF:ODD_DT_OFF + C_HEADS].reshape(nblk, R, C_HEADS).transpose(0, 2, 1)
    pad_row = lambda v: jnp.zeros((1, LANES), F32).at[0, :C_HEADS].set(v)
    pair_shape = (b, C_PAIRS, 2 * C_HEADDIM, C_STATE)
    per_layer = ssm0.ndim == 4
    s0 = ssm0.reshape(pair_shape if per_layer else (ssm0.shape[0],) + pair_shape)
    cv0 = jnp.concatenate([jnp.zeros((b, SUBLANES - (C_CONV - 1), C_CONV_DIM), F32), conv0], axis=1)
    blk = lambda i, c: blk0 + i * nchunk + c
    sspec = pl.BlockSpec((nseq, C_PAIRS, 2 * C_HEADDIM, C_STATE), lambda i, c: (i, 0, 0, 0))
    sout = pl.BlockSpec((None, nseq, C_PAIRS, 2 * C_HEADDIM, C_STATE), lambda i, c: (layer, i, 0, 0, 0))
    cvspec = pl.BlockSpec((nseq, SUBLANES, C_CONV_DIM), lambda i, c: (i, 0, 0))
    const = lambda shape: pl.BlockSpec(shape, lambda i, c: (0,) * len(shape))
    in_specs = [pl.BlockSpec((R, C_INNER), lambda i, c: (blk(i, c), 0)),
                pl.BlockSpec((R, C_INNER), lambda i, c: (blk(i, c), 1)),
                pl.BlockSpec((R, C_BC_W), lambda i, c: (blk(i, c), 2 * C_INNER // C_BC_W)),
                pl.BlockSpec((R, LANES), lambda i, c: (blk(i, c), ODD_DT_OFF // LANES)),
                pl.BlockSpec((None, C_HEADS, R), lambda i, c: (i * nchunk + c, 0, 0)),
                const((C_CONV, C_CONV_DIM)), const((1, C_CONV_DIM)),
                const((1, LANES)), const((C_HEADS, 1)), const((1, LANES)), const((C_HEADS, 1)),
                const((1, C_INNER)), const((1, C_INNER)), sspec if per_layer else sout, cvspec]
    body, xspecs, xargs, aliases = _fill_into(
        functools.partial(_ssd_kernel, T=T, nseq=nseq, nchunk=nchunk), len(in_specs), y_into, 0, s_into, 1)
    y, s, cv = pl.pallas_call(
        body,
        grid=(b // nseq, nchunk),
        in_specs=in_specs + xspecs,
        out_specs=[pl.BlockSpec((R, C_INNER), lambda i, c: (blk(i, c), 0)), sout, cvspec],
        out_shape=[jax.ShapeDtypeStruct((zin.shape[0], C_INNER), F32),
                   jax.ShapeDtypeStruct((N_ODD,) + pair_shape, F32), jax.ShapeDtypeStruct(cv0.shape, F32)],
        input_output_aliases=aliases,
        scratch_shapes=[pltpu.VMEM((nseq, SUBLANES + T, C_CONV_DIM), F32)],
        compiler_params=_cparams(("arbitrary", "arbitrary")),
        name="ssd",
    )(zin, zin, zin, zin, dt_t, conv_w, conv_b.reshape(1, C_CONV_DIM),
      pad_row(dt_bias), dt_bias.reshape(C_HEADS, 1), pad_row(a_log), a_log.reshape(C_HEADS, 1),
      jnp.repeat(d_skip, C_HEADDIM).reshape(1, C_INNER), norm_w.reshape(1, C_INNER), s0, cv0, *xargs)
    return y, s, cv[:, SUBLANES - (C_CONV - 1):]


def _even_w_in(w):
    qkvo = w[:, :2 * A_QK + 2 * A_V]
    gates = w[:, 2 * A_QK + 2 * A_V:A_COLS]
    rwkv = w[:, A_COLS:]
    pad = jnp.zeros((D_MODEL, EVEN_GATE_PAD - 2 * A_HEADS), w.dtype)
    return jnp.concatenate([rwkv, gates, pad, qkvo], axis=1).astype(BF16)


def _odd_w_in(w):
    pad = jnp.zeros((D_MODEL, ODD_DT_PAD - C_HEADS), w.dtype)
    return jnp.concatenate([w, pad], axis=1).astype(BF16)


def kernel(x_prompt, x_sample, state_mlstm_C, state_mlstm_n, state_mlstm_m, state_rwkv_S,
           state_rwkv_shift, state_ssm, state_conv, p_prompt, p_sample,
           norm_mix, norm_ffn, w_ffn_up, w_ffn_down, w_ple_proj, norm_ple, w_ple_gate, norm_final,
           w_in_even, mlstm_b_i, mlstm_b_f, mlstm_norm, rwkv_mu, rwkv_w0, rwkv_w2, rwkv_a0, rwkv_a2,
           rwkv_g2, rwkv_k_k, rwkv_k_a, rwkv_r_k, rwkv_ln_w, rwkv_ln_b, w_out_even,
           w_in_odd, conv_w, conv_b, dt_bias, a_log, d_skip, ssm_norm, w_out_odd):
    bp, Lp, _ = x_prompt.shape
    bs, Ls, _ = x_sample.shape
    n_p, n_s = bp * Lp, bs * Ls
    n_tot = n_p + n_s
    xp = x_prompt.reshape(n_p, D_MODEL)
    xs = x_sample.reshape(n_s, D_MODEL)
    pp = p_prompt.reshape(DEPTH, n_p, PLE_DIM)
    ps = p_sample.reshape(DEPTH, n_s, PLE_DIM)
    x = None

    even_small = (mlstm_b_i, mlstm_b_f, mlstm_norm, rwkv_mu, rwkv_w0, rwkv_w2, rwkv_a0, rwkv_a2,
                  rwkv_g2, rwkv_k_k, rwkv_k_a, rwkv_r_k, rwkv_ln_w, rwkv_ln_b)
    odd_small = (conv_w, conv_b, dt_bias, a_log, d_skip, ssm_norm)

    zeros_even = (jnp.zeros((bp, A_HEADS, A_DK, A_DV), F32), jnp.zeros((bp, A_HEADS, A_DK), F32),
                  jnp.zeros((bp, A_HEADS), F32), jnp.zeros((bp, B_HEADS, B_DH, B_DH), F32),
                  jnp.zeros((bp, B_COLS), F32))
    zeros_odd = (jnp.zeros((bp, C_HEADS, C_HEADDIM, C_STATE), F32),
                 jnp.zeros((bp, C_CONV - 1, C_CONV_DIM), F32))

    st_p_even, st_s_even, st_p_odd, st_s_odd = [], [], [], []
    mC_p = mC_s = ssm_p = ssm_s = None
    for i in range(DEPTH):
        j = i // 2
        if i % 2 == 0:
            w_in = _even_w_in(w_in_even[j])
            if x is None:
                zin = norm_matmul(xp, norm_mix[i], w_in, 512, 0, n_tot)
                zin = norm_matmul(xs, norm_mix[i], w_in, 512, n_p, n_tot, zin)
            else:
                zin = norm_matmul(x, norm_mix[i], w_in, 512)
            small = [t[j] for t in even_small]
            gates_t = zin[:, EVEN_G_OFF:EVEN_G_OFF + 2 * A_HEADS].reshape(
                -1, MLSTM_ROWS, 2 * A_HEADS).transpose(0, 2, 1)
            ha, hb_p, g, sp = even_mixer(zin, gates_t, 0, bp, Lp, (None, None), j, mC_p, *zeros_even, *small)
            ha, hb_s, g, ss = even_mixer(zin, gates_t, n_p, bs, Ls, (ha, g), j, mC_s, state_mlstm_C,
                                         state_mlstm_n[j], state_mlstm_m[j], state_rwkv_S[j], state_rwkv_shift[j],
                                         *small)
            mC_p, mC_s = sp[0], ss[0]
            st_p_even.append(sp)
            st_s_even.append(ss)
            wo = w_out_even[j].astype(BF16)
            terms_p = [(ha, None, wo[:A_V], None), (hb_p, g, wo[A_V:], Lp)]
            terms_s = [(ha, None, wo[:A_V], None), (hb_s, g, wo[A_V:], 0)]
            if x is None:
                xo = matmul_res(terms_p, xp, 0, n_p, None, n_tot)
                x = matmul_res(terms_s, xs, n_p, n_s, xo, n_tot)
            else:
                xo = matmul_res(terms_p, x, 0, n_p)
                x = matmul_res(terms_s, x, n_p, n_s, xo)
        else:
            zin = norm_matmul(x, norm_mix[i], _odd_w_in(w_in_odd[j]), 768)
            small = [t[j] for t in odd_small]
            mix, ssm_p, cv_p = ssd_mixer(zin, 0, bp, Lp, None, j, ssm_p, *zeros_odd, *small)
            mix, ssm_s, cv_s = ssd_mixer(zin, n_p, bs, Ls, mix, j, ssm_s, state_ssm, state_conv[j], *small)
            st_p_odd.append(cv_p)
            st_s_odd.append(cv_s)
            x = matmul_res([(mix, None, w_out_odd[j].astype(BF16), None)], x)
        tail = (norm_ffn[i], w_ffn_up[i].astype(BF16), w_ffn_down[i].astype(BF16),
                norm_ple[i], w_ple_gate[i].astype(BF16))
        wp = w_ple_proj[i].astype(BF16)
        xo = ffn_ple(x, *tail, pp[i], wp, 0)
        x = ffn_ple(x, *tail, ps[i], wp, n_p, xo)
    y_prompt = final_norm(x, norm_final, 0, n_p).reshape(bp, Lp, D_MODEL)
    y_sample = final_norm(x, norm_final, n_p, n_s).reshape(bs, Ls, D_MODEL)
    stack = lambda sts, idx: jnp.stack([s[idx] for s in sts])
    ssm_shape = lambda b: (N_ODD, b, C_HEADS, C_HEADDIM, C_STATE)
    return (y_prompt, y_sample,
            mC_p, stack(st_p_even, 1), stack(st_p_even, 2), stack(st_p_even, 3),
            stack(st_p_even, 4), ssm_p.reshape(ssm_shape(bp)), jnp.stack(st_p_odd),
            mC_s, stack(st_s_even, 1), stack(st_s_even, 2), stack(st_s_even, 3),
            stack(st_s_even, 4), ssm_s.reshape(ssm_shape(bs)), jnp.stack(st_s_odd))
```

```python
import math
import functools
import jax
import jax.numpy as jnp
from jax import lax
from jax.experimental import pallas as pl
from jax.experimental.pallas import tpu as pltpu

D_MODEL = 1024
DEPTH = 4
F32 = jnp.float32
BF16 = jnp.bfloat16
EPS = 1e-6
N_EVEN = (DEPTH + 1) // 2
N_ODD = DEPTH // 2
D_FF = 4 * D_MODEL
PLE_DIM = 256

A_HEADS = 4
A_DK = D_MODEL // 8
A_DV = D_MODEL // 8
A_QK = A_HEADS * A_DK
A_V = A_HEADS * A_DV
A_COLS = 2 * A_QK + 2 * A_V + 2 * A_HEADS

B_HEADS = 8
B_DH = 64
B_W = B_HEADS * B_DH
B_W_RANK = 64
B_A_RANK = 64
B_G_RANK = 128
B_COLS = 3 * B_W + B_W_RANK + B_A_RANK + B_G_RANK
B_DECAY_OFFSET = 0.5
B_GN_EPS = 64e-5

EVEN_COLS = A_COLS + B_COLS
EVEN_OUT = A_V + B_W

C_INNER = 2 * D_MODEL
C_HEADDIM = 64
C_HEADS = C_INNER // C_HEADDIM
C_GROUPS = 4
C_HPG = C_HEADS // C_GROUPS
C_STATE = 128
C_CONV = 4
C_CHUNK = 128
C_CONV_DIM = C_INNER + 2 * C_GROUPS * C_STATE
ODD_COLS = C_INNER + C_CONV_DIM + C_HEADS

LANES = 128
SUBLANES = 8
VMEM_LIMIT = 56 * 1024 * 1024
SCAN_VMEM_LIMIT = 60 * 1024 * 1024
TOKEN_TILE = 512

EVEN_GATE_PAD = 256
EVEN_N = B_COLS + EVEN_GATE_PAD + 2 * A_QK + 2 * A_V
EVEN_G_OFF = B_COLS
EVEN_A_OFF = B_COLS + EVEN_GATE_PAD
ODD_DT_PAD = 256
ODD_N = C_INNER + C_CONV_DIM + ODD_DT_PAD
ODD_DT_OFF = C_INNER + C_CONV_DIM


def _cparams(sem, vmem_limit=VMEM_LIMIT):
    return pltpu.CompilerParams(dimension_semantics=sem, vmem_limit_bytes=vmem_limit)


def _rms(x, g):
    return x * lax.rsqrt(jnp.mean(x * x, axis=-1, keepdims=True) + EPS) * g


def _resident(shape):
    nd = len(shape)
    return pl.BlockSpec(shape, lambda *_: (0,) * nd, pipeline_mode=pl.Buffered(1))


def _rows(width):
    return pl.BlockSpec((TOKEN_TILE, width), lambda i: (i, 0))


def _fill_into(body, n_in, into, out_idx, into2=None, out_idx2=None):
    pairs = [(a, o) for a, o in ((into, out_idx), (into2, out_idx2)) if a is not None]
    if not pairs:
        return body, [], [], {}

    def skipping(*refs):
        return body(*refs[:n_in], *refs[n_in + len(pairs):])

    return (skipping, [pl.BlockSpec(memory_space=pl.ANY)] * len(pairs), [a for a, _ in pairs],
            {n_in + i: o for i, (_, o) in enumerate(pairs)})


def _norm_matmul_kernel(x_ref, g_ref, w_ref, o_ref, *, tn):
    xn = _rms(x_ref[...], g_ref[...]).astype(BF16)
    for n0 in range(0, w_ref.shape[1], tn):
        o_ref[:, n0:n0 + tn] = jnp.dot(xn, w_ref[:, n0:n0 + tn], preferred_element_type=F32)


def norm_matmul(x, g, w, tn, row0=0, total=None, into=None):
    m, n = x.shape[0], w.shape[1]
    total = m if total is None else total
    t0 = row0 // TOKEN_TILE
    in_specs = [_rows(D_MODEL), _resident((1, D_MODEL)), _resident(w.shape)]
    body, xspecs, xargs, aliases = _fill_into(functools.partial(_norm_matmul_kernel, tn=tn), len(in_specs), into, 0)
    return pl.pallas_call(
        body,
        grid=(m // TOKEN_TILE,),
        in_specs=in_specs + xspecs,
        out_specs=pl.BlockSpec((TOKEN_TILE, n), lambda i: (t0 + i, 0)),
        out_shape=jax.ShapeDtypeStruct((total, n), F32),
        input_output_aliases=aliases,
        compiler_params=_cparams(("arbitrary",)),
        name="norm_matmul",
    )(x, g.reshape(1, D_MODEL), w, *xargs)


FFN_CHUNK = 512


def _matmul_res_kernel(*refs, kinds):
    x_ref, o_ref = refs[-2], refs[-1]
    acc = x_ref[...]
    pos = 0
    for has_gate, channel_major in kinds:
        a = refs[pos][...]
        if channel_major:
            a = a.T
        if has_gate:
            a = a * refs[pos + 1][...]
        w_ref = refs[pos + 1 + has_gate]
        pos += 2 + has_gate
        acc = acc + jnp.dot(a.astype(BF16), w_ref[...], preferred_element_type=F32)
    o_ref[...] = acc


def matmul_res(terms, x, row0=0, nrows=None, into=None, total=None):
    m = x.shape[0] if total is None else total
    nrows = m if nrows is None else nrows
    t0 = row0 // TOKEN_TILE
    rows = lambda width: pl.BlockSpec((TOKEN_TILE, width), lambda i: (t0 + i, 0))
    xspec = rows(D_MODEL) if total is None else _rows(D_MODEL)
    specs, args, kinds = [], [], []
    for a, gate, w, seq_len in terms:
        if seq_len is None:
            specs.append(rows(a.shape[1]))
        elif seq_len == 0:
            specs.append(pl.BlockSpec((TOKEN_TILE, a.shape[1]), lambda i: (i, 0)))
        else:
            assert seq_len % TOKEN_TILE == 0
            per_seq = seq_len // TOKEN_TILE
            specs.append(pl.BlockSpec((None, a.shape[1], TOKEN_TILE), lambda i: (i // per_seq, 0, i % per_seq)))
        specs += ([rows(w.shape[0])] if gate is not None else []) + [_resident(w.shape)]
        args += [a] + ([gate] if gate is not None else []) + [w]
        kinds.append((int(gate is not None), bool(seq_len)))
    specs.append(xspec)
    body, xspecs, xargs, aliases = _fill_into(
        functools.partial(_matmul_res_kernel, kinds=tuple(kinds)), len(specs), into, 0)
    return pl.pallas_call(
        body,
        grid=(nrows // TOKEN_TILE,),
        in_specs=specs + xspecs,
        out_specs=rows(D_MODEL),
        out_shape=jax.ShapeDtypeStruct((m, D_MODEL), F32),
        input_output_aliases=aliases,
        compiler_params=_cparams(("arbitrary",)),
        name="matmul_res",
    )(*args, x, *xargs)


def _ffn_ple_kernel(x_ref, gf_ref, wu_ref, wd_ref, gp_ref, wg_ref, p_ref, wp_ref, *rest):
    gl_ref = rest[0] if len(rest) == 3 else None
    o_ref, y_s = rest[-2:]
    x = x_ref[...]
    xn = _rms(x, gf_ref[...]).astype(BF16)
    y_s[...] = x
    for c0 in range(0, D_FF, FFN_CHUNK):
        h = jnp.dot(xn, wu_ref[:, c0:c0 + FFN_CHUNK], preferred_element_type=F32)
        h = jnp.square(jnp.maximum(h, 0.0)).astype(BF16)
        y_s[...] += jnp.dot(h, wd_ref[c0:c0 + FFN_CHUNK, :], preferred_element_type=F32)
    y = y_s[...]
    yn = _rms(y, gp_ref[...]).astype(BF16)
    gate = jax.nn.sigmoid(jnp.dot(yn, wg_ref[...], preferred_element_type=F32))
    proj = jnp.dot(p_ref[...].astype(BF16), wp_ref[...], preferred_element_type=F32)
    out = y + proj * gate
    o_ref[...] = out if gl_ref is None else _rms(out, gl_ref[...])


def ffn_ple(x, g_ffn, wu, wd, g_ple, wg, p, wp, row0, into=None, g_last=None):
    t0 = row0 // TOKEN_TILE
    rows = pl.BlockSpec((TOKEN_TILE, D_MODEL), lambda i: (t0 + i, 0))
    row1 = lambda v: v.reshape(1, D_MODEL)
    in_specs = [rows, _resident((1, D_MODEL)), _resident(wu.shape), _resident(wd.shape),
                _resident((1, D_MODEL)), _resident(wg.shape), _rows(PLE_DIM), _resident(wp.shape)]
    args = [x, row1(g_ffn), wu, wd, row1(g_ple), wg, p, wp]
    if g_last is not None:
        assert into is None
        in_specs.append(_resident((1, D_MODEL)))
        args.append(row1(g_last))
    body, xspecs, xargs, aliases = _fill_into(_ffn_ple_kernel, len(in_specs), into, 0)
    return pl.pallas_call(
        body,
        grid=(p.shape[0] // TOKEN_TILE,),
        in_specs=in_specs + xspecs,
        out_specs=rows if g_last is None else _rows(D_MODEL),
        out_shape=jax.ShapeDtypeStruct(x.shape if g_last is None else (p.shape[0], D_MODEL), F32),
        input_output_aliases=aliases,
        scratch_shapes=[pltpu.VMEM((TOKEN_TILE, D_MODEL), F32)],
        compiler_params=_cparams(("arbitrary",)),
        name="ffn_ple",
    )(*args, *xargs)


def _rwkv_operands(k, a, r, kk_t, ka_t, rk_t):
    kk = k * kk_t
    kk = kk * lax.rsqrt(jnp.maximum(jnp.sum(kk * kk, axis=1, keepdims=True), 1e-24))
    k2 = k * (1.0 + (a - 1.0) * ka_t)
    return -kk, kk * a, k2, jnp.sum(r * k2 * rk_t, axis=1)


def _rwkv_step(s_ref, tile0, ntiles, r, w, k, a, b, vrows):
    ys = []
    for lt in range(ntiles):
        rows = slice((tile0 + lt) * B_DH, (tile0 + lt + 1) * B_DH)
        s = s_ref[rows, :]
        sa = jnp.sum(s * a, axis=0, keepdims=True)
        s = s * w + sa * b + vrows[lt:lt + 1] * k
        s_ref[rows, :] = s
        ys.append(jnp.sum(s * r, axis=0, keepdims=True))
    return jnp.concatenate(ys, axis=0)


LONG_T = 128
LONG_NLT = B_DH // 2
KEY_PITCH = B_DH + SUBLANES
VAL_PITCH = LONG_NLT + SUBLANES


def _rwkv_scan_long_kernel(r_ref, w_ref, k_ref, a_ref, v_ref, kkt_ref, kat_ref, rkt_ref, lnw_ref, lnb_ref, s0_ref,
                           yt_ref, s_ref, or_s, ow_s, ok_s, oa_s, av_s, v_s, y_s):
    nb = yt_ref.shape[0]

    @pl.when(pl.program_id(0) == 0)
    def _():
        s_ref[...] = s0_ref[...]
        for scr in (or_s, ow_s, ok_s, oa_s, av_s, v_s, y_s):
            scr[...] = jnp.zeros(scr.shape, F32)

    lo = lax.broadcasted_iota(jnp.int32, (LONG_T, LANES), 1) < LANES // 2

    def tile(ref, c):
        m = ref[c]
        return jnp.concatenate([m, m], axis=0).T

    def relayout(c, carry):
        for ref, dst in ((r_ref, or_s), (w_ref, ow_s), (k_ref, ok_s), (a_ref, oa_s)):
            dst[pl.ds(c, LONG_T, stride=KEY_PITCH), :] = tile(ref, c)
            dst[pl.ds(c + LONG_NLT, LONG_T, stride=KEY_PITCH), :] = tile(ref, c + LONG_NLT)
        v_s[pl.ds(c, LONG_T, stride=VAL_PITCH), :] = jnp.where(lo, tile(v_ref, c), tile(v_ref, c + LONG_NLT))
        return carry

    lax.fori_loop(0, LONG_NLT, relayout, 0)

    def unpad(scr, pitch, n):
        return scr[...].reshape(LONG_T, pitch, LANES)[:, :n]

    def pad(x, pitch):
        zeros = jnp.zeros((LONG_T, pitch - x.shape[1], LANES), F32)
        return jnp.concatenate([x, zeros], axis=1).reshape(LONG_T * pitch, LANES)

    av, bv, k2, bonus = _rwkv_operands(unpad(ok_s, KEY_PITCH, B_DH), unpad(oa_s, KEY_PITCH, B_DH),
                                       unpad(or_s, KEY_PITCH, B_DH), kkt_ref[...], kat_ref[...], rkt_ref[...])
    av_s[...] = pad(av, KEY_PITCH)
    oa_s[...] = pad(bv, KEY_PITCH)
    ok_s[...] = pad(k2, KEY_PITCH)

    def step(t, carry):
        kr = pl.ds(pl.multiple_of(t * KEY_PITCH, SUBLANES), B_DH)
        vr = pl.ds(pl.multiple_of(t * VAL_PITCH, SUBLANES), LONG_NLT)
        y_s[vr, :] = _rwkv_step(s_ref, 0, LONG_NLT, or_s[kr, :], ow_s[kr, :], ok_s[kr, :], av_s[kr, :],
                                oa_s[kr, :], v_s[vr, :])
        return carry

    lax.fori_loop(0, LONG_T, step, 0)

    def head_sum(x):
        tot = jnp.sum(x, axis=1)
        return tot + pltpu.roll(tot, LANES // 2, axis=1)

    y = unpad(y_s, VAL_PITCH, LONG_NLT)
    d = y - (head_sum(y) * (1.0 / B_DH))[:, None, :]
    var = head_sum(d * d) * (1.0 / B_DH)
    y = (d * lax.rsqrt(var + B_GN_EPS)[:, None, :] * lnw_ref[...] + lnb_ref[...]
         + bonus[:, None, :] * unpad(v_s, VAL_PITCH, LONG_NLT))
    y_s[...] = pad(y, VAL_PITCH)

    def relayout_out(lt, carry):
        yt = y_s[pl.ds(lt, LONG_T, stride=VAL_PITCH), :].T
        for i2 in range(2):
            for b in range(nb):
                row = i2 * (LANES // 2) + b * B_HEADS
                yt_ref[b, pl.ds(lt + LONG_NLT * i2, B_HEADS, stride=B_DH), :] = yt[row:row + B_HEADS, :]
        return carry

    lax.fori_loop(0, LONG_NLT, relayout_out, 0)


def rwkv_scan_long(rc, wc, kc, ac, vc, S0, k_k, k_a, r_k, ln_w, ln_b):
    _, bh, L = rc.shape
    b = bh // B_HEADS
    assert bh * 2 == LANES and L % LONG_T == 0
    per_key = lambda p: jnp.tile(p.reshape(B_HEADS, B_DH).T, (1, LANES // B_HEADS))
    per_val = lambda p: jnp.concatenate(
        [jnp.tile(p.reshape(B_HEADS, 2, LONG_NLT)[:, i2].T, (1, b)) for i2 in range(2)], axis=1)
    s0 = S0.reshape(b, B_HEADS, 2, LONG_NLT, B_DH).transpose(3, 4, 2, 0, 1).reshape(LONG_NLT * B_DH, LANES)
    blk = pl.BlockSpec((B_DH, bh, LONG_T), lambda c: (0, 0, c))
    oblk = pl.BlockSpec((b, B_W, LONG_T), lambda c: (0, 0, c))
    const = lambda shape: pl.BlockSpec(shape, lambda c: (0,) * len(shape))
    big = pltpu.VMEM((LONG_T * KEY_PITCH, LANES), F32)
    small = pltpu.VMEM((LONG_T * VAL_PITCH, LANES), F32)
    yt, s = pl.pallas_call(
        _rwkv_scan_long_kernel,
        grid=(L // LONG_T,),
        in_specs=[blk] * 5 + [const((B_DH, LANES))] * 3 + [const((LONG_NLT, LANES))] * 2
        + [_resident((LONG_NLT * B_DH, LANES))],
        out_specs=[oblk, const((LONG_NLT * B_DH, LANES))],
        out_shape=[jax.ShapeDtypeStruct((b, B_W, L), F32), jax.ShapeDtypeStruct(s0.shape, F32)],
        scratch_shapes=[big] * 5 + [small, small],
        compiler_params=_cparams(("arbitrary",), SCAN_VMEM_LIMIT),
        name="rwkv_scan_long",
    )(rc, wc, kc, ac, vc, per_key(k_k), per_key(k_a), per_key(r_k), per_val(ln_w), per_val(ln_b), s0)
    s = s.reshape(LONG_NLT, B_DH, 2, b, B_HEADS).transpose(3, 4, 2, 0, 1).reshape(b, B_HEADS, B_DH, B_DH)
    return yt, s


def _rwkv_scan_short_kernel(r_ref, w_ref, k_ref, a_ref, v_ref, kkt_ref, kat_ref, rkt_ref, lnw_ref, lnb_ref, s0_ref,
                            y_ref, sout_ref, s_s, or_s, ow_s, ok_s, oa_s, av_s, ov_s, y_s, *, L):
    nt = 2 * B_DH
    for q in range(nt * B_DH // LANES):
        s_s[q * LANES:(q + 1) * LANES, :] = s0_ref[:, q * LANES:(q + 1) * LANES].T
    for ref, dst in ((r_ref, or_s), (w_ref, ow_s), (k_ref, ok_s), (a_ref, oa_s), (v_ref, ov_s)):
        for t in range(L):
            dst[t] = ref[pl.ds(t, LANES, stride=L), :].T.reshape(2, B_DH, LANES)
    shape3 = (L * 2, B_DH, LANES)
    tiles = lambda ref: jnp.concatenate([ref[...]] * L, axis=0)
    av, bv, k2, bonus = _rwkv_operands(ok_s[...].reshape(shape3), oa_s[...].reshape(shape3),
                                       or_s[...].reshape(shape3), tiles(kkt_ref), tiles(kat_ref), tiles(rkt_ref))
    av_s[...] = av.reshape(L, 2, B_DH, LANES)
    oa_s[...] = bv.reshape(L, 2, B_DH, LANES)
    ok_s[...] = k2.reshape(L, 2, B_DH, LANES)

    def step(t, carry):
        for h in range(2):
            y_s[t, h] = _rwkv_step(s_s, h * B_DH, B_DH, or_s[t, h], ow_s[t, h], ok_s[t, h], av_s[t, h],
                                   oa_s[t, h], ov_s[t, h])
        return carry

    lax.fori_loop(0, L, step, 0)

    y = y_s[...]
    d = y - jnp.mean(y, axis=2, keepdims=True)
    var = jnp.mean(d * d, axis=2, keepdims=True)
    y = (d * lax.rsqrt(var + B_GN_EPS) * lnw_ref[...] + lnb_ref[...]
         + bonus.reshape(L, 2, 1, LANES) * ov_s[...])
    for t in range(L):
        y_ref[pl.ds(t, LANES, stride=L), :] = y[t].reshape(nt, LANES).T
    for q in range(nt * B_DH // LANES):
        sout_ref[:, q * LANES:(q + 1) * LANES] = s_s[q * LANES:(q + 1) * LANES, :].T


def rwkv_scan_short(r, w, k, a, v, S0, L, k_k, k_a, r_k, ln_w, ln_b):
    n = r.shape[0]
    b = n // L
    assert b == LANES
    npair = B_HEADS // 2
    wide = lambda p: jnp.broadcast_to(p.reshape(npair, 2, B_DH, 1), (npair, 2, B_DH, LANES))
    s0 = S0.reshape(b, B_HEADS * B_DH * B_DH)
    blk = pl.BlockSpec((n, LANES), lambda p: (0, p))
    cblk = pl.BlockSpec((None, 2, B_DH, LANES), lambda p: (p, 0, 0, 0))
    sblk = pl.BlockSpec((b, 2 * B_DH * B_DH), lambda p: (0, p))
    op = pltpu.VMEM((L, 2, B_DH, LANES), F32)
    y, s = pl.pallas_call(
        functools.partial(_rwkv_scan_short_kernel, L=L),
        grid=(npair,),
        in_specs=[blk] * 5 + [cblk] * 5 + [sblk],
        out_specs=[blk, sblk],
        out_shape=[jax.ShapeDtypeStruct((n, B_W), F32), jax.ShapeDtypeStruct(s0.shape, F32)],
        scratch_shapes=[pltpu.VMEM((2 * B_DH * B_DH, LANES), F32)] + [op] * 7,
        compiler_params=_cparams(("arbitrary",)),
        name="rwkv_scan_short",
    )(r, w, k, a, v, wide(k_k), wide(k_a), wide(r_k), wide(ln_w), wide(ln_b), s0)
    return y, s.reshape(b, B_HEADS, B_DH, B_DH)


RWKV_PREP_ROWS = 256
RWKV_SHORT_ROWS = 64
B_LORA_OFF = 3 * B_W


def _rwkv_prep_kernel(z_ref, sh0_ref, mu_ref, wwa_ref, g2_ref, w0_ref, a0_ref,
                      r_ref, w_ref, k_ref, a_ref, v_ref, g_ref, sh_ref, *scratch, T, nseq, channel_major):
    @pl.when(pl.program_id(1) == 0)
    def _():
        sh_ref[...] = sh0_ref[...]

    z = z_ref[...]
    rowid = lax.broadcasted_iota(jnp.int32, (z.shape[0], 1), 0)
    zprev = pltpu.roll(z, 1, axis=0)
    for u in range(nseq):
        zprev = jnp.where(rowid == u * T, sh_ref[u], zprev)
    for u in range(nseq):
        sh_ref[u] = z[(u + 1) * T - 1:(u + 1) * T, :]
    zs = z + (zprev - z) * mu_ref[...]
    r = zs[:, :B_W]
    k = zs[:, B_W:2 * B_W]
    lora = zs[:, B_LORA_OFF:B_LORA_OFF + LANES]
    lane = lax.broadcasted_iota(jnp.int32, lora.shape, 1)
    lora = jnp.where(lane < B_W_RANK, jnp.tanh(lora), lora).astype(BF16)
    wa = jnp.dot(lora, wwa_ref[...], preferred_element_type=F32)
    w_log = -jax.nn.softplus(-(w0_ref[...] + wa[:, :B_W])) - B_DECAY_OFFSET
    a = jax.nn.sigmoid(a0_ref[...] + wa[:, B_W:])
    zg = zs[:, B_LORA_OFF + LANES:B_LORA_OFF + LANES + B_G_RANK]
    def emit(ref, x):
        if not channel_major:
            ref[...] = x
            return
        xt_s = scratch[0]
        xt = x.T
        for j in range(xt_s.shape[0]):
            for h in range(B_HEADS):
                xt_s[j, h * KEY_PITCH:h * KEY_PITCH + B_DH, :] = xt[h * B_DH:(h + 1) * B_DH, j * LANES:(j + 1) * LANES]
        for c in range(B_DH):
            for j in range(xt_s.shape[0]):
                ref[c, :, j * LANES:(j + 1) * LANES] = xt_s[j, pl.ds(c, B_HEADS, stride=KEY_PITCH), :]

    emit(r_ref, r)
    emit(w_ref, jnp.exp(-jnp.exp(w_log)))
    emit(k_ref, k)
    emit(a_ref, a)
    emit(v_ref, zs[:, 2 * B_W:3 * B_W])
    g_ref[...] = jnp.dot(jax.nn.sigmoid(zg).astype(BF16), g2_ref[...], preferred_element_type=F32)


def rwkv_prep(zin, row0, b, L, gate_into, shift0, mu, w0, w2, a0, a2, g2):
    assert B_W_RANK + B_A_RANK == LANES
    T = math.gcd(L, RWKV_PREP_ROWS)
    channel_major = T == RWKV_PREP_ROWS
    nseq = 1 if channel_major else RWKV_SHORT_ROWS // T
    assert nseq == 1 or (L == T and b % nseq == 0)
    R = T * nseq
    nchunk = L // T
    blk0 = row0 // R
    wwa = jnp.zeros((LANES, 2 * B_W), F32).at[:B_W_RANK, :B_W].set(w2).at[B_W_RANK:, B_W:].set(a2).astype(BF16)
    row = lambda v: v.reshape(1, -1)
    if channel_major:
        blk = pl.BlockSpec((B_DH, B_HEADS, R), lambda i, c: (0, i, c))
        oshape = jax.ShapeDtypeStruct((B_DH, b * B_HEADS, L), F32)
    else:
        blk = pl.BlockSpec((R, B_W), lambda i, c: (i * nchunk + c, 0))
        oshape = jax.ShapeDtypeStruct((b * L, B_W), F32)
    gblk = pl.BlockSpec((R, B_W), lambda i, c: (blk0 + i * nchunk + c, 0))
    shspec = pl.BlockSpec((nseq, 1, B_COLS), lambda i, c: (i, 0, 0))
    const = lambda shape: pl.BlockSpec(shape, lambda i, c: (0,) * len(shape))
    in_specs = [pl.BlockSpec((R, B_COLS), lambda i, c: (blk0 + i * nchunk + c, 0)), shspec, const((1, B_COLS)),
                const((LANES, 2 * B_W)), const((B_G_RANK, B_W)), const((1, B_W)), const((1, B_W))]
    body, xspecs, xargs, aliases = _fill_into(
        functools.partial(_rwkv_prep_kernel, T=T, nseq=nseq, channel_major=channel_major),
        len(in_specs), gate_into, 5)
    outs = pl.pallas_call(
        body,
        grid=(b // nseq, nchunk),
        in_specs=in_specs + xspecs,
        out_specs=[blk] * 5 + [gblk, shspec],
        out_shape=[oshape] * 5
        + [jax.ShapeDtypeStruct((zin.shape[0], B_W), F32), jax.ShapeDtypeStruct((b, 1, B_COLS), F32)],
        input_output_aliases=aliases,
        scratch_shapes=[pltpu.VMEM((R // LANES, B_HEADS * KEY_PITCH, LANES), F32)] if channel_major else [],
        compiler_params=_cparams(("arbitrary", "arbitrary")),
        name="rwkv_prep",
    )(zin, shift0.reshape(b, 1, B_COLS), row(mu), wwa, g2.astype(BF16), row(w0), row(a0), *xargs)
    return outs[:5], outs[5], outs[6].reshape(b, B_COLS)


MLSTM_ROWS = 128
HIGHEST = lax.Precision.HIGHEST


def _mlstm_kernel(q_ref, k_ref, v_ref, o_ref, g_ref, gt_ref, brow_ref, bcol_ref, nw_ref,
                  c0_ref, n0_ref, m0_ref, h_ref, c_ref, n_ref, m_ref, *, T, nseq):
    R = MLSTM_ROWS

    @pl.when(pl.program_id(1) == 0)
    def _():
        c_ref[...] = c0_ref[...]
        n_ref[...] = n0_ref[...]
        m_ref[...] = m0_ref[...]

    shift = T.bit_length() - 1
    ri = lax.broadcasted_iota(jnp.int32, (R, R), 0)
    ci = lax.broadcasted_iota(jnp.int32, (R, R), 1)
    mask = (ci <= ri) & (jnp.right_shift(ri, shift) == jnp.right_shift(ci, shift))
    lmat = mask.astype(F32)
    rowid = lax.broadcasted_iota(jnp.int32, (R, 1), 0)
    rsel = [(rowid >= u * T) & (rowid < (u + 1) * T) for u in range(nseq)]

    g = g_ref[...] + brow_ref[...]
    lane = lax.broadcasted_iota(jnp.int32, g.shape, 1)
    glog = jnp.where((lane >= A_HEADS) & (lane < 2 * A_HEADS), jax.nn.log_sigmoid(g), g)
    gt = gt_ref[...] + bcol_ref[...]
    sub = lax.broadcasted_iota(jnp.int32, gt.shape, 0)
    gtlog = jnp.where(sub >= A_HEADS, jax.nn.log_sigmoid(gt), gt)
    bc_col = jnp.dot(lmat, glog, precision=HIGHEST, preferred_element_type=F32)
    bc_row = lax.dot_general(gtlog, lmat, (((1,), (1,)), ((), ())), precision=HIGHEST,
                             preferred_element_type=F32)
    lane_m = lax.broadcasted_iota(jnp.int32, (1, LANES), 1)
    m_old = [m_ref[u] for u in range(nseq)]
    m_out = [jnp.zeros((1, LANES), F32) for _ in range(nseq)]

    for h in range(A_HEADS):
        hs = slice(h * A_DK, (h + 1) * A_DK)
        bcc = bc_col[:, A_HEADS + h:A_HEADS + h + 1]
        bcr = bc_row[A_HEADS + h:A_HEADS + h + 1, :]
        lir = gtlog[h:h + 1, :]
        lic = glog[:, h:h + 1]
        m_u = [m_old[u][:, h:h + 1] for u in range(nseq)]
        m_col = m_u[0]
        for u in range(1, nseq):
            m_col = jnp.where(rsel[u], m_u[u], m_col)
        dmat = jnp.where(mask, bcc - bcr + lir, -jnp.inf)
        inter = bcc + m_col
        mt = jnp.maximum(inter, jnp.max(dmat, axis=1, keepdims=True))
        p = jnp.exp(dmat - mt)
        qh = q_ref[:, hs] * (A_DK ** -0.5)
        kh = k_ref[:, hs]
        qb, kb, vb = qh.astype(BF16), kh.astype(BF16), v_ref[:, hs].astype(BF16)
        wq = lax.dot_general(qb, kb, (((1,), (1,)), ((), ())), preferred_element_type=F32) * p
        wi = jnp.exp(inter - mt)
        c_old = [c_ref[u, h] for u in range(nseq)]
        n_old = [n_ref[u, h:h + 1, :] for u in range(nseq)]
        qc = jnp.dot(qb, c_old[0].astype(BF16), preferred_element_type=F32)
        qn = jnp.sum(qh * n_old[0], axis=1, keepdims=True)
        for u in range(1, nseq):
            qc = jnp.where(rsel[u], jnp.dot(qb, c_old[u].astype(BF16), preferred_element_type=F32), qc)
            qn = jnp.where(rsel[u], jnp.sum(qh * n_old[u], axis=1, keepdims=True), qn)
        num = jnp.dot(wq.astype(BF16), vb, preferred_element_type=F32) + wi * qc
        den = jnp.sum(wq, axis=1, keepdims=True) + wi * qn
        hh = num / jnp.maximum(jnp.abs(den), jnp.exp(-mt))
        hh = hh * lax.rsqrt(jnp.mean(hh * hh, axis=-1, keepdims=True) + EPS)
        h_ref[:, hs] = hh * nw_ref[:, hs] * jax.nn.sigmoid(o_ref[:, hs])
        for u in range(nseq):
            b_last = bcc[(u + 1) * T - 1:(u + 1) * T, :]
            gs = b_last - bcc + lic
            gmax = jnp.max(gs if nseq == 1 else jnp.where(rsel[u], gs, -jnp.inf), axis=0, keepdims=True)
            m_new = jnp.maximum(b_last + m_u[u], gmax)
            decay = jnp.exp(b_last + m_u[u] - m_new)
            ws = jnp.exp(gs - m_new)
            if nseq > 1:
                ws = jnp.where(rsel[u], ws, 0.0)
            kw = kh * ws
            c_ref[u, h] = decay * c_old[u] + lax.dot_general(
                kw.astype(BF16), vb, (((0,), (0,)), ((), ())), preferred_element_type=F32)
            n_ref[u, h:h + 1, :] = decay * n_old[u] + jnp.sum(kw, axis=0, keepdims=True)
            m_out[u] = jnp.where(lane_m == h, m_new, m_out[u])
    for u in range(nseq):
        m_ref[u] = m_out[u]


def mlstm(zin, gates_t, row0, b, L, h_into, layer, c_into, b_i, b_f, m_norm, C0, n0, m0):
    R = MLSTM_ROWS
    T = math.gcd(L, R)
    nseq = R // T
    assert nseq == 1 or (L == T and b % nseq == 0)
    nchunk = L // T
    blk0 = row0 // R
    bias = jnp.concatenate([b_i, b_f])
    bias_row = jnp.zeros((1, LANES), F32).at[0, :2 * A_HEADS].set(bias)
    bias_col = bias.reshape(2 * A_HEADS, 1)
    m0p = jnp.zeros((b, 1, LANES), F32).at[:, 0, :A_HEADS].set(m0)
    blk = lambda i, c: blk0 + i * nchunk + c
    rowblk = lambda col: pl.BlockSpec((R, A_QK), lambda i, c: (blk(i, c), EVEN_A_OFF // A_QK + col))
    cspec = pl.BlockSpec((nseq, A_HEADS, A_DK, A_DV), lambda i, c: (i, 0, 0, 0))
    cout = pl.BlockSpec((None, nseq, A_HEADS, A_DK, A_DV), lambda i, c: (layer, i, 0, 0, 0))
    nspec = pl.BlockSpec((nseq, A_HEADS, A_DK), lambda i, c: (i, 0, 0))
    mspec = pl.BlockSpec((nseq, 1, LANES), lambda i, c: (i, 0, 0))
    const = lambda shape: pl.BlockSpec(shape, lambda i, c: (0,) * len(shape))
    in_specs = [rowblk(0), rowblk(1), rowblk(2), rowblk(3),
                pl.BlockSpec((R, LANES), lambda i, c: (blk(i, c), EVEN_G_OFF // LANES)),
                pl.BlockSpec((None, 2 * A_HEADS, R), lambda i, c: (blk(i, c), 0, 0)),
                const((1, LANES)), const((2 * A_HEADS, 1)), const((1, A_V)),
                cspec if C0.ndim == 4 else cout, nspec, mspec]
    body, xspecs, xargs, aliases = _fill_into(
        functools.partial(_mlstm_kernel, T=T, nseq=nseq), len(in_specs), h_into, 0, c_into, 1)
    h, C, n, m = pl.pallas_call(
        body,
        grid=(b // nseq, nchunk),
        in_specs=in_specs + xspecs,
        out_specs=[pl.BlockSpec((R, A_V), lambda i, c: (blk(i, c), 0)), cout, nspec, mspec],
        out_shape=[jax.ShapeDtypeStruct((zin.shape[0], A_V), F32),
                   jax.ShapeDtypeStruct((N_EVEN,) + C0.shape[-4:], F32), jax.ShapeDtypeStruct(n0.shape, F32),
                   jax.ShapeDtypeStruct(m0p.shape, F32)],
        input_output_aliases=aliases,
        compiler_params=_cparams(("arbitrary", "arbitrary")),
        name="mlstm",
    )(zin, zin, zin, zin, zin, gates_t, bias_row, bias_col, m_norm.reshape(1, A_V), C0, n0, m0p, *xargs)
    return h, C, n, m[:, 0, :A_HEADS]


def even_mixer(zin, gates_t, row0, b, L, into, layer, c_into, C0, n0, m0, S0, shift0, b_i, b_f, m_norm,
               mu, w0, w2, a0, a2, g2, k_k, k_a, r_k, ln_w, ln_b):
    hA, C, n, m = mlstm(zin, gates_t, row0, b, L, into[0], layer, c_into, b_i, b_f, m_norm, C0, n0, m0)
    ops, g, shift = rwkv_prep(zin, row0, b, L, into[1], shift0, mu, w0, w2, a0, a2, g2)
    if ops[0].ndim == 3:
        hB, S = rwkv_scan_long(*ops, S0, k_k, k_a, r_k, ln_w, ln_b)
    else:
        hB, S = rwkv_scan_short(*ops, S0, L, k_k, k_a, r_k, ln_w, ln_b)
    return hA, hB, g, (C, n, m, S, shift)


C_PAIRS = C_HEADS // 2
C_GROUP_W = C_INNER // C_GROUPS
C_BC_W = 2 * C_GROUPS * C_STATE
SSD_LONG_ROWS = C_CHUNK
SSD_SHORT_ROWS = 32


def _ssd_kernel(z_ref, x_ref, bc_ref, dt_ref, dtt_ref, cw_ref, cb_ref, dtb_ref, dtbt_ref, al_ref, alt_ref,
                dsk_ref, nw_ref, s0_ref, cv0_ref, y_ref, s_ref, cv_ref, stg_ref, *, T, nseq, nchunk):
    R = T * nseq
    tstate = nchunk > 1
    assert not tstate or nseq == 1

    @pl.when(pl.program_id(1) == 0)
    def _():
        if tstate:
            for pr in range(C_PAIRS):
                s_ref[0, pr] = s0_ref[0, pr].T
        else:
            s_ref[...] = s0_ref[...]
        cv_ref[...] = cv0_ref[...]

    shift = T.bit_length() - 1
    ri = lax.broadcasted_iota(jnp.int32, (R, R), 0)
    ci = lax.broadcasted_iota(jnp.int32, (R, R), 1)
    mask = (ci <= ri) & (jnp.right_shift(ri, shift) == jnp.right_shift(ci, shift))
    lmat = mask.astype(F32)
    rowid = lax.broadcasted_iota(jnp.int32, (R, 1), 0)
    rsel = [(rowid >= u * T) & (rowid < (u + 1) * T) for u in range(nseq)]

    def conv_silu(src_ref, src_col, col):
        cols = slice(col, col + LANES)
        accs = []
        for u in range(nseq):
            stg_ref[u, 0:SUBLANES, cols] = cv_ref[u, :, cols]
            stg_ref[u, SUBLANES:SUBLANES + T, cols] = src_ref[u * T:(u + 1) * T, src_col:src_col + LANES]
            acc = cb_ref[:, cols]
            for d in range(C_CONV):
                acc = acc + stg_ref[u, SUBLANES - d:SUBLANES - d + T, cols] * cw_ref[C_CONV - 1 - d:C_CONV - d, cols]
            accs.append(acc)
            cv_ref[u, :, cols] = stg_ref[u, T:T + SUBLANES, cols]
        acc = accs[0] if nseq == 1 else jnp.concatenate(accs, axis=0)
        return acc * jax.nn.sigmoid(acc)

    dtv = jax.nn.softplus(dt_ref[...] + dtb_ref[...])
    dtt = jax.nn.softplus(dtt_ref[...] + dtbt_ref[...])
    cum_col = jnp.dot(lmat, dtv * (-jnp.exp(al_ref[...])), precision=HIGHEST, preferred_element_type=F32)
    cum_row = lax.dot_general(dtt * (-jnp.exp(alt_ref[...])), lmat, (((1,), (1,)), ((), ())),
                              precision=HIGHEST, preferred_element_type=F32)

    lo = lax.broadcasted_iota(jnp.int32, (R, LANES), 1) < C_HEADDIM
    rlo = lax.broadcasted_iota(jnp.int32, (LANES, 1), 0) < C_HEADDIM
    nt = (((1,), (1,)), ((), ()))
    tn = (((0,), (0,)), ((), ()))
    pairs_per_group = C_PAIRS // C_GROUPS
    for g in range(C_GROUPS):
        bgf = conv_silu(bc_ref, g * C_STATE, C_INNER + g * C_STATE)
        bg = bgf.astype(BF16)
        cg = conv_silu(bc_ref, (C_GROUPS + g) * C_STATE, C_INNER + (C_GROUPS + g) * C_STATE).astype(BF16)
        cbm = lax.dot_general(cg, bg, nt, preferred_element_type=F32)
        ys = None
        if tstate:
            bgt = bgf.T.astype(BF16)
            sgt = jnp.concatenate([s_ref[0, g * pairs_per_group + q] for q in range(pairs_per_group)], axis=1)
            ys = jnp.dot(cg, sgt.astype(BF16), preferred_element_type=F32)
        for u in range(0 if tstate else nseq):
            sg = s_ref[u, g * pairs_per_group:(g + 1) * pairs_per_group].reshape(C_GROUP_W, C_STATE)
            t_u = lax.dot_general(cg, sg.astype(BF16), nt, preferred_element_type=F32)
            ys = t_u if u == 0 else jnp.where(rsel[u], t_u, ys)
        for q in range(pairs_per_group):
            pr = g * pairs_per_group + q
            ps = slice(pr * LANES, (pr + 1) * LANES)
            xp = conv_silu(x_ref, pr * LANES, pr * LANES)
            cc = [cum_col[:, 2 * pr + e:2 * pr + e + 1] for e in range(2)]
            intra = None
            for e, keep in ((0, lo), (1, jnp.logical_not(lo))):
                hh = 2 * pr + e
                seg = jnp.exp(jnp.where(mask, cc[e] - cum_row[hh:hh + 1, :], -jnp.inf))
                mix = cbm * seg * dtt[hh:hh + 1, :]
                part = jnp.dot(mix.astype(BF16), jnp.where(keep, xp, 0.0).astype(BF16),
                               preferred_element_type=F32)
                intra = part if intra is None else intra + part
            scale = jnp.where(lo, jnp.exp(cc[0]), jnp.exp(cc[1]))
            yp = intra + scale * ys[:, q * LANES:(q + 1) * LANES] + dsk_ref[:, ps] * xp
            zp = z_ref[:, ps]
            y_ref[:, ps] = yp * (zp * jax.nn.sigmoid(zp))
            for u in range(nseq):
                last = (u + 1) * T - 1
                ct = [cc[e][last:last + 1, :] for e in range(2)]
                tail = jnp.where(lo, jnp.exp(ct[0] - cc[0]) * dtv[:, 2 * pr:2 * pr + 1],
                                 jnp.exp(ct[1] - cc[1]) * dtv[:, 2 * pr + 1:2 * pr + 2])
                xw = xp * tail
                if nseq > 1:
                    xw = jnp.where(rsel[u], xw, 0.0)
                if tstate:
                    upd = jnp.dot(bgt, xw.astype(BF16), preferred_element_type=F32)
                    dec = jnp.where(lo[:1], jnp.exp(ct[0]), jnp.exp(ct[1]))
                else:
                    upd = lax.dot_general(xw.astype(BF16), bg, tn, preferred_element_type=F32)
                    dec = jnp.where(rlo, jnp.exp(ct[0]), jnp.exp(ct[1]))
                s_ref[u, pr] = dec * s_ref[u, pr] + upd

    if tstate:
        @pl.when(pl.program_id(1) == nchunk - 1)
        def _():
            for pr in range(C_PAIRS):
                s_ref[0, pr] = s_ref[0, pr].T

    for g in range(C_GROUPS):
        gs_ = slice(g * C_GROUP_W, (g + 1) * C_GROUP_W)
        yg = y_ref[:, gs_]
        y_ref[:, gs_] = yg * lax.rsqrt(jnp.mean(yg * yg, axis=-1, keepdims=True) + EPS) * nw_ref[:, gs_]


def ssd_mixer(zin, row0, b, L, y_into, layer, s_into, ssm0, conv0, conv_w, conv_b, dt_bias, a_log, d_skip, norm_w):
    T = math.gcd(L, SSD_LONG_ROWS)
    nseq = 1 if T == SSD_LONG_ROWS else SSD_SHORT_ROWS // T
    assert nseq == 1 or (L == T and b % nseq == 0)
    R = T * nseq
    nchunk = L // T
    nblk = b * L // R
    blk0 = row0 // R
    dt_t = zin[row0:row0 + b * L, ODD_DT_OFF:ODD_DT_OFF + C_HEADS].reshape(nblk, R, C_HEADS).transpose(0, 2, 1)
    pad_row = lambda v: jnp.zeros((1, LANES), F32).at[0, :C_HEADS].set(v)
    pair_shape = (b, C_PAIRS, 2 * C_HEADDIM, C_STATE)
    per_layer = ssm0.ndim == 4
    s0 = ssm0.reshape(pair_shape if per_layer else (ssm0.shape[0],) + pair_shape)
    cv0 = jnp.concatenate([jnp.zeros((b, SUBLANES - (C_CONV - 1), C_CONV_DIM), F32), conv0], axis=1)
    blk = lambda i, c: blk0 + i * nchunk + c
    sspec = pl.BlockSpec((nseq, C_PAIRS, 2 * C_HEADDIM, C_STATE), lambda i, c: (i, 0, 0, 0))
    sout = pl.BlockSpec((None, nseq, C_PAIRS, 2 * C_HEADDIM, C_STATE), lambda i, c: (layer, i, 0, 0, 0))
    cvspec = pl.BlockSpec((nseq, SUBLANES, C_CONV_DIM), lambda i, c: (i, 0, 0))
    const = lambda shape: pl.BlockSpec(shape, lambda i, c: (0,) * len(shape))
    in_specs = [pl.BlockSpec((R, C_INNER), lambda i, c: (blk(i, c), 0)),
                pl.BlockSpec((R, C_INNER), lambda i, c: (blk(i, c), 1)),
                pl.BlockSpec((R, C_BC_W), lambda i, c: (blk(i, c), 2 * C_INNER // C_BC_W)),
                pl.BlockSpec((R, LANES), lambda i, c: (blk(i, c), ODD_DT_OFF // LANES)),
                pl.BlockSpec((None, C_HEADS, R), lambda i, c: (i * nchunk + c, 0, 0)),
                const((C_CONV, C_CONV_DIM)), const((1, C_CONV_DIM)),
                const((1, LANES)), const((C_HEADS, 1)), const((1, LANES)), const((C_HEADS, 1)),
                const((1, C_INNER)), const((1, C_INNER)), sspec if per_layer else sout, cvspec]
    body, xspecs, xargs, aliases = _fill_into(
        functools.partial(_ssd_kernel, T=T, nseq=nseq, nchunk=nchunk), len(in_specs), y_into, 0, s_into, 1)
    y, s, cv = pl.pallas_call(
        body,
        grid=(b // nseq, nchunk),
        in_specs=in_specs + xspecs,
        out_specs=[pl.BlockSpec((R, C_INNER), lambda i, c: (blk(i, c), 0)), sout, cvspec],
        out_shape=[jax.ShapeDtypeStruct((zin.shape[0], C_INNER), F32),
                   jax.ShapeDtypeStruct((N_ODD,) + pair_shape, F32), jax.ShapeDtypeStruct(cv0.shape, F32)],
        input_output_aliases=aliases,
        scratch_shapes=[pltpu.VMEM((nseq, SUBLANES + T, C_CONV_DIM), F32)],
        compiler_params=_cparams(("arbitrary", "arbitrary")),
        name="ssd",
    )(zin, zin, zin, zin, dt_t, conv_w, conv_b.reshape(1, C_CONV_DIM),
      pad_row(dt_bias), dt_bias.reshape(C_HEADS, 1), pad_row(a_log), a_log.reshape(C_HEADS, 1),
      jnp.repeat(d_skip, C_HEADDIM).reshape(1, C_INNER), norm_w.reshape(1, C_INNER), s0, cv0, *xargs)
    return y, s, cv[:, SUBLANES - (C_CONV - 1):]


def _even_w_in(w):
    qkvo = w[:, :2 * A_QK + 2 * A_V]
    gates = w[:, 2 * A_QK + 2 * A_V:A_COLS]
    rwkv = w[:, A_COLS:]
    pad = jnp.zeros((D_MODEL, EVEN_GATE_PAD - 2 * A_HEADS), w.dtype)
    return jnp.concatenate([rwkv, gates, pad, qkvo], axis=1).astype(BF16)


def _odd_w_in(w):
    pad = jnp.zeros((D_MODEL, ODD_DT_PAD - C_HEADS), w.dtype)
    return jnp.concatenate([w, pad], axis=1).astype(BF16)


def kernel(x_prompt, x_sample, state_mlstm_C, state_mlstm_n, state_mlstm_m, state_rwkv_S,
           state_rwkv_shift, state_ssm, state_conv, p_prompt, p_sample,
           norm_mix, norm_ffn, w_ffn_up, w_ffn_down, w_ple_proj, norm_ple, w_ple_gate, norm_final,
           w_in_even, mlstm_b_i, mlstm_b_f, mlstm_norm, rwkv_mu, rwkv_w0, rwkv_w2, rwkv_a0, rwkv_a2,
           rwkv_g2, rwkv_k_k, rwkv_k_a, rwkv_r_k, rwkv_ln_w, rwkv_ln_b, w_out_even,
           w_in_odd, conv_w, conv_b, dt_bias, a_log, d_skip, ssm_norm, w_out_odd):
    bp, Lp, _ = x_prompt.shape
    bs, Ls, _ = x_sample.shape
    n_p, n_s = bp * Lp, bs * Ls
    n_tot = n_p + n_s
    xp = x_prompt.reshape(n_p, D_MODEL)
    xs = x_sample.reshape(n_s, D_MODEL)
    pp = p_prompt.reshape(DEPTH, n_p, PLE_DIM)
    ps = p_sample.reshape(DEPTH, n_s, PLE_DIM)
    x = None

    even_small = (mlstm_b_i, mlstm_b_f, mlstm_norm, rwkv_mu, rwkv_w0, rwkv_w2, rwkv_a0, rwkv_a2,
                  rwkv_g2, rwkv_k_k, rwkv_k_a, rwkv_r_k, rwkv_ln_w, rwkv_ln_b)
    odd_small = (conv_w, conv_b, dt_bias, a_log, d_skip, ssm_norm)

    zeros_even = (jnp.zeros((bp, A_HEADS, A_DK, A_DV), F32), jnp.zeros((bp, A_HEADS, A_DK), F32),
                  jnp.zeros((bp, A_HEADS), F32), jnp.zeros((bp, B_HEADS, B_DH, B_DH), F32),
                  jnp.zeros((bp, B_COLS), F32))
    zeros_odd = (jnp.zeros((bp, C_HEADS, C_HEADDIM, C_STATE), F32),
                 jnp.zeros((bp, C_CONV - 1, C_CONV_DIM), F32))

    st_p_even, st_s_even, st_p_odd, st_s_odd = [], [], [], []
    mC_p = mC_s = ssm_p = ssm_s = None
    for i in range(DEPTH):
        j = i // 2
        if i % 2 == 0:
            w_in = _even_w_in(w_in_even[j])
            if x is None:
                zin = norm_matmul(xp, norm_mix[i], w_in, 512, 0, n_tot)
                zin = norm_matmul(xs, norm_mix[i], w_in, 512, n_p, n_tot, zin)
            else:
                zin = norm_matmul(x, norm_mix[i], w_in, 512)
            small = [t[j] for t in even_small]
            gates_t = zin[:, EVEN_G_OFF:EVEN_G_OFF + 2 * A_HEADS].reshape(
                -1, MLSTM_ROWS, 2 * A_HEADS).transpose(0, 2, 1)
            ha, hb_p, g, sp = even_mixer(zin, gates_t, 0, bp, Lp, (None, None), j, mC_p, *zeros_even, *small)
            ha, hb_s, g, ss = even_mixer(zin, gates_t, n_p, bs, Ls, (ha, g), j, mC_s, state_mlstm_C,
                                         state_mlstm_n[j], state_mlstm_m[j], state_rwkv_S[j], state_rwkv_shift[j],
                                         *small)
            mC_p, mC_s = sp[0], ss[0]
            st_p_even.append(sp)
            st_s_even.append(ss)
            wo = w_out_even[j].astype(BF16)
            terms_p = [(ha, None, wo[:A_V], None), (hb_p, g, wo[A_V:], Lp)]
            terms_s = [(ha, None, wo[:A_V], None), (hb_s, g, wo[A_V:], 0)]
            if x is None:
                xo = matmul_res(terms_p, xp, 0, n_p, None, n_tot)
                x = matmul_res(terms_s, xs, n_p, n_s, xo, n_tot)
            else:
                xo = matmul_res(terms_p, x, 0, n_p)
                x = matmul_res(terms_s, x, n_p, n_s, xo)
        else:
            zin = norm_matmul(x, norm_mix[i], _odd_w_in(w_in_odd[j]), 768)
            small = [t[j] for t in odd_small]
            mix, ssm_p, cv_p = ssd_mixer(zin, 0, bp, Lp, None, j, ssm_p, *zeros_odd, *small)
            mix, ssm_s, cv_s = ssd_mixer(zin, n_p, bs, Ls, mix, j, ssm_s, state_ssm, state_conv[j], *small)
            st_p_odd.append(cv_p)
            st_s_odd.append(cv_s)
            x = matmul_res([(mix, None, w_out_odd[j].astype(BF16), None)], x)
        tail = (norm_ffn[i], w_ffn_up[i].astype(BF16), w_ffn_down[i].astype(BF16),
                norm_ple[i], w_ple_gate[i].astype(BF16))
        wp = w_ple_proj[i].astype(BF16)
        if i < DEPTH - 1:
            xo = ffn_ple(x, *tail, pp[i], wp, 0)
            x = ffn_ple(x, *tail, ps[i], wp, n_p, xo)
        else:
            y_prompt = ffn_ple(x, *tail, pp[i], wp, 0, None, norm_final).reshape(bp, Lp, D_MODEL)
            y_sample = ffn_ple(x, *tail, ps[i], wp, n_p, None, norm_final).reshape(bs, Ls, D_MODEL)
    stack = lambda sts, idx: jnp.stack([s[idx] for s in sts])
    ssm_shape = lambda b: (N_ODD, b, C_HEADS, C_HEADDIM, C_STATE)
    return (y_prompt, y_sample,
            mC_p, stack(st_p_even, 1), stack(st_p_even, 2), stack(st_p_even, 3),
            stack(st_p_even, 4), ssm_p.reshape(ssm_shape(bp)), jnp.stack(st_p_odd),
            mC_s, stack(st_s_even, 1), stack(st_s_even, 2), stack(st_s_even, 3),
            stack(st_s_even, 4), ssm_s.reshape(ssm_shape(bs)), jnp.stack(st_s_odd))
```

```python
import math
import functools
import jax
import jax.numpy as jnp
from jax import lax
from jax.experimental import pallas as pl
from jax.experimental.pallas import tpu as pltpu

D_MODEL = 1024
DEPTH = 4
F32 = jnp.float32
BF16 = jnp.bfloat16
EPS = 1e-6
N_EVEN = (DEPTH + 1) // 2
N_ODD = DEPTH // 2
D_FF = 4 * D_MODEL
PLE_DIM = 256

A_HEADS = 4
A_DK = D_MODEL // 8
A_DV = D_MODEL // 8
A_QK = A_HEADS * A_DK
A_V = A_HEADS * A_DV
A_COLS = 2 * A_QK + 2 * A_V + 2 * A_HEADS

B_HEADS = 8
B_DH = 64
B_W = B_HEADS * B_DH
B_W_RANK = 64
B_A_RANK = 64
B_G_RANK = 128
B_COLS = 3 * B_W + B_W_RANK + B_A_RANK + B_G_RANK
B_DECAY_OFFSET = 0.5
B_GN_EPS = 64e-5

EVEN_COLS = A_COLS + B_COLS
EVEN_OUT = A_V + B_W

C_INNER = 2 * D_MODEL
C_HEADDIM = 64
C_HEADS = C_INNER // C_HEADDIM
C_GROUPS = 4
C_HPG = C_HEADS // C_GROUPS
C_STATE = 128
C_CONV = 4
C_CHUNK = 128
C_CONV_DIM = C_INNER + 2 * C_GROUPS * C_STATE
ODD_COLS = C_INNER + C_CONV_DIM + C_HEADS

LANES = 128
SUBLANES = 8
VMEM_LIMIT = 56 * 1024 * 1024
SCAN_VMEM_LIMIT = 60 * 1024 * 1024
TOKEN_TILE = 512

EVEN_GATE_PAD = 256
EVEN_N = B_COLS + EVEN_GATE_PAD + 2 * A_QK + 2 * A_V
EVEN_G_OFF = B_COLS
EVEN_A_OFF = B_COLS + EVEN_GATE_PAD
ODD_DT_PAD = 256
ODD_N = C_INNER + C_CONV_DIM + ODD_DT_PAD
ODD_DT_OFF = C_INNER + C_CONV_DIM


def _cparams(sem, vmem_limit=VMEM_LIMIT):
    return pltpu.CompilerParams(dimension_semantics=sem, vmem_limit_bytes=vmem_limit)


def _rms(x, g):
    return x * lax.rsqrt(jnp.mean(x * x, axis=-1, keepdims=True) + EPS) * g


def _resident(shape):
    nd = len(shape)
    return pl.BlockSpec(shape, lambda *_: (0,) * nd, pipeline_mode=pl.Buffered(1))


def _rows(width):
    return pl.BlockSpec((TOKEN_TILE, width), lambda i: (i, 0))


def _fill_into(body, n_in, into, out_idx, into2=None, out_idx2=None):
    pairs = [(a, o) for a, o in ((into, out_idx), (into2, out_idx2)) if a is not None]
    if not pairs:
        return body, [], [], {}

    def skipping(*refs):
        return body(*refs[:n_in], *refs[n_in + len(pairs):])

    return (skipping, [pl.BlockSpec(memory_space=pl.ANY)] * len(pairs), [a for a, _ in pairs],
            {n_in + i: o for i, (_, o) in enumerate(pairs)})


def _norm_matmul_kernel(x_ref, g_ref, w_ref, o_ref, *, tn):
    xn = _rms(x_ref[...], g_ref[...]).astype(BF16)
    for n0 in range(0, w_ref.shape[1], tn):
        o_ref[:, n0:n0 + tn] = jnp.dot(xn, w_ref[:, n0:n0 + tn], preferred_element_type=F32)


def norm_matmul(x, g, w, tn, row0=0, total=None, into=None):
    m, n = x.shape[0], w.shape[1]
    total = m if total is None else total
    t0 = row0 // TOKEN_TILE
    in_specs = [_rows(D_MODEL), _resident((1, D_MODEL)), _resident(w.shape)]
    body, xspecs, xargs, aliases = _fill_into(functools.partial(_norm_matmul_kernel, tn=tn), len(in_specs), into, 0)
    return pl.pallas_call(
        body,
        grid=(m // TOKEN_TILE,),
        in_specs=in_specs + xspecs,
        out_specs=pl.BlockSpec((TOKEN_TILE, n), lambda i: (t0 + i, 0)),
        out_shape=jax.ShapeDtypeStruct((total, n), F32),
        input_output_aliases=aliases,
        compiler_params=_cparams(("arbitrary",)),
        name="norm_matmul",
    )(x, g.reshape(1, D_MODEL), w, *xargs)


FFN_CHUNK = 512


def _matmul_res_kernel(*refs, kinds):
    x_ref, o_ref = refs[-2], refs[-1]
    acc = x_ref[...]
    pos = 0
    for has_gate, channel_major in kinds:
        a = refs[pos][...]
        if channel_major:
            a = a.T
        if has_gate:
            a = a * refs[pos + 1][...]
        w_ref = refs[pos + 1 + has_gate]
        pos += 2 + has_gate
        acc = acc + jnp.dot(a.astype(BF16), w_ref[...], preferred_element_type=F32)
    o_ref[...] = acc


def matmul_res(terms, x, row0=0, nrows=None, into=None, total=None):
    m = x.shape[0] if total is None else total
    nrows = m if nrows is None else nrows
    t0 = row0 // TOKEN_TILE
    rows = lambda width: pl.BlockSpec((TOKEN_TILE, width), lambda i: (t0 + i, 0))
    xspec = rows(D_MODEL) if total is None else _rows(D_MODEL)
    specs, args, kinds = [], [], []
    for a, gate, w, seq_len in terms:
        if seq_len is None:
            specs.append(rows(a.shape[1]))
        elif seq_len == 0:
            specs.append(pl.BlockSpec((TOKEN_TILE, a.shape[1]), lambda i: (i, 0)))
        else:
            assert seq_len % TOKEN_TILE == 0
            per_seq = seq_len // TOKEN_TILE
            specs.append(pl.BlockSpec((None, a.shape[1], TOKEN_TILE), lambda i: (i // per_seq, 0, i % per_seq)))
        specs += ([rows(w.shape[0])] if gate is not None else []) + [_resident(w.shape)]
        args += [a] + ([gate] if gate is not None else []) + [w]
        kinds.append((int(gate is not None), bool(seq_len)))
    specs.append(xspec)
    body, xspecs, xargs, aliases = _fill_into(
        functools.partial(_matmul_res_kernel, kinds=tuple(kinds)), len(specs), into, 0)
    return pl.pallas_call(
        body,
        grid=(nrows // TOKEN_TILE,),
        in_specs=specs + xspecs,
        out_specs=rows(D_MODEL),
        out_shape=jax.ShapeDtypeStruct((m, D_MODEL), F32),
        input_output_aliases=aliases,
        compiler_params=_cparams(("arbitrary",)),
        name="matmul_res",
    )(*args, x, *xargs)


def _ffn_ple_kernel(x_ref, gf_ref, wu_ref, wd_ref, gp_ref, wg_ref, p_ref, wp_ref, *rest):
    gl_ref = rest[0] if len(rest) == 3 else None
    o_ref, y_s = rest[-2:]
    x = x_ref[...]
    xn = _rms(x, gf_ref[...]).astype(BF16)
    y_s[...] = x
    for c0 in range(0, D_FF, FFN_CHUNK):
        h = jnp.dot(xn, wu_ref[:, c0:c0 + FFN_CHUNK], preferred_element_type=F32)
        h = jnp.square(jnp.maximum(h, 0.0)).astype(BF16)
        y_s[...] += jnp.dot(h, wd_ref[c0:c0 + FFN_CHUNK, :], preferred_element_type=F32)
    y = y_s[...]
    yn = _rms(y, gp_ref[...]).astype(BF16)
    gate = jax.nn.sigmoid(jnp.dot(yn, wg_ref[...], preferred_element_type=F32))
    proj = jnp.dot(p_ref[...].astype(BF16), wp_ref[...], preferred_element_type=F32)
    out = y + proj * gate
    o_ref[...] = out if gl_ref is None else _rms(out, gl_ref[...])


def ffn_ple(x, g_ffn, wu, wd, g_ple, wg, p, wp, row0, into=None, g_last=None):
    t0 = row0 // TOKEN_TILE
    rows = pl.BlockSpec((TOKEN_TILE, D_MODEL), lambda i: (t0 + i, 0))
    row1 = lambda v: v.reshape(1, D_MODEL)
    in_specs = [rows, _resident((1, D_MODEL)), _resident(wu.shape), _resident(wd.shape),
                _resident((1, D_MODEL)), _resident(wg.shape), _rows(PLE_DIM), _resident(wp.shape)]
    args = [x, row1(g_ffn), wu, wd, row1(g_ple), wg, p, wp]
    if g_last is not None:
        assert into is None
        in_specs.append(_resident((1, D_MODEL)))
        args.append(row1(g_last))
    body, xspecs, xargs, aliases = _fill_into(_ffn_ple_kernel, len(in_specs), into, 0)
    return pl.pallas_call(
        body,
        grid=(p.shape[0] // TOKEN_TILE,),
        in_specs=in_specs + xspecs,
        out_specs=rows if g_last is None else _rows(D_MODEL),
        out_shape=jax.ShapeDtypeStruct(x.shape if g_last is None else (p.shape[0], D_MODEL), F32),
        input_output_aliases=aliases,
        scratch_shapes=[pltpu.VMEM((TOKEN_TILE, D_MODEL), F32)],
        compiler_params=_cparams(("arbitrary",)),
        name="ffn_ple",
    )(*args, *xargs)


def _rwkv_operands(k, a, r, kk_t, ka_t, rk_t):
    kk = k * kk_t
    kk = kk * lax.rsqrt(jnp.maximum(jnp.sum(kk * kk, axis=1, keepdims=True), 1e-24))
    k2 = k * (1.0 + (a - 1.0) * ka_t)
    return -kk, kk * a, k2, jnp.sum(r * k2 * rk_t, axis=1)


def _rwkv_step(s_ref, tile0, ntiles, r, w, k, a, b, vrows):
    ys = []
    for lt in range(ntiles):
        rows = slice((tile0 + lt) * B_DH, (tile0 + lt + 1) * B_DH)
        s = s_ref[rows, :]
        sa = jnp.sum(s * a, axis=0, keepdims=True)
        s = s * w + sa * b + vrows[lt:lt + 1] * k
        s_ref[rows, :] = s
        ys.append(jnp.sum(s * r, axis=0, keepdims=True))
    return jnp.concatenate(ys, axis=0)


LONG_T = 128
LONG_NLT = B_DH // 2
KEY_PITCH = B_DH + SUBLANES
VAL_PITCH = LONG_NLT + SUBLANES


def _rwkv_scan_long_kernel(r_ref, w_ref, k_ref, a_ref, v_ref, kkt_ref, kat_ref, rkt_ref, lnw_ref, lnb_ref, s0_ref,
                           yt_ref, s_ref, or_s, ow_s, ok_s, oa_s, av_s, v_s, y_s):
    nb = yt_ref.shape[0]

    @pl.when(pl.program_id(0) == 0)
    def _():
        s_ref[...] = s0_ref[...]
        for scr in (or_s, ow_s, ok_s, oa_s, av_s, v_s, y_s):
            scr[...] = jnp.zeros(scr.shape, F32)

    lo = lax.broadcasted_iota(jnp.int32, (LONG_T, LANES), 1) < LANES // 2

    def tile(ref, c):
        m = ref[c]
        return jnp.concatenate([m, m], axis=0).T

    def relayout(c, carry):
        for ref, dst in ((r_ref, or_s), (w_ref, ow_s), (k_ref, ok_s), (a_ref, oa_s)):
            dst[pl.ds(c, LONG_T, stride=KEY_PITCH), :] = tile(ref, c)
            dst[pl.ds(c + LONG_NLT, LONG_T, stride=KEY_PITCH), :] = tile(ref, c + LONG_NLT)
        v_s[pl.ds(c, LONG_T, stride=VAL_PITCH), :] = jnp.where(lo, tile(v_ref, c), tile(v_ref, c + LONG_NLT))
        return carry

    lax.fori_loop(0, LONG_NLT, relayout, 0)

    def unpad(scr, pitch, n):
        return scr[...].reshape(LONG_T, pitch, LANES)[:, :n]

    def pad(x, pitch):
        zeros = jnp.zeros((LONG_T, pitch - x.shape[1], LANES), F32)
        return jnp.concatenate([x, zeros], axis=1).reshape(LONG_T * pitch, LANES)

    av, bv, k2, bonus = _rwkv_operands(unpad(ok_s, KEY_PITCH, B_DH), unpad(oa_s, KEY_PITCH, B_DH),
                                       unpad(or_s, KEY_PITCH, B_DH), kkt_ref[...], kat_ref[...], rkt_ref[...])
    av_s[...] = pad(av, KEY_PITCH)
    oa_s[...] = pad(bv, KEY_PITCH)
    ok_s[...] = pad(k2, KEY_PITCH)

    def step(t, carry):
        kr = pl.ds(pl.multiple_of(t * KEY_PITCH, SUBLANES), B_DH)
        vr = pl.ds(pl.multiple_of(t * VAL_PITCH, SUBLANES), LONG_NLT)
        y_s[vr, :] = _rwkv_step(s_ref, 0, LONG_NLT, or_s[kr, :], ow_s[kr, :], ok_s[kr, :], av_s[kr, :],
                                oa_s[kr, :], v_s[vr, :])
        return carry

    lax.fori_loop(0, LONG_T, step, 0)

    def head_sum(x):
        tot = jnp.sum(x, axis=1)
        return tot + pltpu.roll(tot, LANES // 2, axis=1)

    y = unpad(y_s, VAL_PITCH, LONG_NLT)
    d = y - (head_sum(y) * (1.0 / B_DH))[:, None, :]
    var = head_sum(d * d) * (1.0 / B_DH)
    y = (d * lax.rsqrt(var + B_GN_EPS)[:, None, :] * lnw_ref[...] + lnb_ref[...]
         + bonus[:, None, :] * unpad(v_s, VAL_PITCH, LONG_NLT))
    y_s[...] = pad(y, VAL_PITCH)

    def relayout_out(lt, carry):
        yt = y_s[pl.ds(lt, LONG_T, stride=VAL_PITCH), :].T
        for i2 in range(2):
            for b in range(nb):
                row = i2 * (LANES // 2) + b * B_HEADS
                yt_ref[b, pl.ds(lt + LONG_NLT * i2, B_HEADS, stride=B_DH), :] = yt[row:row + B_HEADS, :]
        return carry

    lax.fori_loop(0, LONG_NLT, relayout_out, 0)


def rwkv_scan_long(rc, wc, kc, ac, vc, S0, k_k, k_a, r_k, ln_w, ln_b):
    _, bh, L = rc.shape
    b = bh // B_HEADS
    assert bh * 2 == LANES and L % LONG_T == 0
    per_key = lambda p: jnp.tile(p.reshape(B_HEADS, B_DH).T, (1, LANES // B_HEADS))
    per_val = lambda p: jnp.concatenate(
        [jnp.tile(p.reshape(B_HEADS, 2, LONG_NLT)[:, i2].T, (1, b)) for i2 in range(2)], axis=1)
    s0 = S0.reshape(b, B_HEADS, 2, LONG_NLT, B_DH).transpose(3, 4, 2, 0, 1).reshape(LONG_NLT * B_DH, LANES)
    blk = pl.BlockSpec((B_DH, bh, LONG_T), lambda c: (0, 0, c))
    oblk = pl.BlockSpec((b, B_W, LONG_T), lambda c: (0, 0, c))
    const = lambda shape: pl.BlockSpec(shape, lambda c: (0,) * len(shape))
    big = pltpu.VMEM((LONG_T * KEY_PITCH, LANES), F32)
    small = pltpu.VMEM((LONG_T * VAL_PITCH, LANES), F32)
    yt, s = pl.pallas_call(
        _rwkv_scan_long_kernel,
        grid=(L // LONG_T,),
        in_specs=[blk] * 5 + [const((B_DH, LANES))] * 3 + [const((LONG_NLT, LANES))] * 2
        + [_resident((LONG_NLT * B_DH, LANES))],
        out_specs=[oblk, const((LONG_NLT * B_DH, LANES))],
        out_shape=[jax.ShapeDtypeStruct((b, B_W, L), F32), jax.ShapeDtypeStruct(s0.shape, F32)],
        scratch_shapes=[big] * 5 + [small, small],
        compiler_params=_cparams(("arbitrary",), SCAN_VMEM_LIMIT),
        name="rwkv_scan_long",
    )(rc, wc, kc, ac, vc, per_key(k_k), per_key(k_a), per_key(r_k), per_val(ln_w), per_val(ln_b), s0)
    s = s.reshape(LONG_NLT, B_DH, 2, b, B_HEADS).transpose(3, 4, 2, 0, 1).reshape(b, B_HEADS, B_DH, B_DH)
    return yt, s


def _rwkv_scan_short_kernel(r_ref, w_ref, k_ref, a_ref, v_ref, kkt_ref, kat_ref, rkt_ref, lnw_ref, lnb_ref, s0_ref,
                            y_ref, sout_ref, s_s, or_s, ow_s, ok_s, oa_s, av_s, ov_s, y_s, *, L):
    nt = 2 * B_DH
    for q in range(nt * B_DH // LANES):
        s_s[q * LANES:(q + 1) * LANES, :] = s0_ref[:, q * LANES:(q + 1) * LANES].T
    for ref, dst in ((r_ref, or_s), (w_ref, ow_s), (k_ref, ok_s), (a_ref, oa_s), (v_ref, ov_s)):
        for t in range(L):
            dst[t] = ref[pl.ds(t, LANES, stride=L), :].T.reshape(2, B_DH, LANES)
    shape3 = (L * 2, B_DH, LANES)
    tiles = lambda ref: jnp.concatenate([ref[...]] * L, axis=0)
    av, bv, k2, bonus = _rwkv_operands(ok_s[...].reshape(shape3), oa_s[...].reshape(shape3),
                                       or_s[...].reshape(shape3), tiles(kkt_ref), tiles(kat_ref), tiles(rkt_ref))
    av_s[...] = av.reshape(L, 2, B_DH, LANES)
    oa_s[...] = bv.reshape(L, 2, B_DH, LANES)
    ok_s[...] = k2.reshape(L, 2, B_DH, LANES)

    def step(t, carry):
        for h in range(2):
            y_s[t, h] = _rwkv_step(s_s, h * B_DH, B_DH, or_s[t, h], ow_s[t, h], ok_s[t, h], av_s[t, h],
                                   oa_s[t, h], ov_s[t, h])
        return carry

    lax.fori_loop(0, L, step, 0)

    y = y_s[...]
    d = y - jnp.mean(y, axis=2, keepdims=True)
    var = jnp.mean(d * d, axis=2, keepdims=True)
    y = (d * lax.rsqrt(var + B_GN_EPS) * lnw_ref[...] + lnb_ref[...]
         + bonus.reshape(L, 2, 1, LANES) * ov_s[...])
    for t in range(L):
        y_ref[pl.ds(t, LANES, stride=L), :] = y[t].reshape(nt, LANES).T
    for q in range(nt * B_DH // LANES):
        sout_ref[:, q * LANES:(q + 1) * LANES] = s_s[q * LANES:(q + 1) * LANES, :].T


def rwkv_scan_short(r, w, k, a, v, S0, L, k_k, k_a, r_k, ln_w, ln_b):
    n = r.shape[0]
    b = n // L
    assert b == LANES
    npair = B_HEADS // 2
    wide = lambda p: jnp.broadcast_to(p.reshape(npair, 2, B_DH, 1), (npair, 2, B_DH, LANES))
    s0 = S0.reshape(b, B_HEADS * B_DH * B_DH)
    blk = pl.BlockSpec((n, LANES), lambda p: (0, p))
    cblk = pl.BlockSpec((None, 2, B_DH, LANES), lambda p: (p, 0, 0, 0))
    sblk = pl.BlockSpec((b, 2 * B_DH * B_DH), lambda p: (0, p))
    op = pltpu.VMEM((L, 2, B_DH, LANES), F32)
    y, s = pl.pallas_call(
        functools.partial(_rwkv_scan_short_kernel, L=L),
        grid=(npair,),
        in_specs=[blk] * 5 + [cblk] * 5 + [sblk],
        out_specs=[blk, sblk],
        out_shape=[jax.ShapeDtypeStruct((n, B_W), F32), jax.ShapeDtypeStruct(s0.shape, F32)],
        scratch_shapes=[pltpu.VMEM((2 * B_DH * B_DH, LANES), F32)] + [op] * 7,
        compiler_params=_cparams(("arbitrary",)),
        name="rwkv_scan_short",
    )(r, w, k, a, v, wide(k_k), wide(k_a), wide(r_k), wide(ln_w), wide(ln_b), s0)
    return y, s.reshape(b, B_HEADS, B_DH, B_DH)


RWKV_PREP_ROWS = 256
RWKV_SHORT_ROWS = 64
B_LORA_OFF = 3 * B_W


def _rwkv_prep_kernel(z_ref, sh0_ref, mu_ref, wwa_ref, g2_ref, w0_ref, a0_ref,
                      r_ref, w_ref, k_ref, a_ref, v_ref, g_ref, sh_ref, *scratch, T, nseq, channel_major):
    @pl.when(pl.program_id(1) == 0)
    def _():
        sh_ref[...] = sh0_ref[...]

    z = z_ref[...]
    rowid = lax.broadcasted_iota(jnp.int32, (z.shape[0], 1), 0)
    zprev = pltpu.roll(z, 1, axis=0)
    for u in range(nseq):
        zprev = jnp.where(rowid == u * T, sh_ref[u], zprev)
    for u in range(nseq):
        sh_ref[u] = z[(u + 1) * T - 1:(u + 1) * T, :]
    zs = z + (zprev - z) * mu_ref[...]
    r = zs[:, :B_W]
    k = zs[:, B_W:2 * B_W]
    lora = zs[:, B_LORA_OFF:B_LORA_OFF + LANES]
    lane = lax.broadcasted_iota(jnp.int32, lora.shape, 1)
    lora = jnp.where(lane < B_W_RANK, jnp.tanh(lora), lora).astype(BF16)
    wa = jnp.dot(lora, wwa_ref[...], preferred_element_type=F32)
    w_log = -jax.nn.softplus(-(w0_ref[...] + wa[:, :B_W])) - B_DECAY_OFFSET
    a = jax.nn.sigmoid(a0_ref[...] + wa[:, B_W:])
    zg = zs[:, B_LORA_OFF + LANES:B_LORA_OFF + LANES + B_G_RANK]
    def emit(ref, x):
        if not channel_major:
            ref[...] = x
            return
        xt_s = scratch[0]
        xt = x.T
        for j in range(xt_s.shape[0]):
            for h in range(B_HEADS):
                xt_s[j, h * KEY_PITCH:h * KEY_PITCH + B_DH, :] = xt[h * B_DH:(h + 1) * B_DH, j * LANES:(j + 1) * LANES]
        for c in range(B_DH):
            for j in range(xt_s.shape[0]):
                ref[c, :, j * LANES:(j + 1) * LANES] = xt_s[j, pl.ds(c, B_HEADS, stride=KEY_PITCH), :]

    emit(r_ref, r)
    emit(w_ref, jnp.exp(-jnp.exp(w_log)))
    emit(k_ref, k)
    emit(a_ref, a)
    emit(v_ref, zs[:, 2 * B_W:3 * B_W])
    g_ref[...] = jnp.dot(jax.nn.sigmoid(zg).astype(BF16), g2_ref[...], preferred_element_type=F32)


def rwkv_prep(zin, row0, b, L, gate_into, shift0, mu, w0, w2, a0, a2, g2):
    assert B_W_RANK + B_A_RANK == LANES
    T = math.gcd(L, RWKV_PREP_ROWS)
    channel_major = T == RWKV_PREP_ROWS
    nseq = 1 if channel_major else RWKV_SHORT_ROWS // T
    assert nseq == 1 or (L == T and b % nseq == 0)
    R = T * nseq
    nchunk = L // T
    blk0 = row0 // R
    wwa = jnp.zeros((LANES, 2 * B_W), F32).at[:B_W_RANK, :B_W].set(w2).at[B_W_RANK:, B_W:].set(a2).astype(BF16)
    row = lambda v: v.reshape(1, -1)
    if channel_major:
        blk = pl.BlockSpec((B_DH, B_HEADS, R), lambda i, c: (0, i, c))
        oshape = jax.ShapeDtypeStruct((B_DH, b * B_HEADS, L), F32)
    else:
        blk = pl.BlockSpec((R, B_W), lambda i, c: (i * nchunk + c, 0))
        oshape = jax.ShapeDtypeStruct((b * L, B_W), F32)
    gblk = pl.BlockSpec((R, B_W), lambda i, c: (blk0 + i * nchunk + c, 0))
    shspec = pl.BlockSpec((nseq, 1, B_COLS), lambda i, c: (i, 0, 0))
    const = lambda shape: pl.BlockSpec(shape, lambda i, c: (0,) * len(shape))
    in_specs = [pl.BlockSpec((R, B_COLS), lambda i, c: (blk0 + i * nchunk + c, 0)), shspec, const((1, B_COLS)),
                const((LANES, 2 * B_W)), const((B_G_RANK, B_W)), const((1, B_W)), const((1, B_W))]
    body, xspecs, xargs, aliases = _fill_into(
        functools.partial(_rwkv_prep_kernel, T=T, nseq=nseq, channel_major=channel_major),
        len(in_specs), gate_into, 5)
    outs = pl.pallas_call(
        body,
        grid=(b // nseq, nchunk),
        in_specs=in_specs + xspecs,
        out_specs=[blk] * 5 + [gblk, shspec],
        out_shape=[oshape] * 5
        + [jax.ShapeDtypeStruct((zin.shape[0], B_W), F32), jax.ShapeDtypeStruct((b, 1, B_COLS), F32)],
        input_output_aliases=aliases,
        scratch_shapes=[pltpu.VMEM((R // LANES, B_HEADS * KEY_PITCH, LANES), F32)] if channel_major else [],
        compiler_params=_cparams(("arbitrary", "arbitrary")),
        name="rwkv_prep",
    )(zin, shift0.reshape(b, 1, B_COLS), row(mu), wwa, g2.astype(BF16), row(w0), row(a0), *xargs)
    return outs[:5], outs[5], outs[6].reshape(b, B_COLS)


MLSTM_ROWS = 128
HIGHEST = lax.Precision.HIGHEST


def _mlstm_kernel(q_ref, k_ref, v_ref, o_ref, g_ref, gt_ref, brow_ref, bcol_ref, nw_ref,
                  c0_ref, n0_ref, m0_ref, h_ref, c_ref, n_ref, m_ref, *, T, nseq):
    R = MLSTM_ROWS

    @pl.when(pl.program_id(1) == 0)
    def _():
        c_ref[...] = c0_ref[...]
        n_ref[...] = n0_ref[...]
        m_ref[...] = m0_ref[...]

    shift = T.bit_length() - 1
    ri = lax.broadcasted_iota(jnp.int32, (R, R), 0)
    ci = lax.broadcasted_iota(jnp.int32, (R, R), 1)
    mask = (ci <= ri) & (jnp.right_shift(ri, shift) == jnp.right_shift(ci, shift))
    lmat = mask.astype(F32)
    rowid = lax.broadcasted_iota(jnp.int32, (R, 1), 0)
    rsel = [(rowid >= u * T) & (rowid < (u + 1) * T) for u in range(nseq)]

    g = g_ref[...] + brow_ref[...]
    lane = lax.broadcasted_iota(jnp.int32, g.shape, 1)
    glog = jnp.where((lane >= A_HEADS) & (lane < 2 * A_HEADS), jax.nn.log_sigmoid(g), g)
    gt = gt_ref[...] + bcol_ref[...]
    sub = lax.broadcasted_iota(jnp.int32, gt.shape, 0)
    gtlog = jnp.where(sub >= A_HEADS, jax.nn.log_sigmoid(gt), gt)
    bc_col = jnp.dot(lmat, glog, precision=HIGHEST, preferred_element_type=F32)
    bc_row = lax.dot_general(gtlog, lmat, (((1,), (1,)), ((), ())), precision=HIGHEST,
                             preferred_element_type=F32)
    lane_m = lax.broadcasted_iota(jnp.int32, (1, LANES), 1)
    m_old = [m_ref[u] for u in range(nseq)]
    m_out = [jnp.zeros((1, LANES), F32) for _ in range(nseq)]

    for h in range(A_HEADS):
        hs = slice(h * A_DK, (h + 1) * A_DK)
        bcc = bc_col[:, A_HEADS + h:A_HEADS + h + 1]
        bcr = bc_row[A_HEADS + h:A_HEADS + h + 1, :]
        lir = gtlog[h:h + 1, :]
        lic = glog[:, h:h + 1]
        m_u = [m_old[u][:, h:h + 1] for u in range(nseq)]
        m_col = m_u[0]
        for u in range(1, nseq):
            m_col = jnp.where(rsel[u], m_u[u], m_col)
        dmat = jnp.where(mask, bcc - bcr + lir, -jnp.inf)
        inter = bcc + m_col
        mt = jnp.maximum(inter, jnp.max(dmat, axis=1, keepdims=True))
        p = jnp.exp(dmat - mt)
        qh = q_ref[:, hs] * (A_DK ** -0.5)
        kh = k_ref[:, hs]
        qb, kb, vb = qh.astype(BF16), kh.astype(BF16), v_ref[:, hs].astype(BF16)
        wq = lax.dot_general(qb, kb, (((1,), (1,)), ((), ())), preferred_element_type=F32) * p
        wi = jnp.exp(inter - mt)
        c_old = [c_ref[u, h] for u in range(nseq)]
        n_old = [n_ref[u, h:h + 1, :] for u in range(nseq)]
        qc = jnp.dot(qb, c_old[0].astype(BF16), preferred_element_type=F32)
        qn = jnp.sum(qh * n_old[0], axis=1, keepdims=True)
        for u in range(1, nseq):
            qc = jnp.where(rsel[u], jnp.dot(qb, c_old[u].astype(BF16), preferred_element_type=F32), qc)
            qn = jnp.where(rsel[u], jnp.sum(qh * n_old[u], axis=1, keepdims=True), qn)
        num = jnp.dot(wq.astype(BF16), vb, preferred_element_type=F32) + wi * qc
        den = jnp.sum(wq, axis=1, keepdims=True) + wi * qn
        hh = num / jnp.maximum(jnp.abs(den), jnp.exp(-mt))
        hh = hh * lax.rsqrt(jnp.mean(hh * hh, axis=-1, keepdims=True) + EPS)
        h_ref[:, hs] = (hh * nw_ref[:, hs] * jax.nn.sigmoid(o_ref[:, hs])).astype(BF16)
        for u in range(nseq):
            b_last = bcc[(u + 1) * T - 1:(u + 1) * T, :]
            gs = b_last - bcc + lic
            gmax = jnp.max(gs if nseq == 1 else jnp.where(rsel[u], gs, -jnp.inf), axis=0, keepdims=True)
            m_new = jnp.maximum(b_last + m_u[u], gmax)
            decay = jnp.exp(b_last + m_u[u] - m_new)
            ws = jnp.exp(gs - m_new)
            if nseq > 1:
                ws = jnp.where(rsel[u], ws, 0.0)
            kw = kh * ws
            c_ref[u, h] = decay * c_old[u] + lax.dot_general(
                kw.astype(BF16), vb, (((0,), (0,)), ((), ())), preferred_element_type=F32)
            n_ref[u, h:h + 1, :] = decay * n_old[u] + jnp.sum(kw, axis=0, keepdims=True)
            m_out[u] = jnp.where(lane_m == h, m_new, m_out[u])
    for u in range(nseq):
        m_ref[u] = m_out[u]


def mlstm(zin, gates_t, row0, b, L, h_into, layer, c_into, b_i, b_f, m_norm, C0, n0, m0):
    R = MLSTM_ROWS
    T = math.gcd(L, R)
    nseq = R // T
    assert nseq == 1 or (L == T and b % nseq == 0)
    nchunk = L // T
    blk0 = row0 // R
    bias = jnp.concatenate([b_i, b_f])
    bias_row = jnp.zeros((1, LANES), F32).at[0, :2 * A_HEADS].set(bias)
    bias_col = bias.reshape(2 * A_HEADS, 1)
    m0p = jnp.zeros((b, 1, LANES), F32).at[:, 0, :A_HEADS].set(m0)
    blk = lambda i, c: blk0 + i * nchunk + c
    rowblk = lambda col: pl.BlockSpec((R, A_QK), lambda i, c: (blk(i, c), EVEN_A_OFF // A_QK + col))
    cspec = pl.BlockSpec((nseq, A_HEADS, A_DK, A_DV), lambda i, c: (i, 0, 0, 0))
    cout = pl.BlockSpec((None, nseq, A_HEADS, A_DK, A_DV), lambda i, c: (layer, i, 0, 0, 0))
    nspec = pl.BlockSpec((nseq, A_HEADS, A_DK), lambda i, c: (i, 0, 0))
    mspec = pl.BlockSpec((nseq, 1, LANES), lambda i, c: (i, 0, 0))
    const = lambda shape: pl.BlockSpec(shape, lambda i, c: (0,) * len(shape))
    in_specs = [rowblk(0), rowblk(1), rowblk(2), rowblk(3),
                pl.BlockSpec((R, LANES), lambda i, c: (blk(i, c), EVEN_G_OFF // LANES)),
                pl.BlockSpec((None, 2 * A_HEADS, R), lambda i, c: (blk(i, c), 0, 0)),
                const((1, LANES)), const((2 * A_HEADS, 1)), const((1, A_V)),
                cspec if C0.ndim == 4 else cout, nspec, mspec]
    body, xspecs, xargs, aliases = _fill_into(
        functools.partial(_mlstm_kernel, T=T, nseq=nseq), len(in_specs), h_into, 0, c_into, 1)
    h, C, n, m = pl.pallas_call(
        body,
        grid=(b // nseq, nchunk),
        in_specs=in_specs + xspecs,
        out_specs=[pl.BlockSpec((R, A_V), lambda i, c: (blk(i, c), 0)), cout, nspec, mspec],
        out_shape=[jax.ShapeDtypeStruct((zin.shape[0], A_V), BF16),
                   jax.ShapeDtypeStruct((N_EVEN,) + C0.shape[-4:], F32), jax.ShapeDtypeStruct(n0.shape, F32),
                   jax.ShapeDtypeStruct(m0p.shape, F32)],
        input_output_aliases=aliases,
        compiler_params=_cparams(("arbitrary", "arbitrary")),
        name="mlstm",
    )(zin, zin, zin, zin, zin, gates_t, bias_row, bias_col, m_norm.reshape(1, A_V), C0, n0, m0p, *xargs)
    return h, C, n, m[:, 0, :A_HEADS]


def even_mixer(zin, gates_t, row0, b, L, into, layer, c_into, C0, n0, m0, S0, shift0, b_i, b_f, m_norm,
               mu, w0, w2, a0, a2, g2, k_k, k_a, r_k, ln_w, ln_b):
    hA, C, n, m = mlstm(zin, gates_t, row0, b, L, into[0], layer, c_into, b_i, b_f, m_norm, C0, n0, m0)
    ops, g, shift = rwkv_prep(zin, row0, b, L, into[1], shift0, mu, w0, w2, a0, a2, g2)
    if ops[0].ndim == 3:
        hB, S = rwkv_scan_long(*ops, S0, k_k, k_a, r_k, ln_w, ln_b)
    else:
        hB, S = rwkv_scan_short(*ops, S0, L, k_k, k_a, r_k, ln_w, ln_b)
    return hA, hB, g, (C, n, m, S, shift)


C_PAIRS = C_HEADS // 2
C_GROUP_W = C_INNER // C_GROUPS
C_BC_W = 2 * C_GROUPS * C_STATE
SSD_LONG_ROWS = C_CHUNK
SSD_SHORT_ROWS = 32


def _ssd_kernel(z_ref, x_ref, bc_ref, dt_ref, dtt_ref, cw_ref, cb_ref, dtb_ref, dtbt_ref, al_ref, alt_ref,
                dsk_ref, nw_ref, s0_ref, cv0_ref, y_ref, s_ref, cv_ref, stg_ref, yf_s, *, T, nseq, nchunk):
    R = T * nseq
    tstate = nchunk > 1
    assert not tstate or nseq == 1

    @pl.when(pl.program_id(1) == 0)
    def _():
        if tstate:
            for pr in range(C_PAIRS):
                s_ref[0, pr] = s0_ref[0, pr].T
        else:
            s_ref[...] = s0_ref[...]
        cv_ref[...] = cv0_ref[...]

    shift = T.bit_length() - 1
    ri = lax.broadcasted_iota(jnp.int32, (R, R), 0)
    ci = lax.broadcasted_iota(jnp.int32, (R, R), 1)
    mask = (ci <= ri) & (jnp.right_shift(ri, shift) == jnp.right_shift(ci, shift))
    lmat = mask.astype(F32)
    rowid = lax.broadcasted_iota(jnp.int32, (R, 1), 0)
    rsel = [(rowid >= u * T) & (rowid < (u + 1) * T) for u in range(nseq)]

    def conv_silu(src_ref, src_col, col):
        cols = slice(col, col + LANES)
        accs = []
        for u in range(nseq):
            stg_ref[u, 0:SUBLANES, cols] = cv_ref[u, :, cols]
            stg_ref[u, SUBLANES:SUBLANES + T, cols] = src_ref[u * T:(u + 1) * T, src_col:src_col + LANES]
            acc = cb_ref[:, cols]
            for d in range(C_CONV):
                acc = acc + stg_ref[u, SUBLANES - d:SUBLANES - d + T, cols] * cw_ref[C_CONV - 1 - d:C_CONV - d, cols]
            accs.append(acc)
            cv_ref[u, :, cols] = stg_ref[u, T:T + SUBLANES, cols]
        acc = accs[0] if nseq == 1 else jnp.concatenate(accs, axis=0)
        return acc * jax.nn.sigmoid(acc)

    dtv = jax.nn.softplus(dt_ref[...] + dtb_ref[...])
    dtt = jax.nn.softplus(dtt_ref[...] + dtbt_ref[...])
    cum_col = jnp.dot(lmat, dtv * (-jnp.exp(al_ref[...])), precision=HIGHEST, preferred_element_type=F32)
    cum_row = lax.dot_general(dtt * (-jnp.exp(alt_ref[...])), lmat, (((1,), (1,)), ((), ())),
                              precision=HIGHEST, preferred_element_type=F32)

    lo = lax.broadcasted_iota(jnp.int32, (R, LANES), 1) < C_HEADDIM
    rlo = lax.broadcasted_iota(jnp.int32, (LANES, 1), 0) < C_HEADDIM
    nt = (((1,), (1,)), ((), ()))
    tn = (((0,), (0,)), ((), ()))
    pairs_per_group = C_PAIRS // C_GROUPS
    for g in range(C_GROUPS):
        bgf = conv_silu(bc_ref, g * C_STATE, C_INNER + g * C_STATE)
        bg = bgf.astype(BF16)
        cg = conv_silu(bc_ref, (C_GROUPS + g) * C_STATE, C_INNER + (C_GROUPS + g) * C_STATE).astype(BF16)
        cbm = lax.dot_general(cg, bg, nt, preferred_element_type=F32)
        ys = None
        if tstate:
            bgt = bgf.T.astype(BF16)
            sgt = jnp.concatenate([s_ref[0, g * pairs_per_group + q] for q in range(pairs_per_group)], axis=1)
            ys = jnp.dot(cg, sgt.astype(BF16), preferred_element_type=F32)
        for u in range(0 if tstate else nseq):
            sg = s_ref[u, g * pairs_per_group:(g + 1) * pairs_per_group].reshape(C_GROUP_W, C_STATE)
            t_u = lax.dot_general(cg, sg.astype(BF16), nt, preferred_element_type=F32)
            ys = t_u if u == 0 else jnp.where(rsel[u], t_u, ys)
        for q in range(pairs_per_group):
            pr = g * pairs_per_group + q
            ps = slice(pr * LANES, (pr + 1) * LANES)
            xp = conv_silu(x_ref, pr * LANES, pr * LANES)
            cc = [cum_col[:, 2 * pr + e:2 * pr + e + 1] for e in range(2)]
            intra = None
            for e, keep in ((0, lo), (1, jnp.logical_not(lo))):
                hh = 2 * pr + e
                seg = jnp.exp(jnp.where(mask, cc[e] - cum_row[hh:hh + 1, :], -jnp.inf))
                mix = cbm * seg * dtt[hh:hh + 1, :]
                part = jnp.dot(mix.astype(BF16), jnp.where(keep, xp, 0.0).astype(BF16),
                               preferred_element_type=F32)
                intra = part if intra is None else intra + part
            scale = jnp.where(lo, jnp.exp(cc[0]), jnp.exp(cc[1]))
            yp = intra + scale * ys[:, q * LANES:(q + 1) * LANES] + dsk_ref[:, ps] * xp
            zp = z_ref[:, ps]
            yf_s[:, ps] = yp * (zp * jax.nn.sigmoid(zp))
            for u in range(nseq):
                last = (u + 1) * T - 1
                ct = [cc[e][last:last + 1, :] for e in range(2)]
                tail = jnp.where(lo, jnp.exp(ct[0] - cc[0]) * dtv[:, 2 * pr:2 * pr + 1],
                                 jnp.exp(ct[1] - cc[1]) * dtv[:, 2 * pr + 1:2 * pr + 2])
                xw = xp * tail
                if nseq > 1:
                    xw = jnp.where(rsel[u], xw, 0.0)
                if tstate:
                    upd = jnp.dot(bgt, xw.astype(BF16), preferred_element_type=F32)
                    dec = jnp.where(lo[:1], jnp.exp(ct[0]), jnp.exp(ct[1]))
                else:
                    upd = lax.dot_general(xw.astype(BF16), bg, tn, preferred_element_type=F32)
                    dec = jnp.where(rlo, jnp.exp(ct[0]), jnp.exp(ct[1]))
                s_ref[u, pr] = dec * s_ref[u, pr] + upd

    if tstate:
        @pl.when(pl.program_id(1) == nchunk - 1)
        def _():
            for pr in range(C_PAIRS):
                s_ref[0, pr] = s_ref[0, pr].T

    for g in range(C_GROUPS):
        gs_ = slice(g * C_GROUP_W, (g + 1) * C_GROUP_W)
        yg = yf_s[:, gs_]
        yg = yg * lax.rsqrt(jnp.mean(yg * yg, axis=-1, keepdims=True) + EPS) * nw_ref[:, gs_]
        y_ref[:, gs_] = yg.astype(BF16)


def ssd_mixer(zin, row0, b, L, y_into, layer, s_into, ssm0, conv0, conv_w, conv_b, dt_bias, a_log, d_skip, norm_w):
    T = math.gcd(L, SSD_LONG_ROWS)
    nseq = 1 if T == SSD_LONG_ROWS else SSD_SHORT_ROWS // T
    assert nseq == 1 or (L == T and b % nseq == 0)
    R = T * nseq
    nchunk = L // T
    nblk = b * L // R
    blk0 = row0 // R
    dt_t = zin[row0:row0 + b * L, ODD_DT_OFF:ODD_DT_OFF + C_HEADS].reshape(nblk, R, C_HEADS).transpose(0, 2, 1)
    pad_row = lambda v: jnp.zeros((1, LANES), F32).at[0, :C_HEADS].set(v)
    pair_shape = (b, C_PAIRS, 2 * C_HEADDIM, C_STATE)
    per_layer = ssm0.ndim == 4
    s0 = ssm0.reshape(pair_shape if per_layer else (ssm0.shape[0],) + pair_shape)
    cv0 = jnp.concatenate([jnp.zeros((b, SUBLANES - (C_CONV - 1), C_CONV_DIM), F32), conv0], axis=1)
    blk = lambda i, c: blk0 + i * nchunk + c
    sspec = pl.BlockSpec((nseq, C_PAIRS, 2 * C_HEADDIM, C_STATE), lambda i, c: (i, 0, 0, 0))
    sout = pl.BlockSpec((None, nseq, C_PAIRS, 2 * C_HEADDIM, C_STATE), lambda i, c: (layer, i, 0, 0, 0))
    cvspec = pl.BlockSpec((nseq, SUBLANES, C_CONV_DIM), lambda i, c: (i, 0, 0))
    const = lambda shape: pl.BlockSpec(shape, lambda i, c: (0,) * len(shape))
    in_specs = [pl.BlockSpec((R, C_INNER), lambda i, c: (blk(i, c), 0)),
                pl.BlockSpec((R, C_INNER), lambda i, c: (blk(i, c), 1)),
                pl.BlockSpec((R, C_BC_W), lambda i, c: (blk(i, c), 2 * C_INNER // C_BC_W)),
                pl.BlockSpec((R, LANES), lambda i, c: (blk(i, c), ODD_DT_OFF // LANES)),
                pl.BlockSpec((None, C_HEADS, R), lambda i, c: (i * nchunk + c, 0, 0)),
                const((C_CONV, C_CONV_DIM)), const((1, C_CONV_DIM)),
                const((1, LANES)), const((C_HEADS, 1)), const((1, LANES)), const((C_HEADS, 1)),
                const((1, C_INNER)), const((1, C_INNER)), sspec if per_layer else sout, cvspec]
    body, xspecs, xargs, aliases = _fill_into(
        functools.partial(_ssd_kernel, T=T, nseq=nseq, nchunk=nchunk), len(in_specs), y_into, 0, s_into, 1)
    y, s, cv = pl.pallas_call(
        body,
        grid=(b // nseq, nchunk),
        in_specs=in_specs + xspecs,
        out_specs=[pl.BlockSpec((R, C_INNER), lambda i, c: (blk(i, c), 0)), sout, cvspec],
        out_shape=[jax.ShapeDtypeStruct((zin.shape[0], C_INNER), BF16),
                   jax.ShapeDtypeStruct((N_ODD,) + pair_shape, F32), jax.ShapeDtypeStruct(cv0.shape, F32)],
        input_output_aliases=aliases,
        scratch_shapes=[pltpu.VMEM((nseq, SUBLANES + T, C_CONV_DIM), F32), pltpu.VMEM((R, C_INNER), F32)],
        compiler_params=_cparams(("arbitrary", "arbitrary")),
        name="ssd",
    )(zin, zin, zin, zin, dt_t, conv_w, conv_b.reshape(1, C_CONV_DIM),
      pad_row(dt_bias), dt_bias.reshape(C_HEADS, 1), pad_row(a_log), a_log.reshape(C_HEADS, 1),
      jnp.repeat(d_skip, C_HEADDIM).reshape(1, C_INNER), norm_w.reshape(1, C_INNER), s0, cv0, *xargs)
    return y, s, cv[:, SUBLANES - (C_CONV - 1):]


def _even_w_in(w):
    qkvo = w[:, :2 * A_QK + 2 * A_V]
    gates = w[:, 2 * A_QK + 2 * A_V:A_COLS]
    rwkv = w[:, A_COLS:]
    pad = jnp.zeros((D_MODEL, EVEN_GATE_PAD - 2 * A_HEADS), w.dtype)
    return jnp.concatenate([rwkv, gates, pad, qkvo], axis=1).astype(BF16)


def _odd_w_in(w):
    pad = jnp.zeros((D_MODEL, ODD_DT_PAD - C_HEADS), w.dtype)
    return jnp.concatenate([w, pad], axis=1).astype(BF16)


def kernel(x_prompt, x_sample, state_mlstm_C, state_mlstm_n, state_mlstm_m, state_rwkv_S,
           state_rwkv_shift, state_ssm, state_conv, p_prompt, p_sample,
           norm_mix, norm_ffn, w_ffn_up, w_ffn_down, w_ple_proj, norm_ple, w_ple_gate, norm_final,
           w_in_even, mlstm_b_i, mlstm_b_f, mlstm_norm, rwkv_mu, rwkv_w0, rwkv_w2, rwkv_a0, rwkv_a2,
           rwkv_g2, rwkv_k_k, rwkv_k_a, rwkv_r_k, rwkv_ln_w, rwkv_ln_b, w_out_even,
           w_in_odd, conv_w, conv_b, dt_bias, a_log, d_skip, ssm_norm, w_out_odd):
    bp, Lp, _ = x_prompt.shape
    bs, Ls, _ = x_sample.shape
    n_p, n_s = bp * Lp, bs * Ls
    n_tot = n_p + n_s
    xp = x_prompt.reshape(n_p, D_MODEL)
    xs = x_sample.reshape(n_s, D_MODEL)
    pp = p_prompt.reshape(DEPTH, n_p, PLE_DIM)
    ps = p_sample.reshape(DEPTH, n_s, PLE_DIM)
    x = None

    even_small = (mlstm_b_i, mlstm_b_f, mlstm_norm, rwkv_mu, rwkv_w0, rwkv_w2, rwkv_a0, rwkv_a2,
                  rwkv_g2, rwkv_k_k, rwkv_k_a, rwkv_r_k, rwkv_ln_w, rwkv_ln_b)
    odd_small = (conv_w, conv_b, dt_bias, a_log, d_skip, ssm_norm)

    zeros_even = (jnp.zeros((bp, A_HEADS, A_DK, A_DV), F32), jnp.zeros((bp, A_HEADS, A_DK), F32),
                  jnp.zeros((bp, A_HEADS), F32), jnp.zeros((bp, B_HEADS, B_DH, B_DH), F32),
                  jnp.zeros((bp, B_COLS), F32))
    zeros_odd = (jnp.zeros((bp, C_HEADS, C_HEADDIM, C_STATE), F32),
                 jnp.zeros((bp, C_CONV - 1, C_CONV_DIM), F32))

    st_p_even, st_s_even, st_p_odd, st_s_odd = [], [], [], []
    mC_p = mC_s = ssm_p = ssm_s = None
    for i in range(DEPTH):
        j = i // 2
        if i % 2 == 0:
            w_in = _even_w_in(w_in_even[j])
            if x is None:
                zin = norm_matmul(xp, norm_mix[i], w_in, 512, 0, n_tot)
                zin = norm_matmul(xs, norm_mix[i], w_in, 512, n_p, n_tot, zin)
            else:
                zin = norm_matmul(x, norm_mix[i], w_in, 512)
            small = [t[j] for t in even_small]
            gates_t = zin[:, EVEN_G_OFF:EVEN_G_OFF + 2 * A_HEADS].reshape(
                -1, MLSTM_ROWS, 2 * A_HEADS).transpose(0, 2, 1)
            ha, hb_p, g, sp = even_mixer(zin, gates_t, 0, bp, Lp, (None, None), j, mC_p, *zeros_even, *small)
            ha, hb_s, g, ss = even_mixer(zin, gates_t, n_p, bs, Ls, (ha, g), j, mC_s, state_mlstm_C,
                                         state_mlstm_n[j], state_mlstm_m[j], state_rwkv_S[j], state_rwkv_shift[j],
                                         *small)
            mC_p, mC_s = sp[0], ss[0]
            st_p_even.append(sp)
            st_s_even.append(ss)
            wo = w_out_even[j].astype(BF16)
            terms_p = [(ha, None, wo[:A_V], None), (hb_p, g, wo[A_V:], Lp)]
            terms_s = [(ha, None, wo[:A_V], None), (hb_s, g, wo[A_V:], 0)]
            if x is None:
                xo = matmul_res(terms_p, xp, 0, n_p, None, n_tot)
                x = matmul_res(terms_s, xs, n_p, n_s, xo, n_tot)
            else:
                xo = matmul_res(terms_p, x, 0, n_p)
                x = matmul_res(terms_s, x, n_p, n_s, xo)
        else:
            zin = norm_matmul(x, norm_mix[i], _odd_w_in(w_in_odd[j]), 768)
            small = [t[j] for t in odd_small]
            mix, ssm_p, cv_p = ssd_mixer(zin, 0, bp, Lp, None, j, ssm_p, *zeros_odd, *small)
            mix, ssm_s, cv_s = ssd_mixer(zin, n_p, bs, Ls, mix, j, ssm_s, state_ssm, state_conv[j], *small)
            st_p_odd.append(cv_p)
            st_s_odd.append(cv_s)
            x = matmul_res([(mix, None, w_out_odd[j].astype(BF16), None)], x)
        tail = (norm_ffn[i], w_ffn_up[i].astype(BF16), w_ffn_down[i].astype(BF16),
                norm_ple[i], w_ple_gate[i].astype(BF16))
        wp = w_ple_proj[i].astype(BF16)
        if i < DEPTH - 1:
            xo = ffn_ple(x, *tail, pp[i], wp, 0)
            x = ffn_ple(x, *tail, ps[i], wp, n_p, xo)
        else:
            y_prompt = ffn_ple(x, *tail, pp[i], wp, 0, None, norm_final).reshape(bp, Lp, D_MODEL)
            y_sample = ffn_ple(x, *tail, ps[i], wp, n_p, None, norm_final).reshape(bs, Ls, D_MODEL)
    stack = lambda sts, idx: jnp.stack([s[idx] for s in sts])
    ssm_shape = lambda b: (N_ODD, b, C_HEADS, C_HEADDIM, C_STATE)
    return (y_prompt, y_sample,
            mC_p, stack(st_p_even, 1), stack(st_p_even, 2), stack(st_p_even, 3),
            stack(st_p_even, 4), ssm_p.reshape(ssm_shape(bp)), jnp.stack(st_p_odd),
            mC_s, stack(st_s_even, 1), stack(st_s_even, 2), stack(st_s_even, 3),
            stack(st_s_even, 4), ssm_s.reshape(ssm_shape(bs)), jnp.stack(st_s_odd))
```

```python
import math
import functools
import jax
import jax.numpy as jnp
from jax import lax
from jax.experimental import pallas as pl
from jax.experimental.pallas import tpu as pltpu

D_MODEL = 1024
DEPTH = 4
F32 = jnp.float32
BF16 = jnp.bfloat16
EPS = 1e-6
N_EVEN = (DEPTH + 1) // 2
N_ODD = DEPTH // 2
D_FF = 4 * D_MODEL
PLE_DIM = 256

A_HEADS = 4
A_DK = D_MODEL // 8
A_DV = D_MODEL // 8
A_QK = A_HEADS * A_DK
A_V = A_HEADS * A_DV
A_COLS = 2 * A_QK + 2 * A_V + 2 * A_HEADS

B_HEADS = 8
B_DH = 64
B_W = B_HEADS * B_DH
B_W_RANK = 64
B_A_RANK = 64
B_G_RANK = 128
B_COLS = 3 * B_W + B_W_RANK + B_A_RANK + B_G_RANK
B_DECAY_OFFSET = 0.5
B_GN_EPS = 64e-5

EVEN_COLS = A_COLS + B_COLS
EVEN_OUT = A_V + B_W

C_INNER = 2 * D_MODEL
C_HEADDIM = 64
C_HEADS = C_INNER // C_HEADDIM
C_GROUPS = 4
C_HPG = C_HEADS // C_GROUPS
C_STATE = 128
C_CONV = 4
C_CHUNK = 128
C_CONV_DIM = C_INNER + 2 * C_GROUPS * C_STATE
ODD_COLS = C_INNER + C_CONV_DIM + C_HEADS

LANES = 128
SUBLANES = 8
VMEM_LIMIT = 56 * 1024 * 1024
SCAN_VMEM_LIMIT = 60 * 1024 * 1024
TOKEN_TILE = 512

EVEN_GATE_PAD = 256
EVEN_N = B_COLS + EVEN_GATE_PAD + 2 * A_QK + 2 * A_V
EVEN_G_OFF = B_COLS
EVEN_A_OFF = B_COLS + EVEN_GATE_PAD
ODD_DT_PAD = 256
ODD_N = C_INNER + C_CONV_DIM + ODD_DT_PAD
ODD_DT_OFF = C_INNER + C_CONV_DIM


def _cparams(sem, vmem_limit=VMEM_LIMIT):
    return pltpu.CompilerParams(dimension_semantics=sem, vmem_limit_bytes=vmem_limit)


def _rms(x, g):
    return x * lax.rsqrt(jnp.mean(x * x, axis=-1, keepdims=True) + EPS) * g


def _resident(shape):
    nd = len(shape)
    return pl.BlockSpec(shape, lambda *_: (0,) * nd, pipeline_mode=pl.Buffered(1))


def _rows(width):
    return pl.BlockSpec((TOKEN_TILE, width), lambda i: (i, 0))


def _fill_into(body, n_in, into, out_idx, into2=None, out_idx2=None):
    pairs = [(a, o) for a, o in ((into, out_idx), (into2, out_idx2)) if a is not None]
    if not pairs:
        return body, [], [], {}

    def skipping(*refs):
        return body(*refs[:n_in], *refs[n_in + len(pairs):])

    return (skipping, [pl.BlockSpec(memory_space=pl.ANY)] * len(pairs), [a for a, _ in pairs],
            {n_in + i: o for i, (_, o) in enumerate(pairs)})


def _norm_matmul_kernel(x_ref, g_ref, w_ref, o_ref, *, tn):
    xn = _rms(x_ref[...], g_ref[...]).astype(BF16)
    for n0 in range(0, w_ref.shape[1], tn):
        o_ref[:, n0:n0 + tn] = jnp.dot(xn, w_ref[:, n0:n0 + tn], preferred_element_type=F32)


def norm_matmul(x, g, w, tn, row0=0, total=None, into=None):
    m, n = x.shape[0], w.shape[1]
    total = m if total is None else total
    t0 = row0 // TOKEN_TILE
    in_specs = [_rows(D_MODEL), _resident((1, D_MODEL)), _resident(w.shape)]
    body, xspecs, xargs, aliases = _fill_into(functools.partial(_norm_matmul_kernel, tn=tn), len(in_specs), into, 0)
    return pl.pallas_call(
        body,
        grid=(m // TOKEN_TILE,),
        in_specs=in_specs + xspecs,
        out_specs=pl.BlockSpec((TOKEN_TILE, n), lambda i: (t0 + i, 0)),
        out_shape=jax.ShapeDtypeStruct((total, n), F32),
        input_output_aliases=aliases,
        compiler_params=_cparams(("arbitrary",)),
        name="norm_matmul",
    )(x, g.reshape(1, D_MODEL), w, *xargs)


FFN_CHUNK = 512


def _matmul_res_kernel(*refs, kinds):
    x_ref, o_ref = refs[-2], refs[-1]
    acc = x_ref[...]
    pos = 0
    for has_gate, channel_major in kinds:
        a = refs[pos][...]
        if channel_major:
            a = a.T
        if has_gate:
            a = a * refs[pos + 1][...]
        w_ref = refs[pos + 1 + has_gate]
        pos += 2 + has_gate
        acc = acc + jnp.dot(a.astype(BF16), w_ref[...], preferred_element_type=F32)
    o_ref[...] = acc


def matmul_res(terms, x, row0=0, nrows=None, into=None, total=None):
    m = x.shape[0] if total is None else total
    nrows = m if nrows is None else nrows
    t0 = row0 // TOKEN_TILE
    rows = lambda width: pl.BlockSpec((TOKEN_TILE, width), lambda i: (t0 + i, 0))
    xspec = rows(D_MODEL) if total is None else _rows(D_MODEL)
    specs, args, kinds = [], [], []
    for a, gate, w, seq_len in terms:
        if seq_len is None:
            specs.append(rows(a.shape[1]))
        elif seq_len == 0:
            specs.append(pl.BlockSpec((TOKEN_TILE, a.shape[1]), lambda i: (i, 0)))
        else:
            assert seq_len % TOKEN_TILE == 0
            per_seq = seq_len // TOKEN_TILE
            specs.append(pl.BlockSpec((None, a.shape[1], TOKEN_TILE), lambda i: (i // per_seq, 0, i % per_seq)))
        specs += ([rows(w.shape[0])] if gate is not None else []) + [_resident(w.shape)]
        args += [a] + ([gate] if gate is not None else []) + [w]
        kinds.append((int(gate is not None), bool(seq_len)))
    specs.append(xspec)
    body, xspecs, xargs, aliases = _fill_into(
        functools.partial(_matmul_res_kernel, kinds=tuple(kinds)), len(specs), into, 0)
    return pl.pallas_call(
        body,
        grid=(nrows // TOKEN_TILE,),
        in_specs=specs + xspecs,
        out_specs=rows(D_MODEL),
        out_shape=jax.ShapeDtypeStruct((m, D_MODEL), F32),
        input_output_aliases=aliases,
        compiler_params=_cparams(("arbitrary",)),
        name="matmul_res",
    )(*args, x, *xargs)


def _ffn_ple_kernel(x_ref, gf_ref, wu_ref, wd_ref, gp_ref, wg_ref, p_ref, wp_ref, *rest):
    gl_ref = rest[0] if len(rest) == 3 else None
    o_ref, y_s = rest[-2:]
    x = x_ref[...]
    xn = _rms(x, gf_ref[...]).astype(BF16)
    y_s[...] = x
    for c0 in range(0, D_FF, FFN_CHUNK):
        h = jnp.dot(xn, wu_ref[:, c0:c0 + FFN_CHUNK], preferred_element_type=F32)
        h = jnp.square(jnp.maximum(h, 0.0)).astype(BF16)
        y_s[...] += jnp.dot(h, wd_ref[c0:c0 + FFN_CHUNK, :], preferred_element_type=F32)
    y = y_s[...]
    yn = _rms(y, gp_ref[...]).astype(BF16)
    gate = jax.nn.sigmoid(jnp.dot(yn, wg_ref[...], preferred_element_type=F32))
    proj = jnp.dot(p_ref[...].astype(BF16), wp_ref[...], preferred_element_type=F32)
    out = y + proj * gate
    o_ref[...] = out if gl_ref is None else _rms(out, gl_ref[...])


def ffn_ple(x, g_ffn, wu, wd, g_ple, wg, p, wp, row0, into=None, g_last=None):
    t0 = row0 // TOKEN_TILE
    rows = pl.BlockSpec((TOKEN_TILE, D_MODEL), lambda i: (t0 + i, 0))
    row1 = lambda v: v.reshape(1, D_MODEL)
    in_specs = [rows, _resident((1, D_MODEL)), _resident(wu.shape), _resident(wd.shape),
                _resident((1, D_MODEL)), _resident(wg.shape), _rows(PLE_DIM), _resident(wp.shape)]
    args = [x, row1(g_ffn), wu, wd, row1(g_ple), wg, p, wp]
    if g_last is not None:
        assert into is None
        in_specs.append(_resident((1, D_MODEL)))
        args.append(row1(g_last))
    body, xspecs, xargs, aliases = _fill_into(_ffn_ple_kernel, len(in_specs), into, 0)
    return pl.pallas_call(
        body,
        grid=(p.shape[0] // TOKEN_TILE,),
        in_specs=in_specs + xspecs,
        out_specs=rows if g_last is None else _rows(D_MODEL),
        out_shape=jax.ShapeDtypeStruct(x.shape if g_last is None else (p.shape[0], D_MODEL), F32),
        input_output_aliases=aliases,
        scratch_shapes=[pltpu.VMEM((TOKEN_TILE, D_MODEL), F32)],
        compiler_params=_cparams(("arbitrary",)),
        name="ffn_ple",
    )(*args, *xargs)


def _rwkv_operands(k, a, r, kk_t, ka_t, rk_t):
    kk = k * kk_t
    kk = kk * lax.rsqrt(jnp.maximum(jnp.sum(kk * kk, axis=1, keepdims=True), 1e-24))
    k2 = k * (1.0 + (a - 1.0) * ka_t)
    return -kk, kk * a, k2, jnp.sum(r * k2 * rk_t, axis=1)


def _rwkv_step(s_ref, tile0, ntiles, r, w, k, a, b, vrows):
    ys = []
    for lt in range(ntiles):
        rows = slice((tile0 + lt) * B_DH, (tile0 + lt + 1) * B_DH)
        s = s_ref[rows, :]
        sa = jnp.sum(s * a, axis=0, keepdims=True)
        s = s * w + sa * b + vrows[lt:lt + 1] * k
        s_ref[rows, :] = s
        ys.append(jnp.sum(s * r, axis=0, keepdims=True))
    return jnp.concatenate(ys, axis=0)


LONG_T = 128
LONG_NLT = B_DH // 2
KEY_PITCH = B_DH + SUBLANES
VAL_PITCH = LONG_NLT + SUBLANES


def _rwkv_scan_long_kernel(r_ref, w_ref, k_ref, a_ref, v_ref, kkt_ref, kat_ref, rkt_ref, lnw_ref, lnb_ref, s0_ref,
                           yt_ref, s_ref, or_s, ow_s, ok_s, oa_s, av_s, v_s, y_s):
    nb = yt_ref.shape[0]

    @pl.when(pl.program_id(0) == 0)
    def _():
        s_ref[...] = s0_ref[...]
        for scr in (or_s, ow_s, ok_s, oa_s, av_s, v_s, y_s):
            scr[...] = jnp.zeros(scr.shape, F32)

    lo = lax.broadcasted_iota(jnp.int32, (LONG_T, LANES), 1) < LANES // 2

    def tile(ref, c):
        m = ref[c]
        return jnp.concatenate([m, m], axis=0).T

    def relayout(c, carry):
        for ref, dst in ((r_ref, or_s), (w_ref, ow_s), (k_ref, ok_s), (a_ref, oa_s)):
            dst[pl.ds(c, LONG_T, stride=KEY_PITCH), :] = tile(ref, c)
            dst[pl.ds(c + LONG_NLT, LONG_T, stride=KEY_PITCH), :] = tile(ref, c + LONG_NLT)
        v_s[pl.ds(c, LONG_T, stride=VAL_PITCH), :] = jnp.where(lo, tile(v_ref, c), tile(v_ref, c + LONG_NLT))
        return carry

    lax.fori_loop(0, LONG_NLT, relayout, 0)

    def unpad(scr, pitch, n):
        return scr[...].reshape(LONG_T, pitch, LANES)[:, :n]

    def pad(x, pitch):
        zeros = jnp.zeros((LONG_T, pitch - x.shape[1], LANES), F32)
        return jnp.concatenate([x, zeros], axis=1).reshape(LONG_T * pitch, LANES)

    av, bv, k2, bonus = _rwkv_operands(unpad(ok_s, KEY_PITCH, B_DH), unpad(oa_s, KEY_PITCH, B_DH),
                                       unpad(or_s, KEY_PITCH, B_DH), kkt_ref[...], kat_ref[...], rkt_ref[...])
    av_s[...] = pad(av, KEY_PITCH)
    oa_s[...] = pad(bv, KEY_PITCH)
    ok_s[...] = pad(k2, KEY_PITCH)

    def step(t, carry):
        kr = pl.ds(pl.multiple_of(t * KEY_PITCH, SUBLANES), B_DH)
        vr = pl.ds(pl.multiple_of(t * VAL_PITCH, SUBLANES), LONG_NLT)
        y_s[vr, :] = _rwkv_step(s_ref, 0, LONG_NLT, or_s[kr, :], ow_s[kr, :], ok_s[kr, :], av_s[kr, :],
                                oa_s[kr, :], v_s[vr, :])
        return carry

    lax.fori_loop(0, LONG_T, step, 0)

    def head_sum(x):
        tot = jnp.sum(x, axis=1)
        return tot + pltpu.roll(tot, LANES // 2, axis=1)

    y = unpad(y_s, VAL_PITCH, LONG_NLT)
    d = y - (head_sum(y) * (1.0 / B_DH))[:, None, :]
    var = head_sum(d * d) * (1.0 / B_DH)
    y = (d * lax.rsqrt(var + B_GN_EPS)[:, None, :] * lnw_ref[...] + lnb_ref[...]
         + bonus[:, None, :] * unpad(v_s, VAL_PITCH, LONG_NLT))
    y_s[...] = pad(y, VAL_PITCH)

    def relayout_out(lt, carry):
        yt = y_s[pl.ds(lt, LONG_T, stride=VAL_PITCH), :].T
        for i2 in range(2):
            for b in range(nb):
                row = i2 * (LANES // 2) + b * B_HEADS
                yt_ref[b, pl.ds(lt + LONG_NLT * i2, B_HEADS, stride=B_DH), :] = yt[row:row + B_HEADS, :]
        return carry

    lax.fori_loop(0, LONG_NLT, relayout_out, 0)


def rwkv_scan_long(rc, wc, kc, ac, vc, S0, k_k, k_a, r_k, ln_w, ln_b):
    _, bh, L = rc.shape
    b = bh // B_HEADS
    assert bh * 2 == LANES and L % LONG_T == 0
    per_key = lambda p: jnp.tile(p.reshape(B_HEADS, B_DH).T, (1, LANES // B_HEADS))
    per_val = lambda p: jnp.concatenate(
        [jnp.tile(p.reshape(B_HEADS, 2, LONG_NLT)[:, i2].T, (1, b)) for i2 in range(2)], axis=1)
    s0 = S0.reshape(b, B_HEADS, 2, LONG_NLT, B_DH).transpose(3, 4, 2, 0, 1).reshape(LONG_NLT * B_DH, LANES)
    blk = pl.BlockSpec((B_DH, bh, LONG_T), lambda c: (0, 0, c))
    oblk = pl.BlockSpec((b, B_W, LONG_T), lambda c: (0, 0, c))
    const = lambda shape: pl.BlockSpec(shape, lambda c: (0,) * len(shape))
    big = pltpu.VMEM((LONG_T * KEY_PITCH, LANES), F32)
    small = pltpu.VMEM((LONG_T * VAL_PITCH, LANES), F32)
    yt, s = pl.pallas_call(
        _rwkv_scan_long_kernel,
        grid=(L // LONG_T,),
        in_specs=[blk] * 5 + [const((B_DH, LANES))] * 3 + [const((LONG_NLT, LANES))] * 2
        + [_resident((LONG_NLT * B_DH, LANES))],
        out_specs=[oblk, const((LONG_NLT * B_DH, LANES))],
        out_shape=[jax.ShapeDtypeStruct((b, B_W, L), F32), jax.ShapeDtypeStruct(s0.shape, F32)],
        scratch_shapes=[big] * 5 + [small, small],
        compiler_params=_cparams(("arbitrary",), SCAN_VMEM_LIMIT),
        name="rwkv_scan_long",
    )(rc, wc, kc, ac, vc, per_key(k_k), per_key(k_a), per_key(r_k), per_val(ln_w), per_val(ln_b), s0)
    s = s.reshape(LONG_NLT, B_DH, 2, b, B_HEADS).transpose(3, 4, 2, 0, 1).reshape(b, B_HEADS, B_DH, B_DH)
    return yt, s


def _rwkv_scan_short_kernel(r_ref, w_ref, k_ref, a_ref, v_ref, kkt_ref, kat_ref, rkt_ref, lnw_ref, lnb_ref, s0_ref,
                            y_ref, sout_ref, s_s, or_s, ow_s, ok_s, oa_s, av_s, ov_s, y_s, *, L):
    nt = 2 * B_DH
    for q in range(nt * B_DH // LANES):
        s_s[q * LANES:(q + 1) * LANES, :] = s0_ref[:, q * LANES:(q + 1) * LANES].T
    for ref, dst in ((r_ref, or_s), (w_ref, ow_s), (k_ref, ok_s), (a_ref, oa_s), (v_ref, ov_s)):
        for t in range(L):
            dst[t] = ref[pl.ds(t, LANES, stride=L), :].T.reshape(2, B_DH, LANES)
    shape3 = (L * 2, B_DH, LANES)
    tiles = lambda ref: jnp.concatenate([ref[...]] * L, axis=0)
    av, bv, k2, bonus = _rwkv_operands(ok_s[...].reshape(shape3), oa_s[...].reshape(shape3),
                                       or_s[...].reshape(shape3), tiles(kkt_ref), tiles(kat_ref), tiles(rkt_ref))
    av_s[...] = av.reshape(L, 2, B_DH, LANES)
    oa_s[...] = bv.reshape(L, 2, B_DH, LANES)
    ok_s[...] = k2.reshape(L, 2, B_DH, LANES)

    def step(t, carry):
        for h in range(2):
            y_s[t, h] = _rwkv_step(s_s, h * B_DH, B_DH, or_s[t, h], ow_s[t, h], ok_s[t, h], av_s[t, h],
                                   oa_s[t, h], ov_s[t, h])
        return carry

    lax.fori_loop(0, L, step, 0)

    y = y_s[...]
    d = y - jnp.mean(y, axis=2, keepdims=True)
    var = jnp.mean(d * d, axis=2, keepdims=True)
    y = (d * lax.rsqrt(var + B_GN_EPS) * lnw_ref[...] + lnb_ref[...]
         + bonus.reshape(L, 2, 1, LANES) * ov_s[...])
    for t in range(L):
        y_ref[pl.ds(t, LANES, stride=L), :] = y[t].reshape(nt, LANES).T
    for q in range(nt * B_DH // LANES):
        sout_ref[:, q * LANES:(q + 1) * LANES] = s_s[q * LANES:(q + 1) * LANES, :].T


def rwkv_scan_short(r, w, k, a, v, S_all, layer, s_into, L, k_k, k_a, r_k, ln_w, ln_b):
    n = r.shape[0]
    b = n // L
    assert b == LANES
    npair = B_HEADS // 2
    wide = lambda p: jnp.broadcast_to(p.reshape(npair, 2, B_DH, 1), (npair, 2, B_DH, LANES))
    blk = pl.BlockSpec((n, LANES), lambda p: (0, p))
    cblk = pl.BlockSpec((None, 2, B_DH, LANES), lambda p: (p, 0, 0, 0))
    sblk = pl.BlockSpec((None, b, 2 * B_DH * B_DH), lambda p: (layer, 0, p))
    op = pltpu.VMEM((L, 2, B_DH, LANES), F32)
    in_specs = [blk] * 5 + [cblk] * 5 + [sblk]
    body, xspecs, xargs, aliases = _fill_into(
        functools.partial(_rwkv_scan_short_kernel, L=L), len(in_specs), s_into, 1)
    return pl.pallas_call(
        body,
        grid=(npair,),
        in_specs=in_specs + xspecs,
        out_specs=[blk, sblk],
        out_shape=[jax.ShapeDtypeStruct((n, B_W), F32), jax.ShapeDtypeStruct(S_all.shape, F32)],
        input_output_aliases=aliases,
        scratch_shapes=[pltpu.VMEM((2 * B_DH * B_DH, LANES), F32)] + [op] * 7,
        compiler_params=_cparams(("arbitrary",)),
        name="rwkv_scan_short",
    )(r, w, k, a, v, wide(k_k), wide(k_a), wide(r_k), wide(ln_w), wide(ln_b), S_all, *xargs)


RWKV_PREP_ROWS = 256
RWKV_SHORT_ROWS = 64
B_LORA_OFF = 3 * B_W


def _rwkv_prep_kernel(z_ref, sh0_ref, mu_ref, wwa_ref, g2_ref, w0_ref, a0_ref,
                      r_ref, w_ref, k_ref, a_ref, v_ref, g_ref, sh_ref, *scratch, T, nseq, channel_major):
    @pl.when(pl.program_id(1) == 0)
    def _():
        sh_ref[...] = sh0_ref[...]

    z = z_ref[...]
    rowid = lax.broadcasted_iota(jnp.int32, (z.shape[0], 1), 0)
    zprev = pltpu.roll(z, 1, axis=0)
    for u in range(nseq):
        zprev = jnp.where(rowid == u * T, sh_ref[u], zprev)
    for u in range(nseq):
        sh_ref[u] = z[(u + 1) * T - 1:(u + 1) * T, :]
    zs = z + (zprev - z) * mu_ref[...]
    r = zs[:, :B_W]
    k = zs[:, B_W:2 * B_W]
    lora = zs[:, B_LORA_OFF:B_LORA_OFF + LANES]
    lane = lax.broadcasted_iota(jnp.int32, lora.shape, 1)
    lora = jnp.where(lane < B_W_RANK, jnp.tanh(lora), lora).astype(BF16)
    wa = jnp.dot(lora, wwa_ref[...], preferred_element_type=F32)
    w_log = -jax.nn.softplus(-(w0_ref[...] + wa[:, :B_W])) - B_DECAY_OFFSET
    a = jax.nn.sigmoid(a0_ref[...] + wa[:, B_W:])
    zg = zs[:, B_LORA_OFF + LANES:B_LORA_OFF + LANES + B_G_RANK]
    def emit(ref, x):
        if not channel_major:
            ref[...] = x
            return
        xt_s = scratch[0]
        xt = x.T
        for j in range(xt_s.shape[0]):
            for h in range(B_HEADS):
                xt_s[j, h * KEY_PITCH:h * KEY_PITCH + B_DH, :] = xt[h * B_DH:(h + 1) * B_DH, j * LANES:(j + 1) * LANES]
        for c in range(B_DH):
            for j in range(xt_s.shape[0]):
                ref[c, :, j * LANES:(j + 1) * LANES] = xt_s[j, pl.ds(c, B_HEADS, stride=KEY_PITCH), :]

    emit(r_ref, r)
    emit(w_ref, jnp.exp(-jnp.exp(w_log)))
    emit(k_ref, k)
    emit(a_ref, a)
    emit(v_ref, zs[:, 2 * B_W:3 * B_W])
    g_ref[...] = jnp.dot(jax.nn.sigmoid(zg).astype(BF16), g2_ref[...], preferred_element_type=F32)


def rwkv_prep(zin, row0, b, L, gate_into, shift0, mu, w0, w2, a0, a2, g2):
    assert B_W_RANK + B_A_RANK == LANES
    T = math.gcd(L, RWKV_PREP_ROWS)
    channel_major = T == RWKV_PREP_ROWS
    nseq = 1 if channel_major else RWKV_SHORT_ROWS // T
    assert nseq == 1 or (L == T and b % nseq == 0)
    R = T * nseq
    nchunk = L // T
    blk0 = row0 // R
    wwa = jnp.zeros((LANES, 2 * B_W), F32).at[:B_W_RANK, :B_W].set(w2).at[B_W_RANK:, B_W:].set(a2).astype(BF16)
    row = lambda v: v.reshape(1, -1)
    if channel_major:
        blk = pl.BlockSpec((B_DH, B_HEADS, R), lambda i, c: (0, i, c))
        oshape = jax.ShapeDtypeStruct((B_DH, b * B_HEADS, L), F32)
    else:
        blk = pl.BlockSpec((R, B_W), lambda i, c: (i * nchunk + c, 0))
        oshape = jax.ShapeDtypeStruct((b * L, B_W), F32)
    gblk = pl.BlockSpec((R, B_W), lambda i, c: (blk0 + i * nchunk + c, 0))
    shspec = pl.BlockSpec((nseq, 1, B_COLS), lambda i, c: (i, 0, 0))
    const = lambda shape: pl.BlockSpec(shape, lambda i, c: (0,) * len(shape))
    in_specs = [pl.BlockSpec((R, B_COLS), lambda i, c: (blk0 + i * nchunk + c, 0)), shspec, const((1, B_COLS)),
                const((LANES, 2 * B_W)), const((B_G_RANK, B_W)), const((1, B_W)), const((1, B_W))]
    body, xspecs, xargs, aliases = _fill_into(
        functools.partial(_rwkv_prep_kernel, T=T, nseq=nseq, channel_major=channel_major),
        len(in_specs), gate_into, 5)
    outs = pl.pallas_call(
        body,
        grid=(b // nseq, nchunk),
        in_specs=in_specs + xspecs,
        out_specs=[blk] * 5 + [gblk, shspec],
        out_shape=[oshape] * 5
        + [jax.ShapeDtypeStruct((zin.shape[0], B_W), F32), jax.ShapeDtypeStruct((b, 1, B_COLS), F32)],
        input_output_aliases=aliases,
        scratch_shapes=[pltpu.VMEM((R // LANES, B_HEADS * KEY_PITCH, LANES), F32)] if channel_major else [],
        compiler_params=_cparams(("arbitrary", "arbitrary")),
        name="rwkv_prep",
    )(zin, shift0.reshape(b, 1, B_COLS), row(mu), wwa, g2.astype(BF16), row(w0), row(a0), *xargs)
    return outs[:5], outs[5], outs[6].reshape(b, B_COLS)


MLSTM_ROWS = 128
HIGHEST = lax.Precision.HIGHEST


def _mlstm_kernel(q_ref, k_ref, v_ref, o_ref, g_ref, gt_ref, brow_ref, bcol_ref, nw_ref,
                  c0_ref, n0_ref, m0_ref, h_ref, c_ref, n_ref, m_ref, *, T, nseq):
    R = MLSTM_ROWS

    @pl.when(pl.program_id(1) == 0)
    def _():
        c_ref[...] = c0_ref[...]
        n_ref[...] = n0_ref[...]
        m_ref[...] = m0_ref[...]

    shift = T.bit_length() - 1
    ri = lax.broadcasted_iota(jnp.int32, (R, R), 0)
    ci = lax.broadcasted_iota(jnp.int32, (R, R), 1)
    mask = (ci <= ri) & (jnp.right_shift(ri, shift) == jnp.right_shift(ci, shift))
    lmat = mask.astype(F32)
    rowid = lax.broadcasted_iota(jnp.int32, (R, 1), 0)
    rsel = [(rowid >= u * T) & (rowid < (u + 1) * T) for u in range(nseq)]

    g = g_ref[...] + brow_ref[...]
    lane = lax.broadcasted_iota(jnp.int32, g.shape, 1)
    glog = jnp.where((lane >= A_HEADS) & (lane < 2 * A_HEADS), jax.nn.log_sigmoid(g), g)
    gt = gt_ref[...] + bcol_ref[...]
    sub = lax.broadcasted_iota(jnp.int32, gt.shape, 0)
    gtlog = jnp.where(sub >= A_HEADS, jax.nn.log_sigmoid(gt), gt)
    bc_col = jnp.dot(lmat, glog, precision=HIGHEST, preferred_element_type=F32)
    bc_row = lax.dot_general(gtlog, lmat, (((1,), (1,)), ((), ())), precision=HIGHEST,
                             preferred_element_type=F32)
    lane_m = lax.broadcasted_iota(jnp.int32, (1, LANES), 1)
    m_old = [m_ref[u] for u in range(nseq)]
    m_out = [jnp.zeros((1, LANES), F32) for _ in range(nseq)]

    for h in range(A_HEADS):
        hs = slice(h * A_DK, (h + 1) * A_DK)
        bcc = bc_col[:, A_HEADS + h:A_HEADS + h + 1]
        bcr = bc_row[A_HEADS + h:A_HEADS + h + 1, :]
        lir = gtlog[h:h + 1, :]
        lic = glog[:, h:h + 1]
        m_u = [m_old[u][:, h:h + 1] for u in range(nseq)]
        m_col = m_u[0]
        for u in range(1, nseq):
            m_col = jnp.where(rsel[u], m_u[u], m_col)
        dmat = jnp.where(mask, bcc - bcr + lir, -jnp.inf)
        inter = bcc + m_col
        mt = jnp.maximum(inter, jnp.max(dmat, axis=1, keepdims=True))
        p = jnp.exp(dmat - mt)
        qh = q_ref[:, hs] * (A_DK ** -0.5)
        kh = k_ref[:, hs]
        qb, kb, vb = qh.astype(BF16), kh.astype(BF16), v_ref[:, hs].astype(BF16)
        wq = lax.dot_general(qb, kb, (((1,), (1,)), ((), ())), preferred_element_type=F32) * p
        wi = jnp.exp(inter - mt)
        c_old = [c_ref[u, h] for u in range(nseq)]
        n_old = [n_ref[u, h:h + 1, :] for u in range(nseq)]
        qc = jnp.dot(qb, c_old[0].astype(BF16), preferred_element_type=F32)
        qn = jnp.sum(qh * n_old[0], axis=1, keepdims=True)
        for u in range(1, nseq):
            qc = jnp.where(rsel[u], jnp.dot(qb, c_old[u].astype(BF16), preferred_element_type=F32), qc)
            qn = jnp.where(rsel[u], jnp.sum(qh * n_old[u], axis=1, keepdims=True), qn)
        num = jnp.dot(wq.astype(BF16), vb, preferred_element_type=F32) + wi * qc
        den = jnp.sum(wq, axis=1, keepdims=True) + wi * qn
        hh = num / jnp.maximum(jnp.abs(den), jnp.exp(-mt))
        hh = hh * lax.rsqrt(jnp.mean(hh * hh, axis=-1, keepdims=True) + EPS)
        h_ref[:, hs] = (hh * nw_ref[:, hs] * jax.nn.sigmoid(o_ref[:, hs])).astype(BF16)
        for u in range(nseq):
            b_last = bcc[(u + 1) * T - 1:(u + 1) * T, :]
            gs = b_last - bcc + lic
            gmax = jnp.max(gs if nseq == 1 else jnp.where(rsel[u], gs, -jnp.inf), axis=0, keepdims=True)
            m_new = jnp.maximum(b_last + m_u[u], gmax)
            decay = jnp.exp(b_last + m_u[u] - m_new)
            ws = jnp.exp(gs - m_new)
            if nseq > 1:
                ws = jnp.where(rsel[u], ws, 0.0)
            kw = kh * ws
            c_ref[u, h] = decay * c_old[u] + lax.dot_general(
                kw.astype(BF16), vb, (((0,), (0,)), ((), ())), preferred_element_type=F32)
            n_ref[u, h:h + 1, :] = decay * n_old[u] + jnp.sum(kw, axis=0, keepdims=True)
            m_out[u] = jnp.where(lane_m == h, m_new, m_out[u])
    for u in range(nseq):
        m_ref[u] = m_out[u]


def mlstm(zin, gates_t, row0, b, L, h_into, layer, c_into, b_i, b_f, m_norm, C0, n0, m0):
    R = MLSTM_ROWS
    T = math.gcd(L, R)
    nseq = R // T
    assert nseq == 1 or (L == T and b % nseq == 0)
    nchunk = L // T
    blk0 = row0 // R
    bias = jnp.concatenate([b_i, b_f])
    bias_row = jnp.zeros((1, LANES), F32).at[0, :2 * A_HEADS].set(bias)
    bias_col = bias.reshape(2 * A_HEADS, 1)
    m0p = jnp.zeros((b, 1, LANES), F32).at[:, 0, :A_HEADS].set(m0)
    blk = lambda i, c: blk0 + i * nchunk + c
    rowblk = lambda col: pl.BlockSpec((R, A_QK), lambda i, c: (blk(i, c), EVEN_A_OFF // A_QK + col))
    cspec = pl.BlockSpec((nseq, A_HEADS, A_DK, A_DV), lambda i, c: (i, 0, 0, 0))
    cout = pl.BlockSpec((None, nseq, A_HEADS, A_DK, A_DV), lambda i, c: (layer, i, 0, 0, 0))
    nspec = pl.BlockSpec((nseq, A_HEADS, A_DK), lambda i, c: (i, 0, 0))
    mspec = pl.BlockSpec((nseq, 1, LANES), lambda i, c: (i, 0, 0))
    const = lambda shape: pl.BlockSpec(shape, lambda i, c: (0,) * len(shape))
    in_specs = [rowblk(0), rowblk(1), rowblk(2), rowblk(3),
                pl.BlockSpec((R, LANES), lambda i, c: (blk(i, c), EVEN_G_OFF // LANES)),
                pl.BlockSpec((None, 2 * A_HEADS, R), lambda i, c: (blk(i, c), 0, 0)),
                const((1, LANES)), const((2 * A_HEADS, 1)), const((1, A_V)),
                cspec if C0.ndim == 4 else cout, nspec, mspec]
    body, xspecs, xargs, aliases = _fill_into(
        functools.partial(_mlstm_kernel, T=T, nseq=nseq), len(in_specs), h_into, 0, c_into, 1)
    h, C, n, m = pl.pallas_call(
        body,
        grid=(b // nseq, nchunk),
        in_specs=in_specs + xspecs,
        out_specs=[pl.BlockSpec((R, A_V), lambda i, c: (blk(i, c), 0)), cout, nspec, mspec],
        out_shape=[jax.ShapeDtypeStruct((zin.shape[0], A_V), BF16),
                   jax.ShapeDtypeStruct((N_EVEN,) + C0.shape[-4:], F32), jax.ShapeDtypeStruct(n0.shape, F32),
                   jax.ShapeDtypeStruct(m0p.shape, F32)],
        input_output_aliases=aliases,
        compiler_params=_cparams(("arbitrary", "arbitrary")),
        name="mlstm",
    )(zin, zin, zin, zin, zin, gates_t, bias_row, bias_col, m_norm.reshape(1, A_V), C0, n0, m0p, *xargs)
    return h, C, n, m[:, 0, :A_HEADS]


def even_mixer(zin, gates_t, row0, b, L, into, layer, c_into, C0, n0, m0, S0, shift0, b_i, b_f, m_norm,
               mu, w0, w2, a0, a2, g2, k_k, k_a, r_k, ln_w, ln_b):
    hA, C, n, m = mlstm(zin, gates_t, row0, b, L, into[0], layer, c_into[0], b_i, b_f, m_norm, C0, n0, m0)
    ops, g, shift = rwkv_prep(zin, row0, b, L, into[1], shift0, mu, w0, w2, a0, a2, g2)
    if ops[0].ndim == 3:
        hB, S = rwkv_scan_long(*ops, S0, k_k, k_a, r_k, ln_w, ln_b)
    else:
        hB, S = rwkv_scan_short(*ops, S0, layer, c_into[1], L, k_k, k_a, r_k, ln_w, ln_b)
    return hA, hB, g, (C, n, m, S, shift)


C_PAIRS = C_HEADS // 2
C_GROUP_W = C_INNER // C_GROUPS
C_BC_W = 2 * C_GROUPS * C_STATE
SSD_LONG_ROWS = C_CHUNK
SSD_SHORT_ROWS = 32


def _ssd_kernel(z_ref, x_ref, bc_ref, dt_ref, dtt_ref, cw_ref, cb_ref, dtb_ref, dtbt_ref, al_ref, alt_ref,
                dsk_ref, nw_ref, s0_ref, cv0_ref, y_ref, s_ref, cv_ref, stg_ref, yf_s, *, T, nseq, nchunk):
    R = T * nseq
    tstate = nchunk > 1
    assert not tstate or nseq == 1

    @pl.when(pl.program_id(1) == 0)
    def _():
        if tstate:
            for pr in range(C_PAIRS):
                s_ref[0, pr] = s0_ref[0, pr].T
        else:
            s_ref[...] = s0_ref[...]
        cv_ref[...] = cv0_ref[...]

    shift = T.bit_length() - 1
    ri = lax.broadcasted_iota(jnp.int32, (R, R), 0)
    ci = lax.broadcasted_iota(jnp.int32, (R, R), 1)
    mask = (ci <= ri) & (jnp.right_shift(ri, shift) == jnp.right_shift(ci, shift))
    lmat = mask.astype(F32)
    rowid = lax.broadcasted_iota(jnp.int32, (R, 1), 0)
    rsel = [(rowid >= u * T) & (rowid < (u + 1) * T) for u in range(nseq)]

    def conv_silu(src_ref, src_col, col):
        cols = slice(col, col + LANES)
        accs = []
        for u in range(nseq):
            stg_ref[u, 0:SUBLANES, cols] = cv_ref[u, :, cols]
            stg_ref[u, SUBLANES:SUBLANES + T, cols] = src_ref[u * T:(u + 1) * T, src_col:src_col + LANES]
            acc = cb_ref[:, cols]
            for d in range(C_CONV):
                acc = acc + stg_ref[u, SUBLANES - d:SUBLANES - d + T, cols] * cw_ref[C_CONV - 1 - d:C_CONV - d, cols]
            accs.append(acc)
            cv_ref[u, :, cols] = stg_ref[u, T:T + SUBLANES, cols]
        acc = accs[0] if nseq == 1 else jnp.concatenate(accs, axis=0)
        return acc * jax.nn.sigmoid(acc)

    dtv = jax.nn.softplus(dt_ref[...] + dtb_ref[...])
    dtt = jax.nn.softplus(dtt_ref[...] + dtbt_ref[...])
    cum_col = jnp.dot(lmat, dtv * (-jnp.exp(al_ref[...])), precision=HIGHEST, preferred_element_type=F32)
    cum_row = lax.dot_general(dtt * (-jnp.exp(alt_ref[...])), lmat, (((1,), (1,)), ((), ())),
                              precision=HIGHEST, preferred_element_type=F32)

    lo = lax.broadcasted_iota(jnp.int32, (R, LANES), 1) < C_HEADDIM
    rlo = lax.broadcasted_iota(jnp.int32, (LANES, 1), 0) < C_HEADDIM
    nt = (((1,), (1,)), ((), ()))
    tn = (((0,), (0,)), ((), ()))
    pairs_per_group = C_PAIRS // C_GROUPS
    for g in range(C_GROUPS):
        bgf = conv_silu(bc_ref, g * C_STATE, C_INNER + g * C_STATE)
        bg = bgf.astype(BF16)
        cg = conv_silu(bc_ref, (C_GROUPS + g) * C_STATE, C_INNER + (C_GROUPS + g) * C_STATE).astype(BF16)
        cbm = lax.dot_general(cg, bg, nt, preferred_element_type=F32)
        ys = None
        if tstate:
            bgt = bgf.T.astype(BF16)
            sgt = jnp.concatenate([s_ref[0, g * pairs_per_group + q] for q in range(pairs_per_group)], axis=1)
            ys = jnp.dot(cg, sgt.astype(BF16), preferred_element_type=F32)
        for u in range(0 if tstate else nseq):
            sg = s_ref[u, g * pairs_per_group:(g + 1) * pairs_per_group].reshape(C_GROUP_W, C_STATE)
            t_u = lax.dot_general(cg, sg.astype(BF16), nt, preferred_element_type=F32)
            ys = t_u if u == 0 else jnp.where(rsel[u], t_u, ys)
        for q in range(pairs_per_group):
            pr = g * pairs_per_group + q
            ps = slice(pr * LANES, (pr + 1) * LANES)
            xp = conv_silu(x_ref, pr * LANES, pr * LANES)
            cc = [cum_col[:, 2 * pr + e:2 * pr + e + 1] for e in range(2)]
            intra = None
            for e, keep in ((0, lo), (1, jnp.logical_not(lo))):
                hh = 2 * pr + e
                seg = jnp.exp(jnp.where(mask, cc[e] - cum_row[hh:hh + 1, :], -jnp.inf))
                mix = cbm * seg * dtt[hh:hh + 1, :]
                part = jnp.dot(mix.astype(BF16), jnp.where(keep, xp, 0.0).astype(BF16),
                               preferred_element_type=F32)
                intra = part if intra is None else intra + part
            scale = jnp.where(lo, jnp.exp(cc[0]), jnp.exp(cc[1]))
            yp = intra + scale * ys[:, q * LANES:(q + 1) * LANES] + dsk_ref[:, ps] * xp
            zp = z_ref[:, ps]
            yf_s[:, ps] = yp * (zp * jax.nn.sigmoid(zp))
            for u in range(nseq):
                last = (u + 1) * T - 1
                ct = [cc[e][last:last + 1, :] for e in range(2)]
                tail = jnp.where(lo, jnp.exp(ct[0] - cc[0]) * dtv[:, 2 * pr:2 * pr + 1],
                                 jnp.exp(ct[1] - cc[1]) * dtv[:, 2 * pr + 1:2 * pr + 2])
                xw = xp * tail
                if nseq > 1:
                    xw = jnp.where(rsel[u], xw, 0.0)
                if tstate:
                    upd = jnp.dot(bgt, xw.astype(BF16), preferred_element_type=F32)
                    dec = jnp.where(lo[:1], jnp.exp(ct[0]), jnp.exp(ct[1]))
                else:
                    upd = lax.dot_general(xw.astype(BF16), bg, tn, preferred_element_type=F32)
                    dec = jnp.where(rlo, jnp.exp(ct[0]), jnp.exp(ct[1]))
                s_ref[u, pr] = dec * s_ref[u, pr] + upd

    if tstate:
        @pl.when(pl.program_id(1) == nchunk - 1)
        def _():
            for pr in range(C_PAIRS):
                s_ref[0, pr] = s_ref[0, pr].T

    for g in range(C_GROUPS):
        gs_ = slice(g * C_GROUP_W, (g + 1) * C_GROUP_W)
        yg = yf_s[:, gs_]
        yg = yg * lax.rsqrt(jnp.mean(yg * yg, axis=-1, keepdims=True) + EPS) * nw_ref[:, gs_]
        y_ref[:, gs_] = yg.astype(BF16)


def ssd_mixer(zin, row0, b, L, y_into, layer, s_into, ssm0, conv0, conv_w, conv_b, dt_bias, a_log, d_skip, norm_w):
    T = math.gcd(L, SSD_LONG_ROWS)
    nseq = 1 if T == SSD_LONG_ROWS else SSD_SHORT_ROWS // T
    assert nseq == 1 or (L == T and b % nseq == 0)
    R = T * nseq
    nchunk = L // T
    nblk = b * L // R
    blk0 = row0 // R
    dt_t = zin[row0:row0 + b * L, ODD_DT_OFF:ODD_DT_OFF + C_HEADS].reshape(nblk, R, C_HEADS).transpose(0, 2, 1)
    pad_row = lambda v: jnp.zeros((1, LANES), F32).at[0, :C_HEADS].set(v)
    pair_shape = (b, C_PAIRS, 2 * C_HEADDIM, C_STATE)
    per_layer = ssm0.ndim == 4
    s0 = ssm0.reshape(pair_shape if per_layer else (ssm0.shape[0],) + pair_shape)
    cv0 = jnp.concatenate([jnp.zeros((b, SUBLANES - (C_CONV - 1), C_CONV_DIM), F32), conv0], axis=1)
    blk = lambda i, c: blk0 + i * nchunk + c
    sspec = pl.BlockSpec((nseq, C_PAIRS, 2 * C_HEADDIM, C_STATE), lambda i, c: (i, 0, 0, 0))
    sout = pl.BlockSpec((None, nseq, C_PAIRS, 2 * C_HEADDIM, C_STATE), lambda i, c: (layer, i, 0, 0, 0))
    cvspec = pl.BlockSpec((nseq, SUBLANES, C_CONV_DIM), lambda i, c: (i, 0, 0))
    const = lambda shape: pl.BlockSpec(shape, lambda i, c: (0,) * len(shape))
    in_specs = [pl.BlockSpec((R, C_INNER), lambda i, c: (blk(i, c), 0)),
                pl.BlockSpec((R, C_INNER), lambda i, c: (blk(i, c), 1)),
                pl.BlockSpec((R, C_BC_W), lambda i, c: (blk(i, c), 2 * C_INNER // C_BC_W)),
                pl.BlockSpec((R, LANES), lambda i, c: (blk(i, c), ODD_DT_OFF // LANES)),
                pl.BlockSpec((None, C_HEADS, R), lambda i, c: (i * nchunk + c, 0, 0)),
                const((C_CONV, C_CONV_DIM)), const((1, C_CONV_DIM)),
                const((1, LANES)), const((C_HEADS, 1)), const((1, LANES)), const((C_HEADS, 1)),
                const((1, C_INNER)), const((1, C_INNER)), sspec if per_layer else sout, cvspec]
    body, xspecs, xargs, aliases = _fill_into(
        functools.partial(_ssd_kernel, T=T, nseq=nseq, nchunk=nchunk), len(in_specs), y_into, 0, s_into, 1)
    y, s, cv = pl.pallas_call(
        body,
        grid=(b // nseq, nchunk),
        in_specs=in_specs + xspecs,
        out_specs=[pl.BlockSpec((R, C_INNER), lambda i, c: (blk(i, c), 0)), sout, cvspec],
        out_shape=[jax.ShapeDtypeStruct((zin.shape[0], C_INNER), BF16),
                   jax.ShapeDtypeStruct((N_ODD,) + pair_shape, F32), jax.ShapeDtypeStruct(cv0.shape, F32)],
        input_output_aliases=aliases,
        scratch_shapes=[pltpu.VMEM((nseq, SUBLANES + T, C_CONV_DIM), F32), pltpu.VMEM((R, C_INNER), F32)],
        compiler_params=_cparams(("arbitrary", "arbitrary")),
        name="ssd",
    )(zin, zin, zin, zin, dt_t, conv_w, conv_b.reshape(1, C_CONV_DIM),
      pad_row(dt_bias), dt_bias.reshape(C_HEADS, 1), pad_row(a_log), a_log.reshape(C_HEADS, 1),
      jnp.repeat(d_skip, C_HEADDIM).reshape(1, C_INNER), norm_w.reshape(1, C_INNER), s0, cv0, *xargs)
    return y, s, cv[:, SUBLANES - (C_CONV - 1):]


def _even_w_in(w):
    qkvo = w[:, :2 * A_QK + 2 * A_V]
    gates = w[:, 2 * A_QK + 2 * A_V:A_COLS]
    rwkv = w[:, A_COLS:]
    pad = jnp.zeros((D_MODEL, EVEN_GATE_PAD - 2 * A_HEADS), w.dtype)
    return jnp.concatenate([rwkv, gates, pad, qkvo], axis=1).astype(BF16)


def _odd_w_in(w):
    pad = jnp.zeros((D_MODEL, ODD_DT_PAD - C_HEADS), w.dtype)
    return jnp.concatenate([w, pad], axis=1).astype(BF16)


def kernel(x_prompt, x_sample, state_mlstm_C, state_mlstm_n, state_mlstm_m, state_rwkv_S,
           state_rwkv_shift, state_ssm, state_conv, p_prompt, p_sample,
           norm_mix, norm_ffn, w_ffn_up, w_ffn_down, w_ple_proj, norm_ple, w_ple_gate, norm_final,
           w_in_even, mlstm_b_i, mlstm_b_f, mlstm_norm, rwkv_mu, rwkv_w0, rwkv_w2, rwkv_a0, rwkv_a2,
           rwkv_g2, rwkv_k_k, rwkv_k_a, rwkv_r_k, rwkv_ln_w, rwkv_ln_b, w_out_even,
           w_in_odd, conv_w, conv_b, dt_bias, a_log, d_skip, ssm_norm, w_out_odd):
    bp, Lp, _ = x_prompt.shape
    bs, Ls, _ = x_sample.shape
    n_p, n_s = bp * Lp, bs * Ls
    n_tot = n_p + n_s
    xp = x_prompt.reshape(n_p, D_MODEL)
    xs = x_sample.reshape(n_s, D_MODEL)
    pp = p_prompt.reshape(DEPTH, n_p, PLE_DIM)
    ps = p_sample.reshape(DEPTH, n_s, PLE_DIM)
    x = None

    even_small = (mlstm_b_i, mlstm_b_f, mlstm_norm, rwkv_mu, rwkv_w0, rwkv_w2, rwkv_a0, rwkv_a2,
                  rwkv_g2, rwkv_k_k, rwkv_k_a, rwkv_r_k, rwkv_ln_w, rwkv_ln_b)
    odd_small = (conv_w, conv_b, dt_bias, a_log, d_skip, ssm_norm)

    zeros_even = (jnp.zeros((bp, A_HEADS, A_DK, A_DV), F32), jnp.zeros((bp, A_HEADS, A_DK), F32),
                  jnp.zeros((bp, A_HEADS), F32), jnp.zeros((bp, B_HEADS, B_DH, B_DH), F32),
                  jnp.zeros((bp, B_COLS), F32))
    zeros_odd = (jnp.zeros((bp, C_HEADS, C_HEADDIM, C_STATE), F32),
                 jnp.zeros((bp, C_CONV - 1, C_CONV_DIM), F32))

    st_p_even, st_s_even, st_p_odd, st_s_odd = [], [], [], []
    mC_p = mC_s = rS_s = ssm_p = ssm_s = None
    rs_flat = state_rwkv_S.reshape(N_EVEN, bs, B_HEADS * B_DH * B_DH)
    for i in range(DEPTH):
        j = i // 2
        if i % 2 == 0:
            w_in = _even_w_in(w_in_even[j])
            if x is None:
                zin = norm_matmul(xp, norm_mix[i], w_in, 512, 0, n_tot)
                zin = norm_matmul(xs, norm_mix[i], w_in, 512, n_p, n_tot, zin)
            else:
                zin = norm_matmul(x, norm_mix[i], w_in, 512)
            small = [t[j] for t in even_small]
            gates_t = zin[:, EVEN_G_OFF:EVEN_G_OFF + 2 * A_HEADS].reshape(
                -1, MLSTM_ROWS, 2 * A_HEADS).transpose(0, 2, 1)
            ha, hb_p, g, sp = even_mixer(zin, gates_t, 0, bp, Lp, (None, None), j, (mC_p, None), *zeros_even, *small)
            ha, hb_s, g, ss = even_mixer(zin, gates_t, n_p, bs, Ls, (ha, g), j, (mC_s, rS_s), state_mlstm_C,
                                         state_mlstm_n[j], state_mlstm_m[j], rs_flat, state_rwkv_shift[j],
                                         *small)
            mC_p, mC_s, rS_s = sp[0], ss[0], ss[3]
            st_p_even.append(sp)
            st_s_even.append(ss)
            wo = w_out_even[j].astype(BF16)
            terms_p = [(ha, None, wo[:A_V], None), (hb_p, g, wo[A_V:], Lp)]
            terms_s = [(ha, None, wo[:A_V], None), (hb_s, g, wo[A_V:], 0)]
            if x is None:
                xo = matmul_res(terms_p, xp, 0, n_p, None, n_tot)
                x = matmul_res(terms_s, xs, n_p, n_s, xo, n_tot)
            else:
                xo = matmul_res(terms_p, x, 0, n_p)
                x = matmul_res(terms_s, x, n_p, n_s, xo)
        else:
            zin = norm_matmul(x, norm_mix[i], _odd_w_in(w_in_odd[j]), 768)
            small = [t[j] for t in odd_small]
            mix, ssm_p, cv_p = ssd_mixer(zin, 0, bp, Lp, None, j, ssm_p, *zeros_odd, *small)
            mix, ssm_s, cv_s = ssd_mixer(zin, n_p, bs, Ls, mix, j, ssm_s, state_ssm, state_conv[j], *small)
            st_p_odd.append(cv_p)
            st_s_odd.append(cv_s)
            x = matmul_res([(mix, None, w_out_odd[j].astype(BF16), None)], x)
        tail = (norm_ffn[i], w_ffn_up[i].astype(BF16), w_ffn_down[i].astype(BF16),
                norm_ple[i], w_ple_gate[i].astype(BF16))
        wp = w_ple_proj[i].astype(BF16)
        if i < DEPTH - 1:
            xo = ffn_ple(x, *tail, pp[i], wp, 0)
            x = ffn_ple(x, *tail, ps[i], wp, n_p, xo)
        else:
            y_prompt = ffn_ple(x, *tail, pp[i], wp, 0, None, norm_final).reshape(bp, Lp, D_MODEL)
            y_sample = ffn_ple(x, *tail, ps[i], wp, n_p, None, norm_final).reshape(bs, Ls, D_MODEL)
    stack = lambda sts, idx: jnp.stack([s[idx] for s in sts])
    ssm_shape = lambda b: (N_ODD, b, C_HEADS, C_HEADDIM, C_STATE)
    return (y_prompt, y_sample,
            mC_p, stack(st_p_even, 1), stack(st_p_even, 2), stack(st_p_even, 3),
            stack(st_p_even, 4), ssm_p.reshape(ssm_shape(bp)), jnp.stack(st_p_odd),
            mC_s, stack(st_s_even, 1), stack(st_s_even, 2), rS_s.reshape(state_rwkv_S.shape),
            stack(st_s_even, 4), ssm_s.reshape(ssm_shape(bs)), jnp.stack(st_s_odd))
```

```python
import math
import functools
import jax
import jax.numpy as jnp
from jax import lax
from jax.experimental import pallas as pl
from jax.experimental.pallas import tpu as pltpu

D_MODEL = 1024
DEPTH = 4
F32 = jnp.float32
BF16 = jnp.bfloat16
EPS = 1e-6
N_EVEN = (DEPTH + 1) // 2
N_ODD = DEPTH // 2
D_FF = 4 * D_MODEL
PLE_DIM = 256

A_HEADS = 4
A_DK = D_MODEL // 8
A_DV = D_MODEL // 8
A_QK = A_HEADS * A_DK
A_V = A_HEADS * A_DV
A_COLS = 2 * A_QK + 2 * A_V + 2 * A_HEADS

B_HEADS = 8
B_DH = 64
B_W = B_HEADS * B_DH
B_W_RANK = 64
B_A_RANK = 64
B_G_RANK = 128
B_COLS = 3 * B_W + B_W_RANK + B_A_RANK + B_G_RANK
B_DECAY_OFFSET = 0.5
B_GN_EPS = 64e-5

EVEN_COLS = A_COLS + B_COLS
EVEN_OUT = A_V + B_W

C_INNER = 2 * D_MODEL
C_HEADDIM = 64
C_HEADS = C_INNER // C_HEADDIM
C_GROUPS = 4
C_HPG = C_HEADS // C_GROUPS
C_STATE = 128
C_CONV = 4
C_CHUNK = 128
C_CONV_DIM = C_INNER + 2 * C_GROUPS * C_STATE
ODD_COLS = C_INNER + C_CONV_DIM + C_HEADS

LANES = 128
SUBLANES = 8
VMEM_LIMIT = 56 * 1024 * 1024
SCAN_VMEM_LIMIT = 60 * 1024 * 1024
TOKEN_TILE = 512

EVEN_GATE_PAD = 256
EVEN_N = B_COLS + EVEN_GATE_PAD + 2 * A_QK + 2 * A_V
EVEN_G_OFF = B_COLS
EVEN_A_OFF = B_COLS + EVEN_GATE_PAD
ODD_DT_PAD = 256
ODD_N = C_INNER + C_CONV_DIM + ODD_DT_PAD
ODD_DT_OFF = C_INNER + C_CONV_DIM


def _cparams(sem, vmem_limit=VMEM_LIMIT):
    return pltpu.CompilerParams(dimension_semantics=sem, vmem_limit_bytes=vmem_limit)


def _rms(x, g):
    return x * lax.rsqrt(jnp.mean(x * x, axis=-1, keepdims=True) + EPS) * g


def _resident(shape):
    nd = len(shape)
    return pl.BlockSpec(shape, lambda *_: (0,) * nd, pipeline_mode=pl.Buffered(1))


def _rows(width):
    return pl.BlockSpec((TOKEN_TILE, width), lambda i: (i, 0))


def _fill_into(body, n_in, into, out_idx, into2=None, out_idx2=None):
    pairs = [(a, o) for a, o in ((into, out_idx), (into2, out_idx2)) if a is not None]
    if not pairs:
        return body, [], [], {}

    def skipping(*refs):
        return body(*refs[:n_in], *refs[n_in + len(pairs):])

    return (skipping, [pl.BlockSpec(memory_space=pl.ANY)] * len(pairs), [a for a, _ in pairs],
            {n_in + i: o for i, (_, o) in enumerate(pairs)})


def _norm_matmul_kernel(x_ref, g_ref, w_ref, o_ref, *t_refs, tn, t_col):
    xn = _rms(x_ref[...], g_ref[...]).astype(BF16)
    for n0 in range(0, w_ref.shape[1], tn):
        o_ref[:, n0:n0 + tn] = jnp.dot(xn, w_ref[:, n0:n0 + tn], preferred_element_type=F32)
    for t_ref in t_refs:
        t_ref[...] = o_ref[:, t_col:t_col + LANES].T[:t_ref.shape[0]]


def norm_matmul(x, g, w, tn, row0=0, total=None, into=None, t_cols=None, into_t=None):
    m, n = x.shape[0], w.shape[1]
    total = m if total is None else total
    t0 = row0 // TOKEN_TILE
    in_specs = [_rows(D_MODEL), _resident((1, D_MODEL)), _resident(w.shape)]
    out_specs = [pl.BlockSpec((TOKEN_TILE, n), lambda i: (t0 + i, 0))]
    out_shape = [jax.ShapeDtypeStruct((total, n), F32)]
    if t_cols is not None:
        out_specs.append(pl.BlockSpec((t_cols[1], TOKEN_TILE), lambda i: (0, t0 + i)))
        out_shape.append(jax.ShapeDtypeStruct((t_cols[1], total), F32))
    body, xspecs, xargs, aliases = _fill_into(
        functools.partial(_norm_matmul_kernel, tn=tn, t_col=t_cols[0] if t_cols else 0),
        len(in_specs), into, 0, into_t, 1)
    outs = pl.pallas_call(
        body,
        grid=(m // TOKEN_TILE,),
        in_specs=in_specs + xspecs,
        out_specs=out_specs,
        out_shape=out_shape,
        input_output_aliases=aliases,
        compiler_params=_cparams(("arbitrary",)),
        name="norm_matmul",
    )(x, g.reshape(1, D_MODEL), w, *xargs)
    return outs if t_cols is not None else outs[0]


FFN_CHUNK = 512


def _matmul_res_kernel(*refs, kinds):
    x_ref, o_ref = refs[-2], refs[-1]
    acc = x_ref[...]
    pos = 0
    for has_gate, channel_major in kinds:
        a = refs[pos][...]
        if channel_major:
            a = a.T
        if has_gate:
            a = a * refs[pos + 1][...]
        w_ref = refs[pos + 1 + has_gate]
        pos += 2 + has_gate
        acc = acc + jnp.dot(a.astype(BF16), w_ref[...], preferred_element_type=F32)
    o_ref[...] = acc


def matmul_res(terms, x, row0=0, nrows=None, into=None, total=None):
    m = x.shape[0] if total is None else total
    nrows = m if nrows is None else nrows
    t0 = row0 // TOKEN_TILE
    rows = lambda width: pl.BlockSpec((TOKEN_TILE, width), lambda i: (t0 + i, 0))
    xspec = rows(D_MODEL) if total is None else _rows(D_MODEL)
    specs, args, kinds = [], [], []
    for a, gate, w, seq_len in terms:
        if seq_len is None:
            specs.append(rows(a.shape[1]))
        elif seq_len == 0:
            specs.append(pl.BlockSpec((TOKEN_TILE, a.shape[1]), lambda i: (i, 0)))
        else:
            assert seq_len % TOKEN_TILE == 0
            per_seq = seq_len // TOKEN_TILE
            specs.append(pl.BlockSpec((None, a.shape[1], TOKEN_TILE), lambda i: (i // per_seq, 0, i % per_seq)))
        specs += ([rows(w.shape[0])] if gate is not None else []) + [_resident(w.shape)]
        args += [a] + ([gate] if gate is not None else []) + [w]
        kinds.append((int(gate is not None), bool(seq_len)))
    specs.append(xspec)
    body, xspecs, xargs, aliases = _fill_into(
        functools.partial(_matmul_res_kernel, kinds=tuple(kinds)), len(specs), into, 0)
    return pl.pallas_call(
        body,
        grid=(nrows // TOKEN_TILE,),
        in_specs=specs + xspecs,
        out_specs=rows(D_MODEL),
        out_shape=jax.ShapeDtypeStruct((m, D_MODEL), F32),
        input_output_aliases=aliases,
        compiler_params=_cparams(("arbitrary",)),
        name="matmul_res",
    )(*args, x, *xargs)


def _ffn_ple_kernel(x_ref, gf_ref, wu_ref, wd_ref, gp_ref, wg_ref, p_ref, wp_ref, *rest):
    gl_ref = rest[0] if len(rest) == 3 else None
    o_ref, y_s = rest[-2:]
    x = x_ref[...]
    xn = _rms(x, gf_ref[...]).astype(BF16)
    y_s[...] = x
    for c0 in range(0, D_FF, FFN_CHUNK):
        h = jnp.dot(xn, wu_ref[:, c0:c0 + FFN_CHUNK], preferred_element_type=F32)
        h = jnp.square(jnp.maximum(h, 0.0)).astype(BF16)
        y_s[...] += jnp.dot(h, wd_ref[c0:c0 + FFN_CHUNK, :], preferred_element_type=F32)
    y = y_s[...]
    yn = _rms(y, gp_ref[...]).astype(BF16)
    gate = jax.nn.sigmoid(jnp.dot(yn, wg_ref[...], preferred_element_type=F32))
    proj = jnp.dot(p_ref[...].astype(BF16), wp_ref[...], preferred_element_type=F32)
    out = y + proj * gate
    o_ref[...] = out if gl_ref is None else _rms(out, gl_ref[...])


def ffn_ple(x, g_ffn, wu, wd, g_ple, wg, p, wp, row0, into=None, g_last=None):
    t0 = row0 // TOKEN_TILE
    rows = pl.BlockSpec((TOKEN_TILE, D_MODEL), lambda i: (t0 + i, 0))
    row1 = lambda v: v.reshape(1, D_MODEL)
    in_specs = [rows, _resident((1, D_MODEL)), _resident(wu.shape), _resident(wd.shape),
                _resident((1, D_MODEL)), _resident(wg.shape), _rows(PLE_DIM), _resident(wp.shape)]
    args = [x, row1(g_ffn), wu, wd, row1(g_ple), wg, p, wp]
    if g_last is not None:
        assert into is None
        in_specs.append(_resident((1, D_MODEL)))
        args.append(row1(g_last))
    body, xspecs, xargs, aliases = _fill_into(_ffn_ple_kernel, len(in_specs), into, 0)
    return pl.pallas_call(
        body,
        grid=(p.shape[0] // TOKEN_TILE,),
        in_specs=in_specs + xspecs,
        out_specs=rows if g_last is None else _rows(D_MODEL),
        out_shape=jax.ShapeDtypeStruct(x.shape if g_last is None else (p.shape[0], D_MODEL), F32),
        input_output_aliases=aliases,
        scratch_shapes=[pltpu.VMEM((TOKEN_TILE, D_MODEL), F32)],
        compiler_params=_cparams(("arbitrary",)),
        name="ffn_ple",
    )(*args, *xargs)


def _rwkv_operands(k, a, r, kk_t, ka_t, rk_t):
    kk = k * kk_t
    kk = kk * lax.rsqrt(jnp.maximum(jnp.sum(kk * kk, axis=1, keepdims=True), 1e-24))
    k2 = k * (1.0 + (a - 1.0) * ka_t)
    return -kk, kk * a, k2, jnp.sum(r * k2 * rk_t, axis=1)


def _rwkv_step(s_ref, tile0, ntiles, r, w, k, a, b, vrows):
    ys = []
    for lt in range(ntiles):
        rows = slice((tile0 + lt) * B_DH, (tile0 + lt + 1) * B_DH)
        s = s_ref[rows, :]
        sa = jnp.sum(s * a, axis=0, keepdims=True)
        s = s * w + sa * b + vrows[lt:lt + 1] * k
        s_ref[rows, :] = s
        ys.append(jnp.sum(s * r, axis=0, keepdims=True))
    return jnp.concatenate(ys, axis=0)


LONG_T = 128
LONG_NLT = B_DH // 2
KEY_PITCH = B_DH + SUBLANES
VAL_PITCH = LONG_NLT + SUBLANES


def _rwkv_scan_long_kernel(r_ref, w_ref, k_ref, a_ref, v_ref, kkt_ref, kat_ref, rkt_ref, lnw_ref, lnb_ref, s0_ref,
                           yt_ref, s_ref, or_s, ow_s, ok_s, oa_s, av_s, v_s, y_s):
    nb = yt_ref.shape[0]

    @pl.when(pl.program_id(0) == 0)
    def _():
        s_ref[...] = s0_ref[...]
        for scr in (or_s, ow_s, ok_s, oa_s, av_s, v_s, y_s):
            scr[...] = jnp.zeros(scr.shape, F32)

    lo = lax.broadcasted_iota(jnp.int32, (LONG_T, LANES), 1) < LANES // 2

    def tile(ref, c):
        m = ref[c]
        return jnp.concatenate([m, m], axis=0).T

    def relayout(c, carry):
        for ref, dst in ((r_ref, or_s), (w_ref, ow_s), (k_ref, ok_s), (a_ref, oa_s)):
            dst[pl.ds(c, LONG_T, stride=KEY_PITCH), :] = tile(ref, c)
            dst[pl.ds(c + LONG_NLT, LONG_T, stride=KEY_PITCH), :] = tile(ref, c + LONG_NLT)
        v_s[pl.ds(c, LONG_T, stride=VAL_PITCH), :] = jnp.where(lo, tile(v_ref, c), tile(v_ref, c + LONG_NLT))
        return carry

    lax.fori_loop(0, LONG_NLT, relayout, 0)

    def unpad(scr, pitch, n):
        return scr[...].reshape(LONG_T, pitch, LANES)[:, :n]

    def pad(x, pitch):
        zeros = jnp.zeros((LONG_T, pitch - x.shape[1], LANES), F32)
        return jnp.concatenate([x, zeros], axis=1).reshape(LONG_T * pitch, LANES)

    av, bv, k2, bonus = _rwkv_operands(unpad(ok_s, KEY_PITCH, B_DH), unpad(oa_s, KEY_PITCH, B_DH),
                                       unpad(or_s, KEY_PITCH, B_DH), kkt_ref[...], kat_ref[...], rkt_ref[...])
    av_s[...] = pad(av, KEY_PITCH)
    oa_s[...] = pad(bv, KEY_PITCH)
    ok_s[...] = pad(k2, KEY_PITCH)

    def step(t, carry):
        kr = pl.ds(pl.multiple_of(t * KEY_PITCH, SUBLANES), B_DH)
        vr = pl.ds(pl.multiple_of(t * VAL_PITCH, SUBLANES), LONG_NLT)
        y_s[vr, :] = _rwkv_step(s_ref, 0, LONG_NLT, or_s[kr, :], ow_s[kr, :], ok_s[kr, :], av_s[kr, :],
                                oa_s[kr, :], v_s[vr, :])
        return carry

    lax.fori_loop(0, LONG_T, step, 0)

    def head_sum(x):
        tot = jnp.sum(x, axis=1)
        return tot + pltpu.roll(tot, LANES // 2, axis=1)

    y = unpad(y_s, VAL_PITCH, LONG_NLT)
    d = y - (head_sum(y) * (1.0 / B_DH))[:, None, :]
    var = head_sum(d * d) * (1.0 / B_DH)
    y = (d * lax.rsqrt(var + B_GN_EPS)[:, None, :] * lnw_ref[...] + lnb_ref[...]
         + bonus[:, None, :] * unpad(v_s, VAL_PITCH, LONG_NLT))
    y_s[...] = pad(y, VAL_PITCH)

    def relayout_out(lt, carry):
        yt = y_s[pl.ds(lt, LONG_T, stride=VAL_PITCH), :].T
        for i2 in range(2):
            for b in range(nb):
                row = i2 * (LANES // 2) + b * B_HEADS
                yt_ref[b, pl.ds(lt + LONG_NLT * i2, B_HEADS, stride=B_DH), :] = yt[row:row + B_HEADS, :]
        return carry

    lax.fori_loop(0, LONG_NLT, relayout_out, 0)


def rwkv_scan_long(rc, wc, kc, ac, vc, S0, k_k, k_a, r_k, ln_w, ln_b):
    _, bh, L = rc.shape
    b = bh // B_HEADS
    assert bh * 2 == LANES and L % LONG_T == 0
    per_key = lambda p: jnp.tile(p.reshape(B_HEADS, B_DH).T, (1, LANES // B_HEADS))
    per_val = lambda p: jnp.concatenate(
        [jnp.tile(p.reshape(B_HEADS, 2, LONG_NLT)[:, i2].T, (1, b)) for i2 in range(2)], axis=1)
    s0 = S0.reshape(b, B_HEADS, 2, LONG_NLT, B_DH).transpose(3, 4, 2, 0, 1).reshape(LONG_NLT * B_DH, LANES)
    blk = pl.BlockSpec((B_DH, bh, LONG_T), lambda c: (0, 0, c))
    oblk = pl.BlockSpec((b, B_W, LONG_T), lambda c: (0, 0, c))
    const = lambda shape: pl.BlockSpec(shape, lambda c: (0,) * len(shape))
    big = pltpu.VMEM((LONG_T * KEY_PITCH, LANES), F32)
    small = pltpu.VMEM((LONG_T * VAL_PITCH, LANES), F32)
    yt, s = pl.pallas_call(
        _rwkv_scan_long_kernel,
        grid=(L // LONG_T,),
        in_specs=[blk] * 5 + [const((B_DH, LANES))] * 3 + [const((LONG_NLT, LANES))] * 2
        + [_resident((LONG_NLT * B_DH, LANES))],
        out_specs=[oblk, const((LONG_NLT * B_DH, LANES))],
        out_shape=[jax.ShapeDtypeStruct((b, B_W, L), F32), jax.ShapeDtypeStruct(s0.shape, F32)],
        scratch_shapes=[big] * 5 + [small, small],
        compiler_params=_cparams(("arbitrary",), SCAN_VMEM_LIMIT),
        name="rwkv_scan_long",
    )(rc, wc, kc, ac, vc, per_key(k_k), per_key(k_a), per_key(r_k), per_val(ln_w), per_val(ln_b), s0)
    s = s.reshape(LONG_NLT, B_DH, 2, b, B_HEADS).transpose(3, 4, 2, 0, 1).reshape(b, B_HEADS, B_DH, B_DH)
    return yt, s


def _rwkv_scan_short_kernel(r_ref, w_ref, k_ref, a_ref, v_ref, kkt_ref, kat_ref, rkt_ref, lnw_ref, lnb_ref, s0_ref,
                            y_ref, sout_ref, s_s, or_s, ow_s, ok_s, oa_s, av_s, ov_s, y_s, *, L):
    nt = 2 * B_DH
    for q in range(nt * B_DH // LANES):
        s_s[q * LANES:(q + 1) * LANES, :] = s0_ref[:, q * LANES:(q + 1) * LANES].T
    for ref, dst in ((r_ref, or_s), (w_ref, ow_s), (k_ref, ok_s), (a_ref, oa_s), (v_ref, ov_s)):
        for t in range(L):
            dst[t] = ref[pl.ds(t, LANES, stride=L), :].T.reshape(2, B_DH, LANES)
    shape3 = (L * 2, B_DH, LANES)
    tiles = lambda ref: jnp.concatenate([ref[...]] * L, axis=0)
    av, bv, k2, bonus = _rwkv_operands(ok_s[...].reshape(shape3), oa_s[...].reshape(shape3),
                                       or_s[...].reshape(shape3), tiles(kkt_ref), tiles(kat_ref), tiles(rkt_ref))
    av_s[...] = av.reshape(L, 2, B_DH, LANES)
    oa_s[...] = bv.reshape(L, 2, B_DH, LANES)
    ok_s[...] = k2.reshape(L, 2, B_DH, LANES)

    def step(t, carry):
        for h in range(2):
            y_s[t, h] = _rwkv_step(s_s, h * B_DH, B_DH, or_s[t, h], ow_s[t, h], ok_s[t, h], av_s[t, h],
                                   oa_s[t, h], ov_s[t, h])
        return carry

    lax.fori_loop(0, L, step, 0)

    y = y_s[...]
    d = y - jnp.mean(y, axis=2, keepdims=True)
    var = jnp.mean(d * d, axis=2, keepdims=True)
    y = (d * lax.rsqrt(var + B_GN_EPS) * lnw_ref[...] + lnb_ref[...]
         + bonus.reshape(L, 2, 1, LANES) * ov_s[...])
    for t in range(L):
        y_ref[pl.ds(t, LANES, stride=L), :] = y[t].reshape(nt, LANES).T
    for q in range(nt * B_DH // LANES):
        sout_ref[:, q * LANES:(q + 1) * LANES] = s_s[q * LANES:(q + 1) * LANES, :].T


def rwkv_scan_short(r, w, k, a, v, S_all, layer, s_into, L, k_k, k_a, r_k, ln_w, ln_b):
    n = r.shape[0]
    b = n // L
    assert b == LANES
    npair = B_HEADS // 2
    wide = lambda p: jnp.broadcast_to(p.reshape(npair, 2, B_DH, 1), (npair, 2, B_DH, LANES))
    blk = pl.BlockSpec((n, LANES), lambda p: (0, p))
    cblk = pl.BlockSpec((None, 2, B_DH, LANES), lambda p: (p, 0, 0, 0))
    sblk = pl.BlockSpec((None, b, 2 * B_DH * B_DH), lambda p: (layer, 0, p))
    op = pltpu.VMEM((L, 2, B_DH, LANES), F32)
    in_specs = [blk] * 5 + [cblk] * 5 + [sblk]
    body, xspecs, xargs, aliases = _fill_into(
        functools.partial(_rwkv_scan_short_kernel, L=L), len(in_specs), s_into, 1)
    return pl.pallas_call(
        body,
        grid=(npair,),
        in_specs=in_specs + xspecs,
        out_specs=[blk, sblk],
        out_shape=[jax.ShapeDtypeStruct((n, B_W), F32), jax.ShapeDtypeStruct(S_all.shape, F32)],
        input_output_aliases=aliases,
        scratch_shapes=[pltpu.VMEM((2 * B_DH * B_DH, LANES), F32)] + [op] * 7,
        compiler_params=_cparams(("arbitrary",)),
        name="rwkv_scan_short",
    )(r, w, k, a, v, wide(k_k), wide(k_a), wide(r_k), wide(ln_w), wide(ln_b), S_all, *xargs)


RWKV_PREP_ROWS = 256
RWKV_SHORT_ROWS = 64
B_LORA_OFF = 3 * B_W


def _rwkv_prep_kernel(z_ref, sh0_ref, mu_ref, wwa_ref, g2_ref, w0_ref, a0_ref,
                      r_ref, w_ref, k_ref, a_ref, v_ref, g_ref, sh_ref, *scratch, T, nseq, channel_major):
    @pl.when(pl.program_id(1) == 0)
    def _():
        sh_ref[...] = sh0_ref[...]

    z = z_ref[...]
    rowid = lax.broadcasted_iota(jnp.int32, (z.shape[0], 1), 0)
    zprev = pltpu.roll(z, 1, axis=0)
    for u in range(nseq):
        zprev = jnp.where(rowid == u * T, sh_ref[u], zprev)
    for u in range(nseq):
        sh_ref[u] = z[(u + 1) * T - 1:(u + 1) * T, :]
    zs = z + (zprev - z) * mu_ref[...]
    r = zs[:, :B_W]
    k = zs[:, B_W:2 * B_W]
    lora = zs[:, B_LORA_OFF:B_LORA_OFF + LANES]
    lane = lax.broadcasted_iota(jnp.int32, lora.shape, 1)
    lora = jnp.where(lane < B_W_RANK, jnp.tanh(lora), lora).astype(BF16)
    wa = jnp.dot(lora, wwa_ref[...], preferred_element_type=F32)
    w_log = -jax.nn.softplus(-(w0_ref[...] + wa[:, :B_W])) - B_DECAY_OFFSET
    a = jax.nn.sigmoid(a0_ref[...] + wa[:, B_W:])
    zg = zs[:, B_LORA_OFF + LANES:B_LORA_OFF + LANES + B_G_RANK]
    def emit(ref, x):
        if not channel_major:
            ref[...] = x
            return
        xt_s = scratch[0]
        xt = x.T
        for j in range(xt_s.shape[0]):
            for h in range(B_HEADS):
                xt_s[j, h * KEY_PITCH:h * KEY_PITCH + B_DH, :] = xt[h * B_DH:(h + 1) * B_DH, j * LANES:(j + 1) * LANES]
        for c in range(B_DH):
            for j in range(xt_s.shape[0]):
                ref[c, :, j * LANES:(j + 1) * LANES] = xt_s[j, pl.ds(c, B_HEADS, stride=KEY_PITCH), :]

    emit(r_ref, r)
    emit(w_ref, jnp.exp(-jnp.exp(w_log)))
    emit(k_ref, k)
    emit(a_ref, a)
    emit(v_ref, zs[:, 2 * B_W:3 * B_W])
    g_ref[...] = jnp.dot(jax.nn.sigmoid(zg).astype(BF16), g2_ref[...], preferred_element_type=F32)


def rwkv_prep(zin, row0, b, L, gate_into, shift0, mu, w0, w2, a0, a2, g2):
    assert B_W_RANK + B_A_RANK == LANES
    T = math.gcd(L, RWKV_PREP_ROWS)
    channel_major = T == RWKV_PREP_ROWS
    nseq = 1 if channel_major else RWKV_SHORT_ROWS // T
    assert nseq == 1 or (L == T and b % nseq == 0)
    R = T * nseq
    nchunk = L // T
    blk0 = row0 // R
    wwa = jnp.zeros((LANES, 2 * B_W), F32).at[:B_W_RANK, :B_W].set(w2).at[B_W_RANK:, B_W:].set(a2).astype(BF16)
    row = lambda v: v.reshape(1, -1)
    if channel_major:
        blk = pl.BlockSpec((B_DH, B_HEADS, R), lambda i, c: (0, i, c))
        oshape = jax.ShapeDtypeStruct((B_DH, b * B_HEADS, L), F32)
    else:
        blk = pl.BlockSpec((R, B_W), lambda i, c: (i * nchunk + c, 0))
        oshape = jax.ShapeDtypeStruct((b * L, B_W), F32)
    gblk = pl.BlockSpec((R, B_W), lambda i, c: (blk0 + i * nchunk + c, 0))
    shspec = pl.BlockSpec((nseq, 1, B_COLS), lambda i, c: (i, 0, 0))
    const = lambda shape: pl.BlockSpec(shape, lambda i, c: (0,) * len(shape))
    in_specs = [pl.BlockSpec((R, B_COLS), lambda i, c: (blk0 + i * nchunk + c, 0)), shspec, const((1, B_COLS)),
                const((LANES, 2 * B_W)), const((B_G_RANK, B_W)), const((1, B_W)), const((1, B_W))]
    body, xspecs, xargs, aliases = _fill_into(
        functools.partial(_rwkv_prep_kernel, T=T, nseq=nseq, channel_major=channel_major),
        len(in_specs), gate_into, 5)
    outs = pl.pallas_call(
        body,
        grid=(b // nseq, nchunk),
        in_specs=in_specs + xspecs,
        out_specs=[blk] * 5 + [gblk, shspec],
        out_shape=[oshape] * 5
        + [jax.ShapeDtypeStruct((zin.shape[0], B_W), F32), jax.ShapeDtypeStruct((b, 1, B_COLS), F32)],
        input_output_aliases=aliases,
        scratch_shapes=[pltpu.VMEM((R // LANES, B_HEADS * KEY_PITCH, LANES), F32)] if channel_major else [],
        compiler_params=_cparams(("arbitrary", "arbitrary")),
        name="rwkv_prep",
    )(zin, shift0.reshape(b, 1, B_COLS), row(mu), wwa, g2.astype(BF16), row(w0), row(a0), *xargs)
    return outs[:5], outs[5], outs[6].reshape(b, B_COLS)


MLSTM_ROWS = 128
HIGHEST = lax.Precision.HIGHEST


def _mlstm_kernel(q_ref, k_ref, v_ref, o_ref, g_ref, gt_ref, brow_ref, bcol_ref, nw_ref,
                  c0_ref, n0_ref, m0_ref, h_ref, c_ref, n_ref, m_ref, *, T, nseq):
    R = MLSTM_ROWS

    @pl.when(pl.program_id(1) == 0)
    def _():
        c_ref[...] = c0_ref[...]
        n_ref[...] = n0_ref[...]
        m_ref[...] = m0_ref[...]

    shift = T.bit_length() - 1
    ri = lax.broadcasted_iota(jnp.int32, (R, R), 0)
    ci = lax.broadcasted_iota(jnp.int32, (R, R), 1)
    mask = (ci <= ri) & (jnp.right_shift(ri, shift) == jnp.right_shift(ci, shift))
    lmat = mask.astype(F32)
    rowid = lax.broadcasted_iota(jnp.int32, (R, 1), 0)
    rsel = [(rowid >= u * T) & (rowid < (u + 1) * T) for u in range(nseq)]

    g = g_ref[...] + brow_ref[...]
    lane = lax.broadcasted_iota(jnp.int32, g.shape, 1)
    glog = jnp.where((lane >= A_HEADS) & (lane < 2 * A_HEADS), jax.nn.log_sigmoid(g), g)
    gt = gt_ref[...] + bcol_ref[...]
    sub = lax.broadcasted_iota(jnp.int32, gt.shape, 0)
    gtlog = jnp.where(sub >= A_HEADS, jax.nn.log_sigmoid(gt), gt)
    bc_col = jnp.dot(lmat, glog, precision=HIGHEST, preferred_element_type=F32)
    bc_row = lax.dot_general(gtlog, lmat, (((1,), (1,)), ((), ())), precision=HIGHEST,
                             preferred_element_type=F32)
    lane_m = lax.broadcasted_iota(jnp.int32, (1, LANES), 1)
    m_old = [m_ref[u] for u in range(nseq)]
    m_out = [jnp.zeros((1, LANES), F32) for _ in range(nseq)]

    for h in range(A_HEADS):
        hs = slice(h * A_DK, (h + 1) * A_DK)
        bcc = bc_col[:, A_HEADS + h:A_HEADS + h + 1]
        bcr = bc_row[A_HEADS + h:A_HEADS + h + 1, :]
        lir = gtlog[h:h + 1, :]
        lic = glog[:, h:h + 1]
        m_u = [m_old[u][:, h:h + 1] for u in range(nseq)]
        m_col = m_u[0]
        for u in range(1, nseq):
            m_col = jnp.where(rsel[u], m_u[u], m_col)
        dmat = jnp.where(mask, bcc - bcr + lir, -jnp.inf)
        inter = bcc + m_col
        mt = jnp.maximum(inter, jnp.max(dmat, axis=1, keepdims=True))
        p = jnp.exp(dmat - mt)
        qh = q_ref[:, hs] * (A_DK ** -0.5)
        kh = k_ref[:, hs]
        qb, kb, vb = qh.astype(BF16), kh.astype(BF16), v_ref[:, hs].astype(BF16)
        wq = lax.dot_general(qb, kb, (((1,), (1,)), ((), ())), preferred_element_type=F32) * p
        wi = jnp.exp(inter - mt)
        c_old = [c_ref[u, h] for u in range(nseq)]
        n_old = [n_ref[u, h:h + 1, :] for u in range(nseq)]
        qc = jnp.dot(qb, c_old[0].astype(BF16), preferred_element_type=F32)
        qn = jnp.sum(qh * n_old[0], axis=1, keepdims=True)
        for u in range(1, nseq):
            qc = jnp.where(rsel[u], jnp.dot(qb, c_old[u].astype(BF16), preferred_element_type=F32), qc)
            qn = jnp.where(rsel[u], jnp.sum(qh * n_old[u], axis=1, keepdims=True), qn)
        num = jnp.dot(wq.astype(BF16), vb, preferred_element_type=F32) + wi * qc
        den = jnp.sum(wq, axis=1, keepdims=True) + wi * qn
        hh = num / jnp.maximum(jnp.abs(den), jnp.exp(-mt))
        hh = hh * lax.rsqrt(jnp.mean(hh * hh, axis=-1, keepdims=True) + EPS)
        h_ref[:, hs] = (hh * nw_ref[:, hs] * jax.nn.sigmoid(o_ref[:, hs])).astype(BF16)
        for u in range(nseq):
            b_last = bcc[(u + 1) * T - 1:(u + 1) * T, :]
            gs = b_last - bcc + lic
            gmax = jnp.max(gs if nseq == 1 else jnp.where(rsel[u], gs, -jnp.inf), axis=0, keepdims=True)
            m_new = jnp.maximum(b_last + m_u[u], gmax)
            decay = jnp.exp(b_last + m_u[u] - m_new)
            ws = jnp.exp(gs - m_new)
            if nseq > 1:
                ws = jnp.where(rsel[u], ws, 0.0)
            kw = kh * ws
            c_ref[u, h] = decay * c_old[u] + lax.dot_general(
                kw.astype(BF16), vb, (((0,), (0,)), ((), ())), preferred_element_type=F32)
            n_ref[u, h:h + 1, :] = decay * n_old[u] + jnp.sum(kw, axis=0, keepdims=True)
            m_out[u] = jnp.where(lane_m == h, m_new, m_out[u])
    for u in range(nseq):
        m_ref[u] = m_out[u]


def mlstm(zin, gates_t, row0, b, L, h_into, layer, c_into, b_i, b_f, m_norm, C0, n0, m0):
    R = MLSTM_ROWS
    T = math.gcd(L, R)
    nseq = R // T
    assert nseq == 1 or (L == T and b % nseq == 0)
    nchunk = L // T
    blk0 = row0 // R
    bias = jnp.concatenate([b_i, b_f])
    bias_row = jnp.zeros((1, LANES), F32).at[0, :2 * A_HEADS].set(bias)
    bias_col = bias.reshape(2 * A_HEADS, 1)
    m0p = jnp.zeros((b, 1, LANES), F32).at[:, 0, :A_HEADS].set(m0)
    blk = lambda i, c: blk0 + i * nchunk + c
    rowblk = lambda col: pl.BlockSpec((R, A_QK), lambda i, c: (blk(i, c), EVEN_A_OFF // A_QK + col))
    cspec = pl.BlockSpec((nseq, A_HEADS, A_DK, A_DV), lambda i, c: (i, 0, 0, 0))
    cout = pl.BlockSpec((None, nseq, A_HEADS, A_DK, A_DV), lambda i, c: (layer, i, 0, 0, 0))
    nspec = pl.BlockSpec((nseq, A_HEADS, A_DK), lambda i, c: (i, 0, 0))
    mspec = pl.BlockSpec((nseq, 1, LANES), lambda i, c: (i, 0, 0))
    const = lambda shape: pl.BlockSpec(shape, lambda i, c: (0,) * len(shape))
    in_specs = [rowblk(0), rowblk(1), rowblk(2), rowblk(3),
                pl.BlockSpec((R, LANES), lambda i, c: (blk(i, c), EVEN_G_OFF // LANES)),
                pl.BlockSpec((2 * A_HEADS, R), lambda i, c: (0, blk(i, c))),
                const((1, LANES)), const((2 * A_HEADS, 1)), const((1, A_V)),
                cspec if C0.ndim == 4 else cout, nspec, mspec]
    body, xspecs, xargs, aliases = _fill_into(
        functools.partial(_mlstm_kernel, T=T, nseq=nseq), len(in_specs), h_into, 0, c_into, 1)
    h, C, n, m = pl.pallas_call(
        body,
        grid=(b // nseq, nchunk),
        in_specs=in_specs + xspecs,
        out_specs=[pl.BlockSpec((R, A_V), lambda i, c: (blk(i, c), 0)), cout, nspec, mspec],
        out_shape=[jax.ShapeDtypeStruct((zin.shape[0], A_V), BF16),
                   jax.ShapeDtypeStruct((N_EVEN,) + C0.shape[-4:], F32), jax.ShapeDtypeStruct(n0.shape, F32),
                   jax.ShapeDtypeStruct(m0p.shape, F32)],
        input_output_aliases=aliases,
        compiler_params=_cparams(("arbitrary", "arbitrary")),
        name="mlstm",
    )(zin, zin, zin, zin, zin, gates_t, bias_row, bias_col, m_norm.reshape(1, A_V), C0, n0, m0p, *xargs)
    return h, C, n, m[:, 0, :A_HEADS]


def even_mixer(zin, gates_t, row0, b, L, into, layer, c_into, C0, n0, m0, S0, shift0, b_i, b_f, m_norm,
               mu, w0, w2, a0, a2, g2, k_k, k_a, r_k, ln_w, ln_b):
    hA, C, n, m = mlstm(zin, gates_t, row0, b, L, into[0], layer, c_into[0], b_i, b_f, m_norm, C0, n0, m0)
    ops, g, shift = rwkv_prep(zin, row0, b, L, into[1], shift0, mu, w0, w2, a0, a2, g2)
    if ops[0].ndim == 3:
        hB, S = rwkv_scan_long(*ops, S0, k_k, k_a, r_k, ln_w, ln_b)
    else:
        hB, S = rwkv_scan_short(*ops, S0, layer, c_into[1], L, k_k, k_a, r_k, ln_w, ln_b)
    return hA, hB, g, (C, n, m, S, shift)


C_PAIRS = C_HEADS // 2
C_GROUP_W = C_INNER // C_GROUPS
C_BC_W = 2 * C_GROUPS * C_STATE
SSD_LONG_ROWS = C_CHUNK
SSD_SHORT_ROWS = 32


def _ssd_kernel(z_ref, x_ref, bc_ref, dt_ref, dtt_ref, cw_ref, cb_ref, dtb_ref, dtbt_ref, al_ref, alt_ref,
                dsk_ref, nw_ref, s0_ref, cv0_ref, y_ref, s_ref, cv_ref, stg_ref, yf_s, *, T, nseq, nchunk):
    R = T * nseq
    tstate = nchunk > 1
    assert not tstate or nseq == 1

    @pl.when(pl.program_id(1) == 0)
    def _():
        if tstate:
            for pr in range(C_PAIRS):
                s_ref[0, pr] = s0_ref[0, pr].T
        else:
            s_ref[...] = s0_ref[...]
        cv_ref[...] = cv0_ref[...]

    shift = T.bit_length() - 1
    ri = lax.broadcasted_iota(jnp.int32, (R, R), 0)
    ci = lax.broadcasted_iota(jnp.int32, (R, R), 1)
    mask = (ci <= ri) & (jnp.right_shift(ri, shift) == jnp.right_shift(ci, shift))
    lmat = mask.astype(F32)
    rowid = lax.broadcasted_iota(jnp.int32, (R, 1), 0)
    rsel = [(rowid >= u * T) & (rowid < (u + 1) * T) for u in range(nseq)]

    def conv_silu(src_ref, src_col, col):
        cols = slice(col, col + LANES)
        accs = []
        for u in range(nseq):
            stg_ref[u, 0:SUBLANES, cols] = cv_ref[u, :, cols]
            stg_ref[u, SUBLANES:SUBLANES + T, cols] = src_ref[u * T:(u + 1) * T, src_col:src_col + LANES]
            acc = cb_ref[:, cols]
            for d in range(C_CONV):
                acc = acc + stg_ref[u, SUBLANES - d:SUBLANES - d + T, cols] * cw_ref[C_CONV - 1 - d:C_CONV - d, cols]
            accs.append(acc)
            cv_ref[u, :, cols] = stg_ref[u, T:T + SUBLANES, cols]
        acc = accs[0] if nseq == 1 else jnp.concatenate(accs, axis=0)
        return acc * jax.nn.sigmoid(acc)

    dtv = jax.nn.softplus(dt_ref[...] + dtb_ref[...])
    dtt = jax.nn.softplus(dtt_ref[...] + dtbt_ref[...])
    cum_col = jnp.dot(lmat, dtv * (-jnp.exp(al_ref[...])), precision=HIGHEST, preferred_element_type=F32)
    cum_row = lax.dot_general(dtt * (-jnp.exp(alt_ref[...])), lmat, (((1,), (1,)), ((), ())),
                              precision=HIGHEST, preferred_element_type=F32)

    lo = lax.broadcasted_iota(jnp.int32, (R, LANES), 1) < C_HEADDIM
    rlo = lax.broadcasted_iota(jnp.int32, (LANES, 1), 0) < C_HEADDIM
    nt = (((1,), (1,)), ((), ()))
    tn = (((0,), (0,)), ((), ()))
    pairs_per_group = C_PAIRS // C_GROUPS
    for g in range(C_GROUPS):
        bgf = conv_silu(bc_ref, g * C_STATE, C_INNER + g * C_STATE)
        bg = bgf.astype(BF16)
        cg = conv_silu(bc_ref, (C_GROUPS + g) * C_STATE, C_INNER + (C_GROUPS + g) * C_STATE).astype(BF16)
        cbm = lax.dot_general(cg, bg, nt, preferred_element_type=F32)
        ys = None
        if tstate:
            bgt = bgf.T.astype(BF16)
            sgt = jnp.concatenate([s_ref[0, g * pairs_per_group + q] for q in range(pairs_per_group)], axis=1)
            ys = jnp.dot(cg, sgt.astype(BF16), preferred_element_type=F32)
        for u in range(0 if tstate else nseq):
            sg = s_ref[u, g * pairs_per_group:(g + 1) * pairs_per_group].reshape(C_GROUP_W, C_STATE)
            t_u = lax.dot_general(cg, sg.astype(BF16), nt, preferred_element_type=F32)
            ys = t_u if u == 0 else jnp.where(rsel[u], t_u, ys)
        for q in range(pairs_per_group):
            pr = g * pairs_per_group + q
            ps = slice(pr * LANES, (pr + 1) * LANES)
            xp = conv_silu(x_ref, pr * LANES, pr * LANES)
            cc = [cum_col[:, 2 * pr + e:2 * pr + e + 1] for e in range(2)]
            intra = None
            for e, keep in ((0, lo), (1, jnp.logical_not(lo))):
                hh = 2 * pr + e
                seg = jnp.exp(jnp.where(mask, cc[e] - cum_row[hh:hh + 1, :], -jnp.inf))
                mix = cbm * seg * dtt[hh:hh + 1, :]
                part = jnp.dot(mix.astype(BF16), jnp.where(keep, xp, 0.0).astype(BF16),
                               preferred_element_type=F32)
                intra = part if intra is None else intra + part
            scale = jnp.where(lo, jnp.exp(cc[0]), jnp.exp(cc[1]))
            yp = intra + scale * ys[:, q * LANES:(q + 1) * LANES] + dsk_ref[:, ps] * xp
            zp = z_ref[:, ps]
            yf_s[:, ps] = yp * (zp * jax.nn.sigmoid(zp))
            for u in range(nseq):
                last = (u + 1) * T - 1
                ct = [cc[e][last:last + 1, :] for e in range(2)]
                tail = jnp.where(lo, jnp.exp(ct[0] - cc[0]) * dtv[:, 2 * pr:2 * pr + 1],
                                 jnp.exp(ct[1] - cc[1]) * dtv[:, 2 * pr + 1:2 * pr + 2])
                xw = xp * tail
                if nseq > 1:
                    xw = jnp.where(rsel[u], xw, 0.0)
                if tstate:
                    upd = jnp.dot(bgt, xw.astype(BF16), preferred_element_type=F32)
                    dec = jnp.where(lo[:1], jnp.exp(ct[0]), jnp.exp(ct[1]))
                else:
                    upd = lax.dot_general(xw.astype(BF16), bg, tn, preferred_element_type=F32)
                    dec = jnp.where(rlo, jnp.exp(ct[0]), jnp.exp(ct[1]))
                s_ref[u, pr] = dec * s_ref[u, pr] + upd

    if tstate:
        @pl.when(pl.program_id(1) == nchunk - 1)
        def _():
            for pr in range(C_PAIRS):
                s_ref[0, pr] = s_ref[0, pr].T

    for g in range(C_GROUPS):
        gs_ = slice(g * C_GROUP_W, (g + 1) * C_GROUP_W)
        yg = yf_s[:, gs_]
        yg = yg * lax.rsqrt(jnp.mean(yg * yg, axis=-1, keepdims=True) + EPS) * nw_ref[:, gs_]
        y_ref[:, gs_] = yg.astype(BF16)


def ssd_mixer(zin, row0, b, L, y_into, layer, s_into, ssm0, conv0, conv_w, conv_b, dt_bias, a_log, d_skip, norm_w):
    T = math.gcd(L, SSD_LONG_ROWS)
    nseq = 1 if T == SSD_LONG_ROWS else SSD_SHORT_ROWS // T
    assert nseq == 1 or (L == T and b % nseq == 0)
    R = T * nseq
    nchunk = L // T
    nblk = b * L // R
    blk0 = row0 // R
    dt_t = zin[row0:row0 + b * L, ODD_DT_OFF:ODD_DT_OFF + C_HEADS].reshape(nblk, R, C_HEADS).transpose(0, 2, 1)
    pad_row = lambda v: jnp.zeros((1, LANES), F32).at[0, :C_HEADS].set(v)
    pair_shape = (b, C_PAIRS, 2 * C_HEADDIM, C_STATE)
    per_layer = ssm0.ndim == 4
    s0 = ssm0.reshape(pair_shape if per_layer else (ssm0.shape[0],) + pair_shape)
    cv0 = jnp.concatenate([jnp.zeros((b, SUBLANES - (C_CONV - 1), C_CONV_DIM), F32), conv0], axis=1)
    blk = lambda i, c: blk0 + i * nchunk + c
    sspec = pl.BlockSpec((nseq, C_PAIRS, 2 * C_HEADDIM, C_STATE), lambda i, c: (i, 0, 0, 0))
    sout = pl.BlockSpec((None, nseq, C_PAIRS, 2 * C_HEADDIM, C_STATE), lambda i, c: (layer, i, 0, 0, 0))
    cvspec = pl.BlockSpec((nseq, SUBLANES, C_CONV_DIM), lambda i, c: (i, 0, 0))
    const = lambda shape: pl.BlockSpec(shape, lambda i, c: (0,) * len(shape))
    in_specs = [pl.BlockSpec((R, C_INNER), lambda i, c: (blk(i, c), 0)),
                pl.BlockSpec((R, C_INNER), lambda i, c: (blk(i, c), 1)),
                pl.BlockSpec((R, C_BC_W), lambda i, c: (blk(i, c), 2 * C_INNER // C_BC_W)),
                pl.BlockSpec((R, LANES), lambda i, c: (blk(i, c), ODD_DT_OFF // LANES)),
                pl.BlockSpec((None, C_HEADS, R), lambda i, c: (i * nchunk + c, 0, 0)),
                const((C_CONV, C_CONV_DIM)), const((1, C_CONV_DIM)),
                const((1, LANES)), const((C_HEADS, 1)), const((1, LANES)), const((C_HEADS, 1)),
                const((1, C_INNER)), const((1, C_INNER)), sspec if per_layer else sout, cvspec]
    body, xspecs, xargs, aliases = _fill_into(
        functools.partial(_ssd_kernel, T=T, nseq=nseq, nchunk=nchunk), len(in_specs), y_into, 0, s_into, 1)
    y, s, cv = pl.pallas_call(
        body,
        grid=(b // nseq, nchunk),
        in_specs=in_specs + xspecs,
        out_specs=[pl.BlockSpec((R, C_INNER), lambda i, c: (blk(i, c), 0)), sout, cvspec],
        out_shape=[jax.ShapeDtypeStruct((zin.shape[0], C_INNER), BF16),
                   jax.ShapeDtypeStruct((N_ODD,) + pair_shape, F32), jax.ShapeDtypeStruct(cv0.shape, F32)],
        input_output_aliases=aliases,
        scratch_shapes=[pltpu.VMEM((nseq, SUBLANES + T, C_CONV_DIM), F32), pltpu.VMEM((R, C_INNER), F32)],
        compiler_params=_cparams(("arbitrary", "arbitrary")),
        name="ssd",
    )(zin, zin, zin, zin, dt_t, conv_w, conv_b.reshape(1, C_CONV_DIM),
      pad_row(dt_bias), dt_bias.reshape(C_HEADS, 1), pad_row(a_log), a_log.reshape(C_HEADS, 1),
      jnp.repeat(d_skip, C_HEADDIM).reshape(1, C_INNER), norm_w.reshape(1, C_INNER), s0, cv0, *xargs)
    return y, s, cv[:, SUBLANES - (C_CONV - 1):]


def _even_w_in(w):
    qkvo = w[:, :2 * A_QK + 2 * A_V]
    gates = w[:, 2 * A_QK + 2 * A_V:A_COLS]
    rwkv = w[:, A_COLS:]
    pad = jnp.zeros((D_MODEL, EVEN_GATE_PAD - 2 * A_HEADS), w.dtype)
    return jnp.concatenate([rwkv, gates, pad, qkvo], axis=1).astype(BF16)


def _odd_w_in(w):
    pad = jnp.zeros((D_MODEL, ODD_DT_PAD - C_HEADS), w.dtype)
    return jnp.concatenate([w, pad], axis=1).astype(BF16)


def kernel(x_prompt, x_sample, state_mlstm_C, state_mlstm_n, state_mlstm_m, state_rwkv_S,
           state_rwkv_shift, state_ssm, state_conv, p_prompt, p_sample,
           norm_mix, norm_ffn, w_ffn_up, w_ffn_down, w_ple_proj, norm_ple, w_ple_gate, norm_final,
           w_in_even, mlstm_b_i, mlstm_b_f, mlstm_norm, rwkv_mu, rwkv_w0, rwkv_w2, rwkv_a0, rwkv_a2,
           rwkv_g2, rwkv_k_k, rwkv_k_a, rwkv_r_k, rwkv_ln_w, rwkv_ln_b, w_out_even,
           w_in_odd, conv_w, conv_b, dt_bias, a_log, d_skip, ssm_norm, w_out_odd):
    bp, Lp, _ = x_prompt.shape
    bs, Ls, _ = x_sample.shape
    n_p, n_s = bp * Lp, bs * Ls
    n_tot = n_p + n_s
    xp = x_prompt.reshape(n_p, D_MODEL)
    xs = x_sample.reshape(n_s, D_MODEL)
    pp = p_prompt.reshape(DEPTH, n_p, PLE_DIM)
    ps = p_sample.reshape(DEPTH, n_s, PLE_DIM)
    x = None

    even_small = (mlstm_b_i, mlstm_b_f, mlstm_norm, rwkv_mu, rwkv_w0, rwkv_w2, rwkv_a0, rwkv_a2,
                  rwkv_g2, rwkv_k_k, rwkv_k_a, rwkv_r_k, rwkv_ln_w, rwkv_ln_b)
    odd_small = (conv_w, conv_b, dt_bias, a_log, d_skip, ssm_norm)

    zeros_even = (jnp.zeros((bp, A_HEADS, A_DK, A_DV), F32), jnp.zeros((bp, A_HEADS, A_DK), F32),
                  jnp.zeros((bp, A_HEADS), F32), jnp.zeros((bp, B_HEADS, B_DH, B_DH), F32),
                  jnp.zeros((bp, B_COLS), F32))
    zeros_odd = (jnp.zeros((bp, C_HEADS, C_HEADDIM, C_STATE), F32),
                 jnp.zeros((bp, C_CONV - 1, C_CONV_DIM), F32))

    st_p_even, st_s_even, st_p_odd, st_s_odd = [], [], [], []
    mC_p = mC_s = rS_s = ssm_p = ssm_s = None
    rs_flat = state_rwkv_S.reshape(N_EVEN, bs, B_HEADS * B_DH * B_DH)
    for i in range(DEPTH):
        j = i // 2
        if i % 2 == 0:
            w_in = _even_w_in(w_in_even[j])
            gate_cols = (EVEN_G_OFF, 2 * A_HEADS)
            if x is None:
                zin, gates_t = norm_matmul(xp, norm_mix[i], w_in, 512, 0, n_tot, None, gate_cols)
                zin, gates_t = norm_matmul(xs, norm_mix[i], w_in, 512, n_p, n_tot, zin, gate_cols, gates_t)
            else:
                zin, gates_t = norm_matmul(x, norm_mix[i], w_in, 512, t_cols=gate_cols)
            small = [t[j] for t in even_small]
            ha, hb_p, g, sp = even_mixer(zin, gates_t, 0, bp, Lp, (None, None), j, (mC_p, None), *zeros_even, *small)
            ha, hb_s, g, ss = even_mixer(zin, gates_t, n_p, bs, Ls, (ha, g), j, (mC_s, rS_s), state_mlstm_C,
                                         state_mlstm_n[j], state_mlstm_m[j], rs_flat, state_rwkv_shift[j],
                                         *small)
            mC_p, mC_s, rS_s = sp[0], ss[0], ss[3]
            st_p_even.append(sp)
            st_s_even.append(ss)
            wo = w_out_even[j].astype(BF16)
            terms_p = [(ha, None, wo[:A_V], None), (hb_p, g, wo[A_V:], Lp)]
            terms_s = [(ha, None, wo[:A_V], None), (hb_s, g, wo[A_V:], 0)]
            if x is None:
                xo = matmul_res(terms_p, xp, 0, n_p, None, n_tot)
                x = matmul_res(terms_s, xs, n_p, n_s, xo, n_tot)
            else:
                xo = matmul_res(terms_p, x, 0, n_p)
                x = matmul_res(terms_s, x, n_p, n_s, xo)
        else:
            zin = norm_matmul(x, norm_mix[i], _odd_w_in(w_in_odd[j]), 768)
            small = [t[j] for t in odd_small]
            mix, ssm_p, cv_p = ssd_mixer(zin, 0, bp, Lp, None, j, ssm_p, *zeros_odd, *small)
            mix, ssm_s, cv_s = ssd_mixer(zin, n_p, bs, Ls, mix, j, ssm_s, state_ssm, state_conv[j], *small)
            st_p_odd.append(cv_p)
            st_s_odd.append(cv_s)
            x = matmul_res([(mix, None, w_out_odd[j].astype(BF16), None)], x)
        tail = (norm_ffn[i], w_ffn_up[i].astype(BF16), w_ffn_down[i].astype(BF16),
                norm_ple[i], w_ple_gate[i].astype(BF16))
        wp = w_ple_proj[i].astype(BF16)
        if i < DEPTH - 1:
            xo = ffn_ple(x, *tail, pp[i], wp, 0)
            x = ffn_ple(x, *tail, ps[i], wp, n_p, xo)
        else:
            y_prompt = ffn_ple(x, *tail, pp[i], wp, 0, None, norm_final).reshape(bp, Lp, D_MODEL)
            y_sample = ffn_ple(x, *tail, ps[i], wp, n_p, None, norm_final).reshape(bs, Ls, D_MODEL)
    stack = lambda sts, idx: jnp.stack([s[idx] for s in sts])
    ssm_shape = lambda b: (N_ODD, b, C_HEADS, C_HEADDIM, C_STATE)
    return (y_prompt, y_sample,
            mC_p, stack(st_p_even, 1), stack(st_p_even, 2), stack(st_p_even, 3),
            stack(st_p_even, 4), ssm_p.reshape(ssm_shape(bp)), jnp.stack(st_p_odd),
            mC_s, stack(st_s_even, 1), stack(st_s_even, 2), rS_s.reshape(state_rwkv_S.shape),
            stack(st_s_even, 4), ssm_s.reshape(ssm_shape(bs)), jnp.stack(st_s_odd))
```

```python
import math
import functools
import jax
import jax.numpy as jnp
from jax import lax
from jax.experimental import pallas as pl
from jax.experimental.pallas import tpu as pltpu

D_MODEL = 1024
DEPTH = 4
F32 = jnp.float32
BF16 = jnp.bfloat16
EPS = 1e-6
N_EVEN = (DEPTH + 1) // 2
N_ODD = DEPTH // 2
D_FF = 4 * D_MODEL
PLE_DIM = 256

A_HEADS = 4
A_DK = D_MODEL // 8
A_DV = D_MODEL // 8
A_QK = A_HEADS * A_DK
A_V = A_HEADS * A_DV
A_COLS = 2 * A_QK + 2 * A_V + 2 * A_HEADS

B_HEADS = 8
B_DH = 64
B_W = B_HEADS * B_DH
B_W_RANK = 64
B_A_RANK = 64
B_G_RANK = 128
B_COLS = 3 * B_W + B_W_RANK + B_A_RANK + B_G_RANK
B_DECAY_OFFSET = 0.5
B_GN_EPS = 64e-5

EVEN_COLS = A_COLS + B_COLS
EVEN_OUT = A_V + B_W

C_INNER = 2 * D_MODEL
C_HEADDIM = 64
C_HEADS = C_INNER // C_HEADDIM
C_GROUPS = 4
C_HPG = C_HEADS // C_GROUPS
C_STATE = 128
C_CONV = 4
C_CHUNK = 128
C_CONV_DIM = C_INNER + 2 * C_GROUPS * C_STATE
ODD_COLS = C_INNER + C_CONV_DIM + C_HEADS

LANES = 128
SUBLANES = 8
VMEM_LIMIT = 56 * 1024 * 1024
SCAN_VMEM_LIMIT = 60 * 1024 * 1024
TOKEN_TILE = 512

EVEN_GATE_PAD = 256
EVEN_N = B_COLS + EVEN_GATE_PAD + 2 * A_QK + 2 * A_V
EVEN_G_OFF = B_COLS
EVEN_A_OFF = B_COLS + EVEN_GATE_PAD
ODD_DT_PAD = 256
ODD_N = C_INNER + C_CONV_DIM + ODD_DT_PAD
ODD_DT_OFF = C_INNER + C_CONV_DIM


def _cparams(sem, vmem_limit=VMEM_LIMIT):
    return pltpu.CompilerParams(dimension_semantics=sem, vmem_limit_bytes=vmem_limit)


def _rms(x, g):
    return x * lax.rsqrt(jnp.mean(x * x, axis=-1, keepdims=True) + EPS) * g


def _resident(shape):
    nd = len(shape)
    return pl.BlockSpec(shape, lambda *_: (0,) * nd, pipeline_mode=pl.Buffered(1))


def _rows(width):
    return pl.BlockSpec((TOKEN_TILE, width), lambda i: (i, 0))


def _fill_into(body, n_in, into, out_idx, into2=None, out_idx2=None):
    pairs = [(a, o) for a, o in ((into, out_idx), (into2, out_idx2)) if a is not None]
    if not pairs:
        return body, [], [], {}

    def skipping(*refs):
        return body(*refs[:n_in], *refs[n_in + len(pairs):])

    return (skipping, [pl.BlockSpec(memory_space=pl.ANY)] * len(pairs), [a for a, _ in pairs],
            {n_in + i: o for i, (_, o) in enumerate(pairs)})


def _norm_matmul_kernel(x_ref, g_ref, w_ref, o_ref, *t_refs, tn, t_col):
    xn = _rms(x_ref[...], g_ref[...]).astype(BF16)
    for n0 in range(0, w_ref.shape[1], tn):
        o_ref[:, n0:n0 + tn] = jnp.dot(xn, w_ref[:, n0:n0 + tn], preferred_element_type=F32)
    for t_ref in t_refs:
        t_ref[...] = o_ref[:, t_col:t_col + LANES].T[:t_ref.shape[0]]


def norm_matmul(x, g, w, tn, row0=0, total=None, into=None, t_cols=None, into_t=None):
    m, n = x.shape[0], w.shape[1]
    total = m if total is None else total
    t0 = row0 // TOKEN_TILE
    in_specs = [_rows(D_MODEL), _resident((1, D_MODEL)), _resident(w.shape)]
    out_specs = [pl.BlockSpec((TOKEN_TILE, n), lambda i: (t0 + i, 0))]
    out_shape = [jax.ShapeDtypeStruct((total, n), F32)]
    if t_cols is not None:
        out_specs.append(pl.BlockSpec((t_cols[1], TOKEN_TILE), lambda i: (0, t0 + i)))
        out_shape.append(jax.ShapeDtypeStruct((t_cols[1], total), F32))
    body, xspecs, xargs, aliases = _fill_into(
        functools.partial(_norm_matmul_kernel, tn=tn, t_col=t_cols[0] if t_cols else 0),
        len(in_specs), into, 0, into_t, 1)
    outs = pl.pallas_call(
        body,
        grid=(m // TOKEN_TILE,),
        in_specs=in_specs + xspecs,
        out_specs=out_specs,
        out_shape=out_shape,
        input_output_aliases=aliases,
        compiler_params=_cparams(("arbitrary",)),
        name="norm_matmul",
    )(x, g.reshape(1, D_MODEL), w, *xargs)
    return outs if t_cols is not None else outs[0]


FFN_CHUNK = 512


def _matmul_res_kernel(*refs, kinds):
    x_ref, o_ref = refs[-2], refs[-1]
    acc = x_ref[...]
    pos = 0
    for has_gate, channel_major in kinds:
        a = refs[pos][...]
        if channel_major:
            a = a.T
        if has_gate:
            a = a * refs[pos + 1][...]
        w_ref = refs[pos + 1 + has_gate]
        pos += 2 + has_gate
        acc = acc + jnp.dot(a.astype(BF16), w_ref[...], preferred_element_type=F32)
    o_ref[...] = acc


def matmul_res(terms, x, row0=0, nrows=None, into=None, total=None):
    m = x.shape[0] if total is None else total
    nrows = m if nrows is None else nrows
    t0 = row0 // TOKEN_TILE
    rows = lambda width: pl.BlockSpec((TOKEN_TILE, width), lambda i: (t0 + i, 0))
    xspec = rows(D_MODEL) if total is None else _rows(D_MODEL)
    specs, args, kinds = [], [], []
    for a, gate, w, seq_len in terms:
        if seq_len is None:
            specs.append(rows(a.shape[1]))
        elif seq_len == 0:
            specs.append(pl.BlockSpec((TOKEN_TILE, a.shape[1]), lambda i: (i, 0)))
        else:
            assert seq_len % TOKEN_TILE == 0
            per_seq = seq_len // TOKEN_TILE
            specs.append(pl.BlockSpec((None, a.shape[1], TOKEN_TILE), lambda i: (i // per_seq, 0, i % per_seq)))
        specs += ([rows(w.shape[0])] if gate is not None else []) + [_resident(w.shape)]
        args += [a] + ([gate] if gate is not None else []) + [w]
        kinds.append((int(gate is not None), bool(seq_len)))
    specs.append(xspec)
    body, xspecs, xargs, aliases = _fill_into(
        functools.partial(_matmul_res_kernel, kinds=tuple(kinds)), len(specs), into, 0)
    return pl.pallas_call(
        body,
        grid=(nrows // TOKEN_TILE,),
        in_specs=specs + xspecs,
        out_specs=rows(D_MODEL),
        out_shape=jax.ShapeDtypeStruct((m, D_MODEL), F32),
        input_output_aliases=aliases,
        compiler_params=_cparams(("arbitrary",)),
        name="matmul_res",
    )(*args, x, *xargs)


def _ffn_ple_kernel(x_ref, gf_ref, wu_ref, wd_ref, gp_ref, wg_ref, p_ref, wp_ref, *rest):
    gl_ref = rest[0] if len(rest) == 3 else None
    o_ref, y_s = rest[-2:]
    x = x_ref[...]
    xn = _rms(x, gf_ref[...]).astype(BF16)
    y_s[...] = x
    for c0 in range(0, D_FF, FFN_CHUNK):
        h = jnp.dot(xn, wu_ref[:, c0:c0 + FFN_CHUNK], preferred_element_type=F32)
        h = jnp.square(jnp.maximum(h, 0.0)).astype(BF16)
        y_s[...] += jnp.dot(h, wd_ref[c0:c0 + FFN_CHUNK, :], preferred_element_type=F32)
    y = y_s[...]
    yn = _rms(y, gp_ref[...]).astype(BF16)
    gate = jax.nn.sigmoid(jnp.dot(yn, wg_ref[...], preferred_element_type=F32))
    proj = jnp.dot(p_ref[...].astype(BF16), wp_ref[...], preferred_element_type=F32)
    out = y + proj * gate
    o_ref[...] = out if gl_ref is None else _rms(out, gl_ref[...])


def ffn_ple(x, g_ffn, wu, wd, g_ple, wg, p, wp, row0, into=None, g_last=None):
    t0 = row0 // TOKEN_TILE
    rows = pl.BlockSpec((TOKEN_TILE, D_MODEL), lambda i: (t0 + i, 0))
    row1 = lambda v: v.reshape(1, D_MODEL)
    in_specs = [rows, _resident((1, D_MODEL)), _resident(wu.shape), _resident(wd.shape),
                _resident((1, D_MODEL)), _resident(wg.shape), _rows(PLE_DIM), _resident(wp.shape)]
    args = [x, row1(g_ffn), wu, wd, row1(g_ple), wg, p, wp]
    if g_last is not None:
        assert into is None
        in_specs.append(_resident((1, D_MODEL)))
        args.append(row1(g_last))
    body, xspecs, xargs, aliases = _fill_into(_ffn_ple_kernel, len(in_specs), into, 0)
    return pl.pallas_call(
        body,
        grid=(p.shape[0] // TOKEN_TILE,),
        in_specs=in_specs + xspecs,
        out_specs=rows if g_last is None else _rows(D_MODEL),
        out_shape=jax.ShapeDtypeStruct(x.shape if g_last is None else (p.shape[0], D_MODEL), F32),
        input_output_aliases=aliases,
        scratch_shapes=[pltpu.VMEM((TOKEN_TILE, D_MODEL), F32)],
        compiler_params=_cparams(("arbitrary",)),
        name="ffn_ple",
    )(*args, *xargs)


def _rwkv_operands(k, a, r, kk_t, ka_t, rk_t):
    kk = k * kk_t
    kk = kk * lax.rsqrt(jnp.maximum(jnp.sum(kk * kk, axis=1, keepdims=True), 1e-24))
    k2 = k * (1.0 + (a - 1.0) * ka_t)
    return -kk, kk * a, k2, jnp.sum(r * k2 * rk_t, axis=1)


def _rwkv_step(s_ref, tile0, ntiles, r, w, k, a, b, vrows):
    ys = []
    for lt in range(ntiles):
        rows = slice((tile0 + lt) * B_DH, (tile0 + lt + 1) * B_DH)
        s = s_ref[rows, :]
        sa = jnp.sum(s * a, axis=0, keepdims=True)
        s = s * w + sa * b + vrows[lt:lt + 1] * k
        s_ref[rows, :] = s
        ys.append(jnp.sum(s * r, axis=0, keepdims=True))
    return jnp.concatenate(ys, axis=0)


LONG_T = 128
LONG_NLT = B_DH // 2
KEY_PITCH = B_DH + SUBLANES
VAL_PITCH = LONG_NLT + SUBLANES


def _rwkv_scan_long_kernel(r_ref, w_ref, k_ref, a_ref, v_ref, kkt_ref, kat_ref, rkt_ref, lnw_ref, lnb_ref, s0_ref,
                           yt_ref, s_ref, or_s, ow_s, ok_s, oa_s, av_s, v_s, y_s):
    nb = yt_ref.shape[0]

    @pl.when(pl.program_id(0) == 0)
    def _():
        s_ref[...] = s0_ref[...]
        for scr in (or_s, ow_s, ok_s, oa_s, av_s, v_s, y_s):
            scr[...] = jnp.zeros(scr.shape, F32)

    lo = lax.broadcasted_iota(jnp.int32, (LONG_T, LANES), 1) < LANES // 2

    def tile(ref, c):
        m = ref[c]
        return jnp.concatenate([m, m], axis=0).T

    def relayout(c, carry):
        for ref, dst in ((r_ref, or_s), (w_ref, ow_s), (k_ref, ok_s), (a_ref, oa_s)):
            dst[pl.ds(c, LONG_T, stride=KEY_PITCH), :] = tile(ref, c)
            dst[pl.ds(c + LONG_NLT, LONG_T, stride=KEY_PITCH), :] = tile(ref, c + LONG_NLT)
        v_s[pl.ds(c, LONG_T, stride=VAL_PITCH), :] = jnp.where(lo, tile(v_ref, c), tile(v_ref, c + LONG_NLT))
        return carry

    lax.fori_loop(0, LONG_NLT, relayout, 0)

    def unpad(scr, pitch, n):
        return scr[...].reshape(LONG_T, pitch, LANES)[:, :n]

    def pad(x, pitch):
        zeros = jnp.zeros((LONG_T, pitch - x.shape[1], LANES), F32)
        return jnp.concatenate([x, zeros], axis=1).reshape(LONG_T * pitch, LANES)

    av, bv, k2, bonus = _rwkv_operands(unpad(ok_s, KEY_PITCH, B_DH), unpad(oa_s, KEY_PITCH, B_DH),
                                       unpad(or_s, KEY_PITCH, B_DH), kkt_ref[...], kat_ref[...], rkt_ref[...])
    av_s[...] = pad(av, KEY_PITCH)
    oa_s[...] = pad(bv, KEY_PITCH)
    ok_s[...] = pad(k2, KEY_PITCH)

    def step(t, carry):
        kr = pl.ds(pl.multiple_of(t * KEY_PITCH, SUBLANES), B_DH)
        vr = pl.ds(pl.multiple_of(t * VAL_PITCH, SUBLANES), LONG_NLT)
        y_s[vr, :] = _rwkv_step(s_ref, 0, LONG_NLT, or_s[kr, :], ow_s[kr, :], ok_s[kr, :], av_s[kr, :],
                                oa_s[kr, :], v_s[vr, :])
        return carry

    lax.fori_loop(0, LONG_T, step, 0)

    def head_sum(x):
        tot = jnp.sum(x, axis=1)
        return tot + pltpu.roll(tot, LANES // 2, axis=1)

    y = unpad(y_s, VAL_PITCH, LONG_NLT)
    d = y - (head_sum(y) * (1.0 / B_DH))[:, None, :]
    var = head_sum(d * d) * (1.0 / B_DH)
    y = (d * lax.rsqrt(var + B_GN_EPS)[:, None, :] * lnw_ref[...] + lnb_ref[...]
         + bonus[:, None, :] * unpad(v_s, VAL_PITCH, LONG_NLT))
    y_s[...] = pad(y, VAL_PITCH)

    def relayout_out(lt, carry):
        yt = y_s[pl.ds(lt, LONG_T, stride=VAL_PITCH), :].T
        for i2 in range(2):
            for b in range(nb):
                row = i2 * (LANES // 2) + b * B_HEADS
                yt_ref[b, pl.ds(lt + LONG_NLT * i2, B_HEADS, stride=B_DH), :] = yt[row:row + B_HEADS, :]
        return carry

    lax.fori_loop(0, LONG_NLT, relayout_out, 0)


def rwkv_scan_long(rc, wc, kc, ac, vc, S0, k_k, k_a, r_k, ln_w, ln_b):
    _, bh, L = rc.shape
    b = bh // B_HEADS
    assert bh * 2 == LANES and L % LONG_T == 0
    per_key = lambda p: jnp.tile(p.reshape(B_HEADS, B_DH).T, (1, LANES // B_HEADS))
    per_val = lambda p: jnp.concatenate(
        [jnp.tile(p.reshape(B_HEADS, 2, LONG_NLT)[:, i2].T, (1, b)) for i2 in range(2)], axis=1)
    s0 = S0.reshape(b, B_HEADS, 2, LONG_NLT, B_DH).transpose(3, 4, 2, 0, 1).reshape(LONG_NLT * B_DH, LANES)
    blk = pl.BlockSpec((B_DH, bh, LONG_T), lambda c: (0, 0, c))
    oblk = pl.BlockSpec((b, B_W, LONG_T), lambda c: (0, 0, c))
    const = lambda shape: pl.BlockSpec(shape, lambda c: (0,) * len(shape))
    big = pltpu.VMEM((LONG_T * KEY_PITCH, LANES), F32)
    small = pltpu.VMEM((LONG_T * VAL_PITCH, LANES), F32)
    yt, s = pl.pallas_call(
        _rwkv_scan_long_kernel,
        grid=(L // LONG_T,),
        in_specs=[blk] * 5 + [const((B_DH, LANES))] * 3 + [const((LONG_NLT, LANES))] * 2
        + [_resident((LONG_NLT * B_DH, LANES))],
        out_specs=[oblk, const((LONG_NLT * B_DH, LANES))],
        out_shape=[jax.ShapeDtypeStruct((b, B_W, L), F32), jax.ShapeDtypeStruct(s0.shape, F32)],
        scratch_shapes=[big] * 5 + [small, small],
        compiler_params=_cparams(("arbitrary",), SCAN_VMEM_LIMIT),
        name="rwkv_scan_long",
    )(rc, wc, kc, ac, vc, per_key(k_k), per_key(k_a), per_key(r_k), per_val(ln_w), per_val(ln_b), s0)
    s = s.reshape(LONG_NLT, B_DH, 2, b, B_HEADS).transpose(3, 4, 2, 0, 1).reshape(b, B_HEADS, B_DH, B_DH)
    return yt, s


def _rwkv_scan_short_kernel(r_ref, w_ref, k_ref, a_ref, v_ref, kkt_ref, kat_ref, rkt_ref, lnw_ref, lnb_ref, s0_ref,
                            y_ref, sout_ref, s_s, or_s, ow_s, ok_s, oa_s, av_s, ov_s, y_s, *, L):
    nt = 2 * B_DH
    for q in range(nt * B_DH // LANES):
        s_s[q * LANES:(q + 1) * LANES, :] = s0_ref[:, q * LANES:(q + 1) * LANES].T
    for ref, dst in ((r_ref, or_s), (w_ref, ow_s), (k_ref, ok_s), (a_ref, oa_s), (v_ref, ov_s)):
        for t in range(L):
            dst[t] = ref[pl.ds(t, LANES, stride=L), :].T.reshape(2, B_DH, LANES)
    shape3 = (L * 2, B_DH, LANES)
    tiles = lambda ref: jnp.concatenate([ref[...]] * L, axis=0)
    av, bv, k2, bonus = _rwkv_operands(ok_s[...].reshape(shape3), oa_s[...].reshape(shape3),
                                       or_s[...].reshape(shape3), tiles(kkt_ref), tiles(kat_ref), tiles(rkt_ref))
    av_s[...] = av.reshape(L, 2, B_DH, LANES)
    oa_s[...] = bv.reshape(L, 2, B_DH, LANES)
    ok_s[...] = k2.reshape(L, 2, B_DH, LANES)

    def step(t, carry):
        for h in range(2):
            y_s[t, h] = _rwkv_step(s_s, h * B_DH, B_DH, or_s[t, h], ow_s[t, h], ok_s[t, h], av_s[t, h],
                                   oa_s[t, h], ov_s[t, h])
        return carry

    lax.fori_loop(0, L, step, 0)

    y = y_s[...]
    d = y - jnp.mean(y, axis=2, keepdims=True)
    var = jnp.mean(d * d, axis=2, keepdims=True)
    y = (d * lax.rsqrt(var + B_GN_EPS) * lnw_ref[...] + lnb_ref[...]
         + bonus.reshape(L, 2, 1, LANES) * ov_s[...])
    for t in range(L):
        y_ref[pl.ds(t, LANES, stride=L), :] = y[t].reshape(nt, LANES).T
    for q in range(nt * B_DH // LANES):
        sout_ref[:, q * LANES:(q + 1) * LANES] = s_s[q * LANES:(q + 1) * LANES, :].T


def rwkv_scan_short(r, w, k, a, v, S_all, layer, s_into, L, k_k, k_a, r_k, ln_w, ln_b):
    n = r.shape[0]
    b = n // L
    assert b == LANES
    npair = B_HEADS // 2
    wide = lambda p: jnp.broadcast_to(p.reshape(npair, 2, B_DH, 1), (npair, 2, B_DH, LANES))
    blk = pl.BlockSpec((n, LANES), lambda p: (0, p))
    cblk = pl.BlockSpec((None, 2, B_DH, LANES), lambda p: (p, 0, 0, 0))
    sblk = pl.BlockSpec((None, b, 2 * B_DH * B_DH), lambda p: (layer, 0, p))
    op = pltpu.VMEM((L, 2, B_DH, LANES), F32)
    in_specs = [blk] * 5 + [cblk] * 5 + [sblk]
    body, xspecs, xargs, aliases = _fill_into(
        functools.partial(_rwkv_scan_short_kernel, L=L), len(in_specs), s_into, 1)
    return pl.pallas_call(
        body,
        grid=(npair,),
        in_specs=in_specs + xspecs,
        out_specs=[blk, sblk],
        out_shape=[jax.ShapeDtypeStruct((n, B_W), F32), jax.ShapeDtypeStruct(S_all.shape, F32)],
        input_output_aliases=aliases,
        scratch_shapes=[pltpu.VMEM((2 * B_DH * B_DH, LANES), F32)] + [op] * 7,
        compiler_params=_cparams(("arbitrary",)),
        name="rwkv_scan_short",
    )(r, w, k, a, v, wide(k_k), wide(k_a), wide(r_k), wide(ln_w), wide(ln_b), S_all, *xargs)


RWKV_PREP_ROWS = 256
RWKV_SHORT_ROWS = 64
B_LORA_OFF = 3 * B_W


def _rwkv_prep_kernel(z_ref, sh0_ref, mu_ref, wwa_ref, g2_ref, w0_ref, a0_ref,
                      r_ref, w_ref, k_ref, a_ref, v_ref, g_ref, sh_ref, *scratch, T, nseq, channel_major):
    @pl.when(pl.program_id(1) == 0)
    def _():
        sh_ref[...] = sh0_ref[...]

    z = z_ref[...]
    rowid = lax.broadcasted_iota(jnp.int32, (z.shape[0], 1), 0)
    zprev = pltpu.roll(z, 1, axis=0)
    for u in range(nseq):
        zprev = jnp.where(rowid == u * T, sh_ref[u], zprev)
    for u in range(nseq):
        sh_ref[u] = z[(u + 1) * T - 1:(u + 1) * T, :]
    zs = z + (zprev - z) * mu_ref[...]
    r = zs[:, :B_W]
    k = zs[:, B_W:2 * B_W]
    lora = zs[:, B_LORA_OFF:B_LORA_OFF + LANES]
    lane = lax.broadcasted_iota(jnp.int32, lora.shape, 1)
    lora = jnp.where(lane < B_W_RANK, jnp.tanh(lora), lora).astype(BF16)
    wa = jnp.dot(lora, wwa_ref[...], preferred_element_type=F32)
    w_log = -jax.nn.softplus(-(w0_ref[...] + wa[:, :B_W])) - B_DECAY_OFFSET
    a = jax.nn.sigmoid(a0_ref[...] + wa[:, B_W:])
    zg = zs[:, B_LORA_OFF + LANES:B_LORA_OFF + LANES + B_G_RANK]
    def emit(ref, x):
        if not channel_major:
            ref[...] = x
            return
        xt_s = scratch[0]
        xt = x.T
        for j in range(xt_s.shape[0]):
            for h in range(B_HEADS):
                xt_s[j, h * KEY_PITCH:h * KEY_PITCH + B_DH, :] = xt[h * B_DH:(h + 1) * B_DH, j * LANES:(j + 1) * LANES]
        for c in range(B_DH):
            for j in range(xt_s.shape[0]):
                ref[c, :, j * LANES:(j + 1) * LANES] = xt_s[j, pl.ds(c, B_HEADS, stride=KEY_PITCH), :]

    emit(r_ref, r)
    emit(w_ref, jnp.exp(-jnp.exp(w_log)))
    emit(k_ref, k)
    emit(a_ref, a)
    emit(v_ref, zs[:, 2 * B_W:3 * B_W])
    g_ref[...] = jnp.dot(jax.nn.sigmoid(zg).astype(BF16), g2_ref[...], preferred_element_type=F32)


def rwkv_prep(zin, row0, b, L, gate_into, shift0, mu, w0, w2, a0, a2, g2):
    assert B_W_RANK + B_A_RANK == LANES
    T = math.gcd(L, RWKV_PREP_ROWS)
    channel_major = T == RWKV_PREP_ROWS
    nseq = 1 if channel_major else RWKV_SHORT_ROWS // T
    assert nseq == 1 or (L == T and b % nseq == 0)
    R = T * nseq
    nchunk = L // T
    blk0 = row0 // R
    wwa = jnp.zeros((LANES, 2 * B_W), F32).at[:B_W_RANK, :B_W].set(w2).at[B_W_RANK:, B_W:].set(a2).astype(BF16)
    row = lambda v: v.reshape(1, -1)
    if channel_major:
        blk = pl.BlockSpec((B_DH, B_HEADS, R), lambda i, c: (0, i, c))
        oshape = jax.ShapeDtypeStruct((B_DH, b * B_HEADS, L), F32)
    else:
        blk = pl.BlockSpec((R, B_W), lambda i, c: (i * nchunk + c, 0))
        oshape = jax.ShapeDtypeStruct((b * L, B_W), F32)
    gblk = pl.BlockSpec((R, B_W), lambda i, c: (blk0 + i * nchunk + c, 0))
    shspec = pl.BlockSpec((nseq, 1, B_COLS), lambda i, c: (i, 0, 0))
    const = lambda shape: pl.BlockSpec(shape, lambda i, c: (0,) * len(shape))
    in_specs = [pl.BlockSpec((R, B_COLS), lambda i, c: (blk0 + i * nchunk + c, 0)), shspec, const((1, B_COLS)),
                const((LANES, 2 * B_W)), const((B_G_RANK, B_W)), const((1, B_W)), const((1, B_W))]
    body, xspecs, xargs, aliases = _fill_into(
        functools.partial(_rwkv_prep_kernel, T=T, nseq=nseq, channel_major=channel_major),
        len(in_specs), gate_into, 5)
    outs = pl.pallas_call(
        body,
        grid=(b // nseq, nchunk),
        in_specs=in_specs + xspecs,
        out_specs=[blk] * 5 + [gblk, shspec],
        out_shape=[oshape] * 5
        + [jax.ShapeDtypeStruct((zin.shape[0], B_W), F32), jax.ShapeDtypeStruct((b, 1, B_COLS), F32)],
        input_output_aliases=aliases,
        scratch_shapes=[pltpu.VMEM((R // LANES, B_HEADS * KEY_PITCH, LANES), F32)] if channel_major else [],
        compiler_params=_cparams(("arbitrary", "arbitrary")),
        name="rwkv_prep",
    )(zin, shift0.reshape(b, 1, B_COLS), row(mu), wwa, g2.astype(BF16), row(w0), row(a0), *xargs)
    return outs[:5], outs[5], outs[6].reshape(b, B_COLS)


MLSTM_ROWS = 128
HIGHEST = lax.Precision.HIGHEST


def _mlstm_kernel(q_ref, k_ref, v_ref, o_ref, g_ref, gt_ref, brow_ref, bcol_ref, nw_ref,
                  c0_ref, n0_ref, m0_ref, h_ref, c_ref, n_ref, m_ref, *, T, nseq):
    R = MLSTM_ROWS

    @pl.when(pl.program_id(1) == 0)
    def _():
        c_ref[...] = c0_ref[...]
        n_ref[...] = n0_ref[...]
        m_ref[...] = m0_ref[...]

    shift = T.bit_length() - 1
    ri = lax.broadcasted_iota(jnp.int32, (R, R), 0)
    ci = lax.broadcasted_iota(jnp.int32, (R, R), 1)
    mask = (ci <= ri) & (jnp.right_shift(ri, shift) == jnp.right_shift(ci, shift))
    lmat = mask.astype(F32)
    rowid = lax.broadcasted_iota(jnp.int32, (R, 1), 0)
    rsel = [(rowid >= u * T) & (rowid < (u + 1) * T) for u in range(nseq)]

    g = g_ref[...] + brow_ref[...]
    lane = lax.broadcasted_iota(jnp.int32, g.shape, 1)
    glog = jnp.where((lane >= A_HEADS) & (lane < 2 * A_HEADS), jax.nn.log_sigmoid(g), g)
    gt = gt_ref[...] + bcol_ref[...]
    sub = lax.broadcasted_iota(jnp.int32, gt.shape, 0)
    gtlog = jnp.where(sub >= A_HEADS, jax.nn.log_sigmoid(gt), gt)
    bc_col = jnp.dot(lmat, glog, precision=HIGHEST, preferred_element_type=F32)
    bc_row = lax.dot_general(gtlog, lmat, (((1,), (1,)), ((), ())), precision=HIGHEST,
                             preferred_element_type=F32)
    lane_m = lax.broadcasted_iota(jnp.int32, (1, LANES), 1)
    m_old = [m_ref[u] for u in range(nseq)]
    m_out = [jnp.zeros((1, LANES), F32) for _ in range(nseq)]

    for h in range(A_HEADS):
        hs = slice(h * A_DK, (h + 1) * A_DK)
        bcc = bc_col[:, A_HEADS + h:A_HEADS + h + 1]
        bcr = bc_row[A_HEADS + h:A_HEADS + h + 1, :]
        lir = gtlog[h:h + 1, :]
        lic = glog[:, h:h + 1]
        m_u = [m_old[u][:, h:h + 1] for u in range(nseq)]
        m_col = m_u[0]
        for u in range(1, nseq):
            m_col = jnp.where(rsel[u], m_u[u], m_col)
        dmat = jnp.where(mask, bcc - bcr + lir, -jnp.inf)
        inter = bcc + m_col
        mt = jnp.maximum(inter, jnp.max(dmat, axis=1, keepdims=True))
        p = jnp.exp(dmat - mt)
        qh = q_ref[:, hs] * (A_DK ** -0.5)
        kh = k_ref[:, hs]
        qb, kb, vb = qh.astype(BF16), kh.astype(BF16), v_ref[:, hs].astype(BF16)
        wq = lax.dot_general(qb, kb, (((1,), (1,)), ((), ())), preferred_element_type=F32) * p
        wi = jnp.exp(inter - mt)
        c_old = [c_ref[u, h] for u in range(nseq)]
        n_old = [n_ref[u, h:h + 1, :] for u in range(nseq)]
        qc = jnp.dot(qb, c_old[0].astype(BF16), preferred_element_type=F32)
        qn = jnp.sum(qh * n_old[0], axis=1, keepdims=True)
        for u in range(1, nseq):
            qc = jnp.where(rsel[u], jnp.dot(qb, c_old[u].astype(BF16), preferred_element_type=F32), qc)
            qn = jnp.where(rsel[u], jnp.sum(qh * n_old[u], axis=1, keepdims=True), qn)
        num = jnp.dot(wq.astype(BF16), vb, preferred_element_type=F32) + wi * qc
        den = jnp.sum(wq, axis=1, keepdims=True) + wi * qn
        hh = num / jnp.maximum(jnp.abs(den), jnp.exp(-mt))
        hh = hh * lax.rsqrt(jnp.mean(hh * hh, axis=-1, keepdims=True) + EPS)
        h_ref[:, hs] = (hh * nw_ref[:, hs] * jax.nn.sigmoid(o_ref[:, hs])).astype(BF16)
        for u in range(nseq):
            b_last = bcc[(u + 1) * T - 1:(u + 1) * T, :]
            gs = b_last - bcc + lic
            gmax = jnp.max(gs if nseq == 1 else jnp.where(rsel[u], gs, -jnp.inf), axis=0, keepdims=True)
            m_new = jnp.maximum(b_last + m_u[u], gmax)
            decay = jnp.exp(b_last + m_u[u] - m_new)
            ws = jnp.exp(gs - m_new)
            if nseq > 1:
                ws = jnp.where(rsel[u], ws, 0.0)
            kw = kh * ws
            c_ref[u, h] = decay * c_old[u] + lax.dot_general(
                kw.astype(BF16), vb, (((0,), (0,)), ((), ())), preferred_element_type=F32)
            n_ref[u, h:h + 1, :] = decay * n_old[u] + jnp.sum(kw, axis=0, keepdims=True)
            m_out[u] = jnp.where(lane_m == h, m_new, m_out[u])
    for u in range(nseq):
        m_ref[u] = m_out[u]


def mlstm(zin, gates_t, row0, b, L, h_into, layer, c_into, b_i, b_f, m_norm, C0, n0, m0):
    R = MLSTM_ROWS
    T = math.gcd(L, R)
    nseq = R // T
    assert nseq == 1 or (L == T and b % nseq == 0)
    nchunk = L // T
    blk0 = row0 // R
    bias = jnp.concatenate([b_i, b_f])
    bias_row = jnp.zeros((1, LANES), F32).at[0, :2 * A_HEADS].set(bias)
    bias_col = bias.reshape(2 * A_HEADS, 1)
    m0p = jnp.zeros((b, 1, LANES), F32).at[:, 0, :A_HEADS].set(m0)
    blk = lambda i, c: blk0 + i * nchunk + c
    rowblk = lambda col: pl.BlockSpec((R, A_QK), lambda i, c: (blk(i, c), EVEN_A_OFF // A_QK + col))
    cspec = pl.BlockSpec((nseq, A_HEADS, A_DK, A_DV), lambda i, c: (i, 0, 0, 0))
    cout = pl.BlockSpec((None, nseq, A_HEADS, A_DK, A_DV), lambda i, c: (layer, i, 0, 0, 0))
    nspec = pl.BlockSpec((nseq, A_HEADS, A_DK), lambda i, c: (i, 0, 0))
    mspec = pl.BlockSpec((nseq, 1, LANES), lambda i, c: (i, 0, 0))
    const = lambda shape: pl.BlockSpec(shape, lambda i, c: (0,) * len(shape))
    in_specs = [rowblk(0), rowblk(1), rowblk(2), rowblk(3),
                pl.BlockSpec((R, LANES), lambda i, c: (blk(i, c), EVEN_G_OFF // LANES)),
                pl.BlockSpec((2 * A_HEADS, R), lambda i, c: (0, blk(i, c))),
                const((1, LANES)), const((2 * A_HEADS, 1)), const((1, A_V)),
                cspec if C0.ndim == 4 else cout, nspec, mspec]
    body, xspecs, xargs, aliases = _fill_into(
        functools.partial(_mlstm_kernel, T=T, nseq=nseq), len(in_specs), h_into, 0, c_into, 1)
    h, C, n, m = pl.pallas_call(
        body,
        grid=(b // nseq, nchunk),
        in_specs=in_specs + xspecs,
        out_specs=[pl.BlockSpec((R, A_V), lambda i, c: (blk(i, c), 0)), cout, nspec, mspec],
        out_shape=[jax.ShapeDtypeStruct((zin.shape[0], A_V), BF16),
                   jax.ShapeDtypeStruct((N_EVEN,) + C0.shape[-4:], F32), jax.ShapeDtypeStruct(n0.shape, F32),
                   jax.ShapeDtypeStruct(m0p.shape, F32)],
        input_output_aliases=aliases,
        compiler_params=_cparams(("arbitrary", "arbitrary")),
        name="mlstm",
    )(zin, zin, zin, zin, zin, gates_t, bias_row, bias_col, m_norm.reshape(1, A_V), C0, n0, m0p, *xargs)
    return h, C, n, m[:, 0, :A_HEADS]


def even_mixer(zin, gates_t, row0, b, L, into, layer, c_into, C0, n0, m0, S0, shift0, b_i, b_f, m_norm,
               mu, w0, w2, a0, a2, g2, k_k, k_a, r_k, ln_w, ln_b):
    hA, C, n, m = mlstm(zin, gates_t, row0, b, L, into[0], layer, c_into[0], b_i, b_f, m_norm, C0, n0, m0)
    ops, g, shift = rwkv_prep(zin, row0, b, L, into[1], shift0, mu, w0, w2, a0, a2, g2)
    if ops[0].ndim == 3:
        hB, S = rwkv_scan_long(*ops, S0, k_k, k_a, r_k, ln_w, ln_b)
    else:
        hB, S = rwkv_scan_short(*ops, S0, layer, c_into[1], L, k_k, k_a, r_k, ln_w, ln_b)
    return hA, hB, g, (C, n, m, S, shift)


C_PAIRS = C_HEADS // 2
C_GROUP_W = C_INNER // C_GROUPS
C_BC_W = 2 * C_GROUPS * C_STATE
SSD_LONG_ROWS = C_CHUNK
SSD_SHORT_ROWS = 32


def _ssd_kernel(z_ref, x_ref, bc_ref, dt_ref, dtt_ref, cw_ref, cb_ref, dtb_ref, dtbt_ref, al_ref, alt_ref,
                dsk_ref, nw_ref, s0_ref, cv0_ref, y_ref, s_ref, cvo_ref, stg_ref, yf_s, cv_ref, *, T, nseq, nchunk):
    R = T * nseq
    tstate = nchunk > 1
    assert not tstate or nseq == 1

    @pl.when(pl.program_id(1) == 0)
    def _():
        if tstate:
            for pr in range(C_PAIRS):
                s_ref[0, pr] = s0_ref[0, pr].T
        else:
            s_ref[...] = s0_ref[...]
        cv_ref[...] = jnp.zeros(cv_ref.shape, F32)
        for u in range(nseq):
            cv_ref[u, SUBLANES - (C_CONV - 1):SUBLANES, :] = cv0_ref[u]

    shift = T.bit_length() - 1
    ri = lax.broadcasted_iota(jnp.int32, (R, R), 0)
    ci = lax.broadcasted_iota(jnp.int32, (R, R), 1)
    mask = (ci <= ri) & (jnp.right_shift(ri, shift) == jnp.right_shift(ci, shift))
    lmat = mask.astype(F32)
    rowid = lax.broadcasted_iota(jnp.int32, (R, 1), 0)
    rsel = [(rowid >= u * T) & (rowid < (u + 1) * T) for u in range(nseq)]

    def conv_silu(src_ref, src_col, col):
        cols = slice(col, col + LANES)
        accs = []
        for u in range(nseq):
            stg_ref[u, 0:SUBLANES, cols] = cv_ref[u, :, cols]
            stg_ref[u, SUBLANES:SUBLANES + T, cols] = src_ref[u * T:(u + 1) * T, src_col:src_col + LANES]
            acc = cb_ref[:, cols]
            for d in range(C_CONV):
                acc = acc + stg_ref[u, SUBLANES - d:SUBLANES - d + T, cols] * cw_ref[C_CONV - 1 - d:C_CONV - d, cols]
            accs.append(acc)
            cv_ref[u, :, cols] = stg_ref[u, T:T + SUBLANES, cols]
        acc = accs[0] if nseq == 1 else jnp.concatenate(accs, axis=0)
        return acc * jax.nn.sigmoid(acc)

    dtv = jax.nn.softplus(dt_ref[...] + dtb_ref[...])
    dtt = jax.nn.softplus(dtt_ref[...] + dtbt_ref[...])
    cum_col = jnp.dot(lmat, dtv * (-jnp.exp(al_ref[...])), precision=HIGHEST, preferred_element_type=F32)
    cum_row = lax.dot_general(dtt * (-jnp.exp(alt_ref[...])), lmat, (((1,), (1,)), ((), ())),
                              precision=HIGHEST, preferred_element_type=F32)

    lo = lax.broadcasted_iota(jnp.int32, (R, LANES), 1) < C_HEADDIM
    rlo = lax.broadcasted_iota(jnp.int32, (LANES, 1), 0) < C_HEADDIM
    nt = (((1,), (1,)), ((), ()))
    tn = (((0,), (0,)), ((), ()))
    pairs_per_group = C_PAIRS // C_GROUPS
    for g in range(C_GROUPS):
        bgf = conv_silu(bc_ref, g * C_STATE, C_INNER + g * C_STATE)
        bg = bgf.astype(BF16)
        cg = conv_silu(bc_ref, (C_GROUPS + g) * C_STATE, C_INNER + (C_GROUPS + g) * C_STATE).astype(BF16)
        cbm = lax.dot_general(cg, bg, nt, preferred_element_type=F32)
        ys = None
        if tstate:
            bgt = bgf.T.astype(BF16)
            sgt = jnp.concatenate([s_ref[0, g * pairs_per_group + q] for q in range(pairs_per_group)], axis=1)
            ys = jnp.dot(cg, sgt.astype(BF16), preferred_element_type=F32)
        for u in range(0 if tstate else nseq):
            sg = s_ref[u, g * pairs_per_group:(g + 1) * pairs_per_group].reshape(C_GROUP_W, C_STATE)
            t_u = lax.dot_general(cg, sg.astype(BF16), nt, preferred_element_type=F32)
            ys = t_u if u == 0 else jnp.where(rsel[u], t_u, ys)
        for q in range(pairs_per_group):
            pr = g * pairs_per_group + q
            ps = slice(pr * LANES, (pr + 1) * LANES)
            xp = conv_silu(x_ref, pr * LANES, pr * LANES)
            cc = [cum_col[:, 2 * pr + e:2 * pr + e + 1] for e in range(2)]
            intra = None
            for e, keep in ((0, lo), (1, jnp.logical_not(lo))):
                hh = 2 * pr + e
                seg = jnp.exp(jnp.where(mask, cc[e] - cum_row[hh:hh + 1, :], -jnp.inf))
                mix = cbm * seg * dtt[hh:hh + 1, :]
                part = jnp.dot(mix.astype(BF16), jnp.where(keep, xp, 0.0).astype(BF16),
                               preferred_element_type=F32)
                intra = part if intra is None else intra + part
            scale = jnp.where(lo, jnp.exp(cc[0]), jnp.exp(cc[1]))
            yp = intra + scale * ys[:, q * LANES:(q + 1) * LANES] + dsk_ref[:, ps] * xp
            zp = z_ref[:, ps]
            yf_s[:, ps] = yp * (zp * jax.nn.sigmoid(zp))
            for u in range(nseq):
                last = (u + 1) * T - 1
                ct = [cc[e][last:last + 1, :] for e in range(2)]
                tail = jnp.where(lo, jnp.exp(ct[0] - cc[0]) * dtv[:, 2 * pr:2 * pr + 1],
                                 jnp.exp(ct[1] - cc[1]) * dtv[:, 2 * pr + 1:2 * pr + 2])
                xw = xp * tail
                if nseq > 1:
                    xw = jnp.where(rsel[u], xw, 0.0)
                if tstate:
                    upd = jnp.dot(bgt, xw.astype(BF16), preferred_element_type=F32)
                    dec = jnp.where(lo[:1], jnp.exp(ct[0]), jnp.exp(ct[1]))
                else:
                    upd = lax.dot_general(xw.astype(BF16), bg, tn, preferred_element_type=F32)
                    dec = jnp.where(rlo, jnp.exp(ct[0]), jnp.exp(ct[1]))
                s_ref[u, pr] = dec * s_ref[u, pr] + upd

    if tstate:
        @pl.when(pl.program_id(1) == nchunk - 1)
        def _():
            for pr in range(C_PAIRS):
                s_ref[0, pr] = s_ref[0, pr].T

    for u in range(nseq):
        cvo_ref[u] = cv_ref[u, SUBLANES - (C_CONV - 1):SUBLANES, :]

    for g in range(C_GROUPS):
        gs_ = slice(g * C_GROUP_W, (g + 1) * C_GROUP_W)
        yg = yf_s[:, gs_]
        yg = yg * lax.rsqrt(jnp.mean(yg * yg, axis=-1, keepdims=True) + EPS) * nw_ref[:, gs_]
        y_ref[:, gs_] = yg.astype(BF16)


def ssd_mixer(zin, row0, b, L, y_into, layer, s_into, ssm0, conv0, conv_w, conv_b, dt_bias, a_log, d_skip, norm_w):
    T = math.gcd(L, SSD_LONG_ROWS)
    nseq = 1 if T == SSD_LONG_ROWS else SSD_SHORT_ROWS // T
    assert nseq == 1 or (L == T and b % nseq == 0)
    R = T * nseq
    nchunk = L // T
    nblk = b * L // R
    blk0 = row0 // R
    dt_t = zin[row0:row0 + b * L, ODD_DT_OFF:ODD_DT_OFF + C_HEADS].reshape(nblk, R, C_HEADS).transpose(0, 2, 1)
    pad_row = lambda v: jnp.zeros((1, LANES), F32).at[0, :C_HEADS].set(v)
    pair_shape = (b, C_PAIRS, 2 * C_HEADDIM, C_STATE)
    per_layer = ssm0.ndim == 4
    s0 = ssm0.reshape(pair_shape if per_layer else (ssm0.shape[0],) + pair_shape)
    blk = lambda i, c: blk0 + i * nchunk + c
    sspec = pl.BlockSpec((nseq, C_PAIRS, 2 * C_HEADDIM, C_STATE), lambda i, c: (i, 0, 0, 0))
    sout = pl.BlockSpec((None, nseq, C_PAIRS, 2 * C_HEADDIM, C_STATE), lambda i, c: (layer, i, 0, 0, 0))
    cvspec = pl.BlockSpec((nseq, C_CONV - 1, C_CONV_DIM), lambda i, c: (i, 0, 0))
    const = lambda shape: pl.BlockSpec(shape, lambda i, c: (0,) * len(shape))
    in_specs = [pl.BlockSpec((R, C_INNER), lambda i, c: (blk(i, c), 0)),
                pl.BlockSpec((R, C_INNER), lambda i, c: (blk(i, c), 1)),
                pl.BlockSpec((R, C_BC_W), lambda i, c: (blk(i, c), 2 * C_INNER // C_BC_W)),
                pl.BlockSpec((R, LANES), lambda i, c: (blk(i, c), ODD_DT_OFF // LANES)),
                pl.BlockSpec((None, C_HEADS, R), lambda i, c: (i * nchunk + c, 0, 0)),
                const((C_CONV, C_CONV_DIM)), const((1, C_CONV_DIM)),
                const((1, LANES)), const((C_HEADS, 1)), const((1, LANES)), const((C_HEADS, 1)),
                const((1, C_INNER)), const((1, C_INNER)), sspec if per_layer else sout, cvspec]
    body, xspecs, xargs, aliases = _fill_into(
        functools.partial(_ssd_kernel, T=T, nseq=nseq, nchunk=nchunk), len(in_specs), y_into, 0, s_into, 1)
    y, s, cv = pl.pallas_call(
        body,
        grid=(b // nseq, nchunk),
        in_specs=in_specs + xspecs,
        out_specs=[pl.BlockSpec((R, C_INNER), lambda i, c: (blk(i, c), 0)), sout, cvspec],
        out_shape=[jax.ShapeDtypeStruct((zin.shape[0], C_INNER), BF16),
                   jax.ShapeDtypeStruct((N_ODD,) + pair_shape, F32), jax.ShapeDtypeStruct(conv0.shape, F32)],
        input_output_aliases=aliases,
        scratch_shapes=[pltpu.VMEM((nseq, SUBLANES + T, C_CONV_DIM), F32), pltpu.VMEM((R, C_INNER), F32),
                        pltpu.VMEM((nseq, SUBLANES, C_CONV_DIM), F32)],
        compiler_params=_cparams(("arbitrary", "arbitrary")),
        name="ssd",
    )(zin, zin, zin, zin, dt_t, conv_w, conv_b.reshape(1, C_CONV_DIM),
      pad_row(dt_bias), dt_bias.reshape(C_HEADS, 1), pad_row(a_log), a_log.reshape(C_HEADS, 1),
      jnp.repeat(d_skip, C_HEADDIM).reshape(1, C_INNER), norm_w.reshape(1, C_INNER), s0, conv0, *xargs)
    return y, s, cv


def _even_w_in(w):
    qkvo = w[:, :2 * A_QK + 2 * A_V]
    gates = w[:, 2 * A_QK + 2 * A_V:A_COLS]
    rwkv = w[:, A_COLS:]
    pad = jnp.zeros((D_MODEL, EVEN_GATE_PAD - 2 * A_HEADS), w.dtype)
    return jnp.concatenate([rwkv, gates, pad, qkvo], axis=1).astype(BF16)


def _odd_w_in(w):
    pad = jnp.zeros((D_MODEL, ODD_DT_PAD - C_HEADS), w.dtype)
    return jnp.concatenate([w, pad], axis=1).astype(BF16)


def kernel(x_prompt, x_sample, state_mlstm_C, state_mlstm_n, state_mlstm_m, state_rwkv_S,
           state_rwkv_shift, state_ssm, state_conv, p_prompt, p_sample,
           norm_mix, norm_ffn, w_ffn_up, w_ffn_down, w_ple_proj, norm_ple, w_ple_gate, norm_final,
           w_in_even, mlstm_b_i, mlstm_b_f, mlstm_norm, rwkv_mu, rwkv_w0, rwkv_w2, rwkv_a0, rwkv_a2,
           rwkv_g2, rwkv_k_k, rwkv_k_a, rwkv_r_k, rwkv_ln_w, rwkv_ln_b, w_out_even,
           w_in_odd, conv_w, conv_b, dt_bias, a_log, d_skip, ssm_norm, w_out_odd):
    bp, Lp, _ = x_prompt.shape
    bs, Ls, _ = x_sample.shape
    n_p, n_s = bp * Lp, bs * Ls
    n_tot = n_p + n_s
    xp = x_prompt.reshape(n_p, D_MODEL)
    xs = x_sample.reshape(n_s, D_MODEL)
    pp = p_prompt.reshape(DEPTH, n_p, PLE_DIM)
    ps = p_sample.reshape(DEPTH, n_s, PLE_DIM)
    x = None

    even_small = (mlstm_b_i, mlstm_b_f, mlstm_norm, rwkv_mu, rwkv_w0, rwkv_w2, rwkv_a0, rwkv_a2,
                  rwkv_g2, rwkv_k_k, rwkv_k_a, rwkv_r_k, rwkv_ln_w, rwkv_ln_b)
    odd_small = (conv_w, conv_b, dt_bias, a_log, d_skip, ssm_norm)

    zeros_even = (jnp.zeros((bp, A_HEADS, A_DK, A_DV), F32), jnp.zeros((bp, A_HEADS, A_DK), F32),
                  jnp.zeros((bp, A_HEADS), F32), jnp.zeros((bp, B_HEADS, B_DH, B_DH), F32),
                  jnp.zeros((bp, B_COLS), F32))
    zeros_odd = (jnp.zeros((bp, C_HEADS, C_HEADDIM, C_STATE), F32),
                 jnp.zeros((bp, C_CONV - 1, C_CONV_DIM), F32))

    st_p_even, st_s_even, st_p_odd, st_s_odd = [], [], [], []
    mC_p = mC_s = rS_s = ssm_p = ssm_s = None
    rs_flat = state_rwkv_S.reshape(N_EVEN, bs, B_HEADS * B_DH * B_DH)
    for i in range(DEPTH):
        j = i // 2
        if i % 2 == 0:
            w_in = _even_w_in(w_in_even[j])
            gate_cols = (EVEN_G_OFF, 2 * A_HEADS)
            if x is None:
                zin, gates_t = norm_matmul(xp, norm_mix[i], w_in, 512, 0, n_tot, None, gate_cols)
                zin, gates_t = norm_matmul(xs, norm_mix[i], w_in, 512, n_p, n_tot, zin, gate_cols, gates_t)
            else:
                zin, gates_t = norm_matmul(x, norm_mix[i], w_in, 512, t_cols=gate_cols)
            small = [t[j] for t in even_small]
            ha, hb_p, g, sp = even_mixer(zin, gates_t, 0, bp, Lp, (None, None), j, (mC_p, None), *zeros_even, *small)
            ha, hb_s, g, ss = even_mixer(zin, gates_t, n_p, bs, Ls, (ha, g), j, (mC_s, rS_s), state_mlstm_C,
                                         state_mlstm_n[j], state_mlstm_m[j], rs_flat, state_rwkv_shift[j],
                                         *small)
            mC_p, mC_s, rS_s = sp[0], ss[0], ss[3]
            st_p_even.append(sp)
            st_s_even.append(ss)
            wo = w_out_even[j].astype(BF16)
            terms_p = [(ha, None, wo[:A_V], None), (hb_p, g, wo[A_V:], Lp)]
            terms_s = [(ha, None, wo[:A_V], None), (hb_s, g, wo[A_V:], 0)]
            if x is None:
                xo = matmul_res(terms_p, xp, 0, n_p, None, n_tot)
                x = matmul_res(terms_s, xs, n_p, n_s, xo, n_tot)
            else:
                xo = matmul_res(terms_p, x, 0, n_p)
                x = matmul_res(terms_s, x, n_p, n_s, xo)
        else:
            zin = norm_matmul(x, norm_mix[i], _odd_w_in(w_in_odd[j]), 768)
            small = [t[j] for t in odd_small]
            mix, ssm_p, cv_p = ssd_mixer(zin, 0, bp, Lp, None, j, ssm_p, *zeros_odd, *small)
            mix, ssm_s, cv_s = ssd_mixer(zin, n_p, bs, Ls, mix, j, ssm_s, state_ssm, state_conv[j], *small)
            st_p_odd.append(cv_p)
            st_s_odd.append(cv_s)
            x = matmul_res([(mix, None, w_out_odd[j].astype(BF16), None)], x)
        tail = (norm_ffn[i], w_ffn_up[i].astype(BF16), w_ffn_down[i].astype(BF16),
                norm_ple[i], w_ple_gate[i].astype(BF16))
        wp = w_ple_proj[i].astype(BF16)
        if i < DEPTH - 1:
            xo = ffn_ple(x, *tail, pp[i], wp, 0)
            x = ffn_ple(x, *tail, ps[i], wp, n_p, xo)
        else:
            y_prompt = ffn_ple(x, *tail, pp[i], wp, 0, None, norm_final).reshape(bp, Lp, D_MODEL)
            y_sample = ffn_ple(x, *tail, ps[i], wp, n_p, None, norm_final).reshape(bs, Ls, D_MODEL)
    stack = lambda sts, idx: jnp.stack([s[idx] for s in sts])
    ssm_shape = lambda b: (N_ODD, b, C_HEADS, C_HEADDIM, C_STATE)
    return (y_prompt, y_sample,
            mC_p, stack(st_p_even, 1), stack(st_p_even, 2), stack(st_p_even, 3),
            stack(st_p_even, 4), ssm_p.reshape(ssm_shape(bp)), jnp.stack(st_p_odd),
            mC_s, stack(st_s_even, 1), stack(st_s_even, 2), rS_s.reshape(state_rwkv_S.shape),
            stack(st_s_even, 4), ssm_s.reshape(ssm_shape(bs)), jnp.stack(st_s_odd))
```

```python
import math
import functools
import jax
import jax.numpy as jnp
from jax import lax
from jax.experimental import pallas as pl
from jax.experimental.pallas import tpu as pltpu

D_MODEL = 1024
DEPTH = 4
F32 = jnp.float32
BF16 = jnp.bfloat16
EPS = 1e-6
N_EVEN = (DEPTH + 1) // 2
N_ODD = DEPTH // 2
D_FF = 4 * D_MODEL
PLE_DIM = 256

A_HEADS = 4
A_DK = D_MODEL // 8
A_DV = D_MODEL // 8
A_QK = A_HEADS * A_DK
A_V = A_HEADS * A_DV
A_COLS = 2 * A_QK + 2 * A_V + 2 * A_HEADS

B_HEADS = 8
B_DH = 64
B_W = B_HEADS * B_DH
B_W_RANK = 64
B_A_RANK = 64
B_G_RANK = 128
B_COLS = 3 * B_W + B_W_RANK + B_A_RANK + B_G_RANK
B_DECAY_OFFSET = 0.5
B_GN_EPS = 64e-5

EVEN_COLS = A_COLS + B_COLS
EVEN_OUT = A_V + B_W

C_INNER = 2 * D_MODEL
C_HEADDIM = 64
C_HEADS = C_INNER // C_HEADDIM
C_GROUPS = 4
C_HPG = C_HEADS // C_GROUPS
C_STATE = 128
C_CONV = 4
C_CHUNK = 128
C_CONV_DIM = C_INNER + 2 * C_GROUPS * C_STATE
ODD_COLS = C_INNER + C_CONV_DIM + C_HEADS

LANES = 128
SUBLANES = 8
VMEM_LIMIT = 56 * 1024 * 1024
SCAN_VMEM_LIMIT = 60 * 1024 * 1024
TOKEN_TILE = 512

EVEN_GATE_PAD = 256
EVEN_N = B_COLS + EVEN_GATE_PAD + 2 * A_QK + 2 * A_V
EVEN_G_OFF = B_COLS
EVEN_A_OFF = B_COLS + EVEN_GATE_PAD
ODD_DT_PAD = 256
ODD_N = C_INNER + C_CONV_DIM + ODD_DT_PAD
ODD_DT_OFF = C_INNER + C_CONV_DIM


def _cparams(sem, vmem_limit=VMEM_LIMIT):
    return pltpu.CompilerParams(dimension_semantics=sem, vmem_limit_bytes=vmem_limit)


def _rms(x, g):
    return x * lax.rsqrt(jnp.mean(x * x, axis=-1, keepdims=True) + EPS) * g


def _resident(shape):
    nd = len(shape)
    return pl.BlockSpec(shape, lambda *_: (0,) * nd, pipeline_mode=pl.Buffered(1))


def _rows(width):
    return pl.BlockSpec((TOKEN_TILE, width), lambda i: (i, 0))


def _fill_into(body, n_in, into, out_idx, into2=None, out_idx2=None):
    pairs = [(a, o) for a, o in ((into, out_idx), (into2, out_idx2)) if a is not None]
    if not pairs:
        return body, [], [], {}

    def skipping(*refs):
        return body(*refs[:n_in], *refs[n_in + len(pairs):])

    return (skipping, [pl.BlockSpec(memory_space=pl.ANY)] * len(pairs), [a for a, _ in pairs],
            {n_in + i: o for i, (_, o) in enumerate(pairs)})


def _norm_matmul_kernel(x_ref, g_ref, w_ref, o_ref, *t_refs, tn, t_col):
    xn = _rms(x_ref[...], g_ref[...]).astype(BF16)
    for n0 in range(0, w_ref.shape[1], tn):
        o_ref[:, n0:n0 + tn] = jnp.dot(xn, w_ref[:, n0:n0 + tn], preferred_element_type=F32)
    for t_ref in t_refs:
        t_ref[...] = o_ref[:, t_col:t_col + LANES].T[:t_ref.shape[0]]


def norm_matmul(x, g, w, tn, row0=0, total=None, into=None, t_cols=None, into_t=None):
    m, n = x.shape[0], w.shape[1]
    total = m if total is None else total
    t0 = row0 // TOKEN_TILE
    in_specs = [_rows(D_MODEL), _resident((1, D_MODEL)), _resident(w.shape)]
    out_specs = [pl.BlockSpec((TOKEN_TILE, n), lambda i: (t0 + i, 0))]
    out_shape = [jax.ShapeDtypeStruct((total, n), F32)]
    if t_cols is not None:
        out_specs.append(pl.BlockSpec((t_cols[1], TOKEN_TILE), lambda i: (0, t0 + i)))
        out_shape.append(jax.ShapeDtypeStruct((t_cols[1], total), F32))
    body, xspecs, xargs, aliases = _fill_into(
        functools.partial(_norm_matmul_kernel, tn=tn, t_col=t_cols[0] if t_cols else 0),
        len(in_specs), into, 0, into_t, 1)
    outs = pl.pallas_call(
        body,
        grid=(m // TOKEN_TILE,),
        in_specs=in_specs + xspecs,
        out_specs=out_specs,
        out_shape=out_shape,
        input_output_aliases=aliases,
        compiler_params=_cparams(("arbitrary",)),
        name="norm_matmul",
    )(x, g.reshape(1, D_MODEL), w, *xargs)
    return outs if t_cols is not None else outs[0]


FFN_CHUNK = 512


def _matmul_res_kernel(*refs, kinds):
    x_ref, o_ref = refs[-2], refs[-1]
    acc = x_ref[...]
    pos = 0
    for has_gate, channel_major in kinds:
        a = refs[pos][...]
        if channel_major:
            a = a.T
        if has_gate:
            a = a * refs[pos + 1][...]
        w_ref = refs[pos + 1 + has_gate]
        pos += 2 + has_gate
        acc = acc + jnp.dot(a.astype(BF16), w_ref[...], preferred_element_type=F32)
    o_ref[...] = acc


def matmul_res(terms, x, row0=0, nrows=None, into=None, total=None):
    m = x.shape[0] if total is None else total
    nrows = m if nrows is None else nrows
    t0 = row0 // TOKEN_TILE
    rows = lambda width: pl.BlockSpec((TOKEN_TILE, width), lambda i: (t0 + i, 0))
    xspec = rows(D_MODEL) if total is None else _rows(D_MODEL)
    specs, args, kinds = [], [], []
    for a, gate, w, seq_len in terms:
        if seq_len is None:
            specs.append(rows(a.shape[1]))
        elif seq_len == 0:
            specs.append(pl.BlockSpec((TOKEN_TILE, a.shape[1]), lambda i: (i, 0)))
        else:
            assert seq_len % TOKEN_TILE == 0
            per_seq = seq_len // TOKEN_TILE
            specs.append(pl.BlockSpec((None, a.shape[1], TOKEN_TILE), lambda i: (i // per_seq, 0, i % per_seq)))
        specs += ([rows(w.shape[0])] if gate is not None else []) + [_resident(w.shape)]
        args += [a] + ([gate] if gate is not None else []) + [w]
        kinds.append((int(gate is not None), bool(seq_len)))
    specs.append(xspec)
    body, xspecs, xargs, aliases = _fill_into(
        functools.partial(_matmul_res_kernel, kinds=tuple(kinds)), len(specs), into, 0)
    return pl.pallas_call(
        body,
        grid=(nrows // TOKEN_TILE,),
        in_specs=specs + xspecs,
        out_specs=rows(D_MODEL),
        out_shape=jax.ShapeDtypeStruct((m, D_MODEL), F32),
        input_output_aliases=aliases,
        compiler_params=_cparams(("arbitrary",)),
        name="matmul_res",
    )(*args, x, *xargs)


def _ffn_ple_kernel(x_ref, gf_ref, wu_ref, wd_ref, gp_ref, wg_ref, p_ref, wp_ref, *rest):
    gl_ref = rest[0] if len(rest) == 3 else None
    o_ref, y_s = rest[-2:]
    x = x_ref[...]
    xn = _rms(x, gf_ref[...]).astype(BF16)
    y_s[...] = x
    for c0 in range(0, D_FF, FFN_CHUNK):
        h = jnp.dot(xn, wu_ref[:, c0:c0 + FFN_CHUNK], preferred_element_type=F32)
        h = jnp.square(jnp.maximum(h, 0.0)).astype(BF16)
        y_s[...] += jnp.dot(h, wd_ref[c0:c0 + FFN_CHUNK, :], preferred_element_type=F32)
    y = y_s[...]
    yn = _rms(y, gp_ref[...]).astype(BF16)
    gate = jax.nn.sigmoid(jnp.dot(yn, wg_ref[...], preferred_element_type=F32))
    proj = jnp.dot(p_ref[...].astype(BF16), wp_ref[...], preferred_element_type=F32)
    out = y + proj * gate
    o_ref[...] = out if gl_ref is None else _rms(out, gl_ref[...])


def ffn_ple(x, g_ffn, wu, wd, g_ple, wg, p, wp, row0, into=None, g_last=None):
    t0 = row0 // TOKEN_TILE
    rows = pl.BlockSpec((TOKEN_TILE, D_MODEL), lambda i: (t0 + i, 0))
    row1 = lambda v: v.reshape(1, D_MODEL)
    in_specs = [rows, _resident((1, D_MODEL)), _resident(wu.shape), _resident(wd.shape),
                _resident((1, D_MODEL)), _resident(wg.shape), _rows(PLE_DIM), _resident(wp.shape)]
    args = [x, row1(g_ffn), wu, wd, row1(g_ple), wg, p, wp]
    if g_last is not None:
        assert into is None
        in_specs.append(_resident((1, D_MODEL)))
        args.append(row1(g_last))
    body, xspecs, xargs, aliases = _fill_into(_ffn_ple_kernel, len(in_specs), into, 0)
    return pl.pallas_call(
        body,
        grid=(p.shape[0] // TOKEN_TILE,),
        in_specs=in_specs + xspecs,
        out_specs=rows if g_last is None else _rows(D_MODEL),
        out_shape=jax.ShapeDtypeStruct(x.shape if g_last is None else (p.shape[0], D_MODEL), F32),
        input_output_aliases=aliases,
        scratch_shapes=[pltpu.VMEM((TOKEN_TILE, D_MODEL), F32)],
        compiler_params=_cparams(("arbitrary",)),
        name="ffn_ple",
    )(*args, *xargs)


def _rwkv_operands(k, a, r, kk_t, ka_t, rk_t):
    kk = k * kk_t
    kk = kk * lax.rsqrt(jnp.maximum(jnp.sum(kk * kk, axis=1, keepdims=True), 1e-24))
    k2 = k * (1.0 + (a - 1.0) * ka_t)
    return -kk, kk * a, k2, jnp.sum(r * k2 * rk_t, axis=1)


def _rwkv_step(s_ref, tile0, ntiles, r, w, k, a, b, vrows):
    ys = []
    for lt in range(ntiles):
        rows = slice((tile0 + lt) * B_DH, (tile0 + lt + 1) * B_DH)
        s = s_ref[rows, :]
        sa = jnp.sum(s * a, axis=0, keepdims=True)
        s = s * w + sa * b + vrows[lt:lt + 1] * k
        s_ref[rows, :] = s
        ys.append(jnp.sum(s * r, axis=0, keepdims=True))
    return jnp.concatenate(ys, axis=0)


LONG_T = 128
LONG_NLT = B_DH // 2
KEY_PITCH = B_DH + SUBLANES
VAL_PITCH = LONG_NLT + SUBLANES


def _rwkv_scan_long_kernel(r_ref, w_ref, k_ref, a_ref, v_ref, kkt_ref, kat_ref, rkt_ref, lnw_ref, lnb_ref, s0_ref,
                           yt_ref, s_ref, or_s, ow_s, ok_s, oa_s, av_s, v_s, y_s):
    nb = yt_ref.shape[0]

    @pl.when(pl.program_id(0) == 0)
    def _():
        s_ref[...] = s0_ref[...]
        for scr in (or_s, ow_s, ok_s, oa_s, av_s, v_s, y_s):
            scr[...] = jnp.zeros(scr.shape, F32)

    lo = lax.broadcasted_iota(jnp.int32, (LONG_T, LANES), 1) < LANES // 2

    def tile(ref, c):
        m = ref[c]
        return jnp.concatenate([m, m], axis=0).T

    def relayout(c, carry):
        for ref, dst in ((r_ref, or_s), (w_ref, ow_s), (k_ref, ok_s), (a_ref, oa_s)):
            dst[pl.ds(c, LONG_T, stride=KEY_PITCH), :] = tile(ref, c)
            dst[pl.ds(c + LONG_NLT, LONG_T, stride=KEY_PITCH), :] = tile(ref, c + LONG_NLT)
        v_s[pl.ds(c, LONG_T, stride=VAL_PITCH), :] = jnp.where(lo, tile(v_ref, c), tile(v_ref, c + LONG_NLT))
        return carry

    lax.fori_loop(0, LONG_NLT, relayout, 0)

    def unpad(scr, pitch, n):
        return scr[...].reshape(LONG_T, pitch, LANES)[:, :n]

    def pad(x, pitch):
        zeros = jnp.zeros((LONG_T, pitch - x.shape[1], LANES), F32)
        return jnp.concatenate([x, zeros], axis=1).reshape(LONG_T * pitch, LANES)

    av, bv, k2, bonus = _rwkv_operands(unpad(ok_s, KEY_PITCH, B_DH), unpad(oa_s, KEY_PITCH, B_DH),
                                       unpad(or_s, KEY_PITCH, B_DH), kkt_ref[...], kat_ref[...], rkt_ref[...])
    av_s[...] = pad(av, KEY_PITCH)
    oa_s[...] = pad(bv, KEY_PITCH)
    ok_s[...] = pad(k2, KEY_PITCH)

    def step(t, carry):
        kr = pl.ds(pl.multiple_of(t * KEY_PITCH, SUBLANES), B_DH)
        vr = pl.ds(pl.multiple_of(t * VAL_PITCH, SUBLANES), LONG_NLT)
        y_s[vr, :] = _rwkv_step(s_ref, 0, LONG_NLT, or_s[kr, :], ow_s[kr, :], ok_s[kr, :], av_s[kr, :],
                                oa_s[kr, :], v_s[vr, :])
        return carry

    lax.fori_loop(0, LONG_T, step, 0)

    def head_sum(x):
        tot = jnp.sum(x, axis=1)
        return tot + pltpu.roll(tot, LANES // 2, axis=1)

    y = unpad(y_s, VAL_PITCH, LONG_NLT)
    d = y - (head_sum(y) * (1.0 / B_DH))[:, None, :]
    var = head_sum(d * d) * (1.0 / B_DH)
    y = (d * lax.rsqrt(var + B_GN_EPS)[:, None, :] * lnw_ref[...] + lnb_ref[...]
         + bonus[:, None, :] * unpad(v_s, VAL_PITCH, LONG_NLT))
    y_s[...] = pad(y, VAL_PITCH)

    def relayout_out(lt, carry):
        yt = y_s[pl.ds(lt, LONG_T, stride=VAL_PITCH), :].T
        for i2 in range(2):
            for b in range(nb):
                row = i2 * (LANES // 2) + b * B_HEADS
                yt_ref[b, pl.ds(lt + LONG_NLT * i2, B_HEADS, stride=B_DH), :] = yt[row:row + B_HEADS, :]
        return carry

    lax.fori_loop(0, LONG_NLT, relayout_out, 0)


def rwkv_scan_long(rc, wc, kc, ac, vc, S0, k_k, k_a, r_k, ln_w, ln_b):
    _, bh, L = rc.shape
    b = bh // B_HEADS
    assert bh * 2 == LANES and L % LONG_T == 0
    per_key = lambda p: jnp.tile(p.reshape(B_HEADS, B_DH).T, (1, LANES // B_HEADS))
    per_val = lambda p: jnp.concatenate(
        [jnp.tile(p.reshape(B_HEADS, 2, LONG_NLT)[:, i2].T, (1, b)) for i2 in range(2)], axis=1)
    s0 = S0.reshape(b, B_HEADS, 2, LONG_NLT, B_DH).transpose(3, 4, 2, 0, 1).reshape(LONG_NLT * B_DH, LANES)
    blk = pl.BlockSpec((B_DH, bh, LONG_T), lambda c: (0, 0, c))
    oblk = pl.BlockSpec((b, B_W, LONG_T), lambda c: (0, 0, c))
    const = lambda shape: pl.BlockSpec(shape, lambda c: (0,) * len(shape))
    big = pltpu.VMEM((LONG_T * KEY_PITCH, LANES), F32)
    small = pltpu.VMEM((LONG_T * VAL_PITCH, LANES), F32)
    yt, s = pl.pallas_call(
        _rwkv_scan_long_kernel,
        grid=(L // LONG_T,),
        in_specs=[blk] * 5 + [const((B_DH, LANES))] * 3 + [const((LONG_NLT, LANES))] * 2
        + [_resident((LONG_NLT * B_DH, LANES))],
        out_specs=[oblk, const((LONG_NLT * B_DH, LANES))],
        out_shape=[jax.ShapeDtypeStruct((b, B_W, L), F32), jax.ShapeDtypeStruct(s0.shape, F32)],
        scratch_shapes=[big] * 5 + [small, small],
        compiler_params=_cparams(("arbitrary",), SCAN_VMEM_LIMIT),
        name="rwkv_scan_long",
    )(rc, wc, kc, ac, vc, per_key(k_k), per_key(k_a), per_key(r_k), per_val(ln_w), per_val(ln_b), s0)
    s = s.reshape(LONG_NLT, B_DH, 2, b, B_HEADS).transpose(3, 4, 2, 0, 1).reshape(b, B_HEADS, B_DH, B_DH)
    return yt, s


def _rwkv_scan_short_kernel(r_ref, w_ref, k_ref, a_ref, v_ref, kkt_ref, kat_ref, rkt_ref, lnw_ref, lnb_ref, s0_ref,
                            y_ref, sout_ref, s_s, or_s, ow_s, ok_s, oa_s, av_s, ov_s, y_s, *, L):
    nt = 2 * B_DH
    for q in range(nt * B_DH // LANES):
        s_s[q * LANES:(q + 1) * LANES, :] = s0_ref[:, q * LANES:(q + 1) * LANES].T
    for ref, dst in ((r_ref, or_s), (w_ref, ow_s), (k_ref, ok_s), (a_ref, oa_s), (v_ref, ov_s)):
        for t in range(L):
            dst[t] = ref[pl.ds(t, LANES, stride=L), :].T.reshape(2, B_DH, LANES)
    shape3 = (L * 2, B_DH, LANES)
    tiles = lambda ref: jnp.concatenate([ref[...]] * L, axis=0)
    av, bv, k2, bonus = _rwkv_operands(ok_s[...].reshape(shape3), oa_s[...].reshape(shape3),
                                       or_s[...].reshape(shape3), tiles(kkt_ref), tiles(kat_ref), tiles(rkt_ref))
    av_s[...] = av.reshape(L, 2, B_DH, LANES)
    oa_s[...] = bv.reshape(L, 2, B_DH, LANES)
    ok_s[...] = k2.reshape(L, 2, B_DH, LANES)

    def step(t, carry):
        for h in range(2):
            y_s[t, h] = _rwkv_step(s_s, h * B_DH, B_DH, or_s[t, h], ow_s[t, h], ok_s[t, h], av_s[t, h],
                                   oa_s[t, h], ov_s[t, h])
        return carry

    lax.fori_loop(0, L, step, 0)

    y = y_s[...]
    d = y - jnp.mean(y, axis=2, keepdims=True)
    var = jnp.mean(d * d, axis=2, keepdims=True)
    y = (d * lax.rsqrt(var + B_GN_EPS) * lnw_ref[...] + lnb_ref[...]
         + bonus.reshape(L, 2, 1, LANES) * ov_s[...])
    for t in range(L):
        y_ref[pl.ds(t, LANES, stride=L), :] = y[t].reshape(nt, LANES).T
    for q in range(nt * B_DH // LANES):
        sout_ref[:, q * LANES:(q + 1) * LANES] = s_s[q * LANES:(q + 1) * LANES, :].T


def rwkv_scan_short(r, w, k, a, v, S_all, layer, s_into, L, k_k, k_a, r_k, ln_w, ln_b):
    n = r.shape[0]
    b = n // L
    assert b == LANES
    npair = B_HEADS // 2
    wide = lambda p: jnp.broadcast_to(p.reshape(npair, 2, B_DH, 1), (npair, 2, B_DH, LANES))
    blk = pl.BlockSpec((n, LANES), lambda p: (0, p))
    cblk = pl.BlockSpec((None, 2, B_DH, LANES), lambda p: (p, 0, 0, 0))
    sblk = pl.BlockSpec((None, b, 2 * B_DH * B_DH), lambda p: (layer, 0, p))
    op = pltpu.VMEM((L, 2, B_DH, LANES), F32)
    in_specs = [blk] * 5 + [cblk] * 5 + [sblk]
    body, xspecs, xargs, aliases = _fill_into(
        functools.partial(_rwkv_scan_short_kernel, L=L), len(in_specs), s_into, 1)
    return pl.pallas_call(
        body,
        grid=(npair,),
        in_specs=in_specs + xspecs,
        out_specs=[blk, sblk],
        out_shape=[jax.ShapeDtypeStruct((n, B_W), F32), jax.ShapeDtypeStruct(S_all.shape, F32)],
        input_output_aliases=aliases,
        scratch_shapes=[pltpu.VMEM((2 * B_DH * B_DH, LANES), F32)] + [op] * 7,
        compiler_params=_cparams(("arbitrary",)),
        name="rwkv_scan_short",
    )(r, w, k, a, v, wide(k_k), wide(k_a), wide(r_k), wide(ln_w), wide(ln_b), S_all, *xargs)


RWKV_PREP_ROWS = 256
RWKV_SHORT_ROWS = 64
B_LORA_OFF = 3 * B_W


def _rwkv_prep_kernel(z_ref, sh0_ref, mu_ref, wwa_ref, g2_ref, w0_ref, a0_ref,
                      r_ref, w_ref, k_ref, a_ref, v_ref, g_ref, sh_ref, *scratch, T, nseq, channel_major):
    @pl.when(pl.program_id(1) == 0)
    def _():
        sh_ref[...] = sh0_ref[...]

    z = z_ref[...]
    rowid = lax.broadcasted_iota(jnp.int32, (z.shape[0], 1), 0)
    zprev = pltpu.roll(z, 1, axis=0)
    for u in range(nseq):
        zprev = jnp.where(rowid == u * T, sh_ref[u], zprev)
    for u in range(nseq):
        sh_ref[u] = z[(u + 1) * T - 1:(u + 1) * T, :]
    zs = z + (zprev - z) * mu_ref[...]
    r = zs[:, :B_W]
    k = zs[:, B_W:2 * B_W]
    lora = zs[:, B_LORA_OFF:B_LORA_OFF + LANES]
    lane = lax.broadcasted_iota(jnp.int32, lora.shape, 1)
    lora = jnp.where(lane < B_W_RANK, jnp.tanh(lora), lora).astype(BF16)
    wa = jnp.dot(lora, wwa_ref[...], preferred_element_type=F32)
    w_log = -jax.nn.softplus(-(w0_ref[...] + wa[:, :B_W])) - B_DECAY_OFFSET
    a = jax.nn.sigmoid(a0_ref[...] + wa[:, B_W:])
    zg = zs[:, B_LORA_OFF + LANES:B_LORA_OFF + LANES + B_G_RANK]
    def emit(ref, x):
        if not channel_major:
            ref[...] = x
            return
        xt_s = scratch[0]
        xt = x.T
        for j in range(xt_s.shape[0]):
            for h in range(B_HEADS):
                xt_s[j, h * KEY_PITCH:h * KEY_PITCH + B_DH, :] = xt[h * B_DH:(h + 1) * B_DH, j * LANES:(j + 1) * LANES]
        for c in range(B_DH):
            for j in range(xt_s.shape[0]):
                ref[c, :, j * LANES:(j + 1) * LANES] = xt_s[j, pl.ds(c, B_HEADS, stride=KEY_PITCH), :]

    emit(r_ref, r)
    emit(w_ref, jnp.exp(-jnp.exp(w_log)))
    emit(k_ref, k)
    emit(a_ref, a)
    emit(v_ref, zs[:, 2 * B_W:3 * B_W])
    g_ref[...] = jnp.dot(jax.nn.sigmoid(zg).astype(BF16), g2_ref[...], preferred_element_type=F32)


def rwkv_prep(zin, row0, b, L, gate_into, shift0, mu, w0, w2, a0, a2, g2):
    assert B_W_RANK + B_A_RANK == LANES
    T = math.gcd(L, RWKV_PREP_ROWS)
    channel_major = T == RWKV_PREP_ROWS
    nseq = 1 if channel_major else RWKV_SHORT_ROWS // T
    assert nseq == 1 or (L == T and b % nseq == 0)
    R = T * nseq
    nchunk = L // T
    blk0 = row0 // R
    wwa = jnp.zeros((LANES, 2 * B_W), F32).at[:B_W_RANK, :B_W].set(w2).at[B_W_RANK:, B_W:].set(a2).astype(BF16)
    row = lambda v: v.reshape(1, -1)
    if channel_major:
        blk = pl.BlockSpec((B_DH, B_HEADS, R), lambda i, c: (0, i, c))
        oshape = jax.ShapeDtypeStruct((B_DH, b * B_HEADS, L), F32)
    else:
        blk = pl.BlockSpec((R, B_W), lambda i, c: (i * nchunk + c, 0))
        oshape = jax.ShapeDtypeStruct((b * L, B_W), F32)
    gblk = pl.BlockSpec((R, B_W), lambda i, c: (blk0 + i * nchunk + c, 0))
    shspec = pl.BlockSpec((nseq, 1, B_COLS), lambda i, c: (i, 0, 0))
    const = lambda shape: pl.BlockSpec(shape, lambda i, c: (0,) * len(shape))
    in_specs = [pl.BlockSpec((R, B_COLS), lambda i, c: (blk0 + i * nchunk + c, 0)), shspec, const((1, B_COLS)),
                const((LANES, 2 * B_W)), const((B_G_RANK, B_W)), const((1, B_W)), const((1, B_W))]
    body, xspecs, xargs, aliases = _fill_into(
        functools.partial(_rwkv_prep_kernel, T=T, nseq=nseq, channel_major=channel_major),
        len(in_specs), gate_into, 5)
    outs = pl.pallas_call(
        body,
        grid=(b // nseq, nchunk),
        in_specs=in_specs + xspecs,
        out_specs=[blk] * 5 + [gblk, shspec],
        out_shape=[oshape] * 5
        + [jax.ShapeDtypeStruct((zin.shape[0], B_W), F32), jax.ShapeDtypeStruct((b, 1, B_COLS), F32)],
        input_output_aliases=aliases,
        scratch_shapes=[pltpu.VMEM((R // LANES, B_HEADS * KEY_PITCH, LANES), F32)] if channel_major else [],
        compiler_params=_cparams(("arbitrary", "arbitrary")),
        name="rwkv_prep",
    )(zin, shift0.reshape(b, 1, B_COLS), row(mu), wwa, g2.astype(BF16), row(w0), row(a0), *xargs)
    return outs[:5], outs[5], outs[6].reshape(b, B_COLS)


MLSTM_ROWS = 128
HIGHEST = lax.Precision.HIGHEST


def _mlstm_kernel(q_ref, k_ref, v_ref, o_ref, g_ref, gt_ref, brow_ref, bcol_ref, nw_ref,
                  c0_ref, n0_ref, m0_ref, h_ref, c_ref, n_ref, m_ref, *, T, nseq):
    R = MLSTM_ROWS

    @pl.when(pl.program_id(1) == 0)
    def _():
        c_ref[...] = c0_ref[...]
        n_ref[...] = n0_ref[...]
        m_ref[...] = m0_ref[...]

    shift = T.bit_length() - 1
    ri = lax.broadcasted_iota(jnp.int32, (R, R), 0)
    ci = lax.broadcasted_iota(jnp.int32, (R, R), 1)
    mask = (ci <= ri) & (jnp.right_shift(ri, shift) == jnp.right_shift(ci, shift))
    lmat = mask.astype(F32)
    rowid = lax.broadcasted_iota(jnp.int32, (R, 1), 0)
    rsel = [(rowid >= u * T) & (rowid < (u + 1) * T) for u in range(nseq)]

    g = g_ref[...] + brow_ref[...]
    lane = lax.broadcasted_iota(jnp.int32, g.shape, 1)
    glog = jnp.where((lane >= A_HEADS) & (lane < 2 * A_HEADS), jax.nn.log_sigmoid(g), g)
    gt = gt_ref[...] + bcol_ref[...]
    sub = lax.broadcasted_iota(jnp.int32, gt.shape, 0)
    gtlog = jnp.where(sub >= A_HEADS, jax.nn.log_sigmoid(gt), gt)
    bc_col = jnp.dot(lmat, glog, precision=HIGHEST, preferred_element_type=F32)
    bc_row = lax.dot_general(gtlog, lmat, (((1,), (1,)), ((), ())), precision=HIGHEST,
                             preferred_element_type=F32)
    lane_m = lax.broadcasted_iota(jnp.int32, (1, LANES), 1)
    m_old = [m_ref[u] for u in range(nseq)]
    m_out = [jnp.zeros((1, LANES), F32) for _ in range(nseq)]

    for h in range(A_HEADS):
        hs = slice(h * A_DK, (h + 1) * A_DK)
        bcc = bc_col[:, A_HEADS + h:A_HEADS + h + 1]
        bcr = bc_row[A_HEADS + h:A_HEADS + h + 1, :]
        lir = gtlog[h:h + 1, :]
        lic = glog[:, h:h + 1]
        m_u = [m_old[u][:, h:h + 1] for u in range(nseq)]
        m_col = m_u[0]
        for u in range(1, nseq):
            m_col = jnp.where(rsel[u], m_u[u], m_col)
        dmat = jnp.where(mask, bcc - bcr + lir, -jnp.inf)
        inter = bcc + m_col
        mt = jnp.maximum(inter, jnp.max(dmat, axis=1, keepdims=True))
        p = jnp.exp(dmat - mt)
        qh = q_ref[:, hs] * (A_DK ** -0.5)
        kh = k_ref[:, hs]
        qb, kb, vb = qh.astype(BF16), kh.astype(BF16), v_ref[:, hs].astype(BF16)
        wq = lax.dot_general(qb, kb, (((1,), (1,)), ((), ())), preferred_element_type=F32) * p
        wi = jnp.exp(inter - mt)
        c_old = [c_ref[u, h] for u in range(nseq)]
        n_old = [n_ref[u, h:h + 1, :] for u in range(nseq)]
        qc = jnp.dot(qb, c_old[0].astype(BF16), preferred_element_type=F32)
        qn = jnp.sum(qh * n_old[0], axis=1, keepdims=True)
        for u in range(1, nseq):
            qc = jnp.where(rsel[u], jnp.dot(qb, c_old[u].astype(BF16), preferred_element_type=F32), qc)
            qn = jnp.where(rsel[u], jnp.sum(qh * n_old[u], axis=1, keepdims=True), qn)
        num = jnp.dot(wq.astype(BF16), vb, preferred_element_type=F32) + wi * qc
        den = jnp.sum(wq, axis=1, keepdims=True) + wi * qn
        hh = num / jnp.maximum(jnp.abs(den), jnp.exp(-mt))
        hh = hh * lax.rsqrt(jnp.mean(hh * hh, axis=-1, keepdims=True) + EPS)
        h_ref[:, hs] = (hh * nw_ref[:, hs] * jax.nn.sigmoid(o_ref[:, hs])).astype(BF16)
        for u in range(nseq):
            b_last = bcc[(u + 1) * T - 1:(u + 1) * T, :]
            gs = b_last - bcc + lic
            gmax = jnp.max(gs if nseq == 1 else jnp.where(rsel[u], gs, -jnp.inf), axis=0, keepdims=True)
            m_new = jnp.maximum(b_last + m_u[u], gmax)
            decay = jnp.exp(b_last + m_u[u] - m_new)
            ws = jnp.exp(gs - m_new)
            if nseq > 1:
                ws = jnp.where(rsel[u], ws, 0.0)
            kw = kh * ws
            c_ref[u, h] = decay * c_old[u] + lax.dot_general(
                kw.astype(BF16), vb, (((0,), (0,)), ((), ())), preferred_element_type=F32)
            n_ref[u, h:h + 1, :] = decay * n_old[u] + jnp.sum(kw, axis=0, keepdims=True)
            m_out[u] = jnp.where(lane_m == h, m_new, m_out[u])
    for u in range(nseq):
        m_ref[u] = m_out[u]


def mlstm(zin, gates_t, row0, b, L, h_into, layer, c_into, b_i, b_f, m_norm, C0, n0, m0):
    R = MLSTM_ROWS
    T = math.gcd(L, R)
    nseq = R // T
    assert nseq == 1 or (L == T and b % nseq == 0)
    nchunk = L // T
    blk0 = row0 // R
    bias = jnp.concatenate([b_i, b_f])
    bias_row = jnp.zeros((1, LANES), F32).at[0, :2 * A_HEADS].set(bias)
    bias_col = bias.reshape(2 * A_HEADS, 1)
    m0p = jnp.zeros((b, 1, LANES), F32).at[:, 0, :A_HEADS].set(m0)
    blk = lambda i, c: blk0 + i * nchunk + c
    rowblk = lambda col: pl.BlockSpec((R, A_QK), lambda i, c: (blk(i, c), EVEN_A_OFF // A_QK + col))
    cspec = pl.BlockSpec((nseq, A_HEADS, A_DK, A_DV), lambda i, c: (i, 0, 0, 0))
    cout = pl.BlockSpec((None, nseq, A_HEADS, A_DK, A_DV), lambda i, c: (layer, i, 0, 0, 0))
    nspec = pl.BlockSpec((nseq, A_HEADS, A_DK), lambda i, c: (i, 0, 0))
    mspec = pl.BlockSpec((nseq, 1, LANES), lambda i, c: (i, 0, 0))
    const = lambda shape: pl.BlockSpec(shape, lambda i, c: (0,) * len(shape))
    in_specs = [rowblk(0), rowblk(1), rowblk(2), rowblk(3),
                pl.BlockSpec((R, LANES), lambda i, c: (blk(i, c), EVEN_G_OFF // LANES)),
                pl.BlockSpec((2 * A_HEADS, R), lambda i, c: (0, blk(i, c))),
                const((1, LANES)), const((2 * A_HEADS, 1)), const((1, A_V)),
                cspec if C0.ndim == 4 else cout, nspec, mspec]
    body, xspecs, xargs, aliases = _fill_into(
        functools.partial(_mlstm_kernel, T=T, nseq=nseq), len(in_specs), h_into, 0, c_into, 1)
    h, C, n, m = pl.pallas_call(
        body,
        grid=(b // nseq, nchunk),
        in_specs=in_specs + xspecs,
        out_specs=[pl.BlockSpec((R, A_V), lambda i, c: (blk(i, c), 0)), cout, nspec, mspec],
        out_shape=[jax.ShapeDtypeStruct((zin.shape[0], A_V), BF16),
                   jax.ShapeDtypeStruct((N_EVEN,) + C0.shape[-4:], F32), jax.ShapeDtypeStruct(n0.shape, F32),
                   jax.ShapeDtypeStruct(m0p.shape, F32)],
        input_output_aliases=aliases,
        compiler_params=_cparams(("arbitrary", "arbitrary")),
        name="mlstm",
    )(zin, zin, zin, zin, zin, gates_t, bias_row, bias_col, m_norm.reshape(1, A_V), C0, n0, m0p, *xargs)
    return h, C, n, m[:, 0, :A_HEADS]


def even_mixer(zin, gates_t, row0, b, L, into, layer, c_into, C0, n0, m0, S0, shift0, b_i, b_f, m_norm,
               mu, w0, w2, a0, a2, g2, k_k, k_a, r_k, ln_w, ln_b):
    hA, C, n, m = mlstm(zin, gates_t, row0, b, L, into[0], layer, c_into[0], b_i, b_f, m_norm, C0, n0, m0)
    ops, g, shift = rwkv_prep(zin, row0, b, L, into[1], shift0, mu, w0, w2, a0, a2, g2)
    if ops[0].ndim == 3:
        hB, S = rwkv_scan_long(*ops, S0, k_k, k_a, r_k, ln_w, ln_b)
    else:
        hB, S = rwkv_scan_short(*ops, S0, layer, c_into[1], L, k_k, k_a, r_k, ln_w, ln_b)
    return hA, hB, g, (C, n, m, S, shift)


C_PAIRS = C_HEADS // 2
C_GROUP_W = C_INNER // C_GROUPS
C_BC_W = 2 * C_GROUPS * C_STATE
SSD_LONG_ROWS = C_CHUNK
SSD_SHORT_ROWS = 32


def _ssd_kernel(z_ref, x_ref, bc_ref, dt_ref, dtt_ref, cw_ref, cb_ref, dtb_ref, dtbt_ref, al_ref, alt_ref,
                dsk_ref, nw_ref, s0_ref, cv0_ref, y_ref, s_ref, cvo_ref, stg_ref, yf_s, cv_ref, *, T, nseq, nchunk):
    R = T * nseq
    tstate = nchunk > 1
    assert not tstate or nseq == 1

    @pl.when(pl.program_id(1) == 0)
    def _():
        if tstate:
            for pr in range(C_PAIRS):
                s_ref[0, pr] = s0_ref[0, pr].T
        else:
            s_ref[...] = s0_ref[...]
        cv_ref[...] = jnp.zeros(cv_ref.shape, F32)
        for u in range(nseq):
            cv_ref[u, SUBLANES - (C_CONV - 1):SUBLANES, :] = cv0_ref[u]

    shift = T.bit_length() - 1
    ri = lax.broadcasted_iota(jnp.int32, (R, R), 0)
    ci = lax.broadcasted_iota(jnp.int32, (R, R), 1)
    mask = (ci <= ri) & (jnp.right_shift(ri, shift) == jnp.right_shift(ci, shift))
    lmat = mask.astype(F32)
    rowid = lax.broadcasted_iota(jnp.int32, (R, 1), 0)
    rsel = [(rowid >= u * T) & (rowid < (u + 1) * T) for u in range(nseq)]

    def conv_silu(src_ref, src_col, col):
        cols = slice(col, col + LANES)
        accs = []
        for u in range(nseq):
            stg_ref[u, 0:SUBLANES, cols] = cv_ref[u, :, cols]
            stg_ref[u, SUBLANES:SUBLANES + T, cols] = src_ref[u * T:(u + 1) * T, src_col:src_col + LANES]
            acc = cb_ref[:, cols]
            for d in range(C_CONV):
                acc = acc + stg_ref[u, SUBLANES - d:SUBLANES - d + T, cols] * cw_ref[C_CONV - 1 - d:C_CONV - d, cols]
            accs.append(acc)
            cv_ref[u, :, cols] = stg_ref[u, T:T + SUBLANES, cols]
        acc = accs[0] if nseq == 1 else jnp.concatenate(accs, axis=0)
        return acc * jax.nn.sigmoid(acc)

    dtv = jax.nn.softplus(dt_ref[...] + dtb_ref[...])
    dtt = jax.nn.softplus(dtt_ref[...] + dtbt_ref[...])
    cum_col = jnp.dot(lmat, dtv * (-jnp.exp(al_ref[...])), precision=HIGHEST, preferred_element_type=F32)
    cum_row = lax.dot_general(dtt * (-jnp.exp(alt_ref[...])), lmat, (((1,), (1,)), ((), ())),
                              precision=HIGHEST, preferred_element_type=F32)

    lo = lax.broadcasted_iota(jnp.int32, (R, LANES), 1) < C_HEADDIM
    rlo = lax.broadcasted_iota(jnp.int32, (LANES, 1), 0) < C_HEADDIM
    nt = (((1,), (1,)), ((), ()))
    tn = (((0,), (0,)), ((), ()))
    pairs_per_group = C_PAIRS // C_GROUPS
    for g in range(C_GROUPS):
        bgf = conv_silu(bc_ref, g * C_STATE, C_INNER + g * C_STATE)
        bg = bgf.astype(BF16)
        cg = conv_silu(bc_ref, (C_GROUPS + g) * C_STATE, C_INNER + (C_GROUPS + g) * C_STATE).astype(BF16)
        cbm = lax.dot_general(cg, bg, nt, preferred_element_type=F32)
        ys = None
        if tstate:
            bgt = bgf.T.astype(BF16)
            sgt = jnp.concatenate([s_ref[0, g * pairs_per_group + q] for q in range(pairs_per_group)], axis=1)
            ys = jnp.dot(cg, sgt.astype(BF16), preferred_element_type=F32)
        for u in range(0 if tstate else nseq):
            sg = s_ref[u, g * pairs_per_group:(g + 1) * pairs_per_group].reshape(C_GROUP_W, C_STATE)
            t_u = lax.dot_general(cg, sg.astype(BF16), nt, preferred_element_type=F32)
            ys = t_u if u == 0 else jnp.where(rsel[u], t_u, ys)
        for q in range(pairs_per_group):
            pr = g * pairs_per_group + q
            ps = slice(pr * LANES, (pr + 1) * LANES)
            xp = conv_silu(x_ref, pr * LANES, pr * LANES)
            cc = [cum_col[:, 2 * pr + e:2 * pr + e + 1] for e in range(2)]
            intra = None
            for e, keep in ((0, lo), (1, jnp.logical_not(lo))):
                hh = 2 * pr + e
                seg = jnp.exp(jnp.where(mask, cc[e] - cum_row[hh:hh + 1, :], -jnp.inf))
                mix = cbm * seg * dtt[hh:hh + 1, :]
                part = jnp.dot(mix.astype(BF16), jnp.where(keep, xp, 0.0).astype(BF16),
                               preferred_element_type=F32)
                intra = part if intra is None else intra + part
            scale = jnp.where(lo, jnp.exp(cc[0]), jnp.exp(cc[1]))
            yp = intra + scale * ys[:, q * LANES:(q + 1) * LANES] + dsk_ref[:, ps] * xp
            zp = z_ref[:, ps]
            yf_s[:, ps] = yp * (zp * jax.nn.sigmoid(zp))
            for u in range(nseq):
                last = (u + 1) * T - 1
                ct = [cc[e][last:last + 1, :] for e in range(2)]
                tail = jnp.where(lo, jnp.exp(ct[0] - cc[0]) * dtv[:, 2 * pr:2 * pr + 1],
                                 jnp.exp(ct[1] - cc[1]) * dtv[:, 2 * pr + 1:2 * pr + 2])
                xw = xp * tail
                if nseq > 1:
                    xw = jnp.where(rsel[u], xw, 0.0)
                if tstate:
                    upd = jnp.dot(bgt, xw.astype(BF16), preferred_element_type=F32)
                    dec = jnp.where(lo[:1], jnp.exp(ct[0]), jnp.exp(ct[1]))
                else:
                    upd = lax.dot_general(xw.astype(BF16), bg, tn, preferred_element_type=F32)
                    dec = jnp.where(rlo, jnp.exp(ct[0]), jnp.exp(ct[1]))
                s_ref[u, pr] = dec * s_ref[u, pr] + upd

    if tstate:
        @pl.when(pl.program_id(1) == nchunk - 1)
        def _():
            for pr in range(C_PAIRS):
                s_ref[0, pr] = s_ref[0, pr].T

    for u in range(nseq):
        cvo_ref[u] = cv_ref[u, SUBLANES - (C_CONV - 1):SUBLANES, :]

    for g in range(C_GROUPS):
        gs_ = slice(g * C_GROUP_W, (g + 1) * C_GROUP_W)
        yg = yf_s[:, gs_]
        yg = yg * lax.rsqrt(jnp.mean(yg * yg, axis=-1, keepdims=True) + EPS) * nw_ref[:, gs_]
        y_ref[:, gs_] = yg.astype(BF16)


def ssd_mixer(zin, dt_all_t, row0, b, L, y_into, layer, s_into, ssm0, conv0, conv_w, conv_b, dt_bias, a_log, d_skip,
              norm_w):
    T = math.gcd(L, SSD_LONG_ROWS)
    nseq = 1 if T == SSD_LONG_ROWS else SSD_SHORT_ROWS // T
    assert nseq == 1 or (L == T and b % nseq == 0)
    R = T * nseq
    nchunk = L // T
    nblk = b * L // R
    blk0 = row0 // R
    if R % LANES == 0:
        dt_t, dt_spec = dt_all_t, pl.BlockSpec((C_HEADS, R), lambda i, c: (0, blk0 + i * nchunk + c))
    else:
        dt_t = dt_all_t[:, row0:row0 + b * L].reshape(C_HEADS, nblk, R).transpose(1, 0, 2)
        dt_spec = pl.BlockSpec((None, C_HEADS, R), lambda i, c: (i * nchunk + c, 0, 0))
    pad_row = lambda v: jnp.zeros((1, LANES), F32).at[0, :C_HEADS].set(v)
    pair_shape = (b, C_PAIRS, 2 * C_HEADDIM, C_STATE)
    per_layer = ssm0.ndim == 4
    s0 = ssm0.reshape(pair_shape if per_layer else (ssm0.shape[0],) + pair_shape)
    blk = lambda i, c: blk0 + i * nchunk + c
    sspec = pl.BlockSpec((nseq, C_PAIRS, 2 * C_HEADDIM, C_STATE), lambda i, c: (i, 0, 0, 0))
    sout = pl.BlockSpec((None, nseq, C_PAIRS, 2 * C_HEADDIM, C_STATE), lambda i, c: (layer, i, 0, 0, 0))
    cvspec = pl.BlockSpec((nseq, C_CONV - 1, C_CONV_DIM), lambda i, c: (i, 0, 0))
    const = lambda shape: pl.BlockSpec(shape, lambda i, c: (0,) * len(shape))
    in_specs = [pl.BlockSpec((R, C_INNER), lambda i, c: (blk(i, c), 0)),
                pl.BlockSpec((R, C_INNER), lambda i, c: (blk(i, c), 1)),
                pl.BlockSpec((R, C_BC_W), lambda i, c: (blk(i, c), 2 * C_INNER // C_BC_W)),
                pl.BlockSpec((R, LANES), lambda i, c: (blk(i, c), ODD_DT_OFF // LANES)),
                dt_spec,
                const((C_CONV, C_CONV_DIM)), const((1, C_CONV_DIM)),
                const((1, LANES)), const((C_HEADS, 1)), const((1, LANES)), const((C_HEADS, 1)),
                const((1, C_INNER)), const((1, C_INNER)), sspec if per_layer else sout, cvspec]
    body, xspecs, xargs, aliases = _fill_into(
        functools.partial(_ssd_kernel, T=T, nseq=nseq, nchunk=nchunk), len(in_specs), y_into, 0, s_into, 1)
    y, s, cv = pl.pallas_call(
        body,
        grid=(b // nseq, nchunk),
        in_specs=in_specs + xspecs,
        out_specs=[pl.BlockSpec((R, C_INNER), lambda i, c: (blk(i, c), 0)), sout, cvspec],
        out_shape=[jax.ShapeDtypeStruct((zin.shape[0], C_INNER), BF16),
                   jax.ShapeDtypeStruct((N_ODD,) + pair_shape, F32), jax.ShapeDtypeStruct(conv0.shape, F32)],
        input_output_aliases=aliases,
        scratch_shapes=[pltpu.VMEM((nseq, SUBLANES + T, C_CONV_DIM), F32), pltpu.VMEM((R, C_INNER), F32),
                        pltpu.VMEM((nseq, SUBLANES, C_CONV_DIM), F32)],
        compiler_params=_cparams(("arbitrary", "arbitrary")),
        name="ssd",
    )(zin, zin, zin, zin, dt_t, conv_w, conv_b.reshape(1, C_CONV_DIM),
      pad_row(dt_bias), dt_bias.reshape(C_HEADS, 1), pad_row(a_log), a_log.reshape(C_HEADS, 1),
      jnp.repeat(d_skip, C_HEADDIM).reshape(1, C_INNER), norm_w.reshape(1, C_INNER), s0, conv0, *xargs)
    return y, s, cv


def _even_w_in(w):
    qkvo = w[:, :2 * A_QK + 2 * A_V]
    gates = w[:, 2 * A_QK + 2 * A_V:A_COLS]
    rwkv = w[:, A_COLS:]
    pad = jnp.zeros((D_MODEL, EVEN_GATE_PAD - 2 * A_HEADS), w.dtype)
    return jnp.concatenate([rwkv, gates, pad, qkvo], axis=1).astype(BF16)


def _odd_w_in(w):
    pad = jnp.zeros((D_MODEL, ODD_DT_PAD - C_HEADS), w.dtype)
    return jnp.concatenate([w, pad], axis=1).astype(BF16)


def kernel(x_prompt, x_sample, state_mlstm_C, state_mlstm_n, state_mlstm_m, state_rwkv_S,
           state_rwkv_shift, state_ssm, state_conv, p_prompt, p_sample,
           norm_mix, norm_ffn, w_ffn_up, w_ffn_down, w_ple_proj, norm_ple, w_ple_gate, norm_final,
           w_in_even, mlstm_b_i, mlstm_b_f, mlstm_norm, rwkv_mu, rwkv_w0, rwkv_w2, rwkv_a0, rwkv_a2,
           rwkv_g2, rwkv_k_k, rwkv_k_a, rwkv_r_k, rwkv_ln_w, rwkv_ln_b, w_out_even,
           w_in_odd, conv_w, conv_b, dt_bias, a_log, d_skip, ssm_norm, w_out_odd):
    bp, Lp, _ = x_prompt.shape
    bs, Ls, _ = x_sample.shape
    n_p, n_s = bp * Lp, bs * Ls
    n_tot = n_p + n_s
    xp = x_prompt.reshape(n_p, D_MODEL)
    xs = x_sample.reshape(n_s, D_MODEL)
    pp = p_prompt.reshape(DEPTH, n_p, PLE_DIM)
    ps = p_sample.reshape(DEPTH, n_s, PLE_DIM)
    x = None

    even_small = (mlstm_b_i, mlstm_b_f, mlstm_norm, rwkv_mu, rwkv_w0, rwkv_w2, rwkv_a0, rwkv_a2,
                  rwkv_g2, rwkv_k_k, rwkv_k_a, rwkv_r_k, rwkv_ln_w, rwkv_ln_b)
    odd_small = (conv_w, conv_b, dt_bias, a_log, d_skip, ssm_norm)

    zeros_even = (jnp.zeros((bp, A_HEADS, A_DK, A_DV), F32), jnp.zeros((bp, A_HEADS, A_DK), F32),
                  jnp.zeros((bp, A_HEADS), F32), jnp.zeros((bp, B_HEADS, B_DH, B_DH), F32),
                  jnp.zeros((bp, B_COLS), F32))
    zeros_odd = (jnp.zeros((bp, C_HEADS, C_HEADDIM, C_STATE), F32),
                 jnp.zeros((bp, C_CONV - 1, C_CONV_DIM), F32))

    st_p_even, st_s_even, st_p_odd, st_s_odd = [], [], [], []
    mC_p = mC_s = rS_s = ssm_p = ssm_s = None
    rs_flat = state_rwkv_S.reshape(N_EVEN, bs, B_HEADS * B_DH * B_DH)
    for i in range(DEPTH):
        j = i // 2
        if i % 2 == 0:
            w_in = _even_w_in(w_in_even[j])
            gate_cols = (EVEN_G_OFF, 2 * A_HEADS)
            if x is None:
                zin, gates_t = norm_matmul(xp, norm_mix[i], w_in, 512, 0, n_tot, None, gate_cols)
                zin, gates_t = norm_matmul(xs, norm_mix[i], w_in, 512, n_p, n_tot, zin, gate_cols, gates_t)
            else:
                zin, gates_t = norm_matmul(x, norm_mix[i], w_in, 512, t_cols=gate_cols)
            small = [t[j] for t in even_small]
            ha, hb_p, g, sp = even_mixer(zin, gates_t, 0, bp, Lp, (None, None), j, (mC_p, None), *zeros_even, *small)
            ha, hb_s, g, ss = even_mixer(zin, gates_t, n_p, bs, Ls, (ha, g), j, (mC_s, rS_s), state_mlstm_C,
                                         state_mlstm_n[j], state_mlstm_m[j], rs_flat, state_rwkv_shift[j],
                                         *small)
            mC_p, mC_s, rS_s = sp[0], ss[0], ss[3]
            st_p_even.append(sp)
            st_s_even.append(ss)
            wo = w_out_even[j].astype(BF16)
            terms_p = [(ha, None, wo[:A_V], None), (hb_p, g, wo[A_V:], Lp)]
            terms_s = [(ha, None, wo[:A_V], None), (hb_s, g, wo[A_V:], 0)]
            if x is None:
                xo = matmul_res(terms_p, xp, 0, n_p, None, n_tot)
                x = matmul_res(terms_s, xs, n_p, n_s, xo, n_tot)
            else:
                xo = matmul_res(terms_p, x, 0, n_p)
                x = matmul_res(terms_s, x, n_p, n_s, xo)
        else:
            zin, dt_all_t = norm_matmul(x, norm_mix[i], _odd_w_in(w_in_odd[j]), 768,
                                        t_cols=(ODD_DT_OFF, C_HEADS))
            small = [t[j] for t in odd_small]
            mix, ssm_p, cv_p = ssd_mixer(zin, dt_all_t, 0, bp, Lp, None, j, ssm_p, *zeros_odd, *small)
            mix, ssm_s, cv_s = ssd_mixer(zin, dt_all_t, n_p, bs, Ls, mix, j, ssm_s, state_ssm, state_conv[j], *small)
            st_p_odd.append(cv_p)
            st_s_odd.append(cv_s)
            x = matmul_res([(mix, None, w_out_odd[j].astype(BF16), None)], x)
        tail = (norm_ffn[i], w_ffn_up[i].astype(BF16), w_ffn_down[i].astype(BF16),
                norm_ple[i], w_ple_gate[i].astype(BF16))
        wp = w_ple_proj[i].astype(BF16)
        if i < DEPTH - 1:
            xo = ffn_ple(x, *tail, pp[i], wp, 0)
            x = ffn_ple(x, *tail, ps[i], wp, n_p, xo)
        else:
            y_prompt = ffn_ple(x, *tail, pp[i], wp, 0, None, norm_final).reshape(bp, Lp, D_MODEL)
            y_sample = ffn_ple(x, *tail, ps[i], wp, n_p, None, norm_final).reshape(bs, Ls, D_MODEL)
    stack = lambda sts, idx: jnp.stack([s[idx] for s in sts])
    ssm_shape = lambda b: (N_ODD, b, C_HEADS, C_HEADDIM, C_STATE)
    return (y_prompt, y_sample,
            mC_p, stack(st_p_even, 1), stack(st_p_even, 2), stack(st_p_even, 3),
            stack(st_p_even, 4), ssm_p.reshape(ssm_shape(bp)), jnp.stack(st_p_odd),
            mC_s, stack(st_s_even, 1), stack(st_s_even, 2), rS_s.reshape(state_rwkv_S.shape),
            stack(st_s_even, 4), ssm_s.reshape(ssm_shape(bs)), jnp.stack(st_s_odd))
```
